```python
import jax, jax.numpy as jnp
from jax import lax
import numpy as np

D_MODEL = 1024
BATCH = 8
SEQ = 16384
DEPTH = 1

MIX_W = D_MODEL
HEAD_DIM = 64
ATTN_W = MIX_W // 2
N_ATTN_HEADS = ATTN_W // HEAD_DIM
N_KV_HEADS = 2
KV_W = N_KV_HEADS * HEAD_DIM
WINDOW = 128
BLOCK = 128
RET_W = MIX_W - ATTN_W
N_RET_HEADS = 4
RET_HEAD_DIM = RET_W // N_RET_HEADS
RET_CHUNK = 128
IN_W = ATTN_W + 2 * KV_W + 4 * RET_W
D_FF = 2816
CONV_WIDTH = 3
RMS_EPS = 1e-6
GN_EPS = 1e-6
MASK_VALUE = -1e30

kernel_name = "hymba_swa_sink_retention_convffn_sandwich"


def rms_norm(x, w):
    xf = x.astype(jnp.float32)
    y = xf * lax.rsqrt(jnp.mean(xf * xf, axis=-1, keepdims=True) + RMS_EPS)
    return (y * w.astype(jnp.float32)).astype(x.dtype)


def sliding_window_sink_attention(q, k, v, sinks):
    b, s, _ = q.shape
    nb = s // BLOCK
    g = N_ATTN_HEADS // N_KV_HEADS
    qb = q.reshape(b, nb, BLOCK, N_KV_HEADS, g, HEAD_DIM)
    kb = k.reshape(b, nb, BLOCK, N_KV_HEADS, HEAD_DIM)
    vb = v.reshape(b, nb, BLOCK, N_KV_HEADS, HEAD_DIM)
    pad = ((0, 0), (1, 0), (0, 0), (0, 0), (0, 0))
    kk = jnp.concatenate([jnp.pad(kb, pad)[:, :-1], kb], axis=2)
    vv = jnp.concatenate([jnp.pad(vb, pad)[:, :-1], vb], axis=2)
    scores = jnp.einsum('bnqhgd,bnkhd->bnhgqk', qb, kk).astype(jnp.float32) * (HEAD_DIM ** -0.5)
    qpos = jnp.arange(BLOCK)[:, None] + BLOCK
    kpos = jnp.arange(2 * BLOCK)[None, :]
    rel = qpos - kpos
    band = (rel >= 0) & (rel < WINDOW)
    not_pad = (jnp.arange(nb)[:, None, None] > 0) | (kpos >= BLOCK)[None]
    valid = band[None] & not_pad
    scores = jnp.where(valid[None, :, None, None], scores, MASK_VALUE)
    sink = jnp.broadcast_to(
        sinks.astype(jnp.float32).reshape(N_KV_HEADS, g)[None, None, :, :, None, None],
        scores.shape[:-1] + (1,))
    probs = jax.nn.softmax(jnp.concatenate([scores, sink], axis=-1), axis=-1)[..., :-1]
    out = jnp.einsum('bnhgqk,bnkhd->bnqhgd', probs.astype(vv.dtype), vv)
    return out.reshape(b, s, ATTN_W)


def rotate_every_two(x):
    x1 = x[..., ::2]
    x2 = x[..., 1::2]
    return jnp.stack([-x2, x1], axis=-1).reshape(x.shape)


def retention_chunkwise(q, k, v):
    b, s, h, dk = q.shape
    dv = v.shape[-1]
    nc = s // RET_CHUNK
    c = RET_CHUNK
    log_gamma = jnp.log(1.0 - jnp.power(2.0, -5.0 - jnp.arange(h, dtype=jnp.float32)))
    idx = jnp.arange(c, dtype=jnp.float32)
    rel = idx[:, None] - idx[None, :]
    d_intra = jnp.where(rel[None] >= 0,
                        jnp.exp(log_gamma[:, None, None] * jnp.maximum(rel, 0.0)[None]), 0.0)
    xi = jnp.exp(log_gamma[None, :] * (idx[:, None] + 1.0))
    zeta = jnp.exp(log_gamma[None, :] * (c - 1.0 - idx[:, None]))
    chunk_decay = jnp.exp(log_gamma * c)
    qc = q.reshape(b, nc, c, h, dk)
    kc = k.reshape(b, nc, c, h, dk)
    vc = v.reshape(b, nc, c, h, dv)
    inner = jnp.einsum('bnqhd,bnkhd->bnhqk', qc, kc) * d_intra[None, None]
    o_inner = jnp.einsum('bnhqk,bnkhe->bnqhe', inner, vc)
    kv_chunk = jnp.einsum('bnkhd,kh,bnkhe->bnhde', kc, zeta, vc)

    def step(state, kv):
        return chunk_decay[None, :, None, None] * state + kv, state

    _, prev = lax.scan(step, jnp.zeros((b, h, dk, dv), jnp.float32), jnp.moveaxis(kv_chunk, 1, 0))
    prev = jnp.moveaxis(prev, 0, 1)
    o_cross = jnp.einsum('bnqhd,bnhde->bnqhe', qc, prev) * xi[None, None, :, :, None]
    return (o_inner + o_cross).reshape(b, s, h, dv)


def retention_group(q, k, v, gate):
    b, s, _ = q.shape
    dtype = q.dtype
    pos = jnp.arange(s, dtype=jnp.float32)
    angle = 1.0 / jnp.power(10000.0, jnp.linspace(0.0, 1.0, RET_HEAD_DIM // 2, dtype=jnp.float32))
    angle = jnp.repeat(angle, 2)
    sin = jnp.sin(pos[:, None] * angle[None])[None, :, None, :]
    cos = jnp.cos(pos[:, None] * angle[None])[None, :, None, :]
    qf = q.astype(jnp.float32).reshape(b, s, N_RET_HEADS, RET_HEAD_DIM)
    kf = k.astype(jnp.float32).reshape(b, s, N_RET_HEADS, RET_HEAD_DIM) * (RET_HEAD_DIM ** -0.5)
    vf = v.astype(jnp.float32).reshape(b, s, N_RET_HEADS, RET_HEAD_DIM)
    qf = qf * cos + rotate_every_two(qf) * sin
    kf = kf * cos + rotate_every_two(kf) * sin
    o = retention_chunkwise(qf, kf, vf)
    mu = jnp.mean(o, axis=-1, keepdims=True)
    var = jnp.mean(jnp.square(o - mu), axis=-1, keepdims=True)
    o = ((o - mu) * lax.rsqrt(var + GN_EPS)).reshape(b, s, RET_W)
    return (jax.nn.silu(gate.astype(jnp.float32)) * o).astype(dtype)


def causal_depthwise_conv(u, w, bias):
    ch = u.shape[-1]
    y = lax.conv_general_dilated(u, w[:, None, :].astype(u.dtype), window_strides=(1,),
                                 padding=[(CONV_WIDTH - 1, 0)],
                                 dimension_numbers=('NWC', 'WIO', 'NWC'),
                                 feature_group_count=ch)
    return y + bias.astype(u.dtype)


def _fwd_setup_inputs(seed: int = 0) -> dict:
    key = jax.random.key(seed)
    ks = jax.random.split(key, 13)
    f32 = jnp.float32

    def gain(k):
        return 1.0 + 0.02 * jax.random.normal(k, (DEPTH, D_MODEL), f32)

    return {
        "x": jax.random.normal(ks[0], (BATCH, SEQ, D_MODEL), f32),
        "mix_pre_norm": gain(ks[1]),
        "w_in": jax.random.normal(ks[2], (DEPTH, D_MODEL, IN_W), f32) * D_MODEL ** -0.5,
        "attn_sinks": jax.random.normal(ks[3], (DEPTH, N_ATTN_HEADS), f32),
        "w_out": jax.random.normal(ks[4], (DEPTH, MIX_W, D_MODEL), f32) * MIX_W ** -0.5,
        "mix_post_norm": gain(ks[5]),
        "ffn_pre_norm": gain(ks[6]),
        "w_up": jax.random.normal(ks[7], (DEPTH, D_MODEL, 2 * D_FF), f32) * D_MODEL ** -0.5,
        "conv_w": jax.random.normal(ks[8], (DEPTH, CONV_WIDTH, 2 * D_FF), f32) * CONV_WIDTH ** -0.5,
        "conv_b": 0.01 * jax.random.normal(ks[9], (DEPTH, 2 * D_FF), f32),
        "w_down": jax.random.normal(ks[10], (DEPTH, D_FF, D_MODEL), f32) * D_FF ** -0.5,
        "ffn_post_norm": gain(ks[11]),
    }


def _fwd_reference(x, mix_pre_norm, w_in, attn_sinks, w_out, mix_post_norm,
              ffn_pre_norm, w_up, conv_w, conv_b, w_down, ffn_post_norm):
    splits = np.cumsum([ATTN_W, KV_W, KV_W, RET_W, RET_W, RET_W]).tolist()
    for l in range(DEPTH):
        h = rms_norm(x, mix_pre_norm[l])
        proj = jnp.einsum('bsd,de->bse', h, w_in[l])
        q_a, k_a, v_a, q_r, k_r, v_r, g_r = jnp.split(proj, splits, axis=-1)
        attn_out = sliding_window_sink_attention(q_a, k_a, v_a, attn_sinks[l])
        ret_out = retention_group(q_r, k_r, v_r, g_r)
        mixed = jnp.einsum('bse,ed->bsd', jnp.concatenate([attn_out, ret_out], axis=-1), w_out[l])
        x = x + rms_norm(mixed, mix_post_norm[l])
        h = rms_norm(x, ffn_pre_norm[l])
        u = causal_depthwise_conv(jnp.einsum('bsd,df->bsf', h, w_up[l]), conv_w[l], conv_b[l])
        u_gate, u_val = jnp.split(u, 2, axis=-1)
        y = jax.nn.gelu(u_gate, approximate=True) * u_val
        y = jnp.einsum('bsf,fd->bsd', y, w_down[l])
        x = x + rms_norm(y, ffn_post_norm[l])
    return x


import jax as _jax
import jax.numpy as _jnp

TWIN_FORMAT = 'train_step'
FWD_PARAMS = ['x', 'mix_pre_norm', 'w_in', 'attn_sinks', 'w_out', 'mix_post_norm', 'ffn_pre_norm', 'w_up', 'conv_w', 'conv_b', 'w_down', 'ffn_post_norm']
TWIN_WEIGHTS = ['mix_pre_norm', 'w_in', 'attn_sinks', 'w_out', 'mix_post_norm', 'ffn_pre_norm', 'w_up', 'conv_w', 'conv_b', 'w_down', 'ffn_post_norm']
TWIN_DIFF_INPUT = 'x'
TWIN_INPUTS = ['x', 'mix_pre_norm', 'w_in', 'attn_sinks', 'w_out', 'mix_post_norm', 'ffn_pre_norm', 'w_up', 'conv_w', 'conv_b', 'w_down', 'ffn_post_norm', 'loss_target', 'm_mix_pre_norm', 'm_w_in', 'm_attn_sinks', 'm_w_out', 'm_mix_post_norm', 'm_ffn_pre_norm', 'm_w_up', 'm_conv_w', 'm_conv_b', 'm_w_down', 'm_ffn_post_norm', 'v_mix_pre_norm', 'v_w_in', 'v_attn_sinks', 'v_w_out', 'v_mix_post_norm', 'v_ffn_pre_norm', 'v_w_up', 'v_conv_w', 'v_conv_b', 'v_w_down', 'v_ffn_post_norm']
TWIN_OUTPUTS = ['loss', 'grad_x', 'grad_mix_pre_norm', 'grad_w_in', 'grad_attn_sinks', 'grad_w_out', 'grad_mix_post_norm', 'grad_ffn_pre_norm', 'grad_w_up', 'grad_conv_w', 'grad_conv_b', 'grad_w_down', 'grad_ffn_post_norm', 'delta_mix_pre_norm', 'delta_w_in', 'delta_attn_sinks', 'delta_w_out', 'delta_mix_post_norm', 'delta_ffn_pre_norm', 'delta_w_up', 'delta_conv_w', 'delta_conv_b', 'delta_w_down', 'delta_ffn_post_norm', 'new_m_mix_pre_norm', 'new_m_w_in', 'new_m_attn_sinks', 'new_m_w_out', 'new_m_mix_post_norm', 'new_m_ffn_pre_norm', 'new_m_w_up', 'new_m_conv_w', 'new_m_conv_b', 'new_m_w_down', 'new_m_ffn_post_norm', 'new_v_mix_pre_norm', 'new_v_w_in', 'new_v_attn_sinks', 'new_v_w_out', 'new_v_mix_post_norm', 'new_v_ffn_pre_norm', 'new_v_w_up', 'new_v_conv_w', 'new_v_conv_b', 'new_v_w_down', 'new_v_ffn_post_norm']
TWIN_LEAF_KINDS = {'loss': 'loss', 'grad_x': 'grad_x', 'grad_mix_pre_norm': 'grad_w', 'grad_w_in': 'grad_w', 'grad_attn_sinks': 'grad_w', 'grad_w_out': 'grad_w', 'grad_mix_post_norm': 'grad_w', 'grad_ffn_pre_norm': 'grad_w', 'grad_w_up': 'grad_w', 'grad_conv_w': 'grad_w', 'grad_conv_b': 'grad_w', 'grad_w_down': 'grad_w', 'grad_ffn_post_norm': 'grad_w', 'delta_mix_pre_norm': 'delta_w', 'delta_w_in': 'delta_w', 'delta_attn_sinks': 'delta_w', 'delta_w_out': 'delta_w', 'delta_mix_post_norm': 'delta_w', 'delta_ffn_pre_norm': 'delta_w', 'delta_w_up': 'delta_w', 'delta_conv_w': 'delta_w', 'delta_conv_b': 'delta_w', 'delta_w_down': 'delta_w', 'delta_ffn_post_norm': 'delta_w', 'new_m_mix_pre_norm': 'new_m', 'new_m_w_in': 'new_m', 'new_m_attn_sinks': 'new_m', 'new_m_w_out': 'new_m', 'new_m_mix_post_norm': 'new_m', 'new_m_ffn_pre_norm': 'new_m', 'new_m_w_up': 'new_m', 'new_m_conv_w': 'new_m', 'new_m_conv_b': 'new_m', 'new_m_w_down': 'new_m', 'new_m_ffn_post_norm': 'new_m', 'new_v_mix_pre_norm': 'new_v', 'new_v_w_in': 'new_v', 'new_v_attn_sinks': 'new_v', 'new_v_w_out': 'new_v', 'new_v_mix_post_norm': 'new_v', 'new_v_ffn_pre_norm': 'new_v', 'new_v_w_up': 'new_v', 'new_v_conv_w': 'new_v', 'new_v_conv_b': 'new_v', 'new_v_w_down': 'new_v', 'new_v_ffn_post_norm': 'new_v'}


def _forward(args):
    return _fwd_reference(*[args[k] for k in FWD_PARAMS])


def _output_shape():
    def fwd():
        inp = _fwd_setup_inputs(0)
        return _fwd_reference(*[inp[k] for k in FWD_PARAMS])
    out = _jax.eval_shape(fwd)
    return out.shape, out.dtype

N_MICROBATCH = 1
ADAM_LR = 0.001
ADAM_B1 = 0.9
ADAM_B2 = 0.999
ADAM_EPS = 1e-08
ADAM_WD = 0.01
ADAM_STEP = 10
PER_EXAMPLE_BATCH_AXIS = {'x': 0, 'loss_target': 0}
SHARED_INPUTS = []
_WEIGHT_DTYPES = {'mix_pre_norm': _jnp.float32, 'w_in': _jnp.float32, 'attn_sinks': _jnp.float32, 'w_out': _jnp.float32, 'mix_post_norm': _jnp.float32, 'ffn_pre_norm': _jnp.float32, 'w_up': _jnp.float32, 'conv_w': _jnp.float32, 'conv_b': _jnp.float32, 'w_down': _jnp.float32, 'ffn_post_norm': _jnp.float32}
MOMENT_SCALE = {'mix_pre_norm': 1.851013e+00, 'w_in': 1.107320e+00, 'attn_sinks': 2.116971e-01, 'w_out': 1.079257e+00, 'mix_post_norm': 1.283525e+02, 'ffn_pre_norm': 1.062163e+00, 'w_up': 5.035376e-01, 'conv_w': 5.629132e-01, 'conv_b': 1.080942e+00, 'w_down': 1.044017e+00, 'ffn_post_norm': 1.283689e+02}


def _to_microbatches(a, axis):
    t = _jnp.moveaxis(a, axis, 0)
    t = t.reshape((N_MICROBATCH, t.shape[0] // N_MICROBATCH) + t.shape[1:])
    return _jnp.moveaxis(t, 1, axis + 1)


def setup_inputs(seed: int = 0) -> dict:
    inp = _fwd_setup_inputs(seed)
    key = _jax.random.fold_in(_jax.random.key(seed), 7919)
    shape, _ = _output_shape()
    out = dict(inp)
    out["loss_target"] = _jax.random.normal(_jax.random.fold_in(key, 0), shape, _jnp.float32)
    for i, name in enumerate(TWIN_WEIGHTS):
        w = inp[name].astype(_jnp.float32)
        if MOMENT_SCALE is None:
            s = _jnp.sqrt(_jnp.mean(_jnp.square(w)) + 1e-30)
        else:
            s = MOMENT_SCALE[name]
        km, kv = _jax.random.split(_jax.random.fold_in(key, i + 1))
        out[name] = w
        out["m_" + name] = s * _jax.random.normal(km, w.shape, _jnp.float32)
        out["v_" + name] = (s * s) * _jax.random.uniform(kv, w.shape, _jnp.float32, 0.5, 1.5)
    if N_MICROBATCH > 1:
        for name, axis in PER_EXAMPLE_BATCH_AXIS.items():
            out[name] = _to_microbatches(out[name], axis)
    return {'x': out['x'], 'mix_pre_norm': out['mix_pre_norm'], 'w_in': out['w_in'], 'attn_sinks': out['attn_sinks'], 'w_out': out['w_out'], 'mix_post_norm': out['mix_post_norm'], 'ffn_pre_norm': out['ffn_pre_norm'], 'w_up': out['w_up'], 'conv_w': out['conv_w'], 'conv_b': out['conv_b'], 'w_down': out['w_down'], 'ffn_post_norm': out['ffn_post_norm'], 'loss_target': out['loss_target'], 'm_mix_pre_norm': out['m_mix_pre_norm'], 'm_w_in': out['m_w_in'], 'm_attn_sinks': out['m_attn_sinks'], 'm_w_out': out['m_w_out'], 'm_mix_post_norm': out['m_mix_post_norm'], 'm_ffn_pre_norm': out['m_ffn_pre_norm'], 'm_w_up': out['m_w_up'], 'm_conv_w': out['m_conv_w'], 'm_conv_b': out['m_conv_b'], 'm_w_down': out['m_w_down'], 'm_ffn_post_norm': out['m_ffn_post_norm'], 'v_mix_pre_norm': out['v_mix_pre_norm'], 'v_w_in': out['v_w_in'], 'v_attn_sinks': out['v_attn_sinks'], 'v_w_out': out['v_w_out'], 'v_mix_post_norm': out['v_mix_post_norm'], 'v_ffn_pre_norm': out['v_ffn_pre_norm'], 'v_w_up': out['v_w_up'], 'v_conv_w': out['v_conv_w'], 'v_conv_b': out['v_conv_b'], 'v_w_down': out['v_w_down'], 'v_ffn_post_norm': out['v_ffn_post_norm']}


def _loss(weights, diff, rest, loss_target):
    with _jax.named_scope("forward"):
        args = {**rest, TWIN_DIFF_INPUT: diff, **{k: w.astype(_WEIGHT_DTYPES[k]) for k, w in weights.items()}}
        y = _forward(args)
    with _jax.named_scope("loss_head"):
        err = _jnp.square(y.astype(_jnp.float32) - loss_target)
        return 0.5 * _jnp.sum(_jnp.mean(err, axis=-1)) if err.ndim else 0.5 * err


def _adamw(w, g, m, v):
    m = ADAM_B1 * m + (1.0 - ADAM_B1) * g
    v = ADAM_B2 * v + (1.0 - ADAM_B2) * _jnp.square(g)
    m_hat = m / (1.0 - ADAM_B1 ** ADAM_STEP)
    v_hat = v / (1.0 - ADAM_B2 ** ADAM_STEP)
    delta = -ADAM_LR * (m_hat / (_jnp.sqrt(v_hat) + ADAM_EPS) + ADAM_WD * w)
    return delta, m, v


def reference(x, mix_pre_norm, w_in, attn_sinks, w_out, mix_post_norm, ffn_pre_norm, w_up, conv_w, conv_b, w_down, ffn_post_norm, loss_target, m_mix_pre_norm, m_w_in, m_attn_sinks, m_w_out, m_mix_post_norm, m_ffn_pre_norm, m_w_up, m_conv_w, m_conv_b, m_w_down, m_ffn_post_norm, v_mix_pre_norm, v_w_in, v_attn_sinks, v_w_out, v_mix_post_norm, v_ffn_pre_norm, v_w_up, v_conv_w, v_conv_b, v_w_down, v_ffn_post_norm):
    given = dict(x=x, mix_pre_norm=mix_pre_norm, w_in=w_in, attn_sinks=attn_sinks, w_out=w_out, mix_post_norm=mix_post_norm, ffn_pre_norm=ffn_pre_norm, w_up=w_up, conv_w=conv_w, conv_b=conv_b, w_down=w_down, ffn_post_norm=ffn_post_norm, loss_target=loss_target, m_mix_pre_norm=m_mix_pre_norm, m_w_in=m_w_in, m_attn_sinks=m_attn_sinks, m_w_out=m_w_out, m_mix_post_norm=m_mix_post_norm, m_ffn_pre_norm=m_ffn_pre_norm, m_w_up=m_w_up, m_conv_w=m_conv_w, m_conv_b=m_conv_b, m_w_down=m_w_down, m_ffn_post_norm=m_ffn_post_norm, v_mix_pre_norm=v_mix_pre_norm, v_w_in=v_w_in, v_attn_sinks=v_attn_sinks, v_w_out=v_w_out, v_mix_post_norm=v_mix_post_norm, v_ffn_pre_norm=v_ffn_pre_norm, v_w_up=v_w_up, v_conv_w=v_conv_w, v_conv_b=v_conv_b, v_w_down=v_w_down, v_ffn_post_norm=v_ffn_post_norm)
    weights = {n: given[n] for n in TWIN_WEIGHTS}
    shared = {n: given[n] for n in SHARED_INPUTS}
    per_example = {n: given[n] for n in ['x']}
    grad_fn = _jax.value_and_grad(_loss, argnums=(0, 1))

    def one_microbatch(ex, loss_target):
        ex = dict(ex)
        diff = ex.pop(TWIN_DIFF_INPUT)
        return grad_fn(weights, diff, {**shared, **ex}, loss_target)

    if N_MICROBATCH == 1:
        loss, (grad_w, grad_x) = one_microbatch(per_example, given["loss_target"])
    else:
        def body(carry, xs):
            loss_sum, grad_sum = carry
            l_k, (gw_k, gx_k) = one_microbatch(xs[0], xs[1])
            with _jax.named_scope("update"):
                return (loss_sum + l_k, _jax.tree.map(_jnp.add, grad_sum, gw_k)), gx_k

        init = (_jnp.zeros((), _jnp.float32), _jax.tree.map(_jnp.zeros_like, weights))
        (loss, grad_w), grad_x = _jax.lax.scan(body, init, (per_example, given["loss_target"]))
    with _jax.named_scope("update"):
        delta_w, new_m, new_v = {}, {}, {}
        for n in TWIN_WEIGHTS:
            delta_w[n], new_m[n], new_v[n] = _adamw(weights[n], grad_w[n], given["m_" + n], given["v_" + n])
    return (loss, grad_x, *[grad_w[n] for n in TWIN_WEIGHTS], *[delta_w[n] for n in TWIN_WEIGHTS],
            *[new_m[n] for n in TWIN_WEIGHTS], *[new_v[n] for n in TWIN_WEIGHTS])
```

```python
import functools
import math

import jax
import jax.numpy as jnp
from jax import lax
from jax.experimental import pallas as pl
from jax.experimental.pallas import tpu as pltpu

F32 = jnp.float32
BF = jnp.bfloat16

N_DEV = 8
D_MODEL = 1024
HEAD_DIM = 64
ATTN_W = 512
N_ATTN_HEADS = 8
KV_W = 128
BLOCK = 128
RET_W = 512
N_RET_HEADS = 4
RET_HEAD_DIM = 128
IN_W = 2816
D_FF = 2816
RMS_EPS = 1e-6
GN_EPS = 1e-6
MASK_VALUE = -1e30
ATTN_SCALE = HEAD_DIM ** -0.5
RET_K_SCALE = RET_HEAD_DIM ** -0.5
GELU_C = math.sqrt(2.0 / math.pi)
GELU_A = 0.044715

ADAM_LR = 0.001
ADAM_B1 = 0.9
ADAM_B2 = 0.999
ADAM_EPS = 1e-08
ADAM_WD = 0.01
ADAM_STEP = 10

VMEM_LIMIT_BYTES = 56 * 1024 * 1024
FF_CHUNK = 256
N_FF_CHUNKS = D_FF // FF_CHUNK

QA0, KA0, VA0, QR0, KR0, VR0, GR0 = 0, 512, 640, 768, 1280, 1792, 2304

MESH_ID = pl.DeviceIdType.MESH


def _pcall(body, **kw):
    return pl.pallas_call(body, **kw)


def _params(sem=None):
    return pltpu.CompilerParams(dimension_semantics=sem, vmem_limit_bytes=VMEM_LIMIT_BYTES)


def _dot(a, b):
    return jnp.dot(a, b, preferred_element_type=F32)


def _dot_nt(a, b):
    return lax.dot_general(a, b, (((1,), (1,)), ((), ())), preferred_element_type=F32)


def _dot_tn(a, b):
    return lax.dot_general(a, b, (((0,), (0,)), ((), ())), preferred_element_type=F32)


def _vmem_full():
    return pl.BlockSpec(memory_space=pltpu.VMEM)


def _smem_full():
    return pl.BlockSpec(memory_space=pltpu.SMEM)


def _rows(tm, w):
    return pl.BlockSpec((tm, w), lambda i: (i, 0))


def _const(shape):
    return pl.BlockSpec(shape, lambda i: tuple(0 for _ in shape))


def _rms_stats(x):
    r = lax.rsqrt(jnp.mean(x * x, axis=-1, keepdims=True) + RMS_EPS)
    return r, x * r


def _rms_bwd(n, r, dn):
    return r * (dn - n * jnp.mean(dn * n, axis=-1, keepdims=True))


def _rot(x, even):
    w = x.shape[1]
    return jnp.where(even, pltpu.roll(x, w - 1, 1), pltpu.roll(x, 1, 1))


def _peers():
    x, y, c = lax.axis_index("x"), lax.axis_index("y"), lax.axis_index("c")
    flips = [(0, 0, 1), (1, 0, 0), (0, 1, 0), (1, 1, 0), (1, 0, 1), (0, 1, 1), (1, 1, 1)]
    peers = [(x ^ fx, y ^ fy, c ^ fc) for fx, fy, fc in flips]
    return 4 * x + 2 * y + c, peers


def _all_gather(arrs, name):
    n = len(arrs)

    def body(*refs):
        ins, outs = refs[:n], refs[n:2 * n]
        send_sems, recv_sems, local_sems = refs[2 * n:]
        me, peers = _peers()
        local = [pltpu.make_async_copy(ins[a], outs[a].at[me], local_sems.at[a]) for a in range(n)]
        for cp in local:
            cp.start()

        def remote(a, k, slot):
            return pltpu.make_async_remote_copy(
                src_ref=ins[a], dst_ref=outs[a].at[slot], send_sem=send_sems.at[a, k],
                recv_sem=recv_sems.at[a, k], device_id=peers[k], device_id_type=MESH_ID)

        for a in range(n):
            for k in range(N_DEV - 1):
                remote(a, k, me).start()
        for a in range(n):
            for k in range(N_DEV - 1):
                px, py, pc = peers[k]
                remote(a, k, 4 * px + 2 * py + pc).wait_recv()
        for a in range(n):
            for k in range(N_DEV - 1):
                remote(a, k, me).wait_send()
        for cp in local:
            cp.wait()

    any_spec = pl.BlockSpec(memory_space=pl.ANY)
    return _pcall(
        body, name=name,
        out_shape=[jax.ShapeDtypeStruct((N_DEV,) + a.shape, a.dtype) for a in arrs],
        in_specs=[any_spec] * n, out_specs=[any_spec] * n,
        scratch_shapes=[pltpu.SemaphoreType.DMA((n, N_DEV - 1)), pltpu.SemaphoreType.DMA((n, N_DEV - 1)),
                        pltpu.SemaphoreType.DMA((n,))],
    )(*arrs)


def _all_to_all(arrs, name):
    n = len(arrs)

    def body(*refs):
        ins, outs = refs[:n], refs[n:2 * n]
        send_sems, recv_sems, local_sems = refs[2 * n:]
        me, peers = _peers()
        local = [pltpu.make_async_copy(ins[a].at[me], outs[a].at[me], local_sems.at[a]) for a in range(n)]
        for cp in local:
            cp.start()

        def remote(a, k, src_slot, dst_slot):
            return pltpu.make_async_remote_copy(
                src_ref=ins[a].at[src_slot], dst_ref=outs[a].at[dst_slot], send_sem=send_sems.at[a, k],
                recv_sem=recv_sems.at[a, k], device_id=peers[k], device_id_type=MESH_ID)

        def slot(k):
            px, py, pc = peers[k]
            return 4 * px + 2 * py + pc

        for a in range(n):
            for k in range(N_DEV - 1):
                remote(a, k, slot(k), me).start()
        for a in range(n):
            for k in range(N_DEV - 1):
                remote(a, k, me, slot(k)).wait_recv()
        for a in range(n):
            for k in range(N_DEV - 1):
                remote(a, k, slot(k), me).wait_send()
        for cp in local:
            cp.wait()

    any_spec = pl.BlockSpec(memory_space=pl.ANY)
    return _pcall(
        body, name=name,
        out_shape=[jax.ShapeDtypeStruct(a.shape, a.dtype) for a in arrs],
        in_specs=[any_spec] * n, out_specs=[any_spec] * n,
        scratch_shapes=[pltpu.SemaphoreType.DMA((n, N_DEV - 1)), pltpu.SemaphoreType.DMA((n, N_DEV - 1)),
                        pltpu.SemaphoreType.DMA((n,))],
    )(*arrs)


def _in_proj(x, g1, w_in, cos, sin_s, tm):
    T = x.shape[0]

    def body(x_ref, g_ref, w_ref, cos_ref, sin_ref, h_ref, qa_ref, ka_ref, va_ref, qr_ref, kr_ref, vr_ref, gr_ref):
        r, n = _rms_stats(x_ref[...])
        h = (n * g_ref[...]).astype(BF)
        h_ref[...] = h

        def proj(c0, w):
            return _dot(h, w_ref[:, c0:c0 + w])

        qa_ref[...] = proj(QA0, ATTN_W).astype(BF)
        ka_ref[...] = proj(KA0, KV_W).astype(BF)
        va_ref[...] = proj(VA0, KV_W).astype(BF)
        vr_ref[...] = proj(VR0, RET_W).astype(BF)
        gr_ref[...] = proj(GR0, RET_W)
        cos_t, sin_t = cos_ref[...], sin_ref[...]
        even = lax.broadcasted_iota(jnp.int32, (tm, RET_HEAD_DIM), 1) % 2 == 0
        for hd in range(N_RET_HEADS):
            c = hd * RET_HEAD_DIM
            q = proj(QR0 + c, RET_HEAD_DIM)
            k = proj(KR0 + c, RET_HEAD_DIM) * RET_K_SCALE
            qr_ref[:, c:c + RET_HEAD_DIM] = (q * cos_t + _rot(q, even) * sin_t).astype(BF)
            kr_ref[:, c:c + RET_HEAD_DIM] = (k * cos_t + _rot(k, even) * sin_t).astype(BF)

    widths = [D_MODEL, ATTN_W, KV_W, KV_W, RET_W, RET_W, RET_W, RET_W]
    dts = [BF] * 7 + [F32]
    return _pcall(
        body, name="in_proj", grid=(T // tm,),
        in_specs=[_rows(tm, D_MODEL), _const((1, D_MODEL)), _vmem_full(), _rows(tm, RET_HEAD_DIM),
                  _rows(tm, RET_HEAD_DIM)],
        out_specs=[_rows(tm, w) for w in widths],
        out_shape=[jax.ShapeDtypeStruct((T, w), dt) for w, dt in zip(widths, dts)],
        compiler_params=_params(("parallel",)),
    )(x, g1, w_in, cos, sin_s)


def _kv_variants(kk):
    kf = kk.astype(F32)
    lo = lax.broadcasted_iota(jnp.int32, kf.shape, 1) < HEAD_DIM
    h0_lo = jnp.where(lo, kf, 0.0)
    h1_hi = jnp.where(lo, 0.0, kf)
    h0_hi = pltpu.roll(h0_lo, HEAD_DIM, 1)
    h1_lo = pltpu.roll(h1_hi, HEAD_DIM, 1)
    return [[h0_lo.astype(BF), h0_hi.astype(BF)], [h1_lo.astype(BF), h1_hi.astype(BF)]]


def _col_to_tile(tile, col, head):
    lane = lax.broadcasted_iota(jnp.int32, tile.shape, 1)
    return jnp.where(lane == head, col, tile)


def _attn_fwd(sinks, qa, ka, va):
    T = qa.shape[0]
    nb = T // BLOCK

    def body(sink_ref, q_ref, kc_ref, kp_ref, vc_ref, vp_ref, a_ref, lse_ref, lset_ref):
        n = pl.program_id(0)
        kv = _kv_variants(jnp.concatenate([kp_ref[...], kc_ref[...]], axis=0))
        vv = _kv_variants(jnp.concatenate([vp_ref[...], vc_ref[...]], axis=0))
        i = lax.broadcasted_iota(jnp.int32, (BLOCK, 2 * BLOCK), 0)
        j = lax.broadcasted_iota(jnp.int32, (BLOCK, 2 * BLOCK), 1)
        valid = (j > i) & (j <= i + BLOCK) & ((n > 0) | (j >= BLOCK))
        lse_all = jnp.zeros((BLOCK, BLOCK), F32)
        for pair in range(N_ATTN_HEADS // 2):
            h = pair // 2
            qp = q_ref[:, pair * 128:(pair + 1) * 128]
            acc = jnp.zeros((BLOCK, 128), F32)
            for e in range(2):
                head = 2 * pair + e
                sink = sink_ref[head]
                s = jnp.where(valid, _dot_nt(qp, kv[h][e]) * ATTN_SCALE, MASK_VALUE)
                m = jnp.maximum(jnp.max(s, axis=-1, keepdims=True), sink)
                p = jnp.exp(s - m)
                z = jnp.sum(p, axis=-1, keepdims=True) + jnp.exp(sink - m)
                acc = acc + _dot((p * (1.0 / z)).astype(BF), vv[h][e])
                lse_all = _col_to_tile(lse_all, m + jnp.log(z), head)
            a_ref[:, pair * 128:(pair + 1) * 128] = acc.astype(BF)
        lse_ref[...] = lse_all
        lset_ref[...] = lse_all.T[:N_ATTN_HEADS, :]

    cur = lambda w: pl.BlockSpec((BLOCK, w), lambda n: (n, 0))
    prev = lambda w: pl.BlockSpec((BLOCK, w), lambda n: (jnp.maximum(n - 1, 0), 0))
    return _pcall(
        body, name="attn_fwd", grid=(nb,),
        in_specs=[_smem_full(), cur(ATTN_W), cur(KV_W), prev(KV_W), cur(KV_W), prev(KV_W)],
        out_specs=[cur(ATTN_W), cur(BLOCK), pl.BlockSpec((N_ATTN_HEADS, BLOCK), lambda n: (0, n))],
        out_shape=[jax.ShapeDtypeStruct((T, ATTN_W), BF), jax.ShapeDtypeStruct((T, BLOCK), F32),
                   jax.ShapeDtypeStruct((N_ATTN_HEADS, T), F32)],
        compiler_params=_params(("parallel",)),
    )(sinks, qa, ka, ka, va, va)


def _ret_fwd(decay, qr, kr, vr, gr, d_intra, xi_b, zeta_b):
    T = qr.shape[0]
    nc = T // BLOCK
    H, C = N_RET_HEADS, RET_HEAD_DIM

    def body(decay_ref, q_ref, k_ref, v_ref, g_ref, d_ref, xi_ref, zeta_ref, o_ref, s_ref, r_ref, state):
        @pl.when(pl.program_id(0) == 0)
        def _():
            state[...] = jnp.zeros_like(state)

        for h in range(H):
            cs = slice(h * C, (h + 1) * C)
            q, k, v = q_ref[:, cs], k_ref[:, cs], v_ref[:, cs]
            st = state[h]
            st_b = st.astype(BF)
            s_ref[0, h] = st_b
            inner = (_dot_nt(q, k) * d_ref[h]).astype(BF)
            o = _dot(inner, v) + _dot(q, st_b) * xi_ref[h]
            kz = (k.astype(F32) * zeta_ref[h]).astype(BF)
            state[h] = decay_ref[h] * st + _dot_tn(kz, v)
            o_ref[:, cs] = o
            mu = jnp.mean(o, axis=-1, keepdims=True)
            oc = o - mu
            rs = lax.rsqrt(jnp.mean(oc * oc, axis=-1, keepdims=True) + GN_EPS)
            g = g_ref[:, cs]
            r_ref[:, cs] = (g * jax.nn.sigmoid(g) * (oc * rs)).astype(BF)

    cur = pl.BlockSpec((BLOCK, RET_W), lambda n: (n, 0))
    tab = pl.BlockSpec((H, C, C), lambda n: (0, 0, 0))
    return _pcall(
        body, name="ret_fwd", grid=(nc,),
        in_specs=[_smem_full(), cur, cur, cur, cur, tab, tab, tab],
        out_specs=[cur, pl.BlockSpec((1, H, C, C), lambda n: (n, 0, 0, 0)), cur],
        out_shape=[jax.ShapeDtypeStruct((T, RET_W), F32), jax.ShapeDtypeStruct((nc, H, C, C), BF),
                   jax.ShapeDtypeStruct((T, RET_W), BF)],
        scratch_shapes=[pltpu.VMEM((H, C, C), F32)],
        compiler_params=_params(("arbitrary",)),
    )(decay, qr, kr, vr, gr, d_intra, xi_b, zeta_b)


def _out_proj(a, r, w_out, x, g2, g3, tm):
    T = x.shape[0]

    def body(a_ref, r_ref, w_ref, x_ref, g2_ref, g3_ref, mixed_ref, x1_ref, h2_ref):
        mixed = _dot(a_ref[...], w_ref[:ATTN_W, :]) + _dot(r_ref[...], w_ref[ATTN_W:, :])
        mixed_ref[...] = mixed
        _, n2 = _rms_stats(mixed)
        x1 = x_ref[...] + n2 * g2_ref[...]
        x1_ref[...] = x1
        _, n3 = _rms_stats(x1)
        h2_ref[...] = (n3 * g3_ref[...]).astype(BF)

    return _pcall(
        body, name="out_proj", grid=(T // tm,),
        in_specs=[_rows(tm, ATTN_W), _rows(tm, RET_W), _vmem_full(), _rows(tm, D_MODEL), _const((1, D_MODEL)),
                  _const((1, D_MODEL))],
        out_specs=[_rows(tm, D_MODEL)] * 3,
        out_shape=[jax.ShapeDtypeStruct((T, D_MODEL), F32), jax.ShapeDtypeStruct((T, D_MODEL), F32),
                   jax.ShapeDtypeStruct((T, D_MODEL), BF)],
        compiler_params=_params(("parallel",)),
    )(a, r, w_out, x, g2, g3)


def _shift_down(cur, k, edge_rows):
    out = pltpu.roll(cur, k, 0)
    row = lax.broadcasted_iota(jnp.int32, cur.shape, 0)
    for r in range(k):
        out = jnp.where(row == r, edge_rows[r], out)
    return out


def _shift_up(cur, k, edge_rows):
    tm = cur.shape[0]
    out = pltpu.roll(cur, tm - k, 0)
    row = lax.broadcasted_iota(jnp.int32, cur.shape, 0)
    for r in range(k):
        out = jnp.where(row == tm - k + r, edge_rows[r], out)
    return out


def _conv_taps(up_ref, halo_ref, first, cols):
    cur = up_ref[:, cols]
    e6 = jnp.where(first, 0.0, halo_ref[pl.ds(6, 1), cols])
    e7 = jnp.where(first, 0.0, halo_ref[pl.ds(7, 1), cols])
    return _shift_down(cur, 2, [e6, e7]), _shift_down(cur, 1, [e7]), cur


def _gelu_parts(x):
    x2 = x * x
    th = jnp.tanh(GELU_C * (x + GELU_A * x * x2))
    gelu = 0.5 * x * (1.0 + th)
    dgelu = 0.5 * (1.0 + th) + 0.5 * x * (1.0 - th * th) * (GELU_C * (1.0 + 3.0 * GELU_A * x2))
    return gelu, dgelu


def _ffn_up(h2, w_up, conv_w, conv_b, tm):
    T = h2.shape[0]

    def body(h_ref, w_ref, cw_ref, cb_ref, up_ref, y_ref, halo):
        first = pl.program_id(0) == 0
        h = h_ref[...]
        for c in range(2 * N_FF_CHUNKS):
            cols = slice(c * FF_CHUNK, (c + 1) * FF_CHUNK)
            up_ref[:, cols] = _dot(h, w_ref[:, cols])
        for c in range(N_FF_CHUNKS):
            u = []
            for part in range(2):
                cols = slice(part * D_FF + c * FF_CHUNK, part * D_FF + (c + 1) * FF_CHUNK)
                t2, t1, t0 = _conv_taps(up_ref, halo, first, cols)
                u.append(cw_ref[pl.ds(0, 1), cols] * t2 + cw_ref[pl.ds(1, 1), cols] * t1
                         + cw_ref[pl.ds(2, 1), cols] * t0 + cb_ref[:, cols])
            gelu, _ = _gelu_parts(u[0])
            y_ref[:, c * FF_CHUNK:(c + 1) * FF_CHUNK] = (gelu * u[1]).astype(BF)
        halo[...] = up_ref[tm - 8:tm, :]

    return _pcall(
        body, name="ffn_up", grid=(T // tm,),
        in_specs=[_rows(tm, D_MODEL), _vmem_full(), _const((3, 2 * D_FF)), _const((1, 2 * D_FF))],
        out_specs=[_rows(tm, 2 * D_FF), _rows(tm, D_FF)],
        out_shape=[jax.ShapeDtypeStruct((T, 2 * D_FF), F32), jax.ShapeDtypeStruct((T, D_FF), BF)],
        scratch_shapes=[pltpu.VMEM((8, 2 * D_FF), F32)],
        compiler_params=_params(("arbitrary",)),
    )(h2, w_up, conv_w, conv_b)


def _ffn_down(y, w_down, x1, g4, target, tm):
    T = y.shape[0]

    def body(y_ref, w_ref, x1_ref, g_ref, t_ref, z_ref, dout_ref, loss_ref):
        @pl.when(pl.program_id(0) == 0)
        def _():
            loss_ref[...] = jnp.zeros_like(loss_ref)

        z = _dot(y_ref[...], w_ref[...])
        z_ref[...] = z
        _, n4 = _rms_stats(z)
        err = x1_ref[...] + n4 * g_ref[...] - t_ref[...]
        dout_ref[...] = err * (1.0 / D_MODEL)
        loss_ref[...] += 0.5 * jnp.sum(jnp.mean(err * err, axis=-1, keepdims=True), axis=0, keepdims=True)

    return _pcall(
        body, name="ffn_down", grid=(T // tm,),
        in_specs=[_rows(tm, D_FF), _vmem_full(), _rows(tm, D_MODEL), _const((1, D_MODEL)), _rows(tm, D_MODEL)],
        out_specs=[_rows(tm, D_MODEL), _rows(tm, D_MODEL), _const((8, 128))],
        out_shape=[jax.ShapeDtypeStruct((T, D_MODEL), F32), jax.ShapeDtypeStruct((T, D_MODEL), F32),
                   jax.ShapeDtypeStruct((8, 128), F32)],
        compiler_params=_params(("arbitrary",)),
    )(y, w_down, x1, g4, target)


def _ffn_bwd_a(z, dout, g4, w_down, up, conv_w, conv_b, tm):
    T = z.shape[0]
    nt = T // tm
    hb = tm // 8

    def body(z_ref, dout_ref, g_ref, w_ref, up_ref, halo_ref, cw_ref, cb_ref,
             dz_ref, dup_ref, dg4_ref, dcb_ref, dcw_ref, carry):
        i = pl.program_id(0)

        @pl.when(i == 0)
        def _():
            dg4_ref[...] = jnp.zeros_like(dg4_ref)
            dcb_ref[...] = jnp.zeros_like(dcb_ref)
            dcw_ref[...] = jnp.zeros_like(dcw_ref)
            carry[...] = jnp.zeros_like(carry)

        first = i == nt - 1
        r4, n4 = _rms_stats(z_ref[...])
        dout = dout_ref[...]
        dg4_ref[...] += jnp.sum(dout * n4, axis=0, keepdims=True)
        dz = _rms_bwd(n4, r4, dout * g_ref[...]).astype(BF)
        dz_ref[...] = dz
        for c in range(N_FF_CHUNKS):
            dy = _dot_nt(dz, w_ref[c * FF_CHUNK:(c + 1) * FF_CHUNK, :])
            taps, u, cols2 = [], [], []
            for part in range(2):
                cols = slice(part * D_FF + c * FF_CHUNK, part * D_FF + (c + 1) * FF_CHUNK)
                cols2.append(cols)
                t = _conv_taps(up_ref, halo_ref, first, cols)
                taps.append(t)
                u.append(cw_ref[pl.ds(0, 1), cols] * t[0] + cw_ref[pl.ds(1, 1), cols] * t[1]
                         + cw_ref[pl.ds(2, 1), cols] * t[2] + cb_ref[:, cols])
            gelu, dgelu = _gelu_parts(u[0])
            du = [dy * u[1] * dgelu, dy * gelu]
            for part in range(2):
                cols = cols2[part]
                d = du[part]
                dcb_ref[:, cols] += jnp.sum(d, axis=0, keepdims=True)
                for k in range(3):
                    dcw_ref[pl.ds(k, 1), cols] += jnp.sum(d * taps[part][k], axis=0, keepdims=True)
                e0 = carry[pl.ds(0, 1), cols]
                e1 = carry[pl.ds(1, 1), cols]
                dup = (cw_ref[pl.ds(2, 1), cols] * d + cw_ref[pl.ds(1, 1), cols] * _shift_up(d, 1, [e0])
                       + cw_ref[pl.ds(0, 1), cols] * _shift_up(d, 2, [e0, e1]))
                dup_ref[:, cols] = dup.astype(BF)
                carry[:, cols] = d[0:8, :]

    rev = lambda w: pl.BlockSpec((tm, w), lambda i: (nt - 1 - i, 0))
    halo = pl.BlockSpec((8, 2 * D_FF), lambda i: (jnp.maximum((nt - 1 - i) * hb - 1, 0), 0))
    return _pcall(
        body, name="ffn_bwd_a", grid=(nt,),
        in_specs=[rev(D_MODEL), rev(D_MODEL), _const((1, D_MODEL)), _vmem_full(), rev(2 * D_FF), halo,
                  _const((3, 2 * D_FF)), _const((1, 2 * D_FF))],
        out_specs=[rev(D_MODEL), rev(2 * D_FF), _const((1, D_MODEL)), _const((1, 2 * D_FF)),
                   _const((3, 2 * D_FF))],
        out_shape=[jax.ShapeDtypeStruct((T, D_MODEL), BF), jax.ShapeDtypeStruct((T, 2 * D_FF), BF),
                   jax.ShapeDtypeStruct((1, D_MODEL), F32), jax.ShapeDtypeStruct((1, 2 * D_FF), F32),
                   jax.ShapeDtypeStruct((3, 2 * D_FF), F32)],
        scratch_shapes=[pltpu.VMEM((8, 2 * D_FF), F32)],
        compiler_params=_params(("arbitrary",)),
    )(z, dout, g4, w_down, up, up, conv_w, conv_b)


def _ffn_bwd_b(dup, w_up, x1, dout, g3, mixed, g2, w_out, tm):
    T = x1.shape[0]

    def body(dup_ref, wup_ref, x1_ref, dout_ref, g3_ref, mixed_ref, g2_ref, wout_ref,
             dx1_ref, dmixed_ref, da_ref, dr_ref, dg3_ref, dg2_ref):
        @pl.when(pl.program_id(0) == 0)
        def _():
            dg3_ref[...] = jnp.zeros_like(dg3_ref)
            dg2_ref[...] = jnp.zeros_like(dg2_ref)

        dh2 = _dot_nt(dup_ref[...], wup_ref[...])
        r3, n3 = _rms_stats(x1_ref[...])
        dg3_ref[...] += jnp.sum(dh2 * n3, axis=0, keepdims=True)
        dx1 = dout_ref[...] + _rms_bwd(n3, r3, dh2 * g3_ref[...])
        dx1_ref[...] = dx1
        r2, n2 = _rms_stats(mixed_ref[...])
        dg2_ref[...] += jnp.sum(dx1 * n2, axis=0, keepdims=True)
        dmixed = _rms_bwd(n2, r2, dx1 * g2_ref[...]).astype(BF)
        dmixed_ref[...] = dmixed
        da_ref[...] = _dot_nt(dmixed, wout_ref[:ATTN_W, :])
        dr_ref[...] = _dot_nt(dmixed, wout_ref[ATTN_W:, :])

    return _pcall(
        body, name="ffn_bwd_b", grid=(T // tm,),
        in_specs=[_rows(tm, 2 * D_FF), _vmem_full(), _rows(tm, D_MODEL), _rows(tm, D_MODEL), _const((1, D_MODEL)),
                  _rows(tm, D_MODEL), _const((1, D_MODEL)), _vmem_full()],
        out_specs=[_rows(tm, D_MODEL), _rows(tm, D_MODEL), _rows(tm, ATTN_W), _rows(tm, RET_W),
                   _const((1, D_MODEL)), _const((1, D_MODEL))],
        out_shape=[jax.ShapeDtypeStruct((T, D_MODEL), F32), jax.ShapeDtypeStruct((T, D_MODEL), BF),
                   jax.ShapeDtypeStruct((T, ATTN_W), F32), jax.ShapeDtypeStruct((T, RET_W), F32),
                   jax.ShapeDtypeStruct((1, D_MODEL), F32), jax.ShapeDtypeStruct((1, D_MODEL), F32)],
        compiler_params=_params(("arbitrary",)),
    )(dup, w_up, x1, dout, g3, mixed, g2, w_out)


def _ret_bwd(decay, qr, kr, vr, gr, o, states, dr, d_intra, d_intra_t, xi_b, zeta_b, cos, sin_s):
    T = qr.shape[0]
    nc = T // BLOCK
    H, C = N_RET_HEADS, RET_HEAD_DIM

    def body(decay_ref, q_ref, k_ref, v_ref, g_ref, o_ref, s_ref, dr_ref, d_ref, dt_ref, xi_ref, zeta_ref,
             cos_ref, sin_ref, dret_ref, gstate):
        @pl.when(pl.program_id(0) == 0)
        def _():
            gstate[...] = jnp.zeros_like(gstate)

        cos_t, sin_t = cos_ref[...], sin_ref[...]
        even = lax.broadcasted_iota(jnp.int32, (BLOCK, C), 1) % 2 == 0
        for h in range(H):
            cs = slice(h * C, (h + 1) * C)
            q, k, v = q_ref[:, cs], k_ref[:, cs], v_ref[:, cs]
            g, o_h, dr_h = g_ref[:, cs], o_ref[:, cs], dr_ref[:, cs]
            mu = jnp.mean(o_h, axis=-1, keepdims=True)
            oc = o_h - mu
            rs = lax.rsqrt(jnp.mean(oc * oc, axis=-1, keepdims=True) + GN_EPS)
            on = oc * rs
            sg = jax.nn.sigmoid(g)
            dg = dr_h * on * (sg * (1.0 + g * (1.0 - sg)))
            don = dr_h * (g * sg)
            do = rs * (don - jnp.mean(don, axis=-1, keepdims=True)
                       - on * jnp.mean(don * on, axis=-1, keepdims=True))
            do_b = do.astype(BF)
            dox_b = (do * xi_ref[h]).astype(BF)
            gst = gstate[h]
            gst_b = gst.astype(BF)
            kz = (k.astype(F32) * zeta_ref[h]).astype(BF)
            da_b = (_dot_nt(do_b, v) * d_ref[h]).astype(BF)
            dat_b = (_dot_nt(v, do_b) * dt_ref[h]).astype(BF)
            mt_b = (_dot_nt(k, q) * dt_ref[h]).astype(BF)
            dq = _dot(da_b, k) + _dot_nt(dox_b, s_ref[0, h])
            dk = _dot(dat_b, q) + _dot_nt(v, gst_b) * zeta_ref[h]
            dv = _dot(mt_b, do_b) + _dot(kz, gst_b)
            gstate[h] = decay_ref[h] * gst + _dot_tn(q, dox_b)
            dq = dq * cos_t - _rot(dq, even) * sin_t
            dk = (dk * cos_t - _rot(dk, even) * sin_t) * RET_K_SCALE
            dret_ref[:, h * C:(h + 1) * C] = dq.astype(BF)
            dret_ref[:, RET_W + h * C:RET_W + (h + 1) * C] = dk.astype(BF)
            dret_ref[:, 2 * RET_W + h * C:2 * RET_W + (h + 1) * C] = dv.astype(BF)
            dret_ref[:, 3 * RET_W + h * C:3 * RET_W + (h + 1) * C] = dg.astype(BF)

    rev = lambda w: pl.BlockSpec((BLOCK, w), lambda n: (nc - 1 - n, 0))
    tab = pl.BlockSpec((H, C, C), lambda n: (0, 0, 0))
    return _pcall(
        body, name="ret_bwd", grid=(nc,),
        in_specs=[_smem_full(), rev(RET_W), rev(RET_W), rev(RET_W), rev(RET_W), rev(RET_W),
                  pl.BlockSpec((1, H, C, C), lambda n: (nc - 1 - n, 0, 0, 0)), rev(RET_W), tab, tab, tab, tab,
                  rev(C), rev(C)],
        out_specs=rev(4 * RET_W),
        out_shape=jax.ShapeDtypeStruct((T, 4 * RET_W), BF),
        scratch_shapes=[pltpu.VMEM((H, C, C), F32)],
        compiler_params=_params(("arbitrary",)),
    )(decay, qr, kr, vr, gr, o, states, dr, d_intra, d_intra_t, xi_b, zeta_b, cos, sin_s)


def _attn_bwd_dq(sinks, qa, ka, va, da, lse):
    T = qa.shape[0]
    nb = T // BLOCK

    def body(sink_ref, q_ref, kc_ref, kp_ref, vc_ref, vp_ref, da_ref, lse_ref, dq_ref, deltat_ref, dsink_ref):
        n = pl.program_id(0)

        @pl.when(n == 0)
        def _():
            dsink_ref[...] = jnp.zeros_like(dsink_ref)

        kv = _kv_variants(jnp.concatenate([kp_ref[...], kc_ref[...]], axis=0))
        vv = _kv_variants(jnp.concatenate([vp_ref[...], vc_ref[...]], axis=0))
        i = lax.broadcasted_iota(jnp.int32, (BLOCK, 2 * BLOCK), 0)
        j = lax.broadcasted_iota(jnp.int32, (BLOCK, 2 * BLOCK), 1)
        valid = (j > i) & (j <= i + BLOCK) & ((n > 0) | (j >= BLOCK))
        lane = lax.broadcasted_iota(jnp.int32, (BLOCK, BLOCK), 1)
        row8 = lax.broadcasted_iota(jnp.int32, (N_ATTN_HEADS, BLOCK), 0)
        lse_tile = lse_ref[...]
        delta_all = jnp.zeros((BLOCK, BLOCK), F32)
        dsink = jnp.zeros((N_ATTN_HEADS, BLOCK), F32)
        for pair in range(N_ATTN_HEADS // 2):
            h = pair // 2
            qp = q_ref[:, pair * 128:(pair + 1) * 128]
            dop = da_ref[:, pair * 128:(pair + 1) * 128].astype(BF)
            acc = jnp.zeros((BLOCK, 128), F32)
            for e in range(2):
                head = 2 * pair + e
                lse_h = jnp.sum(jnp.where(lane == head, lse_tile, 0.0), axis=-1, keepdims=True)
                s = jnp.where(valid, _dot_nt(qp, kv[h][e]) * ATTN_SCALE, MASK_VALUE)
                p = jnp.exp(s - lse_h)
                dp = _dot_nt(dop, vv[h][e])
                delta = jnp.sum(p * dp, axis=-1, keepdims=True)
                ds = (p * (dp - delta) * ATTN_SCALE).astype(BF)
                acc = acc + _dot(ds, kv[h][e])
                delta_all = _col_to_tile(delta_all, delta, head)
                ps = jnp.exp(sink_ref[head] - lse_h)
                dsink = jnp.where(row8 == head, jnp.sum(-ps * delta, axis=0, keepdims=True), dsink)
            dq_ref[:, pair * 128:(pair + 1) * 128] = acc.astype(BF)
        deltat_ref[...] = delta_all.T[:N_ATTN_HEADS, :]
        dsink_ref[...] += dsink

    cur = lambda w: pl.BlockSpec((BLOCK, w), lambda n: (n, 0))
    prev = lambda w: pl.BlockSpec((BLOCK, w), lambda n: (jnp.maximum(n - 1, 0), 0))
    return _pcall(
        body, name="attn_bwd_dq", grid=(nb,),
        in_specs=[_smem_full(), cur(ATTN_W), cur(KV_W), prev(KV_W), cur(KV_W), prev(KV_W), cur(ATTN_W), cur(BLOCK)],
        out_specs=[cur(ATTN_W), pl.BlockSpec((N_ATTN_HEADS, BLOCK), lambda n: (0, n)),
                   _const((N_ATTN_HEADS, BLOCK))],
        out_shape=[jax.ShapeDtypeStruct((T, ATTN_W), BF), jax.ShapeDtypeStruct((N_ATTN_HEADS, T), F32),
                   jax.ShapeDtypeStruct((N_ATTN_HEADS, BLOCK), F32)],
        compiler_params=_params(("arbitrary",)),
    )(sinks, qa, ka, ka, va, va, da, lse)


def _attn_bwd_dkv(qa, ka, va, da, lse_t, delta_t):
    T = qa.shape[0]
    nb = T // BLOCK

    def body(qc_ref, qn_ref, dac_ref, dan_ref, k_ref, v_ref, lc_ref, ln_ref, dc_ref, dn_ref, dk_ref, dv_ref):
        m = pl.program_id(0)
        kv = _kv_variants(k_ref[...])
        vv = _kv_variants(v_ref[...])
        j = lax.broadcasted_iota(jnp.int32, (BLOCK, 2 * BLOCK), 0)
        c = lax.broadcasted_iota(jnp.int32, (BLOCK, 2 * BLOCK), 1)
        valid = ((c < BLOCK) & (j <= c)) | ((c >= BLOCK) & (j > c - BLOCK) & (m < nb - 1))
        lo = lax.broadcasted_iota(jnp.int32, (BLOCK, 128), 1) < HEAD_DIM
        dk = jnp.zeros((BLOCK, 128), F32)
        dv = jnp.zeros((BLOCK, 128), F32)
        for pair in range(N_ATTN_HEADS // 2):
            h = pair // 2
            ps = slice(pair * 128, (pair + 1) * 128)
            q2 = jnp.concatenate([qc_ref[:, ps], qn_ref[:, ps]], axis=0)
            do2 = jnp.concatenate([dac_ref[:, ps], dan_ref[:, ps]], axis=0).astype(BF)
            for e in range(2):
                head = 2 * pair + e
                lse = jnp.concatenate([lc_ref[pl.ds(head, 1), :], ln_ref[pl.ds(head, 1), :]], axis=1)
                delta = jnp.concatenate([dc_ref[pl.ds(head, 1), :], dn_ref[pl.ds(head, 1), :]], axis=1)
                st = jnp.where(valid, _dot_nt(kv[h][e], q2) * ATTN_SCALE, MASK_VALUE)
                pt = jnp.where(valid, jnp.exp(st - lse), 0.0)
                dpt = _dot_nt(vv[h][e], do2)
                dst = (pt * (dpt - delta) * ATTN_SCALE).astype(BF)
                half = lo if e == 0 else jnp.logical_not(lo)
                dv_e = jnp.where(half, _dot(pt.astype(BF), do2), 0.0)
                dk_e = jnp.where(half, _dot(dst, q2), 0.0)
                if e != h:
                    dv_e = pltpu.roll(dv_e, HEAD_DIM, 1)
                    dk_e = pltpu.roll(dk_e, HEAD_DIM, 1)
                dv = dv + dv_e
                dk = dk + dk_e
        dk_ref[...] = dk.astype(BF)
        dv_ref[...] = dv.astype(BF)

    cur = lambda w: pl.BlockSpec((BLOCK, w), lambda m: (m, 0))
    nxt = lambda w: pl.BlockSpec((BLOCK, w), lambda m: (jnp.minimum(m + 1, nb - 1), 0))
    tcur = pl.BlockSpec((N_ATTN_HEADS, BLOCK), lambda m: (0, m))
    tnxt = pl.BlockSpec((N_ATTN_HEADS, BLOCK), lambda m: (0, jnp.minimum(m + 1, nb - 1)))
    return _pcall(
        body, name="attn_bwd_dkv", grid=(nb,),
        in_specs=[cur(ATTN_W), nxt(ATTN_W), cur(ATTN_W), nxt(ATTN_W), cur(KV_W), cur(KV_W), tcur, tnxt, tcur, tnxt],
        out_specs=[cur(KV_W), cur(KV_W)],
        out_shape=[jax.ShapeDtypeStruct((T, KV_W), BF), jax.ShapeDtypeStruct((T, KV_W), BF)],
        compiler_params=_params(("parallel",)),
    )(qa, qa, da, da, ka, va, lse_t, lse_t, delta_t, delta_t)


def _in_proj_bwd(dqa, dka, dva, dret, w_in, x, g1, dx1, tm):
    T = x.shape[0]

    def body(dqa_ref, dka_ref, dva_ref, dret_ref, w_ref, x_ref, g_ref, dx1_ref, dx_ref, dg1_ref):
        @pl.when(pl.program_id(0) == 0)
        def _():
            dg1_ref[...] = jnp.zeros_like(dg1_ref)

        dh = (_dot_nt(dqa_ref[...], w_ref[:, QA0:QA0 + ATTN_W]) + _dot_nt(dka_ref[...], w_ref[:, KA0:KA0 + KV_W])
              + _dot_nt(dva_ref[...], w_ref[:, VA0:VA0 + KV_W]) + _dot_nt(dret_ref[...], w_ref[:, QR0:IN_W]))
        r, n = _rms_stats(x_ref[...])
        dg1_ref[...] += jnp.sum(dh * n, axis=0, keepdims=True)
        dx_ref[...] = dx1_ref[...] + _rms_bwd(n, r, dh * g_ref[...])

    return _pcall(
        body, name="in_proj_bwd", grid=(T // tm,),
        in_specs=[_rows(tm, ATTN_W), _rows(tm, KV_W), _rows(tm, KV_W), _rows(tm, 4 * RET_W), _vmem_full(),
                  _rows(tm, D_MODEL), _const((1, D_MODEL)), _rows(tm, D_MODEL)],
        out_specs=[_rows(tm, D_MODEL), _const((1, D_MODEL))],
        out_shape=[jax.ShapeDtypeStruct((T, D_MODEL), F32), jax.ShapeDtypeStruct((1, D_MODEL), F32)],
        compiler_params=_params(("arbitrary",)),
    )(dqa, dka, dva, dret, w_in, x, g1, dx1)


def _wgrad(a_list, b_list, tk, name):
    T = a_list[0].shape[0]
    na, nbb = len(a_list), len(b_list)
    m_sizes = [a.shape[1] for a in a_list]
    n_sizes = [b.shape[1] for b in b_list]
    M, N = sum(m_sizes), sum(n_sizes)
    nk = T // tk
    chunk = 512

    def body(*refs):
        a_refs, b_refs = refs[:na], refs[na:na + nbb]
        out_ref, acc = refs[na + nbb], refs[na + nbb + 1]
        k = pl.program_id(0)

        @pl.when(k == 0)
        def _():
            acc[...] = jnp.zeros_like(acc)

        r0 = 0
        for ai in range(na):
            a = a_refs[ai][...]
            c0 = 0
            for bi in range(nbb):
                for s in range(0, n_sizes[bi], chunk):
                    w = min(chunk, n_sizes[bi] - s)
                    acc[r0:r0 + m_sizes[ai], c0 + s:c0 + s + w] += _dot_tn(a, b_refs[bi][:, s:s + w])
                c0 += n_sizes[bi]
            r0 += m_sizes[ai]

        @pl.when(k == nk - 1)
        def _():
            pltpu.sync_copy(acc, out_ref)

    return _pcall(
        body, name=name, grid=(nk,),
        in_specs=[_rows(tk, w) for w in m_sizes + n_sizes],
        out_specs=pl.BlockSpec(memory_space=pl.ANY),
        out_shape=jax.ShapeDtypeStruct((M, N), F32),
        scratch_shapes=[pltpu.VMEM((M, N), F32)],
        compiler_params=_params(("arbitrary",)),
    )(*a_list, *b_list)


def _adamw_math(w, g, m, v):
    m = ADAM_B1 * m + (1.0 - ADAM_B1) * g
    v = ADAM_B2 * v + (1.0 - ADAM_B2) * (g * g)
    m_hat = m / (1.0 - ADAM_B1 ** ADAM_STEP)
    v_hat = v / (1.0 - ADAM_B2 ** ADAM_STEP)
    delta = -ADAM_LR * (m_hat / (jnp.sqrt(v_hat) + ADAM_EPS) + ADAM_WD * w)
    return delta, m, v


def _sum_parts(parts_ref):
    g = parts_ref[0]
    for i in range(1, N_DEV):
        g = g + parts_ref[i]
    return g


def _adamw_shard(parts, w, m, v, tr, name):
    R, C = w.shape

    def body(p_ref, w_ref, m_ref, v_ref, g_ref, d_ref, nm_ref, nv_ref):
        g = _sum_parts(p_ref)
        g_ref[...] = g
        d_ref[...], nm_ref[...], nv_ref[...] = _adamw_math(w_ref[...], g, m_ref[...], v_ref[...])

    blk = pl.BlockSpec((tr, C), lambda i: (i, 0))
    return _pcall(
        body, name=name, grid=(R // tr,),
        in_specs=[pl.BlockSpec((N_DEV, tr, C), lambda i: (0, i, 0)), blk, blk, blk],
        out_specs=[blk] * 4,
        out_shape=[jax.ShapeDtypeStruct((R, C), F32)] * 4,
        compiler_params=_params(("parallel",)),
    )(parts, w, m, v)


def _sum_small(parts):
    def body(p_ref, g_ref):
        g_ref[...] = _sum_parts(p_ref)

    return _pcall(body, name="sum_small", out_shape=jax.ShapeDtypeStruct(parts.shape[1:], F32),
                  in_specs=[_vmem_full()], out_specs=_vmem_full())(parts)


def _adamw_small(g, w, m, v, name):
    def body(g_ref, w_ref, m_ref, v_ref, d_ref, nm_ref, nv_ref):
        d_ref[...], nm_ref[...], nv_ref[...] = _adamw_math(w_ref[...], g_ref[...], m_ref[...], v_ref[...])

    return _pcall(body, name=name, out_shape=[jax.ShapeDtypeStruct(w.shape, F32)] * 3,
                  in_specs=[_vmem_full()] * 4, out_specs=[_vmem_full()] * 3)(g, w, m, v)


def _tables(T):
    h, c = N_RET_HEADS, BLOCK
    pos = jnp.arange(T, dtype=F32)
    angle = 1.0 / jnp.power(10000.0, jnp.linspace(0.0, 1.0, RET_HEAD_DIM // 2, dtype=F32))
    angle = jnp.repeat(angle, 2)
    sin = jnp.sin(pos[:, None] * angle[None])
    cos = jnp.cos(pos[:, None] * angle[None])
    even = (jnp.arange(RET_HEAD_DIM) % 2 == 0)[None, :]
    sin_s = jnp.where(even, -sin, sin)
    log_gamma = jnp.log(1.0 - jnp.power(2.0, -5.0 - jnp.arange(h, dtype=F32)))
    idx = jnp.arange(c, dtype=F32)
    rel = idx[:, None] - idx[None, :]
    d_intra = jnp.where(rel[None] >= 0, jnp.exp(log_gamma[:, None, None] * jnp.maximum(rel, 0.0)[None]), 0.0)
    xi = jnp.exp(log_gamma[None, :] * (idx[:, None] + 1.0))
    zeta = jnp.exp(log_gamma[None, :] * (c - 1.0 - idx[:, None]))
    decay = jnp.exp(log_gamma * c)
    xi_b = jnp.broadcast_to(xi.T[:, :, None], (h, c, RET_HEAD_DIM))
    zeta_b = jnp.broadcast_to(zeta.T[:, :, None], (h, c, RET_HEAD_DIM))
    return cos, sin_s, d_intra, jnp.swapaxes(d_intra, 1, 2), xi_b, zeta_b, decay


def _to_shards(full, cols):
    r = full.shape[0]
    return jnp.swapaxes(full.reshape(r, N_DEV, cols), 0, 1)


def _from_shards(sh):
    n, r, cols = sh.shape
    return jnp.swapaxes(sh, 0, 1).reshape(r, n * cols)


SMALL_ROWS = 216


def _pack_small(gains, conv_b, conv_w, sinks):
    parts = [g.reshape(8, 128) for g in gains] + [conv_b.reshape(44, 128), conv_w.reshape(132, 128),
                                                  jnp.pad(sinks.reshape(1, 8), ((0, 0), (0, 120)))]
    packed = jnp.concatenate(parts, axis=0)
    return jnp.pad(packed, ((0, SMALL_ROWS - packed.shape[0]), (0, 0)))


def kernel(x, mix_pre_norm, w_in, attn_sinks, w_out, mix_post_norm, ffn_pre_norm, w_up, conv_w, conv_b, w_down, ffn_post_norm, loss_target, m_mix_pre_norm, m_w_in, m_attn_sinks, m_w_out, m_mix_post_norm, m_ffn_pre_norm, m_w_up, m_conv_w, m_conv_b, m_w_down, m_ffn_post_norm, v_mix_pre_norm, v_w_in, v_attn_sinks, v_w_out, v_mix_post_norm, v_ffn_pre_norm, v_w_up, v_conv_w, v_conv_b, v_w_down, v_ffn_post_norm):
    T = x.shape[1]
    tm = min(512, T)
    tm_ff = min(256, T)
    x2 = x.reshape(T, D_MODEL)
    target = loss_target.reshape(T, D_MODEL)
    me = 4 * lax.axis_index("x") + 2 * lax.axis_index("y") + lax.axis_index("c")

    g_in, g_up, g_out, g_down, g_cw = _all_gather(
        [w_in[0].astype(BF), w_up[0].astype(BF), w_out[0].astype(BF), w_down[0].astype(BF), conv_w[0]],
        "gather_weights")
    w_in_f = _from_shards(g_in)
    w_up_f = _from_shards(g_up)
    w_out_f = g_out.reshape(D_MODEL, D_MODEL)
    w_down_f = g_down.reshape(D_FF, D_MODEL)
    conv_w_f = _from_shards(g_cw)
    cos, sin_s, d_intra, d_intra_t, xi_b, zeta_b, decay = _tables(T)
    sinks = attn_sinks.reshape(N_ATTN_HEADS)

    h1, qa, ka, va, qr, kr, vr, gr = _in_proj(x2, mix_pre_norm, w_in_f, cos, sin_s, tm)
    a, lse, lse_t = _attn_fwd(sinks, qa, ka, va)
    o, states, r = _ret_fwd(decay, qr, kr, vr, gr, d_intra, xi_b, zeta_b)
    mixed, x1, h2 = _out_proj(a, r, w_out_f, x2, mix_post_norm, ffn_pre_norm, tm)
    up, y = _ffn_up(h2, w_up_f, conv_w_f, conv_b, tm_ff)
    z, dout, loss_acc = _ffn_down(y, w_down_f, x1, ffn_post_norm, target, tm)
    loss = lax.psum(loss_acc[0, 0], ("x", "y", "c"))

    dz, dup, dg4, dcb, dcw = _ffn_bwd_a(z, dout, ffn_post_norm, w_down_f, up, conv_w_f, conv_b, tm_ff)
    dx1, dmixed, da, dr, dg3, dg2 = _ffn_bwd_b(dup, w_up_f, x1, dout, ffn_pre_norm, mixed, mix_post_norm,
                                               w_out_f, tm_ff)
    dret = _ret_bwd(decay, qr, kr, vr, gr, o, states, dr, d_intra, d_intra_t, xi_b, zeta_b, cos, sin_s)
    dqa, delta_t, dsink = _attn_bwd_dq(sinks, qa, ka, va, da, lse)
    dka, dva = _attn_bwd_dkv(qa, ka, va, da, lse_t, delta_t)
    grad_x, dg1 = _in_proj_bwd(dqa, dka, dva, dret, w_in_f, x2, mix_pre_norm, dx1, tm)
    gw_down = _wgrad([y], [dz], tm, "wgrad_down")
    gw_up = _wgrad([h2], [dup], tm, "wgrad_up")
    gw_out = _wgrad([a, r], [dmixed], tm, "wgrad_out")
    gw_in = _wgrad([h1], [dqa, dka, dva, dret], tm, "wgrad_in")

    small = _pack_small([dg1, dg2, dg3, dg4], dcb, dcw, dsink[:, 0])
    p_in, p_up, p_out, p_down = _all_to_all(
        [_to_shards(gw_in, IN_W // N_DEV), _to_shards(gw_up, 2 * D_FF // N_DEV),
         gw_out.reshape(N_DEV, D_MODEL // N_DEV, D_MODEL), gw_down.reshape(N_DEV, D_FF // N_DEV, D_MODEL)],
        "exchange_grads")
    (small_all,) = _all_gather([small], "gather_small")
    g_small = _sum_small(small_all)

    g_w_in, d_w_in, nm_w_in, nv_w_in = _adamw_shard(p_in, w_in[0], m_w_in[0], v_w_in[0], 256, "adamw_in")
    g_w_up, d_w_up, nm_w_up, nv_w_up = _adamw_shard(p_up, w_up[0], m_w_up[0], v_w_up[0], 256, "adamw_up")
    g_w_out, d_w_out, nm_w_out, nv_w_out = _adamw_shard(p_out, w_out[0], m_w_out[0], v_w_out[0], 128, "adamw_out")
    g_w_down, d_w_down, nm_w_down, nv_w_down = _adamw_shard(p_down, w_down[0], m_w_down[0], v_w_down[0], 176,
                                                            "adamw_down")
    gains = [mix_pre_norm, mix_post_norm, ffn_pre_norm, ffn_post_norm]
    m_gains = [m_mix_pre_norm, m_mix_post_norm, m_ffn_pre_norm, m_ffn_post_norm]
    v_gains = [v_mix_pre_norm, v_mix_post_norm, v_ffn_pre_norm, v_ffn_post_norm]
    zeros_cw = jnp.zeros((3, 2 * D_FF), F32)
    w_small = _pack_small(gains, conv_b, zeros_cw, attn_sinks)
    m_small = _pack_small(m_gains, m_conv_b, zeros_cw, m_attn_sinks)
    v_small = _pack_small(v_gains, v_conv_b, zeros_cw, v_attn_sinks)
    d_small, nm_small, nv_small = _adamw_small(g_small, w_small, m_small, v_small, "adamw_small")
    shard_cols = 2 * D_FF // N_DEV
    g_cw = lax.dynamic_slice(g_small[76:208].reshape(3, 2 * D_FF), (0, me * shard_cols), (3, shard_cols))
    d_cw, nm_cw, nv_cw = _adamw_small(g_cw, conv_w[0], m_conv_w[0], v_conv_w[0], "adamw_conv_w")

    def unpack(p):
        gains_o = [p[8 * i:8 * i + 8].reshape(1, D_MODEL) for i in range(4)]
        return gains_o, p[32:76].reshape(1, 2 * D_FF), p[208:209, :N_ATTN_HEADS]

    def leaves(p, w_in_s, w_out_s, w_up_s, cw_s, w_down_s):
        (pre1, post1, pre2, post2), cb, sk = unpack(p)
        return [pre1, w_in_s[None], sk, w_out_s[None], post1, pre2, w_up_s[None], cw_s[None], cb, w_down_s[None],
                post2]

    return (loss, grad_x.reshape(1, T, D_MODEL),
            *leaves(g_small, g_w_in, g_w_out, g_w_up, g_cw, g_w_down),
            *leaves(d_small, d_w_in, d_w_out, d_w_up, d_cw, d_w_down),
            *leaves(nm_small, nm_w_in, nm_w_out, nm_w_up, nm_cw, nm_w_down),
            *leaves(nv_small, nv_w_in, nv_w_out, nv_w_up, nv_cw, nv_w_down))
```

```python
import functools
import math

import jax
import jax.numpy as jnp
from jax import lax
from jax.experimental import pallas as pl
from jax.experimental.pallas import tpu as pltpu

F32 = jnp.float32
BF = jnp.bfloat16

N_DEV = 8
D_MODEL = 1024
HEAD_DIM = 64
ATTN_W = 512
N_ATTN_HEADS = 8
KV_W = 128
BLOCK = 128
RET_W = 512
N_RET_HEADS = 4
RET_HEAD_DIM = 128
IN_W = 2816
D_FF = 2816
RMS_EPS = 1e-6
GN_EPS = 1e-6
MASK_VALUE = -1e30
ATTN_SCALE = HEAD_DIM ** -0.5
RET_K_SCALE = RET_HEAD_DIM ** -0.5
GELU_C = math.sqrt(2.0 / math.pi)
GELU_A = 0.044715

ADAM_LR = 0.001
ADAM_B1 = 0.9
ADAM_B2 = 0.999
ADAM_EPS = 1e-08
ADAM_WD = 0.01
ADAM_STEP = 10

VMEM_LIMIT_BYTES = 56 * 1024 * 1024
FF_CHUNK = 256
N_FF_CHUNKS = D_FF // FF_CHUNK

QA0, KA0, VA0, QR0, KR0, VR0, GR0 = 0, 512, 640, 768, 1280, 1792, 2304

MESH_ID = pl.DeviceIdType.MESH


def _pcall(body, **kw):
    return pl.pallas_call(body, **kw)


def _params(sem=None):
    return pltpu.CompilerParams(dimension_semantics=sem, vmem_limit_bytes=VMEM_LIMIT_BYTES)


def _dot(a, b):
    return jnp.dot(a, b, preferred_element_type=F32)


def _dot_nt(a, b):
    return lax.dot_general(a, b, (((1,), (1,)), ((), ())), preferred_element_type=F32)


def _dot_tn(a, b):
    return lax.dot_general(a, b, (((0,), (0,)), ((), ())), preferred_element_type=F32)


def _vmem_full():
    return pl.BlockSpec(memory_space=pltpu.VMEM)


def _smem_full():
    return pl.BlockSpec(memory_space=pltpu.SMEM)


def _rows(tm, w):
    return pl.BlockSpec((tm, w), lambda i: (i, 0))


def _const(shape):
    return pl.BlockSpec(shape, lambda i: tuple(0 for _ in shape))


def _rms_stats(x):
    r = lax.rsqrt(jnp.mean(x * x, axis=-1, keepdims=True) + RMS_EPS)
    return r, x * r


def _rms_bwd(n, r, dn):
    return r * (dn - n * jnp.mean(dn * n, axis=-1, keepdims=True))


def _rot(x, even):
    w = x.shape[1]
    return jnp.where(even, pltpu.roll(x, w - 1, 1), pltpu.roll(x, 1, 1))


def _peers():
    x, y, c = lax.axis_index("x"), lax.axis_index("y"), lax.axis_index("c")
    flips = [(0, 0, 1), (1, 0, 0), (0, 1, 0), (1, 1, 0), (1, 0, 1), (0, 1, 1), (1, 1, 1)]
    peers = [(x ^ fx, y ^ fy, c ^ fc) for fx, fy, fc in flips]
    return 4 * x + 2 * y + c, peers


class _Exchange:
    def __init__(self, gathers, swaps, send_sems, recv_sems, local_sems):
        self.me, self.peers = _peers()
        self.slots = [4 * px + 2 * py + pc for px, py, pc in self.peers]
        self.pairs = [(src, dst, True) for src, dst in gathers] + [(src, dst, False) for src, dst in swaps]
        self.send_sems, self.recv_sems, self.local_sems = send_sems, recv_sems, local_sems

    @staticmethod
    def scratch(n):
        return [pltpu.SemaphoreType.DMA((n, N_DEV - 1)), pltpu.SemaphoreType.DMA((n, N_DEV - 1)),
                pltpu.SemaphoreType.DMA((n,))]

    def _local(self, a):
        src, dst, whole = self.pairs[a]
        return pltpu.make_async_copy(src if whole else src.at[self.me], dst.at[self.me], self.local_sems.at[a])

    def _remote(self, a, k, outgoing):
        src, dst, whole = self.pairs[a]
        if outgoing:
            s, d = (src if whole else src.at[self.slots[k]]), dst.at[self.me]
        else:
            s, d = (src if whole else src.at[self.me]), dst.at[self.slots[k]]
        return pltpu.make_async_remote_copy(
            src_ref=s, dst_ref=d, send_sem=self.send_sems.at[a, k], recv_sem=self.recv_sems.at[a, k],
            device_id=self.peers[k], device_id_type=MESH_ID)

    def start(self):
        for a in range(len(self.pairs)):
            self._local(a).start()
            for k in range(N_DEV - 1):
                self._remote(a, k, True).start()

    def wait(self):
        for a in range(len(self.pairs)):
            for k in range(N_DEV - 1):
                self._remote(a, k, False).wait_recv()
        for a in range(len(self.pairs)):
            for k in range(N_DEV - 1):
                self._remote(a, k, True).wait_send()
            self._local(a).wait()


ANY_SPEC = pl.BlockSpec(memory_space=pl.ANY)


def _exchange_shapes(gathers, swaps):
    return ([jax.ShapeDtypeStruct((N_DEV,) + a.shape, a.dtype) for a in gathers]
            + [jax.ShapeDtypeStruct(a.shape, a.dtype) for a in swaps])


def _exchange_of(ins, outs, sems, ng):
    return _Exchange(list(zip(ins[:ng], outs[:ng])), list(zip(ins[ng:], outs[ng:])), *sems)


def _exchange_call(gathers, swaps, name):
    ng, ns = len(gathers), len(swaps)
    n = ng + ns

    def body(*refs):
        ex = _exchange_of(refs[:n], refs[n:2 * n], refs[2 * n:], ng)
        ex.start()
        ex.wait()

    return _pcall(
        body, name=name, out_shape=_exchange_shapes(gathers, swaps),
        in_specs=[ANY_SPEC] * (ng + ns), out_specs=[ANY_SPEC] * (ng + ns),
        scratch_shapes=_Exchange.scratch(ng + ns),
    )(*gathers, *swaps)


def _in_proj(x, g1, w_in, cos, sin_s, tm, gathers):
    T = x.shape[0]
    ng = len(gathers)
    nt = T // tm

    def body(x_ref, g_ref, w_ref, cos_ref, sin_ref, *rest):
        ex = _exchange_of(rest[:ng], rest[ng + 8:2 * ng + 8], rest[2 * ng + 8:], ng)
        h_ref, qa_ref, ka_ref, va_ref, qr_ref, kr_ref, vr_ref, gr_ref = rest[ng:ng + 8]
        pl.when(pl.program_id(0) == 0)(ex.start)
        r, n = _rms_stats(x_ref[...])
        h = (n * g_ref[...]).astype(BF)
        h_ref[...] = h

        def proj(c0, w):
            return _dot(h, w_ref[:, c0:c0 + w])

        qa_ref[...] = proj(QA0, ATTN_W).astype(BF)
        ka_ref[...] = proj(KA0, KV_W).astype(BF)
        va_ref[...] = proj(VA0, KV_W).astype(BF)
        vr_ref[...] = proj(VR0, RET_W).astype(BF)
        gr_ref[...] = proj(GR0, RET_W)
        cos_t, sin_t = cos_ref[...], sin_ref[...]
        even = lax.broadcasted_iota(jnp.int32, (tm, RET_HEAD_DIM), 1) % 2 == 0
        for hd in range(N_RET_HEADS):
            c = hd * RET_HEAD_DIM
            q = proj(QR0 + c, RET_HEAD_DIM)
            k = proj(KR0 + c, RET_HEAD_DIM) * RET_K_SCALE
            qr_ref[:, c:c + RET_HEAD_DIM] = (q * cos_t + _rot(q, even) * sin_t).astype(BF)
            kr_ref[:, c:c + RET_HEAD_DIM] = (k * cos_t + _rot(k, even) * sin_t).astype(BF)
        pl.when(pl.program_id(0) == nt - 1)(ex.wait)

    widths = [D_MODEL, ATTN_W, KV_W, KV_W, RET_W, RET_W, RET_W, RET_W]
    dts = [BF] * 7 + [F32]
    outs = _pcall(
        body, name="in_proj", grid=(nt,),
        in_specs=[_rows(tm, D_MODEL), _const((1, D_MODEL)), _vmem_full(), _rows(tm, RET_HEAD_DIM),
                  _rows(tm, RET_HEAD_DIM)] + [ANY_SPEC] * ng,
        out_specs=[_rows(tm, w) for w in widths] + [ANY_SPEC] * ng,
        out_shape=[jax.ShapeDtypeStruct((T, w), dt) for w, dt in zip(widths, dts)] + _exchange_shapes(gathers, []),
        scratch_shapes=_Exchange.scratch(ng),
        compiler_params=_params(("arbitrary",)),
    )(x, g1, w_in, cos, sin_s, *gathers)
    return outs[:8], outs[8:]


def _kv_variants(kk):
    kf = kk.astype(F32)
    lo = lax.broadcasted_iota(jnp.int32, kf.shape, 1) < HEAD_DIM
    h0_lo = jnp.where(lo, kf, 0.0)
    h1_hi = jnp.where(lo, 0.0, kf)
    h0_hi = pltpu.roll(h0_lo, HEAD_DIM, 1)
    h1_lo = pltpu.roll(h1_hi, HEAD_DIM, 1)
    return [[h0_lo.astype(BF), h0_hi.astype(BF)], [h1_lo.astype(BF), h1_hi.astype(BF)]]


def _col_to_tile(tile, col, head):
    lane = lax.broadcasted_iota(jnp.int32, tile.shape, 1)
    return jnp.where(lane == head, col, tile)


def _attn_fwd(sinks, qa, ka, va):
    T = qa.shape[0]
    nb = T // BLOCK

    def body(sink_ref, q_ref, kc_ref, kp_ref, vc_ref, vp_ref, a_ref, lse_ref, lset_ref):
        n = pl.program_id(0)
        kv = _kv_variants(jnp.concatenate([kp_ref[...], kc_ref[...]], axis=0))
        vv = _kv_variants(jnp.concatenate([vp_ref[...], vc_ref[...]], axis=0))
        i = lax.broadcasted_iota(jnp.int32, (BLOCK, 2 * BLOCK), 0)
        j = lax.broadcasted_iota(jnp.int32, (BLOCK, 2 * BLOCK), 1)
        valid = (j > i) & (j <= i + BLOCK) & ((n > 0) | (j >= BLOCK))
        lse_all = jnp.zeros((BLOCK, BLOCK), F32)
        for pair in range(N_ATTN_HEADS // 2):
            h = pair // 2
            qp = q_ref[:, pair * 128:(pair + 1) * 128]
            acc = jnp.zeros((BLOCK, 128), F32)
            for e in range(2):
                head = 2 * pair + e
                sink = sink_ref[head]
                s = jnp.where(valid, _dot_nt(qp, kv[h][e]) * ATTN_SCALE, MASK_VALUE)
                m = jnp.maximum(jnp.max(s, axis=-1, keepdims=True), sink)
                p = jnp.exp(s - m)
                z = jnp.sum(p, axis=-1, keepdims=True) + jnp.exp(sink - m)
                acc = acc + _dot((p * (1.0 / z)).astype(BF), vv[h][e])
                lse_all = _col_to_tile(lse_all, m + jnp.log(z), head)
            a_ref[:, pair * 128:(pair + 1) * 128] = acc.astype(BF)
        lse_ref[...] = lse_all
        lset_ref[...] = lse_all.T[:N_ATTN_HEADS, :]

    cur = lambda w: pl.BlockSpec((BLOCK, w), lambda n: (n, 0))
    prev = lambda w: pl.BlockSpec((BLOCK, w), lambda n: (jnp.maximum(n - 1, 0), 0))
    return _pcall(
        body, name="attn_fwd", grid=(nb,),
        in_specs=[_smem_full(), cur(ATTN_W), cur(KV_W), prev(KV_W), cur(KV_W), prev(KV_W)],
        out_specs=[cur(ATTN_W), cur(BLOCK), pl.BlockSpec((N_ATTN_HEADS, BLOCK), lambda n: (0, n))],
        out_shape=[jax.ShapeDtypeStruct((T, ATTN_W), BF), jax.ShapeDtypeStruct((T, BLOCK), F32),
                   jax.ShapeDtypeStruct((N_ATTN_HEADS, T), F32)],
        compiler_params=_params(("parallel",)),
    )(sinks, qa, ka, ka, va, va)


def _ret_fwd(decay, qr, kr, vr, gr, d_intra, xi_b, zeta_b):
    T = qr.shape[0]
    nc = T // BLOCK
    H, C = N_RET_HEADS, RET_HEAD_DIM

    def body(decay_ref, q_ref, k_ref, v_ref, g_ref, d_ref, xi_ref, zeta_ref, o_ref, s_ref, r_ref, state):
        @pl.when(pl.program_id(0) == 0)
        def _():
            state[...] = jnp.zeros_like(state)

        for h in range(H):
            cs = slice(h * C, (h + 1) * C)
            q, k, v = q_ref[:, cs], k_ref[:, cs], v_ref[:, cs]
            st = state[h]
            st_b = st.astype(BF)
            s_ref[0, h] = st_b
            inner = (_dot_nt(q, k) * d_ref[h]).astype(BF)
            o = _dot(inner, v) + _dot(q, st_b) * xi_ref[h]
            kz = (k.astype(F32) * zeta_ref[h]).astype(BF)
            state[h] = decay_ref[h] * st + _dot_tn(kz, v)
            o_ref[:, cs] = o
            mu = jnp.mean(o, axis=-1, keepdims=True)
            oc = o - mu
            rs = lax.rsqrt(jnp.mean(oc * oc, axis=-1, keepdims=True) + GN_EPS)
            g = g_ref[:, cs]
            r_ref[:, cs] = (g * jax.nn.sigmoid(g) * (oc * rs)).astype(BF)

    cur = pl.BlockSpec((BLOCK, RET_W), lambda n: (n, 0))
    tab = pl.BlockSpec((H, C, C), lambda n: (0, 0, 0))
    return _pcall(
        body, name="ret_fwd", grid=(nc,),
        in_specs=[_smem_full(), cur, cur, cur, cur, tab, tab, tab],
        out_specs=[cur, pl.BlockSpec((1, H, C, C), lambda n: (n, 0, 0, 0)), cur],
        out_shape=[jax.ShapeDtypeStruct((T, RET_W), F32), jax.ShapeDtypeStruct((nc, H, C, C), BF),
                   jax.ShapeDtypeStruct((T, RET_W), BF)],
        scratch_shapes=[pltpu.VMEM((H, C, C), F32)],
        compiler_params=_params(("arbitrary",)),
    )(decay, qr, kr, vr, gr, d_intra, xi_b, zeta_b)


def _out_proj(a, r, w_out, x, g2, g3, tm):
    T = x.shape[0]

    def body(a_ref, r_ref, w_ref, x_ref, g2_ref, g3_ref, mixed_ref, x1_ref, h2_ref):
        mixed = _dot(a_ref[...], w_ref[:ATTN_W, :]) + _dot(r_ref[...], w_ref[ATTN_W:, :])
        mixed_ref[...] = mixed
        _, n2 = _rms_stats(mixed)
        x1 = x_ref[...] + n2 * g2_ref[...]
        x1_ref[...] = x1
        _, n3 = _rms_stats(x1)
        h2_ref[...] = (n3 * g3_ref[...]).astype(BF)

    return _pcall(
        body, name="out_proj", grid=(T // tm,),
        in_specs=[_rows(tm, ATTN_W), _rows(tm, RET_W), _vmem_full(), _rows(tm, D_MODEL), _const((1, D_MODEL)),
                  _const((1, D_MODEL))],
        out_specs=[_rows(tm, D_MODEL)] * 3,
        out_shape=[jax.ShapeDtypeStruct((T, D_MODEL), F32), jax.ShapeDtypeStruct((T, D_MODEL), F32),
                   jax.ShapeDtypeStruct((T, D_MODEL), BF)],
        compiler_params=_params(("parallel",)),
    )(a, r, w_out, x, g2, g3)


def _shift_down(cur, k, before):
    out = pltpu.roll(cur, k, 0)
    row = lax.broadcasted_iota(jnp.int32, before.shape, 0)
    top = jnp.where(row < k, pltpu.roll(before, k, 0), out[0:8])
    return jnp.concatenate([top, out[8:]], axis=0)


def _shift_up(cur, k, after):
    tm = cur.shape[0]
    out = pltpu.roll(cur, tm - k, 0)
    row = lax.broadcasted_iota(jnp.int32, after.shape, 0)
    bot = jnp.where(row >= 8 - k, pltpu.roll(after, 8 - k, 0), out[tm - 8:])
    return jnp.concatenate([out[:tm - 8], bot], axis=0)


def _gelu_parts(x):
    x2 = x * x
    th = jnp.tanh(GELU_C * (x + GELU_A * x * x2))
    gelu = 0.5 * x * (1.0 + th)
    dgelu = 0.5 * (1.0 + th) + 0.5 * x * (1.0 - th * th) * (GELU_C * (1.0 + 3.0 * GELU_A * x2))
    return gelu, dgelu


def _ffn_up(h2, w_up, conv_w, conv_b, tm):
    T = h2.shape[0]

    def body(h_ref, w_ref, cw_ref, cb_ref, upb_ref, u_ref, y_ref, halo):
        first = pl.program_id(0) == 0
        h = h_ref[...]
        for c in range(N_FF_CHUNKS):
            u = []
            for part in range(2):
                cols = slice(part * D_FF + c * FF_CHUNK, part * D_FF + (c + 1) * FF_CHUNK)
                cur = _dot(h, w_ref[:, cols])
                upb_ref[:, cols] = cur.astype(BF)
                before = jnp.where(first, 0.0, halo[:, cols])
                halo[:, cols] = cur[tm - 8:tm, :]
                u_c = (cw_ref[pl.ds(0, 1), cols] * _shift_down(cur, 2, before)
                       + cw_ref[pl.ds(1, 1), cols] * _shift_down(cur, 1, before)
                       + cw_ref[pl.ds(2, 1), cols] * cur + cb_ref[:, cols])
                u_ref[:, cols] = u_c
                u.append(u_c)
            gelu, _ = _gelu_parts(u[0])
            y_ref[:, c * FF_CHUNK:(c + 1) * FF_CHUNK] = (gelu * u[1]).astype(BF)

    return _pcall(
        body, name="ffn_up", grid=(T // tm,),
        in_specs=[_rows(tm, D_MODEL), _vmem_full(), _const((3, 2 * D_FF)), _const((1, 2 * D_FF))],
        out_specs=[_rows(tm, 2 * D_FF), _rows(tm, 2 * D_FF), _rows(tm, D_FF)],
        out_shape=[jax.ShapeDtypeStruct((T, 2 * D_FF), BF), jax.ShapeDtypeStruct((T, 2 * D_FF), F32),
                   jax.ShapeDtypeStruct((T, D_FF), BF)],
        scratch_shapes=[pltpu.VMEM((8, 2 * D_FF), F32)],
        compiler_params=_params(("arbitrary",)),
    )(h2, w_up, conv_w, conv_b)


def _ffn_down(y, w_down, x1, g4, target, tm):
    T = y.shape[0]

    def body(y_ref, w_ref, x1_ref, g_ref, t_ref, z_ref, dout_ref, loss_ref):
        @pl.when(pl.program_id(0) == 0)
        def _():
            loss_ref[...] = jnp.zeros_like(loss_ref)

        z = _dot(y_ref[...], w_ref[...])
        z_ref[...] = z
        _, n4 = _rms_stats(z)
        err = x1_ref[...] + n4 * g_ref[...] - t_ref[...]
        dout_ref[...] = err * (1.0 / D_MODEL)
        loss_ref[...] += 0.5 * jnp.sum(jnp.mean(err * err, axis=-1, keepdims=True), axis=0, keepdims=True)

    return _pcall(
        body, name="ffn_down", grid=(T // tm,),
        in_specs=[_rows(tm, D_FF), _vmem_full(), _rows(tm, D_MODEL), _const((1, D_MODEL)), _rows(tm, D_MODEL)],
        out_specs=[_rows(tm, D_MODEL), _rows(tm, D_MODEL), _const((8, 128))],
        out_shape=[jax.ShapeDtypeStruct((T, D_MODEL), F32), jax.ShapeDtypeStruct((T, D_MODEL), F32),
                   jax.ShapeDtypeStruct((8, 128), F32)],
        compiler_params=_params(("arbitrary",)),
    )(y, w_down, x1, g4, target)


def _ffn_bwd_a(z, dout, g4, w_down, u, up_b, conv_w, tm):
    T = z.shape[0]
    nt = T // tm

    def body(z_ref, dout_ref, g_ref, w_ref, u_ref, up_ref, cw_ref, dz_ref, dup_ref, dg4_ref, dcb_ref, dcw_ref, carry):
        @pl.when(pl.program_id(0) == 0)
        def _():
            dg4_ref[...] = jnp.zeros_like(dg4_ref)
            dcb_ref[...] = jnp.zeros_like(dcb_ref)
            dcw_ref[...] = jnp.zeros_like(dcw_ref)
            carry[...] = jnp.zeros_like(carry)

        r4, n4 = _rms_stats(z_ref[...])
        dout = dout_ref[...]
        dg4_ref[...] += jnp.sum(dout * n4, axis=0, keepdims=True)
        dz = _rms_bwd(n4, r4, dout * g_ref[...]).astype(BF)
        dz_ref[...] = dz
        for c in range(N_FF_CHUNKS):
            gate = slice(c * FF_CHUNK, (c + 1) * FF_CHUNK)
            val = slice(D_FF + c * FF_CHUNK, D_FF + (c + 1) * FF_CHUNK)
            dy = _dot_nt(dz, w_ref[gate, :])
            gelu, dgelu = _gelu_parts(u_ref[:, gate])
            for cols, d in ((gate, dy * u_ref[:, val] * dgelu), (val, dy * gelu)):
                after = carry[:, cols]
                d1 = _shift_up(d, 1, after)
                d2 = _shift_up(d, 2, after)
                upc = up_ref[:, cols].astype(F32)
                dcb_ref[:, cols] += jnp.sum(d, axis=0, keepdims=True)
                dcw_ref[pl.ds(2, 1), cols] += jnp.sum(d * upc, axis=0, keepdims=True)
                dcw_ref[pl.ds(1, 1), cols] += jnp.sum(d1 * upc, axis=0, keepdims=True)
                dcw_ref[pl.ds(0, 1), cols] += jnp.sum(d2 * upc, axis=0, keepdims=True)
                dup_ref[:, cols] = (cw_ref[pl.ds(2, 1), cols] * d + cw_ref[pl.ds(1, 1), cols] * d1
                                    + cw_ref[pl.ds(0, 1), cols] * d2).astype(BF)
                carry[:, cols] = d[0:8, :]

    rev = lambda w: pl.BlockSpec((tm, w), lambda i: (nt - 1 - i, 0))
    return _pcall(
        body, name="ffn_bwd_a", grid=(nt,),
        in_specs=[rev(D_MODEL), rev(D_MODEL), _const((1, D_MODEL)), _vmem_full(), rev(2 * D_FF), rev(2 * D_FF),
                  _const((3, 2 * D_FF))],
        out_specs=[rev(D_MODEL), rev(2 * D_FF), _const((1, D_MODEL)), _const((1, 2 * D_FF)),
                   _const((3, 2 * D_FF))],
        out_shape=[jax.ShapeDtypeStruct((T, D_MODEL), BF), jax.ShapeDtypeStruct((T, 2 * D_FF), BF),
                   jax.ShapeDtypeStruct((1, D_MODEL), F32), jax.ShapeDtypeStruct((1, 2 * D_FF), F32),
                   jax.ShapeDtypeStruct((3, 2 * D_FF), F32)],
        scratch_shapes=[pltpu.VMEM((8, 2 * D_FF), F32)],
        compiler_params=_params(("arbitrary",)),
    )(z, dout, g4, w_down, u, up_b, conv_w)


def _ffn_bwd_b(dup, w_up, x1, dout, g3, mixed, g2, w_out, tm, swaps):
    T = x1.shape[0]
    ns = len(swaps)
    nt = T // tm

    def body(dup_ref, wup_ref, x1_ref, dout_ref, g3_ref, mixed_ref, g2_ref, wout_ref, *rest):
        ex = _exchange_of(rest[:ns], rest[ns + 6:2 * ns + 6], rest[2 * ns + 6:], 0)
        dx1_ref, dmixed_ref, da_ref, dr_ref, dg3_ref, dg2_ref = rest[ns:ns + 6]

        @pl.when(pl.program_id(0) == 0)
        def _():
            ex.start()
            dg3_ref[...] = jnp.zeros_like(dg3_ref)
            dg2_ref[...] = jnp.zeros_like(dg2_ref)

        dh2 = _dot_nt(dup_ref[...], wup_ref[...])
        r3, n3 = _rms_stats(x1_ref[...])
        dg3_ref[...] += jnp.sum(dh2 * n3, axis=0, keepdims=True)
        dx1 = dout_ref[...] + _rms_bwd(n3, r3, dh2 * g3_ref[...])
        dx1_ref[...] = dx1
        r2, n2 = _rms_stats(mixed_ref[...])
        dg2_ref[...] += jnp.sum(dx1 * n2, axis=0, keepdims=True)
        dmixed = _rms_bwd(n2, r2, dx1 * g2_ref[...]).astype(BF)
        dmixed_ref[...] = dmixed
        da_ref[...] = _dot_nt(dmixed, wout_ref[:ATTN_W, :])
        dr_ref[...] = _dot_nt(dmixed, wout_ref[ATTN_W:, :])
        pl.when(pl.program_id(0) == nt - 1)(ex.wait)

    outs = _pcall(
        body, name="ffn_bwd_b", grid=(nt,),
        in_specs=[_rows(tm, 2 * D_FF), _vmem_full(), _rows(tm, D_MODEL), _rows(tm, D_MODEL), _const((1, D_MODEL)),
                  _rows(tm, D_MODEL), _const((1, D_MODEL)), _vmem_full()] + [ANY_SPEC] * ns,
        out_specs=[_rows(tm, D_MODEL), _rows(tm, D_MODEL), _rows(tm, ATTN_W), _rows(tm, RET_W),
                   _const((1, D_MODEL)), _const((1, D_MODEL))] + [ANY_SPEC] * ns,
        out_shape=[jax.ShapeDtypeStruct((T, D_MODEL), F32), jax.ShapeDtypeStruct((T, D_MODEL), BF),
                   jax.ShapeDtypeStruct((T, ATTN_W), F32), jax.ShapeDtypeStruct((T, RET_W), F32),
                   jax.ShapeDtypeStruct((1, D_MODEL), F32), jax.ShapeDtypeStruct((1, D_MODEL), F32)]
        + _exchange_shapes([], swaps),
        scratch_shapes=_Exchange.scratch(ns),
        compiler_params=_params(("arbitrary",)),
    )(dup, w_up, x1, dout, g3, mixed, g2, w_out, *swaps)
    return outs[:6], outs[6:]


def _ret_bwd(decay, qr, kr, vr, gr, o, states, dr, d_intra, d_intra_t, xi_b, zeta_b, cos, sin_s, swaps):
    T = qr.shape[0]
    nc = T // BLOCK
    H, C = N_RET_HEADS, RET_HEAD_DIM
    ns = len(swaps)

    def body(decay_ref, q_ref, k_ref, v_ref, g_ref, o_ref, s_ref, dr_ref, d_ref, dt_ref, xi_ref, zeta_ref,
             cos_ref, sin_ref, *rest):
        ex = _exchange_of(rest[:ns], rest[ns + 1:2 * ns + 1], rest[2 * ns + 2:], 0)
        dret_ref, gstate = rest[ns], rest[2 * ns + 1]

        @pl.when(pl.program_id(0) == 0)
        def _():
            ex.start()
            gstate[...] = jnp.zeros_like(gstate)

        cos_t, sin_t = cos_ref[...], sin_ref[...]
        even = lax.broadcasted_iota(jnp.int32, (BLOCK, C), 1) % 2 == 0
        for h in range(H):
            cs = slice(h * C, (h + 1) * C)
            q, k, v = q_ref[:, cs], k_ref[:, cs], v_ref[:, cs]
            g, o_h, dr_h = g_ref[:, cs], o_ref[:, cs], dr_ref[:, cs]
            mu = jnp.mean(o_h, axis=-1, keepdims=True)
            oc = o_h - mu
            rs = lax.rsqrt(jnp.mean(oc * oc, axis=-1, keepdims=True) + GN_EPS)
            on = oc * rs
            sg = jax.nn.sigmoid(g)
            dg = dr_h * on * (sg * (1.0 + g * (1.0 - sg)))
            don = dr_h * (g * sg)
            do = rs * (don - jnp.mean(don, axis=-1, keepdims=True)
                       - on * jnp.mean(don * on, axis=-1, keepdims=True))
            do_b = do.astype(BF)
            dox_b = (do * xi_ref[h]).astype(BF)
            gst = gstate[h]
            gst_b = gst.astype(BF)
            kz = (k.astype(F32) * zeta_ref[h]).astype(BF)
            da_b = (_dot_nt(do_b, v) * d_ref[h]).astype(BF)
            dat_b = (_dot_nt(v, do_b) * dt_ref[h]).astype(BF)
            mt_b = (_dot_nt(k, q) * dt_ref[h]).astype(BF)
            dq = _dot(da_b, k) + _dot_nt(dox_b, s_ref[0, h])
            dk = _dot(dat_b, q) + _dot_nt(v, gst_b) * zeta_ref[h]
            dv = _dot(mt_b, do_b) + _dot(kz, gst_b)
            gstate[h] = decay_ref[h] * gst + _dot_tn(q, dox_b)
            dq = dq * cos_t - _rot(dq, even) * sin_t
            dk = (dk * cos_t - _rot(dk, even) * sin_t) * RET_K_SCALE
            dret_ref[:, h * C:(h + 1) * C] = dq.astype(BF)
            dret_ref[:, RET_W + h * C:RET_W + (h + 1) * C] = dk.astype(BF)
            dret_ref[:, 2 * RET_W + h * C:2 * RET_W + (h + 1) * C] = dv.astype(BF)
            dret_ref[:, 3 * RET_W + h * C:3 * RET_W + (h + 1) * C] = dg.astype(BF)
        pl.when(pl.program_id(0) == nc - 1)(ex.wait)

    rev = lambda w: pl.BlockSpec((BLOCK, w), lambda n: (nc - 1 - n, 0))
    tab = pl.BlockSpec((H, C, C), lambda n: (0, 0, 0))
    outs = _pcall(
        body, name="ret_bwd", grid=(nc,),
        in_specs=[_smem_full(), rev(RET_W), rev(RET_W), rev(RET_W), rev(RET_W), rev(RET_W),
                  pl.BlockSpec((1, H, C, C), lambda n: (nc - 1 - n, 0, 0, 0)), rev(RET_W), tab, tab, tab, tab,
                  rev(C), rev(C)] + [ANY_SPEC] * ns,
        out_specs=[rev(4 * RET_W)] + [ANY_SPEC] * ns,
        out_shape=[jax.ShapeDtypeStruct((T, 4 * RET_W), BF)] + _exchange_shapes([], swaps),
        scratch_shapes=[pltpu.VMEM((H, C, C), F32)] + _Exchange.scratch(ns),
        compiler_params=_params(("arbitrary",)),
    )(decay, qr, kr, vr, gr, o, states, dr, d_intra, d_intra_t, xi_b, zeta_b, cos, sin_s, *swaps)
    return outs[0], outs[1:]


def _attn_bwd_dq(sinks, qa, ka, va, da, lse):
    T = qa.shape[0]
    nb = T // BLOCK

    def body(sink_ref, q_ref, kc_ref, kp_ref, vc_ref, vp_ref, da_ref, lse_ref, dq_ref, deltat_ref, dsink_ref):
        n = pl.program_id(0)

        @pl.when(n == 0)
        def _():
            dsink_ref[...] = jnp.zeros_like(dsink_ref)

        kv = _kv_variants(jnp.concatenate([kp_ref[...], kc_ref[...]], axis=0))
        vv = _kv_variants(jnp.concatenate([vp_ref[...], vc_ref[...]], axis=0))
        i = lax.broadcasted_iota(jnp.int32, (BLOCK, 2 * BLOCK), 0)
        j = lax.broadcasted_iota(jnp.int32, (BLOCK, 2 * BLOCK), 1)
        valid = (j > i) & (j <= i + BLOCK) & ((n > 0) | (j >= BLOCK))
        lane = lax.broadcasted_iota(jnp.int32, (BLOCK, BLOCK), 1)
        row8 = lax.broadcasted_iota(jnp.int32, (N_ATTN_HEADS, BLOCK), 0)
        lse_tile = lse_ref[...]
        delta_all = jnp.zeros((BLOCK, BLOCK), F32)
        dsink = jnp.zeros((N_ATTN_HEADS, BLOCK), F32)
        for pair in range(N_ATTN_HEADS // 2):
            h = pair // 2
            qp = q_ref[:, pair * 128:(pair + 1) * 128]
            dop = da_ref[:, pair * 128:(pair + 1) * 128].astype(BF)
            acc = jnp.zeros((BLOCK, 128), F32)
            for e in range(2):
                head = 2 * pair + e
                lse_h = jnp.sum(jnp.where(lane == head, lse_tile, 0.0), axis=-1, keepdims=True)
                s = jnp.where(valid, _dot_nt(qp, kv[h][e]) * ATTN_SCALE, MASK_VALUE)
                p = jnp.exp(s - lse_h)
                dp = _dot_nt(dop, vv[h][e])
                delta = jnp.sum(p * dp, axis=-1, keepdims=True)
                ds = (p * (dp - delta) * ATTN_SCALE).astype(BF)
                acc = acc + _dot(ds, kv[h][e])
                delta_all = _col_to_tile(delta_all, delta, head)
                ps = jnp.exp(sink_ref[head] - lse_h)
                dsink = jnp.where(row8 == head, jnp.sum(-ps * delta, axis=0, keepdims=True), dsink)
            dq_ref[:, pair * 128:(pair + 1) * 128] = acc.astype(BF)
        deltat_ref[...] = delta_all.T[:N_ATTN_HEADS, :]
        dsink_ref[...] += dsink

    cur = lambda w: pl.BlockSpec((BLOCK, w), lambda n: (n, 0))
    prev = lambda w: pl.BlockSpec((BLOCK, w), lambda n: (jnp.maximum(n - 1, 0), 0))
    return _pcall(
        body, name="attn_bwd_dq", grid=(nb,),
        in_specs=[_smem_full(), cur(ATTN_W), cur(KV_W), prev(KV_W), cur(KV_W), prev(KV_W), cur(ATTN_W), cur(BLOCK)],
        out_specs=[cur(ATTN_W), pl.BlockSpec((N_ATTN_HEADS, BLOCK), lambda n: (0, n)),
                   _const((N_ATTN_HEADS, BLOCK))],
        out_shape=[jax.ShapeDtypeStruct((T, ATTN_W), BF), jax.ShapeDtypeStruct((N_ATTN_HEADS, T), F32),
                   jax.ShapeDtypeStruct((N_ATTN_HEADS, BLOCK), F32)],
        compiler_params=_params(("arbitrary",)),
    )(sinks, qa, ka, ka, va, va, da, lse)


def _attn_bwd_dkv(qa, ka, va, da, lse_t, delta_t):
    T = qa.shape[0]
    nb = T // BLOCK

    def body(qc_ref, qn_ref, dac_ref, dan_ref, k_ref, v_ref, lc_ref, ln_ref, dc_ref, dn_ref, dk_ref, dv_ref):
        m = pl.program_id(0)
        kv = _kv_variants(k_ref[...])
        vv = _kv_variants(v_ref[...])
        j = lax.broadcasted_iota(jnp.int32, (BLOCK, 2 * BLOCK), 0)
        c = lax.broadcasted_iota(jnp.int32, (BLOCK, 2 * BLOCK), 1)
        valid = ((c < BLOCK) & (j <= c)) | ((c >= BLOCK) & (j > c - BLOCK) & (m < nb - 1))
        lo = lax.broadcasted_iota(jnp.int32, (BLOCK, 128), 1) < HEAD_DIM
        dk = jnp.zeros((BLOCK, 128), F32)
        dv = jnp.zeros((BLOCK, 128), F32)
        for pair in range(N_ATTN_HEADS // 2):
            h = pair // 2
            ps = slice(pair * 128, (pair + 1) * 128)
            q2 = jnp.concatenate([qc_ref[:, ps], qn_ref[:, ps]], axis=0)
            do2 = jnp.concatenate([dac_ref[:, ps], dan_ref[:, ps]], axis=0).astype(BF)
            for e in range(2):
                head = 2 * pair + e
                lse = jnp.concatenate([lc_ref[pl.ds(head, 1), :], ln_ref[pl.ds(head, 1), :]], axis=1)
                delta = jnp.concatenate([dc_ref[pl.ds(head, 1), :], dn_ref[pl.ds(head, 1), :]], axis=1)
                st = jnp.where(valid, _dot_nt(kv[h][e], q2) * ATTN_SCALE, MASK_VALUE)
                pt = jnp.where(valid, jnp.exp(st - lse), 0.0)
                dpt = _dot_nt(vv[h][e], do2)
                dst = (pt * (dpt - delta) * ATTN_SCALE).astype(BF)
                half = lo if e == 0 else jnp.logical_not(lo)
                dv_e = jnp.where(half, _dot(pt.astype(BF), do2), 0.0)
                dk_e = jnp.where(half, _dot(dst, q2), 0.0)
                if e != h:
                    dv_e = pltpu.roll(dv_e, HEAD_DIM, 1)
                    dk_e = pltpu.roll(dk_e, HEAD_DIM, 1)
                dv = dv + dv_e
                dk = dk + dk_e
        dk_ref[...] = dk.astype(BF)
        dv_ref[...] = dv.astype(BF)

    cur = lambda w: pl.BlockSpec((BLOCK, w), lambda m: (m, 0))
    nxt = lambda w: pl.BlockSpec((BLOCK, w), lambda m: (jnp.minimum(m + 1, nb - 1), 0))
    tcur = pl.BlockSpec((N_ATTN_HEADS, BLOCK), lambda m: (0, m))
    tnxt = pl.BlockSpec((N_ATTN_HEADS, BLOCK), lambda m: (0, jnp.minimum(m + 1, nb - 1)))
    return _pcall(
        body, name="attn_bwd_dkv", grid=(nb,),
        in_specs=[cur(ATTN_W), nxt(ATTN_W), cur(ATTN_W), nxt(ATTN_W), cur(KV_W), cur(KV_W), tcur, tnxt, tcur, tnxt],
        out_specs=[cur(KV_W), cur(KV_W)],
        out_shape=[jax.ShapeDtypeStruct((T, KV_W), BF), jax.ShapeDtypeStruct((T, KV_W), BF)],
        compiler_params=_params(("parallel",)),
    )(qa, qa, da, da, ka, va, lse_t, lse_t, delta_t, delta_t)


def _in_proj_bwd(dqa, dka, dva, dret, w_in, x, g1, dx1, tm):
    T = x.shape[0]

    def body(dqa_ref, dka_ref, dva_ref, dret_ref, w_ref, x_ref, g_ref, dx1_ref, dx_ref, dg1_ref):
        @pl.when(pl.program_id(0) == 0)
        def _():
            dg1_ref[...] = jnp.zeros_like(dg1_ref)

        dh = (_dot_nt(dqa_ref[...], w_ref[:, QA0:QA0 + ATTN_W]) + _dot_nt(dka_ref[...], w_ref[:, KA0:KA0 + KV_W])
              + _dot_nt(dva_ref[...], w_ref[:, VA0:VA0 + KV_W]) + _dot_nt(dret_ref[...], w_ref[:, QR0:IN_W]))
        r, n = _rms_stats(x_ref[...])
        dg1_ref[...] += jnp.sum(dh * n, axis=0, keepdims=True)
        dx_ref[...] = dx1_ref[...] + _rms_bwd(n, r, dh * g_ref[...])

    return _pcall(
        body, name="in_proj_bwd", grid=(T // tm,),
        in_specs=[_rows(tm, ATTN_W), _rows(tm, KV_W), _rows(tm, KV_W), _rows(tm, 4 * RET_W), _vmem_full(),
                  _rows(tm, D_MODEL), _const((1, D_MODEL)), _rows(tm, D_MODEL)],
        out_specs=[_rows(tm, D_MODEL), _const((1, D_MODEL))],
        out_shape=[jax.ShapeDtypeStruct((T, D_MODEL), F32), jax.ShapeDtypeStruct((1, D_MODEL), F32)],
        compiler_params=_params(("arbitrary",)),
    )(dqa, dka, dva, dret, w_in, x, g1, dx1)


def _wgrad(a_list, b_list, tk, name):
    T = a_list[0].shape[0]
    na, nbb = len(a_list), len(b_list)
    m_sizes = [a.shape[1] for a in a_list]
    n_sizes = [b.shape[1] for b in b_list]
    M, N = sum(m_sizes), sum(n_sizes)
    nk = T // tk
    chunk = 512

    def body(*refs):
        a_refs, b_refs = refs[:na], refs[na:na + nbb]
        out_ref, acc = refs[na + nbb], refs[na + nbb + 1]
        k = pl.program_id(0)

        @pl.when(k == 0)
        def _():
            acc[...] = jnp.zeros_like(acc)

        r0 = 0
        for ai in range(na):
            a = a_refs[ai][...]
            c0 = 0
            for bi in range(nbb):
                for s in range(0, n_sizes[bi], chunk):
                    w = min(chunk, n_sizes[bi] - s)
                    acc[r0:r0 + m_sizes[ai], c0 + s:c0 + s + w] += _dot_tn(a, b_refs[bi][:, s:s + w])
                c0 += n_sizes[bi]
            r0 += m_sizes[ai]

        @pl.when(k == nk - 1)
        def _():
            pltpu.sync_copy(acc, out_ref)

    return _pcall(
        body, name=name, grid=(nk,),
        in_specs=[_rows(tk, w) for w in m_sizes + n_sizes],
        out_specs=pl.BlockSpec(memory_space=pl.ANY),
        out_shape=jax.ShapeDtypeStruct((M, N), F32),
        scratch_shapes=[pltpu.VMEM((M, N), F32)],
        compiler_params=_params(("arbitrary",)),
    )(*a_list, *b_list)


def _adamw_math(w, g, m, v):
    m = ADAM_B1 * m + (1.0 - ADAM_B1) * g
    v = ADAM_B2 * v + (1.0 - ADAM_B2) * (g * g)
    m_hat = m / (1.0 - ADAM_B1 ** ADAM_STEP)
    v_hat = v / (1.0 - ADAM_B2 ** ADAM_STEP)
    delta = -ADAM_LR * (m_hat / (jnp.sqrt(v_hat) + ADAM_EPS) + ADAM_WD * w)
    return delta, m, v


def _sum_parts(parts_ref):
    g = parts_ref[0]
    for i in range(1, N_DEV):
        g = g + parts_ref[i]
    return g


def _adamw_shard(parts, w, m, v, tr, name):
    R, C = w.shape

    def body(p_ref, w_ref, m_ref, v_ref, g_ref, d_ref, nm_ref, nv_ref):
        g = _sum_parts(p_ref)
        g_ref[...] = g
        d_ref[...], nm_ref[...], nv_ref[...] = _adamw_math(w_ref[...], g, m_ref[...], v_ref[...])

    blk = pl.BlockSpec((tr, C), lambda i: (i, 0))
    return _pcall(
        body, name=name, grid=(R // tr,),
        in_specs=[pl.BlockSpec((N_DEV, tr, C), lambda i: (0, i, 0)), blk, blk, blk],
        out_specs=[blk] * 4,
        out_shape=[jax.ShapeDtypeStruct((R, C), F32)] * 4,
        compiler_params=_params(("parallel",)),
    )(parts, w, m, v)


def _sum_small(parts):
    def body(p_ref, g_ref):
        g_ref[...] = _sum_parts(p_ref)

    return _pcall(body, name="sum_small", out_shape=jax.ShapeDtypeStruct(parts.shape[1:], F32),
                  in_specs=[_vmem_full()], out_specs=_vmem_full())(parts)


def _adamw_small(g, w, m, v, name):
    def body(g_ref, w_ref, m_ref, v_ref, d_ref, nm_ref, nv_ref):
        d_ref[...], nm_ref[...], nv_ref[...] = _adamw_math(w_ref[...], g_ref[...], m_ref[...], v_ref[...])

    return _pcall(body, name=name, out_shape=[jax.ShapeDtypeStruct(w.shape, F32)] * 3,
                  in_specs=[_vmem_full()] * 4, out_specs=[_vmem_full()] * 3)(g, w, m, v)


def _tables(T):
    h, c = N_RET_HEADS, BLOCK
    pos = jnp.arange(T, dtype=F32)
    angle = 1.0 / jnp.power(10000.0, jnp.linspace(0.0, 1.0, RET_HEAD_DIM // 2, dtype=F32))
    angle = jnp.repeat(angle, 2)
    sin = jnp.sin(pos[:, None] * angle[None])
    cos = jnp.cos(pos[:, None] * angle[None])
    even = (jnp.arange(RET_HEAD_DIM) % 2 == 0)[None, :]
    sin_s = jnp.where(even, -sin, sin)
    log_gamma = jnp.log(1.0 - jnp.power(2.0, -5.0 - jnp.arange(h, dtype=F32)))
    idx = jnp.arange(c, dtype=F32)
    rel = idx[:, None] - idx[None, :]
    d_intra = jnp.where(rel[None] >= 0, jnp.exp(log_gamma[:, None, None] * jnp.maximum(rel, 0.0)[None]), 0.0)
    xi = jnp.exp(log_gamma[None, :] * (idx[:, None] + 1.0))
    zeta = jnp.exp(log_gamma[None, :] * (c - 1.0 - idx[:, None]))
    decay = jnp.exp(log_gamma * c)
    xi_b = jnp.broadcast_to(xi.T[:, :, None], (h, c, RET_HEAD_DIM))
    zeta_b = jnp.broadcast_to(zeta.T[:, :, None], (h, c, RET_HEAD_DIM))
    return cos, sin_s, d_intra, jnp.swapaxes(d_intra, 1, 2), xi_b, zeta_b, decay


def _to_shards(full, cols):
    r = full.shape[0]
    return jnp.swapaxes(full.reshape(r, N_DEV, cols), 0, 1)


def _from_shards(sh):
    n, r, cols = sh.shape
    return jnp.swapaxes(sh, 0, 1).reshape(r, n * cols)


SMALL_ROWS = 216


def _pack_small(gains, conv_b, conv_w, sinks):
    parts = [g.reshape(8, 128) for g in gains] + [conv_b.reshape(44, 128), conv_w.reshape(132, 128),
                                                  jnp.pad(sinks.reshape(1, 8), ((0, 0), (0, 120)))]
    packed = jnp.concatenate(parts, axis=0)
    return jnp.pad(packed, ((0, SMALL_ROWS - packed.shape[0]), (0, 0)))


def kernel(x, mix_pre_norm, w_in, attn_sinks, w_out, mix_post_norm, ffn_pre_norm, w_up, conv_w, conv_b, w_down, ffn_post_norm, loss_target, m_mix_pre_norm, m_w_in, m_attn_sinks, m_w_out, m_mix_post_norm, m_ffn_pre_norm, m_w_up, m_conv_w, m_conv_b, m_w_down, m_ffn_post_norm, v_mix_pre_norm, v_w_in, v_attn_sinks, v_w_out, v_mix_post_norm, v_ffn_pre_norm, v_w_up, v_conv_w, v_conv_b, v_w_down, v_ffn_post_norm):
    T = x.shape[1]
    tm = min(512, T)
    tm_ff = min(256, T)
    x2 = x.reshape(T, D_MODEL)
    target = loss_target.reshape(T, D_MODEL)
    me = 4 * lax.axis_index("x") + 2 * lax.axis_index("y") + lax.axis_index("c")

    g_in, g_cw = _exchange_call([w_in[0].astype(BF), conv_w[0]], [], "gather_w_in")
    w_in_f = _from_shards(g_in)
    conv_w_f = _from_shards(g_cw)
    cos, sin_s, d_intra, d_intra_t, xi_b, zeta_b, decay = _tables(T)
    sinks = attn_sinks.reshape(N_ATTN_HEADS)

    (h1, qa, ka, va, qr, kr, vr, gr), (g_up, g_down, g_out) = _in_proj(
        x2, mix_pre_norm, w_in_f, cos, sin_s, tm, [w_up[0].astype(BF), w_down[0].astype(BF), w_out[0].astype(BF)])
    w_up_f = _from_shards(g_up)
    w_out_f = g_out.reshape(D_MODEL, D_MODEL)
    w_down_f = g_down.reshape(D_FF, D_MODEL)
    a, lse, lse_t = _attn_fwd(sinks, qa, ka, va)
    o, states, r = _ret_fwd(decay, qr, kr, vr, gr, d_intra, xi_b, zeta_b)
    mixed, x1, h2 = _out_proj(a, r, w_out_f, x2, mix_post_norm, ffn_pre_norm, tm)
    up_b, u, y = _ffn_up(h2, w_up_f, conv_w_f, conv_b, tm_ff)
    z, dout, loss_acc = _ffn_down(y, w_down_f, x1, ffn_post_norm, target, tm)
    loss = lax.psum(loss_acc[0, 0], ("x", "y", "c"))

    dz, dup, dg4, dcb, dcw = _ffn_bwd_a(z, dout, ffn_post_norm, w_down_f, u, up_b, conv_w_f, tm_ff)
    gw_down = _wgrad([y], [dz], tm, "wgrad_down")
    gw_up = _wgrad([h2], [dup], tm, "wgrad_up")
    (dx1, dmixed, da, dr, dg3, dg2), (p_up,) = _ffn_bwd_b(
        dup, w_up_f, x1, dout, ffn_pre_norm, mixed, mix_post_norm, w_out_f, tm_ff,
        [_to_shards(gw_up, 2 * D_FF // N_DEV)])
    dret, (p_down,) = _ret_bwd(decay, qr, kr, vr, gr, o, states, dr, d_intra, d_intra_t, xi_b, zeta_b, cos, sin_s,
                               [gw_down.reshape(N_DEV, D_FF // N_DEV, D_MODEL)])
    dqa, delta_t, dsink = _attn_bwd_dq(sinks, qa, ka, va, da, lse)
    dka, dva = _attn_bwd_dkv(qa, ka, va, da, lse_t, delta_t)
    grad_x, dg1 = _in_proj_bwd(dqa, dka, dva, dret, w_in_f, x2, mix_pre_norm, dx1, tm)
    gw_out = _wgrad([a, r], [dmixed], tm, "wgrad_out")
    gw_in = _wgrad([h1], [dqa, dka, dva, dret], tm, "wgrad_in")

    small = _pack_small([dg1, dg2, dg3, dg4], dcb, dcw, dsink[:, 0])
    small_all, p_in, p_out = _exchange_call(
        [small], [_to_shards(gw_in, IN_W // N_DEV), gw_out.reshape(N_DEV, D_MODEL // N_DEV, D_MODEL)],
        "exchange_last")
    g_small = _sum_small(small_all)

    g_w_in, d_w_in, nm_w_in, nv_w_in = _adamw_shard(p_in, w_in[0], m_w_in[0], v_w_in[0], 256, "adamw_in")
    g_w_up, d_w_up, nm_w_up, nv_w_up = _adamw_shard(p_up, w_up[0], m_w_up[0], v_w_up[0], 256, "adamw_up")
    g_w_out, d_w_out, nm_w_out, nv_w_out = _adamw_shard(p_out, w_out[0], m_w_out[0], v_w_out[0], 128, "adamw_out")
    g_w_down, d_w_down, nm_w_down, nv_w_down = _adamw_shard(p_down, w_down[0], m_w_down[0], v_w_down[0], 176,
                                                            "adamw_down")
    gains = [mix_pre_norm, mix_post_norm, ffn_pre_norm, ffn_post_norm]
    m_gains = [m_mix_pre_norm, m_mix_post_norm, m_ffn_pre_norm, m_ffn_post_norm]
    v_gains = [v_mix_pre_norm, v_mix_post_norm, v_ffn_pre_norm, v_ffn_post_norm]
    zeros_cw = jnp.zeros((3, 2 * D_FF), F32)
    w_small = _pack_small(gains, conv_b, zeros_cw, attn_sinks)
    m_small = _pack_small(m_gains, m_conv_b, zeros_cw, m_attn_sinks)
    v_small = _pack_small(v_gains, v_conv_b, zeros_cw, v_attn_sinks)
    d_small, nm_small, nv_small = _adamw_small(g_small, w_small, m_small, v_small, "adamw_small")
    shard_cols = 2 * D_FF // N_DEV
    g_cw = lax.dynamic_slice(g_small[76:208].reshape(3, 2 * D_FF), (0, me * shard_cols), (3, shard_cols))
    d_cw, nm_cw, nv_cw = _adamw_small(g_cw, conv_w[0], m_conv_w[0], v_conv_w[0], "adamw_conv_w")

    def unpack(p):
        gains_o = [p[8 * i:8 * i + 8].reshape(1, D_MODEL) for i in range(4)]
        return gains_o, p[32:76].reshape(1, 2 * D_FF), p[208:209, :N_ATTN_HEADS]

    def leaves(p, w_in_s, w_out_s, w_up_s, cw_s, w_down_s):
        (pre1, post1, pre2, post2), cb, sk = unpack(p)
        return [pre1, w_in_s[None], sk, w_out_s[None], post1, pre2, w_up_s[None], cw_s[None], cb, w_down_s[None],
                post2]

    return (loss, grad_x.reshape(1, T, D_MODEL),
            *leaves(g_small, g_w_in, g_w_out, g_w_up, g_cw, g_w_down),
            *leaves(d_small, d_w_in, d_w_out, d_w_up, d_cw, d_w_down),
            *leaves(nm_small, nm_w_in, nm_w_out, nm_w_up, nm_cw, nm_w_down),
            *leaves(nv_small, nv_w_in, nv_w_out, nv_w_up, nv_cw, nv_w_down))
```

```python
import functools
import math

import jax
import jax.numpy as jnp
from jax import lax
from jax.experimental import pallas as pl
from jax.experimental.pallas import tpu as pltpu

F32 = jnp.float32
BF = jnp.bfloat16

N_DEV = 8
D_MODEL = 1024
HEAD_DIM = 64
ATTN_W = 512
N_ATTN_HEADS = 8
KV_W = 128
BLOCK = 128
RET_W = 512
N_RET_HEADS = 4
RET_HEAD_DIM = 128
IN_W = 2816
D_FF = 2816
RMS_EPS = 1e-6
GN_EPS = 1e-6
MASK_VALUE = -1e30
ATTN_SCALE = HEAD_DIM ** -0.5
RET_K_SCALE = RET_HEAD_DIM ** -0.5
GELU_C = math.sqrt(2.0 / math.pi)
GELU_A = 0.044715

ADAM_LR = 0.001
ADAM_B1 = 0.9
ADAM_B2 = 0.999
ADAM_EPS = 1e-08
ADAM_WD = 0.01
ADAM_STEP = 10

VMEM_LIMIT_BYTES = 56 * 1024 * 1024
FF_CHUNK = 256
N_FF_CHUNKS = D_FF // FF_CHUNK

QA0, KA0, VA0, QR0, KR0, VR0, GR0 = 0, 512, 640, 768, 1280, 1792, 2304

MESH_ID = pl.DeviceIdType.MESH


def _pcall(body, **kw):
    return pl.pallas_call(body, **kw)


def _params(sem=None):
    return pltpu.CompilerParams(dimension_semantics=sem, vmem_limit_bytes=VMEM_LIMIT_BYTES)


def _dot(a, b):
    return jnp.dot(a, b, preferred_element_type=F32)


def _dot_nt(a, b):
    return lax.dot_general(a, b, (((1,), (1,)), ((), ())), preferred_element_type=F32)


def _dot_tn(a, b):
    return lax.dot_general(a, b, (((0,), (0,)), ((), ())), preferred_element_type=F32)


def _vmem_full():
    return pl.BlockSpec(memory_space=pltpu.VMEM)


def _smem_full():
    return pl.BlockSpec(memory_space=pltpu.SMEM)


def _rows(tm, w):
    return pl.BlockSpec((tm, w), lambda i: (i, 0))


def _const(shape):
    return pl.BlockSpec(shape, lambda i: tuple(0 for _ in shape))


def _rms_stats(x):
    r = lax.rsqrt(jnp.mean(x * x, axis=-1, keepdims=True) + RMS_EPS)
    return r, x * r


def _rms_bwd(n, r, dn):
    return r * (dn - n * jnp.mean(dn * n, axis=-1, keepdims=True))


def _rot(x, even):
    w = x.shape[1]
    return jnp.where(even, pltpu.roll(x, w - 1, 1), pltpu.roll(x, 1, 1))


def _peers():
    x, y, c = lax.axis_index("x"), lax.axis_index("y"), lax.axis_index("c")
    flips = [(0, 0, 1), (1, 0, 0), (0, 1, 0), (1, 1, 0), (1, 0, 1), (0, 1, 1), (1, 1, 1)]
    peers = [(x ^ fx, y ^ fy, c ^ fc) for fx, fy, fc in flips]
    return 4 * x + 2 * y + c, peers


class _Exchange:
    def __init__(self, gathers, swaps, send_sems, recv_sems, local_sems):
        self.me, self.peers = _peers()
        self.slots = [4 * px + 2 * py + pc for px, py, pc in self.peers]
        self.pairs = [(src, dst, True) for src, dst in gathers] + [(src, dst, False) for src, dst in swaps]
        self.send_sems, self.recv_sems, self.local_sems = send_sems, recv_sems, local_sems

    @staticmethod
    def scratch(n):
        return [pltpu.SemaphoreType.DMA((n, N_DEV - 1)), pltpu.SemaphoreType.DMA((n, N_DEV - 1)),
                pltpu.SemaphoreType.DMA((n,))]

    def _local(self, a):
        src, dst, whole = self.pairs[a]
        return pltpu.make_async_copy(src if whole else src.at[self.me], dst.at[self.me], self.local_sems.at[a])

    def _remote(self, a, k, outgoing):
        src, dst, whole = self.pairs[a]
        if outgoing:
            s, d = (src if whole else src.at[self.slots[k]]), dst.at[self.me]
        else:
            s, d = (src if whole else src.at[self.me]), dst.at[self.slots[k]]
        return pltpu.make_async_remote_copy(
            src_ref=s, dst_ref=d, send_sem=self.send_sems.at[a, k], recv_sem=self.recv_sems.at[a, k],
            device_id=self.peers[k], device_id_type=MESH_ID)

    def start(self):
        for a in range(len(self.pairs)):
            self._local(a).start()
            for k in range(N_DEV - 1):
                self._remote(a, k, True).start()

    def wait(self):
        for a in range(len(self.pairs)):
            for k in range(N_DEV - 1):
                self._remote(a, k, False).wait_recv()
        for a in range(len(self.pairs)):
            for k in range(N_DEV - 1):
                self._remote(a, k, True).wait_send()
            self._local(a).wait()


ANY_SPEC = pl.BlockSpec(memory_space=pl.ANY)


def _exchange_shapes(gathers, swaps):
    return ([jax.ShapeDtypeStruct((N_DEV,) + a.shape, a.dtype) for a in gathers]
            + [jax.ShapeDtypeStruct(a.shape, a.dtype) for a in swaps])


def _exchange_of(ins, outs, sems, ng):
    return _Exchange(list(zip(ins[:ng], outs[:ng])), list(zip(ins[ng:], outs[ng:])), *sems)


def _exchange_call(gathers, swaps, name):
    ng, ns = len(gathers), len(swaps)
    n = ng + ns

    def body(*refs):
        ex = _exchange_of(refs[:n], refs[n:2 * n], refs[2 * n:], ng)
        ex.start()
        ex.wait()

    return _pcall(
        body, name=name, out_shape=_exchange_shapes(gathers, swaps),
        in_specs=[ANY_SPEC] * (ng + ns), out_specs=[ANY_SPEC] * (ng + ns),
        scratch_shapes=_Exchange.scratch(ng + ns),
    )(*gathers, *swaps)


def _in_proj(x, g1, w_in, cos, sin_s, tm, gathers):
    T = x.shape[0]
    ng = len(gathers)
    nt = T // tm

    def body(x_ref, g_ref, w_ref, cos_ref, sin_ref, *rest):
        ex = _exchange_of(rest[:ng], rest[ng + 8:2 * ng + 8], rest[2 * ng + 8:], ng)
        h_ref, qa_ref, ka_ref, va_ref, qr_ref, kr_ref, vr_ref, gr_ref = rest[ng:ng + 8]
        pl.when(pl.program_id(0) == 0)(ex.start)
        r, n = _rms_stats(x_ref[...])
        h = (n * g_ref[...]).astype(BF)
        h_ref[...] = h

        def proj(c0, w):
            return _dot(h, w_ref[:, c0:c0 + w])

        qa_ref[...] = proj(QA0, ATTN_W).astype(BF)
        ka_ref[...] = proj(KA0, KV_W).astype(BF)
        va_ref[...] = proj(VA0, KV_W).astype(BF)
        vr_ref[...] = proj(VR0, RET_W).astype(BF)
        gr_ref[...] = proj(GR0, RET_W)
        cos_t, sin_t = cos_ref[...], sin_ref[...]
        even = lax.broadcasted_iota(jnp.int32, (tm, RET_HEAD_DIM), 1) % 2 == 0
        for hd in range(N_RET_HEADS):
            c = hd * RET_HEAD_DIM
            q = proj(QR0 + c, RET_HEAD_DIM)
            k = proj(KR0 + c, RET_HEAD_DIM) * RET_K_SCALE
            qr_ref[:, c:c + RET_HEAD_DIM] = (q * cos_t + _rot(q, even) * sin_t).astype(BF)
            kr_ref[:, c:c + RET_HEAD_DIM] = (k * cos_t + _rot(k, even) * sin_t).astype(BF)
        pl.when(pl.program_id(0) == nt - 1)(ex.wait)

    widths = [D_MODEL, ATTN_W, KV_W, KV_W, RET_W, RET_W, RET_W, RET_W]
    dts = [BF] * 7 + [F32]
    outs = _pcall(
        body, name="in_proj", grid=(nt,),
        in_specs=[_rows(tm, D_MODEL), _const((1, D_MODEL)), _vmem_full(), _rows(tm, RET_HEAD_DIM),
                  _rows(tm, RET_HEAD_DIM)] + [ANY_SPEC] * ng,
        out_specs=[_rows(tm, w) for w in widths] + [ANY_SPEC] * ng,
        out_shape=[jax.ShapeDtypeStruct((T, w), dt) for w, dt in zip(widths, dts)] + _exchange_shapes(gathers, []),
        scratch_shapes=_Exchange.scratch(ng),
        compiler_params=_params(("arbitrary",)),
    )(x, g1, w_in, cos, sin_s, *gathers)
    return outs[:8], outs[8:]


def _kv_variants(kk):
    kf = kk.astype(F32)
    lo = lax.broadcasted_iota(jnp.int32, kf.shape, 1) < HEAD_DIM
    h0_lo = jnp.where(lo, kf, 0.0)
    h1_hi = jnp.where(lo, 0.0, kf)
    h0_hi = pltpu.roll(h0_lo, HEAD_DIM, 1)
    h1_lo = pltpu.roll(h1_hi, HEAD_DIM, 1)
    return [[h0_lo.astype(BF), h0_hi.astype(BF)], [h1_lo.astype(BF), h1_hi.astype(BF)]]


def _col_to_tile(tile, col, head):
    lane = lax.broadcasted_iota(jnp.int32, tile.shape, 1)
    return jnp.where(lane == head, col, tile)


def _tri(rows, key_major=False):
    i = lax.broadcasted_iota(jnp.int32, (rows, BLOCK), 0) & (BLOCK - 1)
    j = lax.broadcasted_iota(jnp.int32, (rows, BLOCK), 1)
    return i > j if key_major else j > i


def _fold(x2, tri, first_above):
    a, b = x2[:, :BLOCK], x2[:, BLOCK:]
    return jnp.where(tri, a, b) if first_above else jnp.where(tri, b, a)


def _unfold(x, tri, first_above):
    up, low = jnp.where(tri, x, 0.0), jnp.where(tri, 0.0, x)
    return jnp.concatenate([up, low] if first_above else [low, up], axis=1).astype(BF)


def _scaled(q):
    return (q.astype(F32) * ATTN_SCALE).astype(BF)


def _cat_variants(prev, cur):
    return [[jnp.concatenate([prev[h][e], cur[h][e]], axis=0) for e in range(2)] for h in range(2)]


def _block_variants(prev_ref, cur_ref, nbs):
    var = [_kv_variants(prev_ref[...])] + [_kv_variants(cur_ref[b * BLOCK:(b + 1) * BLOCK, :]) for b in range(nbs)]
    return [_cat_variants(var[b], var[b + 1]) for b in range(nbs)]


def _head_cols(col, nbs):
    tiles = []
    for b in range(nbs):
        t = jnp.zeros((BLOCK, BLOCK), F32)
        for head in range(N_ATTN_HEADS):
            r0 = (b * N_ATTN_HEADS + head) * BLOCK
            t = _col_to_tile(t, col[r0:r0 + BLOCK, :], head)
        tiles.append(t)
    return tiles


def _attn_fwd(sinks, qa, ka, va, nbs, gathers):
    T = qa.shape[0]
    steps = T // (BLOCK * nbs)
    R = nbs * N_ATTN_HEADS * BLOCK
    ng = len(gathers)

    def body(sink_ref, q_ref, kc_ref, kp_ref, vc_ref, vp_ref, *rest):
        ex = _exchange_of(rest[:ng], rest[ng + 3:2 * ng + 3], rest[2 * ng + 3:], ng)
        a_ref, lse_ref, lset_ref = rest[ng:ng + 3]
        n = pl.program_id(0)
        pl.when(n == 0)(ex.start)
        kcat = _block_variants(kp_ref, kc_ref, nbs)
        vcat = _block_variants(vp_ref, vc_ref, nbs)
        tri1 = _tri(BLOCK)
        tiles = []
        for b in range(nbs):
            for pair in range(N_ATTN_HEADS // 2):
                qp = _scaled(q_ref[b * BLOCK:(b + 1) * BLOCK, pair * 128:(pair + 1) * 128])
                for e in range(2):
                    s = _fold(_dot_nt(qp, kcat[b][pair // 2][e]), tri1, True)
                    if b == 0:
                        s = jnp.where(tri1 & (n == 0), MASK_VALUE, s)
                    tiles.append(s)
        s = jnp.concatenate(tiles, axis=0)
        sink = jnp.concatenate([jnp.full((BLOCK, 1), sink_ref[head], F32)
                                for _ in range(nbs) for head in range(N_ATTN_HEADS)], axis=0)
        m = jnp.maximum(jnp.max(s, axis=-1, keepdims=True), sink)
        p = jnp.exp(s - m)
        z = jnp.sum(p, axis=-1, keepdims=True) + jnp.exp(sink - m)
        p2 = _unfold(p * (1.0 / z), _tri(R), True)
        for b in range(nbs):
            for pair in range(N_ATTN_HEADS // 2):
                r0 = (b * N_ATTN_HEADS + 2 * pair) * BLOCK
                acc = (_dot(p2[r0:r0 + BLOCK, :], vcat[b][pair // 2][0])
                       + _dot(p2[r0 + BLOCK:r0 + 2 * BLOCK, :], vcat[b][pair // 2][1]))
                a_ref[b * BLOCK:(b + 1) * BLOCK, pair * 128:(pair + 1) * 128] = acc.astype(BF)
        for b, t in enumerate(_head_cols(m + jnp.log(z), nbs)):
            lse_ref[b * BLOCK:(b + 1) * BLOCK, :] = t
            lset_ref[:, b * BLOCK:(b + 1) * BLOCK] = t.T[:N_ATTN_HEADS, :]
        pl.when(n == steps - 1)(ex.wait)

    cur = lambda w: pl.BlockSpec((BLOCK * nbs, w), lambda n: (n, 0))
    prev = lambda w: pl.BlockSpec((BLOCK, w), lambda n: (jnp.maximum(n * nbs - 1, 0), 0))
    outs = _pcall(
        body, name="attn_fwd", grid=(steps,),
        in_specs=[_smem_full(), cur(ATTN_W), cur(KV_W), prev(KV_W), cur(KV_W), prev(KV_W)] + [ANY_SPEC] * ng,
        out_specs=[cur(ATTN_W), cur(BLOCK), pl.BlockSpec((N_ATTN_HEADS, BLOCK * nbs), lambda n: (0, n))]
        + [ANY_SPEC] * ng,
        out_shape=[jax.ShapeDtypeStruct((T, ATTN_W), BF), jax.ShapeDtypeStruct((T, BLOCK), F32),
                   jax.ShapeDtypeStruct((N_ATTN_HEADS, T), F32)] + _exchange_shapes(gathers, []),
        scratch_shapes=_Exchange.scratch(ng),
        compiler_params=_params(("arbitrary",)),
    )(sinks, qa, ka, ka, va, va, *gathers)
    return outs[:3], outs[3:]


def _ret_fwd(decay, qr, kr, vr, gr, d_intra, xi_b, zeta_b, ncs):
    T = qr.shape[0]
    nc = T // BLOCK
    H, C = N_RET_HEADS, RET_HEAD_DIM

    def body(decay_ref, q_ref, k_ref, v_ref, g_ref, d_ref, xi_ref, zeta_ref, o_ref, s_ref, r_ref, state):
        @pl.when(pl.program_id(0) == 0)
        def _():
            state[...] = jnp.zeros_like(state)

        for h in range(H):
            cs = slice(h * C, (h + 1) * C)
            st = state[h]
            for b in range(ncs):
                rows = slice(b * BLOCK, (b + 1) * BLOCK)
                q, k, v = q_ref[rows, cs], k_ref[rows, cs], v_ref[rows, cs]
                st_b = st.astype(BF)
                s_ref[b, h] = st_b
                inner = (_dot_nt(q, k) * d_ref[h]).astype(BF)
                o = _dot(inner, v) + _dot(q, st_b) * xi_ref[h]
                kz = (k.astype(F32) * zeta_ref[h]).astype(BF)
                st = decay_ref[h] * st + _dot_tn(kz, v)
                o_ref[rows, cs] = o
                mu = jnp.mean(o, axis=-1, keepdims=True)
                oc = o - mu
                rs = lax.rsqrt(jnp.mean(oc * oc, axis=-1, keepdims=True) + GN_EPS)
                g = g_ref[rows, cs]
                r_ref[rows, cs] = (g * jax.nn.sigmoid(g) * (oc * rs)).astype(BF)
            state[h] = st

    cur = pl.BlockSpec((BLOCK * ncs, RET_W), lambda n: (n, 0))
    tab = pl.BlockSpec((H, C, C), lambda n: (0, 0, 0))
    return _pcall(
        body, name="ret_fwd", grid=(nc // ncs,),
        in_specs=[_smem_full(), cur, cur, cur, cur, tab, tab, tab],
        out_specs=[cur, pl.BlockSpec((ncs, H, C, C), lambda n: (n, 0, 0, 0)), cur],
        out_shape=[jax.ShapeDtypeStruct((T, RET_W), F32), jax.ShapeDtypeStruct((nc, H, C, C), BF),
                   jax.ShapeDtypeStruct((T, RET_W), BF)],
        scratch_shapes=[pltpu.VMEM((H, C, C), F32)],
        compiler_params=_params(("arbitrary",)),
    )(decay, qr, kr, vr, gr, d_intra, xi_b, zeta_b)


def _out_proj(a, r, w_out, x, g2, g3, tm):
    T = x.shape[0]

    def body(a_ref, r_ref, w_ref, x_ref, g2_ref, g3_ref, mixed_ref, x1_ref, h2_ref):
        mixed = _dot(a_ref[...], w_ref[:ATTN_W, :]) + _dot(r_ref[...], w_ref[ATTN_W:, :])
        mixed_ref[...] = mixed
        _, n2 = _rms_stats(mixed)
        x1 = x_ref[...] + n2 * g2_ref[...]
        x1_ref[...] = x1
        _, n3 = _rms_stats(x1)
        h2_ref[...] = (n3 * g3_ref[...]).astype(BF)

    return _pcall(
        body, name="out_proj", grid=(T // tm,),
        in_specs=[_rows(tm, ATTN_W), _rows(tm, RET_W), _vmem_full(), _rows(tm, D_MODEL), _const((1, D_MODEL)),
                  _const((1, D_MODEL))],
        out_specs=[_rows(tm, D_MODEL)] * 3,
        out_shape=[jax.ShapeDtypeStruct((T, D_MODEL), F32), jax.ShapeDtypeStruct((T, D_MODEL), F32),
                   jax.ShapeDtypeStruct((T, D_MODEL), BF)],
        compiler_params=_params(("parallel",)),
    )(a, r, w_out, x, g2, g3)


def _shift_down(cur, k, before):
    out = pltpu.roll(cur, k, 0)
    row = lax.broadcasted_iota(jnp.int32, before.shape, 0)
    top = jnp.where(row < k, pltpu.roll(before, k, 0), out[0:8])
    return jnp.concatenate([top, out[8:]], axis=0)


def _shift_up(cur, k, after):
    tm = cur.shape[0]
    out = pltpu.roll(cur, tm - k, 0)
    row = lax.broadcasted_iota(jnp.int32, after.shape, 0)
    bot = jnp.where(row >= 8 - k, pltpu.roll(after, 8 - k, 0), out[tm - 8:])
    return jnp.concatenate([out[:tm - 8], bot], axis=0)


def _gelu_parts(x):
    x2 = x * x
    th = jnp.tanh(GELU_C * (x + GELU_A * x * x2))
    gelu = 0.5 * x * (1.0 + th)
    dgelu = 0.5 * (1.0 + th) + 0.5 * x * (1.0 - th * th) * (GELU_C * (1.0 + 3.0 * GELU_A * x2))
    return gelu, dgelu


def _ffn_up(h2, w_up, conv_w, conv_b, tm):
    T = h2.shape[0]

    def body(h_ref, w_ref, cw_ref, cb_ref, upb_ref, u_ref, y_ref, halo):
        first = pl.program_id(0) == 0
        h = h_ref[...]
        for c in range(N_FF_CHUNKS):
            u = []
            for part in range(2):
                cols = slice(part * D_FF + c * FF_CHUNK, part * D_FF + (c + 1) * FF_CHUNK)
                cur = _dot(h, w_ref[:, cols])
                upb_ref[:, cols] = cur.astype(BF)
                before = jnp.where(first, 0.0, halo[:, cols])
                halo[:, cols] = cur[tm - 8:tm, :]
                u_c = (cw_ref[pl.ds(0, 1), cols] * _shift_down(cur, 2, before)
                       + cw_ref[pl.ds(1, 1), cols] * _shift_down(cur, 1, before)
                       + cw_ref[pl.ds(2, 1), cols] * cur + cb_ref[:, cols])
                u_ref[:, cols] = u_c
                u.append(u_c)
            gelu, _ = _gelu_parts(u[0])
            y_ref[:, c * FF_CHUNK:(c + 1) * FF_CHUNK] = (gelu * u[1]).astype(BF)

    return _pcall(
        body, name="ffn_up", grid=(T // tm,),
        in_specs=[_rows(tm, D_MODEL), _vmem_full(), _const((3, 2 * D_FF)), _const((1, 2 * D_FF))],
        out_specs=[_rows(tm, 2 * D_FF), _rows(tm, 2 * D_FF), _rows(tm, D_FF)],
        out_shape=[jax.ShapeDtypeStruct((T, 2 * D_FF), BF), jax.ShapeDtypeStruct((T, 2 * D_FF), F32),
                   jax.ShapeDtypeStruct((T, D_FF), BF)],
        scratch_shapes=[pltpu.VMEM((8, 2 * D_FF), F32)],
        compiler_params=_params(("arbitrary",)),
    )(h2, w_up, conv_w, conv_b)


def _ffn_down(y, w_down, x1, g4, target, tm):
    T = y.shape[0]

    def body(y_ref, w_ref, x1_ref, g_ref, t_ref, z_ref, dout_ref, loss_ref):
        @pl.when(pl.program_id(0) == 0)
        def _():
            loss_ref[...] = jnp.zeros_like(loss_ref)

        z = _dot(y_ref[...], w_ref[...])
        z_ref[...] = z
        _, n4 = _rms_stats(z)
        err = x1_ref[...] + n4 * g_ref[...] - t_ref[...]
        dout_ref[...] = err * (1.0 / D_MODEL)
        loss_ref[...] += 0.5 * jnp.sum(jnp.mean(err * err, axis=-1, keepdims=True), axis=0, keepdims=True)

    return _pcall(
        body, name="ffn_down", grid=(T // tm,),
        in_specs=[_rows(tm, D_FF), _vmem_full(), _rows(tm, D_MODEL), _const((1, D_MODEL)), _rows(tm, D_MODEL)],
        out_specs=[_rows(tm, D_MODEL), _rows(tm, D_MODEL), _const((8, 128))],
        out_shape=[jax.ShapeDtypeStruct((T, D_MODEL), F32), jax.ShapeDtypeStruct((T, D_MODEL), F32),
                   jax.ShapeDtypeStruct((8, 128), F32)],
        compiler_params=_params(("arbitrary",)),
    )(y, w_down, x1, g4, target)


def _ffn_bwd_a(z, dout, g4, w_down, u, up_b, conv_w, tm):
    T = z.shape[0]
    nt = T // tm

    def body(z_ref, dout_ref, g_ref, w_ref, u_ref, up_ref, cw_ref, dz_ref, dup_ref, dg4_ref, dcb_ref, dcw_ref, carry):
        @pl.when(pl.program_id(0) == 0)
        def _():
            dg4_ref[...] = jnp.zeros_like(dg4_ref)
            dcb_ref[...] = jnp.zeros_like(dcb_ref)
            dcw_ref[...] = jnp.zeros_like(dcw_ref)
            carry[...] = jnp.zeros_like(carry)

        r4, n4 = _rms_stats(z_ref[...])
        dout = dout_ref[...]
        dg4_ref[...] += jnp.sum(dout * n4, axis=0, keepdims=True)
        dz = _rms_bwd(n4, r4, dout * g_ref[...]).astype(BF)
        dz_ref[...] = dz
        for c in range(N_FF_CHUNKS):
            gate = slice(c * FF_CHUNK, (c + 1) * FF_CHUNK)
            val = slice(D_FF + c * FF_CHUNK, D_FF + (c + 1) * FF_CHUNK)
            dy = _dot_nt(dz, w_ref[gate, :])
            gelu, dgelu = _gelu_parts(u_ref[:, gate])
            for cols, d in ((gate, dy * u_ref[:, val] * dgelu), (val, dy * gelu)):
                after = carry[:, cols]
                d1 = _shift_up(d, 1, after)
                d2 = _shift_up(d, 2, after)
                upc = up_ref[:, cols].astype(F32)
                dcb_ref[:, cols] += jnp.sum(d, axis=0, keepdims=True)
                dcw_ref[pl.ds(2, 1), cols] += jnp.sum(d * upc, axis=0, keepdims=True)
                dcw_ref[pl.ds(1, 1), cols] += jnp.sum(d1 * upc, axis=0, keepdims=True)
                dcw_ref[pl.ds(0, 1), cols] += jnp.sum(d2 * upc, axis=0, keepdims=True)
                dup_ref[:, cols] = (cw_ref[pl.ds(2, 1), cols] * d + cw_ref[pl.ds(1, 1), cols] * d1
                                    + cw_ref[pl.ds(0, 1), cols] * d2).astype(BF)
                carry[:, cols] = d[0:8, :]

    rev = lambda w: pl.BlockSpec((tm, w), lambda i: (nt - 1 - i, 0))
    return _pcall(
        body, name="ffn_bwd_a", grid=(nt,),
        in_specs=[rev(D_MODEL), rev(D_MODEL), _const((1, D_MODEL)), _vmem_full(), rev(2 * D_FF), rev(2 * D_FF),
                  _const((3, 2 * D_FF))],
        out_specs=[rev(D_MODEL), rev(2 * D_FF), _const((1, D_MODEL)), _const((1, 2 * D_FF)),
                   _const((3, 2 * D_FF))],
        out_shape=[jax.ShapeDtypeStruct((T, D_MODEL), BF), jax.ShapeDtypeStruct((T, 2 * D_FF), BF),
                   jax.ShapeDtypeStruct((1, D_MODEL), F32), jax.ShapeDtypeStruct((1, 2 * D_FF), F32),
                   jax.ShapeDtypeStruct((3, 2 * D_FF), F32)],
        scratch_shapes=[pltpu.VMEM((8, 2 * D_FF), F32)],
        compiler_params=_params(("arbitrary",)),
    )(z, dout, g4, w_down, u, up_b, conv_w)


def _ffn_bwd_b(dup, w_up, x1, dout, g3, mixed, g2, w_out, tm, swaps):
    T = x1.shape[0]
    ns = len(swaps)
    nt = T // tm

    def body(dup_ref, wup_ref, x1_ref, dout_ref, g3_ref, mixed_ref, g2_ref, wout_ref, *rest):
        ex = _exchange_of(rest[:ns], rest[ns + 6:2 * ns + 6], rest[2 * ns + 6:], 0)
        dx1_ref, dmixed_ref, da_ref, dr_ref, dg3_ref, dg2_ref = rest[ns:ns + 6]

        @pl.when(pl.program_id(0) == 0)
        def _():
            ex.start()
            dg3_ref[...] = jnp.zeros_like(dg3_ref)
            dg2_ref[...] = jnp.zeros_like(dg2_ref)

        dh2 = _dot_nt(dup_ref[...], wup_ref[...])
        r3, n3 = _rms_stats(x1_ref[...])
        dg3_ref[...] += jnp.sum(dh2 * n3, axis=0, keepdims=True)
        dx1 = dout_ref[...] + _rms_bwd(n3, r3, dh2 * g3_ref[...])
        dx1_ref[...] = dx1
        r2, n2 = _rms_stats(mixed_ref[...])
        dg2_ref[...] += jnp.sum(dx1 * n2, axis=0, keepdims=True)
        dmixed = _rms_bwd(n2, r2, dx1 * g2_ref[...]).astype(BF)
        dmixed_ref[...] = dmixed
        da_ref[...] = _dot_nt(dmixed, wout_ref[:ATTN_W, :])
        dr_ref[...] = _dot_nt(dmixed, wout_ref[ATTN_W:, :])
        pl.when(pl.program_id(0) == nt - 1)(ex.wait)

    outs = _pcall(
        body, name="ffn_bwd_b", grid=(nt,),
        in_specs=[_rows(tm, 2 * D_FF), _vmem_full(), _rows(tm, D_MODEL), _rows(tm, D_MODEL), _const((1, D_MODEL)),
                  _rows(tm, D_MODEL), _const((1, D_MODEL)), _vmem_full()] + [ANY_SPEC] * ns,
        out_specs=[_rows(tm, D_MODEL), _rows(tm, D_MODEL), _rows(tm, ATTN_W), _rows(tm, RET_W),
                   _const((1, D_MODEL)), _const((1, D_MODEL))] + [ANY_SPEC] * ns,
        out_shape=[jax.ShapeDtypeStruct((T, D_MODEL), F32), jax.ShapeDtypeStruct((T, D_MODEL), BF),
                   jax.ShapeDtypeStruct((T, ATTN_W), F32), jax.ShapeDtypeStruct((T, RET_W), F32),
                   jax.ShapeDtypeStruct((1, D_MODEL), F32), jax.ShapeDtypeStruct((1, D_MODEL), F32)]
        + _exchange_shapes([], swaps),
        scratch_shapes=_Exchange.scratch(ns),
        compiler_params=_params(("arbitrary",)),
    )(dup, w_up, x1, dout, g3, mixed, g2, w_out, *swaps)
    return outs[:6], outs[6:]


def _ret_bwd(decay, qr, kr, vr, gr, o, states, dr, d_intra, d_intra_t, xi_b, zeta_b, cos, sin_s, swaps, ncs):
    T = qr.shape[0]
    nc = T // BLOCK
    H, C = N_RET_HEADS, RET_HEAD_DIM
    ns = len(swaps)

    def body(decay_ref, q_ref, k_ref, v_ref, g_ref, o_ref, s_ref, dr_ref, d_ref, dt_ref, xi_ref, zeta_ref,
             cos_ref, sin_ref, *rest):
        ex = _exchange_of(rest[:ns], rest[ns + 1:2 * ns + 1], rest[2 * ns + 2:], 0)
        dret_ref, gstate = rest[ns], rest[2 * ns + 1]

        @pl.when(pl.program_id(0) == 0)
        def _():
            ex.start()
            gstate[...] = jnp.zeros_like(gstate)

        even = lax.broadcasted_iota(jnp.int32, (BLOCK, C), 1) % 2 == 0
        for h in range(H):
            cs = slice(h * C, (h + 1) * C)
            gst = gstate[h]
            for b in reversed(range(ncs)):
                rows = slice(b * BLOCK, (b + 1) * BLOCK)
                cos_t, sin_t = cos_ref[rows, :], sin_ref[rows, :]
                q, k, v = q_ref[rows, cs], k_ref[rows, cs], v_ref[rows, cs]
                g, o_h, dr_h = g_ref[rows, cs], o_ref[rows, cs], dr_ref[rows, cs]
                mu = jnp.mean(o_h, axis=-1, keepdims=True)
                oc = o_h - mu
                rs = lax.rsqrt(jnp.mean(oc * oc, axis=-1, keepdims=True) + GN_EPS)
                on = oc * rs
                sg = jax.nn.sigmoid(g)
                dg = dr_h * on * (sg * (1.0 + g * (1.0 - sg)))
                don = dr_h * (g * sg)
                do = rs * (don - jnp.mean(don, axis=-1, keepdims=True)
                           - on * jnp.mean(don * on, axis=-1, keepdims=True))
                do_b = do.astype(BF)
                dox_b = (do * xi_ref[h]).astype(BF)
                gst_b = gst.astype(BF)
                kz = (k.astype(F32) * zeta_ref[h]).astype(BF)
                da_b = (_dot_nt(do_b, v) * d_ref[h]).astype(BF)
                dat_b = (_dot_nt(v, do_b) * dt_ref[h]).astype(BF)
                mt_b = (_dot_nt(k, q) * dt_ref[h]).astype(BF)
                dq = _dot(da_b, k) + _dot_nt(dox_b, s_ref[b, h])
                dk = _dot(dat_b, q) + _dot_nt(v, gst_b) * zeta_ref[h]
                dv = _dot(mt_b, do_b) + _dot(kz, gst_b)
                gst = decay_ref[h] * gst + _dot_tn(q, dox_b)
                dq = dq * cos_t - _rot(dq, even) * sin_t
                dk = (dk * cos_t - _rot(dk, even) * sin_t) * RET_K_SCALE
                dret_ref[rows, h * C:(h + 1) * C] = dq.astype(BF)
                dret_ref[rows, RET_W + h * C:RET_W + (h + 1) * C] = dk.astype(BF)
                dret_ref[rows, 2 * RET_W + h * C:2 * RET_W + (h + 1) * C] = dv.astype(BF)
                dret_ref[rows, 3 * RET_W + h * C:3 * RET_W + (h + 1) * C] = dg.astype(BF)
            gstate[h] = gst
        pl.when(pl.program_id(0) == steps - 1)(ex.wait)

    steps = nc // ncs
    rev = lambda w: pl.BlockSpec((BLOCK * ncs, w), lambda n: (steps - 1 - n, 0))
    tab = pl.BlockSpec((H, C, C), lambda n: (0, 0, 0))
    outs = _pcall(
        body, name="ret_bwd", grid=(steps,),
        in_specs=[_smem_full(), rev(RET_W), rev(RET_W), rev(RET_W), rev(RET_W), rev(RET_W),
                  pl.BlockSpec((ncs, H, C, C), lambda n: (steps - 1 - n, 0, 0, 0)), rev(RET_W), tab, tab, tab, tab,
                  rev(C), rev(C)] + [ANY_SPEC] * ns,
        out_specs=[rev(4 * RET_W)] + [ANY_SPEC] * ns,
        out_shape=[jax.ShapeDtypeStruct((T, 4 * RET_W), BF)] + _exchange_shapes([], swaps),
        scratch_shapes=[pltpu.VMEM((H, C, C), F32)] + _Exchange.scratch(ns),
        compiler_params=_params(("arbitrary",)),
    )(decay, qr, kr, vr, gr, o, states, dr, d_intra, d_intra_t, xi_b, zeta_b, cos, sin_s, *swaps)
    return outs[0], outs[1:]


def _attn_bwd_dq(sinks, qa, ka, va, da, lse, nbs, swaps):
    T = qa.shape[0]
    steps = T // (BLOCK * nbs)
    R = nbs * N_ATTN_HEADS * BLOCK
    ns = len(swaps)

    def body(sink_ref, q_ref, kc_ref, kp_ref, vc_ref, vp_ref, da_ref, lse_ref, *rest):
        ex = _exchange_of(rest[:ns], rest[ns + 3:2 * ns + 3], rest[2 * ns + 3:], 0)
        dq_ref, deltat_ref, dsink_ref = rest[ns:ns + 3]
        n = pl.program_id(0)

        @pl.when(n == 0)
        def _():
            ex.start()
            dsink_ref[...] = jnp.zeros_like(dsink_ref)

        kcat = _block_variants(kp_ref, kc_ref, nbs)
        vcat = _block_variants(vp_ref, vc_ref, nbs)
        tri1 = _tri(BLOCK)
        lane = lax.broadcasted_iota(jnp.int32, (BLOCK, BLOCK), 1)
        s_tiles, dp_tiles, lse_cols = [], [], []
        for b in range(nbs):
            rows = slice(b * BLOCK, (b + 1) * BLOCK)
            lse_tile = lse_ref[rows, :]
            for pair in range(N_ATTN_HEADS // 2):
                qp = _scaled(q_ref[rows, pair * 128:(pair + 1) * 128])
                dop = da_ref[rows, pair * 128:(pair + 1) * 128].astype(BF)
                for e in range(2):
                    s = _fold(_dot_nt(qp, kcat[b][pair // 2][e]), tri1, True)
                    if b == 0:
                        s = jnp.where(tri1 & (n == 0), MASK_VALUE, s)
                    s_tiles.append(s)
                    dp_tiles.append(_fold(_dot_nt(dop, vcat[b][pair // 2][e]), tri1, True))
                    lse_cols.append(jnp.sum(jnp.where(lane == 2 * pair + e, lse_tile, 0.0), axis=-1, keepdims=True))
        lse_c = jnp.concatenate(lse_cols, axis=0)
        p = jnp.exp(jnp.concatenate(s_tiles, axis=0) - lse_c)
        dp = jnp.concatenate(dp_tiles, axis=0)
        delta = jnp.sum(p * dp, axis=-1, keepdims=True)
        ds2 = _unfold(p * (dp - delta), _tri(R), True)
        for b in range(nbs):
            for pair in range(N_ATTN_HEADS // 2):
                r0 = (b * N_ATTN_HEADS + 2 * pair) * BLOCK
                acc = (_dot(ds2[r0:r0 + BLOCK, :], kcat[b][pair // 2][0])
                       + _dot(ds2[r0 + BLOCK:r0 + 2 * BLOCK, :], kcat[b][pair // 2][1]))
                dq_ref[b * BLOCK:(b + 1) * BLOCK, pair * 128:(pair + 1) * 128] = (acc * ATTN_SCALE).astype(BF)
        for b, t in enumerate(_head_cols(delta, nbs)):
            deltat_ref[:, b * BLOCK:(b + 1) * BLOCK] = t.T[:N_ATTN_HEADS, :]
        sink = jnp.concatenate([jnp.full((BLOCK, 1), sink_ref[head], F32)
                                for _ in range(nbs) for head in range(N_ATTN_HEADS)], axis=0)
        ds_sink = -jnp.exp(sink - lse_c) * delta
        row8 = lax.broadcasted_iota(jnp.int32, (N_ATTN_HEADS, BLOCK), 0)
        dsink = jnp.zeros((N_ATTN_HEADS, BLOCK), F32)
        for b in range(nbs):
            for head in range(N_ATTN_HEADS):
                r0 = (b * N_ATTN_HEADS + head) * BLOCK
                dsink = dsink + jnp.where(row8 == head, jnp.sum(ds_sink[r0:r0 + BLOCK, :], axis=0, keepdims=True), 0.0)
        dsink_ref[...] += dsink
        pl.when(n == steps - 1)(ex.wait)

    cur = lambda w: pl.BlockSpec((BLOCK * nbs, w), lambda n: (n, 0))
    prev = lambda w: pl.BlockSpec((BLOCK, w), lambda n: (jnp.maximum(n * nbs - 1, 0), 0))
    outs = _pcall(
        body, name="attn_bwd_dq", grid=(steps,),
        in_specs=[_smem_full(), cur(ATTN_W), cur(KV_W), prev(KV_W), cur(KV_W), prev(KV_W), cur(ATTN_W), cur(BLOCK)]
        + [ANY_SPEC] * ns,
        out_specs=[cur(ATTN_W), pl.BlockSpec((N_ATTN_HEADS, BLOCK * nbs), lambda n: (0, n)),
                   _const((N_ATTN_HEADS, BLOCK))] + [ANY_SPEC] * ns,
        out_shape=[jax.ShapeDtypeStruct((T, ATTN_W), BF), jax.ShapeDtypeStruct((N_ATTN_HEADS, T), F32),
                   jax.ShapeDtypeStruct((N_ATTN_HEADS, BLOCK), F32)] + _exchange_shapes([], swaps),
        scratch_shapes=_Exchange.scratch(ns),
        compiler_params=_params(("arbitrary",)),
    )(sinks, qa, ka, ka, va, va, da, lse, *swaps)
    return outs[:3], outs[3:]


def _attn_bwd_dkv(qa, ka, va, da, lse_t, delta_t, nbs):
    T = qa.shape[0]
    nb = T // BLOCK
    steps = nb // nbs
    R = nbs * N_ATTN_HEADS * BLOCK

    def body(qc_ref, qn_ref, dac_ref, dan_ref, k_ref, v_ref, lc_ref, ln_ref, dc_ref, dn_ref, dk_ref, dv_ref):
        n = pl.program_id(0)
        tri1 = _tri(BLOCK, True)
        lo = lax.broadcasted_iota(jnp.int32, (BLOCK, 128), 1) < HEAD_DIM
        kv = [_kv_variants(k_ref[b * BLOCK:(b + 1) * BLOCK, :]) for b in range(nbs)]
        vv = [_kv_variants(v_ref[b * BLOCK:(b + 1) * BLOCK, :]) for b in range(nbs)]
        qcat, docat = [], []
        s_tiles, dp_tiles, lse_tiles, delta_tiles = [], [], [], []
        for b in range(nbs):
            rows = slice(b * BLOCK, (b + 1) * BLOCK)
            nrows = slice((b + 1) * BLOCK, (b + 2) * BLOCK)
            inside = b < nbs - 1
            for pair in range(N_ATTN_HEADS // 2):
                ps = slice(pair * 128, (pair + 1) * 128)
                q2 = _scaled(jnp.concatenate([qc_ref[rows, ps], qc_ref[nrows, ps] if inside else qn_ref[:, ps]], axis=0))
                do2 = jnp.concatenate([dac_ref[rows, ps], dac_ref[nrows, ps] if inside else dan_ref[:, ps]],
                                      axis=0).astype(BF)
                qcat.append(q2)
                docat.append(do2)
                for e in range(2):
                    one = pl.ds(2 * pair + e, 1)
                    s = _fold(_dot_nt(kv[b][pair // 2][e], q2), tri1, False)
                    if not inside:
                        s = jnp.where(tri1 & (n == steps - 1), MASK_VALUE, s)
                    s_tiles.append(s)
                    dp_tiles.append(_fold(_dot_nt(vv[b][pair // 2][e], do2), tri1, False))
                    lse_tiles.append(jnp.where(tri1, lc_ref[one, nrows] if inside else ln_ref[one, :], lc_ref[one, rows]))
                    delta_tiles.append(jnp.where(tri1, dc_ref[one, nrows] if inside else dn_ref[one, :],
                                                 dc_ref[one, rows]))
        pt = jnp.exp(jnp.concatenate(s_tiles, axis=0) - jnp.concatenate(lse_tiles, axis=0))
        dst = pt * (jnp.concatenate(dp_tiles, axis=0) - jnp.concatenate(delta_tiles, axis=0))
        tri = _tri(R, True)
        pt2 = _unfold(pt, tri, False)
        dst2 = _unfold(dst, tri, False)
        for b in range(nbs):
            dk = jnp.zeros((BLOCK, 128), F32)
            dv = jnp.zeros((BLOCK, 128), F32)
            for pair in range(N_ATTN_HEADS // 2):
                h = pair // 2
                for e in range(2):
                    r0 = (b * N_ATTN_HEADS + 2 * pair + e) * BLOCK
                    half = lo if e == 0 else jnp.logical_not(lo)
                    dv_e = jnp.where(half, _dot(pt2[r0:r0 + BLOCK, :], docat[b * 4 + pair]), 0.0)
                    dk_e = jnp.where(half, _dot(dst2[r0:r0 + BLOCK, :], qcat[b * 4 + pair]), 0.0)
                    if e != h:
                        dv_e = pltpu.roll(dv_e, HEAD_DIM, 1)
                        dk_e = pltpu.roll(dk_e, HEAD_DIM, 1)
                    dv = dv + dv_e
                    dk = dk + dk_e
            dk_ref[b * BLOCK:(b + 1) * BLOCK, :] = dk.astype(BF)
            dv_ref[b * BLOCK:(b + 1) * BLOCK, :] = dv.astype(BF)

    cur = lambda w: pl.BlockSpec((BLOCK * nbs, w), lambda n: (n, 0))
    nxt = lambda w: pl.BlockSpec((BLOCK, w), lambda n: (jnp.minimum((n + 1) * nbs, nb - 1), 0))
    tcur = pl.BlockSpec((N_ATTN_HEADS, BLOCK * nbs), lambda n: (0, n))
    tnxt = pl.BlockSpec((N_ATTN_HEADS, BLOCK), lambda n: (0, jnp.minimum((n + 1) * nbs, nb - 1)))
    return _pcall(
        body, name="attn_bwd_dkv", grid=(steps,),
        in_specs=[cur(ATTN_W), nxt(ATTN_W), cur(ATTN_W), nxt(ATTN_W), cur(KV_W), cur(KV_W), tcur, tnxt, tcur, tnxt],
        out_specs=[cur(KV_W), cur(KV_W)],
        out_shape=[jax.ShapeDtypeStruct((T, KV_W), BF), jax.ShapeDtypeStruct((T, KV_W), BF)],
        compiler_params=_params(("parallel",)),
    )(qa, qa, da, da, ka, va, lse_t, lse_t, delta_t, delta_t)


def _in_proj_bwd(dqa, dka, dva, dret, w_in, x, g1, dx1, tm):
    T = x.shape[0]

    def body(dqa_ref, dka_ref, dva_ref, dret_ref, w_ref, x_ref, g_ref, dx1_ref, dx_ref, dg1_ref):
        @pl.when(pl.program_id(0) == 0)
        def _():
            dg1_ref[...] = jnp.zeros_like(dg1_ref)

        dh = (_dot_nt(dqa_ref[...], w_ref[:, QA0:QA0 + ATTN_W]) + _dot_nt(dka_ref[...], w_ref[:, KA0:KA0 + KV_W])
              + _dot_nt(dva_ref[...], w_ref[:, VA0:VA0 + KV_W]) + _dot_nt(dret_ref[...], w_ref[:, QR0:IN_W]))
        r, n = _rms_stats(x_ref[...])
        dg1_ref[...] += jnp.sum(dh * n, axis=0, keepdims=True)
        dx_ref[...] = dx1_ref[...] + _rms_bwd(n, r, dh * g_ref[...])

    return _pcall(
        body, name="in_proj_bwd", grid=(T // tm,),
        in_specs=[_rows(tm, ATTN_W), _rows(tm, KV_W), _rows(tm, KV_W), _rows(tm, 4 * RET_W), _vmem_full(),
                  _rows(tm, D_MODEL), _const((1, D_MODEL)), _rows(tm, D_MODEL)],
        out_specs=[_rows(tm, D_MODEL), _const((1, D_MODEL))],
        out_shape=[jax.ShapeDtypeStruct((T, D_MODEL), F32), jax.ShapeDtypeStruct((1, D_MODEL), F32)],
        compiler_params=_params(("arbitrary",)),
    )(dqa, dka, dva, dret, w_in, x, g1, dx1)


def _wgrad(a_list, b_list, tk, name):
    T = a_list[0].shape[0]
    na, nbb = len(a_list), len(b_list)
    m_sizes = [a.shape[1] for a in a_list]
    n_sizes = [b.shape[1] for b in b_list]
    M, N = sum(m_sizes), sum(n_sizes)
    nk = T // tk
    chunk = 512

    def body(*refs):
        a_refs, b_refs = refs[:na], refs[na:na + nbb]
        out_ref, acc = refs[na + nbb], refs[na + nbb + 1]
        k = pl.program_id(0)

        @pl.when(k == 0)
        def _():
            acc[...] = jnp.zeros_like(acc)

        r0 = 0
        for ai in range(na):
            a = a_refs[ai][...]
            c0 = 0
            for bi in range(nbb):
                for s in range(0, n_sizes[bi], chunk):
                    w = min(chunk, n_sizes[bi] - s)
                    acc[r0:r0 + m_sizes[ai], c0 + s:c0 + s + w] += _dot_tn(a, b_refs[bi][:, s:s + w])
                c0 += n_sizes[bi]
            r0 += m_sizes[ai]

        @pl.when(k == nk - 1)
        def _():
            pltpu.sync_copy(acc, out_ref)

    return _pcall(
        body, name=name, grid=(nk,),
        in_specs=[_rows(tk, w) for w in m_sizes + n_sizes],
        out_specs=pl.BlockSpec(memory_space=pl.ANY),
        out_shape=jax.ShapeDtypeStruct((M, N), F32),
        scratch_shapes=[pltpu.VMEM((M, N), F32)],
        compiler_params=_params(("arbitrary",)),
    )(*a_list, *b_list)


def _adamw_math(w, g, m, v):
    m = ADAM_B1 * m + (1.0 - ADAM_B1) * g
    v = ADAM_B2 * v + (1.0 - ADAM_B2) * (g * g)
    m_hat = m / (1.0 - ADAM_B1 ** ADAM_STEP)
    v_hat = v / (1.0 - ADAM_B2 ** ADAM_STEP)
    delta = -ADAM_LR * (m_hat / (jnp.sqrt(v_hat) + ADAM_EPS) + ADAM_WD * w)
    return delta, m, v


def _sum_parts(parts_ref):
    g = parts_ref[0]
    for i in range(1, N_DEV):
        g = g + parts_ref[i]
    return g


def _adamw_shard(parts, w, m, v, tr, name):
    R, C = w.shape

    def body(p_ref, w_ref, m_ref, v_ref, g_ref, d_ref, nm_ref, nv_ref):
        g = _sum_parts(p_ref)
        g_ref[...] = g
        d_ref[...], nm_ref[...], nv_ref[...] = _adamw_math(w_ref[...], g, m_ref[...], v_ref[...])

    blk = pl.BlockSpec((tr, C), lambda i: (i, 0))
    return _pcall(
        body, name=name, grid=(R // tr,),
        in_specs=[pl.BlockSpec((N_DEV, tr, C), lambda i: (0, i, 0)), blk, blk, blk],
        out_specs=[blk] * 4,
        out_shape=[jax.ShapeDtypeStruct((R, C), F32)] * 4,
        compiler_params=_params(("parallel",)),
    )(parts, w, m, v)


def _sum_small(parts):
    def body(p_ref, g_ref):
        g_ref[...] = _sum_parts(p_ref)

    return _pcall(body, name="sum_small", out_shape=jax.ShapeDtypeStruct(parts.shape[1:], F32),
                  in_specs=[_vmem_full()], out_specs=_vmem_full())(parts)


def _adamw_small(g, w, m, v, name):
    def body(g_ref, w_ref, m_ref, v_ref, d_ref, nm_ref, nv_ref):
        d_ref[...], nm_ref[...], nv_ref[...] = _adamw_math(w_ref[...], g_ref[...], m_ref[...], v_ref[...])

    return _pcall(body, name=name, out_shape=[jax.ShapeDtypeStruct(w.shape, F32)] * 3,
                  in_specs=[_vmem_full()] * 4, out_specs=[_vmem_full()] * 3)(g, w, m, v)


def _tables(T):
    h, c = N_RET_HEADS, BLOCK
    pos = jnp.arange(T, dtype=F32)
    angle = 1.0 / jnp.power(10000.0, jnp.linspace(0.0, 1.0, RET_HEAD_DIM // 2, dtype=F32))
    angle = jnp.repeat(angle, 2)
    sin = jnp.sin(pos[:, None] * angle[None])
    cos = jnp.cos(pos[:, None] * angle[None])
    even = (jnp.arange(RET_HEAD_DIM) % 2 == 0)[None, :]
    sin_s = jnp.where(even, -sin, sin)
    log_gamma = jnp.log(1.0 - jnp.power(2.0, -5.0 - jnp.arange(h, dtype=F32)))
    idx = jnp.arange(c, dtype=F32)
    rel = idx[:, None] - idx[None, :]
    d_intra = jnp.where(rel[None] >= 0, jnp.exp(log_gamma[:, None, None] * jnp.maximum(rel, 0.0)[None]), 0.0)
    xi = jnp.exp(log_gamma[None, :] * (idx[:, None] + 1.0))
    zeta = jnp.exp(log_gamma[None, :] * (c - 1.0 - idx[:, None]))
    decay = jnp.exp(log_gamma * c)
    xi_b = jnp.broadcast_to(xi.T[:, :, None], (h, c, RET_HEAD_DIM))
    zeta_b = jnp.broadcast_to(zeta.T[:, :, None], (h, c, RET_HEAD_DIM))
    return cos, sin_s, d_intra, jnp.swapaxes(d_intra, 1, 2), xi_b, zeta_b, decay


def _to_shards(full, cols):
    r = full.shape[0]
    return jnp.swapaxes(full.reshape(r, N_DEV, cols), 0, 1)


def _from_shards(sh):
    n, r, cols = sh.shape
    return jnp.swapaxes(sh, 0, 1).reshape(r, n * cols)


SMALL_ROWS = 216


def _pack_small(gains, conv_b, conv_w, sinks):
    parts = [g.reshape(8, 128) for g in gains] + [conv_b.reshape(44, 128), conv_w.reshape(132, 128),
                                                  jnp.pad(sinks.reshape(1, 8), ((0, 0), (0, 120)))]
    packed = jnp.concatenate(parts, axis=0)
    return jnp.pad(packed, ((0, SMALL_ROWS - packed.shape[0]), (0, 0)))


def kernel(x, mix_pre_norm, w_in, attn_sinks, w_out, mix_post_norm, ffn_pre_norm, w_up, conv_w, conv_b, w_down, ffn_post_norm, loss_target, m_mix_pre_norm, m_w_in, m_attn_sinks, m_w_out, m_mix_post_norm, m_ffn_pre_norm, m_w_up, m_conv_w, m_conv_b, m_w_down, m_ffn_post_norm, v_mix_pre_norm, v_w_in, v_attn_sinks, v_w_out, v_mix_post_norm, v_ffn_pre_norm, v_w_up, v_conv_w, v_conv_b, v_w_down, v_ffn_post_norm):
    T = x.shape[1]
    tm = min(512, T)
    tm_ff = min(256, T)
    nbs = min(4, T // BLOCK)
    x2 = x.reshape(T, D_MODEL)
    target = loss_target.reshape(T, D_MODEL)
    me = 4 * lax.axis_index("x") + 2 * lax.axis_index("y") + lax.axis_index("c")

    g_in, g_cw = _exchange_call([w_in[0].astype(BF), conv_w[0]], [], "gather_w_in")
    w_in_f = _from_shards(g_in)
    conv_w_f = _from_shards(g_cw)
    cos, sin_s, d_intra, d_intra_t, xi_b, zeta_b, decay = _tables(T)
    sinks = attn_sinks.reshape(N_ATTN_HEADS)

    (h1, qa, ka, va, qr, kr, vr, gr), (g_up, g_out) = _in_proj(
        x2, mix_pre_norm, w_in_f, cos, sin_s, tm, [w_up[0].astype(BF), w_out[0].astype(BF)])
    w_up_f = _from_shards(g_up)
    w_out_f = g_out.reshape(D_MODEL, D_MODEL)
    (a, lse, lse_t), (g_down,) = _attn_fwd(sinks, qa, ka, va, nbs, [w_down[0].astype(BF)])
    w_down_f = g_down.reshape(D_FF, D_MODEL)
    o, states, r = _ret_fwd(decay, qr, kr, vr, gr, d_intra, xi_b, zeta_b, nbs)
    mixed, x1, h2 = _out_proj(a, r, w_out_f, x2, mix_post_norm, ffn_pre_norm, tm)
    up_b, u, y = _ffn_up(h2, w_up_f, conv_w_f, conv_b, tm_ff)
    z, dout, loss_acc = _ffn_down(y, w_down_f, x1, ffn_post_norm, target, tm)
    loss = lax.psum(loss_acc[0, 0], ("x", "y", "c"))

    dz, dup, dg4, dcb, dcw = _ffn_bwd_a(z, dout, ffn_post_norm, w_down_f, u, up_b, conv_w_f, tm_ff)
    gw_down = _wgrad([y], [dz], tm, "wgrad_down")
    gw_up = _wgrad([h2], [dup], tm, "wgrad_up")
    (dx1, dmixed, da, dr, dg3, dg2), (p_up,) = _ffn_bwd_b(
        dup, w_up_f, x1, dout, ffn_pre_norm, mixed, mix_post_norm, w_out_f, tm_ff,
        [_to_shards(gw_up, 2 * D_FF // N_DEV)])
    gw_out = _wgrad([a, r], [dmixed], tm, "wgrad_out")
    dret, (p_down,) = _ret_bwd(decay, qr, kr, vr, gr, o, states, dr, d_intra, d_intra_t, xi_b, zeta_b, cos, sin_s,
                               [gw_down.reshape(N_DEV, D_FF // N_DEV, D_MODEL)], nbs)
    (dqa, delta_t, dsink), (p_out,) = _attn_bwd_dq(sinks, qa, ka, va, da, lse, nbs,
                                                   [gw_out.reshape(N_DEV, D_MODEL // N_DEV, D_MODEL)])
    dka, dva = _attn_bwd_dkv(qa, ka, va, da, lse_t, delta_t, nbs)
    grad_x, dg1 = _in_proj_bwd(dqa, dka, dva, dret, w_in_f, x2, mix_pre_norm, dx1, tm)
    gw_in = _wgrad([h1], [dqa, dka, dva, dret], tm, "wgrad_in")

    small = _pack_small([dg1, dg2, dg3, dg4], dcb, dcw, dsink[:, 0])
    small_all, p_in = _exchange_call([small], [_to_shards(gw_in, IN_W // N_DEV)], "exchange_last")
    g_small = _sum_small(small_all)

    g_w_in, d_w_in, nm_w_in, nv_w_in = _adamw_shard(p_in, w_in[0], m_w_in[0], v_w_in[0], 256, "adamw_in")
    g_w_up, d_w_up, nm_w_up, nv_w_up = _adamw_shard(p_up, w_up[0], m_w_up[0], v_w_up[0], 256, "adamw_up")
    g_w_out, d_w_out, nm_w_out, nv_w_out = _adamw_shard(p_out, w_out[0], m_w_out[0], v_w_out[0], 128, "adamw_out")
    g_w_down, d_w_down, nm_w_down, nv_w_down = _adamw_shard(p_down, w_down[0], m_w_down[0], v_w_down[0], 176,
                                                            "adamw_down")
    gains = [mix_pre_norm, mix_post_norm, ffn_pre_norm, ffn_post_norm]
    m_gains = [m_mix_pre_norm, m_mix_post_norm, m_ffn_pre_norm, m_ffn_post_norm]
    v_gains = [v_mix_pre_norm, v_mix_post_norm, v_ffn_pre_norm, v_ffn_post_norm]
    zeros_cw = jnp.zeros((3, 2 * D_FF), F32)
    w_small = _pack_small(gains, conv_b, zeros_cw, attn_sinks)
    m_small = _pack_small(m_gains, m_conv_b, zeros_cw, m_attn_sinks)
    v_small = _pack_small(v_gains, v_conv_b, zeros_cw, v_attn_sinks)
    d_small, nm_small, nv_small = _adamw_small(g_small, w_small, m_small, v_small, "adamw_small")
    shard_cols = 2 * D_FF // N_DEV
    g_cw = lax.dynamic_slice(g_small[76:208].reshape(3, 2 * D_FF), (0, me * shard_cols), (3, shard_cols))
    d_cw, nm_cw, nv_cw = _adamw_small(g_cw, conv_w[0], m_conv_w[0], v_conv_w[0], "adamw_conv_w")

    def unpack(p):
        gains_o = [p[8 * i:8 * i + 8].reshape(1, D_MODEL) for i in range(4)]
        return gains_o, p[32:76].reshape(1, 2 * D_FF), p[208:209, :N_ATTN_HEADS]

    def leaves(p, w_in_s, w_out_s, w_up_s, cw_s, w_down_s):
        (pre1, post1, pre2, post2), cb, sk = unpack(p)
        return [pre1, w_in_s[None], sk, w_out_s[None], post1, pre2, w_up_s[None], cw_s[None], cb, w_down_s[None],
                post2]

    return (loss, grad_x.reshape(1, T, D_MODEL),
            *leaves(g_small, g_w_in, g_w_out, g_w_up, g_cw, g_w_down),
            *leaves(d_small, d_w_in, d_w_out, d_w_up, d_cw, d_w_down),
            *leaves(nm_small, nm_w_in, nm_w_out, nm_w_up, nm_cw, nm_w_down),
            *leaves(nv_small, nv_w_in, nv_w_out, nv_w_up, nv_cw, nv_w_down))
```

```python
import functools
import math

import jax
import jax.numpy as jnp
from jax import lax
from jax.experimental import pallas as pl
from jax.experimental.pallas import tpu as pltpu

F32 = jnp.float32
BF = jnp.bfloat16

N_DEV = 8
D_MODEL = 1024
HEAD_DIM = 64
ATTN_W = 512
N_ATTN_HEADS = 8
KV_W = 128
BLOCK = 128
RET_W = 512
N_RET_HEADS = 4
RET_HEAD_DIM = 128
IN_W = 2816
D_FF = 2816
RMS_EPS = 1e-6
GN_EPS = 1e-6
MASK_VALUE = -1e30
ATTN_SCALE = HEAD_DIM ** -0.5
RET_K_SCALE = RET_HEAD_DIM ** -0.5
GELU_C = math.sqrt(2.0 / math.pi)
GELU_A = 0.044715

ADAM_LR = 0.001
ADAM_B1 = 0.9
ADAM_B2 = 0.999
ADAM_EPS = 1e-08
ADAM_WD = 0.01
ADAM_STEP = 10

VMEM_LIMIT_BYTES = 56 * 1024 * 1024
FF_SHARD = 2 * D_FF // N_DEV
FF_PAIRS = N_DEV // 2

QA0, KA0, VA0, QR0, KR0, VR0, GR0 = 0, 512, 640, 768, 1280, 1792, 2304

MESH_ID = pl.DeviceIdType.MESH


def _pcall(body, **kw):
    return pl.pallas_call(body, **kw)


def _params(sem=None):
    return pltpu.CompilerParams(dimension_semantics=sem, vmem_limit_bytes=VMEM_LIMIT_BYTES)


def _dot(a, b):
    return jnp.dot(a, b, preferred_element_type=F32)


def _dot_nt(a, b):
    return lax.dot_general(a, b, (((1,), (1,)), ((), ())), preferred_element_type=F32)


def _dot_tn(a, b):
    return lax.dot_general(a, b, (((0,), (0,)), ((), ())), preferred_element_type=F32)


def _vmem_full():
    return pl.BlockSpec(memory_space=pltpu.VMEM)


def _smem_full():
    return pl.BlockSpec(memory_space=pltpu.SMEM)


def _rows(tm, w):
    return pl.BlockSpec((tm, w), lambda i: (i, 0))


def _const(shape):
    return pl.BlockSpec(shape, lambda i: tuple(0 for _ in shape))


def _rms_stats(x):
    r = lax.rsqrt(jnp.mean(x * x, axis=-1, keepdims=True) + RMS_EPS)
    return r, x * r


def _rms_bwd(n, r, dn):
    return r * (dn - n * jnp.mean(dn * n, axis=-1, keepdims=True))


def _rot(x, even):
    w = x.shape[1]
    return jnp.where(even, pltpu.roll(x, w - 1, 1), pltpu.roll(x, 1, 1))


def _peers():
    x, y, c = lax.axis_index("x"), lax.axis_index("y"), lax.axis_index("c")
    flips = [(0, 0, 1), (1, 0, 0), (0, 1, 0), (1, 1, 0), (1, 0, 1), (0, 1, 1), (1, 1, 1)]
    peers = [(x ^ fx, y ^ fy, c ^ fc) for fx, fy, fc in flips]
    return 4 * x + 2 * y + c, peers


class _Exchange:
    def __init__(self, gathers, swaps, send_sems, recv_sems, local_sems):
        self.me, self.peers = _peers()
        self.slots = [4 * px + 2 * py + pc for px, py, pc in self.peers]
        self.pairs = [(src, dst, True) for src, dst in gathers] + [(src, dst, False) for src, dst in swaps]
        self.send_sems, self.recv_sems, self.local_sems = send_sems, recv_sems, local_sems

    @staticmethod
    def scratch(n):
        return [pltpu.SemaphoreType.DMA((n, N_DEV - 1)), pltpu.SemaphoreType.DMA((n, N_DEV - 1)),
                pltpu.SemaphoreType.DMA((n,))]

    def _parts(self, a, slot):
        src, _, whole = self.pairs[a]
        half = N_DEV // 2
        if whole:
            return [(None, src)]
        if isinstance(src, tuple):
            return [(slot < half, src[0].at[jnp.minimum(slot, half - 1)]),
                    (slot >= half, src[1].at[jnp.maximum(slot - half, 0)])]
        return [(None, src.at[slot])]

    def _local(self, a, src):
        return pltpu.make_async_copy(src, self.pairs[a][1].at[self.me], self.local_sems.at[a])

    def _remote(self, a, k, src, slot):
        return pltpu.make_async_remote_copy(
            src_ref=src, dst_ref=self.pairs[a][1].at[slot], send_sem=self.send_sems.at[a, k],
            recv_sem=self.recv_sems.at[a, k], device_id=self.peers[k], device_id_type=MESH_ID)

    def start(self):
        def go(cond, copy):
            if cond is None:
                copy.start()
            else:
                pl.when(cond)(copy.start)

        for a in range(len(self.pairs)):
            for cond, src in self._parts(a, self.me):
                go(cond, self._local(a, src))
            for k in range(N_DEV - 1):
                for cond, src in self._parts(a, self.slots[k]):
                    go(cond, self._remote(a, k, src, self.me))

    def wait(self):
        for a in range(len(self.pairs)):
            src = self._parts(a, self.me)[0][1]
            for k in range(N_DEV - 1):
                self._remote(a, k, src, self.slots[k]).wait_recv()
        for a in range(len(self.pairs)):
            src = self._parts(a, self.me)[0][1]
            for k in range(N_DEV - 1):
                self._remote(a, k, src, self.me).wait_send()
            self._local(a, src).wait()


ANY_SPEC = pl.BlockSpec(memory_space=pl.ANY)


def _exchange_shapes(gathers, swaps):
    return ([jax.ShapeDtypeStruct((N_DEV,) + a.shape, a.dtype) for a in gathers]
            + [jax.ShapeDtypeStruct(a.shape, a.dtype) for a in swaps])


def _exchange_of(ins, outs, sems, ng):
    return _Exchange(list(zip(ins[:ng], outs[:ng])), list(zip(ins[ng:], outs[ng:])), *sems)


def _exchange_call(gathers, swaps, name):
    ng, ns = len(gathers), len(swaps)
    n = ng + ns

    def body(*refs):
        ex = _exchange_of(refs[:n], refs[n:2 * n], refs[2 * n:], ng)
        ex.start()
        ex.wait()

    return _pcall(
        body, name=name, out_shape=_exchange_shapes(gathers, swaps),
        in_specs=[ANY_SPEC] * (ng + ns), out_specs=[ANY_SPEC] * (ng + ns),
        scratch_shapes=_Exchange.scratch(ng + ns),
    )(*gathers, *swaps)


def _in_proj(x, g1, w_in, cos, sin_s, tm, gathers):
    T = x.shape[0]
    ng = len(gathers)
    nt = T // tm

    def body(x_ref, g_ref, w_ref, cos_ref, sin_ref, *rest):
        ex = _exchange_of(rest[:ng], rest[ng + 8:2 * ng + 8], rest[2 * ng + 8:], ng)
        h_ref, qa_ref, ka_ref, va_ref, qr_ref, kr_ref, vr_ref, gr_ref = rest[ng:ng + 8]
        pl.when(pl.program_id(0) == 0)(ex.start)
        r, n = _rms_stats(x_ref[...])
        h = (n * g_ref[...]).astype(BF)
        h_ref[...] = h

        def proj(c0, w):
            return _dot(h, w_ref[:, c0:c0 + w])

        qa_ref[...] = proj(QA0, ATTN_W).astype(BF)
        ka_ref[...] = proj(KA0, KV_W).astype(BF)
        va_ref[...] = proj(VA0, KV_W).astype(BF)
        vr_ref[...] = proj(VR0, RET_W).astype(BF)
        gr_ref[...] = proj(GR0, RET_W)
        cos_t, sin_t = cos_ref[...], sin_ref[...]
        even = lax.broadcasted_iota(jnp.int32, (tm, RET_HEAD_DIM), 1) % 2 == 0
        for hd in range(N_RET_HEADS):
            c = hd * RET_HEAD_DIM
            q = proj(QR0 + c, RET_HEAD_DIM)
            k = proj(KR0 + c, RET_HEAD_DIM) * RET_K_SCALE
            qr_ref[:, c:c + RET_HEAD_DIM] = (q * cos_t + _rot(q, even) * sin_t).astype(BF)
            kr_ref[:, c:c + RET_HEAD_DIM] = (k * cos_t + _rot(k, even) * sin_t).astype(BF)
        pl.when(pl.program_id(0) == nt - 1)(ex.wait)

    widths = [D_MODEL, ATTN_W, KV_W, KV_W, RET_W, RET_W, RET_W, RET_W]
    dts = [BF] * 7 + [F32]
    outs = _pcall(
        body, name="in_proj", grid=(nt,),
        in_specs=[_rows(tm, D_MODEL), _const((1, D_MODEL)), _vmem_full(), _rows(tm, RET_HEAD_DIM),
                  _rows(tm, RET_HEAD_DIM)] + [ANY_SPEC] * ng,
        out_specs=[_rows(tm, w) for w in widths] + [ANY_SPEC] * ng,
        out_shape=[jax.ShapeDtypeStruct((T, w), dt) for w, dt in zip(widths, dts)] + _exchange_shapes(gathers, []),
        scratch_shapes=_Exchange.scratch(ng),
        compiler_params=_params(("arbitrary",)),
    )(x, g1, w_in, cos, sin_s, *gathers)
    return outs[:8], outs[8:]


def _kv_variants(kk):
    kf = kk.astype(F32)
    lo = lax.broadcasted_iota(jnp.int32, kf.shape, 1) < HEAD_DIM
    h0_lo = jnp.where(lo, kf, 0.0)
    h1_hi = jnp.where(lo, 0.0, kf)
    h0_hi = pltpu.roll(h0_lo, HEAD_DIM, 1)
    h1_lo = pltpu.roll(h1_hi, HEAD_DIM, 1)
    return [[h0_lo.astype(BF), h0_hi.astype(BF)], [h1_lo.astype(BF), h1_hi.astype(BF)]]


def _col_to_tile(tile, col, head):
    lane = lax.broadcasted_iota(jnp.int32, tile.shape, 1)
    return jnp.where(lane == head, col, tile)


def _tri(rows, key_major=False):
    i = lax.broadcasted_iota(jnp.int32, (rows, BLOCK), 0) & (BLOCK - 1)
    j = lax.broadcasted_iota(jnp.int32, (rows, BLOCK), 1)
    return i > j if key_major else j > i


def _fold(x2, tri, first_above):
    a, b = x2[:, :BLOCK], x2[:, BLOCK:]
    return jnp.where(tri, a, b) if first_above else jnp.where(tri, b, a)


def _unfold(x, tri, first_above):
    up, low = jnp.where(tri, x, 0.0), jnp.where(tri, 0.0, x)
    return jnp.concatenate([up, low] if first_above else [low, up], axis=1).astype(BF)


def _scaled(q):
    return (q.astype(F32) * ATTN_SCALE).astype(BF)


def _cat_variants(prev, cur):
    return [[jnp.concatenate([prev[h][e], cur[h][e]], axis=0) for e in range(2)] for h in range(2)]


def _block_variants(prev_ref, cur_ref, nbs):
    var = [_kv_variants(prev_ref[...])] + [_kv_variants(cur_ref[b * BLOCK:(b + 1) * BLOCK, :]) for b in range(nbs)]
    return [_cat_variants(var[b], var[b + 1]) for b in range(nbs)]


def _head_cols(col, nbs):
    tiles = []
    for b in range(nbs):
        t = jnp.zeros((BLOCK, BLOCK), F32)
        for head in range(N_ATTN_HEADS):
            r0 = (b * N_ATTN_HEADS + head) * BLOCK
            t = _col_to_tile(t, col[r0:r0 + BLOCK, :], head)
        tiles.append(t)
    return tiles


def _attn_fwd(sinks, qa, ka, va, nbs, gathers):
    T = qa.shape[0]
    steps = T // (BLOCK * nbs)
    R = nbs * N_ATTN_HEADS * BLOCK
    ng = len(gathers)

    def body(sink_ref, q_ref, kc_ref, kp_ref, vc_ref, vp_ref, *rest):
        ex = _exchange_of(rest[:ng], rest[ng + 3:2 * ng + 3], rest[2 * ng + 3:], ng)
        a_ref, lse_ref, lset_ref = rest[ng:ng + 3]
        n = pl.program_id(0)
        pl.when(n == 0)(ex.start)
        kcat = _block_variants(kp_ref, kc_ref, nbs)
        vcat = _block_variants(vp_ref, vc_ref, nbs)
        tri1 = _tri(BLOCK)
        tiles = []
        for b in range(nbs):
            for pair in range(N_ATTN_HEADS // 2):
                qp = _scaled(q_ref[b * BLOCK:(b + 1) * BLOCK, pair * 128:(pair + 1) * 128])
                for e in range(2):
                    s = _fold(_dot_nt(qp, kcat[b][pair // 2][e]), tri1, True)
                    if b == 0:
                        s = jnp.where(tri1 & (n == 0), MASK_VALUE, s)
                    tiles.append(s)
        s = jnp.concatenate(tiles, axis=0)
        sink = jnp.concatenate([jnp.full((BLOCK, 1), sink_ref[head], F32)
                                for _ in range(nbs) for head in range(N_ATTN_HEADS)], axis=0)
        m = jnp.maximum(jnp.max(s, axis=-1, keepdims=True), sink)
        p = jnp.exp(s - m)
        z = jnp.sum(p, axis=-1, keepdims=True) + jnp.exp(sink - m)
        p2 = _unfold(p * (1.0 / z), _tri(R), True)
        for b in range(nbs):
            for pair in range(N_ATTN_HEADS // 2):
                r0 = (b * N_ATTN_HEADS + 2 * pair) * BLOCK
                acc = (_dot(p2[r0:r0 + BLOCK, :], vcat[b][pair // 2][0])
                       + _dot(p2[r0 + BLOCK:r0 + 2 * BLOCK, :], vcat[b][pair // 2][1]))
                a_ref[b * BLOCK:(b + 1) * BLOCK, pair * 128:(pair + 1) * 128] = acc.astype(BF)
        for b, t in enumerate(_head_cols(m + jnp.log(z), nbs)):
            lse_ref[b * BLOCK:(b + 1) * BLOCK, :] = t
            lset_ref[:, b * BLOCK:(b + 1) * BLOCK] = t.T[:N_ATTN_HEADS, :]
        pl.when(n == steps - 1)(ex.wait)

    cur = lambda w: pl.BlockSpec((BLOCK * nbs, w), lambda n: (n, 0))
    prev = lambda w: pl.BlockSpec((BLOCK, w), lambda n: (jnp.maximum(n * nbs - 1, 0), 0))
    outs = _pcall(
        body, name="attn_fwd", grid=(steps,),
        in_specs=[_smem_full(), cur(ATTN_W), cur(KV_W), prev(KV_W), cur(KV_W), prev(KV_W)] + [ANY_SPEC] * ng,
        out_specs=[cur(ATTN_W), cur(BLOCK), pl.BlockSpec((N_ATTN_HEADS, BLOCK * nbs), lambda n: (0, n))]
        + [ANY_SPEC] * ng,
        out_shape=[jax.ShapeDtypeStruct((T, ATTN_W), BF), jax.ShapeDtypeStruct((T, BLOCK), F32),
                   jax.ShapeDtypeStruct((N_ATTN_HEADS, T), F32)] + _exchange_shapes(gathers, []),
        scratch_shapes=_Exchange.scratch(ng),
        compiler_params=_params(("arbitrary",)),
    )(sinks, qa, ka, ka, va, va, *gathers)
    return outs[:3], outs[3:]


def _ret_fwd(decay, qr, kr, vr, gr, d_intra, xi_b, zeta_b, ncs):
    T = qr.shape[0]
    nc = T // BLOCK
    H, C = N_RET_HEADS, RET_HEAD_DIM

    def body(decay_ref, q_ref, k_ref, v_ref, g_ref, d_ref, xi_ref, zeta_ref, o_ref, s_ref, r_ref, state):
        @pl.when(pl.program_id(0) == 0)
        def _():
            state[...] = jnp.zeros_like(state)

        for h in range(H):
            cs = slice(h * C, (h + 1) * C)
            st = state[h]
            for b in range(ncs):
                rows = slice(b * BLOCK, (b + 1) * BLOCK)
                q, k, v = q_ref[rows, cs], k_ref[rows, cs], v_ref[rows, cs]
                st_b = st.astype(BF)
                s_ref[b, h] = st_b
                inner = (_dot_nt(q, k) * d_ref[h]).astype(BF)
                o = _dot(inner, v) + _dot(q, st_b) * xi_ref[h]
                kz = (k.astype(F32) * zeta_ref[h]).astype(BF)
                st = decay_ref[h] * st + _dot_tn(kz, v)
                o_ref[rows, cs] = o
                mu = jnp.mean(o, axis=-1, keepdims=True)
                oc = o - mu
                rs = lax.rsqrt(jnp.mean(oc * oc, axis=-1, keepdims=True) + GN_EPS)
                g = g_ref[rows, cs]
                r_ref[rows, cs] = (g * jax.nn.sigmoid(g) * (oc * rs)).astype(BF)
            state[h] = st

    cur = pl.BlockSpec((BLOCK * ncs, RET_W), lambda n: (n, 0))
    tab = pl.BlockSpec((H, C, C), lambda n: (0, 0, 0))
    return _pcall(
        body, name="ret_fwd", grid=(nc // ncs,),
        in_specs=[_smem_full(), cur, cur, cur, cur, tab, tab, tab],
        out_specs=[cur, pl.BlockSpec((ncs, H, C, C), lambda n: (n, 0, 0, 0)), cur],
        out_shape=[jax.ShapeDtypeStruct((T, RET_W), F32), jax.ShapeDtypeStruct((nc, H, C, C), BF),
                   jax.ShapeDtypeStruct((T, RET_W), BF)],
        scratch_shapes=[pltpu.VMEM((H, C, C), F32)],
        compiler_params=_params(("arbitrary",)),
    )(decay, qr, kr, vr, gr, d_intra, xi_b, zeta_b)


def _out_proj(a, r, w_out, x, g2, g3, tm):
    T = x.shape[0]

    def body(a_ref, r_ref, w_ref, x_ref, g2_ref, g3_ref, mixed_ref, x1_ref, h2_ref):
        mixed = _dot(a_ref[...], w_ref[:ATTN_W, :]) + _dot(r_ref[...], w_ref[ATTN_W:, :])
        mixed_ref[...] = mixed
        _, n2 = _rms_stats(mixed)
        x1 = x_ref[...] + n2 * g2_ref[...]
        x1_ref[...] = x1
        _, n3 = _rms_stats(x1)
        h2_ref[...] = (n3 * g3_ref[...]).astype(BF)

    return _pcall(
        body, name="out_proj", grid=(T // tm,),
        in_specs=[_rows(tm, ATTN_W), _rows(tm, RET_W), _vmem_full(), _rows(tm, D_MODEL), _const((1, D_MODEL)),
                  _const((1, D_MODEL))],
        out_specs=[_rows(tm, D_MODEL)] * 3,
        out_shape=[jax.ShapeDtypeStruct((T, D_MODEL), F32), jax.ShapeDtypeStruct((T, D_MODEL), F32),
                   jax.ShapeDtypeStruct((T, D_MODEL), BF)],
        compiler_params=_params(("parallel",)),
    )(a, r, w_out, x, g2, g3)


def _shift_down(cur, k, before):
    out = pltpu.roll(cur, k, 0)
    row = lax.broadcasted_iota(jnp.int32, before.shape, 0)
    top = jnp.where(row < k, pltpu.roll(before, k, 0), out[0:8])
    return jnp.concatenate([top, out[8:]], axis=0)


def _shift_up(cur, k, after):
    tm = cur.shape[0]
    out = pltpu.roll(cur, tm - k, 0)
    row = lax.broadcasted_iota(jnp.int32, after.shape, 0)
    bot = jnp.where(row >= 8 - k, pltpu.roll(after, 8 - k, 0), out[tm - 8:])
    return jnp.concatenate([out[:tm - 8], bot], axis=0)


def _gelu_parts(x):
    x2 = x * x
    th = jnp.tanh(GELU_C * (x + GELU_A * x * x2))
    gelu = 0.5 * x * (1.0 + th)
    dgelu = 0.5 * (1.0 + th) + 0.5 * x * (1.0 - th * th) * (GELU_C * (1.0 + 3.0 * GELU_A * x2))
    return gelu, dgelu


def _ffn_fwd(h2, w_up8, conv_w8, conv_b8, w_down, x1, g4, target, tm):
    T = h2.shape[0]

    def body(h_ref, wu_ref, cw_ref, cb_ref, wd_ref, x1_ref, g_ref, t_ref,
             upb_ref, u_ref, y_ref, dout_ref, dz_ref, dg4_ref, loss_ref, halo):
        first = pl.program_id(0) == 0

        @pl.when(first)
        def _():
            loss_ref[...] = jnp.zeros_like(loss_ref)
            dg4_ref[...] = jnp.zeros_like(dg4_ref)

        h = h_ref[...]
        z = jnp.zeros((tm, D_MODEL), F32)
        for s in range(FF_PAIRS):
            u = []
            for sh in (s, s + FF_PAIRS):
                cur = _dot(h, wu_ref[sh])
                upb_ref[sh] = cur.astype(BF)
                before = jnp.where(first, 0.0, halo[sh])
                halo[sh] = cur[tm - 8:tm, :]
                u_c = (cw_ref[sh, pl.ds(0, 1), :] * _shift_down(cur, 2, before)
                       + cw_ref[sh, pl.ds(1, 1), :] * _shift_down(cur, 1, before)
                       + cw_ref[sh, pl.ds(2, 1), :] * cur + cb_ref[sh])
                u_ref[sh] = u_c
                u.append(u_c)
            gelu, _ = _gelu_parts(u[0])
            y = (gelu * u[1]).astype(BF)
            y_ref[s] = y
            z = z + _dot(y, wd_ref[s * FF_SHARD:(s + 1) * FF_SHARD, :])
        r4, n4 = _rms_stats(z)
        err = x1_ref[...] + n4 * g_ref[...] - t_ref[...]
        dout = err * (1.0 / D_MODEL)
        dout_ref[...] = dout
        loss_ref[...] += 0.5 * jnp.sum(jnp.mean(err * err, axis=-1, keepdims=True), axis=0, keepdims=True)
        dg4_ref[...] += jnp.sum(dout * n4, axis=0, keepdims=True)
        dz_ref[...] = _rms_bwd(n4, r4, dout * g_ref[...]).astype(BF)

    sh8 = pl.BlockSpec((N_DEV, tm, FF_SHARD), lambda i: (0, i, 0))
    return _pcall(
        body, name="ffn_fwd", grid=(T // tm,),
        in_specs=[_rows(tm, D_MODEL), _vmem_full(), _const((N_DEV, 3, FF_SHARD)), _const((N_DEV, 1, FF_SHARD)),
                  _vmem_full(), _rows(tm, D_MODEL), _const((1, D_MODEL)), _rows(tm, D_MODEL)],
        out_specs=[sh8, sh8, pl.BlockSpec((FF_PAIRS, tm, FF_SHARD), lambda i: (0, i, 0)), _rows(tm, D_MODEL),
                   _rows(tm, D_MODEL), _const((1, D_MODEL)), _const((8, 128))],
        out_shape=[jax.ShapeDtypeStruct((N_DEV, T, FF_SHARD), BF), jax.ShapeDtypeStruct((N_DEV, T, FF_SHARD), F32),
                   jax.ShapeDtypeStruct((FF_PAIRS, T, FF_SHARD), BF), jax.ShapeDtypeStruct((T, D_MODEL), F32),
                   jax.ShapeDtypeStruct((T, D_MODEL), BF), jax.ShapeDtypeStruct((1, D_MODEL), F32),
                   jax.ShapeDtypeStruct((8, 128), F32)],
        scratch_shapes=[pltpu.VMEM((N_DEV, 8, FF_SHARD), F32)],
        compiler_params=_params(("arbitrary",)),
    )(h2, w_up8, conv_w8, conv_b8, w_down, x1, g4, target)


def _ffn_bwd_a(dz, h2, w_down4, u8, up_b8, y4, conv_w8, tm):
    T = dz.shape[0]
    nt = T // tm

    def body(dz_ref, h_ref, wd_ref, ug_ref, uv_ref, upg_ref, upv_ref, y_ref, cwg_ref, cwv_ref,
             dupg_ref, dupv_ref, dcbg_ref, dcbv_ref, dcwg_ref, dcwv_ref, gwug_ref, gwuv_ref, gwd_ref, carry):
        @pl.when(pl.program_id(1) == 0)
        def _():
            for ref in (dcbg_ref, dcbv_ref, dcwg_ref, dcwv_ref, gwug_ref, gwuv_ref, gwd_ref, carry):
                ref[...] = jnp.zeros_like(ref)

        dz = dz_ref[...]
        h = h_ref[...]
        dy = _dot_nt(dz, wd_ref[0])
        gwd_ref[0] += _dot_tn(y_ref[0], dz)
        gelu, dgelu = _gelu_parts(ug_ref[0])
        parts = ((0, dy * uv_ref[0] * dgelu, upg_ref, cwg_ref, dupg_ref, dcbg_ref, dcwg_ref, gwug_ref),
                 (1, dy * gelu, upv_ref, cwv_ref, dupv_ref, dcbv_ref, dcwv_ref, gwuv_ref))
        for part, d, up_ref, cw_ref, dup_ref, dcb_ref, dcw_ref, gwu_ref in parts:
            after = carry[part]
            d1 = _shift_up(d, 1, after)
            d2 = _shift_up(d, 2, after)
            carry[part] = d[0:8, :]
            upc = up_ref[0].astype(F32)
            dcb_ref[0] += jnp.sum(d, axis=0, keepdims=True)
            dcw_ref[0, pl.ds(2, 1), :] += jnp.sum(d * upc, axis=0, keepdims=True)
            dcw_ref[0, pl.ds(1, 1), :] += jnp.sum(d1 * upc, axis=0, keepdims=True)
            dcw_ref[0, pl.ds(0, 1), :] += jnp.sum(d2 * upc, axis=0, keepdims=True)
            dup = (cw_ref[0, pl.ds(2, 1), :] * d + cw_ref[0, pl.ds(1, 1), :] * d1
                   + cw_ref[0, pl.ds(0, 1), :] * d2).astype(BF)
            dup_ref[0] = dup
            gwu_ref[0] += _dot_tn(h, dup)

    rev = pl.BlockSpec((tm, D_MODEL), lambda s, i: (nt - 1 - i, 0))
    gate = lambda r: pl.BlockSpec((1, r, FF_SHARD), lambda s, i: (s, nt - 1 - i if r == tm else 0, 0))
    val = lambda r: pl.BlockSpec((1, r, FF_SHARD), lambda s, i: (s + FF_PAIRS, nt - 1 - i if r == tm else 0, 0))
    acc = lambda r, w: pl.BlockSpec((1, r, w), lambda s, i: (s, 0, 0))
    half = lambda r, dt: jax.ShapeDtypeStruct((FF_PAIRS, r, FF_SHARD), dt)
    return _pcall(
        body, name="ffn_bwd_a", grid=(FF_PAIRS, nt),
        in_specs=[rev, rev, acc(FF_SHARD, D_MODEL), gate(tm), val(tm), gate(tm), val(tm), gate(tm), gate(3), val(3)],
        out_specs=[gate(tm), gate(tm), acc(1, FF_SHARD), acc(1, FF_SHARD), acc(3, FF_SHARD), acc(3, FF_SHARD),
                   acc(D_MODEL, FF_SHARD), acc(D_MODEL, FF_SHARD), acc(FF_SHARD, D_MODEL)],
        out_shape=[half(T, BF), half(T, BF), half(1, F32), half(1, F32), half(3, F32), half(3, F32),
                   half(D_MODEL, F32), half(D_MODEL, F32), jax.ShapeDtypeStruct((FF_PAIRS, FF_SHARD, D_MODEL), F32)],
        scratch_shapes=[pltpu.VMEM((2, 8, FF_SHARD), F32)],
        compiler_params=_params(("arbitrary", "arbitrary")),
    )(dz, h2, w_down4, u8, u8, up_b8, up_b8, y4, conv_w8, conv_w8)


def _ffn_bwd_b(dup_g, dup_v, w_up8, x1, dout, g3, mixed, g2, w_out, tm, gwu_g, gwu_v):
    T = x1.shape[0]
    nt = T // tm

    def body(dupg_ref, dupv_ref, wup_ref, x1_ref, dout_ref, g3_ref, mixed_ref, g2_ref, wout_ref, gwug_ref, gwuv_ref,
             dx1_ref, dmixed_ref, da_ref, dr_ref, dg3_ref, dg2_ref, pup_ref, *sems):
        ex = _Exchange([], [((gwug_ref, gwuv_ref), pup_ref)], *sems)

        @pl.when(pl.program_id(0) == 0)
        def _():
            ex.start()
            dg3_ref[...] = jnp.zeros_like(dg3_ref)
            dg2_ref[...] = jnp.zeros_like(dg2_ref)

        dh2 = jnp.zeros((tm, D_MODEL), F32)
        for s in range(FF_PAIRS):
            dh2 = dh2 + _dot_nt(dupg_ref[s], wup_ref[s]) + _dot_nt(dupv_ref[s], wup_ref[s + FF_PAIRS])
        r3, n3 = _rms_stats(x1_ref[...])
        dg3_ref[...] += jnp.sum(dh2 * n3, axis=0, keepdims=True)
        dx1 = dout_ref[...] + _rms_bwd(n3, r3, dh2 * g3_ref[...])
        dx1_ref[...] = dx1
        r2, n2 = _rms_stats(mixed_ref[...])
        dg2_ref[...] += jnp.sum(dx1 * n2, axis=0, keepdims=True)
        dmixed = _rms_bwd(n2, r2, dx1 * g2_ref[...]).astype(BF)
        dmixed_ref[...] = dmixed
        da_ref[...] = _dot_nt(dmixed, wout_ref[:ATTN_W, :])
        dr_ref[...] = _dot_nt(dmixed, wout_ref[ATTN_W:, :])
        pl.when(pl.program_id(0) == nt - 1)(ex.wait)

    half = pl.BlockSpec((FF_PAIRS, tm, FF_SHARD), lambda i: (0, i, 0))
    outs = _pcall(
        body, name="ffn_bwd_b", grid=(nt,),
        in_specs=[half, half, _vmem_full(), _rows(tm, D_MODEL), _rows(tm, D_MODEL), _const((1, D_MODEL)),
                  _rows(tm, D_MODEL), _const((1, D_MODEL)), _vmem_full(), ANY_SPEC, ANY_SPEC],
        out_specs=[_rows(tm, D_MODEL), _rows(tm, D_MODEL), _rows(tm, ATTN_W), _rows(tm, RET_W),
                   _const((1, D_MODEL)), _const((1, D_MODEL)), ANY_SPEC],
        out_shape=[jax.ShapeDtypeStruct((T, D_MODEL), F32), jax.ShapeDtypeStruct((T, D_MODEL), BF),
                   jax.ShapeDtypeStruct((T, ATTN_W), F32), jax.ShapeDtypeStruct((T, RET_W), F32),
                   jax.ShapeDtypeStruct((1, D_MODEL), F32), jax.ShapeDtypeStruct((1, D_MODEL), F32),
                   jax.ShapeDtypeStruct((N_DEV, D_MODEL, FF_SHARD), F32)],
        scratch_shapes=_Exchange.scratch(1),
        compiler_params=_params(("arbitrary",)),
    )(dup_g, dup_v, w_up8, x1, dout, g3, mixed, g2, w_out, gwu_g, gwu_v)
    return outs[:6], outs[6]


def _ret_bwd(decay, qr, kr, vr, gr, o, states, dr, d_intra, d_intra_t, xi_b, zeta_b, cos, sin_s, swaps, ncs):
    T = qr.shape[0]
    nc = T // BLOCK
    H, C = N_RET_HEADS, RET_HEAD_DIM
    ns = len(swaps)

    def body(decay_ref, q_ref, k_ref, v_ref, g_ref, o_ref, s_ref, dr_ref, d_ref, dt_ref, xi_ref, zeta_ref,
             cos_ref, sin_ref, *rest):
        ex = _exchange_of(rest[:ns], rest[ns + 1:2 * ns + 1], rest[2 * ns + 2:], 0)
        dret_ref, gstate = rest[ns], rest[2 * ns + 1]

        @pl.when(pl.program_id(0) == 0)
        def _():
            ex.start()
            gstate[...] = jnp.zeros_like(gstate)

        even = lax.broadcasted_iota(jnp.int32, (BLOCK, C), 1) % 2 == 0
        for h in range(H):
            cs = slice(h * C, (h + 1) * C)
            gst = gstate[h]
            for b in reversed(range(ncs)):
                rows = slice(b * BLOCK, (b + 1) * BLOCK)
                cos_t, sin_t = cos_ref[rows, :], sin_ref[rows, :]
                q, k, v = q_ref[rows, cs], k_ref[rows, cs], v_ref[rows, cs]
                g, o_h, dr_h = g_ref[rows, cs], o_ref[rows, cs], dr_ref[rows, cs]
                mu = jnp.mean(o_h, axis=-1, keepdims=True)
                oc = o_h - mu
                rs = lax.rsqrt(jnp.mean(oc * oc, axis=-1, keepdims=True) + GN_EPS)
                on = oc * rs
                sg = jax.nn.sigmoid(g)
                dg = dr_h * on * (sg * (1.0 + g * (1.0 - sg)))
                don = dr_h * (g * sg)
                do = rs * (don - jnp.mean(don, axis=-1, keepdims=True)
                           - on * jnp.mean(don * on, axis=-1, keepdims=True))
                do_b = do.astype(BF)
                dox_b = (do * xi_ref[h]).astype(BF)
                gst_b = gst.astype(BF)
                kz = (k.astype(F32) * zeta_ref[h]).astype(BF)
                da_b = (_dot_nt(do_b, v) * d_ref[h]).astype(BF)
                dat_b = (_dot_nt(v, do_b) * dt_ref[h]).astype(BF)
                mt_b = (_dot_nt(k, q) * dt_ref[h]).astype(BF)
                dq = _dot(da_b, k) + _dot_nt(dox_b, s_ref[b, h])
                dk = _dot(dat_b, q) + _dot_nt(v, gst_b) * zeta_ref[h]
                dv = _dot(mt_b, do_b) + _dot(kz, gst_b)
                gst = decay_ref[h] * gst + _dot_tn(q, dox_b)
                dq = dq * cos_t - _rot(dq, even) * sin_t
                dk = (dk * cos_t - _rot(dk, even) * sin_t) * RET_K_SCALE
                dret_ref[rows, h * C:(h + 1) * C] = dq.astype(BF)
                dret_ref[rows, RET_W + h * C:RET_W + (h + 1) * C] = dk.astype(BF)
                dret_ref[rows, 2 * RET_W + h * C:2 * RET_W + (h + 1) * C] = dv.astype(BF)
                dret_ref[rows, 3 * RET_W + h * C:3 * RET_W + (h + 1) * C] = dg.astype(BF)
            gstate[h] = gst
        pl.when(pl.program_id(0) == steps - 1)(ex.wait)

    steps = nc // ncs
    rev = lambda w: pl.BlockSpec((BLOCK * ncs, w), lambda n: (steps - 1 - n, 0))
    tab = pl.BlockSpec((H, C, C), lambda n: (0, 0, 0))
    outs = _pcall(
        body, name="ret_bwd", grid=(steps,),
        in_specs=[_smem_full(), rev(RET_W), rev(RET_W), rev(RET_W), rev(RET_W), rev(RET_W),
                  pl.BlockSpec((ncs, H, C, C), lambda n: (steps - 1 - n, 0, 0, 0)), rev(RET_W), tab, tab, tab, tab,
                  rev(C), rev(C)] + [ANY_SPEC] * ns,
        out_specs=[rev(4 * RET_W)] + [ANY_SPEC] * ns,
        out_shape=[jax.ShapeDtypeStruct((T, 4 * RET_W), BF)] + _exchange_shapes([], swaps),
        scratch_shapes=[pltpu.VMEM((H, C, C), F32)] + _Exchange.scratch(ns),
        compiler_params=_params(("arbitrary",)),
    )(decay, qr, kr, vr, gr, o, states, dr, d_intra, d_intra_t, xi_b, zeta_b, cos, sin_s, *swaps)
    return outs[0], outs[1:]


def _attn_bwd_dq(sinks, qa, ka, va, da, lse, nbs, swaps):
    T = qa.shape[0]
    steps = T // (BLOCK * nbs)
    R = nbs * N_ATTN_HEADS * BLOCK
    ns = len(swaps)

    def body(sink_ref, q_ref, kc_ref, kp_ref, vc_ref, vp_ref, da_ref, lse_ref, *rest):
        ex = _exchange_of(rest[:ns], rest[ns + 3:2 * ns + 3], rest[2 * ns + 3:], 0)
        dq_ref, deltat_ref, dsink_ref = rest[ns:ns + 3]
        n = pl.program_id(0)

        @pl.when(n == 0)
        def _():
            ex.start()
            dsink_ref[...] = jnp.zeros_like(dsink_ref)

        kcat = _block_variants(kp_ref, kc_ref, nbs)
        vcat = _block_variants(vp_ref, vc_ref, nbs)
        tri1 = _tri(BLOCK)
        lane = lax.broadcasted_iota(jnp.int32, (BLOCK, BLOCK), 1)
        s_tiles, dp_tiles, lse_cols = [], [], []
        for b in range(nbs):
            rows = slice(b * BLOCK, (b + 1) * BLOCK)
            lse_tile = lse_ref[rows, :]
            for pair in range(N_ATTN_HEADS // 2):
                qp = _scaled(q_ref[rows, pair * 128:(pair + 1) * 128])
                dop = da_ref[rows, pair * 128:(pair + 1) * 128].astype(BF)
                for e in range(2):
                    s = _fold(_dot_nt(qp, kcat[b][pair // 2][e]), tri1, True)
                    if b == 0:
                        s = jnp.where(tri1 & (n == 0), MASK_VALUE, s)
                    s_tiles.append(s)
                    dp_tiles.append(_fold(_dot_nt(dop, vcat[b][pair // 2][e]), tri1, True))
                    lse_cols.append(jnp.sum(jnp.where(lane == 2 * pair + e, lse_tile, 0.0), axis=-1, keepdims=True))
        lse_c = jnp.concatenate(lse_cols, axis=0)
        p = jnp.exp(jnp.concatenate(s_tiles, axis=0) - lse_c)
        dp = jnp.concatenate(dp_tiles, axis=0)
        delta = jnp.sum(p * dp, axis=-1, keepdims=True)
        ds2 = _unfold(p * (dp - delta), _tri(R), True)
        for b in range(nbs):
            for pair in range(N_ATTN_HEADS // 2):
                r0 = (b * N_ATTN_HEADS + 2 * pair) * BLOCK
                acc = (_dot(ds2[r0:r0 + BLOCK, :], kcat[b][pair // 2][0])
                       + _dot(ds2[r0 + BLOCK:r0 + 2 * BLOCK, :], kcat[b][pair // 2][1]))
                dq_ref[b * BLOCK:(b + 1) * BLOCK, pair * 128:(pair + 1) * 128] = (acc * ATTN_SCALE).astype(BF)
        for b, t in enumerate(_head_cols(delta, nbs)):
            deltat_ref[:, b * BLOCK:(b + 1) * BLOCK] = t.T[:N_ATTN_HEADS, :]
        sink = jnp.concatenate([jnp.full((BLOCK, 1), sink_ref[head], F32)
                                for _ in range(nbs) for head in range(N_ATTN_HEADS)], axis=0)
        ds_sink = -jnp.exp(sink - lse_c) * delta
        row8 = lax.broadcasted_iota(jnp.int32, (N_ATTN_HEADS, BLOCK), 0)
        dsink = jnp.zeros((N_ATTN_HEADS, BLOCK), F32)
        for b in range(nbs):
            for head in range(N_ATTN_HEADS):
                r0 = (b * N_ATTN_HEADS + head) * BLOCK
                dsink = dsink + jnp.where(row8 == head, jnp.sum(ds_sink[r0:r0 + BLOCK, :], axis=0, keepdims=True), 0.0)
        dsink_ref[...] += dsink
        pl.when(n == steps - 1)(ex.wait)

    cur = lambda w: pl.BlockSpec((BLOCK * nbs, w), lambda n: (n, 0))
    prev = lambda w: pl.BlockSpec((BLOCK, w), lambda n: (jnp.maximum(n * nbs - 1, 0), 0))
    outs = _pcall(
        body, name="attn_bwd_dq", grid=(steps,),
        in_specs=[_smem_full(), cur(ATTN_W), cur(KV_W), prev(KV_W), cur(KV_W), prev(KV_W), cur(ATTN_W), cur(BLOCK)]
        + [ANY_SPEC] * ns,
        out_specs=[cur(ATTN_W), pl.BlockSpec((N_ATTN_HEADS, BLOCK * nbs), lambda n: (0, n)),
                   _const((N_ATTN_HEADS, BLOCK))] + [ANY_SPEC] * ns,
        out_shape=[jax.ShapeDtypeStruct((T, ATTN_W), BF), jax.ShapeDtypeStruct((N_ATTN_HEADS, T), F32),
                   jax.ShapeDtypeStruct((N_ATTN_HEADS, BLOCK), F32)] + _exchange_shapes([], swaps),
        scratch_shapes=_Exchange.scratch(ns),
        compiler_params=_params(("arbitrary",)),
    )(sinks, qa, ka, ka, va, va, da, lse, *swaps)
    return outs[:3], outs[3:]


def _attn_bwd_dkv(qa, ka, va, da, lse_t, delta_t, nbs):
    T = qa.shape[0]
    nb = T // BLOCK
    steps = nb // nbs
    R = nbs * N_ATTN_HEADS * BLOCK

    def body(qc_ref, qn_ref, dac_ref, dan_ref, k_ref, v_ref, lc_ref, ln_ref, dc_ref, dn_ref, dk_ref, dv_ref):
        n = pl.program_id(0)
        tri1 = _tri(BLOCK, True)
        lo = lax.broadcasted_iota(jnp.int32, (BLOCK, 128), 1) < HEAD_DIM
        kv = [_kv_variants(k_ref[b * BLOCK:(b + 1) * BLOCK, :]) for b in range(nbs)]
        vv = [_kv_variants(v_ref[b * BLOCK:(b + 1) * BLOCK, :]) for b in range(nbs)]
        qcat, docat = [], []
        s_tiles, dp_tiles, lse_tiles, delta_tiles = [], [], [], []
        for b in range(nbs):
            rows = slice(b * BLOCK, (b + 1) * BLOCK)
            nrows = slice((b + 1) * BLOCK, (b + 2) * BLOCK)
            inside = b < nbs - 1
            for pair in range(N_ATTN_HEADS // 2):
                ps = slice(pair * 128, (pair + 1) * 128)
                q2 = _scaled(jnp.concatenate([qc_ref[rows, ps], qc_ref[nrows, ps] if inside else qn_ref[:, ps]], axis=0))
                do2 = jnp.concatenate([dac_ref[rows, ps], dac_ref[nrows, ps] if inside else dan_ref[:, ps]],
                                      axis=0).astype(BF)
                qcat.append(q2)
                docat.append(do2)
                for e in range(2):
                    one = pl.ds(2 * pair + e, 1)
                    s = _fold(_dot_nt(kv[b][pair // 2][e], q2), tri1, False)
                    if not inside:
                        s = jnp.where(tri1 & (n == steps - 1), MASK_VALUE, s)
                    s_tiles.append(s)
                    dp_tiles.append(_fold(_dot_nt(vv[b][pair // 2][e], do2), tri1, False))
                    lse_tiles.append(jnp.where(tri1, lc_ref[one, nrows] if inside else ln_ref[one, :], lc_ref[one, rows]))
                    delta_tiles.append(jnp.where(tri1, dc_ref[one, nrows] if inside else dn_ref[one, :],
                                                 dc_ref[one, rows]))
        pt = jnp.exp(jnp.concatenate(s_tiles, axis=0) - jnp.concatenate(lse_tiles, axis=0))
        dst = pt * (jnp.concatenate(dp_tiles, axis=0) - jnp.concatenate(delta_tiles, axis=0))
        tri = _tri(R, True)
        pt2 = _unfold(pt, tri, False)
        dst2 = _unfold(dst, tri, False)
        for b in range(nbs):
            dk = jnp.zeros((BLOCK, 128), F32)
            dv = jnp.zeros((BLOCK, 128), F32)
            for pair in range(N_ATTN_HEADS // 2):
                h = pair // 2
                for e in range(2):
                    r0 = (b * N_ATTN_HEADS + 2 * pair + e) * BLOCK
                    half = lo if e == 0 else jnp.logical_not(lo)
                    dv_e = jnp.where(half, _dot(pt2[r0:r0 + BLOCK, :], docat[b * 4 + pair]), 0.0)
                    dk_e = jnp.where(half, _dot(dst2[r0:r0 + BLOCK, :], qcat[b * 4 + pair]), 0.0)
                    if e != h:
                        dv_e = pltpu.roll(dv_e, HEAD_DIM, 1)
                        dk_e = pltpu.roll(dk_e, HEAD_DIM, 1)
                    dv = dv + dv_e
                    dk = dk + dk_e
            dk_ref[b * BLOCK:(b + 1) * BLOCK, :] = dk.astype(BF)
            dv_ref[b * BLOCK:(b + 1) * BLOCK, :] = dv.astype(BF)

    cur = lambda w: pl.BlockSpec((BLOCK * nbs, w), lambda n: (n, 0))
    nxt = lambda w: pl.BlockSpec((BLOCK, w), lambda n: (jnp.minimum((n + 1) * nbs, nb - 1), 0))
    tcur = pl.BlockSpec((N_ATTN_HEADS, BLOCK * nbs), lambda n: (0, n))
    tnxt = pl.BlockSpec((N_ATTN_HEADS, BLOCK), lambda n: (0, jnp.minimum((n + 1) * nbs, nb - 1)))
    return _pcall(
        body, name="attn_bwd_dkv", grid=(steps,),
        in_specs=[cur(ATTN_W), nxt(ATTN_W), cur(ATTN_W), nxt(ATTN_W), cur(KV_W), cur(KV_W), tcur, tnxt, tcur, tnxt],
        out_specs=[cur(KV_W), cur(KV_W)],
        out_shape=[jax.ShapeDtypeStruct((T, KV_W), BF), jax.ShapeDtypeStruct((T, KV_W), BF)],
        compiler_params=_params(("parallel",)),
    )(qa, qa, da, da, ka, va, lse_t, lse_t, delta_t, delta_t)


def _in_proj_bwd(dqa, dka, dva, dret, w_in, x, g1, dx1, tm):
    T = x.shape[0]

    def body(dqa_ref, dka_ref, dva_ref, dret_ref, w_ref, x_ref, g_ref, dx1_ref, dx_ref, dg1_ref):
        @pl.when(pl.program_id(0) == 0)
        def _():
            dg1_ref[...] = jnp.zeros_like(dg1_ref)

        dh = (_dot_nt(dqa_ref[...], w_ref[:, QA0:QA0 + ATTN_W]) + _dot_nt(dka_ref[...], w_ref[:, KA0:KA0 + KV_W])
              + _dot_nt(dva_ref[...], w_ref[:, VA0:VA0 + KV_W]) + _dot_nt(dret_ref[...], w_ref[:, QR0:IN_W]))
        r, n = _rms_stats(x_ref[...])
        dg1_ref[...] += jnp.sum(dh * n, axis=0, keepdims=True)
        dx_ref[...] = dx1_ref[...] + _rms_bwd(n, r, dh * g_ref[...])

    return _pcall(
        body, name="in_proj_bwd", grid=(T // tm,),
        in_specs=[_rows(tm, ATTN_W), _rows(tm, KV_W), _rows(tm, KV_W), _rows(tm, 4 * RET_W), _vmem_full(),
                  _rows(tm, D_MODEL), _const((1, D_MODEL)), _rows(tm, D_MODEL)],
        out_specs=[_rows(tm, D_MODEL), _const((1, D_MODEL))],
        out_shape=[jax.ShapeDtypeStruct((T, D_MODEL), F32), jax.ShapeDtypeStruct((1, D_MODEL), F32)],
        compiler_params=_params(("arbitrary",)),
    )(dqa, dka, dva, dret, w_in, x, g1, dx1)


def _wgrad(a_list, b_list, tk, name):
    T = a_list[0].shape[0]
    na, nbb = len(a_list), len(b_list)
    m_sizes = [a.shape[1] for a in a_list]
    n_sizes = [b.shape[1] for b in b_list]
    M, N = sum(m_sizes), sum(n_sizes)
    nk = T // tk
    chunk = 512

    def body(*refs):
        a_refs, b_refs = refs[:na], refs[na:na + nbb]
        out_ref, acc = refs[na + nbb], refs[na + nbb + 1]
        k = pl.program_id(0)

        @pl.when(k == 0)
        def _():
            acc[...] = jnp.zeros_like(acc)

        r0 = 0
        for ai in range(na):
            a = a_refs[ai][...]
            c0 = 0
            for bi in range(nbb):
                for s in range(0, n_sizes[bi], chunk):
                    w = min(chunk, n_sizes[bi] - s)
                    acc[r0:r0 + m_sizes[ai], c0 + s:c0 + s + w] += _dot_tn(a, b_refs[bi][:, s:s + w])
                c0 += n_sizes[bi]
            r0 += m_sizes[ai]

        @pl.when(k == nk - 1)
        def _():
            pltpu.sync_copy(acc, out_ref)

    return _pcall(
        body, name=name, grid=(nk,),
        in_specs=[_rows(tk, w) for w in m_sizes + n_sizes],
        out_specs=pl.BlockSpec(memory_space=pl.ANY),
        out_shape=jax.ShapeDtypeStruct((M, N), F32),
        scratch_shapes=[pltpu.VMEM((M, N), F32)],
        compiler_params=_params(("arbitrary",)),
    )(*a_list, *b_list)


def _adamw_math(w, g, m, v):
    m = ADAM_B1 * m + (1.0 - ADAM_B1) * g
    v = ADAM_B2 * v + (1.0 - ADAM_B2) * (g * g)
    m_hat = m / (1.0 - ADAM_B1 ** ADAM_STEP)
    v_hat = v / (1.0 - ADAM_B2 ** ADAM_STEP)
    delta = -ADAM_LR * (m_hat / (jnp.sqrt(v_hat) + ADAM_EPS) + ADAM_WD * w)
    return delta, m, v


def _sum_parts(parts_ref):
    g = parts_ref[0]
    for i in range(1, N_DEV):
        g = g + parts_ref[i]
    return g


def _adamw_shard(parts, w, m, v, tr, name):
    R, C = w.shape

    def body(p_ref, w_ref, m_ref, v_ref, g_ref, d_ref, nm_ref, nv_ref):
        g = _sum_parts(p_ref)
        g_ref[...] = g
        d_ref[...], nm_ref[...], nv_ref[...] = _adamw_math(w_ref[...], g, m_ref[...], v_ref[...])

    blk = pl.BlockSpec((tr, C), lambda i: (i, 0))
    return _pcall(
        body, name=name, grid=(R // tr,),
        in_specs=[pl.BlockSpec((N_DEV, tr, C), lambda i: (0, i, 0)), blk, blk, blk],
        out_specs=[blk] * 4,
        out_shape=[jax.ShapeDtypeStruct((R, C), F32)] * 4,
        compiler_params=_params(("parallel",)),
    )(parts, w, m, v)


def _sum_small(parts):
    def body(p_ref, g_ref):
        g_ref[...] = _sum_parts(p_ref)

    return _pcall(body, name="sum_small", out_shape=jax.ShapeDtypeStruct(parts.shape[1:], F32),
                  in_specs=[_vmem_full()], out_specs=_vmem_full())(parts)


def _adamw_small(g, w, m, v, name):
    def body(g_ref, w_ref, m_ref, v_ref, d_ref, nm_ref, nv_ref):
        d_ref[...], nm_ref[...], nv_ref[...] = _adamw_math(w_ref[...], g_ref[...], m_ref[...], v_ref[...])

    return _pcall(body, name=name, out_shape=[jax.ShapeDtypeStruct(w.shape, F32)] * 3,
                  in_specs=[_vmem_full()] * 4, out_specs=[_vmem_full()] * 3)(g, w, m, v)


def _tables(T):
    h, c = N_RET_HEADS, BLOCK
    pos = jnp.arange(T, dtype=F32)
    angle = 1.0 / jnp.power(10000.0, jnp.linspace(0.0, 1.0, RET_HEAD_DIM // 2, dtype=F32))
    angle = jnp.repeat(angle, 2)
    sin = jnp.sin(pos[:, None] * angle[None])
    cos = jnp.cos(pos[:, None] * angle[None])
    even = (jnp.arange(RET_HEAD_DIM) % 2 == 0)[None, :]
    sin_s = jnp.where(even, -sin, sin)
    log_gamma = jnp.log(1.0 - jnp.power(2.0, -5.0 - jnp.arange(h, dtype=F32)))
    idx = jnp.arange(c, dtype=F32)
    rel = idx[:, None] - idx[None, :]
    d_intra = jnp.where(rel[None] >= 0, jnp.exp(log_gamma[:, None, None] * jnp.maximum(rel, 0.0)[None]), 0.0)
    xi = jnp.exp(log_gamma[None, :] * (idx[:, None] + 1.0))
    zeta = jnp.exp(log_gamma[None, :] * (c - 1.0 - idx[:, None]))
    decay = jnp.exp(log_gamma * c)
    xi_b = jnp.broadcast_to(xi.T[:, :, None], (h, c, RET_HEAD_DIM))
    zeta_b = jnp.broadcast_to(zeta.T[:, :, None], (h, c, RET_HEAD_DIM))
    return cos, sin_s, d_intra, jnp.swapaxes(d_intra, 1, 2), xi_b, zeta_b, decay


def _to_shards(full, cols):
    r = full.shape[0]
    return jnp.swapaxes(full.reshape(r, N_DEV, cols), 0, 1)


def _from_shards(sh):
    n, r, cols = sh.shape
    return jnp.swapaxes(sh, 0, 1).reshape(r, n * cols)


SMALL_ROWS = 216


def _pack_small(gains, conv_b, conv_w, sinks):
    parts = [g.reshape(8, 128) for g in gains] + [conv_b.reshape(44, 128), conv_w.reshape(132, 128),
                                                  jnp.pad(sinks.reshape(1, 8), ((0, 0), (0, 120)))]
    packed = jnp.concatenate(parts, axis=0)
    return jnp.pad(packed, ((0, SMALL_ROWS - packed.shape[0]), (0, 0)))


def kernel(x, mix_pre_norm, w_in, attn_sinks, w_out, mix_post_norm, ffn_pre_norm, w_up, conv_w, conv_b, w_down, ffn_post_norm, loss_target, m_mix_pre_norm, m_w_in, m_attn_sinks, m_w_out, m_mix_post_norm, m_ffn_pre_norm, m_w_up, m_conv_w, m_conv_b, m_w_down, m_ffn_post_norm, v_mix_pre_norm, v_w_in, v_attn_sinks, v_w_out, v_mix_post_norm, v_ffn_pre_norm, v_w_up, v_conv_w, v_conv_b, v_w_down, v_ffn_post_norm):
    T = x.shape[1]
    tm = min(512, T)
    tm_ff = min(256, T)
    nbs = min(4, T // BLOCK)
    x2 = x.reshape(T, D_MODEL)
    target = loss_target.reshape(T, D_MODEL)
    me = 4 * lax.axis_index("x") + 2 * lax.axis_index("y") + lax.axis_index("c")

    g_in, g_cw = _exchange_call([w_in[0].astype(BF), conv_w[0]], [], "gather_w_in")
    w_in_f = _from_shards(g_in)
    cos, sin_s, d_intra, d_intra_t, xi_b, zeta_b, decay = _tables(T)
    sinks = attn_sinks.reshape(N_ATTN_HEADS)

    (h1, qa, ka, va, qr, kr, vr, gr), (w_up8, g_out) = _in_proj(
        x2, mix_pre_norm, w_in_f, cos, sin_s, tm, [w_up[0].astype(BF), w_out[0].astype(BF)])
    w_out_f = g_out.reshape(D_MODEL, D_MODEL)
    (a, lse, lse_t), (g_down,) = _attn_fwd(sinks, qa, ka, va, nbs, [w_down[0].astype(BF)])
    w_down_f = g_down.reshape(D_FF, D_MODEL)
    o, states, r = _ret_fwd(decay, qr, kr, vr, gr, d_intra, xi_b, zeta_b, nbs)
    mixed, x1, h2 = _out_proj(a, r, w_out_f, x2, mix_post_norm, ffn_pre_norm, tm)
    up_b8, u8, y4, dout, dz, dg4, loss_acc = _ffn_fwd(
        h2, w_up8, g_cw, conv_b.reshape(N_DEV, 1, FF_SHARD), w_down_f, x1, ffn_post_norm, target, tm_ff)
    loss = lax.psum(loss_acc[0, 0], ("x", "y", "c"))

    dup_g, dup_v, dcb_g, dcb_v, dcw_g, dcw_v, gwu_g, gwu_v, gw_down4 = _ffn_bwd_a(
        dz, h2, w_down_f.reshape(FF_PAIRS, FF_SHARD, D_MODEL), u8, up_b8, y4, g_cw, tm)
    dcb = jnp.concatenate([dcb_g, dcb_v], axis=0).reshape(1, 2 * D_FF)
    dcw = _from_shards(jnp.concatenate([dcw_g, dcw_v], axis=0))
    gw_down = gw_down4.reshape(D_FF, D_MODEL)
    (dx1, dmixed, da, dr, dg3, dg2), p_up = _ffn_bwd_b(
        dup_g, dup_v, w_up8, x1, dout, ffn_pre_norm, mixed, mix_post_norm, w_out_f, tm_ff, gwu_g, gwu_v)
    gw_out = _wgrad([a, r], [dmixed], tm, "wgrad_out")
    dret, (p_down,) = _ret_bwd(decay, qr, kr, vr, gr, o, states, dr, d_intra, d_intra_t, xi_b, zeta_b, cos, sin_s,
                               [gw_down.reshape(N_DEV, D_FF // N_DEV, D_MODEL)], nbs)
    (dqa, delta_t, dsink), (p_out,) = _attn_bwd_dq(sinks, qa, ka, va, da, lse, nbs,
                                                   [gw_out.reshape(N_DEV, D_MODEL // N_DEV, D_MODEL)])
    dka, dva = _attn_bwd_dkv(qa, ka, va, da, lse_t, delta_t, nbs)
    grad_x, dg1 = _in_proj_bwd(dqa, dka, dva, dret, w_in_f, x2, mix_pre_norm, dx1, tm)
    gw_in = _wgrad([h1], [dqa, dka, dva, dret], tm, "wgrad_in")

    small = _pack_small([dg1, dg2, dg3, dg4], dcb, dcw, dsink[:, 0])
    small_all, p_in = _exchange_call([small], [_to_shards(gw_in, IN_W // N_DEV)], "exchange_last")
    g_small = _sum_small(small_all)

    g_w_in, d_w_in, nm_w_in, nv_w_in = _adamw_shard(p_in, w_in[0], m_w_in[0], v_w_in[0], 256, "adamw_in")
    g_w_up, d_w_up, nm_w_up, nv_w_up = _adamw_shard(p_up, w_up[0], m_w_up[0], v_w_up[0], 256, "adamw_up")
    g_w_out, d_w_out, nm_w_out, nv_w_out = _adamw_shard(p_out, w_out[0], m_w_out[0], v_w_out[0], 128, "adamw_out")
    g_w_down, d_w_down, nm_w_down, nv_w_down = _adamw_shard(p_down, w_down[0], m_w_down[0], v_w_down[0], 176,
                                                            "adamw_down")
    gains = [mix_pre_norm, mix_post_norm, ffn_pre_norm, ffn_post_norm]
    m_gains = [m_mix_pre_norm, m_mix_post_norm, m_ffn_pre_norm, m_ffn_post_norm]
    v_gains = [v_mix_pre_norm, v_mix_post_norm, v_ffn_pre_norm, v_ffn_post_norm]
    zeros_cw = jnp.zeros((3, 2 * D_FF), F32)
    w_small = _pack_small(gains, conv_b, zeros_cw, attn_sinks)
    m_small = _pack_small(m_gains, m_conv_b, zeros_cw, m_attn_sinks)
    v_small = _pack_small(v_gains, v_conv_b, zeros_cw, v_attn_sinks)
    d_small, nm_small, nv_small = _adamw_small(g_small, w_small, m_small, v_small, "adamw_small")
    shard_cols = 2 * D_FF // N_DEV
    g_cw = lax.dynamic_slice(g_small[76:208].reshape(3, 2 * D_FF), (0, me * shard_cols), (3, shard_cols))
    d_cw, nm_cw, nv_cw = _adamw_small(g_cw, conv_w[0], m_conv_w[0], v_conv_w[0], "adamw_conv_w")

    def unpack(p):
        gains_o = [p[8 * i:8 * i + 8].reshape(1, D_MODEL) for i in range(4)]
        return gains_o, p[32:76].reshape(1, 2 * D_FF), p[208:209, :N_ATTN_HEADS]

    def leaves(p, w_in_s, w_out_s, w_up_s, cw_s, w_down_s):
        (pre1, post1, pre2, post2), cb, sk = unpack(p)
        return [pre1, w_in_s[None], sk, w_out_s[None], post1, pre2, w_up_s[None], cw_s[None], cb, w_down_s[None],
                post2]

    return (loss, grad_x.reshape(1, T, D_MODEL),
            *leaves(g_small, g_w_in, g_w_out, g_w_up, g_cw, g_w_down),
            *leaves(d_small, d_w_in, d_w_out, d_w_up, d_cw, d_w_down),
            *leaves(nm_small, nm_w_in, nm_w_out, nm_w_up, nm_cw, nm_w_down),
            *leaves(nv_small, nv_w_in, nv_w_out, nv_w_up, nv_cw, nv_w_down))
```

```python
import functools
import math

import jax
import jax.numpy as jnp
from jax import lax
from jax.experimental import pallas as pl
from jax.experimental.pallas import tpu as pltpu

F32 = jnp.float32
BF = jnp.bfloat16

N_DEV = 8
D_MODEL = 1024
HEAD_DIM = 64
ATTN_W = 512
N_ATTN_HEADS = 8
KV_W = 128
BLOCK = 128
RET_W = 512
N_RET_HEADS = 4
RET_HEAD_DIM = 128
IN_W = 2816
D_FF = 2816
RMS_EPS = 1e-6
GN_EPS = 1e-6
MASK_VALUE = -1e30
ATTN_SCALE = HEAD_DIM ** -0.5
RET_K_SCALE = RET_HEAD_DIM ** -0.5
GELU_C = math.sqrt(2.0 / math.pi)
GELU_A = 0.044715

ADAM_LR = 0.001
ADAM_B1 = 0.9
ADAM_B2 = 0.999
ADAM_EPS = 1e-08
ADAM_WD = 0.01
ADAM_STEP = 10

VMEM_LIMIT_BYTES = 56 * 1024 * 1024
FF_SHARD = 2 * D_FF // N_DEV
FF_PAIRS = N_DEV // 2

QA0, KA0, VA0, QR0, KR0, VR0, GR0 = 0, 512, 640, 768, 1280, 1792, 2304

MESH_ID = pl.DeviceIdType.MESH


def _pcall(body, **kw):
    return pl.pallas_call(body, **kw)


def _params(sem=None):
    return pltpu.CompilerParams(dimension_semantics=sem, vmem_limit_bytes=VMEM_LIMIT_BYTES)


def _dot(a, b):
    return jnp.dot(a, b, preferred_element_type=F32)


def _dot_nt(a, b):
    return lax.dot_general(a, b, (((1,), (1,)), ((), ())), preferred_element_type=F32)


def _dot_tn(a, b):
    return lax.dot_general(a, b, (((0,), (0,)), ((), ())), preferred_element_type=F32)


def _vmem_full():
    return pl.BlockSpec(memory_space=pltpu.VMEM)


def _smem_full():
    return pl.BlockSpec(memory_space=pltpu.SMEM)


def _rows(tm, w):
    return pl.BlockSpec((tm, w), lambda i: (i, 0))


def _const(shape):
    return pl.BlockSpec(shape, lambda i: tuple(0 for _ in shape))


def _rms_stats(x):
    r = lax.rsqrt(jnp.mean(x * x, axis=-1, keepdims=True) + RMS_EPS)
    return r, x * r


def _rms_bwd(n, r, dn):
    return r * (dn - n * jnp.mean(dn * n, axis=-1, keepdims=True))


def _rot(x, even):
    w = x.shape[1]
    return jnp.where(even, pltpu.roll(x, w - 1, 1), pltpu.roll(x, 1, 1))


def _peers():
    x, y, c = lax.axis_index("x"), lax.axis_index("y"), lax.axis_index("c")
    flips = [(0, 0, 1), (1, 0, 0), (0, 1, 0), (1, 1, 0), (1, 0, 1), (0, 1, 1), (1, 1, 1)]
    peers = [(x ^ fx, y ^ fy, c ^ fc) for fx, fy, fc in flips]
    return 4 * x + 2 * y + c, peers


class _Exchange:
    def __init__(self, gathers, swaps, send_sems, recv_sems, local_sems):
        self.me, self.peers = _peers()
        self.slots = [4 * px + 2 * py + pc for px, py, pc in self.peers]
        self.pairs = [(src, dst, True) for src, dst in gathers] + [(src, dst, False) for src, dst in swaps]
        self.send_sems, self.recv_sems, self.local_sems = send_sems, recv_sems, local_sems

    @staticmethod
    def scratch(n):
        return [pltpu.SemaphoreType.DMA((n, N_DEV - 1)), pltpu.SemaphoreType.DMA((n, N_DEV - 1)),
                pltpu.SemaphoreType.DMA((n,))]

    def _parts(self, a, slot):
        src, _, whole = self.pairs[a]
        half = N_DEV // 2
        if whole:
            return [(None, src)]
        if isinstance(src, tuple):
            return [(slot < half, src[0].at[jnp.minimum(slot, half - 1)]),
                    (slot >= half, src[1].at[jnp.maximum(slot - half, 0)])]
        return [(None, src.at[slot])]

    def _local(self, a, src):
        return pltpu.make_async_copy(src, self.pairs[a][1].at[self.me], self.local_sems.at[a])

    def _remote(self, a, k, src, slot):
        return pltpu.make_async_remote_copy(
            src_ref=src, dst_ref=self.pairs[a][1].at[slot], send_sem=self.send_sems.at[a, k],
            recv_sem=self.recv_sems.at[a, k], device_id=self.peers[k], device_id_type=MESH_ID)

    def start(self):
        def go(cond, copy):
            if cond is None:
                copy.start()
            else:
                pl.when(cond)(copy.start)

        for a in range(len(self.pairs)):
            for cond, src in self._parts(a, self.me):
                go(cond, self._local(a, src))
            for k in range(N_DEV - 1):
                for cond, src in self._parts(a, self.slots[k]):
                    go(cond, self._remote(a, k, src, self.me))

    def wait(self):
        for a in range(len(self.pairs)):
            src = self._parts(a, self.me)[0][1]
            for k in range(N_DEV - 1):
                self._remote(a, k, src, self.slots[k]).wait_recv()
        for a in range(len(self.pairs)):
            src = self._parts(a, self.me)[0][1]
            for k in range(N_DEV - 1):
                self._remote(a, k, src, self.me).wait_send()
            self._local(a, src).wait()


ANY_SPEC = pl.BlockSpec(memory_space=pl.ANY)


def _exchange_shapes(gathers, swaps):
    return ([jax.ShapeDtypeStruct((N_DEV,) + a.shape, a.dtype) for a in gathers]
            + [jax.ShapeDtypeStruct(a.shape, a.dtype) for a in swaps])


def _exchange_of(ins, outs, sems, ng):
    return _Exchange(list(zip(ins[:ng], outs[:ng])), list(zip(ins[ng:], outs[ng:])), *sems)


def _exchange_call(gathers, swaps, name):
    ng, ns = len(gathers), len(swaps)
    n = ng + ns

    def body(*refs):
        ex = _exchange_of(refs[:n], refs[n:2 * n], refs[2 * n:], ng)
        ex.start()
        ex.wait()

    return _pcall(
        body, name=name, out_shape=_exchange_shapes(gathers, swaps),
        in_specs=[ANY_SPEC] * (ng + ns), out_specs=[ANY_SPEC] * (ng + ns),
        scratch_shapes=_Exchange.scratch(ng + ns),
    )(*gathers, *swaps)


def _in_proj(x, g1, w_in, cos, sin_s, tm, gathers):
    T = x.shape[0]
    ng = len(gathers)
    nt = T // tm

    def body(x_ref, g_ref, w_ref, cos_ref, sin_ref, *rest):
        ex = _exchange_of(rest[:ng], rest[ng + 8:2 * ng + 8], rest[2 * ng + 8:], ng)
        h_ref, qa_ref, ka_ref, va_ref, qr_ref, kr_ref, vr_ref, gr_ref = rest[ng:ng + 8]
        pl.when(pl.program_id(0) == 0)(ex.start)
        r, n = _rms_stats(x_ref[...])
        h = (n * g_ref[...]).astype(BF)
        h_ref[...] = h

        def proj(c0, w):
            return _dot(h, w_ref[:, c0:c0 + w])

        qa_ref[...] = proj(QA0, ATTN_W).astype(BF)
        ka_ref[...] = proj(KA0, KV_W).astype(BF)
        va_ref[...] = proj(VA0, KV_W).astype(BF)
        vr_ref[...] = proj(VR0, RET_W).astype(BF)
        gr_ref[...] = proj(GR0, RET_W)
        cos_t, sin_t = cos_ref[...], sin_ref[...]
        even = lax.broadcasted_iota(jnp.int32, (tm, RET_HEAD_DIM), 1) % 2 == 0
        for hd in range(N_RET_HEADS):
            c = hd * RET_HEAD_DIM
            q = proj(QR0 + c, RET_HEAD_DIM)
            k = proj(KR0 + c, RET_HEAD_DIM) * RET_K_SCALE
            qr_ref[:, c:c + RET_HEAD_DIM] = (q * cos_t + _rot(q, even) * sin_t).astype(BF)
            kr_ref[:, c:c + RET_HEAD_DIM] = (k * cos_t + _rot(k, even) * sin_t).astype(BF)
        pl.when(pl.program_id(0) == nt - 1)(ex.wait)

    widths = [D_MODEL, ATTN_W, KV_W, KV_W, RET_W, RET_W, RET_W, RET_W]
    dts = [BF] * 7 + [F32]
    outs = _pcall(
        body, name="in_proj", grid=(nt,),
        in_specs=[_rows(tm, D_MODEL), _const((1, D_MODEL)), _vmem_full(), _rows(tm, RET_HEAD_DIM),
                  _rows(tm, RET_HEAD_DIM)] + [ANY_SPEC] * ng,
        out_specs=[_rows(tm, w) for w in widths] + [ANY_SPEC] * ng,
        out_shape=[jax.ShapeDtypeStruct((T, w), dt) for w, dt in zip(widths, dts)] + _exchange_shapes(gathers, []),
        scratch_shapes=_Exchange.scratch(ng),
        compiler_params=_params(("arbitrary",)),
    )(x, g1, w_in, cos, sin_s, *gathers)
    return outs[:8], outs[8:]


def _kv_variants(kk):
    kf = kk.astype(F32)
    lo = lax.broadcasted_iota(jnp.int32, kf.shape, 1) < HEAD_DIM
    h0_lo = jnp.where(lo, kf, 0.0)
    h1_hi = jnp.where(lo, 0.0, kf)
    h0_hi = pltpu.roll(h0_lo, HEAD_DIM, 1)
    h1_lo = pltpu.roll(h1_hi, HEAD_DIM, 1)
    return [[h0_lo.astype(BF), h0_hi.astype(BF)], [h1_lo.astype(BF), h1_hi.astype(BF)]]


def _col_to_tile(tile, col, head):
    lane = lax.broadcasted_iota(jnp.int32, tile.shape, 1)
    return jnp.where(lane == head, col, tile)


def _tri(rows, key_major=False):
    i = lax.broadcasted_iota(jnp.int32, (rows, BLOCK), 0) & (BLOCK - 1)
    j = lax.broadcasted_iota(jnp.int32, (rows, BLOCK), 1)
    return i > j if key_major else j > i


def _fold(x2, tri, first_above):
    a, b = x2[:, :BLOCK], x2[:, BLOCK:]
    return jnp.where(tri, a, b) if first_above else jnp.where(tri, b, a)


def _unfold(x, tri, first_above):
    up, low = jnp.where(tri, x, 0.0), jnp.where(tri, 0.0, x)
    return jnp.concatenate([up, low] if first_above else [low, up], axis=1).astype(BF)


def _scaled(q):
    return (q.astype(F32) * ATTN_SCALE).astype(BF)


def _cat_variants(prev, cur):
    return [[jnp.concatenate([prev[h][e], cur[h][e]], axis=0) for e in range(2)] for h in range(2)]


def _block_variants(prev_ref, cur_ref, nbs):
    var = [_kv_variants(prev_ref[...])] + [_kv_variants(cur_ref[b * BLOCK:(b + 1) * BLOCK, :]) for b in range(nbs)]
    return [_cat_variants(var[b], var[b + 1]) for b in range(nbs)]


def _head_cols(col, nbs):
    tiles = []
    for b in range(nbs):
        t = jnp.zeros((BLOCK, BLOCK), F32)
        for head in range(N_ATTN_HEADS):
            r0 = (b * N_ATTN_HEADS + head) * BLOCK
            t = _col_to_tile(t, col[r0:r0 + BLOCK, :], head)
        tiles.append(t)
    return tiles


def _attn_fwd(sinks, qa, ka, va, nbs, gathers):
    T = qa.shape[0]
    steps = T // (BLOCK * nbs)
    R = nbs * N_ATTN_HEADS * BLOCK
    ng = len(gathers)

    def body(sink_ref, q_ref, kc_ref, kp_ref, vc_ref, vp_ref, *rest):
        ex = _exchange_of(rest[:ng], rest[ng + 3:2 * ng + 3], rest[2 * ng + 3:], ng)
        a_ref, lse_ref, lset_ref = rest[ng:ng + 3]
        n = pl.program_id(0)
        pl.when(n == 0)(ex.start)
        kcat = _block_variants(kp_ref, kc_ref, nbs)
        vcat = _block_variants(vp_ref, vc_ref, nbs)
        tri1 = _tri(BLOCK)
        tiles = []
        for b in range(nbs):
            for pair in range(N_ATTN_HEADS // 2):
                qp = _scaled(q_ref[b * BLOCK:(b + 1) * BLOCK, pair * 128:(pair + 1) * 128])
                for e in range(2):
                    s = _fold(_dot_nt(qp, kcat[b][pair // 2][e]), tri1, True)
                    if b == 0:
                        s = jnp.where(tri1 & (n == 0), MASK_VALUE, s)
                    tiles.append(s)
        s = jnp.concatenate(tiles, axis=0)
        sink = jnp.concatenate([jnp.full((BLOCK, 1), sink_ref[head], F32)
                                for _ in range(nbs) for head in range(N_ATTN_HEADS)], axis=0)
        m = jnp.maximum(jnp.max(s, axis=-1, keepdims=True), sink)
        p = jnp.exp(s - m)
        z = jnp.sum(p, axis=-1, keepdims=True) + jnp.exp(sink - m)
        p2 = _unfold(p * (1.0 / z), _tri(R), True)
        for b in range(nbs):
            for pair in range(N_ATTN_HEADS // 2):
                r0 = (b * N_ATTN_HEADS + 2 * pair) * BLOCK
                acc = (_dot(p2[r0:r0 + BLOCK, :], vcat[b][pair // 2][0])
                       + _dot(p2[r0 + BLOCK:r0 + 2 * BLOCK, :], vcat[b][pair // 2][1]))
                a_ref[b * BLOCK:(b + 1) * BLOCK, pair * 128:(pair + 1) * 128] = acc.astype(BF)
        for b, t in enumerate(_head_cols(m + jnp.log(z), nbs)):
            lse_ref[b * BLOCK:(b + 1) * BLOCK, :] = t
            lset_ref[:, b * BLOCK:(b + 1) * BLOCK] = t.T[:N_ATTN_HEADS, :]
        pl.when(n == steps - 1)(ex.wait)

    cur = lambda w: pl.BlockSpec((BLOCK * nbs, w), lambda n: (n, 0))
    prev = lambda w: pl.BlockSpec((BLOCK, w), lambda n: (jnp.maximum(n * nbs - 1, 0), 0))
    outs = _pcall(
        body, name="attn_fwd", grid=(steps,),
        in_specs=[_smem_full(), cur(ATTN_W), cur(KV_W), prev(KV_W), cur(KV_W), prev(KV_W)] + [ANY_SPEC] * ng,
        out_specs=[cur(ATTN_W), cur(BLOCK), pl.BlockSpec((N_ATTN_HEADS, BLOCK * nbs), lambda n: (0, n))]
        + [ANY_SPEC] * ng,
        out_shape=[jax.ShapeDtypeStruct((T, ATTN_W), BF), jax.ShapeDtypeStruct((T, BLOCK), F32),
                   jax.ShapeDtypeStruct((N_ATTN_HEADS, T), F32)] + _exchange_shapes(gathers, []),
        scratch_shapes=_Exchange.scratch(ng),
        compiler_params=_params(("arbitrary",)),
    )(sinks, qa, ka, ka, va, va, *gathers)
    return outs[:3], outs[3:]


def _ret_fwd(decay, qr, kr, vr, gr, d_intra, xi_b, zeta_b, ncs):
    T = qr.shape[0]
    nc = T // BLOCK
    H, C = N_RET_HEADS, RET_HEAD_DIM

    def body(decay_ref, q_ref, k_ref, v_ref, g_ref, d_ref, xi_ref, zeta_ref, o_ref, s_ref, r_ref, state):
        @pl.when(pl.program_id(0) == 0)
        def _():
            state[...] = jnp.zeros_like(state)

        for h in range(H):
            cs = slice(h * C, (h + 1) * C)
            st = state[h]
            for b in range(ncs):
                rows = slice(b * BLOCK, (b + 1) * BLOCK)
                q, k, v = q_ref[rows, cs], k_ref[rows, cs], v_ref[rows, cs]
                st_b = st.astype(BF)
                s_ref[b, h] = st_b
                inner = (_dot_nt(q, k) * d_ref[h]).astype(BF)
                o = _dot(inner, v) + _dot(q, st_b) * xi_ref[h]
                kz = (k.astype(F32) * zeta_ref[h]).astype(BF)
                st = decay_ref[h] * st + _dot_tn(kz, v)
                o_ref[rows, cs] = o
                mu = jnp.mean(o, axis=-1, keepdims=True)
                oc = o - mu
                rs = lax.rsqrt(jnp.mean(oc * oc, axis=-1, keepdims=True) + GN_EPS)
                g = g_ref[rows, cs]
                r_ref[rows, cs] = (g * jax.nn.sigmoid(g) * (oc * rs)).astype(BF)
            state[h] = st

    cur = pl.BlockSpec((BLOCK * ncs, RET_W), lambda n: (n, 0))
    tab = pl.BlockSpec((H, C, C), lambda n: (0, 0, 0))
    return _pcall(
        body, name="ret_fwd", grid=(nc // ncs,),
        in_specs=[_smem_full(), cur, cur, cur, cur, tab, tab, tab],
        out_specs=[cur, pl.BlockSpec((ncs, H, C, C), lambda n: (n, 0, 0, 0)), cur],
        out_shape=[jax.ShapeDtypeStruct((T, RET_W), F32), jax.ShapeDtypeStruct((nc, H, C, C), BF),
                   jax.ShapeDtypeStruct((T, RET_W), BF)],
        scratch_shapes=[pltpu.VMEM((H, C, C), F32)],
        compiler_params=_params(("arbitrary",)),
    )(decay, qr, kr, vr, gr, d_intra, xi_b, zeta_b)


def _out_proj(a, r, w_out, x, g2, g3, tm):
    T = x.shape[0]

    def body(a_ref, r_ref, w_ref, x_ref, g2_ref, g3_ref, mixed_ref, x1_ref, h2_ref):
        mixed = _dot(a_ref[...], w_ref[:ATTN_W, :]) + _dot(r_ref[...], w_ref[ATTN_W:, :])
        mixed_ref[...] = mixed
        _, n2 = _rms_stats(mixed)
        x1 = x_ref[...] + n2 * g2_ref[...]
        x1_ref[...] = x1
        _, n3 = _rms_stats(x1)
        h2_ref[...] = (n3 * g3_ref[...]).astype(BF)

    return _pcall(
        body, name="out_proj", grid=(T // tm,),
        in_specs=[_rows(tm, ATTN_W), _rows(tm, RET_W), _vmem_full(), _rows(tm, D_MODEL), _const((1, D_MODEL)),
                  _const((1, D_MODEL))],
        out_specs=[_rows(tm, D_MODEL)] * 3,
        out_shape=[jax.ShapeDtypeStruct((T, D_MODEL), F32), jax.ShapeDtypeStruct((T, D_MODEL), F32),
                   jax.ShapeDtypeStruct((T, D_MODEL), BF)],
        compiler_params=_params(("parallel",)),
    )(a, r, w_out, x, g2, g3)


def _shift_down(cur, k, before):
    out = pltpu.roll(cur, k, 0)
    row = lax.broadcasted_iota(jnp.int32, before.shape, 0)
    top = jnp.where(row < k, pltpu.roll(before, k, 0), out[0:8])
    return jnp.concatenate([top, out[8:]], axis=0)


def _shift_up(cur, k, after):
    tm = cur.shape[0]
    out = pltpu.roll(cur, tm - k, 0)
    row = lax.broadcasted_iota(jnp.int32, after.shape, 0)
    bot = jnp.where(row >= 8 - k, pltpu.roll(after, 8 - k, 0), out[tm - 8:])
    return jnp.concatenate([out[:tm - 8], bot], axis=0)


def _gelu_parts(x):
    x2 = x * x
    th = jnp.tanh(GELU_C * (x + GELU_A * x * x2))
    gelu = 0.5 * x * (1.0 + th)
    dgelu = 0.5 * (1.0 + th) + 0.5 * x * (1.0 - th * th) * (GELU_C * (1.0 + 3.0 * GELU_A * x2))
    return gelu, dgelu


def _ffn_fwd(h2, w_up8, conv_w8, conv_b8, w_down4, x1, g4, target, tm):
    T = h2.shape[0]
    nt = T // tm

    def body(h_ref, wug_ref, wuv_ref, cwg_ref, cwv_ref, cbg_ref, cbv_ref, wd_ref, x1_ref, g_ref, t_ref,
             upg_ref, upv_ref, ug_ref, uv_ref, y_ref, dout_ref, dz_ref, dg4_ref, loss_ref, halo, z_acc):
        s = pl.program_id(1)
        first = pl.program_id(0) == 0

        @pl.when(first & (s == 0))
        def _():
            loss_ref[...] = jnp.zeros_like(loss_ref)
            dg4_ref[...] = jnp.zeros_like(dg4_ref)

        h = h_ref[...]
        u = []
        parts = ((wug_ref, cwg_ref, cbg_ref, upg_ref, ug_ref), (wuv_ref, cwv_ref, cbv_ref, upv_ref, uv_ref))
        for part, (wu_ref, cw_ref, cb_ref, up_ref, u_ref) in enumerate(parts):
            cur = _dot(h, wu_ref[0])
            up_ref[0] = cur.astype(BF)
            before = jnp.where(first, 0.0, halo[part, s])
            halo[part, s] = cur[tm - 8:tm, :]
            u_c = (cw_ref[0, pl.ds(0, 1), :] * _shift_down(cur, 2, before)
                   + cw_ref[0, pl.ds(1, 1), :] * _shift_down(cur, 1, before)
                   + cw_ref[0, pl.ds(2, 1), :] * cur + cb_ref[0])
            u_ref[0] = u_c
            u.append(u_c)
        gelu, _ = _gelu_parts(u[0])
        y = (gelu * u[1]).astype(BF)
        y_ref[0] = y
        z_part = _dot(y, wd_ref[0])

        @pl.when(s == 0)
        def _():
            z_acc[...] = z_part

        @pl.when(s > 0)
        def _():
            z_acc[...] += z_part

        @pl.when(s == FF_PAIRS - 1)
        def _():
            r4, n4 = _rms_stats(z_acc[...])
            err = x1_ref[...] + n4 * g_ref[...] - t_ref[...]
            dout = err * (1.0 / D_MODEL)
            dout_ref[...] = dout
            loss_ref[...] += 0.5 * jnp.sum(jnp.mean(err * err, axis=-1, keepdims=True), axis=0, keepdims=True)
            dg4_ref[...] += jnp.sum(dout * n4, axis=0, keepdims=True)
            dz_ref[...] = _rms_bwd(n4, r4, dout * g_ref[...]).astype(BF)

    rows = pl.BlockSpec((tm, D_MODEL), lambda i, s: (i, 0))
    one = lambda shape: pl.BlockSpec(shape, lambda i, s: tuple(0 for _ in shape))
    gate = lambda r, w: pl.BlockSpec((1, r, w), lambda i, s: (s, 0, 0))
    val = lambda r, w: pl.BlockSpec((1, r, w), lambda i, s: (s + FF_PAIRS, 0, 0))
    tile = pl.BlockSpec((1, tm, FF_SHARD), lambda i, s: (s, i, 0))
    half = lambda dt: jax.ShapeDtypeStruct((FF_PAIRS, T, FF_SHARD), dt)
    return _pcall(
        body, name="ffn_fwd", grid=(nt, FF_PAIRS),
        in_specs=[rows, gate(D_MODEL, FF_SHARD), val(D_MODEL, FF_SHARD), gate(3, FF_SHARD), val(3, FF_SHARD),
                  gate(1, FF_SHARD), val(1, FF_SHARD), gate(FF_SHARD, D_MODEL), rows, one((1, D_MODEL)), rows],
        out_specs=[tile] * 5 + [rows, rows, one((1, D_MODEL)), one((8, 128))],
        out_shape=[half(BF), half(BF), half(F32), half(F32), half(BF), jax.ShapeDtypeStruct((T, D_MODEL), F32),
                   jax.ShapeDtypeStruct((T, D_MODEL), BF), jax.ShapeDtypeStruct((1, D_MODEL), F32),
                   jax.ShapeDtypeStruct((8, 128), F32)],
        scratch_shapes=[pltpu.VMEM((2, FF_PAIRS, 8, FF_SHARD), F32), pltpu.VMEM((tm, D_MODEL), F32)],
        compiler_params=_params(("arbitrary", "arbitrary")),
    )(h2, w_up8, w_up8, conv_w8, conv_w8, conv_b8, conv_b8, w_down4, x1, g4, target)


def _ffn_bwd_a(dz, h2, w_down4, u_g, u_v, up_g, up_v, y4, conv_w8, tm):
    T = dz.shape[0]
    nt = T // tm

    def body(dz_ref, h_ref, wd_ref, ug_ref, uv_ref, upg_ref, upv_ref, y_ref, cwg_ref, cwv_ref,
             dupg_ref, dupv_ref, dcbg_ref, dcbv_ref, dcwg_ref, dcwv_ref, gwug_ref, gwuv_ref, gwd_ref, carry):
        @pl.when(pl.program_id(1) == 0)
        def _():
            for ref in (dcbg_ref, dcbv_ref, dcwg_ref, dcwv_ref, gwug_ref, gwuv_ref, gwd_ref, carry):
                ref[...] = jnp.zeros_like(ref)

        dz = dz_ref[...]
        h = h_ref[...]
        dy = _dot_nt(dz, wd_ref[0])
        gwd_ref[0] += _dot_tn(y_ref[0], dz)
        gelu, dgelu = _gelu_parts(ug_ref[0])
        parts = ((0, dy * uv_ref[0] * dgelu, upg_ref, cwg_ref, dupg_ref, dcbg_ref, dcwg_ref, gwug_ref),
                 (1, dy * gelu, upv_ref, cwv_ref, dupv_ref, dcbv_ref, dcwv_ref, gwuv_ref))
        for part, d, up_ref, cw_ref, dup_ref, dcb_ref, dcw_ref, gwu_ref in parts:
            after = carry[part]
            d1 = _shift_up(d, 1, after)
            d2 = _shift_up(d, 2, after)
            carry[part] = d[0:8, :]
            upc = up_ref[0].astype(F32)
            dcb_ref[0] += jnp.sum(d, axis=0, keepdims=True)
            dcw_ref[0, pl.ds(2, 1), :] += jnp.sum(d * upc, axis=0, keepdims=True)
            dcw_ref[0, pl.ds(1, 1), :] += jnp.sum(d1 * upc, axis=0, keepdims=True)
            dcw_ref[0, pl.ds(0, 1), :] += jnp.sum(d2 * upc, axis=0, keepdims=True)
            dup = (cw_ref[0, pl.ds(2, 1), :] * d + cw_ref[0, pl.ds(1, 1), :] * d1
                   + cw_ref[0, pl.ds(0, 1), :] * d2).astype(BF)
            dup_ref[0] = dup
            gwu_ref[0] += _dot_tn(h, dup)

    rev = pl.BlockSpec((tm, D_MODEL), lambda s, i: (nt - 1 - i, 0))
    tile = pl.BlockSpec((1, tm, FF_SHARD), lambda s, i: (s, nt - 1 - i, 0))
    acc = lambda r, w: pl.BlockSpec((1, r, w), lambda s, i: (s, 0, 0))
    acc_val = pl.BlockSpec((1, 3, FF_SHARD), lambda s, i: (s + FF_PAIRS, 0, 0))
    half = lambda r, dt: jax.ShapeDtypeStruct((FF_PAIRS, r, FF_SHARD), dt)
    return _pcall(
        body, name="ffn_bwd_a", grid=(FF_PAIRS, nt),
        in_specs=[rev, rev, acc(FF_SHARD, D_MODEL), tile, tile, tile, tile, tile, acc(3, FF_SHARD), acc_val],
        out_specs=[tile, tile, acc(1, FF_SHARD), acc(1, FF_SHARD), acc(3, FF_SHARD), acc(3, FF_SHARD),
                   acc(D_MODEL, FF_SHARD), acc(D_MODEL, FF_SHARD), acc(FF_SHARD, D_MODEL)],
        out_shape=[half(T, BF), half(T, BF), half(1, F32), half(1, F32), half(3, F32), half(3, F32),
                   half(D_MODEL, F32), half(D_MODEL, F32), jax.ShapeDtypeStruct((FF_PAIRS, FF_SHARD, D_MODEL), F32)],
        scratch_shapes=[pltpu.VMEM((2, 8, FF_SHARD), F32)],
        compiler_params=_params(("arbitrary", "arbitrary")),
    )(dz, h2, w_down4, u_g, u_v, up_g, up_v, y4, conv_w8, conv_w8)


def _ffn_bwd_b(dup_g, dup_v, w_up8, x1, dout, g3, mixed, g2, w_out, tm, gwu_g, gwu_v):
    T = x1.shape[0]
    nt = T // tm

    def body(dupg_ref, dupv_ref, wup_ref, x1_ref, dout_ref, g3_ref, mixed_ref, g2_ref, wout_ref, gwug_ref, gwuv_ref,
             dx1_ref, dmixed_ref, da_ref, dr_ref, dg3_ref, dg2_ref, pup_ref, *sems):
        ex = _Exchange([], [((gwug_ref, gwuv_ref), pup_ref)], *sems)

        @pl.when(pl.program_id(0) == 0)
        def _():
            ex.start()
            dg3_ref[...] = jnp.zeros_like(dg3_ref)
            dg2_ref[...] = jnp.zeros_like(dg2_ref)

        dh2 = jnp.zeros((tm, D_MODEL), F32)
        for s in range(FF_PAIRS):
            dh2 = dh2 + _dot_nt(dupg_ref[s], wup_ref[s]) + _dot_nt(dupv_ref[s], wup_ref[s + FF_PAIRS])
        r3, n3 = _rms_stats(x1_ref[...])
        dg3_ref[...] += jnp.sum(dh2 * n3, axis=0, keepdims=True)
        dx1 = dout_ref[...] + _rms_bwd(n3, r3, dh2 * g3_ref[...])
        dx1_ref[...] = dx1
        r2, n2 = _rms_stats(mixed_ref[...])
        dg2_ref[...] += jnp.sum(dx1 * n2, axis=0, keepdims=True)
        dmixed = _rms_bwd(n2, r2, dx1 * g2_ref[...]).astype(BF)
        dmixed_ref[...] = dmixed
        da_ref[...] = _dot_nt(dmixed, wout_ref[:ATTN_W, :])
        dr_ref[...] = _dot_nt(dmixed, wout_ref[ATTN_W:, :])
        pl.when(pl.program_id(0) == nt - 1)(ex.wait)

    half = pl.BlockSpec((FF_PAIRS, tm, FF_SHARD), lambda i: (0, i, 0))
    outs = _pcall(
        body, name="ffn_bwd_b", grid=(nt,),
        in_specs=[half, half, _vmem_full(), _rows(tm, D_MODEL), _rows(tm, D_MODEL), _const((1, D_MODEL)),
                  _rows(tm, D_MODEL), _const((1, D_MODEL)), _vmem_full(), ANY_SPEC, ANY_SPEC],
        out_specs=[_rows(tm, D_MODEL), _rows(tm, D_MODEL), _rows(tm, ATTN_W), _rows(tm, RET_W),
                   _const((1, D_MODEL)), _const((1, D_MODEL)), ANY_SPEC],
        out_shape=[jax.ShapeDtypeStruct((T, D_MODEL), F32), jax.ShapeDtypeStruct((T, D_MODEL), BF),
                   jax.ShapeDtypeStruct((T, ATTN_W), F32), jax.ShapeDtypeStruct((T, RET_W), F32),
                   jax.ShapeDtypeStruct((1, D_MODEL), F32), jax.ShapeDtypeStruct((1, D_MODEL), F32),
                   jax.ShapeDtypeStruct((N_DEV, D_MODEL, FF_SHARD), F32)],
        scratch_shapes=_Exchange.scratch(1),
        compiler_params=_params(("arbitrary",)),
    )(dup_g, dup_v, w_up8, x1, dout, g3, mixed, g2, w_out, gwu_g, gwu_v)
    return outs[:6], outs[6]


def _ret_bwd(decay, qr, kr, vr, gr, o, states, dr, d_intra, d_intra_t, xi_b, zeta_b, cos, sin_s, swaps, ncs):
    T = qr.shape[0]
    nc = T // BLOCK
    H, C = N_RET_HEADS, RET_HEAD_DIM
    ns = len(swaps)

    def body(decay_ref, q_ref, k_ref, v_ref, g_ref, o_ref, s_ref, dr_ref, d_ref, dt_ref, xi_ref, zeta_ref,
             cos_ref, sin_ref, *rest):
        ex = _exchange_of(rest[:ns], rest[ns + 1:2 * ns + 1], rest[2 * ns + 2:], 0)
        dret_ref, gstate = rest[ns], rest[2 * ns + 1]

        @pl.when(pl.program_id(0) == 0)
        def _():
            ex.start()
            gstate[...] = jnp.zeros_like(gstate)

        even = lax.broadcasted_iota(jnp.int32, (BLOCK, C), 1) % 2 == 0
        for h in range(H):
            cs = slice(h * C, (h + 1) * C)
            gst = gstate[h]
            for b in reversed(range(ncs)):
                rows = slice(b * BLOCK, (b + 1) * BLOCK)
                cos_t, sin_t = cos_ref[rows, :], sin_ref[rows, :]
                q, k, v = q_ref[rows, cs], k_ref[rows, cs], v_ref[rows, cs]
                g, o_h, dr_h = g_ref[rows, cs], o_ref[rows, cs], dr_ref[rows, cs]
                mu = jnp.mean(o_h, axis=-1, keepdims=True)
                oc = o_h - mu
                rs = lax.rsqrt(jnp.mean(oc * oc, axis=-1, keepdims=True) + GN_EPS)
                on = oc * rs
                sg = jax.nn.sigmoid(g)
                dg = dr_h * on * (sg * (1.0 + g * (1.0 - sg)))
                don = dr_h * (g * sg)
                do = rs * (don - jnp.mean(don, axis=-1, keepdims=True)
                           - on * jnp.mean(don * on, axis=-1, keepdims=True))
                do_b = do.astype(BF)
                dox_b = (do * xi_ref[h]).astype(BF)
                gst_b = gst.astype(BF)
                kz = (k.astype(F32) * zeta_ref[h]).astype(BF)
                da_b = (_dot_nt(do_b, v) * d_ref[h]).astype(BF)
                dat_b = (_dot_nt(v, do_b) * dt_ref[h]).astype(BF)
                mt_b = (_dot_nt(k, q) * dt_ref[h]).astype(BF)
                dq = _dot(da_b, k) + _dot_nt(dox_b, s_ref[b, h])
                dk = _dot(dat_b, q) + _dot_nt(v, gst_b) * zeta_ref[h]
                dv = _dot(mt_b, do_b) + _dot(kz, gst_b)
                gst = decay_ref[h] * gst + _dot_tn(q, dox_b)
                dq = dq * cos_t - _rot(dq, even) * sin_t
                dk = (dk * cos_t - _rot(dk, even) * sin_t) * RET_K_SCALE
                dret_ref[rows, h * C:(h + 1) * C] = dq.astype(BF)
                dret_ref[rows, RET_W + h * C:RET_W + (h + 1) * C] = dk.astype(BF)
                dret_ref[rows, 2 * RET_W + h * C:2 * RET_W + (h + 1) * C] = dv.astype(BF)
                dret_ref[rows, 3 * RET_W + h * C:3 * RET_W + (h + 1) * C] = dg.astype(BF)
            gstate[h] = gst
        pl.when(pl.program_id(0) == steps - 1)(ex.wait)

    steps = nc // ncs
    rev = lambda w: pl.BlockSpec((BLOCK * ncs, w), lambda n: (steps - 1 - n, 0))
    tab = pl.BlockSpec((H, C, C), lambda n: (0, 0, 0))
    outs = _pcall(
        body, name="ret_bwd", grid=(steps,),
        in_specs=[_smem_full(), rev(RET_W), rev(RET_W), rev(RET_W), rev(RET_W), rev(RET_W),
                  pl.BlockSpec((ncs, H, C, C), lambda n: (steps - 1 - n, 0, 0, 0)), rev(RET_W), tab, tab, tab, tab,
                  rev(C), rev(C)] + [ANY_SPEC] * ns,
        out_specs=[rev(4 * RET_W)] + [ANY_SPEC] * ns,
        out_shape=[jax.ShapeDtypeStruct((T, 4 * RET_W), BF)] + _exchange_shapes([], swaps),
        scratch_shapes=[pltpu.VMEM((H, C, C), F32)] + _Exchange.scratch(ns),
        compiler_params=_params(("arbitrary",)),
    )(decay, qr, kr, vr, gr, o, states, dr, d_intra, d_intra_t, xi_b, zeta_b, cos, sin_s, *swaps)
    return outs[0], outs[1:]


def _attn_bwd_dq(sinks, qa, ka, va, da, lse, nbs, swaps):
    T = qa.shape[0]
    steps = T // (BLOCK * nbs)
    R = nbs * N_ATTN_HEADS * BLOCK
    ns = len(swaps)

    def body(sink_ref, q_ref, kc_ref, kp_ref, vc_ref, vp_ref, da_ref, lse_ref, *rest):
        ex = _exchange_of(rest[:ns], rest[ns + 3:2 * ns + 3], rest[2 * ns + 3:], 0)
        dq_ref, deltat_ref, dsink_ref = rest[ns:ns + 3]
        n = pl.program_id(0)

        @pl.when(n == 0)
        def _():
            ex.start()
            dsink_ref[...] = jnp.zeros_like(dsink_ref)

        kcat = _block_variants(kp_ref, kc_ref, nbs)
        vcat = _block_variants(vp_ref, vc_ref, nbs)
        tri1 = _tri(BLOCK)
        lane = lax.broadcasted_iota(jnp.int32, (BLOCK, BLOCK), 1)
        s_tiles, dp_tiles, lse_cols = [], [], []
        for b in range(nbs):
            rows = slice(b * BLOCK, (b + 1) * BLOCK)
            lse_tile = lse_ref[rows, :]
            for pair in range(N_ATTN_HEADS // 2):
                qp = _scaled(q_ref[rows, pair * 128:(pair + 1) * 128])
                dop = da_ref[rows, pair * 128:(pair + 1) * 128].astype(BF)
                for e in range(2):
                    s = _fold(_dot_nt(qp, kcat[b][pair // 2][e]), tri1, True)
                    if b == 0:
                        s = jnp.where(tri1 & (n == 0), MASK_VALUE, s)
                    s_tiles.append(s)
                    dp_tiles.append(_fold(_dot_nt(dop, vcat[b][pair // 2][e]), tri1, True))
                    lse_cols.append(jnp.sum(jnp.where(lane == 2 * pair + e, lse_tile, 0.0), axis=-1, keepdims=True))
        lse_c = jnp.concatenate(lse_cols, axis=0)
        p = jnp.exp(jnp.concatenate(s_tiles, axis=0) - lse_c)
        dp = jnp.concatenate(dp_tiles, axis=0)
        delta = jnp.sum(p * dp, axis=-1, keepdims=True)
        ds2 = _unfold(p * (dp - delta), _tri(R), True)
        for b in range(nbs):
            for pair in range(N_ATTN_HEADS // 2):
                r0 = (b * N_ATTN_HEADS + 2 * pair) * BLOCK
                acc = (_dot(ds2[r0:r0 + BLOCK, :], kcat[b][pair // 2][0])
                       + _dot(ds2[r0 + BLOCK:r0 + 2 * BLOCK, :], kcat[b][pair // 2][1]))
                dq_ref[b * BLOCK:(b + 1) * BLOCK, pair * 128:(pair + 1) * 128] = (acc * ATTN_SCALE).astype(BF)
        for b, t in enumerate(_head_cols(delta, nbs)):
            deltat_ref[:, b * BLOCK:(b + 1) * BLOCK] = t.T[:N_ATTN_HEADS, :]
        sink = jnp.concatenate([jnp.full((BLOCK, 1), sink_ref[head], F32)
                                for _ in range(nbs) for head in range(N_ATTN_HEADS)], axis=0)
        ds_sink = -jnp.exp(sink - lse_c) * delta
        row8 = lax.broadcasted_iota(jnp.int32, (N_ATTN_HEADS, BLOCK), 0)
        dsink = jnp.zeros((N_ATTN_HEADS, BLOCK), F32)
        for b in range(nbs):
            for head in range(N_ATTN_HEADS):
                r0 = (b * N_ATTN_HEADS + head) * BLOCK
                dsink = dsink + jnp.where(row8 == head, jnp.sum(ds_sink[r0:r0 + BLOCK, :], axis=0, keepdims=True), 0.0)
        dsink_ref[...] += dsink
        pl.when(n == steps - 1)(ex.wait)

    cur = lambda w: pl.BlockSpec((BLOCK * nbs, w), lambda n: (n, 0))
    prev = lambda w: pl.BlockSpec((BLOCK, w), lambda n: (jnp.maximum(n * nbs - 1, 0), 0))
    outs = _pcall(
        body, name="attn_bwd_dq", grid=(steps,),
        in_specs=[_smem_full(), cur(ATTN_W), cur(KV_W), prev(KV_W), cur(KV_W), prev(KV_W), cur(ATTN_W), cur(BLOCK)]
        + [ANY_SPEC] * ns,
        out_specs=[cur(ATTN_W), pl.BlockSpec((N_ATTN_HEADS, BLOCK * nbs), lambda n: (0, n)),
                   _const((N_ATTN_HEADS, BLOCK))] + [ANY_SPEC] * ns,
        out_shape=[jax.ShapeDtypeStruct((T, ATTN_W), BF), jax.ShapeDtypeStruct((N_ATTN_HEADS, T), F32),
                   jax.ShapeDtypeStruct((N_ATTN_HEADS, BLOCK), F32)] + _exchange_shapes([], swaps),
        scratch_shapes=_Exchange.scratch(ns),
        compiler_params=_params(("arbitrary",)),
    )(sinks, qa, ka, ka, va, va, da, lse, *swaps)
    return outs[:3], outs[3:]


def _attn_bwd_dkv(qa, ka, va, da, lse_t, delta_t, nbs):
    T = qa.shape[0]
    nb = T // BLOCK
    steps = nb // nbs
    R = nbs * N_ATTN_HEADS * BLOCK

    def body(qc_ref, qn_ref, dac_ref, dan_ref, k_ref, v_ref, lc_ref, ln_ref, dc_ref, dn_ref, dk_ref, dv_ref):
        n = pl.program_id(0)
        tri1 = _tri(BLOCK, True)
        lo = lax.broadcasted_iota(jnp.int32, (BLOCK, 128), 1) < HEAD_DIM
        kv = [_kv_variants(k_ref[b * BLOCK:(b + 1) * BLOCK, :]) for b in range(nbs)]
        vv = [_kv_variants(v_ref[b * BLOCK:(b + 1) * BLOCK, :]) for b in range(nbs)]
        qcat, docat = [], []
        s_tiles, dp_tiles, lse_tiles, delta_tiles = [], [], [], []
        for b in range(nbs):
            rows = slice(b * BLOCK, (b + 1) * BLOCK)
            nrows = slice((b + 1) * BLOCK, (b + 2) * BLOCK)
            inside = b < nbs - 1
            for pair in range(N_ATTN_HEADS // 2):
                ps = slice(pair * 128, (pair + 1) * 128)
                q2 = _scaled(jnp.concatenate([qc_ref[rows, ps], qc_ref[nrows, ps] if inside else qn_ref[:, ps]], axis=0))
                do2 = jnp.concatenate([dac_ref[rows, ps], dac_ref[nrows, ps] if inside else dan_ref[:, ps]],
                                      axis=0).astype(BF)
                qcat.append(q2)
                docat.append(do2)
                for e in range(2):
                    one = pl.ds(2 * pair + e, 1)
                    s = _fold(_dot_nt(kv[b][pair // 2][e], q2), tri1, False)
                    if not inside:
                        s = jnp.where(tri1 & (n == steps - 1), MASK_VALUE, s)
                    s_tiles.append(s)
                    dp_tiles.append(_fold(_dot_nt(vv[b][pair // 2][e], do2), tri1, False))
                    lse_tiles.append(jnp.where(tri1, lc_ref[one, nrows] if inside else ln_ref[one, :], lc_ref[one, rows]))
                    delta_tiles.append(jnp.where(tri1, dc_ref[one, nrows] if inside else dn_ref[one, :],
                                                 dc_ref[one, rows]))
        pt = jnp.exp(jnp.concatenate(s_tiles, axis=0) - jnp.concatenate(lse_tiles, axis=0))
        dst = pt * (jnp.concatenate(dp_tiles, axis=0) - jnp.concatenate(delta_tiles, axis=0))
        tri = _tri(R, True)
        pt2 = _unfold(pt, tri, False)
        dst2 = _unfold(dst, tri, False)
        for b in range(nbs):
            dk = jnp.zeros((BLOCK, 128), F32)
            dv = jnp.zeros((BLOCK, 128), F32)
            for pair in range(N_ATTN_HEADS // 2):
                h = pair // 2
                for e in range(2):
                    r0 = (b * N_ATTN_HEADS + 2 * pair + e) * BLOCK
                    half = lo if e == 0 else jnp.logical_not(lo)
                    dv_e = jnp.where(half, _dot(pt2[r0:r0 + BLOCK, :], docat[b * 4 + pair]), 0.0)
                    dk_e = jnp.where(half, _dot(dst2[r0:r0 + BLOCK, :], qcat[b * 4 + pair]), 0.0)
                    if e != h:
                        dv_e = pltpu.roll(dv_e, HEAD_DIM, 1)
                        dk_e = pltpu.roll(dk_e, HEAD_DIM, 1)
                    dv = dv + dv_e
                    dk = dk + dk_e
            dk_ref[b * BLOCK:(b + 1) * BLOCK, :] = dk.astype(BF)
            dv_ref[b * BLOCK:(b + 1) * BLOCK, :] = dv.astype(BF)

    cur = lambda w: pl.BlockSpec((BLOCK * nbs, w), lambda n: (n, 0))
    nxt = lambda w: pl.BlockSpec((BLOCK, w), lambda n: (jnp.minimum((n + 1) * nbs, nb - 1), 0))
    tcur = pl.BlockSpec((N_ATTN_HEADS, BLOCK * nbs), lambda n: (0, n))
    tnxt = pl.BlockSpec((N_ATTN_HEADS, BLOCK), lambda n: (0, jnp.minimum((n + 1) * nbs, nb - 1)))
    return _pcall(
        body, name="attn_bwd_dkv", grid=(steps,),
        in_specs=[cur(ATTN_W), nxt(ATTN_W), cur(ATTN_W), nxt(ATTN_W), cur(KV_W), cur(KV_W), tcur, tnxt, tcur, tnxt],
        out_specs=[cur(KV_W), cur(KV_W)],
        out_shape=[jax.ShapeDtypeStruct((T, KV_W), BF), jax.ShapeDtypeStruct((T, KV_W), BF)],
        compiler_params=_params(("parallel",)),
    )(qa, qa, da, da, ka, va, lse_t, lse_t, delta_t, delta_t)


def _in_proj_bwd(dqa, dka, dva, dret, w_in, x, g1, dx1, tm):
    T = x.shape[0]

    def body(dqa_ref, dka_ref, dva_ref, dret_ref, w_ref, x_ref, g_ref, dx1_ref, dx_ref, dg1_ref):
        @pl.when(pl.program_id(0) == 0)
        def _():
            dg1_ref[...] = jnp.zeros_like(dg1_ref)

        dh = (_dot_nt(dqa_ref[...], w_ref[:, QA0:QA0 + ATTN_W]) + _dot_nt(dka_ref[...], w_ref[:, KA0:KA0 + KV_W])
              + _dot_nt(dva_ref[...], w_ref[:, VA0:VA0 + KV_W]) + _dot_nt(dret_ref[...], w_ref[:, QR0:IN_W]))
        r, n = _rms_stats(x_ref[...])
        dg1_ref[...] += jnp.sum(dh * n, axis=0, keepdims=True)
        dx_ref[...] = dx1_ref[...] + _rms_bwd(n, r, dh * g_ref[...])

    return _pcall(
        body, name="in_proj_bwd", grid=(T // tm,),
        in_specs=[_rows(tm, ATTN_W), _rows(tm, KV_W), _rows(tm, KV_W), _rows(tm, 4 * RET_W), _vmem_full(),
                  _rows(tm, D_MODEL), _const((1, D_MODEL)), _rows(tm, D_MODEL)],
        out_specs=[_rows(tm, D_MODEL), _const((1, D_MODEL))],
        out_shape=[jax.ShapeDtypeStruct((T, D_MODEL), F32), jax.ShapeDtypeStruct((1, D_MODEL), F32)],
        compiler_params=_params(("arbitrary",)),
    )(dqa, dka, dva, dret, w_in, x, g1, dx1)


def _wgrad(a_list, b_list, tk, name):
    T = a_list[0].shape[0]
    na, nbb = len(a_list), len(b_list)
    m_sizes = [a.shape[1] for a in a_list]
    n_sizes = [b.shape[1] for b in b_list]
    M, N = sum(m_sizes), sum(n_sizes)
    nk = T // tk
    chunk = 512

    def body(*refs):
        a_refs, b_refs = refs[:na], refs[na:na + nbb]
        out_ref, acc = refs[na + nbb], refs[na + nbb + 1]
        k = pl.program_id(0)

        @pl.when(k == 0)
        def _():
            acc[...] = jnp.zeros_like(acc)

        r0 = 0
        for ai in range(na):
            a = a_refs[ai][...]
            c0 = 0
            for bi in range(nbb):
                for s in range(0, n_sizes[bi], chunk):
                    w = min(chunk, n_sizes[bi] - s)
                    acc[r0:r0 + m_sizes[ai], c0 + s:c0 + s + w] += _dot_tn(a, b_refs[bi][:, s:s + w])
                c0 += n_sizes[bi]
            r0 += m_sizes[ai]

        @pl.when(k == nk - 1)
        def _():
            pltpu.sync_copy(acc, out_ref)

    return _pcall(
        body, name=name, grid=(nk,),
        in_specs=[_rows(tk, w) for w in m_sizes + n_sizes],
        out_specs=pl.BlockSpec(memory_space=pl.ANY),
        out_shape=jax.ShapeDtypeStruct((M, N), F32),
        scratch_shapes=[pltpu.VMEM((M, N), F32)],
        compiler_params=_params(("arbitrary",)),
    )(*a_list, *b_list)


def _adamw_math(w, g, m, v):
    m = ADAM_B1 * m + (1.0 - ADAM_B1) * g
    v = ADAM_B2 * v + (1.0 - ADAM_B2) * (g * g)
    m_hat = m / (1.0 - ADAM_B1 ** ADAM_STEP)
    v_hat = v / (1.0 - ADAM_B2 ** ADAM_STEP)
    delta = -ADAM_LR * (m_hat / (jnp.sqrt(v_hat) + ADAM_EPS) + ADAM_WD * w)
    return delta, m, v


def _sum_parts(parts_ref):
    g = parts_ref[0]
    for i in range(1, N_DEV):
        g = g + parts_ref[i]
    return g


def _adamw_shard(parts, w, m, v, tr, name):
    R, C = w.shape

    def body(p_ref, w_ref, m_ref, v_ref, g_ref, d_ref, nm_ref, nv_ref):
        g = _sum_parts(p_ref)
        g_ref[...] = g
        d_ref[...], nm_ref[...], nv_ref[...] = _adamw_math(w_ref[...], g, m_ref[...], v_ref[...])

    blk = pl.BlockSpec((tr, C), lambda i: (i, 0))
    return _pcall(
        body, name=name, grid=(R // tr,),
        in_specs=[pl.BlockSpec((N_DEV, tr, C), lambda i: (0, i, 0)), blk, blk, blk],
        out_specs=[blk] * 4,
        out_shape=[jax.ShapeDtypeStruct((R, C), F32)] * 4,
        compiler_params=_params(("parallel",)),
    )(parts, w, m, v)


def _sum_small(parts):
    def body(p_ref, g_ref):
        g_ref[...] = _sum_parts(p_ref)

    return _pcall(body, name="sum_small", out_shape=jax.ShapeDtypeStruct(parts.shape[1:], F32),
                  in_specs=[_vmem_full()], out_specs=_vmem_full())(parts)


def _adamw_small(g, w, m, v, name):
    def body(g_ref, w_ref, m_ref, v_ref, d_ref, nm_ref, nv_ref):
        d_ref[...], nm_ref[...], nv_ref[...] = _adamw_math(w_ref[...], g_ref[...], m_ref[...], v_ref[...])

    return _pcall(body, name=name, out_shape=[jax.ShapeDtypeStruct(w.shape, F32)] * 3,
                  in_specs=[_vmem_full()] * 4, out_specs=[_vmem_full()] * 3)(g, w, m, v)


def _tables(T):
    h, c = N_RET_HEADS, BLOCK
    pos = jnp.arange(T, dtype=F32)
    angle = 1.0 / jnp.power(10000.0, jnp.linspace(0.0, 1.0, RET_HEAD_DIM // 2, dtype=F32))
    angle = jnp.repeat(angle, 2)
    sin = jnp.sin(pos[:, None] * angle[None])
    cos = jnp.cos(pos[:, None] * angle[None])
    even = (jnp.arange(RET_HEAD_DIM) % 2 == 0)[None, :]
    sin_s = jnp.where(even, -sin, sin)
    log_gamma = jnp.log(1.0 - jnp.power(2.0, -5.0 - jnp.arange(h, dtype=F32)))
    idx = jnp.arange(c, dtype=F32)
    rel = idx[:, None] - idx[None, :]
    d_intra = jnp.where(rel[None] >= 0, jnp.exp(log_gamma[:, None, None] * jnp.maximum(rel, 0.0)[None]), 0.0)
    xi = jnp.exp(log_gamma[None, :] * (idx[:, None] + 1.0))
    zeta = jnp.exp(log_gamma[None, :] * (c - 1.0 - idx[:, None]))
    decay = jnp.exp(log_gamma * c)
    xi_b = jnp.broadcast_to(xi.T[:, :, None], (h, c, RET_HEAD_DIM))
    zeta_b = jnp.broadcast_to(zeta.T[:, :, None], (h, c, RET_HEAD_DIM))
    return cos, sin_s, d_intra, jnp.swapaxes(d_intra, 1, 2), xi_b, zeta_b, decay


def _to_shards(full, cols):
    r = full.shape[0]
    return jnp.swapaxes(full.reshape(r, N_DEV, cols), 0, 1)


def _from_shards(sh):
    n, r, cols = sh.shape
    return jnp.swapaxes(sh, 0, 1).reshape(r, n * cols)


SMALL_ROWS = 216


def _pack_small(gains, conv_b, conv_w, sinks):
    parts = [g.reshape(8, 128) for g in gains] + [conv_b.reshape(44, 128), conv_w.reshape(132, 128),
                                                  jnp.pad(sinks.reshape(1, 8), ((0, 0), (0, 120)))]
    packed = jnp.concatenate(parts, axis=0)
    return jnp.pad(packed, ((0, SMALL_ROWS - packed.shape[0]), (0, 0)))


def kernel(x, mix_pre_norm, w_in, attn_sinks, w_out, mix_post_norm, ffn_pre_norm, w_up, conv_w, conv_b, w_down, ffn_post_norm, loss_target, m_mix_pre_norm, m_w_in, m_attn_sinks, m_w_out, m_mix_post_norm, m_ffn_pre_norm, m_w_up, m_conv_w, m_conv_b, m_w_down, m_ffn_post_norm, v_mix_pre_norm, v_w_in, v_attn_sinks, v_w_out, v_mix_post_norm, v_ffn_pre_norm, v_w_up, v_conv_w, v_conv_b, v_w_down, v_ffn_post_norm):
    T = x.shape[1]
    tm = min(512, T)
    tm_big = min(1024, T)
    nbs = min(4, T // BLOCK)
    x2 = x.reshape(T, D_MODEL)
    target = loss_target.reshape(T, D_MODEL)
    me = 4 * lax.axis_index("x") + 2 * lax.axis_index("y") + lax.axis_index("c")

    g_in, g_cw = _exchange_call([w_in[0].astype(BF), conv_w[0]], [], "gather_w_in")
    w_in_f = _from_shards(g_in)
    cos, sin_s, d_intra, d_intra_t, xi_b, zeta_b, decay = _tables(T)
    sinks = attn_sinks.reshape(N_ATTN_HEADS)

    (h1, qa, ka, va, qr, kr, vr, gr), (w_up8, g_out) = _in_proj(
        x2, mix_pre_norm, w_in_f, cos, sin_s, tm_big, [w_up[0].astype(BF), w_out[0].astype(BF)])
    w_out_f = g_out.reshape(D_MODEL, D_MODEL)
    (a, lse, lse_t), (g_down,) = _attn_fwd(sinks, qa, ka, va, nbs, [w_down[0].astype(BF)])
    o, states, r = _ret_fwd(decay, qr, kr, vr, gr, d_intra, xi_b, zeta_b, nbs)
    mixed, x1, h2 = _out_proj(a, r, w_out_f, x2, mix_post_norm, ffn_pre_norm, tm_big)
    w_down4 = g_down.reshape(FF_PAIRS, FF_SHARD, D_MODEL)
    up_g, up_v, u_g, u_v, y4, dout, dz, dg4, loss_acc = _ffn_fwd(
        h2, w_up8, g_cw, conv_b.reshape(N_DEV, 1, FF_SHARD), w_down4, x1, ffn_post_norm, target, tm)
    loss = lax.psum(loss_acc[0, 0], ("x", "y", "c"))

    dup_g, dup_v, dcb_g, dcb_v, dcw_g, dcw_v, gwu_g, gwu_v, gw_down4 = _ffn_bwd_a(
        dz, h2, w_down4, u_g, u_v, up_g, up_v, y4, g_cw, tm)
    dcb = jnp.concatenate([dcb_g, dcb_v], axis=0).reshape(1, 2 * D_FF)
    dcw = _from_shards(jnp.concatenate([dcw_g, dcw_v], axis=0))
    gw_down = gw_down4.reshape(D_FF, D_MODEL)
    (dx1, dmixed, da, dr, dg3, dg2), p_up = _ffn_bwd_b(
        dup_g, dup_v, w_up8, x1, dout, ffn_pre_norm, mixed, mix_post_norm, w_out_f, tm, gwu_g, gwu_v)
    gw_out = _wgrad([a, r], [dmixed], tm, "wgrad_out")
    dret, (p_down,) = _ret_bwd(decay, qr, kr, vr, gr, o, states, dr, d_intra, d_intra_t, xi_b, zeta_b, cos, sin_s,
                               [gw_down.reshape(N_DEV, D_FF // N_DEV, D_MODEL)], nbs)
    (dqa, delta_t, dsink), (p_out,) = _attn_bwd_dq(sinks, qa, ka, va, da, lse, nbs,
                                                   [gw_out.reshape(N_DEV, D_MODEL // N_DEV, D_MODEL)])
    dka, dva = _attn_bwd_dkv(qa, ka, va, da, lse_t, delta_t, nbs)
    grad_x, dg1 = _in_proj_bwd(dqa, dka, dva, dret, w_in_f, x2, mix_pre_norm, dx1, tm)
    gw_in = _wgrad([h1], [dqa, dka, dva, dret], tm, "wgrad_in")

    small = _pack_small([dg1, dg2, dg3, dg4], dcb, dcw, dsink[:, 0])
    small_all, p_in = _exchange_call([small], [_to_shards(gw_in, IN_W // N_DEV)], "exchange_last")
    g_small = _sum_small(small_all)

    g_w_in, d_w_in, nm_w_in, nv_w_in = _adamw_shard(p_in, w_in[0], m_w_in[0], v_w_in[0], 256, "adamw_in")
    g_w_up, d_w_up, nm_w_up, nv_w_up = _adamw_shard(p_up, w_up[0], m_w_up[0], v_w_up[0], 256, "adamw_up")
    g_w_out, d_w_out, nm_w_out, nv_w_out = _adamw_shard(p_out, w_out[0], m_w_out[0], v_w_out[0], 128, "adamw_out")
    g_w_down, d_w_down, nm_w_down, nv_w_down = _adamw_shard(p_down, w_down[0], m_w_down[0], v_w_down[0], 176,
                                                            "adamw_down")
    gains = [mix_pre_norm, mix_post_norm, ffn_pre_norm, ffn_post_norm]
    m_gains = [m_mix_pre_norm, m_mix_post_norm, m_ffn_pre_norm, m_ffn_post_norm]
    v_gains = [v_mix_pre_norm, v_mix_post_norm, v_ffn_pre_norm, v_ffn_post_norm]
    zeros_cw = jnp.zeros((3, 2 * D_FF), F32)
    w_small = _pack_small(gains, conv_b, zeros_cw, attn_sinks)
    m_small = _pack_small(m_gains, m_conv_b, zeros_cw, m_attn_sinks)
    v_small = _pack_small(v_gains, v_conv_b, zeros_cw, v_attn_sinks)
    d_small, nm_small, nv_small = _adamw_small(g_small, w_small, m_small, v_small, "adamw_small")
    shard_cols = 2 * D_FF // N_DEV
    g_cw = lax.dynamic_slice(g_small[76:208].reshape(3, 2 * D_FF), (0, me * shard_cols), (3, shard_cols))
    d_cw, nm_cw, nv_cw = _adamw_small(g_cw, conv_w[0], m_conv_w[0], v_conv_w[0], "adamw_conv_w")

    def unpack(p):
        gains_o = [p[8 * i:8 * i + 8].reshape(1, D_MODEL) for i in range(4)]
        return gains_o, p[32:76].reshape(1, 2 * D_FF), p[208:209, :N_ATTN_HEADS]

    def leaves(p, w_in_s, w_out_s, w_up_s, cw_s, w_down_s):
        (pre1, post1, pre2, post2), cb, sk = unpack(p)
        return [pre1, w_in_s[None], sk, w_out_s[None], post1, pre2, w_up_s[None], cw_s[None], cb, w_down_s[None],
                post2]

    return (loss, grad_x.reshape(1, T, D_MODEL),
            *leaves(g_small, g_w_in, g_w_out, g_w_up, g_cw, g_w_down),
            *leaves(d_small, d_w_in, d_w_out, d_w_up, d_cw, d_w_down),
            *leaves(nm_small, nm_w_in, nm_w_out, nm_w_up, nm_cw, nm_w_down),
            *leaves(nv_small, nv_w_in, nv_w_out, nv_w_up, nv_cw, nv_w_down))
```

```python
import functools
import math

import jax
import jax.numpy as jnp
from jax import lax
from jax.experimental import pallas as pl
from jax.experimental.pallas import tpu as pltpu

F32 = jnp.float32
BF = jnp.bfloat16

N_DEV = 8
D_MODEL = 1024
HEAD_DIM = 64
ATTN_W = 512
N_ATTN_HEADS = 8
KV_W = 128
BLOCK = 128
RET_W = 512
N_RET_HEADS = 4
RET_HEAD_DIM = 128
IN_W = 2816
D_FF = 2816
RMS_EPS = 1e-6
GN_EPS = 1e-6
MASK_VALUE = -1e30
ATTN_SCALE = HEAD_DIM ** -0.5
RET_K_SCALE = RET_HEAD_DIM ** -0.5
GELU_C = math.sqrt(2.0 / math.pi)
GELU_A = 0.044715

ADAM_LR = 0.001
ADAM_B1 = 0.9
ADAM_B2 = 0.999
ADAM_EPS = 1e-08
ADAM_WD = 0.01
ADAM_STEP = 10

VMEM_LIMIT_BYTES = 56 * 1024 * 1024
FF_SHARD = 2 * D_FF // N_DEV
FF_PAIRS = N_DEV // 2

QA0, KA0, VA0, QR0, KR0, VR0, GR0 = 0, 512, 640, 768, 1280, 1792, 2304

MESH_ID = pl.DeviceIdType.MESH


def _pcall(body, **kw):
    return pl.pallas_call(body, **kw)


def _params(sem=None):
    return pltpu.CompilerParams(dimension_semantics=sem, vmem_limit_bytes=VMEM_LIMIT_BYTES)


def _dot(a, b):
    return jnp.dot(a, b, preferred_element_type=F32)


def _dot_nt(a, b):
    return lax.dot_general(a, b, (((1,), (1,)), ((), ())), preferred_element_type=F32)


def _dot_tn(a, b):
    return lax.dot_general(a, b, (((0,), (0,)), ((), ())), preferred_element_type=F32)


def _vmem_full():
    return pl.BlockSpec(memory_space=pltpu.VMEM)


def _smem_full():
    return pl.BlockSpec(memory_space=pltpu.SMEM)


def _rows(tm, w):
    return pl.BlockSpec((tm, w), lambda i: (i, 0))


def _const(shape):
    return pl.BlockSpec(shape, lambda i: tuple(0 for _ in shape))


def _rms_stats(x):
    r = lax.rsqrt(jnp.mean(x * x, axis=-1, keepdims=True) + RMS_EPS)
    return r, x * r


def _rms_bwd(n, r, dn):
    return r * (dn - n * jnp.mean(dn * n, axis=-1, keepdims=True))


def _rot(x, even):
    w = x.shape[1]
    return jnp.where(even, pltpu.roll(x, w - 1, 1), pltpu.roll(x, 1, 1))


def _peers():
    x, y, c = lax.axis_index("x"), lax.axis_index("y"), lax.axis_index("c")
    flips = [(0, 0, 1), (1, 0, 0), (0, 1, 0), (1, 1, 0), (1, 0, 1), (0, 1, 1), (1, 1, 1)]
    peers = [(x ^ fx, y ^ fy, c ^ fc) for fx, fy, fc in flips]
    return 4 * x + 2 * y + c, peers


class _Exchange:
    def __init__(self, gathers, swaps, send_sems, recv_sems, local_sems):
        self.me, self.peers = _peers()
        self.slots = [4 * px + 2 * py + pc for px, py, pc in self.peers]
        self.pairs = [(src, dst, True) for src, dst in gathers] + [(src, dst, False) for src, dst in swaps]
        self.send_sems, self.recv_sems, self.local_sems = send_sems, recv_sems, local_sems

    @staticmethod
    def scratch(n):
        return [pltpu.SemaphoreType.DMA((n, N_DEV - 1)), pltpu.SemaphoreType.DMA((n, N_DEV - 1)),
                pltpu.SemaphoreType.DMA((n,))]

    def _parts(self, a, slot):
        src, _, whole = self.pairs[a]
        half = N_DEV // 2
        if whole:
            return [(None, src)]
        if isinstance(src, tuple):
            return [(slot < half, src[0].at[jnp.minimum(slot, half - 1)]),
                    (slot >= half, src[1].at[jnp.maximum(slot - half, 0)])]
        return [(None, src.at[slot])]

    def _local(self, a, src):
        return pltpu.make_async_copy(src, self.pairs[a][1].at[self.me], self.local_sems.at[a])

    def _remote(self, a, k, src, slot):
        return pltpu.make_async_remote_copy(
            src_ref=src, dst_ref=self.pairs[a][1].at[slot], send_sem=self.send_sems.at[a, k],
            recv_sem=self.recv_sems.at[a, k], device_id=self.peers[k], device_id_type=MESH_ID)

    def start(self):
        def go(cond, copy):
            if cond is None:
                copy.start()
            else:
                pl.when(cond)(copy.start)

        for a in range(len(self.pairs)):
            for cond, src in self._parts(a, self.me):
                go(cond, self._local(a, src))
            for k in range(N_DEV - 1):
                for cond, src in self._parts(a, self.slots[k]):
                    go(cond, self._remote(a, k, src, self.me))

    def wait(self):
        for a in range(len(self.pairs)):
            src = self._parts(a, self.me)[0][1]
            for k in range(N_DEV - 1):
                self._remote(a, k, src, self.slots[k]).wait_recv()
        for a in range(len(self.pairs)):
            src = self._parts(a, self.me)[0][1]
            for k in range(N_DEV - 1):
                self._remote(a, k, src, self.me).wait_send()
            self._local(a, src).wait()


ANY_SPEC = pl.BlockSpec(memory_space=pl.ANY)


def _exchange_shapes(gathers, swaps):
    return ([jax.ShapeDtypeStruct((N_DEV,) + a.shape, a.dtype) for a in gathers]
            + [jax.ShapeDtypeStruct(a.shape, a.dtype) for a in swaps])


def _exchange_of(ins, outs, sems, ng):
    return _Exchange(list(zip(ins[:ng], outs[:ng])), list(zip(ins[ng:], outs[ng:])), *sems)


def _exchange_call(gathers, swaps, name):
    ng, ns = len(gathers), len(swaps)
    n = ng + ns

    def body(*refs):
        ex = _exchange_of(refs[:n], refs[n:2 * n], refs[2 * n:], ng)
        ex.start()
        ex.wait()

    return _pcall(
        body, name=name, out_shape=_exchange_shapes(gathers, swaps),
        in_specs=[ANY_SPEC] * (ng + ns), out_specs=[ANY_SPEC] * (ng + ns),
        scratch_shapes=_Exchange.scratch(ng + ns),
    )(*gathers, *swaps)


def _in_proj(x, g1, w_in, cos, sin_s, tm, gathers):
    T = x.shape[0]
    ng = len(gathers)
    nt = T // tm

    def body(x_ref, g_ref, w_ref, cos_ref, sin_ref, *rest):
        ex = _exchange_of(rest[:ng], rest[ng + 8:2 * ng + 8], rest[2 * ng + 8:], ng)
        h_ref, qa_ref, ka_ref, va_ref, qr_ref, kr_ref, vr_ref, gr_ref = rest[ng:ng + 8]
        pl.when(pl.program_id(0) == 0)(ex.start)
        r, n = _rms_stats(x_ref[...])
        h = (n * g_ref[...]).astype(BF)
        h_ref[...] = h

        def proj(c0, w):
            return _dot(h, w_ref[:, c0:c0 + w])

        qa_ref[...] = proj(QA0, ATTN_W).astype(BF)
        ka_ref[...] = proj(KA0, KV_W).astype(BF)
        va_ref[...] = proj(VA0, KV_W).astype(BF)
        vr_ref[...] = proj(VR0, RET_W).astype(BF)
        gr_ref[...] = proj(GR0, RET_W)
        cos_t, sin_t = cos_ref[...], sin_ref[...]
        even = lax.broadcasted_iota(jnp.int32, (tm, RET_HEAD_DIM), 1) % 2 == 0
        for hd in range(N_RET_HEADS):
            c = hd * RET_HEAD_DIM
            q = proj(QR0 + c, RET_HEAD_DIM)
            k = proj(KR0 + c, RET_HEAD_DIM) * RET_K_SCALE
            qr_ref[:, c:c + RET_HEAD_DIM] = (q * cos_t + _rot(q, even) * sin_t).astype(BF)
            kr_ref[:, c:c + RET_HEAD_DIM] = (k * cos_t + _rot(k, even) * sin_t).astype(BF)
        pl.when(pl.program_id(0) == nt - 1)(ex.wait)

    widths = [D_MODEL, ATTN_W, KV_W, KV_W, RET_W, RET_W, RET_W, RET_W]
    dts = [BF] * 7 + [F32]
    outs = _pcall(
        body, name="in_proj", grid=(nt,),
        in_specs=[_rows(tm, D_MODEL), _const((1, D_MODEL)), _vmem_full(), _rows(tm, RET_HEAD_DIM),
                  _rows(tm, RET_HEAD_DIM)] + [ANY_SPEC] * ng,
        out_specs=[_rows(tm, w) for w in widths] + [ANY_SPEC] * ng,
        out_shape=[jax.ShapeDtypeStruct((T, w), dt) for w, dt in zip(widths, dts)] + _exchange_shapes(gathers, []),
        scratch_shapes=_Exchange.scratch(ng),
        compiler_params=_params(("arbitrary",)),
    )(x, g1, w_in, cos, sin_s, *gathers)
    return outs[:8], outs[8:]


def _kv_variants(kk):
    kf = kk.astype(F32)
    lo = lax.broadcasted_iota(jnp.int32, kf.shape, 1) < HEAD_DIM
    h0_lo = jnp.where(lo, kf, 0.0)
    h1_hi = jnp.where(lo, 0.0, kf)
    h0_hi = pltpu.roll(h0_lo, HEAD_DIM, 1)
    h1_lo = pltpu.roll(h1_hi, HEAD_DIM, 1)
    return [[h0_lo.astype(BF), h0_hi.astype(BF)], [h1_lo.astype(BF), h1_hi.astype(BF)]]


def _col_to_tile(tile, col, head):
    lane = lax.broadcasted_iota(jnp.int32, tile.shape, 1)
    return jnp.where(lane == head, col, tile)


def _tri(rows, key_major=False):
    i = lax.broadcasted_iota(jnp.int32, (rows, BLOCK), 0) & (BLOCK - 1)
    j = lax.broadcasted_iota(jnp.int32, (rows, BLOCK), 1)
    return i > j if key_major else j > i


def _fold(x2, tri, first_above):
    a, b = x2[:, :BLOCK], x2[:, BLOCK:]
    return jnp.where(tri, a, b) if first_above else jnp.where(tri, b, a)


def _unfold(x, tri, first_above):
    up, low = jnp.where(tri, x, 0.0), jnp.where(tri, 0.0, x)
    return jnp.concatenate([up, low] if first_above else [low, up], axis=1).astype(BF)


def _scaled(q):
    return (q.astype(F32) * ATTN_SCALE).astype(BF)


def _cat_variants(prev, cur):
    return [[jnp.concatenate([prev[h][e], cur[h][e]], axis=0) for e in range(2)] for h in range(2)]


def _block_variants(prev_ref, cur_ref, nbs):
    var = [_kv_variants(prev_ref[...])] + [_kv_variants(cur_ref[b * BLOCK:(b + 1) * BLOCK, :]) for b in range(nbs)]
    return [_cat_variants(var[b], var[b + 1]) for b in range(nbs)]


def _head_cols(col, nbs):
    tiles = []
    for b in range(nbs):
        t = jnp.zeros((BLOCK, BLOCK), F32)
        for head in range(N_ATTN_HEADS):
            r0 = (b * N_ATTN_HEADS + head) * BLOCK
            t = _col_to_tile(t, col[r0:r0 + BLOCK, :], head)
        tiles.append(t)
    return tiles


def _attn_fwd(sinks, qa, ka, va, nbs, gathers):
    T = qa.shape[0]
    steps = T // (BLOCK * nbs)
    R = nbs * N_ATTN_HEADS * BLOCK
    ng = len(gathers)

    def body(sink_ref, q_ref, kc_ref, kp_ref, vc_ref, vp_ref, *rest):
        ex = _exchange_of(rest[:ng], rest[ng + 3:2 * ng + 3], rest[2 * ng + 3:], ng)
        a_ref, lse_ref, lset_ref = rest[ng:ng + 3]
        n = pl.program_id(0)
        pl.when(n == 0)(ex.start)
        kcat = _block_variants(kp_ref, kc_ref, nbs)
        vcat = _block_variants(vp_ref, vc_ref, nbs)
        tri1 = _tri(BLOCK)
        tiles = []
        for b in range(nbs):
            for pair in range(N_ATTN_HEADS // 2):
                qp = _scaled(q_ref[b * BLOCK:(b + 1) * BLOCK, pair * 128:(pair + 1) * 128])
                for e in range(2):
                    s = _fold(_dot_nt(qp, kcat[b][pair // 2][e]), tri1, True)
                    if b == 0:
                        s = jnp.where(tri1 & (n == 0), MASK_VALUE, s)
                    tiles.append(s)
        s = jnp.concatenate(tiles, axis=0)
        sink = jnp.concatenate([jnp.full((BLOCK, 1), sink_ref[head], F32)
                                for _ in range(nbs) for head in range(N_ATTN_HEADS)], axis=0)
        m = jnp.maximum(jnp.max(s, axis=-1, keepdims=True), sink)
        p = jnp.exp(s - m)
        z = jnp.sum(p, axis=-1, keepdims=True) + jnp.exp(sink - m)
        p2 = _unfold(p * (1.0 / z), _tri(R), True)
        for b in range(nbs):
            for pair in range(N_ATTN_HEADS // 2):
                r0 = (b * N_ATTN_HEADS + 2 * pair) * BLOCK
                acc = (_dot(p2[r0:r0 + BLOCK, :], vcat[b][pair // 2][0])
                       + _dot(p2[r0 + BLOCK:r0 + 2 * BLOCK, :], vcat[b][pair // 2][1]))
                a_ref[b * BLOCK:(b + 1) * BLOCK, pair * 128:(pair + 1) * 128] = acc.astype(BF)
        for b, t in enumerate(_head_cols(m + jnp.log(z), nbs)):
            lse_ref[b * BLOCK:(b + 1) * BLOCK, :] = t
            lset_ref[:, b * BLOCK:(b + 1) * BLOCK] = t.T[:N_ATTN_HEADS, :]
        pl.when(n == steps - 1)(ex.wait)

    cur = lambda w: pl.BlockSpec((BLOCK * nbs, w), lambda n: (n, 0))
    prev = lambda w: pl.BlockSpec((BLOCK, w), lambda n: (jnp.maximum(n * nbs - 1, 0), 0))
    outs = _pcall(
        body, name="attn_fwd", grid=(steps,),
        in_specs=[_smem_full(), cur(ATTN_W), cur(KV_W), prev(KV_W), cur(KV_W), prev(KV_W)] + [ANY_SPEC] * ng,
        out_specs=[cur(ATTN_W), cur(BLOCK), pl.BlockSpec((N_ATTN_HEADS, BLOCK * nbs), lambda n: (0, n))]
        + [ANY_SPEC] * ng,
        out_shape=[jax.ShapeDtypeStruct((T, ATTN_W), BF), jax.ShapeDtypeStruct((T, BLOCK), F32),
                   jax.ShapeDtypeStruct((N_ATTN_HEADS, T), F32)] + _exchange_shapes(gathers, []),
        scratch_shapes=_Exchange.scratch(ng),
        compiler_params=_params(("arbitrary",)),
    )(sinks, qa, ka, ka, va, va, *gathers)
    return outs[:3], outs[3:]


def _ret_fwd(decay, qr, kr, vr, gr, d_intra, xi_b, zeta_b, ncs):
    T = qr.shape[0]
    nc = T // BLOCK
    H, C = N_RET_HEADS, RET_HEAD_DIM

    def body(decay_ref, q_ref, k_ref, v_ref, g_ref, d_ref, xi_ref, zeta_ref, o_ref, s_ref, r_ref, state):
        @pl.when(pl.program_id(0) == 0)
        def _():
            state[...] = jnp.zeros_like(state)

        for h in range(H):
            cs = slice(h * C, (h + 1) * C)
            st = state[h]
            for b in range(ncs):
                rows = slice(b * BLOCK, (b + 1) * BLOCK)
                q, k, v = q_ref[rows, cs], k_ref[rows, cs], v_ref[rows, cs]
                st_b = st.astype(BF)
                s_ref[b, h] = st_b
                inner = (_dot_nt(q, k) * d_ref[h]).astype(BF)
                o = _dot(inner, v) + _dot(q, st_b) * xi_ref[h]
                kz = (k.astype(F32) * zeta_ref[h]).astype(BF)
                st = decay_ref[h] * st + _dot_tn(kz, v)
                o_ref[rows, cs] = o
                mu = jnp.mean(o, axis=-1, keepdims=True)
                oc = o - mu
                rs = lax.rsqrt(jnp.mean(oc * oc, axis=-1, keepdims=True) + GN_EPS)
                g = g_ref[rows, cs]
                r_ref[rows, cs] = (g * jax.nn.sigmoid(g) * (oc * rs)).astype(BF)
            state[h] = st

    cur = pl.BlockSpec((BLOCK * ncs, RET_W), lambda n: (n, 0))
    tab = pl.BlockSpec((H, C, C), lambda n: (0, 0, 0))
    return _pcall(
        body, name="ret_fwd", grid=(nc // ncs,),
        in_specs=[_smem_full(), cur, cur, cur, cur, tab, tab, tab],
        out_specs=[cur, pl.BlockSpec((ncs, H, C, C), lambda n: (n, 0, 0, 0)), cur],
        out_shape=[jax.ShapeDtypeStruct((T, RET_W), F32), jax.ShapeDtypeStruct((nc, H, C, C), BF),
                   jax.ShapeDtypeStruct((T, RET_W), BF)],
        scratch_shapes=[pltpu.VMEM((H, C, C), F32)],
        compiler_params=_params(("arbitrary",)),
    )(decay, qr, kr, vr, gr, d_intra, xi_b, zeta_b)


def _out_proj(a, r, w_out, x, g2, g3, tm):
    T = x.shape[0]

    def body(a_ref, r_ref, w_ref, x_ref, g2_ref, g3_ref, mixed_ref, x1_ref, h2_ref):
        mixed = _dot(a_ref[...], w_ref[:ATTN_W, :]) + _dot(r_ref[...], w_ref[ATTN_W:, :])
        mixed_ref[...] = mixed
        _, n2 = _rms_stats(mixed)
        x1 = x_ref[...] + n2 * g2_ref[...]
        x1_ref[...] = x1
        _, n3 = _rms_stats(x1)
        h2_ref[...] = (n3 * g3_ref[...]).astype(BF)

    return _pcall(
        body, name="out_proj", grid=(T // tm,),
        in_specs=[_rows(tm, ATTN_W), _rows(tm, RET_W), _vmem_full(), _rows(tm, D_MODEL), _const((1, D_MODEL)),
                  _const((1, D_MODEL))],
        out_specs=[_rows(tm, D_MODEL)] * 3,
        out_shape=[jax.ShapeDtypeStruct((T, D_MODEL), F32), jax.ShapeDtypeStruct((T, D_MODEL), F32),
                   jax.ShapeDtypeStruct((T, D_MODEL), BF)],
        compiler_params=_params(("parallel",)),
    )(a, r, w_out, x, g2, g3)


def _shift_down(cur, k, before):
    out = pltpu.roll(cur, k, 0)
    row = lax.broadcasted_iota(jnp.int32, before.shape, 0)
    top = jnp.where(row < k, pltpu.roll(before, k, 0), out[0:8])
    return jnp.concatenate([top, out[8:]], axis=0)


def _shift_up(cur, k, after):
    tm = cur.shape[0]
    out = pltpu.roll(cur, tm - k, 0)
    row = lax.broadcasted_iota(jnp.int32, after.shape, 0)
    bot = jnp.where(row >= 8 - k, pltpu.roll(after, 8 - k, 0), out[tm - 8:])
    return jnp.concatenate([out[:tm - 8], bot], axis=0)


def _gelu_parts(x):
    x2 = x * x
    th = jnp.tanh(GELU_C * (x + GELU_A * x * x2))
    gelu = 0.5 * x * (1.0 + th)
    dgelu = 0.5 * (1.0 + th) + 0.5 * x * (1.0 - th * th) * (GELU_C * (1.0 + 3.0 * GELU_A * x2))
    return gelu, dgelu


def _ffn_fwd(h2, w_up8, conv_w8, conv_b8, w_down4, x1, g4, target, tm):
    T = h2.shape[0]
    nt = T // tm

    def body(h_ref, wu_ref, cwg_ref, cwv_ref, cbg_ref, cbv_ref, wd_ref, x1_ref, g_ref, t_ref,
             upg_ref, upv_ref, ug_ref, uv_ref, y_ref, dout_ref, dz_ref, dg4_ref, loss_ref, halo, z_acc):
        s = pl.program_id(1)
        first = pl.program_id(0) == 0

        @pl.when(first & (s == 0))
        def _():
            loss_ref[...] = jnp.zeros_like(loss_ref)
            dg4_ref[...] = jnp.zeros_like(dg4_ref)

        h = h_ref[...]
        u = []
        parts = ((cwg_ref, cbg_ref, upg_ref, ug_ref), (cwv_ref, cbv_ref, upv_ref, uv_ref))
        for part, (cw_ref, cb_ref, up_ref, u_ref) in enumerate(parts):
            cur = _dot(h, wu_ref[s + part * FF_PAIRS])
            up_ref[0] = cur.astype(BF)
            before = jnp.where(first, 0.0, halo[part, s])
            halo[part, s] = cur[tm - 8:tm, :]
            u_c = (cw_ref[0, pl.ds(0, 1), :] * _shift_down(cur, 2, before)
                   + cw_ref[0, pl.ds(1, 1), :] * _shift_down(cur, 1, before)
                   + cw_ref[0, pl.ds(2, 1), :] * cur + cb_ref[0])
            u_ref[0] = u_c
            u.append(u_c)
        gelu, _ = _gelu_parts(u[0])
        y = (gelu * u[1]).astype(BF)
        y_ref[0] = y
        z_part = _dot(y, wd_ref[s])

        @pl.when(s == 0)
        def _():
            z_acc[...] = z_part

        @pl.when(s > 0)
        def _():
            z_acc[...] += z_part

        @pl.when(s == FF_PAIRS - 1)
        def _():
            r4, n4 = _rms_stats(z_acc[...])
            err = x1_ref[...] + n4 * g_ref[...] - t_ref[...]
            dout = err * (1.0 / D_MODEL)
            dout_ref[...] = dout
            loss_ref[...] += 0.5 * jnp.sum(jnp.mean(err * err, axis=-1, keepdims=True), axis=0, keepdims=True)
            dg4_ref[...] += jnp.sum(dout * n4, axis=0, keepdims=True)
            dz_ref[...] = _rms_bwd(n4, r4, dout * g_ref[...]).astype(BF)

    rows = pl.BlockSpec((tm, D_MODEL), lambda i, s: (i, 0))
    one = lambda shape: pl.BlockSpec(shape, lambda i, s: tuple(0 for _ in shape))
    gate = lambda r, w: pl.BlockSpec((1, r, w), lambda i, s: (s, 0, 0))
    val = lambda r, w: pl.BlockSpec((1, r, w), lambda i, s: (s + FF_PAIRS, 0, 0))
    tile = pl.BlockSpec((1, tm, FF_SHARD), lambda i, s: (s, i, 0))
    half = lambda dt: jax.ShapeDtypeStruct((FF_PAIRS, T, FF_SHARD), dt)
    return _pcall(
        body, name="ffn_fwd", grid=(nt, FF_PAIRS),
        in_specs=[rows, _vmem_full(), gate(3, FF_SHARD), val(3, FF_SHARD), gate(1, FF_SHARD), val(1, FF_SHARD),
                  _vmem_full(), rows, one((1, D_MODEL)), rows],
        out_specs=[tile] * 5 + [rows, rows, one((1, D_MODEL)), one((8, 128))],
        out_shape=[half(BF), half(BF), half(F32), half(F32), half(BF), jax.ShapeDtypeStruct((T, D_MODEL), F32),
                   jax.ShapeDtypeStruct((T, D_MODEL), BF), jax.ShapeDtypeStruct((1, D_MODEL), F32),
                   jax.ShapeDtypeStruct((8, 128), F32)],
        scratch_shapes=[pltpu.VMEM((2, FF_PAIRS, 8, FF_SHARD), F32), pltpu.VMEM((tm, D_MODEL), F32)],
        compiler_params=_params(("arbitrary", "arbitrary")),
    )(h2, w_up8, conv_w8, conv_w8, conv_b8, conv_b8, w_down4, x1, g4, target)


def _ffn_bwd_a(dz, h2, w_down4, u_g, u_v, up_g, up_v, y4, conv_w8, tm):
    T = dz.shape[0]
    nt = T // tm

    def body(dz_ref, h_ref, wd_ref, ug_ref, uv_ref, upg_ref, upv_ref, y_ref, cwg_ref, cwv_ref,
             dupg_ref, dupv_ref, dcbg_ref, dcbv_ref, dcwg_ref, dcwv_ref, gwug_ref, gwuv_ref, gwd_ref, carry):
        @pl.when(pl.program_id(1) == 0)
        def _():
            for ref in (dcbg_ref, dcbv_ref, dcwg_ref, dcwv_ref, gwug_ref, gwuv_ref, gwd_ref, carry):
                ref[...] = jnp.zeros_like(ref)

        dz = dz_ref[...]
        h = h_ref[...]
        dy = _dot_nt(dz, wd_ref[0])
        gwd_ref[0] += _dot_tn(y_ref[0], dz)
        gelu, dgelu = _gelu_parts(ug_ref[0])
        parts = ((0, dy * uv_ref[0] * dgelu, upg_ref, cwg_ref, dupg_ref, dcbg_ref, dcwg_ref, gwug_ref),
                 (1, dy * gelu, upv_ref, cwv_ref, dupv_ref, dcbv_ref, dcwv_ref, gwuv_ref))
        for part, d, up_ref, cw_ref, dup_ref, dcb_ref, dcw_ref, gwu_ref in parts:
            after = carry[part]
            d1 = _shift_up(d, 1, after)
            d2 = _shift_up(d, 2, after)
            carry[part] = d[0:8, :]
            upc = up_ref[0].astype(F32)
            dcb_ref[0] += jnp.sum(d, axis=0, keepdims=True)
            dcw_ref[0, pl.ds(2, 1), :] += jnp.sum(d * upc, axis=0, keepdims=True)
            dcw_ref[0, pl.ds(1, 1), :] += jnp.sum(d1 * upc, axis=0, keepdims=True)
            dcw_ref[0, pl.ds(0, 1), :] += jnp.sum(d2 * upc, axis=0, keepdims=True)
            dup = (cw_ref[0, pl.ds(2, 1), :] * d + cw_ref[0, pl.ds(1, 1), :] * d1
                   + cw_ref[0, pl.ds(0, 1), :] * d2).astype(BF)
            dup_ref[0] = dup
            gwu_ref[0] += _dot_tn(h, dup)

    rev = pl.BlockSpec((tm, D_MODEL), lambda s, i: (nt - 1 - i, 0))
    tile = pl.BlockSpec((1, tm, FF_SHARD), lambda s, i: (s, nt - 1 - i, 0))
    acc = lambda r, w: pl.BlockSpec((1, r, w), lambda s, i: (s, 0, 0))
    acc_val = pl.BlockSpec((1, 3, FF_SHARD), lambda s, i: (s + FF_PAIRS, 0, 0))
    half = lambda r, dt: jax.ShapeDtypeStruct((FF_PAIRS, r, FF_SHARD), dt)
    return _pcall(
        body, name="ffn_bwd_a", grid=(FF_PAIRS, nt),
        in_specs=[rev, rev, acc(FF_SHARD, D_MODEL), tile, tile, tile, tile, tile, acc(3, FF_SHARD), acc_val],
        out_specs=[tile, tile, acc(1, FF_SHARD), acc(1, FF_SHARD), acc(3, FF_SHARD), acc(3, FF_SHARD),
                   acc(D_MODEL, FF_SHARD), acc(D_MODEL, FF_SHARD), acc(FF_SHARD, D_MODEL)],
        out_shape=[half(T, BF), half(T, BF), half(1, F32), half(1, F32), half(3, F32), half(3, F32),
                   half(D_MODEL, F32), half(D_MODEL, F32), jax.ShapeDtypeStruct((FF_PAIRS, FF_SHARD, D_MODEL), F32)],
        scratch_shapes=[pltpu.VMEM((2, 8, FF_SHARD), F32)],
        compiler_params=_params(("arbitrary", "arbitrary")),
    )(dz, h2, w_down4, u_g, u_v, up_g, up_v, y4, conv_w8, conv_w8)


def _ffn_bwd_b(dup_g, dup_v, w_up8, x1, dout, g3, mixed, g2, w_out, tm, gwu_g, gwu_v):
    T = x1.shape[0]
    nt = T // tm

    def body(dupg_ref, dupv_ref, wup_ref, x1_ref, dout_ref, g3_ref, mixed_ref, g2_ref, wout_ref, gwug_ref, gwuv_ref,
             dx1_ref, dmixed_ref, da_ref, dr_ref, dg3_ref, dg2_ref, pup_ref, *sems):
        ex = _Exchange([], [((gwug_ref, gwuv_ref), pup_ref)], *sems)

        @pl.when(pl.program_id(0) == 0)
        def _():
            ex.start()
            dg3_ref[...] = jnp.zeros_like(dg3_ref)
            dg2_ref[...] = jnp.zeros_like(dg2_ref)

        dh2 = jnp.zeros((tm, D_MODEL), F32)
        for s in range(FF_PAIRS):
            dh2 = dh2 + _dot_nt(dupg_ref[s], wup_ref[s]) + _dot_nt(dupv_ref[s], wup_ref[s + FF_PAIRS])
        r3, n3 = _rms_stats(x1_ref[...])
        dg3_ref[...] += jnp.sum(dh2 * n3, axis=0, keepdims=True)
        dx1 = dout_ref[...] + _rms_bwd(n3, r3, dh2 * g3_ref[...])
        dx1_ref[...] = dx1
        r2, n2 = _rms_stats(mixed_ref[...])
        dg2_ref[...] += jnp.sum(dx1 * n2, axis=0, keepdims=True)
        dmixed = _rms_bwd(n2, r2, dx1 * g2_ref[...]).astype(BF)
        dmixed_ref[...] = dmixed
        da_ref[...] = _dot_nt(dmixed, wout_ref[:ATTN_W, :])
        dr_ref[...] = _dot_nt(dmixed, wout_ref[ATTN_W:, :])
        pl.when(pl.program_id(0) == nt - 1)(ex.wait)

    half = pl.BlockSpec((FF_PAIRS, tm, FF_SHARD), lambda i: (0, i, 0))
    outs = _pcall(
        body, name="ffn_bwd_b", grid=(nt,),
        in_specs=[half, half, _vmem_full(), _rows(tm, D_MODEL), _rows(tm, D_MODEL), _const((1, D_MODEL)),
                  _rows(tm, D_MODEL), _const((1, D_MODEL)), _vmem_full(), ANY_SPEC, ANY_SPEC],
        out_specs=[_rows(tm, D_MODEL), _rows(tm, D_MODEL), _rows(tm, ATTN_W), _rows(tm, RET_W),
                   _const((1, D_MODEL)), _const((1, D_MODEL)), ANY_SPEC],
        out_shape=[jax.ShapeDtypeStruct((T, D_MODEL), F32), jax.ShapeDtypeStruct((T, D_MODEL), BF),
                   jax.ShapeDtypeStruct((T, ATTN_W), F32), jax.ShapeDtypeStruct((T, RET_W), F32),
                   jax.ShapeDtypeStruct((1, D_MODEL), F32), jax.ShapeDtypeStruct((1, D_MODEL), F32),
                   jax.ShapeDtypeStruct((N_DEV, D_MODEL, FF_SHARD), F32)],
        scratch_shapes=_Exchange.scratch(1),
        compiler_params=_params(("arbitrary",)),
    )(dup_g, dup_v, w_up8, x1, dout, g3, mixed, g2, w_out, gwu_g, gwu_v)
    return outs[:6], outs[6]


def _ret_bwd(decay, qr, kr, vr, gr, o, states, dr, d_intra, d_intra_t, xi_b, zeta_b, cos, sin_s, swaps, ncs):
    T = qr.shape[0]
    nc = T // BLOCK
    H, C = N_RET_HEADS, RET_HEAD_DIM
    ns = len(swaps)

    def body(decay_ref, q_ref, k_ref, v_ref, g_ref, o_ref, s_ref, dr_ref, d_ref, dt_ref, xi_ref, zeta_ref,
             cos_ref, sin_ref, *rest):
        ex = _exchange_of(rest[:ns], rest[ns + 1:2 * ns + 1], rest[2 * ns + 2:], 0)
        dret_ref, gstate = rest[ns], rest[2 * ns + 1]

        @pl.when(pl.program_id(0) == 0)
        def _():
            ex.start()
            gstate[...] = jnp.zeros_like(gstate)

        even = lax.broadcasted_iota(jnp.int32, (BLOCK, C), 1) % 2 == 0
        for h in range(H):
            cs = slice(h * C, (h + 1) * C)
            gst = gstate[h]
            for b in reversed(range(ncs)):
                rows = slice(b * BLOCK, (b + 1) * BLOCK)
                cos_t, sin_t = cos_ref[rows, :], sin_ref[rows, :]
                q, k, v = q_ref[rows, cs], k_ref[rows, cs], v_ref[rows, cs]
                g, o_h, dr_h = g_ref[rows, cs], o_ref[rows, cs], dr_ref[rows, cs]
                mu = jnp.mean(o_h, axis=-1, keepdims=True)
                oc = o_h - mu
                rs = lax.rsqrt(jnp.mean(oc * oc, axis=-1, keepdims=True) + GN_EPS)
                on = oc * rs
                sg = jax.nn.sigmoid(g)
                dg = dr_h * on * (sg * (1.0 + g * (1.0 - sg)))
                don = dr_h * (g * sg)
                do = rs * (don - jnp.mean(don, axis=-1, keepdims=True)
                           - on * jnp.mean(don * on, axis=-1, keepdims=True))
                do_b = do.astype(BF)
                dox_b = (do * xi_ref[h]).astype(BF)
                gst_b = gst.astype(BF)
                kz = (k.astype(F32) * zeta_ref[h]).astype(BF)
                da_b = (_dot_nt(do_b, v) * d_ref[h]).astype(BF)
                dat_b = (_dot_nt(v, do_b) * dt_ref[h]).astype(BF)
                mt_b = (_dot_nt(k, q) * dt_ref[h]).astype(BF)
                dq = _dot(da_b, k) + _dot_nt(dox_b, s_ref[b, h])
                dk = _dot(dat_b, q) + _dot_nt(v, gst_b) * zeta_ref[h]
                dv = _dot(mt_b, do_b) + _dot(kz, gst_b)
                gst = decay_ref[h] * gst + _dot_tn(q, dox_b)
                dq = dq * cos_t - _rot(dq, even) * sin_t
                dk = (dk * cos_t - _rot(dk, even) * sin_t) * RET_K_SCALE
                dret_ref[rows, h * C:(h + 1) * C] = dq.astype(BF)
                dret_ref[rows, RET_W + h * C:RET_W + (h + 1) * C] = dk.astype(BF)
                dret_ref[rows, 2 * RET_W + h * C:2 * RET_W + (h + 1) * C] = dv.astype(BF)
                dret_ref[rows, 3 * RET_W + h * C:3 * RET_W + (h + 1) * C] = dg.astype(BF)
            gstate[h] = gst
        pl.when(pl.program_id(0) == steps - 1)(ex.wait)

    steps = nc // ncs
    rev = lambda w: pl.BlockSpec((BLOCK * ncs, w), lambda n: (steps - 1 - n, 0))
    tab = pl.BlockSpec((H, C, C), lambda n: (0, 0, 0))
    outs = _pcall(
        body, name="ret_bwd", grid=(steps,),
        in_specs=[_smem_full(), rev(RET_W), rev(RET_W), rev(RET_W), rev(RET_W), rev(RET_W),
                  pl.BlockSpec((ncs, H, C, C), lambda n: (steps - 1 - n, 0, 0, 0)), rev(RET_W), tab, tab, tab, tab,
                  rev(C), rev(C)] + [ANY_SPEC] * ns,
        out_specs=[rev(4 * RET_W)] + [ANY_SPEC] * ns,
        out_shape=[jax.ShapeDtypeStruct((T, 4 * RET_W), BF)] + _exchange_shapes([], swaps),
        scratch_shapes=[pltpu.VMEM((H, C, C), F32)] + _Exchange.scratch(ns),
        compiler_params=_params(("arbitrary",)),
    )(decay, qr, kr, vr, gr, o, states, dr, d_intra, d_intra_t, xi_b, zeta_b, cos, sin_s, *swaps)
    return outs[0], outs[1:]


def _attn_bwd_dq(sinks, qa, ka, va, da, lse, nbs, swaps):
    T = qa.shape[0]
    steps = T // (BLOCK * nbs)
    R = nbs * N_ATTN_HEADS * BLOCK
    ns = len(swaps)

    def body(sink_ref, q_ref, kc_ref, kp_ref, vc_ref, vp_ref, da_ref, lse_ref, *rest):
        ex = _exchange_of(rest[:ns], rest[ns + 3:2 * ns + 3], rest[2 * ns + 3:], 0)
        dq_ref, deltat_ref, dsink_ref = rest[ns:ns + 3]
        n = pl.program_id(0)

        @pl.when(n == 0)
        def _():
            ex.start()
            dsink_ref[...] = jnp.zeros_like(dsink_ref)

        kcat = _block_variants(kp_ref, kc_ref, nbs)
        vcat = _block_variants(vp_ref, vc_ref, nbs)
        tri1 = _tri(BLOCK)
        lane = lax.broadcasted_iota(jnp.int32, (BLOCK, BLOCK), 1)
        s_tiles, dp_tiles, lse_cols = [], [], []
        for b in range(nbs):
            rows = slice(b * BLOCK, (b + 1) * BLOCK)
            lse_tile = lse_ref[rows, :]
            for pair in range(N_ATTN_HEADS // 2):
                qp = _scaled(q_ref[rows, pair * 128:(pair + 1) * 128])
                dop = da_ref[rows, pair * 128:(pair + 1) * 128].astype(BF)
                for e in range(2):
                    s = _fold(_dot_nt(qp, kcat[b][pair // 2][e]), tri1, True)
                    if b == 0:
                        s = jnp.where(tri1 & (n == 0), MASK_VALUE, s)
                    s_tiles.append(s)
                    dp_tiles.append(_fold(_dot_nt(dop, vcat[b][pair // 2][e]), tri1, True))
                    lse_cols.append(jnp.sum(jnp.where(lane == 2 * pair + e, lse_tile, 0.0), axis=-1, keepdims=True))
        lse_c = jnp.concatenate(lse_cols, axis=0)
        p = jnp.exp(jnp.concatenate(s_tiles, axis=0) - lse_c)
        dp = jnp.concatenate(dp_tiles, axis=0)
        delta = jnp.sum(p * dp, axis=-1, keepdims=True)
        ds2 = _unfold(p * (dp - delta), _tri(R), True)
        for b in range(nbs):
            for pair in range(N_ATTN_HEADS // 2):
                r0 = (b * N_ATTN_HEADS + 2 * pair) * BLOCK
                acc = (_dot(ds2[r0:r0 + BLOCK, :], kcat[b][pair // 2][0])
                       + _dot(ds2[r0 + BLOCK:r0 + 2 * BLOCK, :], kcat[b][pair // 2][1]))
                dq_ref[b * BLOCK:(b + 1) * BLOCK, pair * 128:(pair + 1) * 128] = (acc * ATTN_SCALE).astype(BF)
        for b, t in enumerate(_head_cols(delta, nbs)):
            deltat_ref[:, b * BLOCK:(b + 1) * BLOCK] = t.T[:N_ATTN_HEADS, :]
        sink = jnp.concatenate([jnp.full((BLOCK, 1), sink_ref[head], F32)
                                for _ in range(nbs) for head in range(N_ATTN_HEADS)], axis=0)
        ds_sink = -jnp.exp(sink - lse_c) * delta
        row8 = lax.broadcasted_iota(jnp.int32, (N_ATTN_HEADS, BLOCK), 0)
        dsink = jnp.zeros((N_ATTN_HEADS, BLOCK), F32)
        for b in range(nbs):
            for head in range(N_ATTN_HEADS):
                r0 = (b * N_ATTN_HEADS + head) * BLOCK
                dsink = dsink + jnp.where(row8 == head, jnp.sum(ds_sink[r0:r0 + BLOCK, :], axis=0, keepdims=True), 0.0)
        dsink_ref[...] += dsink
        pl.when(n == steps - 1)(ex.wait)

    cur = lambda w: pl.BlockSpec((BLOCK * nbs, w), lambda n: (n, 0))
    prev = lambda w: pl.BlockSpec((BLOCK, w), lambda n: (jnp.maximum(n * nbs - 1, 0), 0))
    outs = _pcall(
        body, name="attn_bwd_dq", grid=(steps,),
        in_specs=[_smem_full(), cur(ATTN_W), cur(KV_W), prev(KV_W), cur(KV_W), prev(KV_W), cur(ATTN_W), cur(BLOCK)]
        + [ANY_SPEC] * ns,
        out_specs=[cur(ATTN_W), pl.BlockSpec((N_ATTN_HEADS, BLOCK * nbs), lambda n: (0, n)),
                   _const((N_ATTN_HEADS, BLOCK))] + [ANY_SPEC] * ns,
        out_shape=[jax.ShapeDtypeStruct((T, ATTN_W), BF), jax.ShapeDtypeStruct((N_ATTN_HEADS, T), F32),
                   jax.ShapeDtypeStruct((N_ATTN_HEADS, BLOCK), F32)] + _exchange_shapes([], swaps),
        scratch_shapes=_Exchange.scratch(ns),
        compiler_params=_params(("arbitrary",)),
    )(sinks, qa, ka, ka, va, va, da, lse, *swaps)
    return outs[:3], outs[3:]


def _attn_bwd_dkv(qa, ka, va, da, lse_t, delta_t, nbs):
    T = qa.shape[0]
    nb = T // BLOCK
    steps = nb // nbs
    R = nbs * N_ATTN_HEADS * BLOCK

    def body(qc_ref, qn_ref, dac_ref, dan_ref, k_ref, v_ref, lc_ref, ln_ref, dc_ref, dn_ref, dk_ref, dv_ref):
        n = pl.program_id(0)
        tri1 = _tri(BLOCK, True)
        lo = lax.broadcasted_iota(jnp.int32, (BLOCK, 128), 1) < HEAD_DIM
        kv = [_kv_variants(k_ref[b * BLOCK:(b + 1) * BLOCK, :]) for b in range(nbs)]
        vv = [_kv_variants(v_ref[b * BLOCK:(b + 1) * BLOCK, :]) for b in range(nbs)]
        qcat, docat = [], []
        s_tiles, dp_tiles, lse_tiles, delta_tiles = [], [], [], []
        for b in range(nbs):
            rows = slice(b * BLOCK, (b + 1) * BLOCK)
            nrows = slice((b + 1) * BLOCK, (b + 2) * BLOCK)
            inside = b < nbs - 1
            for pair in range(N_ATTN_HEADS // 2):
                ps = slice(pair * 128, (pair + 1) * 128)
                q2 = _scaled(jnp.concatenate([qc_ref[rows, ps], qc_ref[nrows, ps] if inside else qn_ref[:, ps]], axis=0))
                do2 = jnp.concatenate([dac_ref[rows, ps], dac_ref[nrows, ps] if inside else dan_ref[:, ps]],
                                      axis=0).astype(BF)
                qcat.append(q2)
                docat.append(do2)
                for e in range(2):
                    one = pl.ds(2 * pair + e, 1)
                    s = _fold(_dot_nt(kv[b][pair // 2][e], q2), tri1, False)
                    if not inside:
                        s = jnp.where(tri1 & (n == steps - 1), MASK_VALUE, s)
                    s_tiles.append(s)
                    dp_tiles.append(_fold(_dot_nt(vv[b][pair // 2][e], do2), tri1, False))
                    lse_tiles.append(jnp.where(tri1, lc_ref[one, nrows] if inside else ln_ref[one, :], lc_ref[one, rows]))
                    delta_tiles.append(jnp.where(tri1, dc_ref[one, nrows] if inside else dn_ref[one, :],
                                                 dc_ref[one, rows]))
        pt = jnp.exp(jnp.concatenate(s_tiles, axis=0) - jnp.concatenate(lse_tiles, axis=0))
        dst = pt * (jnp.concatenate(dp_tiles, axis=0) - jnp.concatenate(delta_tiles, axis=0))
        tri = _tri(R, True)
        pt2 = _unfold(pt, tri, False)
        dst2 = _unfold(dst, tri, False)
        for b in range(nbs):
            dk = jnp.zeros((BLOCK, 128), F32)
            dv = jnp.zeros((BLOCK, 128), F32)
            for pair in range(N_ATTN_HEADS // 2):
                h = pair // 2
                for e in range(2):
                    r0 = (b * N_ATTN_HEADS + 2 * pair + e) * BLOCK
                    half = lo if e == 0 else jnp.logical_not(lo)
                    dv_e = jnp.where(half, _dot(pt2[r0:r0 + BLOCK, :], docat[b * 4 + pair]), 0.0)
                    dk_e = jnp.where(half, _dot(dst2[r0:r0 + BLOCK, :], qcat[b * 4 + pair]), 0.0)
                    if e != h:
                        dv_e = pltpu.roll(dv_e, HEAD_DIM, 1)
                        dk_e = pltpu.roll(dk_e, HEAD_DIM, 1)
                    dv = dv + dv_e
                    dk = dk + dk_e
            dk_ref[b * BLOCK:(b + 1) * BLOCK, :] = dk.astype(BF)
            dv_ref[b * BLOCK:(b + 1) * BLOCK, :] = dv.astype(BF)

    cur = lambda w: pl.BlockSpec((BLOCK * nbs, w), lambda n: (n, 0))
    nxt = lambda w: pl.BlockSpec((BLOCK, w), lambda n: (jnp.minimum((n + 1) * nbs, nb - 1), 0))
    tcur = pl.BlockSpec((N_ATTN_HEADS, BLOCK * nbs), lambda n: (0, n))
    tnxt = pl.BlockSpec((N_ATTN_HEADS, BLOCK), lambda n: (0, jnp.minimum((n + 1) * nbs, nb - 1)))
    return _pcall(
        body, name="attn_bwd_dkv", grid=(steps,),
        in_specs=[cur(ATTN_W), nxt(ATTN_W), cur(ATTN_W), nxt(ATTN_W), cur(KV_W), cur(KV_W), tcur, tnxt, tcur, tnxt],
        out_specs=[cur(KV_W), cur(KV_W)],
        out_shape=[jax.ShapeDtypeStruct((T, KV_W), BF), jax.ShapeDtypeStruct((T, KV_W), BF)],
        compiler_params=_params(("parallel",)),
    )(qa, qa, da, da, ka, va, lse_t, lse_t, delta_t, delta_t)


def _in_proj_bwd(dqa, dka, dva, dret, w_in, x, g1, dx1, tm):
    T = x.shape[0]

    def body(dqa_ref, dka_ref, dva_ref, dret_ref, w_ref, x_ref, g_ref, dx1_ref, dx_ref, dg1_ref):
        @pl.when(pl.program_id(0) == 0)
        def _():
            dg1_ref[...] = jnp.zeros_like(dg1_ref)

        dh = (_dot_nt(dqa_ref[...], w_ref[:, QA0:QA0 + ATTN_W]) + _dot_nt(dka_ref[...], w_ref[:, KA0:KA0 + KV_W])
              + _dot_nt(dva_ref[...], w_ref[:, VA0:VA0 + KV_W]) + _dot_nt(dret_ref[...], w_ref[:, QR0:IN_W]))
        r, n = _rms_stats(x_ref[...])
        dg1_ref[...] += jnp.sum(dh * n, axis=0, keepdims=True)
        dx_ref[...] = dx1_ref[...] + _rms_bwd(n, r, dh * g_ref[...])

    return _pcall(
        body, name="in_proj_bwd", grid=(T // tm,),
        in_specs=[_rows(tm, ATTN_W), _rows(tm, KV_W), _rows(tm, KV_W), _rows(tm, 4 * RET_W), _vmem_full(),
                  _rows(tm, D_MODEL), _const((1, D_MODEL)), _rows(tm, D_MODEL)],
        out_specs=[_rows(tm, D_MODEL), _const((1, D_MODEL))],
        out_shape=[jax.ShapeDtypeStruct((T, D_MODEL), F32), jax.ShapeDtypeStruct((1, D_MODEL), F32)],
        compiler_params=_params(("arbitrary",)),
    )(dqa, dka, dva, dret, w_in, x, g1, dx1)


def _wgrad(a_list, b_list, tk, name):
    T = a_list[0].shape[0]
    na, nbb = len(a_list), len(b_list)
    m_sizes = [a.shape[1] for a in a_list]
    n_sizes = [b.shape[1] for b in b_list]
    M, N = sum(m_sizes), sum(n_sizes)
    nk = T // tk
    chunk = 512

    def body(*refs):
        a_refs, b_refs = refs[:na], refs[na:na + nbb]
        out_ref, acc = refs[na + nbb], refs[na + nbb + 1]
        k = pl.program_id(0)

        @pl.when(k == 0)
        def _():
            acc[...] = jnp.zeros_like(acc)

        r0 = 0
        for ai in range(na):
            a = a_refs[ai][...]
            c0 = 0
            for bi in range(nbb):
                for s in range(0, n_sizes[bi], chunk):
                    w = min(chunk, n_sizes[bi] - s)
                    acc[r0:r0 + m_sizes[ai], c0 + s:c0 + s + w] += _dot_tn(a, b_refs[bi][:, s:s + w])
                c0 += n_sizes[bi]
            r0 += m_sizes[ai]

        @pl.when(k == nk - 1)
        def _():
            pltpu.sync_copy(acc, out_ref)

    return _pcall(
        body, name=name, grid=(nk,),
        in_specs=[_rows(tk, w) for w in m_sizes + n_sizes],
        out_specs=pl.BlockSpec(memory_space=pl.ANY),
        out_shape=jax.ShapeDtypeStruct((M, N), F32),
        scratch_shapes=[pltpu.VMEM((M, N), F32)],
        compiler_params=_params(("arbitrary",)),
    )(*a_list, *b_list)


def _adamw_math(w, g, m, v):
    m = ADAM_B1 * m + (1.0 - ADAM_B1) * g
    v = ADAM_B2 * v + (1.0 - ADAM_B2) * (g * g)
    m_hat = m / (1.0 - ADAM_B1 ** ADAM_STEP)
    v_hat = v / (1.0 - ADAM_B2 ** ADAM_STEP)
    delta = -ADAM_LR * (m_hat / (jnp.sqrt(v_hat) + ADAM_EPS) + ADAM_WD * w)
    return delta, m, v


def _sum_parts(parts_ref):
    g = parts_ref[0].astype(F32)
    for i in range(1, N_DEV):
        g = g + parts_ref[i].astype(F32)
    return g


def _adamw_shard(parts, w, m, v, tr, name):
    R, C = w.shape

    def body(p_ref, w_ref, m_ref, v_ref, g_ref, d_ref, nm_ref, nv_ref):
        g = _sum_parts(p_ref)
        g_ref[...] = g
        d_ref[...], nm_ref[...], nv_ref[...] = _adamw_math(w_ref[...], g, m_ref[...], v_ref[...])

    blk = pl.BlockSpec((tr, C), lambda i: (i, 0))
    return _pcall(
        body, name=name, grid=(R // tr,),
        in_specs=[pl.BlockSpec((N_DEV, tr, C), lambda i: (0, i, 0)), blk, blk, blk],
        out_specs=[blk] * 4,
        out_shape=[jax.ShapeDtypeStruct((R, C), F32)] * 4,
        compiler_params=_params(("parallel",)),
    )(parts, w, m, v)


def _sum_small(parts):
    def body(p_ref, g_ref):
        g_ref[...] = _sum_parts(p_ref)

    return _pcall(body, name="sum_small", out_shape=jax.ShapeDtypeStruct(parts.shape[1:], F32),
                  in_specs=[_vmem_full()], out_specs=_vmem_full())(parts)


def _adamw_small(g, w, m, v, name):
    def body(g_ref, w_ref, m_ref, v_ref, d_ref, nm_ref, nv_ref):
        d_ref[...], nm_ref[...], nv_ref[...] = _adamw_math(w_ref[...], g_ref[...], m_ref[...], v_ref[...])

    return _pcall(body, name=name, out_shape=[jax.ShapeDtypeStruct(w.shape, F32)] * 3,
                  in_specs=[_vmem_full()] * 4, out_specs=[_vmem_full()] * 3)(g, w, m, v)


def _tables(T):
    h, c = N_RET_HEADS, BLOCK
    pos = jnp.arange(T, dtype=F32)
    angle = 1.0 / jnp.power(10000.0, jnp.linspace(0.0, 1.0, RET_HEAD_DIM // 2, dtype=F32))
    angle = jnp.repeat(angle, 2)
    sin = jnp.sin(pos[:, None] * angle[None])
    cos = jnp.cos(pos[:, None] * angle[None])
    even = (jnp.arange(RET_HEAD_DIM) % 2 == 0)[None, :]
    sin_s = jnp.where(even, -sin, sin)
    log_gamma = jnp.log(1.0 - jnp.power(2.0, -5.0 - jnp.arange(h, dtype=F32)))
    idx = jnp.arange(c, dtype=F32)
    rel = idx[:, None] - idx[None, :]
    d_intra = jnp.where(rel[None] >= 0, jnp.exp(log_gamma[:, None, None] * jnp.maximum(rel, 0.0)[None]), 0.0)
    xi = jnp.exp(log_gamma[None, :] * (idx[:, None] + 1.0))
    zeta = jnp.exp(log_gamma[None, :] * (c - 1.0 - idx[:, None]))
    decay = jnp.exp(log_gamma * c)
    xi_b = jnp.broadcast_to(xi.T[:, :, None], (h, c, RET_HEAD_DIM))
    zeta_b = jnp.broadcast_to(zeta.T[:, :, None], (h, c, RET_HEAD_DIM))
    return cos, sin_s, d_intra, jnp.swapaxes(d_intra, 1, 2), xi_b, zeta_b, decay


def _to_shards(full, cols):
    r = full.shape[0]
    return jnp.swapaxes(full.reshape(r, N_DEV, cols), 0, 1)


def _from_shards(sh):
    n, r, cols = sh.shape
    return jnp.swapaxes(sh, 0, 1).reshape(r, n * cols)


SMALL_ROWS = 216


def _pack_small(gains, conv_b, conv_w, sinks):
    parts = [g.reshape(8, 128) for g in gains] + [conv_b.reshape(44, 128), conv_w.reshape(132, 128),
                                                  jnp.pad(sinks.reshape(1, 8), ((0, 0), (0, 120)))]
    packed = jnp.concatenate(parts, axis=0)
    return jnp.pad(packed, ((0, SMALL_ROWS - packed.shape[0]), (0, 0)))


def kernel(x, mix_pre_norm, w_in, attn_sinks, w_out, mix_post_norm, ffn_pre_norm, w_up, conv_w, conv_b, w_down, ffn_post_norm, loss_target, m_mix_pre_norm, m_w_in, m_attn_sinks, m_w_out, m_mix_post_norm, m_ffn_pre_norm, m_w_up, m_conv_w, m_conv_b, m_w_down, m_ffn_post_norm, v_mix_pre_norm, v_w_in, v_attn_sinks, v_w_out, v_mix_post_norm, v_ffn_pre_norm, v_w_up, v_conv_w, v_conv_b, v_w_down, v_ffn_post_norm):
    T = x.shape[1]
    tm = min(512, T)
    tm_big = min(1024, T)
    nbs = min(4, T // BLOCK)
    x2 = x.reshape(T, D_MODEL)
    target = loss_target.reshape(T, D_MODEL)
    me = 4 * lax.axis_index("x") + 2 * lax.axis_index("y") + lax.axis_index("c")

    g_in, g_cw = _exchange_call([w_in[0].astype(BF), conv_w[0]], [], "gather_w_in")
    w_in_f = _from_shards(g_in)
    cos, sin_s, d_intra, d_intra_t, xi_b, zeta_b, decay = _tables(T)
    sinks = attn_sinks.reshape(N_ATTN_HEADS)

    (h1, qa, ka, va, qr, kr, vr, gr), (w_up8,) = _in_proj(
        x2, mix_pre_norm, w_in_f, cos, sin_s, tm_big, [w_up[0].astype(BF)])
    (a, lse, lse_t), (g_down, g_out) = _attn_fwd(sinks, qa, ka, va, nbs,
                                                 [w_down[0].astype(BF), w_out[0].astype(BF)])
    w_out_f = g_out.reshape(D_MODEL, D_MODEL)
    o, states, r = _ret_fwd(decay, qr, kr, vr, gr, d_intra, xi_b, zeta_b, nbs)
    mixed, x1, h2 = _out_proj(a, r, w_out_f, x2, mix_post_norm, ffn_pre_norm, tm_big)
    w_down4 = g_down.reshape(FF_PAIRS, FF_SHARD, D_MODEL)
    up_g, up_v, u_g, u_v, y4, dout, dz, dg4, loss_acc = _ffn_fwd(
        h2, w_up8, g_cw, conv_b.reshape(N_DEV, 1, FF_SHARD), w_down4, x1, ffn_post_norm, target, tm)
    loss = lax.psum(loss_acc[0, 0], ("x", "y", "c"))

    dup_g, dup_v, dcb_g, dcb_v, dcw_g, dcw_v, gwu_g, gwu_v, gw_down4 = _ffn_bwd_a(
        dz, h2, w_down4, u_g, u_v, up_g, up_v, y4, g_cw, tm)
    dcb = jnp.concatenate([dcb_g, dcb_v], axis=0).reshape(1, 2 * D_FF)
    dcw = _from_shards(jnp.concatenate([dcw_g, dcw_v], axis=0))
    gw_down = gw_down4.reshape(D_FF, D_MODEL)
    (dx1, dmixed, da, dr, dg3, dg2), p_up = _ffn_bwd_b(
        dup_g, dup_v, w_up8, x1, dout, ffn_pre_norm, mixed, mix_post_norm, w_out_f, tm, gwu_g, gwu_v)
    gw_out = _wgrad([a, r], [dmixed], tm, "wgrad_out")
    dret, (p_down,) = _ret_bwd(decay, qr, kr, vr, gr, o, states, dr, d_intra, d_intra_t, xi_b, zeta_b, cos, sin_s,
                               [gw_down.reshape(N_DEV, D_FF // N_DEV, D_MODEL)], nbs)
    (dqa, delta_t, dsink), (p_out,) = _attn_bwd_dq(sinks, qa, ka, va, da, lse, nbs,
                                                   [gw_out.reshape(N_DEV, D_MODEL // N_DEV, D_MODEL)])
    dka, dva = _attn_bwd_dkv(qa, ka, va, da, lse_t, delta_t, nbs)
    grad_x, dg1 = _in_proj_bwd(dqa, dka, dva, dret, w_in_f, x2, mix_pre_norm, dx1, tm)
    gw_in = _wgrad([h1], [dqa, dka, dva, dret], tm, "wgrad_in")

    small = _pack_small([dg1, dg2, dg3, dg4], dcb, dcw, dsink[:, 0])
    small_all, p_in = _exchange_call([small], [_to_shards(gw_in, IN_W // N_DEV).astype(BF)], "exchange_last")
    g_small = _sum_small(small_all)

    g_w_in, d_w_in, nm_w_in, nv_w_in = _adamw_shard(p_in, w_in[0], m_w_in[0], v_w_in[0], 256, "adamw_in")
    g_w_up, d_w_up, nm_w_up, nv_w_up = _adamw_shard(p_up, w_up[0], m_w_up[0], v_w_up[0], 256, "adamw_up")
    g_w_out, d_w_out, nm_w_out, nv_w_out = _adamw_shard(p_out, w_out[0], m_w_out[0], v_w_out[0], 128, "adamw_out")
    g_w_down, d_w_down, nm_w_down, nv_w_down = _adamw_shard(p_down, w_down[0], m_w_down[0], v_w_down[0], 176,
                                                            "adamw_down")
    gains = [mix_pre_norm, mix_post_norm, ffn_pre_norm, ffn_post_norm]
    m_gains = [m_mix_pre_norm, m_mix_post_norm, m_ffn_pre_norm, m_ffn_post_norm]
    v_gains = [v_mix_pre_norm, v_mix_post_norm, v_ffn_pre_norm, v_ffn_post_norm]
    zeros_cw = jnp.zeros((3, 2 * D_FF), F32)
    w_small = _pack_small(gains, conv_b, zeros_cw, attn_sinks)
    m_small = _pack_small(m_gains, m_conv_b, zeros_cw, m_attn_sinks)
    v_small = _pack_small(v_gains, v_conv_b, zeros_cw, v_attn_sinks)
    d_small, nm_small, nv_small = _adamw_small(g_small, w_small, m_small, v_small, "adamw_small")
    shard_cols = 2 * D_FF // N_DEV
    g_cw = lax.dynamic_slice(g_small[76:208].reshape(3, 2 * D_FF), (0, me * shard_cols), (3, shard_cols))
    d_cw, nm_cw, nv_cw = _adamw_small(g_cw, conv_w[0], m_conv_w[0], v_conv_w[0], "adamw_conv_w")

    def unpack(p):
        gains_o = [p[8 * i:8 * i + 8].reshape(1, D_MODEL) for i in range(4)]
        return gains_o, p[32:76].reshape(1, 2 * D_FF), p[208:209, :N_ATTN_HEADS]

    def leaves(p, w_in_s, w_out_s, w_up_s, cw_s, w_down_s):
        (pre1, post1, pre2, post2), cb, sk = unpack(p)
        return [pre1, w_in_s[None], sk, w_out_s[None], post1, pre2, w_up_s[None], cw_s[None], cb, w_down_s[None],
                post2]

    return (loss, grad_x.reshape(1, T, D_MODEL),
            *leaves(g_small, g_w_in, g_w_out, g_w_up, g_cw, g_w_down),
            *leaves(d_small, d_w_in, d_w_out, d_w_up, d_cw, d_w_down),
            *leaves(nm_small, nm_w_in, nm_w_out, nm_w_up, nm_cw, nm_w_down),
            *leaves(nv_small, nv_w_in, nv_w_out, nv_w_up, nv_cw, nv_w_down))
```

```python
import functools
import math

import jax
import jax.numpy as jnp
from jax import lax
from jax.experimental import pallas as pl
from jax.experimental.pallas import tpu as pltpu

F32 = jnp.float32
BF = jnp.bfloat16

N_DEV = 8
D_MODEL = 1024
HEAD_DIM = 64
ATTN_W = 512
N_ATTN_HEADS = 8
KV_W = 128
BLOCK = 128
RET_W = 512
N_RET_HEADS = 4
RET_HEAD_DIM = 128
IN_W = 2816
D_FF = 2816
RMS_EPS = 1e-6
GN_EPS = 1e-6
MASK_VALUE = -1e30
ATTN_SCALE = HEAD_DIM ** -0.5
RET_K_SCALE = RET_HEAD_DIM ** -0.5
GELU_C = math.sqrt(2.0 / math.pi)
GELU_A = 0.044715

ADAM_LR = 0.001
ADAM_B1 = 0.9
ADAM_B2 = 0.999
ADAM_EPS = 1e-08
ADAM_WD = 0.01
ADAM_STEP = 10

VMEM_LIMIT_BYTES = 56 * 1024 * 1024
FF_SHARD = 2 * D_FF // N_DEV
FF_PAIRS = N_DEV // 2

QA0, KA0, VA0, QR0, KR0, VR0, GR0 = 0, 512, 640, 768, 1280, 1792, 2304

MESH_ID = pl.DeviceIdType.MESH


def _pcall(body, **kw):
    return pl.pallas_call(body, **kw)


def _params(sem=None):
    return pltpu.CompilerParams(dimension_semantics=sem, vmem_limit_bytes=VMEM_LIMIT_BYTES)


def _dot(a, b):
    return jnp.dot(a, b, preferred_element_type=F32)


def _dot_nt(a, b):
    return lax.dot_general(a, b, (((1,), (1,)), ((), ())), preferred_element_type=F32)


def _dot_tn(a, b):
    return lax.dot_general(a, b, (((0,), (0,)), ((), ())), preferred_element_type=F32)


def _vmem_full():
    return pl.BlockSpec(memory_space=pltpu.VMEM)


def _smem_full():
    return pl.BlockSpec(memory_space=pltpu.SMEM)


def _rows(tm, w):
    return pl.BlockSpec((tm, w), lambda i: (i, 0))


def _const(shape):
    return pl.BlockSpec(shape, lambda i: tuple(0 for _ in shape))


def _rms_stats(x):
    r = lax.rsqrt(jnp.mean(x * x, axis=-1, keepdims=True) + RMS_EPS)
    return r, x * r


def _rms_bwd(n, r, dn):
    return r * (dn - n * jnp.mean(dn * n, axis=-1, keepdims=True))


def _rot(x, even):
    w = x.shape[1]
    return jnp.where(even, pltpu.roll(x, w - 1, 1), pltpu.roll(x, 1, 1))


def _peers():
    x, y, c = lax.axis_index("x"), lax.axis_index("y"), lax.axis_index("c")
    flips = [(0, 0, 1), (1, 0, 0), (0, 1, 0), (1, 1, 0), (1, 0, 1), (0, 1, 1), (1, 1, 1)]
    peers = [(x ^ fx, y ^ fy, c ^ fc) for fx, fy, fc in flips]
    return 4 * x + 2 * y + c, peers


class _Exchange:
    def __init__(self, gathers, swaps, send_sems, recv_sems, local_sems):
        self.me, self.peers = _peers()
        self.slots = [4 * px + 2 * py + pc for px, py, pc in self.peers]
        self.pairs = [(src, dst, True) for src, dst in gathers] + [(src, dst, False) for src, dst in swaps]
        self.send_sems, self.recv_sems, self.local_sems = send_sems, recv_sems, local_sems

    @staticmethod
    def scratch(n):
        return [pltpu.SemaphoreType.DMA((n, N_DEV - 1)), pltpu.SemaphoreType.DMA((n, N_DEV - 1)),
                pltpu.SemaphoreType.DMA((n,))]

    def _parts(self, a, slot):
        src, _, whole = self.pairs[a]
        half = N_DEV // 2
        if whole:
            return [(None, src)]
        if isinstance(src, tuple):
            return [(slot < half, src[0].at[jnp.minimum(slot, half - 1)]),
                    (slot >= half, src[1].at[jnp.maximum(slot - half, 0)])]
        return [(None, src.at[slot])]

    def _local(self, a, src):
        return pltpu.make_async_copy(src, self.pairs[a][1].at[self.me], self.local_sems.at[a])

    def _remote(self, a, k, src, slot):
        return pltpu.make_async_remote_copy(
            src_ref=src, dst_ref=self.pairs[a][1].at[slot], send_sem=self.send_sems.at[a, k],
            recv_sem=self.recv_sems.at[a, k], device_id=self.peers[k], device_id_type=MESH_ID)

    def start(self):
        def go(cond, copy):
            if cond is None:
                copy.start()
            else:
                pl.when(cond)(copy.start)

        for a in range(len(self.pairs)):
            for cond, src in self._parts(a, self.me):
                go(cond, self._local(a, src))
            for k in range(N_DEV - 1):
                for cond, src in self._parts(a, self.slots[k]):
                    go(cond, self._remote(a, k, src, self.me))

    def wait(self):
        for a in range(len(self.pairs)):
            src = self._parts(a, self.me)[0][1]
            for k in range(N_DEV - 1):
                self._remote(a, k, src, self.slots[k]).wait_recv()
        for a in range(len(self.pairs)):
            src = self._parts(a, self.me)[0][1]
            for k in range(N_DEV - 1):
                self._remote(a, k, src, self.me).wait_send()
            self._local(a, src).wait()


ANY_SPEC = pl.BlockSpec(memory_space=pl.ANY)


def _exchange_shapes(gathers, swaps):
    return ([jax.ShapeDtypeStruct((N_DEV,) + a.shape, a.dtype) for a in gathers]
            + [jax.ShapeDtypeStruct(a.shape, a.dtype) for a in swaps])


def _exchange_of(ins, outs, sems, ng):
    return _Exchange(list(zip(ins[:ng], outs[:ng])), list(zip(ins[ng:], outs[ng:])), *sems)


def _exchange_call(gathers, swaps, name):
    ng, ns = len(gathers), len(swaps)
    n = ng + ns

    def body(*refs):
        ex = _exchange_of(refs[:n], refs[n:2 * n], refs[2 * n:], ng)
        ex.start()
        ex.wait()

    return _pcall(
        body, name=name, out_shape=_exchange_shapes(gathers, swaps),
        in_specs=[ANY_SPEC] * (ng + ns), out_specs=[ANY_SPEC] * (ng + ns),
        scratch_shapes=_Exchange.scratch(ng + ns),
    )(*gathers, *swaps)


def _in_proj(x, g1, w_in, cos, sin_s, tm, gathers):
    T = x.shape[0]
    ng = len(gathers)
    nt = T // tm

    def body(x_ref, g_ref, w_ref, cos_ref, sin_ref, *rest):
        ex = _exchange_of(rest[:ng], rest[ng + 8:2 * ng + 8], rest[2 * ng + 8:], ng)
        h_ref, qa_ref, ka_ref, va_ref, qr_ref, kr_ref, vr_ref, gr_ref = rest[ng:ng + 8]
        pl.when(pl.program_id(0) == 0)(ex.start)
        r, n = _rms_stats(x_ref[...])
        h = (n * g_ref[...]).astype(BF)
        h_ref[...] = h

        def proj(c0, w):
            return _dot(h, w_ref[:, c0:c0 + w])

        qa_ref[...] = proj(QA0, ATTN_W).astype(BF)
        kva = proj(KA0, 2 * KV_W)
        ka_ref[...] = kva[:, :KV_W].astype(BF)
        va_ref[...] = kva[:, KV_W:].astype(BF)
        vr_ref[...] = proj(VR0, RET_W).astype(BF)
        gr_ref[...] = proj(GR0, RET_W)
        cos_t, sin_t = cos_ref[...], sin_ref[...]
        even = lax.broadcasted_iota(jnp.int32, (tm, RET_HEAD_DIM), 1) % 2 == 0
        for c0, scale, out_ref in ((QR0, None, qr_ref), (KR0, RET_K_SCALE, kr_ref)):
            full = proj(c0, RET_W)
            for hd in range(N_RET_HEADS):
                cs = slice(hd * RET_HEAD_DIM, (hd + 1) * RET_HEAD_DIM)
                t = full[:, cs] if scale is None else full[:, cs] * scale
                out_ref[:, cs] = (t * cos_t + _rot(t, even) * sin_t).astype(BF)
        pl.when(pl.program_id(0) == nt - 1)(ex.wait)

    widths = [D_MODEL, ATTN_W, KV_W, KV_W, RET_W, RET_W, RET_W, RET_W]
    dts = [BF] * 7 + [F32]
    outs = _pcall(
        body, name="in_proj", grid=(nt,),
        in_specs=[_rows(tm, D_MODEL), _const((1, D_MODEL)), _vmem_full(), _rows(tm, RET_HEAD_DIM),
                  _rows(tm, RET_HEAD_DIM)] + [ANY_SPEC] * ng,
        out_specs=[_rows(tm, w) for w in widths] + [ANY_SPEC] * ng,
        out_shape=[jax.ShapeDtypeStruct((T, w), dt) for w, dt in zip(widths, dts)] + _exchange_shapes(gathers, []),
        scratch_shapes=_Exchange.scratch(ng),
        compiler_params=_params(("arbitrary",)),
    )(x, g1, w_in, cos, sin_s, *gathers)
    return outs[:8], outs[8:]


def _kv_variants(kk):
    kf = kk.astype(F32)
    lo = lax.broadcasted_iota(jnp.int32, kf.shape, 1) < HEAD_DIM
    h0_lo = jnp.where(lo, kf, 0.0)
    h1_hi = jnp.where(lo, 0.0, kf)
    h0_hi = pltpu.roll(h0_lo, HEAD_DIM, 1)
    h1_lo = pltpu.roll(h1_hi, HEAD_DIM, 1)
    return [[h0_lo.astype(BF), h0_hi.astype(BF)], [h1_lo.astype(BF), h1_hi.astype(BF)]]


def _col_to_tile(tile, col, head):
    lane = lax.broadcasted_iota(jnp.int32, tile.shape, 1)
    return jnp.where(lane == head, col, tile)


def _tri(rows, key_major=False):
    i = lax.broadcasted_iota(jnp.int32, (rows, BLOCK), 0) & (BLOCK - 1)
    j = lax.broadcasted_iota(jnp.int32, (rows, BLOCK), 1)
    return i > j if key_major else j > i


def _fold(x2, tri, first_above):
    a, b = x2[:, :BLOCK], x2[:, BLOCK:]
    return jnp.where(tri, a, b) if first_above else jnp.where(tri, b, a)


def _unfold(x, tri, first_above):
    up, low = jnp.where(tri, x, 0.0), jnp.where(tri, 0.0, x)
    return jnp.concatenate([up, low] if first_above else [low, up], axis=1).astype(BF)


def _scaled(q):
    return (q.astype(F32) * ATTN_SCALE).astype(BF)


def _cat_variants(prev, cur):
    return [[jnp.concatenate([prev[h][e], cur[h][e]], axis=0) for e in range(2)] for h in range(2)]


def _block_variants(prev_ref, cur_ref, nbs):
    var = [_kv_variants(prev_ref[...])] + [_kv_variants(cur_ref[b * BLOCK:(b + 1) * BLOCK, :]) for b in range(nbs)]
    return [_cat_variants(var[b], var[b + 1]) for b in range(nbs)]


def _head_cols(col, nbs):
    tiles = []
    for b in range(nbs):
        t = jnp.zeros((BLOCK, BLOCK), F32)
        for head in range(N_ATTN_HEADS):
            r0 = (b * N_ATTN_HEADS + head) * BLOCK
            t = _col_to_tile(t, col[r0:r0 + BLOCK, :], head)
        tiles.append(t)
    return tiles


def _attn_fwd(sinks, qa, ka, va, nbs, gathers):
    T = qa.shape[0]
    steps = T // (BLOCK * nbs)
    R = nbs * N_ATTN_HEADS * BLOCK
    ng = len(gathers)

    def body(sink_ref, q_ref, kc_ref, kp_ref, vc_ref, vp_ref, *rest):
        ex = _exchange_of(rest[:ng], rest[ng + 3:2 * ng + 3], rest[2 * ng + 3:], ng)
        a_ref, lse_ref, lset_ref = rest[ng:ng + 3]
        n = pl.program_id(0)
        pl.when(n == 0)(ex.start)
        kcat = _block_variants(kp_ref, kc_ref, nbs)
        vcat = _block_variants(vp_ref, vc_ref, nbs)
        tri1 = _tri(BLOCK)
        tiles = []
        for b in range(nbs):
            for pair in range(N_ATTN_HEADS // 2):
                qp = _scaled(q_ref[b * BLOCK:(b + 1) * BLOCK, pair * 128:(pair + 1) * 128])
                for e in range(2):
                    s = _fold(_dot_nt(qp, kcat[b][pair // 2][e]), tri1, True)
                    if b == 0:
                        s = jnp.where(tri1 & (n == 0), MASK_VALUE, s)
                    tiles.append(s)
        s = jnp.concatenate(tiles, axis=0)
        sink = jnp.concatenate([jnp.full((BLOCK, 1), sink_ref[head], F32)
                                for _ in range(nbs) for head in range(N_ATTN_HEADS)], axis=0)
        m = jnp.maximum(jnp.max(s, axis=-1, keepdims=True), sink)
        p = jnp.exp(s - m)
        z = jnp.sum(p, axis=-1, keepdims=True) + jnp.exp(sink - m)
        p2 = _unfold(p * (1.0 / z), _tri(R), True)
        for b in range(nbs):
            for pair in range(N_ATTN_HEADS // 2):
                r0 = (b * N_ATTN_HEADS + 2 * pair) * BLOCK
                acc = (_dot(p2[r0:r0 + BLOCK, :], vcat[b][pair // 2][0])
                       + _dot(p2[r0 + BLOCK:r0 + 2 * BLOCK, :], vcat[b][pair // 2][1]))
                a_ref[b * BLOCK:(b + 1) * BLOCK, pair * 128:(pair + 1) * 128] = acc.astype(BF)
        for b, t in enumerate(_head_cols(m + jnp.log(z), nbs)):
            lse_ref[b * BLOCK:(b + 1) * BLOCK, :] = t
            lset_ref[:, b * BLOCK:(b + 1) * BLOCK] = t.T[:N_ATTN_HEADS, :]
        pl.when(n == steps - 1)(ex.wait)

    cur = lambda w: pl.BlockSpec((BLOCK * nbs, w), lambda n: (n, 0))
    prev = lambda w: pl.BlockSpec((BLOCK, w), lambda n: (jnp.maximum(n * nbs - 1, 0), 0))
    outs = _pcall(
        body, name="attn_fwd", grid=(steps,),
        in_specs=[_smem_full(), cur(ATTN_W), cur(KV_W), prev(KV_W), cur(KV_W), prev(KV_W)] + [ANY_SPEC] * ng,
        out_specs=[cur(ATTN_W), cur(BLOCK), pl.BlockSpec((N_ATTN_HEADS, BLOCK * nbs), lambda n: (0, n))]
        + [ANY_SPEC] * ng,
        out_shape=[jax.ShapeDtypeStruct((T, ATTN_W), BF), jax.ShapeDtypeStruct((T, BLOCK), F32),
                   jax.ShapeDtypeStruct((N_ATTN_HEADS, T), F32)] + _exchange_shapes(gathers, []),
        scratch_shapes=_Exchange.scratch(ng),
        compiler_params=_params(("arbitrary",)),
    )(sinks, qa, ka, ka, va, va, *gathers)
    return outs[:3], outs[3:]


def _ret_fwd(decay, qr, kr, vr, gr, d_intra, xi_b, zeta_b, ncs):
    T = qr.shape[0]
    nc = T // BLOCK
    H, C = N_RET_HEADS, RET_HEAD_DIM

    def body(decay_ref, q_ref, k_ref, v_ref, g_ref, d_ref, xi_ref, zeta_ref, o_ref, s_ref, r_ref, state):
        @pl.when(pl.program_id(0) == 0)
        def _():
            state[...] = jnp.zeros_like(state)

        pairs = [(b, h) for b in range(ncs) for h in range(H)]
        sl = lambda b, h: (slice(b * BLOCK, (b + 1) * BLOCK), slice(h * C, (h + 1) * C))
        tab = lambda ref: jnp.concatenate([ref[h] for _, h in pairs], axis=0)
        q = [q_ref[sl(b, h)] for b, h in pairs]
        k = [k_ref[sl(b, h)] for b, h in pairs]
        v = [v_ref[sl(b, h)] for b, h in pairs]
        inner = (jnp.concatenate([_dot_nt(q[i], k[i]) for i in range(len(pairs))], axis=0) * tab(d_ref)).astype(BF)
        kz = (jnp.concatenate(k, axis=0).astype(F32) * tab(zeta_ref)).astype(BF)
        o1 = [_dot(inner[i * BLOCK:(i + 1) * BLOCK, :], v[i]) for i in range(len(pairs))]
        kv = [_dot_tn(kz[i * BLOCK:(i + 1) * BLOCK, :], v[i]) for i in range(len(pairs))]
        st_b = [None] * len(pairs)
        for h in range(H):
            st = state[h]
            for b in range(ncs):
                i = b * H + h
                st_b[i] = st.astype(BF)
                s_ref[b, h] = st_b[i]
                st = decay_ref[h] * st + kv[i]
            state[h] = st
        o2 = jnp.concatenate([_dot(q[i], st_b[i]) for i in range(len(pairs))], axis=0)
        o = jnp.concatenate(o1, axis=0) + o2 * tab(xi_ref)
        mu = jnp.mean(o, axis=-1, keepdims=True)
        oc = o - mu
        rs = lax.rsqrt(jnp.mean(oc * oc, axis=-1, keepdims=True) + GN_EPS)
        g = jnp.concatenate([g_ref[sl(b, h)] for b, h in pairs], axis=0)
        r = (g * jax.nn.sigmoid(g) * (oc * rs)).astype(BF)
        for i, (b, h) in enumerate(pairs):
            o_ref[sl(b, h)] = o[i * BLOCK:(i + 1) * BLOCK, :]
            r_ref[sl(b, h)] = r[i * BLOCK:(i + 1) * BLOCK, :]

    cur = pl.BlockSpec((BLOCK * ncs, RET_W), lambda n: (n, 0))
    tab = pl.BlockSpec((H, C, C), lambda n: (0, 0, 0))
    return _pcall(
        body, name="ret_fwd", grid=(nc // ncs,),
        in_specs=[_smem_full(), cur, cur, cur, cur, tab, tab, tab],
        out_specs=[cur, pl.BlockSpec((ncs, H, C, C), lambda n: (n, 0, 0, 0)), cur],
        out_shape=[jax.ShapeDtypeStruct((T, RET_W), F32), jax.ShapeDtypeStruct((nc, H, C, C), BF),
                   jax.ShapeDtypeStruct((T, RET_W), BF)],
        scratch_shapes=[pltpu.VMEM((H, C, C), F32)],
        compiler_params=_params(("arbitrary",)),
    )(decay, qr, kr, vr, gr, d_intra, xi_b, zeta_b)


def _out_proj(a, r, w_out, x, g2, g3, tm):
    T = x.shape[0]

    def body(a_ref, r_ref, w_ref, x_ref, g2_ref, g3_ref, mixed_ref, x1_ref, h2_ref):
        mixed = _dot(a_ref[...], w_ref[:ATTN_W, :]) + _dot(r_ref[...], w_ref[ATTN_W:, :])
        mixed_ref[...] = mixed
        _, n2 = _rms_stats(mixed)
        x1 = x_ref[...] + n2 * g2_ref[...]
        x1_ref[...] = x1
        _, n3 = _rms_stats(x1)
        h2_ref[...] = (n3 * g3_ref[...]).astype(BF)

    return _pcall(
        body, name="out_proj", grid=(T // tm,),
        in_specs=[_rows(tm, ATTN_W), _rows(tm, RET_W), _vmem_full(), _rows(tm, D_MODEL), _const((1, D_MODEL)),
                  _const((1, D_MODEL))],
        out_specs=[_rows(tm, D_MODEL)] * 3,
        out_shape=[jax.ShapeDtypeStruct((T, D_MODEL), F32), jax.ShapeDtypeStruct((T, D_MODEL), F32),
                   jax.ShapeDtypeStruct((T, D_MODEL), BF)],
        compiler_params=_params(("parallel",)),
    )(a, r, w_out, x, g2, g3)


def _shift_down(cur, k, before):
    out = pltpu.roll(cur, k, 0)
    row = lax.broadcasted_iota(jnp.int32, before.shape, 0)
    top = jnp.where(row < k, pltpu.roll(before, k, 0), out[0:8])
    return jnp.concatenate([top, out[8:]], axis=0)


def _shift_up(cur, k, after):
    tm = cur.shape[0]
    out = pltpu.roll(cur, tm - k, 0)
    row = lax.broadcasted_iota(jnp.int32, after.shape, 0)
    bot = jnp.where(row >= 8 - k, pltpu.roll(after, 8 - k, 0), out[tm - 8:])
    return jnp.concatenate([out[:tm - 8], bot], axis=0)


def _gelu_parts(x):
    x2 = x * x
    th = jnp.tanh(GELU_C * (x + GELU_A * x * x2))
    gelu = 0.5 * x * (1.0 + th)
    dgelu = 0.5 * (1.0 + th) + 0.5 * x * (1.0 - th * th) * (GELU_C * (1.0 + 3.0 * GELU_A * x2))
    return gelu, dgelu


def _ffn_fwd(h2, w_up8, conv_w8, conv_b8, w_down4, x1, g4, target, tm):
    T = h2.shape[0]
    nt = T // tm

    def body(h_ref, wu_ref, cwg_ref, cwv_ref, cbg_ref, cbv_ref, wd_ref, x1_ref, g_ref, t_ref,
             upg_ref, upv_ref, ug_ref, uv_ref, y_ref, dout_ref, dz_ref, dg4_ref, loss_ref, halo, z_acc):
        s = pl.program_id(1)
        first = pl.program_id(0) == 0

        @pl.when(first & (s == 0))
        def _():
            loss_ref[...] = jnp.zeros_like(loss_ref)
            dg4_ref[...] = jnp.zeros_like(dg4_ref)

        h = h_ref[...]
        u = []
        parts = ((cwg_ref, cbg_ref, upg_ref, ug_ref), (cwv_ref, cbv_ref, upv_ref, uv_ref))
        for part, (cw_ref, cb_ref, up_ref, u_ref) in enumerate(parts):
            cur = _dot(h, wu_ref[s + part * FF_PAIRS])
            up_ref[0] = cur.astype(BF)
            before = jnp.where(first, 0.0, halo[part, s])
            halo[part, s] = cur[tm - 8:tm, :]
            u_c = (cw_ref[0, pl.ds(0, 1), :] * _shift_down(cur, 2, before)
                   + cw_ref[0, pl.ds(1, 1), :] * _shift_down(cur, 1, before)
                   + cw_ref[0, pl.ds(2, 1), :] * cur + cb_ref[0])
            u_ref[0] = u_c
            u.append(u_c)
        gelu, _ = _gelu_parts(u[0])
        y = (gelu * u[1]).astype(BF)
        y_ref[0] = y
        z_part = _dot(y, wd_ref[s])

        @pl.when(s == 0)
        def _():
            z_acc[...] = z_part

        @pl.when(s > 0)
        def _():
            z_acc[...] += z_part

        @pl.when(s == FF_PAIRS - 1)
        def _():
            r4, n4 = _rms_stats(z_acc[...])
            err = x1_ref[...] + n4 * g_ref[...] - t_ref[...]
            dout = err * (1.0 / D_MODEL)
            dout_ref[...] = dout
            loss_ref[...] += 0.5 * jnp.sum(jnp.mean(err * err, axis=-1, keepdims=True), axis=0, keepdims=True)
            dg4_ref[...] += jnp.sum(dout * n4, axis=0, keepdims=True)
            dz_ref[...] = _rms_bwd(n4, r4, dout * g_ref[...]).astype(BF)

    rows = pl.BlockSpec((tm, D_MODEL), lambda i, s: (i, 0))
    one = lambda shape: pl.BlockSpec(shape, lambda i, s: tuple(0 for _ in shape))
    gate = lambda r, w: pl.BlockSpec((1, r, w), lambda i, s: (s, 0, 0))
    val = lambda r, w: pl.BlockSpec((1, r, w), lambda i, s: (s + FF_PAIRS, 0, 0))
    tile = pl.BlockSpec((1, tm, FF_SHARD), lambda i, s: (s, i, 0))
    half = lambda dt: jax.ShapeDtypeStruct((FF_PAIRS, T, FF_SHARD), dt)
    return _pcall(
        body, name="ffn_fwd", grid=(nt, FF_PAIRS),
        in_specs=[rows, _vmem_full(), gate(3, FF_SHARD), val(3, FF_SHARD), gate(1, FF_SHARD), val(1, FF_SHARD),
                  _vmem_full(), rows, one((1, D_MODEL)), rows],
        out_specs=[tile] * 5 + [rows, rows, one((1, D_MODEL)), one((8, 128))],
        out_shape=[half(BF), half(BF), half(F32), half(F32), half(BF), jax.ShapeDtypeStruct((T, D_MODEL), F32),
                   jax.ShapeDtypeStruct((T, D_MODEL), BF), jax.ShapeDtypeStruct((1, D_MODEL), F32),
                   jax.ShapeDtypeStruct((8, 128), F32)],
        scratch_shapes=[pltpu.VMEM((2, FF_PAIRS, 8, FF_SHARD), F32), pltpu.VMEM((tm, D_MODEL), F32)],
        compiler_params=_params(("arbitrary", "arbitrary")),
    )(h2, w_up8, conv_w8, conv_w8, conv_b8, conv_b8, w_down4, x1, g4, target)


def _ffn_bwd_a(dz, h2, w_down4, u_g, u_v, up_g, up_v, y4, conv_w8, tm):
    T = dz.shape[0]
    nt = T // tm

    def body(dz_ref, h_ref, wd_ref, ug_ref, uv_ref, upg_ref, upv_ref, y_ref, cwg_ref, cwv_ref,
             dupg_ref, dupv_ref, dcbg_ref, dcbv_ref, dcwg_ref, dcwv_ref, gwug_ref, gwuv_ref, gwd_ref, carry):
        @pl.when(pl.program_id(1) == 0)
        def _():
            for ref in (dcbg_ref, dcbv_ref, dcwg_ref, dcwv_ref, gwug_ref, gwuv_ref, gwd_ref, carry):
                ref[...] = jnp.zeros_like(ref)

        dz = dz_ref[...]
        h = h_ref[...]
        dy = _dot_nt(dz, wd_ref[0])
        gwd_ref[0] += _dot_tn(y_ref[0], dz)
        gelu, dgelu = _gelu_parts(ug_ref[0])
        parts = ((0, dy * uv_ref[0] * dgelu, upg_ref, cwg_ref, dupg_ref, dcbg_ref, dcwg_ref, gwug_ref),
                 (1, dy * gelu, upv_ref, cwv_ref, dupv_ref, dcbv_ref, dcwv_ref, gwuv_ref))
        for part, d, up_ref, cw_ref, dup_ref, dcb_ref, dcw_ref, gwu_ref in parts:
            after = carry[part]
            d1 = _shift_up(d, 1, after)
            d2 = _shift_up(d, 2, after)
            carry[part] = d[0:8, :]
            upc = up_ref[0].astype(F32)
            dcb_ref[0] += jnp.sum(d, axis=0, keepdims=True)
            dcw_ref[0, pl.ds(2, 1), :] += jnp.sum(d * upc, axis=0, keepdims=True)
            dcw_ref[0, pl.ds(1, 1), :] += jnp.sum(d1 * upc, axis=0, keepdims=True)
            dcw_ref[0, pl.ds(0, 1), :] += jnp.sum(d2 * upc, axis=0, keepdims=True)
            dup = (cw_ref[0, pl.ds(2, 1), :] * d + cw_ref[0, pl.ds(1, 1), :] * d1
                   + cw_ref[0, pl.ds(0, 1), :] * d2).astype(BF)
            dup_ref[0] = dup
            gwu_ref[0] += _dot_tn(h, dup)

    rev = pl.BlockSpec((tm, D_MODEL), lambda s, i: (nt - 1 - i, 0))
    tile = pl.BlockSpec((1, tm, FF_SHARD), lambda s, i: (s, nt - 1 - i, 0))
    acc = lambda r, w: pl.BlockSpec((1, r, w), lambda s, i: (s, 0, 0))
    acc_val = pl.BlockSpec((1, 3, FF_SHARD), lambda s, i: (s + FF_PAIRS, 0, 0))
    half = lambda r, dt: jax.ShapeDtypeStruct((FF_PAIRS, r, FF_SHARD), dt)
    return _pcall(
        body, name="ffn_bwd_a", grid=(FF_PAIRS, nt),
        in_specs=[rev, rev, acc(FF_SHARD, D_MODEL), tile, tile, tile, tile, tile, acc(3, FF_SHARD), acc_val],
        out_specs=[tile, tile, acc(1, FF_SHARD), acc(1, FF_SHARD), acc(3, FF_SHARD), acc(3, FF_SHARD),
                   acc(D_MODEL, FF_SHARD), acc(D_MODEL, FF_SHARD), acc(FF_SHARD, D_MODEL)],
        out_shape=[half(T, BF), half(T, BF), half(1, F32), half(1, F32), half(3, F32), half(3, F32),
                   half(D_MODEL, F32), half(D_MODEL, F32), jax.ShapeDtypeStruct((FF_PAIRS, FF_SHARD, D_MODEL), F32)],
        scratch_shapes=[pltpu.VMEM((2, 8, FF_SHARD), F32)],
        compiler_params=_params(("arbitrary", "arbitrary")),
    )(dz, h2, w_down4, u_g, u_v, up_g, up_v, y4, conv_w8, conv_w8)


def _ffn_bwd_b(dup_g, dup_v, w_up8, x1, dout, g3, mixed, g2, w_out, tm, gwu_g, gwu_v):
    T = x1.shape[0]
    nt = T // tm

    def body(dupg_ref, dupv_ref, wup_ref, x1_ref, dout_ref, g3_ref, mixed_ref, g2_ref, wout_ref, gwug_ref, gwuv_ref,
             dx1_ref, dmixed_ref, da_ref, dr_ref, dg3_ref, dg2_ref, pup_ref, *sems):
        ex = _Exchange([], [((gwug_ref, gwuv_ref), pup_ref)], *sems)

        @pl.when(pl.program_id(0) == 0)
        def _():
            ex.start()
            dg3_ref[...] = jnp.zeros_like(dg3_ref)
            dg2_ref[...] = jnp.zeros_like(dg2_ref)

        dh2 = jnp.zeros((tm, D_MODEL), F32)
        for s in range(FF_PAIRS):
            dh2 = dh2 + _dot_nt(dupg_ref[s], wup_ref[s]) + _dot_nt(dupv_ref[s], wup_ref[s + FF_PAIRS])
        r3, n3 = _rms_stats(x1_ref[...])
        dg3_ref[...] += jnp.sum(dh2 * n3, axis=0, keepdims=True)
        dx1 = dout_ref[...] + _rms_bwd(n3, r3, dh2 * g3_ref[...])
        dx1_ref[...] = dx1
        r2, n2 = _rms_stats(mixed_ref[...])
        dg2_ref[...] += jnp.sum(dx1 * n2, axis=0, keepdims=True)
        dmixed = _rms_bwd(n2, r2, dx1 * g2_ref[...]).astype(BF)
        dmixed_ref[...] = dmixed
        da_ref[...] = _dot_nt(dmixed, wout_ref[:ATTN_W, :])
        dr_ref[...] = _dot_nt(dmixed, wout_ref[ATTN_W:, :])
        pl.when(pl.program_id(0) == nt - 1)(ex.wait)

    half = pl.BlockSpec((FF_PAIRS, tm, FF_SHARD), lambda i: (0, i, 0))
    outs = _pcall(
        body, name="ffn_bwd_b", grid=(nt,),
        in_specs=[half, half, _vmem_full(), _rows(tm, D_MODEL), _rows(tm, D_MODEL), _const((1, D_MODEL)),
                  _rows(tm, D_MODEL), _const((1, D_MODEL)), _vmem_full(), ANY_SPEC, ANY_SPEC],
        out_specs=[_rows(tm, D_MODEL), _rows(tm, D_MODEL), _rows(tm, ATTN_W), _rows(tm, RET_W),
                   _const((1, D_MODEL)), _const((1, D_MODEL)), ANY_SPEC],
        out_shape=[jax.ShapeDtypeStruct((T, D_MODEL), F32), jax.ShapeDtypeStruct((T, D_MODEL), BF),
                   jax.ShapeDtypeStruct((T, ATTN_W), F32), jax.ShapeDtypeStruct((T, RET_W), F32),
                   jax.ShapeDtypeStruct((1, D_MODEL), F32), jax.ShapeDtypeStruct((1, D_MODEL), F32),
                   jax.ShapeDtypeStruct((N_DEV, D_MODEL, FF_SHARD), F32)],
        scratch_shapes=_Exchange.scratch(1),
        compiler_params=_params(("arbitrary",)),
    )(dup_g, dup_v, w_up8, x1, dout, g3, mixed, g2, w_out, gwu_g, gwu_v)
    return outs[:6], outs[6]


def _ret_bwd(decay, qr, kr, vr, gr, o, states, dr, d_intra, d_intra_t, xi_b, zeta_b, cos, sin_s, swaps, ncs):
    T = qr.shape[0]
    nc = T // BLOCK
    H, C = N_RET_HEADS, RET_HEAD_DIM
    ns = len(swaps)

    def body(decay_ref, q_ref, k_ref, v_ref, g_ref, o_ref, s_ref, dr_ref, d_ref, dt_ref, xi_ref, zeta_ref,
             cos_ref, sin_ref, *rest):
        ex = _exchange_of(rest[:ns], rest[ns + 1:2 * ns + 1], rest[2 * ns + 2:], 0)
        dret_ref, gstate = rest[ns], rest[2 * ns + 1]

        @pl.when(pl.program_id(0) == 0)
        def _():
            ex.start()
            gstate[...] = jnp.zeros_like(gstate)

        pairs = [(b, h) for b in range(ncs) for h in range(H)]
        n = len(pairs)
        sl = lambda b, h: (slice(b * BLOCK, (b + 1) * BLOCK), slice(h * C, (h + 1) * C))
        cat = lambda ref: jnp.concatenate([ref[sl(b, h)] for b, h in pairs], axis=0)
        tab = lambda ref: jnp.concatenate([ref[h] for _, h in pairs], axis=0)
        part = lambda x, i: x[i * BLOCK:(i + 1) * BLOCK, :]
        q = [q_ref[sl(b, h)] for b, h in pairs]
        k = [k_ref[sl(b, h)] for b, h in pairs]
        v = [v_ref[sl(b, h)] for b, h in pairs]
        g, o_all, dr_all = cat(g_ref), cat(o_ref), cat(dr_ref)
        mu = jnp.mean(o_all, axis=-1, keepdims=True)
        oc = o_all - mu
        rs = lax.rsqrt(jnp.mean(oc * oc, axis=-1, keepdims=True) + GN_EPS)
        on = oc * rs
        sg = jax.nn.sigmoid(g)
        dg = (dr_all * on * (sg * (1.0 + g * (1.0 - sg)))).astype(BF)
        don = dr_all * (g * sg)
        do = rs * (don - jnp.mean(don, axis=-1, keepdims=True) - on * jnp.mean(don * on, axis=-1, keepdims=True))
        do_b = do.astype(BF)
        dox_b = (do * tab(xi_ref)).astype(BF)
        zeta = tab(zeta_ref)
        kz = (jnp.concatenate(k, axis=0).astype(F32) * zeta).astype(BF)
        d_t = tab(dt_ref)
        da_b = (jnp.concatenate([_dot_nt(part(do_b, i), v[i]) for i in range(n)], axis=0) * tab(d_ref)).astype(BF)
        dat_b = (jnp.concatenate([_dot_nt(v[i], part(do_b, i)) for i in range(n)], axis=0) * d_t).astype(BF)
        mt_b = (jnp.concatenate([_dot_nt(k[i], q[i]) for i in range(n)], axis=0) * d_t).astype(BF)
        dq = [_dot(part(da_b, i), k[i]) + _dot_nt(part(dox_b, i), s_ref[pairs[i]]) for i in range(n)]
        dk1 = [_dot(part(dat_b, i), q[i]) for i in range(n)]
        dv1 = [_dot(part(mt_b, i), part(do_b, i)) for i in range(n)]
        qtd = [_dot_tn(q[i], part(dox_b, i)) for i in range(n)]
        gst_b = [None] * n
        for h in range(H):
            gst = gstate[h]
            for b in reversed(range(ncs)):
                i = b * H + h
                gst_b[i] = gst.astype(BF)
                gst = decay_ref[h] * gst + qtd[i]
            gstate[h] = gst
        dk2 = jnp.concatenate([_dot_nt(v[i], gst_b[i]) for i in range(n)], axis=0) * zeta
        dv = jnp.concatenate([dv1[i] + _dot(part(kz, i), gst_b[i]) for i in range(n)], axis=0).astype(BF)
        even = lax.broadcasted_iota(jnp.int32, (n * BLOCK, C), 1) % 2 == 0
        cos_t = jnp.concatenate([cos_ref[b * BLOCK:(b + 1) * BLOCK, :] for b, _ in pairs], axis=0)
        sin_t = jnp.concatenate([sin_ref[b * BLOCK:(b + 1) * BLOCK, :] for b, _ in pairs], axis=0)
        dq = jnp.concatenate(dq, axis=0)
        dk = jnp.concatenate(dk1, axis=0) + dk2
        dq = (dq * cos_t - _rot(dq, even) * sin_t).astype(BF)
        dk = ((dk * cos_t - _rot(dk, even) * sin_t) * RET_K_SCALE).astype(BF)
        for i, (b, h) in enumerate(pairs):
            rows = slice(b * BLOCK, (b + 1) * BLOCK)
            for j, x in enumerate((dq, dk, dv, dg)):
                dret_ref[rows, j * RET_W + h * C:j * RET_W + (h + 1) * C] = part(x, i)
        pl.when(pl.program_id(0) == steps - 1)(ex.wait)

    steps = nc // ncs
    rev = lambda w: pl.BlockSpec((BLOCK * ncs, w), lambda n: (steps - 1 - n, 0))
    tab = pl.BlockSpec((H, C, C), lambda n: (0, 0, 0))
    outs = _pcall(
        body, name="ret_bwd", grid=(steps,),
        in_specs=[_smem_full(), rev(RET_W), rev(RET_W), rev(RET_W), rev(RET_W), rev(RET_W),
                  pl.BlockSpec((ncs, H, C, C), lambda n: (steps - 1 - n, 0, 0, 0)), rev(RET_W), tab, tab, tab, tab,
                  rev(C), rev(C)] + [ANY_SPEC] * ns,
        out_specs=[rev(4 * RET_W)] + [ANY_SPEC] * ns,
        out_shape=[jax.ShapeDtypeStruct((T, 4 * RET_W), BF)] + _exchange_shapes([], swaps),
        scratch_shapes=[pltpu.VMEM((H, C, C), F32)] + _Exchange.scratch(ns),
        compiler_params=_params(("arbitrary",)),
    )(decay, qr, kr, vr, gr, o, states, dr, d_intra, d_intra_t, xi_b, zeta_b, cos, sin_s, *swaps)
    return outs[0], outs[1:]


def _attn_bwd_dq(sinks, qa, ka, va, da, lse, nbs, swaps):
    T = qa.shape[0]
    steps = T // (BLOCK * nbs)
    R = nbs * N_ATTN_HEADS * BLOCK
    ns = len(swaps)

    def body(sink_ref, q_ref, kc_ref, kp_ref, vc_ref, vp_ref, da_ref, lse_ref, *rest):
        ex = _exchange_of(rest[:ns], rest[ns + 3:2 * ns + 3], rest[2 * ns + 3:], 0)
        dq_ref, deltat_ref, dsink_ref = rest[ns:ns + 3]
        n = pl.program_id(0)

        @pl.when(n == 0)
        def _():
            ex.start()
            dsink_ref[...] = jnp.zeros_like(dsink_ref)

        kcat = _block_variants(kp_ref, kc_ref, nbs)
        vcat = _block_variants(vp_ref, vc_ref, nbs)
        tri1 = _tri(BLOCK)
        lane = lax.broadcasted_iota(jnp.int32, (BLOCK, BLOCK), 1)
        s_tiles, dp_tiles, lse_cols = [], [], []
        for b in range(nbs):
            rows = slice(b * BLOCK, (b + 1) * BLOCK)
            lse_tile = lse_ref[rows, :]
            for pair in range(N_ATTN_HEADS // 2):
                qp = _scaled(q_ref[rows, pair * 128:(pair + 1) * 128])
                dop = da_ref[rows, pair * 128:(pair + 1) * 128].astype(BF)
                for e in range(2):
                    s = _fold(_dot_nt(qp, kcat[b][pair // 2][e]), tri1, True)
                    if b == 0:
                        s = jnp.where(tri1 & (n == 0), MASK_VALUE, s)
                    s_tiles.append(s)
                    dp_tiles.append(_fold(_dot_nt(dop, vcat[b][pair // 2][e]), tri1, True))
                    lse_cols.append(jnp.sum(jnp.where(lane == 2 * pair + e, lse_tile, 0.0), axis=-1, keepdims=True))
        lse_c = jnp.concatenate(lse_cols, axis=0)
        p = jnp.exp(jnp.concatenate(s_tiles, axis=0) - lse_c)
        dp = jnp.concatenate(dp_tiles, axis=0)
        delta = jnp.sum(p * dp, axis=-1, keepdims=True)
        ds2 = _unfold(p * (dp - delta), _tri(R), True)
        for b in range(nbs):
            for pair in range(N_ATTN_HEADS // 2):
                r0 = (b * N_ATTN_HEADS + 2 * pair) * BLOCK
                acc = (_dot(ds2[r0:r0 + BLOCK, :], kcat[b][pair // 2][0])
                       + _dot(ds2[r0 + BLOCK:r0 + 2 * BLOCK, :], kcat[b][pair // 2][1]))
                dq_ref[b * BLOCK:(b + 1) * BLOCK, pair * 128:(pair + 1) * 128] = (acc * ATTN_SCALE).astype(BF)
        for b, t in enumerate(_head_cols(delta, nbs)):
            deltat_ref[:, b * BLOCK:(b + 1) * BLOCK] = t.T[:N_ATTN_HEADS, :]
        sink = jnp.concatenate([jnp.full((BLOCK, 1), sink_ref[head], F32)
                                for _ in range(nbs) for head in range(N_ATTN_HEADS)], axis=0)
        ds_sink = -jnp.exp(sink - lse_c) * delta
        row8 = lax.broadcasted_iota(jnp.int32, (N_ATTN_HEADS, BLOCK), 0)
        dsink = jnp.zeros((N_ATTN_HEADS, BLOCK), F32)
        for b in range(nbs):
            for head in range(N_ATTN_HEADS):
                r0 = (b * N_ATTN_HEADS + head) * BLOCK
                dsink = dsink + jnp.where(row8 == head, jnp.sum(ds_sink[r0:r0 + BLOCK, :], axis=0, keepdims=True), 0.0)
        dsink_ref[...] += dsink
        pl.when(n == steps - 1)(ex.wait)

    cur = lambda w: pl.BlockSpec((BLOCK * nbs, w), lambda n: (n, 0))
    prev = lambda w: pl.BlockSpec((BLOCK, w), lambda n: (jnp.maximum(n * nbs - 1, 0), 0))
    outs = _pcall(
        body, name="attn_bwd_dq", grid=(steps,),
        in_specs=[_smem_full(), cur(ATTN_W), cur(KV_W), prev(KV_W), cur(KV_W), prev(KV_W), cur(ATTN_W), cur(BLOCK)]
        + [ANY_SPEC] * ns,
        out_specs=[cur(ATTN_W), pl.BlockSpec((N_ATTN_HEADS, BLOCK * nbs), lambda n: (0, n)),
                   _const((N_ATTN_HEADS, BLOCK))] + [ANY_SPEC] * ns,
        out_shape=[jax.ShapeDtypeStruct((T, ATTN_W), BF), jax.ShapeDtypeStruct((N_ATTN_HEADS, T), F32),
                   jax.ShapeDtypeStruct((N_ATTN_HEADS, BLOCK), F32)] + _exchange_shapes([], swaps),
        scratch_shapes=_Exchange.scratch(ns),
        compiler_params=_params(("arbitrary",)),
    )(sinks, qa, ka, ka, va, va, da, lse, *swaps)
    return outs[:3], outs[3:]


def _attn_bwd_dkv(qa, ka, va, da, lse_t, delta_t, nbs):
    T = qa.shape[0]
    nb = T // BLOCK
    steps = nb // nbs
    R = nbs * N_ATTN_HEADS * BLOCK

    def body(qc_ref, qn_ref, dac_ref, dan_ref, k_ref, v_ref, lc_ref, ln_ref, dc_ref, dn_ref, dk_ref, dv_ref):
        n = pl.program_id(0)
        tri1 = _tri(BLOCK, True)
        lo = lax.broadcasted_iota(jnp.int32, (BLOCK, 128), 1) < HEAD_DIM
        kv = [_kv_variants(k_ref[b * BLOCK:(b + 1) * BLOCK, :]) for b in range(nbs)]
        vv = [_kv_variants(v_ref[b * BLOCK:(b + 1) * BLOCK, :]) for b in range(nbs)]
        qcat, docat = [], []
        s_tiles, dp_tiles, lse_tiles, delta_tiles = [], [], [], []
        for b in range(nbs):
            rows = slice(b * BLOCK, (b + 1) * BLOCK)
            nrows = slice((b + 1) * BLOCK, (b + 2) * BLOCK)
            inside = b < nbs - 1
            for pair in range(N_ATTN_HEADS // 2):
                ps = slice(pair * 128, (pair + 1) * 128)
                q2 = _scaled(jnp.concatenate([qc_ref[rows, ps], qc_ref[nrows, ps] if inside else qn_ref[:, ps]], axis=0))
                do2 = jnp.concatenate([dac_ref[rows, ps], dac_ref[nrows, ps] if inside else dan_ref[:, ps]],
                                      axis=0).astype(BF)
                qcat.append(q2)
                docat.append(do2)
                for e in range(2):
                    one = pl.ds(2 * pair + e, 1)
                    s = _fold(_dot_nt(kv[b][pair // 2][e], q2), tri1, False)
                    if not inside:
                        s = jnp.where(tri1 & (n == steps - 1), MASK_VALUE, s)
                    s_tiles.append(s)
                    dp_tiles.append(_fold(_dot_nt(vv[b][pair // 2][e], do2), tri1, False))
                    lse_tiles.append(jnp.where(tri1, lc_ref[one, nrows] if inside else ln_ref[one, :], lc_ref[one, rows]))
                    delta_tiles.append(jnp.where(tri1, dc_ref[one, nrows] if inside else dn_ref[one, :],
                                                 dc_ref[one, rows]))
        pt = jnp.exp(jnp.concatenate(s_tiles, axis=0) - jnp.concatenate(lse_tiles, axis=0))
        dst = pt * (jnp.concatenate(dp_tiles, axis=0) - jnp.concatenate(delta_tiles, axis=0))
        tri = _tri(R, True)
        pt2 = _unfold(pt, tri, False)
        dst2 = _unfold(dst, tri, False)
        for b in range(nbs):
            dk = jnp.zeros((BLOCK, 128), F32)
            dv = jnp.zeros((BLOCK, 128), F32)
            for pair in range(N_ATTN_HEADS // 2):
                h = pair // 2
                for e in range(2):
                    r0 = (b * N_ATTN_HEADS + 2 * pair + e) * BLOCK
                    half = lo if e == 0 else jnp.logical_not(lo)
                    dv_e = jnp.where(half, _dot(pt2[r0:r0 + BLOCK, :], docat[b * 4 + pair]), 0.0)
                    dk_e = jnp.where(half, _dot(dst2[r0:r0 + BLOCK, :], qcat[b * 4 + pair]), 0.0)
                    if e != h:
                        dv_e = pltpu.roll(dv_e, HEAD_DIM, 1)
                        dk_e = pltpu.roll(dk_e, HEAD_DIM, 1)
                    dv = dv + dv_e
                    dk = dk + dk_e
            dk_ref[b * BLOCK:(b + 1) * BLOCK, :] = dk.astype(BF)
            dv_ref[b * BLOCK:(b + 1) * BLOCK, :] = dv.astype(BF)

    cur = lambda w: pl.BlockSpec((BLOCK * nbs, w), lambda n: (n, 0))
    nxt = lambda w: pl.BlockSpec((BLOCK, w), lambda n: (jnp.minimum((n + 1) * nbs, nb - 1), 0))
    tcur = pl.BlockSpec((N_ATTN_HEADS, BLOCK * nbs), lambda n: (0, n))
    tnxt = pl.BlockSpec((N_ATTN_HEADS, BLOCK), lambda n: (0, jnp.minimum((n + 1) * nbs, nb - 1)))
    return _pcall(
        body, name="attn_bwd_dkv", grid=(steps,),
        in_specs=[cur(ATTN_W), nxt(ATTN_W), cur(ATTN_W), nxt(ATTN_W), cur(KV_W), cur(KV_W), tcur, tnxt, tcur, tnxt],
        out_specs=[cur(KV_W), cur(KV_W)],
        out_shape=[jax.ShapeDtypeStruct((T, KV_W), BF), jax.ShapeDtypeStruct((T, KV_W), BF)],
        compiler_params=_params(("parallel",)),
    )(qa, qa, da, da, ka, va, lse_t, lse_t, delta_t, delta_t)


def _in_proj_bwd(dqa, dka, dva, dret, w_in, x, g1, dx1, tm):
    T = x.shape[0]

    def body(dqa_ref, dka_ref, dva_ref, dret_ref, w_ref, x_ref, g_ref, dx1_ref, dx_ref, dg1_ref):
        @pl.when(pl.program_id(0) == 0)
        def _():
            dg1_ref[...] = jnp.zeros_like(dg1_ref)

        dh = (_dot_nt(dqa_ref[...], w_ref[:, QA0:QA0 + ATTN_W]) + _dot_nt(dka_ref[...], w_ref[:, KA0:KA0 + KV_W])
              + _dot_nt(dva_ref[...], w_ref[:, VA0:VA0 + KV_W]) + _dot_nt(dret_ref[...], w_ref[:, QR0:IN_W]))
        r, n = _rms_stats(x_ref[...])
        dg1_ref[...] += jnp.sum(dh * n, axis=0, keepdims=True)
        dx_ref[...] = dx1_ref[...] + _rms_bwd(n, r, dh * g_ref[...])

    return _pcall(
        body, name="in_proj_bwd", grid=(T // tm,),
        in_specs=[_rows(tm, ATTN_W), _rows(tm, KV_W), _rows(tm, KV_W), _rows(tm, 4 * RET_W), _vmem_full(),
                  _rows(tm, D_MODEL), _const((1, D_MODEL)), _rows(tm, D_MODEL)],
        out_specs=[_rows(tm, D_MODEL), _const((1, D_MODEL))],
        out_shape=[jax.ShapeDtypeStruct((T, D_MODEL), F32), jax.ShapeDtypeStruct((1, D_MODEL), F32)],
        compiler_params=_params(("arbitrary",)),
    )(dqa, dka, dva, dret, w_in, x, g1, dx1)


def _wgrad(a_list, b_list, tk, name):
    T = a_list[0].shape[0]
    na, nbb = len(a_list), len(b_list)
    m_sizes = [a.shape[1] for a in a_list]
    n_sizes = [b.shape[1] for b in b_list]
    M, N = sum(m_sizes), sum(n_sizes)
    nk = T // tk
    chunk = 512

    def body(*refs):
        a_refs, b_refs = refs[:na], refs[na:na + nbb]
        out_ref, acc = refs[na + nbb], refs[na + nbb + 1]
        k = pl.program_id(0)

        @pl.when(k == 0)
        def _():
            acc[...] = jnp.zeros_like(acc)

        r0 = 0
        for ai in range(na):
            a = a_refs[ai][...]
            c0 = 0
            for bi in range(nbb):
                for s in range(0, n_sizes[bi], chunk):
                    w = min(chunk, n_sizes[bi] - s)
                    acc[r0:r0 + m_sizes[ai], c0 + s:c0 + s + w] += _dot_tn(a, b_refs[bi][:, s:s + w])
                c0 += n_sizes[bi]
            r0 += m_sizes[ai]

        @pl.when(k == nk - 1)
        def _():
            pltpu.sync_copy(acc, out_ref)

    return _pcall(
        body, name=name, grid=(nk,),
        in_specs=[_rows(tk, w) for w in m_sizes + n_sizes],
        out_specs=pl.BlockSpec(memory_space=pl.ANY),
        out_shape=jax.ShapeDtypeStruct((M, N), F32),
        scratch_shapes=[pltpu.VMEM((M, N), F32)],
        compiler_params=_params(("arbitrary",)),
    )(*a_list, *b_list)


def _adamw_math(w, g, m, v):
    m = ADAM_B1 * m + (1.0 - ADAM_B1) * g
    v = ADAM_B2 * v + (1.0 - ADAM_B2) * (g * g)
    m_hat = m / (1.0 - ADAM_B1 ** ADAM_STEP)
    v_hat = v / (1.0 - ADAM_B2 ** ADAM_STEP)
    delta = -ADAM_LR * (m_hat / (jnp.sqrt(v_hat) + ADAM_EPS) + ADAM_WD * w)
    return delta, m, v


def _sum_parts(parts_ref):
    g = parts_ref[0].astype(F32)
    for i in range(1, N_DEV):
        g = g + parts_ref[i].astype(F32)
    return g


def _adamw_shard(parts, w, m, v, tr, name):
    R, C = w.shape

    def body(p_ref, w_ref, m_ref, v_ref, g_ref, d_ref, nm_ref, nv_ref):
        g = _sum_parts(p_ref)
        g_ref[...] = g
        d_ref[...], nm_ref[...], nv_ref[...] = _adamw_math(w_ref[...], g, m_ref[...], v_ref[...])

    blk = pl.BlockSpec((tr, C), lambda i: (i, 0))
    return _pcall(
        body, name=name, grid=(R // tr,),
        in_specs=[pl.BlockSpec((N_DEV, tr, C), lambda i: (0, i, 0)), blk, blk, blk],
        out_specs=[blk] * 4,
        out_shape=[jax.ShapeDtypeStruct((R, C), F32)] * 4,
        compiler_params=_params(("parallel",)),
    )(parts, w, m, v)


def _sum_small(parts):
    def body(p_ref, g_ref):
        g_ref[...] = _sum_parts(p_ref)

    return _pcall(body, name="sum_small", out_shape=jax.ShapeDtypeStruct(parts.shape[1:], F32),
                  in_specs=[_vmem_full()], out_specs=_vmem_full())(parts)


def _adamw_small(g, w, m, v, name):
    def body(g_ref, w_ref, m_ref, v_ref, d_ref, nm_ref, nv_ref):
        d_ref[...], nm_ref[...], nv_ref[...] = _adamw_math(w_ref[...], g_ref[...], m_ref[...], v_ref[...])

    return _pcall(body, name=name, out_shape=[jax.ShapeDtypeStruct(w.shape, F32)] * 3,
                  in_specs=[_vmem_full()] * 4, out_specs=[_vmem_full()] * 3)(g, w, m, v)


def _tables(T):
    h, c = N_RET_HEADS, BLOCK
    pos = jnp.arange(T, dtype=F32)
    angle = 1.0 / jnp.power(10000.0, jnp.linspace(0.0, 1.0, RET_HEAD_DIM // 2, dtype=F32))
    angle = jnp.repeat(angle, 2)
    sin = jnp.sin(pos[:, None] * angle[None])
    cos = jnp.cos(pos[:, None] * angle[None])
    even = (jnp.arange(RET_HEAD_DIM) % 2 == 0)[None, :]
    sin_s = jnp.where(even, -sin, sin)
    log_gamma = jnp.log(1.0 - jnp.power(2.0, -5.0 - jnp.arange(h, dtype=F32)))
    idx = jnp.arange(c, dtype=F32)
    rel = idx[:, None] - idx[None, :]
    d_intra = jnp.where(rel[None] >= 0, jnp.exp(log_gamma[:, None, None] * jnp.maximum(rel, 0.0)[None]), 0.0)
    xi = jnp.exp(log_gamma[None, :] * (idx[:, None] + 1.0))
    zeta = jnp.exp(log_gamma[None, :] * (c - 1.0 - idx[:, None]))
    decay = jnp.exp(log_gamma * c)
    xi_b = jnp.broadcast_to(xi.T[:, :, None], (h, c, RET_HEAD_DIM))
    zeta_b = jnp.broadcast_to(zeta.T[:, :, None], (h, c, RET_HEAD_DIM))
    return cos, sin_s, d_intra, jnp.swapaxes(d_intra, 1, 2), xi_b, zeta_b, decay


def _to_shards(full, cols):
    r = full.shape[0]
    return jnp.swapaxes(full.reshape(r, N_DEV, cols), 0, 1)


def _from_shards(sh):
    n, r, cols = sh.shape
    return jnp.swapaxes(sh, 0, 1).reshape(r, n * cols)


SMALL_ROWS = 216


def _pack_small(gains, conv_b, conv_w, sinks):
    parts = [g.reshape(8, 128) for g in gains] + [conv_b.reshape(44, 128), conv_w.reshape(132, 128),
                                                  jnp.pad(sinks.reshape(1, 8), ((0, 0), (0, 120)))]
    packed = jnp.concatenate(parts, axis=0)
    return jnp.pad(packed, ((0, SMALL_ROWS - packed.shape[0]), (0, 0)))


def kernel(x, mix_pre_norm, w_in, attn_sinks, w_out, mix_post_norm, ffn_pre_norm, w_up, conv_w, conv_b, w_down, ffn_post_norm, loss_target, m_mix_pre_norm, m_w_in, m_attn_sinks, m_w_out, m_mix_post_norm, m_ffn_pre_norm, m_w_up, m_conv_w, m_conv_b, m_w_down, m_ffn_post_norm, v_mix_pre_norm, v_w_in, v_attn_sinks, v_w_out, v_mix_post_norm, v_ffn_pre_norm, v_w_up, v_conv_w, v_conv_b, v_w_down, v_ffn_post_norm):
    T = x.shape[1]
    tm = min(512, T)
    tm_big = min(1024, T)
    nbs = min(4, T // BLOCK)
    x2 = x.reshape(T, D_MODEL)
    target = loss_target.reshape(T, D_MODEL)
    me = 4 * lax.axis_index("x") + 2 * lax.axis_index("y") + lax.axis_index("c")

    g_in, g_cw = _exchange_call([w_in[0].astype(BF), conv_w[0]], [], "gather_w_in")
    w_in_f = _from_shards(g_in)
    cos, sin_s, d_intra, d_intra_t, xi_b, zeta_b, decay = _tables(T)
    sinks = attn_sinks.reshape(N_ATTN_HEADS)

    (h1, qa, ka, va, qr, kr, vr, gr), (w_up8,) = _in_proj(
        x2, mix_pre_norm, w_in_f, cos, sin_s, tm_big, [w_up[0].astype(BF)])
    (a, lse, lse_t), (g_down, g_out) = _attn_fwd(sinks, qa, ka, va, nbs,
                                                 [w_down[0].astype(BF), w_out[0].astype(BF)])
    w_out_f = g_out.reshape(D_MODEL, D_MODEL)
    o, states, r = _ret_fwd(decay, qr, kr, vr, gr, d_intra, xi_b, zeta_b, nbs)
    mixed, x1, h2 = _out_proj(a, r, w_out_f, x2, mix_post_norm, ffn_pre_norm, tm_big)
    w_down4 = g_down.reshape(FF_PAIRS, FF_SHARD, D_MODEL)
    up_g, up_v, u_g, u_v, y4, dout, dz, dg4, loss_acc = _ffn_fwd(
        h2, w_up8, g_cw, conv_b.reshape(N_DEV, 1, FF_SHARD), w_down4, x1, ffn_post_norm, target, tm)
    loss = lax.psum(loss_acc[0, 0], ("x", "y", "c"))

    dup_g, dup_v, dcb_g, dcb_v, dcw_g, dcw_v, gwu_g, gwu_v, gw_down4 = _ffn_bwd_a(
        dz, h2, w_down4, u_g, u_v, up_g, up_v, y4, g_cw, tm)
    dcb = jnp.concatenate([dcb_g, dcb_v], axis=0).reshape(1, 2 * D_FF)
    dcw = _from_shards(jnp.concatenate([dcw_g, dcw_v], axis=0))
    gw_down = gw_down4.reshape(D_FF, D_MODEL)
    (dx1, dmixed, da, dr, dg3, dg2), p_up = _ffn_bwd_b(
        dup_g, dup_v, w_up8, x1, dout, ffn_pre_norm, mixed, mix_post_norm, w_out_f, tm, gwu_g, gwu_v)
    gw_out = _wgrad([a, r], [dmixed], tm, "wgrad_out")
    dret, (p_down,) = _ret_bwd(decay, qr, kr, vr, gr, o, states, dr, d_intra, d_intra_t, xi_b, zeta_b, cos, sin_s,
                               [gw_down.reshape(N_DEV, D_FF // N_DEV, D_MODEL)], nbs)
    (dqa, delta_t, dsink), (p_out,) = _attn_bwd_dq(sinks, qa, ka, va, da, lse, nbs,
                                                   [gw_out.reshape(N_DEV, D_MODEL // N_DEV, D_MODEL)])
    dka, dva = _attn_bwd_dkv(qa, ka, va, da, lse_t, delta_t, nbs)
    grad_x, dg1 = _in_proj_bwd(dqa, dka, dva, dret, w_in_f, x2, mix_pre_norm, dx1, tm)
    gw_in = _wgrad([h1], [dqa, dka, dva, dret], tm, "wgrad_in")

    small = _pack_small([dg1, dg2, dg3, dg4], dcb, dcw, dsink[:, 0])
    small_all, p_in = _exchange_call([small], [_to_shards(gw_in, IN_W // N_DEV).astype(BF)], "exchange_last")
    g_small = _sum_small(small_all)

    g_w_in, d_w_in, nm_w_in, nv_w_in = _adamw_shard(p_in, w_in[0], m_w_in[0], v_w_in[0], 256, "adamw_in")
    g_w_up, d_w_up, nm_w_up, nv_w_up = _adamw_shard(p_up, w_up[0], m_w_up[0], v_w_up[0], 256, "adamw_up")
    g_w_out, d_w_out, nm_w_out, nv_w_out = _adamw_shard(p_out, w_out[0], m_w_out[0], v_w_out[0], 128, "adamw_out")
    g_w_down, d_w_down, nm_w_down, nv_w_down = _adamw_shard(p_down, w_down[0], m_w_down[0], v_w_down[0], 176,
                                                            "adamw_down")
    gains = [mix_pre_norm, mix_post_norm, ffn_pre_norm, ffn_post_norm]
    m_gains = [m_mix_pre_norm, m_mix_post_norm, m_ffn_pre_norm, m_ffn_post_norm]
    v_gains = [v_mix_pre_norm, v_mix_post_norm, v_ffn_pre_norm, v_ffn_post_norm]
    zeros_cw = jnp.zeros((3, 2 * D_FF), F32)
    w_small = _pack_small(gains, conv_b, zeros_cw, attn_sinks)
    m_small = _pack_small(m_gains, m_conv_b, zeros_cw, m_attn_sinks)
    v_small = _pack_small(v_gains, v_conv_b, zeros_cw, v_attn_sinks)
    d_small, nm_small, nv_small = _adamw_small(g_small, w_small, m_small, v_small, "adamw_small")
    shard_cols = 2 * D_FF // N_DEV
    g_cw = lax.dynamic_slice(g_small[76:208].reshape(3, 2 * D_FF), (0, me * shard_cols), (3, shard_cols))
    d_cw, nm_cw, nv_cw = _adamw_small(g_cw, conv_w[0], m_conv_w[0], v_conv_w[0], "adamw_conv_w")

    def unpack(p):
        gains_o = [p[8 * i:8 * i + 8].reshape(1, D_MODEL) for i in range(4)]
        return gains_o, p[32:76].reshape(1, 2 * D_FF), p[208:209, :N_ATTN_HEADS]

    def leaves(p, w_in_s, w_out_s, w_up_s, cw_s, w_down_s):
        (pre1, post1, pre2, post2), cb, sk = unpack(p)
        return [pre1, w_in_s[None], sk, w_out_s[None], post1, pre2, w_up_s[None], cw_s[None], cb, w_down_s[None],
                post2]

    return (loss, grad_x.reshape(1, T, D_MODEL),
            *leaves(g_small, g_w_in, g_w_out, g_w_up, g_cw, g_w_down),
            *leaves(d_small, d_w_in, d_w_out, d_w_up, d_cw, d_w_down),
            *leaves(nm_small, nm_w_in, nm_w_out, nm_w_up, nm_cw, nm_w_down),
            *leaves(nv_small, nv_w_in, nv_w_out, nv_w_up, nv_cw, nv_w_down))
```

```python
import functools
import math

import jax
import jax.numpy as jnp
from jax import lax
from jax.experimental import pallas as pl
from jax.experimental.pallas import tpu as pltpu

F32 = jnp.float32
BF = jnp.bfloat16

N_DEV = 8
D_MODEL = 1024
HEAD_DIM = 64
ATTN_W = 512
N_ATTN_HEADS = 8
KV_W = 128
BLOCK = 128
RET_W = 512
N_RET_HEADS = 4
RET_HEAD_DIM = 128
IN_W = 2816
D_FF = 2816
RMS_EPS = 1e-6
GN_EPS = 1e-6
MASK_VALUE = -1e30
ATTN_SCALE = HEAD_DIM ** -0.5
RET_K_SCALE = RET_HEAD_DIM ** -0.5
GELU_C = math.sqrt(2.0 / math.pi)
GELU_A = 0.044715

ADAM_LR = 0.001
ADAM_B1 = 0.9
ADAM_B2 = 0.999
ADAM_EPS = 1e-08
ADAM_WD = 0.01
ADAM_STEP = 10

VMEM_LIMIT_BYTES = 56 * 1024 * 1024
FF_SHARD = 2 * D_FF // N_DEV
FF_PAIRS = N_DEV // 2

QA0, KA0, VA0, QR0, KR0, VR0, GR0 = 0, 512, 640, 768, 1280, 1792, 2304

MESH_ID = pl.DeviceIdType.MESH


def _pcall(body, **kw):
    return pl.pallas_call(body, **kw)


def _params(sem=None):
    return pltpu.CompilerParams(dimension_semantics=sem, vmem_limit_bytes=VMEM_LIMIT_BYTES)


def _dot(a, b):
    return jnp.dot(a, b, preferred_element_type=F32)


def _dot_nt(a, b):
    return lax.dot_general(a, b, (((1,), (1,)), ((), ())), preferred_element_type=F32)


def _dot_tn(a, b):
    return lax.dot_general(a, b, (((0,), (0,)), ((), ())), preferred_element_type=F32)


def _vmem_full():
    return pl.BlockSpec(memory_space=pltpu.VMEM)


def _smem_full():
    return pl.BlockSpec(memory_space=pltpu.SMEM)


def _rows(tm, w):
    return pl.BlockSpec((tm, w), lambda i: (i, 0))


def _const(shape):
    return pl.BlockSpec(shape, lambda i: tuple(0 for _ in shape))


def _rms_stats(x):
    r = lax.rsqrt(jnp.mean(x * x, axis=-1, keepdims=True) + RMS_EPS)
    return r, x * r


def _rms_bwd(n, r, dn):
    return r * (dn - n * jnp.mean(dn * n, axis=-1, keepdims=True))


def _rot(x, even):
    w = x.shape[1]
    return jnp.where(even, pltpu.roll(x, w - 1, 1), pltpu.roll(x, 1, 1))


def _peers():
    x, y, c = lax.axis_index("x"), lax.axis_index("y"), lax.axis_index("c")
    flips = [(0, 0, 1), (1, 0, 0), (0, 1, 0), (1, 1, 0), (1, 0, 1), (0, 1, 1), (1, 1, 1)]
    peers = [(x ^ fx, y ^ fy, c ^ fc) for fx, fy, fc in flips]
    return 4 * x + 2 * y + c, peers


SAME_CORE_PEERS = 4


class _Exchange:
    def __init__(self, gathers, swaps, send_sems, recv_sems, local_sems):
        self.me, self.peers = _peers()
        self.slots = [4 * px + 2 * py + pc for px, py, pc in self.peers]
        self.pairs = [(src, dst, True) for src, dst in gathers] + [(src, dst, False) for src, dst in swaps]
        self.send_sems, self.recv_sems, self.local_sems = send_sems, recv_sems, local_sems

    @staticmethod
    def scratch(n):
        return [pltpu.SemaphoreType.DMA((n, N_DEV - 1)), pltpu.SemaphoreType.DMA((n, N_DEV - 1)),
                pltpu.SemaphoreType.DMA((n,))]

    def _parts(self, a, slot):
        src, _, whole = self.pairs[a]
        half = N_DEV // 2
        if whole:
            return [(None, src)]
        if isinstance(src, tuple):
            return [(slot < half, src[0].at[jnp.minimum(slot, half - 1)]),
                    (slot >= half, src[1].at[jnp.maximum(slot - half, 0)])]
        return [(None, src.at[slot])]

    def _local(self, a, src):
        return pltpu.make_async_copy(src, self.pairs[a][1].at[self.me], self.local_sems.at[a])

    def _remote(self, a, k, src, slot):
        return pltpu.make_async_remote_copy(
            src_ref=src, dst_ref=self.pairs[a][1].at[slot], send_sem=self.send_sems.at[a, k],
            recv_sem=self.recv_sems.at[a, k], device_id=self.peers[k], device_id_type=MESH_ID)

    def start(self):
        def go(cond, copy):
            if cond is None:
                copy.start()
            else:
                pl.when(cond)(copy.start)

        for a in range(len(self.pairs)):
            for cond, src in self._parts(a, self.me):
                go(cond, self._local(a, src))
            for k in range(SAME_CORE_PEERS if self.pairs[a][2] else N_DEV - 1):
                for cond, src in self._parts(a, self.slots[k]):
                    go(cond, self._remote(a, k, src, self.me))

    def _pass_on(self, a, j):
        k = j + SAME_CORE_PEERS - 1
        block = self.pairs[a][1].at[self.slots[j]]
        return pltpu.make_async_remote_copy(
            src_ref=block, dst_ref=block, send_sem=self.send_sems.at[a, k], recv_sem=self.recv_sems.at[a, k],
            device_id=self.peers[0], device_id_type=MESH_ID)

    def wait(self):
        for a in range(len(self.pairs)):
            src = self._parts(a, self.me)[0][1]
            if self.pairs[a][2]:
                for j in range(1, SAME_CORE_PEERS):
                    self._remote(a, j, src, self.slots[j]).wait_recv()
                    self._pass_on(a, j).start()
                self._remote(a, 0, src, self.slots[0]).wait_recv()
            for k in range(SAME_CORE_PEERS if self.pairs[a][2] else 0, N_DEV - 1):
                self._remote(a, k, src, self.slots[k]).wait_recv()
        for a in range(len(self.pairs)):
            src = self._parts(a, self.me)[0][1]
            for k in range(SAME_CORE_PEERS if self.pairs[a][2] else N_DEV - 1):
                self._remote(a, k, src, self.me).wait_send()
            if self.pairs[a][2]:
                for j in range(1, SAME_CORE_PEERS):
                    self._pass_on(a, j).wait_send()
            self._local(a, src).wait()


ANY_SPEC = pl.BlockSpec(memory_space=pl.ANY)


def _exchange_shapes(gathers, swaps):
    return ([jax.ShapeDtypeStruct((N_DEV,) + a.shape, a.dtype) for a in gathers]
            + [jax.ShapeDtypeStruct(a.shape, a.dtype) for a in swaps])


def _exchange_of(ins, outs, sems, ng):
    return _Exchange(list(zip(ins[:ng], outs[:ng])), list(zip(ins[ng:], outs[ng:])), *sems)


def _exchange_call(gathers, swaps, name):
    ng, ns = len(gathers), len(swaps)
    n = ng + ns

    def body(*refs):
        ex = _exchange_of(refs[:n], refs[n:2 * n], refs[2 * n:], ng)
        ex.start()
        ex.wait()

    return _pcall(
        body, name=name, out_shape=_exchange_shapes(gathers, swaps),
        in_specs=[ANY_SPEC] * (ng + ns), out_specs=[ANY_SPEC] * (ng + ns),
        scratch_shapes=_Exchange.scratch(ng + ns),
    )(*gathers, *swaps)


def _in_proj(x, g1, w_in, cos, sin_s, tm, gathers):
    T = x.shape[0]
    ng = len(gathers)
    nt = T // tm

    def body(x_ref, g_ref, w_ref, cos_ref, sin_ref, *rest):
        ex = _exchange_of(rest[:ng], rest[ng + 8:2 * ng + 8], rest[2 * ng + 8:], ng)
        h_ref, qa_ref, ka_ref, va_ref, qr_ref, kr_ref, vr_ref, gr_ref = rest[ng:ng + 8]
        pl.when(pl.program_id(0) == 0)(ex.start)
        r, n = _rms_stats(x_ref[...])
        h = (n * g_ref[...]).astype(BF)
        h_ref[...] = h

        def proj(c0, w):
            return _dot(h, w_ref[:, c0:c0 + w])

        qa_ref[...] = proj(QA0, ATTN_W).astype(BF)
        kva = proj(KA0, 2 * KV_W)
        ka_ref[...] = kva[:, :KV_W].astype(BF)
        va_ref[...] = kva[:, KV_W:].astype(BF)
        vr_ref[...] = proj(VR0, RET_W).astype(BF)
        gr_ref[...] = proj(GR0, RET_W)
        cos_t, sin_t = cos_ref[...], sin_ref[...]
        even = lax.broadcasted_iota(jnp.int32, (tm, RET_HEAD_DIM), 1) % 2 == 0
        for c0, scale, out_ref in ((QR0, None, qr_ref), (KR0, RET_K_SCALE, kr_ref)):
            full = proj(c0, RET_W)
            for hd in range(N_RET_HEADS):
                cs = slice(hd * RET_HEAD_DIM, (hd + 1) * RET_HEAD_DIM)
                t = full[:, cs] if scale is None else full[:, cs] * scale
                out_ref[:, cs] = (t * cos_t + _rot(t, even) * sin_t).astype(BF)
        pl.when(pl.program_id(0) == nt - 1)(ex.wait)

    widths = [D_MODEL, ATTN_W, KV_W, KV_W, RET_W, RET_W, RET_W, RET_W]
    dts = [BF] * 7 + [F32]
    outs = _pcall(
        body, name="in_proj", grid=(nt,),
        in_specs=[_rows(tm, D_MODEL), _const((1, D_MODEL)), _vmem_full(), _rows(tm, RET_HEAD_DIM),
                  _rows(tm, RET_HEAD_DIM)] + [ANY_SPEC] * ng,
        out_specs=[_rows(tm, w) for w in widths] + [ANY_SPEC] * ng,
        out_shape=[jax.ShapeDtypeStruct((T, w), dt) for w, dt in zip(widths, dts)] + _exchange_shapes(gathers, []),
        scratch_shapes=_Exchange.scratch(ng),
        compiler_params=_params(("arbitrary",)),
    )(x, g1, w_in, cos, sin_s, *gathers)
    return outs[:8], outs[8:]


def _kv_variants(kk):
    kf = kk.astype(F32)
    lo = lax.broadcasted_iota(jnp.int32, kf.shape, 1) < HEAD_DIM
    h0_lo = jnp.where(lo, kf, 0.0)
    h1_hi = jnp.where(lo, 0.0, kf)
    h0_hi = pltpu.roll(h0_lo, HEAD_DIM, 1)
    h1_lo = pltpu.roll(h1_hi, HEAD_DIM, 1)
    return [[h0_lo.astype(BF), h0_hi.astype(BF)], [h1_lo.astype(BF), h1_hi.astype(BF)]]


def _col_to_tile(tile, col, head):
    lane = lax.broadcasted_iota(jnp.int32, tile.shape, 1)
    return jnp.where(lane == head, col, tile)


def _tri(rows, key_major=False):
    i = lax.broadcasted_iota(jnp.int32, (rows, BLOCK), 0) & (BLOCK - 1)
    j = lax.broadcasted_iota(jnp.int32, (rows, BLOCK), 1)
    return i > j if key_major else j > i


def _fold(x2, tri, first_above):
    a, b = x2[:, :BLOCK], x2[:, BLOCK:]
    return jnp.where(tri, a, b) if first_above else jnp.where(tri, b, a)


def _unfold(x, tri, first_above):
    up, low = jnp.where(tri, x, 0.0), jnp.where(tri, 0.0, x)
    return jnp.concatenate([up, low] if first_above else [low, up], axis=1).astype(BF)


def _scaled(q):
    return (q.astype(F32) * ATTN_SCALE).astype(BF)


def _cat_variants(prev, cur):
    return [[jnp.concatenate([prev[h][e], cur[h][e]], axis=0) for e in range(2)] for h in range(2)]


def _block_variants(prev_ref, cur_ref, nbs):
    var = [_kv_variants(prev_ref[...])] + [_kv_variants(cur_ref[b * BLOCK:(b + 1) * BLOCK, :]) for b in range(nbs)]
    return [_cat_variants(var[b], var[b + 1]) for b in range(nbs)]


def _head_cols(col, nbs):
    tiles = []
    for b in range(nbs):
        t = jnp.zeros((BLOCK, BLOCK), F32)
        for head in range(N_ATTN_HEADS):
            r0 = (b * N_ATTN_HEADS + head) * BLOCK
            t = _col_to_tile(t, col[r0:r0 + BLOCK, :], head)
        tiles.append(t)
    return tiles


def _attn_fwd(sinks, qa, ka, va, nbs, gathers):
    T = qa.shape[0]
    steps = T // (BLOCK * nbs)
    R = nbs * N_ATTN_HEADS * BLOCK
    ng = len(gathers)

    def body(sink_ref, q_ref, kc_ref, kp_ref, vc_ref, vp_ref, *rest):
        ex = _exchange_of(rest[:ng], rest[ng + 3:2 * ng + 3], rest[2 * ng + 3:], ng)
        a_ref, lse_ref, lset_ref = rest[ng:ng + 3]
        n = pl.program_id(0)
        pl.when(n == 0)(ex.start)
        kcat = _block_variants(kp_ref, kc_ref, nbs)
        vcat = _block_variants(vp_ref, vc_ref, nbs)
        tri1 = _tri(BLOCK)
        tiles = []
        for b in range(nbs):
            for pair in range(N_ATTN_HEADS // 2):
                qp = _scaled(q_ref[b * BLOCK:(b + 1) * BLOCK, pair * 128:(pair + 1) * 128])
                for e in range(2):
                    s = _fold(_dot_nt(qp, kcat[b][pair // 2][e]), tri1, True)
                    if b == 0:
                        s = jnp.where(tri1 & (n == 0), MASK_VALUE, s)
                    tiles.append(s)
        s = jnp.concatenate(tiles, axis=0)
        sink = jnp.concatenate([jnp.full((BLOCK, 1), sink_ref[head], F32)
                                for _ in range(nbs) for head in range(N_ATTN_HEADS)], axis=0)
        m = jnp.maximum(jnp.max(s, axis=-1, keepdims=True), sink)
        p = jnp.exp(s - m)
        z = jnp.sum(p, axis=-1, keepdims=True) + jnp.exp(sink - m)
        p2 = _unfold(p * (1.0 / z), _tri(R), True)
        for b in range(nbs):
            for pair in range(N_ATTN_HEADS // 2):
                r0 = (b * N_ATTN_HEADS + 2 * pair) * BLOCK
                acc = (_dot(p2[r0:r0 + BLOCK, :], vcat[b][pair // 2][0])
                       + _dot(p2[r0 + BLOCK:r0 + 2 * BLOCK, :], vcat[b][pair // 2][1]))
                a_ref[b * BLOCK:(b + 1) * BLOCK, pair * 128:(pair + 1) * 128] = acc.astype(BF)
        for b, t in enumerate(_head_cols(m + jnp.log(z), nbs)):
            lse_ref[b * BLOCK:(b + 1) * BLOCK, :] = t
            lset_ref[:, b * BLOCK:(b + 1) * BLOCK] = t.T[:N_ATTN_HEADS, :]
        pl.when(n == steps - 1)(ex.wait)

    cur = lambda w: pl.BlockSpec((BLOCK * nbs, w), lambda n: (n, 0))
    prev = lambda w: pl.BlockSpec((BLOCK, w), lambda n: (jnp.maximum(n * nbs - 1, 0), 0))
    outs = _pcall(
        body, name="attn_fwd", grid=(steps,),
        in_specs=[_smem_full(), cur(ATTN_W), cur(KV_W), prev(KV_W), cur(KV_W), prev(KV_W)] + [ANY_SPEC] * ng,
        out_specs=[cur(ATTN_W), cur(BLOCK), pl.BlockSpec((N_ATTN_HEADS, BLOCK * nbs), lambda n: (0, n))]
        + [ANY_SPEC] * ng,
        out_shape=[jax.ShapeDtypeStruct((T, ATTN_W), BF), jax.ShapeDtypeStruct((T, BLOCK), F32),
                   jax.ShapeDtypeStruct((N_ATTN_HEADS, T), F32)] + _exchange_shapes(gathers, []),
        scratch_shapes=_Exchange.scratch(ng),
        compiler_params=_params(("arbitrary",)),
    )(sinks, qa, ka, ka, va, va, *gathers)
    return outs[:3], outs[3:]


def _ret_fwd(decay, qr, kr, vr, gr, d_intra, xi_b, zeta_b, ncs):
    T = qr.shape[0]
    nc = T // BLOCK
    H, C = N_RET_HEADS, RET_HEAD_DIM

    def body(decay_ref, q_ref, k_ref, v_ref, g_ref, d_ref, xi_ref, zeta_ref, o_ref, s_ref, r_ref, state):
        @pl.when(pl.program_id(0) == 0)
        def _():
            state[...] = jnp.zeros_like(state)

        pairs = [(b, h) for b in range(ncs) for h in range(H)]
        sl = lambda b, h: (slice(b * BLOCK, (b + 1) * BLOCK), slice(h * C, (h + 1) * C))
        tab = lambda ref: jnp.concatenate([ref[h] for _, h in pairs], axis=0)
        q = [q_ref[sl(b, h)] for b, h in pairs]
        k = [k_ref[sl(b, h)] for b, h in pairs]
        v = [v_ref[sl(b, h)] for b, h in pairs]
        inner = (jnp.concatenate([_dot_nt(q[i], k[i]) for i in range(len(pairs))], axis=0) * tab(d_ref)).astype(BF)
        kz = (jnp.concatenate(k, axis=0).astype(F32) * tab(zeta_ref)).astype(BF)
        o1 = [_dot(inner[i * BLOCK:(i + 1) * BLOCK, :], v[i]) for i in range(len(pairs))]
        kv = [_dot_tn(kz[i * BLOCK:(i + 1) * BLOCK, :], v[i]) for i in range(len(pairs))]
        st_b = [None] * len(pairs)
        for h in range(H):
            st = state[h]
            for b in range(ncs):
                i = b * H + h
                st_b[i] = st.astype(BF)
                s_ref[b, h] = st_b[i]
                st = decay_ref[h] * st + kv[i]
            state[h] = st
        o2 = jnp.concatenate([_dot(q[i], st_b[i]) for i in range(len(pairs))], axis=0)
        o = jnp.concatenate(o1, axis=0) + o2 * tab(xi_ref)
        mu = jnp.mean(o, axis=-1, keepdims=True)
        oc = o - mu
        rs = lax.rsqrt(jnp.mean(oc * oc, axis=-1, keepdims=True) + GN_EPS)
        g = jnp.concatenate([g_ref[sl(b, h)] for b, h in pairs], axis=0)
        r = (g * jax.nn.sigmoid(g) * (oc * rs)).astype(BF)
        for i, (b, h) in enumerate(pairs):
            o_ref[sl(b, h)] = o[i * BLOCK:(i + 1) * BLOCK, :]
            r_ref[sl(b, h)] = r[i * BLOCK:(i + 1) * BLOCK, :]

    cur = pl.BlockSpec((BLOCK * ncs, RET_W), lambda n: (n, 0))
    tab = pl.BlockSpec((H, C, C), lambda n: (0, 0, 0))
    return _pcall(
        body, name="ret_fwd", grid=(nc // ncs,),
        in_specs=[_smem_full(), cur, cur, cur, cur, tab, tab, tab],
        out_specs=[cur, pl.BlockSpec((ncs, H, C, C), lambda n: (n, 0, 0, 0)), cur],
        out_shape=[jax.ShapeDtypeStruct((T, RET_W), F32), jax.ShapeDtypeStruct((nc, H, C, C), BF),
                   jax.ShapeDtypeStruct((T, RET_W), BF)],
        scratch_shapes=[pltpu.VMEM((H, C, C), F32)],
        compiler_params=_params(("arbitrary",)),
    )(decay, qr, kr, vr, gr, d_intra, xi_b, zeta_b)


def _out_proj(a, r, w_out, x, g2, g3, tm):
    T = x.shape[0]

    def body(a_ref, r_ref, w_ref, x_ref, g2_ref, g3_ref, mixed_ref, x1_ref, h2_ref):
        mixed = _dot(a_ref[...], w_ref[:ATTN_W, :]) + _dot(r_ref[...], w_ref[ATTN_W:, :])
        mixed_ref[...] = mixed
        _, n2 = _rms_stats(mixed)
        x1 = x_ref[...] + n2 * g2_ref[...]
        x1_ref[...] = x1
        _, n3 = _rms_stats(x1)
        h2_ref[...] = (n3 * g3_ref[...]).astype(BF)

    return _pcall(
        body, name="out_proj", grid=(T // tm,),
        in_specs=[_rows(tm, ATTN_W), _rows(tm, RET_W), _vmem_full(), _rows(tm, D_MODEL), _const((1, D_MODEL)),
                  _const((1, D_MODEL))],
        out_specs=[_rows(tm, D_MODEL)] * 3,
        out_shape=[jax.ShapeDtypeStruct((T, D_MODEL), F32), jax.ShapeDtypeStruct((T, D_MODEL), F32),
                   jax.ShapeDtypeStruct((T, D_MODEL), BF)],
        compiler_params=_params(("parallel",)),
    )(a, r, w_out, x, g2, g3)


def _shift_down(cur, k, before):
    out = pltpu.roll(cur, k, 0)
    row = lax.broadcasted_iota(jnp.int32, before.shape, 0)
    top = jnp.where(row < k, pltpu.roll(before, k, 0), out[0:8])
    return jnp.concatenate([top, out[8:]], axis=0)


def _shift_up(cur, k, after):
    tm = cur.shape[0]
    out = pltpu.roll(cur, tm - k, 0)
    row = lax.broadcasted_iota(jnp.int32, after.shape, 0)
    bot = jnp.where(row >= 8 - k, pltpu.roll(after, 8 - k, 0), out[tm - 8:])
    return jnp.concatenate([out[:tm - 8], bot], axis=0)


def _gelu_parts(x):
    x2 = x * x
    th = jnp.tanh(GELU_C * (x + GELU_A * x * x2))
    gelu = 0.5 * x * (1.0 + th)
    dgelu = 0.5 * (1.0 + th) + 0.5 * x * (1.0 - th * th) * (GELU_C * (1.0 + 3.0 * GELU_A * x2))
    return gelu, dgelu


def _ffn_fwd(h2, w_up8, conv_w8, conv_b8, w_down4, x1, g4, target, tm):
    T = h2.shape[0]
    nt = T // tm

    def body(h_ref, wu_ref, cwg_ref, cwv_ref, cbg_ref, cbv_ref, wd_ref, x1_ref, g_ref, t_ref,
             upg_ref, upv_ref, ug_ref, uv_ref, y_ref, dout_ref, dz_ref, dg4_ref, loss_ref, halo, z_acc):
        s = pl.program_id(1)
        first = pl.program_id(0) == 0

        @pl.when(first & (s == 0))
        def _():
            loss_ref[...] = jnp.zeros_like(loss_ref)
            dg4_ref[...] = jnp.zeros_like(dg4_ref)

        h = h_ref[...]
        u = []
        parts = ((cwg_ref, cbg_ref, upg_ref, ug_ref), (cwv_ref, cbv_ref, upv_ref, uv_ref))
        for part, (cw_ref, cb_ref, up_ref, u_ref) in enumerate(parts):
            cur = _dot(h, wu_ref[s + part * FF_PAIRS])
            up_ref[0] = cur.astype(BF)
            before = jnp.where(first, 0.0, halo[part, s])
            halo[part, s] = cur[tm - 8:tm, :]
            u_c = (cw_ref[0, pl.ds(0, 1), :] * _shift_down(cur, 2, before)
                   + cw_ref[0, pl.ds(1, 1), :] * _shift_down(cur, 1, before)
                   + cw_ref[0, pl.ds(2, 1), :] * cur + cb_ref[0])
            u_ref[0] = u_c
            u.append(u_c)
        gelu, _ = _gelu_parts(u[0])
        y = (gelu * u[1]).astype(BF)
        y_ref[0] = y
        z_part = _dot(y, wd_ref[s])

        @pl.when(s == 0)
        def _():
            z_acc[...] = z_part

        @pl.when(s > 0)
        def _():
            z_acc[...] += z_part

        @pl.when(s == FF_PAIRS - 1)
        def _():
            r4, n4 = _rms_stats(z_acc[...])
            err = x1_ref[...] + n4 * g_ref[...] - t_ref[...]
            dout = err * (1.0 / D_MODEL)
            dout_ref[...] = dout
            loss_ref[...] += 0.5 * jnp.sum(jnp.mean(err * err, axis=-1, keepdims=True), axis=0, keepdims=True)
            dg4_ref[...] += jnp.sum(dout * n4, axis=0, keepdims=True)
            dz_ref[...] = _rms_bwd(n4, r4, dout * g_ref[...]).astype(BF)

    rows = pl.BlockSpec((tm, D_MODEL), lambda i, s: (i, 0))
    one = lambda shape: pl.BlockSpec(shape, lambda i, s: tuple(0 for _ in shape))
    gate = lambda r, w: pl.BlockSpec((1, r, w), lambda i, s: (s, 0, 0))
    val = lambda r, w: pl.BlockSpec((1, r, w), lambda i, s: (s + FF_PAIRS, 0, 0))
    tile = pl.BlockSpec((1, tm, FF_SHARD), lambda i, s: (s, i, 0))
    half = lambda dt: jax.ShapeDtypeStruct((FF_PAIRS, T, FF_SHARD), dt)
    return _pcall(
        body, name="ffn_fwd", grid=(nt, FF_PAIRS),
        in_specs=[rows, _vmem_full(), gate(3, FF_SHARD), val(3, FF_SHARD), gate(1, FF_SHARD), val(1, FF_SHARD),
                  _vmem_full(), rows, one((1, D_MODEL)), rows],
        out_specs=[tile] * 5 + [rows, rows, one((1, D_MODEL)), one((8, 128))],
        out_shape=[half(BF), half(BF), half(F32), half(F32), half(BF), jax.ShapeDtypeStruct((T, D_MODEL), F32),
                   jax.ShapeDtypeStruct((T, D_MODEL), BF), jax.ShapeDtypeStruct((1, D_MODEL), F32),
                   jax.ShapeDtypeStruct((8, 128), F32)],
        scratch_shapes=[pltpu.VMEM((2, FF_PAIRS, 8, FF_SHARD), F32), pltpu.VMEM((tm, D_MODEL), F32)],
        compiler_params=_params(("arbitrary", "arbitrary")),
    )(h2, w_up8, conv_w8, conv_w8, conv_b8, conv_b8, w_down4, x1, g4, target)


def _ffn_bwd_a(dz, h2, w_down4, u_g, u_v, up_g, up_v, y4, conv_w8, tm):
    T = dz.shape[0]
    nt = T // tm

    def body(dz_ref, h_ref, wd_ref, ug_ref, uv_ref, upg_ref, upv_ref, y_ref, cwg_ref, cwv_ref,
             dupg_ref, dupv_ref, dcbg_ref, dcbv_ref, dcwg_ref, dcwv_ref, gwug_ref, gwuv_ref, gwd_ref, carry):
        @pl.when(pl.program_id(1) == 0)
        def _():
            for ref in (dcbg_ref, dcbv_ref, dcwg_ref, dcwv_ref, gwug_ref, gwuv_ref, gwd_ref, carry):
                ref[...] = jnp.zeros_like(ref)

        dz = dz_ref[...]
        h = h_ref[...]
        dy = _dot_nt(dz, wd_ref[0])
        gwd_ref[0] += _dot_tn(y_ref[0], dz)
        gelu, dgelu = _gelu_parts(ug_ref[0])
        parts = ((0, dy * uv_ref[0] * dgelu, upg_ref, cwg_ref, dupg_ref, dcbg_ref, dcwg_ref, gwug_ref),
                 (1, dy * gelu, upv_ref, cwv_ref, dupv_ref, dcbv_ref, dcwv_ref, gwuv_ref))
        for part, d, up_ref, cw_ref, dup_ref, dcb_ref, dcw_ref, gwu_ref in parts:
            after = carry[part]
            d1 = _shift_up(d, 1, after)
            d2 = _shift_up(d, 2, after)
            carry[part] = d[0:8, :]
            upc = up_ref[0].astype(F32)
            dcb_ref[0] += jnp.sum(d, axis=0, keepdims=True)
            dcw_ref[0, pl.ds(2, 1), :] += jnp.sum(d * upc, axis=0, keepdims=True)
            dcw_ref[0, pl.ds(1, 1), :] += jnp.sum(d1 * upc, axis=0, keepdims=True)
            dcw_ref[0, pl.ds(0, 1), :] += jnp.sum(d2 * upc, axis=0, keepdims=True)
            dup = (cw_ref[0, pl.ds(2, 1), :] * d + cw_ref[0, pl.ds(1, 1), :] * d1
                   + cw_ref[0, pl.ds(0, 1), :] * d2).astype(BF)
            dup_ref[0] = dup
            gwu_ref[0] += _dot_tn(h, dup)

    rev = pl.BlockSpec((tm, D_MODEL), lambda s, i: (nt - 1 - i, 0))
    tile = pl.BlockSpec((1, tm, FF_SHARD), lambda s, i: (s, nt - 1 - i, 0))
    acc = lambda r, w: pl.BlockSpec((1, r, w), lambda s, i: (s, 0, 0))
    acc_val = pl.BlockSpec((1, 3, FF_SHARD), lambda s, i: (s + FF_PAIRS, 0, 0))
    half = lambda r, dt: jax.ShapeDtypeStruct((FF_PAIRS, r, FF_SHARD), dt)
    return _pcall(
        body, name="ffn_bwd_a", grid=(FF_PAIRS, nt),
        in_specs=[rev, rev, acc(FF_SHARD, D_MODEL), tile, tile, tile, tile, tile, acc(3, FF_SHARD), acc_val],
        out_specs=[tile, tile, acc(1, FF_SHARD), acc(1, FF_SHARD), acc(3, FF_SHARD), acc(3, FF_SHARD),
                   acc(D_MODEL, FF_SHARD), acc(D_MODEL, FF_SHARD), acc(FF_SHARD, D_MODEL)],
        out_shape=[half(T, BF), half(T, BF), half(1, F32), half(1, F32), half(3, F32), half(3, F32),
                   half(D_MODEL, F32), half(D_MODEL, F32), jax.ShapeDtypeStruct((FF_PAIRS, FF_SHARD, D_MODEL), F32)],
        scratch_shapes=[pltpu.VMEM((2, 8, FF_SHARD), F32)],
        compiler_params=_params(("arbitrary", "arbitrary")),
    )(dz, h2, w_down4, u_g, u_v, up_g, up_v, y4, conv_w8, conv_w8)


def _ffn_bwd_b(dup_g, dup_v, w_up8, x1, dout, g3, mixed, g2, w_out, tm, gwu_g, gwu_v):
    T = x1.shape[0]
    nt = T // tm

    def body(dupg_ref, dupv_ref, wup_ref, x1_ref, dout_ref, g3_ref, mixed_ref, g2_ref, wout_ref, gwug_ref, gwuv_ref,
             dx1_ref, dmixed_ref, da_ref, dr_ref, dg3_ref, dg2_ref, pup_ref, *sems):
        ex = _Exchange([], [((gwug_ref, gwuv_ref), pup_ref)], *sems)

        @pl.when(pl.program_id(0) == 0)
        def _():
            ex.start()
            dg3_ref[...] = jnp.zeros_like(dg3_ref)
            dg2_ref[...] = jnp.zeros_like(dg2_ref)

        dh2 = jnp.zeros((tm, D_MODEL), F32)
        for s in range(FF_PAIRS):
            dh2 = dh2 + _dot_nt(dupg_ref[s], wup_ref[s]) + _dot_nt(dupv_ref[s], wup_ref[s + FF_PAIRS])
        r3, n3 = _rms_stats(x1_ref[...])
        dg3_ref[...] += jnp.sum(dh2 * n3, axis=0, keepdims=True)
        dx1 = dout_ref[...] + _rms_bwd(n3, r3, dh2 * g3_ref[...])
        dx1_ref[...] = dx1
        r2, n2 = _rms_stats(mixed_ref[...])
        dg2_ref[...] += jnp.sum(dx1 * n2, axis=0, keepdims=True)
        dmixed = _rms_bwd(n2, r2, dx1 * g2_ref[...]).astype(BF)
        dmixed_ref[...] = dmixed
        da_ref[...] = _dot_nt(dmixed, wout_ref[:ATTN_W, :])
        dr_ref[...] = _dot_nt(dmixed, wout_ref[ATTN_W:, :])
        pl.when(pl.program_id(0) == nt - 1)(ex.wait)

    half = pl.BlockSpec((FF_PAIRS, tm, FF_SHARD), lambda i: (0, i, 0))
    outs = _pcall(
        body, name="ffn_bwd_b", grid=(nt,),
        in_specs=[half, half, _vmem_full(), _rows(tm, D_MODEL), _rows(tm, D_MODEL), _const((1, D_MODEL)),
                  _rows(tm, D_MODEL), _const((1, D_MODEL)), _vmem_full(), ANY_SPEC, ANY_SPEC],
        out_specs=[_rows(tm, D_MODEL), _rows(tm, D_MODEL), _rows(tm, ATTN_W), _rows(tm, RET_W),
                   _const((1, D_MODEL)), _const((1, D_MODEL)), ANY_SPEC],
        out_shape=[jax.ShapeDtypeStruct((T, D_MODEL), F32), jax.ShapeDtypeStruct((T, D_MODEL), BF),
                   jax.ShapeDtypeStruct((T, ATTN_W), F32), jax.ShapeDtypeStruct((T, RET_W), F32),
                   jax.ShapeDtypeStruct((1, D_MODEL), F32), jax.ShapeDtypeStruct((1, D_MODEL), F32),
                   jax.ShapeDtypeStruct((N_DEV, D_MODEL, FF_SHARD), F32)],
        scratch_shapes=_Exchange.scratch(1),
        compiler_params=_params(("arbitrary",)),
    )(dup_g, dup_v, w_up8, x1, dout, g3, mixed, g2, w_out, gwu_g, gwu_v)
    return outs[:6], outs[6]


def _ret_bwd(decay, qr, kr, vr, gr, o, states, dr, d_intra, d_intra_t, xi_b, zeta_b, cos, sin_s, swaps, ncs):
    T = qr.shape[0]
    nc = T // BLOCK
    H, C = N_RET_HEADS, RET_HEAD_DIM
    ns = len(swaps)

    def body(decay_ref, q_ref, k_ref, v_ref, g_ref, o_ref, s_ref, dr_ref, d_ref, dt_ref, xi_ref, zeta_ref,
             cos_ref, sin_ref, *rest):
        ex = _exchange_of(rest[:ns], rest[ns + 1:2 * ns + 1], rest[2 * ns + 2:], 0)
        dret_ref, gstate = rest[ns], rest[2 * ns + 1]

        @pl.when(pl.program_id(0) == 0)
        def _():
            ex.start()
            gstate[...] = jnp.zeros_like(gstate)

        pairs = [(b, h) for b in range(ncs) for h in range(H)]
        n = len(pairs)
        sl = lambda b, h: (slice(b * BLOCK, (b + 1) * BLOCK), slice(h * C, (h + 1) * C))
        cat = lambda ref: jnp.concatenate([ref[sl(b, h)] for b, h in pairs], axis=0)
        tab = lambda ref: jnp.concatenate([ref[h] for _, h in pairs], axis=0)
        part = lambda x, i: x[i * BLOCK:(i + 1) * BLOCK, :]
        q = [q_ref[sl(b, h)] for b, h in pairs]
        k = [k_ref[sl(b, h)] for b, h in pairs]
        v = [v_ref[sl(b, h)] for b, h in pairs]
        g, o_all, dr_all = cat(g_ref), cat(o_ref), cat(dr_ref)
        mu = jnp.mean(o_all, axis=-1, keepdims=True)
        oc = o_all - mu
        rs = lax.rsqrt(jnp.mean(oc * oc, axis=-1, keepdims=True) + GN_EPS)
        on = oc * rs
        sg = jax.nn.sigmoid(g)
        dg = (dr_all * on * (sg * (1.0 + g * (1.0 - sg)))).astype(BF)
        don = dr_all * (g * sg)
        do = rs * (don - jnp.mean(don, axis=-1, keepdims=True) - on * jnp.mean(don * on, axis=-1, keepdims=True))
        do_b = do.astype(BF)
        dox_b = (do * tab(xi_ref)).astype(BF)
        zeta = tab(zeta_ref)
        kz = (jnp.concatenate(k, axis=0).astype(F32) * zeta).astype(BF)
        d_t = tab(dt_ref)
        da_b = (jnp.concatenate([_dot_nt(part(do_b, i), v[i]) for i in range(n)], axis=0) * tab(d_ref)).astype(BF)
        dat_b = (jnp.concatenate([_dot_nt(v[i], part(do_b, i)) for i in range(n)], axis=0) * d_t).astype(BF)
        mt_b = (jnp.concatenate([_dot_nt(k[i], q[i]) for i in range(n)], axis=0) * d_t).astype(BF)
        dq = [_dot(part(da_b, i), k[i]) + _dot_nt(part(dox_b, i), s_ref[pairs[i]]) for i in range(n)]
        dk1 = [_dot(part(dat_b, i), q[i]) for i in range(n)]
        dv1 = [_dot(part(mt_b, i), part(do_b, i)) for i in range(n)]
        qtd = [_dot_tn(q[i], part(dox_b, i)) for i in range(n)]
        gst_b = [None] * n
        for h in range(H):
            gst = gstate[h]
            for b in reversed(range(ncs)):
                i = b * H + h
                gst_b[i] = gst.astype(BF)
                gst = decay_ref[h] * gst + qtd[i]
            gstate[h] = gst
        dk2 = jnp.concatenate([_dot_nt(v[i], gst_b[i]) for i in range(n)], axis=0) * zeta
        dv = jnp.concatenate([dv1[i] + _dot(part(kz, i), gst_b[i]) for i in range(n)], axis=0).astype(BF)
        even = lax.broadcasted_iota(jnp.int32, (n * BLOCK, C), 1) % 2 == 0
        cos_t = jnp.concatenate([cos_ref[b * BLOCK:(b + 1) * BLOCK, :] for b, _ in pairs], axis=0)
        sin_t = jnp.concatenate([sin_ref[b * BLOCK:(b + 1) * BLOCK, :] for b, _ in pairs], axis=0)
        dq = jnp.concatenate(dq, axis=0)
        dk = jnp.concatenate(dk1, axis=0) + dk2
        dq = (dq * cos_t - _rot(dq, even) * sin_t).astype(BF)
        dk = ((dk * cos_t - _rot(dk, even) * sin_t) * RET_K_SCALE).astype(BF)
        for i, (b, h) in enumerate(pairs):
            rows = slice(b * BLOCK, (b + 1) * BLOCK)
            for j, x in enumerate((dq, dk, dv, dg)):
                dret_ref[rows, j * RET_W + h * C:j * RET_W + (h + 1) * C] = part(x, i)
        pl.when(pl.program_id(0) == steps - 1)(ex.wait)

    steps = nc // ncs
    rev = lambda w: pl.BlockSpec((BLOCK * ncs, w), lambda n: (steps - 1 - n, 0))
    tab = pl.BlockSpec((H, C, C), lambda n: (0, 0, 0))
    outs = _pcall(
        body, name="ret_bwd", grid=(steps,),
        in_specs=[_smem_full(), rev(RET_W), rev(RET_W), rev(RET_W), rev(RET_W), rev(RET_W),
                  pl.BlockSpec((ncs, H, C, C), lambda n: (steps - 1 - n, 0, 0, 0)), rev(RET_W), tab, tab, tab, tab,
                  rev(C), rev(C)] + [ANY_SPEC] * ns,
        out_specs=[rev(4 * RET_W)] + [ANY_SPEC] * ns,
        out_shape=[jax.ShapeDtypeStruct((T, 4 * RET_W), BF)] + _exchange_shapes([], swaps),
        scratch_shapes=[pltpu.VMEM((H, C, C), F32)] + _Exchange.scratch(ns),
        compiler_params=_params(("arbitrary",)),
    )(decay, qr, kr, vr, gr, o, states, dr, d_intra, d_intra_t, xi_b, zeta_b, cos, sin_s, *swaps)
    return outs[0], outs[1:]


def _attn_bwd_dq(sinks, qa, ka, va, da, lse, nbs, swaps):
    T = qa.shape[0]
    steps = T // (BLOCK * nbs)
    R = nbs * N_ATTN_HEADS * BLOCK
    ns = len(swaps)

    def body(sink_ref, q_ref, kc_ref, kp_ref, vc_ref, vp_ref, da_ref, lse_ref, *rest):
        ex = _exchange_of(rest[:ns], rest[ns + 3:2 * ns + 3], rest[2 * ns + 3:], 0)
        dq_ref, deltat_ref, dsink_ref = rest[ns:ns + 3]
        n = pl.program_id(0)

        @pl.when(n == 0)
        def _():
            ex.start()
            dsink_ref[...] = jnp.zeros_like(dsink_ref)

        kcat = _block_variants(kp_ref, kc_ref, nbs)
        vcat = _block_variants(vp_ref, vc_ref, nbs)
        tri1 = _tri(BLOCK)
        lane = lax.broadcasted_iota(jnp.int32, (BLOCK, BLOCK), 1)
        s_tiles, dp_tiles, lse_cols = [], [], []
        for b in range(nbs):
            rows = slice(b * BLOCK, (b + 1) * BLOCK)
            lse_tile = lse_ref[rows, :]
            for pair in range(N_ATTN_HEADS // 2):
                qp = _scaled(q_ref[rows, pair * 128:(pair + 1) * 128])
                dop = da_ref[rows, pair * 128:(pair + 1) * 128].astype(BF)
                for e in range(2):
                    s = _fold(_dot_nt(qp, kcat[b][pair // 2][e]), tri1, True)
                    if b == 0:
                        s = jnp.where(tri1 & (n == 0), MASK_VALUE, s)
                    s_tiles.append(s)
                    dp_tiles.append(_fold(_dot_nt(dop, vcat[b][pair // 2][e]), tri1, True))
                    lse_cols.append(jnp.sum(jnp.where(lane == 2 * pair + e, lse_tile, 0.0), axis=-1, keepdims=True))
        lse_c = jnp.concatenate(lse_cols, axis=0)
        p = jnp.exp(jnp.concatenate(s_tiles, axis=0) - lse_c)
        dp = jnp.concatenate(dp_tiles, axis=0)
        delta = jnp.sum(p * dp, axis=-1, keepdims=True)
        ds2 = _unfold(p * (dp - delta), _tri(R), True)
        for b in range(nbs):
            for pair in range(N_ATTN_HEADS // 2):
                r0 = (b * N_ATTN_HEADS + 2 * pair) * BLOCK
                acc = (_dot(ds2[r0:r0 + BLOCK, :], kcat[b][pair // 2][0])
                       + _dot(ds2[r0 + BLOCK:r0 + 2 * BLOCK, :], kcat[b][pair // 2][1]))
                dq_ref[b * BLOCK:(b + 1) * BLOCK, pair * 128:(pair + 1) * 128] = (acc * ATTN_SCALE).astype(BF)
        for b, t in enumerate(_head_cols(delta, nbs)):
            deltat_ref[:, b * BLOCK:(b + 1) * BLOCK] = t.T[:N_ATTN_HEADS, :]
        sink = jnp.concatenate([jnp.full((BLOCK, 1), sink_ref[head], F32)
                                for _ in range(nbs) for head in range(N_ATTN_HEADS)], axis=0)
        ds_sink = -jnp.exp(sink - lse_c) * delta
        row8 = lax.broadcasted_iota(jnp.int32, (N_ATTN_HEADS, BLOCK), 0)
        dsink = jnp.zeros((N_ATTN_HEADS, BLOCK), F32)
        for b in range(nbs):
            for head in range(N_ATTN_HEADS):
                r0 = (b * N_ATTN_HEADS + head) * BLOCK
                dsink = dsink + jnp.where(row8 == head, jnp.sum(ds_sink[r0:r0 + BLOCK, :], axis=0, keepdims=True), 0.0)
        dsink_ref[...] += dsink
        pl.when(n == steps - 1)(ex.wait)

    cur = lambda w: pl.BlockSpec((BLOCK * nbs, w), lambda n: (n, 0))
    prev = lambda w: pl.BlockSpec((BLOCK, w), lambda n: (jnp.maximum(n * nbs - 1, 0), 0))
    outs = _pcall(
        body, name="attn_bwd_dq", grid=(steps,),
        in_specs=[_smem_full(), cur(ATTN_W), cur(KV_W), prev(KV_W), cur(KV_W), prev(KV_W), cur(ATTN_W), cur(BLOCK)]
        + [ANY_SPEC] * ns,
        out_specs=[cur(ATTN_W), pl.BlockSpec((N_ATTN_HEADS, BLOCK * nbs), lambda n: (0, n)),
                   _const((N_ATTN_HEADS, BLOCK))] + [ANY_SPEC] * ns,
        out_shape=[jax.ShapeDtypeStruct((T, ATTN_W), BF), jax.ShapeDtypeStruct((N_ATTN_HEADS, T), F32),
                   jax.ShapeDtypeStruct((N_ATTN_HEADS, BLOCK), F32)] + _exchange_shapes([], swaps),
        scratch_shapes=_Exchange.scratch(ns),
        compiler_params=_params(("arbitrary",)),
    )(sinks, qa, ka, ka, va, va, da, lse, *swaps)
    return outs[:3], outs[3:]


def _attn_bwd_dkv(qa, ka, va, da, lse_t, delta_t, nbs):
    T = qa.shape[0]
    nb = T // BLOCK
    steps = nb // nbs
    R = nbs * N_ATTN_HEADS * BLOCK

    def body(qc_ref, qn_ref, dac_ref, dan_ref, k_ref, v_ref, lc_ref, ln_ref, dc_ref, dn_ref, dk_ref, dv_ref):
        n = pl.program_id(0)
        tri1 = _tri(BLOCK, True)
        lo = lax.broadcasted_iota(jnp.int32, (BLOCK, 128), 1) < HEAD_DIM
        kv = [_kv_variants(k_ref[b * BLOCK:(b + 1) * BLOCK, :]) for b in range(nbs)]
        vv = [_kv_variants(v_ref[b * BLOCK:(b + 1) * BLOCK, :]) for b in range(nbs)]
        qcat, docat = [], []
        s_tiles, dp_tiles, lse_tiles, delta_tiles = [], [], [], []
        for b in range(nbs):
            rows = slice(b * BLOCK, (b + 1) * BLOCK)
            nrows = slice((b + 1) * BLOCK, (b + 2) * BLOCK)
            inside = b < nbs - 1
            for pair in range(N_ATTN_HEADS // 2):
                ps = slice(pair * 128, (pair + 1) * 128)
                q2 = _scaled(jnp.concatenate([qc_ref[rows, ps], qc_ref[nrows, ps] if inside else qn_ref[:, ps]], axis=0))
                do2 = jnp.concatenate([dac_ref[rows, ps], dac_ref[nrows, ps] if inside else dan_ref[:, ps]],
                                      axis=0).astype(BF)
                qcat.append(q2)
                docat.append(do2)
                for e in range(2):
                    one = pl.ds(2 * pair + e, 1)
                    s = _fold(_dot_nt(kv[b][pair // 2][e], q2), tri1, False)
                    if not inside:
                        s = jnp.where(tri1 & (n == steps - 1), MASK_VALUE, s)
                    s_tiles.append(s)
                    dp_tiles.append(_fold(_dot_nt(vv[b][pair // 2][e], do2), tri1, False))
                    lse_tiles.append(jnp.where(tri1, lc_ref[one, nrows] if inside else ln_ref[one, :], lc_ref[one, rows]))
                    delta_tiles.append(jnp.where(tri1, dc_ref[one, nrows] if inside else dn_ref[one, :],
                                                 dc_ref[one, rows]))
        pt = jnp.exp(jnp.concatenate(s_tiles, axis=0) - jnp.concatenate(lse_tiles, axis=0))
        dst = pt * (jnp.concatenate(dp_tiles, axis=0) - jnp.concatenate(delta_tiles, axis=0))
        tri = _tri(R, True)
        pt2 = _unfold(pt, tri, False)
        dst2 = _unfold(dst, tri, False)
        for b in range(nbs):
            dk = jnp.zeros((BLOCK, 128), F32)
            dv = jnp.zeros((BLOCK, 128), F32)
            for pair in range(N_ATTN_HEADS // 2):
                h = pair // 2
                for e in range(2):
                    r0 = (b * N_ATTN_HEADS + 2 * pair + e) * BLOCK
                    half = lo if e == 0 else jnp.logical_not(lo)
                    dv_e = jnp.where(half, _dot(pt2[r0:r0 + BLOCK, :], docat[b * 4 + pair]), 0.0)
                    dk_e = jnp.where(half, _dot(dst2[r0:r0 + BLOCK, :], qcat[b * 4 + pair]), 0.0)
                    if e != h:
                        dv_e = pltpu.roll(dv_e, HEAD_DIM, 1)
                        dk_e = pltpu.roll(dk_e, HEAD_DIM, 1)
                    dv = dv + dv_e
                    dk = dk + dk_e
            dk_ref[b * BLOCK:(b + 1) * BLOCK, :] = dk.astype(BF)
            dv_ref[b * BLOCK:(b + 1) * BLOCK, :] = dv.astype(BF)

    cur = lambda w: pl.BlockSpec((BLOCK * nbs, w), lambda n: (n, 0))
    nxt = lambda w: pl.BlockSpec((BLOCK, w), lambda n: (jnp.minimum((n + 1) * nbs, nb - 1), 0))
    tcur = pl.BlockSpec((N_ATTN_HEADS, BLOCK * nbs), lambda n: (0, n))
    tnxt = pl.BlockSpec((N_ATTN_HEADS, BLOCK), lambda n: (0, jnp.minimum((n + 1) * nbs, nb - 1)))
    return _pcall(
        body, name="attn_bwd_dkv", grid=(steps,),
        in_specs=[cur(ATTN_W), nxt(ATTN_W), cur(ATTN_W), nxt(ATTN_W), cur(KV_W), cur(KV_W), tcur, tnxt, tcur, tnxt],
        out_specs=[cur(KV_W), cur(KV_W)],
        out_shape=[jax.ShapeDtypeStruct((T, KV_W), BF), jax.ShapeDtypeStruct((T, KV_W), BF)],
        compiler_params=_params(("parallel",)),
    )(qa, qa, da, da, ka, va, lse_t, lse_t, delta_t, delta_t)


def _in_proj_bwd(dqa, dka, dva, dret, w_in, x, g1, dx1, tm):
    T = x.shape[0]

    def body(dqa_ref, dka_ref, dva_ref, dret_ref, w_ref, x_ref, g_ref, dx1_ref, dx_ref, dg1_ref):
        @pl.when(pl.program_id(0) == 0)
        def _():
            dg1_ref[...] = jnp.zeros_like(dg1_ref)

        dh = (_dot_nt(dqa_ref[...], w_ref[:, QA0:QA0 + ATTN_W]) + _dot_nt(dka_ref[...], w_ref[:, KA0:KA0 + KV_W])
              + _dot_nt(dva_ref[...], w_ref[:, VA0:VA0 + KV_W]) + _dot_nt(dret_ref[...], w_ref[:, QR0:IN_W]))
        r, n = _rms_stats(x_ref[...])
        dg1_ref[...] += jnp.sum(dh * n, axis=0, keepdims=True)
        dx_ref[...] = dx1_ref[...] + _rms_bwd(n, r, dh * g_ref[...])

    return _pcall(
        body, name="in_proj_bwd", grid=(T // tm,),
        in_specs=[_rows(tm, ATTN_W), _rows(tm, KV_W), _rows(tm, KV_W), _rows(tm, 4 * RET_W), _vmem_full(),
                  _rows(tm, D_MODEL), _const((1, D_MODEL)), _rows(tm, D_MODEL)],
        out_specs=[_rows(tm, D_MODEL), _const((1, D_MODEL))],
        out_shape=[jax.ShapeDtypeStruct((T, D_MODEL), F32), jax.ShapeDtypeStruct((1, D_MODEL), F32)],
        compiler_params=_params(("arbitrary",)),
    )(dqa, dka, dva, dret, w_in, x, g1, dx1)


def _wgrad(a_list, b_list, tk, name):
    T = a_list[0].shape[0]
    na, nbb = len(a_list), len(b_list)
    m_sizes = [a.shape[1] for a in a_list]
    n_sizes = [b.shape[1] for b in b_list]
    M, N = sum(m_sizes), sum(n_sizes)
    nk = T // tk
    chunk = 512

    def body(*refs):
        a_refs, b_refs = refs[:na], refs[na:na + nbb]
        out_ref, acc = refs[na + nbb], refs[na + nbb + 1]
        k = pl.program_id(0)

        @pl.when(k == 0)
        def _():
            acc[...] = jnp.zeros_like(acc)

        r0 = 0
        for ai in range(na):
            a = a_refs[ai][...]
            c0 = 0
            for bi in range(nbb):
                for s in range(0, n_sizes[bi], chunk):
                    w = min(chunk, n_sizes[bi] - s)
                    acc[r0:r0 + m_sizes[ai], c0 + s:c0 + s + w] += _dot_tn(a, b_refs[bi][:, s:s + w])
                c0 += n_sizes[bi]
            r0 += m_sizes[ai]

        @pl.when(k == nk - 1)
        def _():
            pltpu.sync_copy(acc, out_ref)

    return _pcall(
        body, name=name, grid=(nk,),
        in_specs=[_rows(tk, w) for w in m_sizes + n_sizes],
        out_specs=pl.BlockSpec(memory_space=pl.ANY),
        out_shape=jax.ShapeDtypeStruct((M, N), F32),
        scratch_shapes=[pltpu.VMEM((M, N), F32)],
        compiler_params=_params(("arbitrary",)),
    )(*a_list, *b_list)


def _adamw_math(w, g, m, v):
    m = ADAM_B1 * m + (1.0 - ADAM_B1) * g
    v = ADAM_B2 * v + (1.0 - ADAM_B2) * (g * g)
    m_hat = m / (1.0 - ADAM_B1 ** ADAM_STEP)
    v_hat = v / (1.0 - ADAM_B2 ** ADAM_STEP)
    delta = -ADAM_LR * (m_hat / (jnp.sqrt(v_hat) + ADAM_EPS) + ADAM_WD * w)
    return delta, m, v


def _sum_parts(parts_ref):
    g = parts_ref[0].astype(F32)
    for i in range(1, N_DEV):
        g = g + parts_ref[i].astype(F32)
    return g


def _adamw_shard(parts, w, m, v, tr, name):
    R, C = w.shape

    def body(p_ref, w_ref, m_ref, v_ref, g_ref, d_ref, nm_ref, nv_ref):
        g = _sum_parts(p_ref)
        g_ref[...] = g
        d_ref[...], nm_ref[...], nv_ref[...] = _adamw_math(w_ref[...], g, m_ref[...], v_ref[...])

    blk = pl.BlockSpec((tr, C), lambda i: (i, 0))
    return _pcall(
        body, name=name, grid=(R // tr,),
        in_specs=[pl.BlockSpec((N_DEV, tr, C), lambda i: (0, i, 0)), blk, blk, blk],
        out_specs=[blk] * 4,
        out_shape=[jax.ShapeDtypeStruct((R, C), F32)] * 4,
        compiler_params=_params(("parallel",)),
    )(parts, w, m, v)


def _sum_small(parts):
    def body(p_ref, g_ref):
        g_ref[...] = _sum_parts(p_ref)

    return _pcall(body, name="sum_small", out_shape=jax.ShapeDtypeStruct(parts.shape[1:], F32),
                  in_specs=[_vmem_full()], out_specs=_vmem_full())(parts)


def _adamw_small(g, w, m, v, name):
    def body(g_ref, w_ref, m_ref, v_ref, d_ref, nm_ref, nv_ref):
        d_ref[...], nm_ref[...], nv_ref[...] = _adamw_math(w_ref[...], g_ref[...], m_ref[...], v_ref[...])

    return _pcall(body, name=name, out_shape=[jax.ShapeDtypeStruct(w.shape, F32)] * 3,
                  in_specs=[_vmem_full()] * 4, out_specs=[_vmem_full()] * 3)(g, w, m, v)


def _tables(T):
    h, c = N_RET_HEADS, BLOCK
    pos = jnp.arange(T, dtype=F32)
    angle = 1.0 / jnp.power(10000.0, jnp.linspace(0.0, 1.0, RET_HEAD_DIM // 2, dtype=F32))
    angle = jnp.repeat(angle, 2)
    sin = jnp.sin(pos[:, None] * angle[None])
    cos = jnp.cos(pos[:, None] * angle[None])
    even = (jnp.arange(RET_HEAD_DIM) % 2 == 0)[None, :]
    sin_s = jnp.where(even, -sin, sin)
    log_gamma = jnp.log(1.0 - jnp.power(2.0, -5.0 - jnp.arange(h, dtype=F32)))
    idx = jnp.arange(c, dtype=F32)
    rel = idx[:, None] - idx[None, :]
    d_intra = jnp.where(rel[None] >= 0, jnp.exp(log_gamma[:, None, None] * jnp.maximum(rel, 0.0)[None]), 0.0)
    xi = jnp.exp(log_gamma[None, :] * (idx[:, None] + 1.0))
    zeta = jnp.exp(log_gamma[None, :] * (c - 1.0 - idx[:, None]))
    decay = jnp.exp(log_gamma * c)
    xi_b = jnp.broadcast_to(xi.T[:, :, None], (h, c, RET_HEAD_DIM))
    zeta_b = jnp.broadcast_to(zeta.T[:, :, None], (h, c, RET_HEAD_DIM))
    return cos, sin_s, d_intra, jnp.swapaxes(d_intra, 1, 2), xi_b, zeta_b, decay


def _to_shards(full, cols):
    r = full.shape[0]
    return jnp.swapaxes(full.reshape(r, N_DEV, cols), 0, 1)


def _from_shards(sh):
    n, r, cols = sh.shape
    return jnp.swapaxes(sh, 0, 1).reshape(r, n * cols)


SMALL_ROWS = 216


def _pack_small(gains, conv_b, conv_w, sinks):
    parts = [g.reshape(8, 128) for g in gains] + [conv_b.reshape(44, 128), conv_w.reshape(132, 128),
                                                  jnp.pad(sinks.reshape(1, 8), ((0, 0), (0, 120)))]
    packed = jnp.concatenate(parts, axis=0)
    return jnp.pad(packed, ((0, SMALL_ROWS - packed.shape[0]), (0, 0)))


def kernel(x, mix_pre_norm, w_in, attn_sinks, w_out, mix_post_norm, ffn_pre_norm, w_up, conv_w, conv_b, w_down, ffn_post_norm, loss_target, m_mix_pre_norm, m_w_in, m_attn_sinks, m_w_out, m_mix_post_norm, m_ffn_pre_norm, m_w_up, m_conv_w, m_conv_b, m_w_down, m_ffn_post_norm, v_mix_pre_norm, v_w_in, v_attn_sinks, v_w_out, v_mix_post_norm, v_ffn_pre_norm, v_w_up, v_conv_w, v_conv_b, v_w_down, v_ffn_post_norm):
    T = x.shape[1]
    tm = min(512, T)
    tm_big = min(1024, T)
    nbs = min(8, T // BLOCK)
    x2 = x.reshape(T, D_MODEL)
    target = loss_target.reshape(T, D_MODEL)
    me = 4 * lax.axis_index("x") + 2 * lax.axis_index("y") + lax.axis_index("c")

    g_in, g_cw = _exchange_call([w_in[0].astype(BF), conv_w[0]], [], "gather_w_in")
    w_in_f = _from_shards(g_in)
    cos, sin_s, d_intra, d_intra_t, xi_b, zeta_b, decay = _tables(T)
    sinks = attn_sinks.reshape(N_ATTN_HEADS)

    (h1, qa, ka, va, qr, kr, vr, gr), (w_up8,) = _in_proj(
        x2, mix_pre_norm, w_in_f, cos, sin_s, tm_big, [w_up[0].astype(BF)])
    (a, lse, lse_t), (g_down, g_out) = _attn_fwd(sinks, qa, ka, va, nbs,
                                                 [w_down[0].astype(BF), w_out[0].astype(BF)])
    w_out_f = g_out.reshape(D_MODEL, D_MODEL)
    o, states, r = _ret_fwd(decay, qr, kr, vr, gr, d_intra, xi_b, zeta_b, nbs)
    mixed, x1, h2 = _out_proj(a, r, w_out_f, x2, mix_post_norm, ffn_pre_norm, tm_big)
    w_down4 = g_down.reshape(FF_PAIRS, FF_SHARD, D_MODEL)
    up_g, up_v, u_g, u_v, y4, dout, dz, dg4, loss_acc = _ffn_fwd(
        h2, w_up8, g_cw, conv_b.reshape(N_DEV, 1, FF_SHARD), w_down4, x1, ffn_post_norm, target, tm)
    loss = lax.psum(loss_acc[0, 0], ("x", "y", "c"))

    dup_g, dup_v, dcb_g, dcb_v, dcw_g, dcw_v, gwu_g, gwu_v, gw_down4 = _ffn_bwd_a(
        dz, h2, w_down4, u_g, u_v, up_g, up_v, y4, g_cw, tm)
    dcb = jnp.concatenate([dcb_g, dcb_v], axis=0).reshape(1, 2 * D_FF)
    dcw = _from_shards(jnp.concatenate([dcw_g, dcw_v], axis=0))
    gw_down = gw_down4.reshape(D_FF, D_MODEL)
    (dx1, dmixed, da, dr, dg3, dg2), p_up = _ffn_bwd_b(
        dup_g, dup_v, w_up8, x1, dout, ffn_pre_norm, mixed, mix_post_norm, w_out_f, tm, gwu_g, gwu_v)
    gw_out = _wgrad([a, r], [dmixed], tm, "wgrad_out")
    dret, (p_down,) = _ret_bwd(decay, qr, kr, vr, gr, o, states, dr, d_intra, d_intra_t, xi_b, zeta_b, cos, sin_s,
                               [gw_down.reshape(N_DEV, D_FF // N_DEV, D_MODEL)], nbs)
    (dqa, delta_t, dsink), (p_out,) = _attn_bwd_dq(sinks, qa, ka, va, da, lse, nbs,
                                                   [gw_out.reshape(N_DEV, D_MODEL // N_DEV, D_MODEL)])
    dka, dva = _attn_bwd_dkv(qa, ka, va, da, lse_t, delta_t, nbs)
    grad_x, dg1 = _in_proj_bwd(dqa, dka, dva, dret, w_in_f, x2, mix_pre_norm, dx1, tm)
    gw_in = _wgrad([h1], [dqa, dka, dva, dret], tm, "wgrad_in")

    small = _pack_small([dg1, dg2, dg3, dg4], dcb, dcw, dsink[:, 0])
    small_all, p_in = _exchange_call([small], [_to_shards(gw_in, IN_W // N_DEV).astype(BF)], "exchange_last")
    g_small = _sum_small(small_all)

    g_w_in, d_w_in, nm_w_in, nv_w_in = _adamw_shard(p_in, w_in[0], m_w_in[0], v_w_in[0], 256, "adamw_in")
    g_w_up, d_w_up, nm_w_up, nv_w_up = _adamw_shard(p_up, w_up[0], m_w_up[0], v_w_up[0], 256, "adamw_up")
    g_w_out, d_w_out, nm_w_out, nv_w_out = _adamw_shard(p_out, w_out[0], m_w_out[0], v_w_out[0], 128, "adamw_out")
    g_w_down, d_w_down, nm_w_down, nv_w_down = _adamw_shard(p_down, w_down[0], m_w_down[0], v_w_down[0], 176,
                                                            "adamw_down")
    gains = [mix_pre_norm, mix_post_norm, ffn_pre_norm, ffn_post_norm]
    m_gains = [m_mix_pre_norm, m_mix_post_norm, m_ffn_pre_norm, m_ffn_post_norm]
    v_gains = [v_mix_pre_norm, v_mix_post_norm, v_ffn_pre_norm, v_ffn_post_norm]
    zeros_cw = jnp.zeros((3, 2 * D_FF), F32)
    w_small = _pack_small(gains, conv_b, zeros_cw, attn_sinks)
    m_small = _pack_small(m_gains, m_conv_b, zeros_cw, m_attn_sinks)
    v_small = _pack_small(v_gains, v_conv_b, zeros_cw, v_attn_sinks)
    d_small, nm_small, nv_small = _adamw_small(g_small, w_small, m_small, v_small, "adamw_small")
    shard_cols = 2 * D_FF // N_DEV
    g_cw = lax.dynamic_slice(g_small[76:208].reshape(3, 2 * D_FF), (0, me * shard_cols), (3, shard_cols))
    d_cw, nm_cw, nv_cw = _adamw_small(g_cw, conv_w[0], m_conv_w[0], v_conv_w[0], "adamw_conv_w")

    def unpack(p):
        gains_o = [p[8 * i:8 * i + 8].reshape(1, D_MODEL) for i in range(4)]
        return gains_o, p[32:76].reshape(1, 2 * D_FF), p[208:209, :N_ATTN_HEADS]

    def leaves(p, w_in_s, w_out_s, w_up_s, cw_s, w_down_s):
        (pre1, post1, pre2, post2), cb, sk = unpack(p)
        return [pre1, w_in_s[None], sk, w_out_s[None], post1, pre2, w_up_s[None], cw_s[None], cb, w_down_s[None],
                post2]

    return (loss, grad_x.reshape(1, T, D_MODEL),
            *leaves(g_small, g_w_in, g_w_out, g_w_up, g_cw, g_w_down),
            *leaves(d_small, d_w_in, d_w_out, d_w_up, d_cw, d_w_down),
            *leaves(nm_small, nm_w_in, nm_w_out, nm_w_up, nm_cw, nm_w_down),
            *leaves(nv_small, nv_w_in, nv_w_out, nv_w_up, nv_cw, nv_w_down))
```

```python
import functools
import math

import jax
import jax.numpy as jnp
from jax import lax
from jax.experimental import pallas as pl
from jax.experimental.pallas import tpu as pltpu

F32 = jnp.float32
BF = jnp.bfloat16

N_DEV = 8
D_MODEL = 1024
HEAD_DIM = 64
ATTN_W = 512
N_ATTN_HEADS = 8
KV_W = 128
BLOCK = 128
RET_W = 512
N_RET_HEADS = 4
RET_HEAD_DIM = 128
IN_W = 2816
D_FF = 2816
RMS_EPS = 1e-6
GN_EPS = 1e-6
MASK_VALUE = -1e30
ATTN_SCALE = HEAD_DIM ** -0.5
RET_K_SCALE = RET_HEAD_DIM ** -0.5
GELU_C = math.sqrt(2.0 / math.pi)
GELU_A = 0.044715

ADAM_LR = 0.001
ADAM_B1 = 0.9
ADAM_B2 = 0.999
ADAM_EPS = 1e-08
ADAM_WD = 0.01
ADAM_STEP = 10

VMEM_LIMIT_BYTES = 56 * 1024 * 1024
FF_SHARD = 2 * D_FF // N_DEV
FF_PAIRS = N_DEV // 2

QA0, KA0, VA0, QR0, KR0, VR0, GR0 = 0, 512, 640, 768, 1280, 1792, 2304

MESH_ID = pl.DeviceIdType.MESH


def _pcall(body, **kw):
    return pl.pallas_call(body, **kw)


def _params(sem=None):
    return pltpu.CompilerParams(dimension_semantics=sem, vmem_limit_bytes=VMEM_LIMIT_BYTES)


def _dot(a, b):
    return jnp.dot(a, b, preferred_element_type=F32)


def _dot_nt(a, b):
    return lax.dot_general(a, b, (((1,), (1,)), ((), ())), preferred_element_type=F32)


def _dot_tn(a, b):
    return lax.dot_general(a, b, (((0,), (0,)), ((), ())), preferred_element_type=F32)


def _vmem_full():
    return pl.BlockSpec(memory_space=pltpu.VMEM)


def _smem_full():
    return pl.BlockSpec(memory_space=pltpu.SMEM)


def _rows(tm, w):
    return pl.BlockSpec((tm, w), lambda i: (i, 0))


def _const(shape):
    return pl.BlockSpec(shape, lambda i: tuple(0 for _ in shape))


def _rms_stats(x):
    r = lax.rsqrt(jnp.mean(x * x, axis=-1, keepdims=True) + RMS_EPS)
    return r, x * r


def _rms_bwd(n, r, dn):
    return r * (dn - n * jnp.mean(dn * n, axis=-1, keepdims=True))


def _rot(x, even):
    w = x.shape[1]
    return jnp.where(even, pltpu.roll(x, w - 1, 1), pltpu.roll(x, 1, 1))


def _peers():
    x, y, c = lax.axis_index("x"), lax.axis_index("y"), lax.axis_index("c")
    flips = [(0, 0, 1), (1, 0, 0), (0, 1, 0), (1, 1, 0), (1, 0, 1), (0, 1, 1), (1, 1, 1)]
    peers = [(x ^ fx, y ^ fy, c ^ fc) for fx, fy, fc in flips]
    return 4 * x + 2 * y + c, peers


SAME_CORE_PEERS = 4


class _Exchange:
    def __init__(self, gathers, swaps, send_sems, recv_sems, local_sems):
        self.me, self.peers = _peers()
        self.slots = [4 * px + 2 * py + pc for px, py, pc in self.peers]
        self.pairs = [(src, dst, True) for src, dst in gathers] + [(src, dst, False) for src, dst in swaps]
        self.send_sems, self.recv_sems, self.local_sems = send_sems, recv_sems, local_sems

    @staticmethod
    def scratch(n):
        return [pltpu.SemaphoreType.DMA((n, N_DEV - 1)), pltpu.SemaphoreType.DMA((n, N_DEV - 1)),
                pltpu.SemaphoreType.DMA((n,))]

    def _parts(self, a, slot):
        src, _, whole = self.pairs[a]
        half = N_DEV // 2
        if whole:
            return [(None, src)]
        if isinstance(src, tuple):
            return [(slot < half, src[0].at[jnp.minimum(slot, half - 1)]),
                    (slot >= half, src[1].at[jnp.maximum(slot - half, 0)])]
        return [(None, src.at[slot])]

    def _local(self, a, src):
        return pltpu.make_async_copy(src, self.pairs[a][1].at[self.me], self.local_sems.at[a])

    def _remote(self, a, k, src, slot):
        return pltpu.make_async_remote_copy(
            src_ref=src, dst_ref=self.pairs[a][1].at[slot], send_sem=self.send_sems.at[a, k],
            recv_sem=self.recv_sems.at[a, k], device_id=self.peers[k], device_id_type=MESH_ID)

    def start(self):
        def go(cond, copy):
            if cond is None:
                copy.start()
            else:
                pl.when(cond)(copy.start)

        for a in range(len(self.pairs)):
            for cond, src in self._parts(a, self.me):
                go(cond, self._local(a, src))
            for k in range(SAME_CORE_PEERS if self.pairs[a][2] else N_DEV - 1):
                for cond, src in self._parts(a, self.slots[k]):
                    go(cond, self._remote(a, k, src, self.me))

    def _pass_on(self, a, j):
        k = j + SAME_CORE_PEERS - 1
        block = self.pairs[a][1].at[self.slots[j]]
        return pltpu.make_async_remote_copy(
            src_ref=block, dst_ref=block, send_sem=self.send_sems.at[a, k], recv_sem=self.recv_sems.at[a, k],
            device_id=self.peers[0], device_id_type=MESH_ID)

    def wait(self):
        for a in range(len(self.pairs)):
            src = self._parts(a, self.me)[0][1]
            if self.pairs[a][2]:
                for j in range(1, SAME_CORE_PEERS):
                    self._remote(a, j, src, self.slots[j]).wait_recv()
                    self._pass_on(a, j).start()
                self._remote(a, 0, src, self.slots[0]).wait_recv()
            for k in range(SAME_CORE_PEERS if self.pairs[a][2] else 0, N_DEV - 1):
                self._remote(a, k, src, self.slots[k]).wait_recv()
        for a in range(len(self.pairs)):
            src = self._parts(a, self.me)[0][1]
            for k in range(SAME_CORE_PEERS if self.pairs[a][2] else N_DEV - 1):
                self._remote(a, k, src, self.me).wait_send()
            if self.pairs[a][2]:
                for j in range(1, SAME_CORE_PEERS):
                    self._pass_on(a, j).wait_send()
            self._local(a, src).wait()


ANY_SPEC = pl.BlockSpec(memory_space=pl.ANY)


def _exchange_shapes(gathers, swaps):
    return ([jax.ShapeDtypeStruct((N_DEV,) + a.shape, a.dtype) for a in gathers]
            + [jax.ShapeDtypeStruct(a.shape, a.dtype) for a in swaps])


def _exchange_of(ins, outs, sems, ng):
    return _Exchange(list(zip(ins[:ng], outs[:ng])), list(zip(ins[ng:], outs[ng:])), *sems)


def _exchange_call(gathers, swaps, name):
    ng, ns = len(gathers), len(swaps)
    n = ng + ns

    def body(*refs):
        ex = _exchange_of(refs[:n], refs[n:2 * n], refs[2 * n:], ng)
        ex.start()
        ex.wait()

    return _pcall(
        body, name=name, out_shape=_exchange_shapes(gathers, swaps),
        in_specs=[ANY_SPEC] * (ng + ns), out_specs=[ANY_SPEC] * (ng + ns),
        scratch_shapes=_Exchange.scratch(ng + ns),
    )(*gathers, *swaps)


def _in_proj(x, g1, w_in, cos, sin_s, tm, gathers):
    T = x.shape[0]
    ng = len(gathers)
    nt = T // tm

    def body(x_ref, g_ref, w_ref, cos_ref, sin_ref, *rest):
        ex = _exchange_of(rest[:ng], rest[ng + 8:2 * ng + 8], rest[2 * ng + 8:], ng)
        h_ref, qa_ref, ka_ref, va_ref, qr_ref, kr_ref, vr_ref, gr_ref = rest[ng:ng + 8]
        pl.when(pl.program_id(0) == 0)(ex.start)
        r, n = _rms_stats(x_ref[...])
        h = (n * g_ref[...]).astype(BF)
        h_ref[...] = h

        def proj(c0, w):
            return _dot(h, w_ref[:, c0:c0 + w])

        qa_ref[...] = proj(QA0, ATTN_W).astype(BF)
        kva = proj(KA0, 2 * KV_W)
        ka_ref[...] = kva[:, :KV_W].astype(BF)
        va_ref[...] = kva[:, KV_W:].astype(BF)
        vr_ref[...] = proj(VR0, RET_W).astype(BF)
        gr_ref[...] = proj(GR0, RET_W)
        cos_t, sin_t = cos_ref[...], sin_ref[...]
        even = lax.broadcasted_iota(jnp.int32, (tm, RET_HEAD_DIM), 1) % 2 == 0
        for c0, scale, out_ref in ((QR0, None, qr_ref), (KR0, RET_K_SCALE, kr_ref)):
            full = proj(c0, RET_W)
            for hd in range(N_RET_HEADS):
                cs = slice(hd * RET_HEAD_DIM, (hd + 1) * RET_HEAD_DIM)
                t = full[:, cs] if scale is None else full[:, cs] * scale
                out_ref[:, cs] = (t * cos_t + _rot(t, even) * sin_t).astype(BF)
        pl.when(pl.program_id(0) == nt - 1)(ex.wait)

    widths = [D_MODEL, ATTN_W, KV_W, KV_W, RET_W, RET_W, RET_W, RET_W]
    dts = [BF] * 7 + [F32]
    outs = _pcall(
        body, name="in_proj", grid=(nt,),
        in_specs=[_rows(tm, D_MODEL), _const((1, D_MODEL)), _vmem_full(), _rows(tm, RET_HEAD_DIM),
                  _rows(tm, RET_HEAD_DIM)] + [ANY_SPEC] * ng,
        out_specs=[_rows(tm, w) for w in widths] + [ANY_SPEC] * ng,
        out_shape=[jax.ShapeDtypeStruct((T, w), dt) for w, dt in zip(widths, dts)] + _exchange_shapes(gathers, []),
        scratch_shapes=_Exchange.scratch(ng),
        compiler_params=_params(("arbitrary",)),
    )(x, g1, w_in, cos, sin_s, *gathers)
    return outs[:8], outs[8:]


def _kv_variants(kk):
    kf = kk.astype(F32)
    lo = lax.broadcasted_iota(jnp.int32, kf.shape, 1) < HEAD_DIM
    h0_lo = jnp.where(lo, kf, 0.0)
    h1_hi = jnp.where(lo, 0.0, kf)
    h0_hi = pltpu.roll(h0_lo, HEAD_DIM, 1)
    h1_lo = pltpu.roll(h1_hi, HEAD_DIM, 1)
    return [[h0_lo.astype(BF), h0_hi.astype(BF)], [h1_lo.astype(BF), h1_hi.astype(BF)]]


def _col_to_tile(tile, col, head):
    lane = lax.broadcasted_iota(jnp.int32, tile.shape, 1)
    return jnp.where(lane == head, col, tile)


def _tri(rows, key_major=False):
    i = lax.broadcasted_iota(jnp.int32, (rows, BLOCK), 0) & (BLOCK - 1)
    j = lax.broadcasted_iota(jnp.int32, (rows, BLOCK), 1)
    return i > j if key_major else j > i


def _fold(x2, tri, first_above):
    a, b = x2[:, :BLOCK], x2[:, BLOCK:]
    return jnp.where(tri, a, b) if first_above else jnp.where(tri, b, a)


def _unfold(x, tri, first_above):
    up, low = jnp.where(tri, x, 0.0), jnp.where(tri, 0.0, x)
    return jnp.concatenate([up, low] if first_above else [low, up], axis=1).astype(BF)


def _scaled(q):
    return (q.astype(F32) * ATTN_SCALE).astype(BF)


def _cat_variants(prev, cur):
    return [[jnp.concatenate([prev[h][e], cur[h][e]], axis=0) for e in range(2)] for h in range(2)]


def _block_variants(prev_ref, cur_ref, nbs):
    var = [_kv_variants(prev_ref[...])] + [_kv_variants(cur_ref[b * BLOCK:(b + 1) * BLOCK, :]) for b in range(nbs)]
    return [_cat_variants(var[b], var[b + 1]) for b in range(nbs)]


def _head_cols(col, nbs):
    tiles = []
    for b in range(nbs):
        t = jnp.zeros((BLOCK, BLOCK), F32)
        for head in range(N_ATTN_HEADS):
            r0 = (b * N_ATTN_HEADS + head) * BLOCK
            t = _col_to_tile(t, col[r0:r0 + BLOCK, :], head)
        tiles.append(t)
    return tiles


def _attn_fwd(sinks, qa, ka, va, nbs, gathers):
    T = qa.shape[0]
    steps = T // (BLOCK * nbs)
    R = nbs * N_ATTN_HEADS * BLOCK
    ng = len(gathers)

    def body(sink_ref, q_ref, kc_ref, kp_ref, vc_ref, vp_ref, *rest):
        ex = _exchange_of(rest[:ng], rest[ng + 3:2 * ng + 3], rest[2 * ng + 3:], ng)
        a_ref, lse_ref, lset_ref = rest[ng:ng + 3]
        n = pl.program_id(0)
        pl.when(n == 0)(ex.start)
        kcat = _block_variants(kp_ref, kc_ref, nbs)
        vcat = _block_variants(vp_ref, vc_ref, nbs)
        tri1 = _tri(BLOCK)
        tiles = []
        for b in range(nbs):
            for pair in range(N_ATTN_HEADS // 2):
                qp = _scaled(q_ref[b * BLOCK:(b + 1) * BLOCK, pair * 128:(pair + 1) * 128])
                for e in range(2):
                    s = _fold(_dot_nt(qp, kcat[b][pair // 2][e]), tri1, True)
                    if b == 0:
                        s = jnp.where(tri1 & (n == 0), MASK_VALUE, s)
                    tiles.append(s)
        s = jnp.concatenate(tiles, axis=0)
        sink = jnp.concatenate([jnp.full((BLOCK, 1), sink_ref[head], F32)
                                for _ in range(nbs) for head in range(N_ATTN_HEADS)], axis=0)
        m = jnp.maximum(jnp.max(s, axis=-1, keepdims=True), sink)
        p = jnp.exp(s - m)
        z = jnp.sum(p, axis=-1, keepdims=True) + jnp.exp(sink - m)
        p2 = _unfold(p * (1.0 / z), _tri(R), True)
        for b in range(nbs):
            for pair in range(N_ATTN_HEADS // 2):
                r0 = (b * N_ATTN_HEADS + 2 * pair) * BLOCK
                acc = (_dot(p2[r0:r0 + BLOCK, :], vcat[b][pair // 2][0])
                       + _dot(p2[r0 + BLOCK:r0 + 2 * BLOCK, :], vcat[b][pair // 2][1]))
                a_ref[b * BLOCK:(b + 1) * BLOCK, pair * 128:(pair + 1) * 128] = acc.astype(BF)
        for b, t in enumerate(_head_cols(m + jnp.log(z), nbs)):
            lse_ref[b * BLOCK:(b + 1) * BLOCK, :] = t
            lset_ref[:, b * BLOCK:(b + 1) * BLOCK] = t.T[:N_ATTN_HEADS, :]
        pl.when(n == steps - 1)(ex.wait)

    cur = lambda w: pl.BlockSpec((BLOCK * nbs, w), lambda n: (n, 0))
    prev = lambda w: pl.BlockSpec((BLOCK, w), lambda n: (jnp.maximum(n * nbs - 1, 0), 0))
    outs = _pcall(
        body, name="attn_fwd", grid=(steps,),
        in_specs=[_smem_full(), cur(ATTN_W), cur(KV_W), prev(KV_W), cur(KV_W), prev(KV_W)] + [ANY_SPEC] * ng,
        out_specs=[cur(ATTN_W), cur(BLOCK), pl.BlockSpec((N_ATTN_HEADS, BLOCK * nbs), lambda n: (0, n))]
        + [ANY_SPEC] * ng,
        out_shape=[jax.ShapeDtypeStruct((T, ATTN_W), BF), jax.ShapeDtypeStruct((T, BLOCK), F32),
                   jax.ShapeDtypeStruct((N_ATTN_HEADS, T), F32)] + _exchange_shapes(gathers, []),
        scratch_shapes=_Exchange.scratch(ng),
        compiler_params=_params(("arbitrary",)),
    )(sinks, qa, ka, ka, va, va, *gathers)
    return outs[:3], outs[3:]


def _ret_fwd(decay, qr, kr, vr, gr, d_intra, xi_b, zeta_b, ncs):
    T = qr.shape[0]
    nc = T // BLOCK
    H, C = N_RET_HEADS, RET_HEAD_DIM

    def body(decay_ref, q_ref, k_ref, v_ref, g_ref, d_ref, xi_ref, zeta_ref, o_ref, s_ref, r_ref, state):
        @pl.when(pl.program_id(0) == 0)
        def _():
            state[...] = jnp.zeros_like(state)

        pairs = [(b, h) for b in range(ncs) for h in range(H)]
        sl = lambda b, h: (slice(b * BLOCK, (b + 1) * BLOCK), slice(h * C, (h + 1) * C))
        tab = lambda ref: jnp.concatenate([ref[h] for _, h in pairs], axis=0)
        q = [q_ref[sl(b, h)] for b, h in pairs]
        k = [k_ref[sl(b, h)] for b, h in pairs]
        v = [v_ref[sl(b, h)] for b, h in pairs]
        inner = (jnp.concatenate([_dot_nt(q[i], k[i]) for i in range(len(pairs))], axis=0) * tab(d_ref)).astype(BF)
        kz = (jnp.concatenate(k, axis=0).astype(F32) * tab(zeta_ref)).astype(BF)
        o1 = [_dot(inner[i * BLOCK:(i + 1) * BLOCK, :], v[i]) for i in range(len(pairs))]
        kv = [_dot_tn(kz[i * BLOCK:(i + 1) * BLOCK, :], v[i]) for i in range(len(pairs))]
        st_b = [None] * len(pairs)
        for h in range(H):
            st = state[h]
            for b in range(ncs):
                i = b * H + h
                st_b[i] = st.astype(BF)
                s_ref[b, h] = st_b[i]
                st = decay_ref[h] * st + kv[i]
            state[h] = st
        o2 = jnp.concatenate([_dot(q[i], st_b[i]) for i in range(len(pairs))], axis=0)
        o = jnp.concatenate(o1, axis=0) + o2 * tab(xi_ref)
        mu = jnp.mean(o, axis=-1, keepdims=True)
        oc = o - mu
        rs = lax.rsqrt(jnp.mean(oc * oc, axis=-1, keepdims=True) + GN_EPS)
        g = jnp.concatenate([g_ref[sl(b, h)] for b, h in pairs], axis=0)
        r = (g * jax.nn.sigmoid(g) * (oc * rs)).astype(BF)
        for i, (b, h) in enumerate(pairs):
            o_ref[sl(b, h)] = o[i * BLOCK:(i + 1) * BLOCK, :]
            r_ref[sl(b, h)] = r[i * BLOCK:(i + 1) * BLOCK, :]

    cur = pl.BlockSpec((BLOCK * ncs, RET_W), lambda n: (n, 0))
    tab = pl.BlockSpec((H, C, C), lambda n: (0, 0, 0))
    return _pcall(
        body, name="ret_fwd", grid=(nc // ncs,),
        in_specs=[_smem_full(), cur, cur, cur, cur, tab, tab, tab],
        out_specs=[cur, pl.BlockSpec((ncs, H, C, C), lambda n: (n, 0, 0, 0)), cur],
        out_shape=[jax.ShapeDtypeStruct((T, RET_W), F32), jax.ShapeDtypeStruct((nc, H, C, C), BF),
                   jax.ShapeDtypeStruct((T, RET_W), BF)],
        scratch_shapes=[pltpu.VMEM((H, C, C), F32)],
        compiler_params=_params(("arbitrary",)),
    )(decay, qr, kr, vr, gr, d_intra, xi_b, zeta_b)


def _out_proj(a, r, w_out, x, g2, g3, tm):
    T = x.shape[0]

    def body(a_ref, r_ref, w_ref, x_ref, g2_ref, g3_ref, mixed_ref, x1_ref, h2_ref):
        mixed = _dot(a_ref[...], w_ref[:ATTN_W, :]) + _dot(r_ref[...], w_ref[ATTN_W:, :])
        mixed_ref[...] = mixed
        _, n2 = _rms_stats(mixed)
        x1 = x_ref[...] + n2 * g2_ref[...]
        x1_ref[...] = x1
        _, n3 = _rms_stats(x1)
        h2_ref[...] = (n3 * g3_ref[...]).astype(BF)

    return _pcall(
        body, name="out_proj", grid=(T // tm,),
        in_specs=[_rows(tm, ATTN_W), _rows(tm, RET_W), _vmem_full(), _rows(tm, D_MODEL), _const((1, D_MODEL)),
                  _const((1, D_MODEL))],
        out_specs=[_rows(tm, D_MODEL)] * 3,
        out_shape=[jax.ShapeDtypeStruct((T, D_MODEL), F32), jax.ShapeDtypeStruct((T, D_MODEL), F32),
                   jax.ShapeDtypeStruct((T, D_MODEL), BF)],
        compiler_params=_params(("parallel",)),
    )(a, r, w_out, x, g2, g3)


def _shift_down(cur, k, before):
    out = pltpu.roll(cur, k, 0)
    row = lax.broadcasted_iota(jnp.int32, before.shape, 0)
    top = jnp.where(row < k, pltpu.roll(before, k, 0), out[0:8])
    return jnp.concatenate([top, out[8:]], axis=0)


def _shift_up(cur, k, after):
    tm = cur.shape[0]
    out = pltpu.roll(cur, tm - k, 0)
    row = lax.broadcasted_iota(jnp.int32, after.shape, 0)
    bot = jnp.where(row >= 8 - k, pltpu.roll(after, 8 - k, 0), out[tm - 8:])
    return jnp.concatenate([out[:tm - 8], bot], axis=0)


def _gelu_parts(x):
    x2 = x * x
    th = jnp.tanh(GELU_C * (x + GELU_A * x * x2))
    gelu = 0.5 * x * (1.0 + th)
    dgelu = 0.5 * (1.0 + th) + 0.5 * x * (1.0 - th * th) * (GELU_C * (1.0 + 3.0 * GELU_A * x2))
    return gelu, dgelu


def _ffn_fwd(h2, w_up8, conv_w8, conv_b8, w_down4, x1, g4, target, tm):
    T = h2.shape[0]
    nt = T // tm

    def body(h_ref, wu_ref, cwg_ref, cwv_ref, cbg_ref, cbv_ref, wd_ref, x1_ref, g_ref, t_ref,
             upg_ref, upv_ref, ug_ref, uv_ref, y_ref, dout_ref, dz_ref, dg4_ref, loss_ref, halo, z_acc):
        s = pl.program_id(1)
        first = pl.program_id(0) == 0

        @pl.when(first & (s == 0))
        def _():
            loss_ref[...] = jnp.zeros_like(loss_ref)
            dg4_ref[...] = jnp.zeros_like(dg4_ref)

        h = h_ref[...]
        u = []
        parts = ((cwg_ref, cbg_ref, upg_ref, ug_ref), (cwv_ref, cbv_ref, upv_ref, uv_ref))
        for part, (cw_ref, cb_ref, up_ref, u_ref) in enumerate(parts):
            cur = _dot(h, wu_ref[s + part * FF_PAIRS])
            up_ref[0] = cur.astype(BF)
            before = jnp.where(first, 0.0, halo[part, s])
            halo[part, s] = cur[tm - 8:tm, :]
            u_c = (cw_ref[0, pl.ds(0, 1), :] * _shift_down(cur, 2, before)
                   + cw_ref[0, pl.ds(1, 1), :] * _shift_down(cur, 1, before)
                   + cw_ref[0, pl.ds(2, 1), :] * cur + cb_ref[0])
            u_ref[0] = u_c
            u.append(u_c)
        gelu, _ = _gelu_parts(u[0])
        y = (gelu * u[1]).astype(BF)
        y_ref[0] = y
        z_part = _dot(y, wd_ref[s])

        @pl.when(s == 0)
        def _():
            z_acc[...] = z_part

        @pl.when(s > 0)
        def _():
            z_acc[...] += z_part

        @pl.when(s == FF_PAIRS - 1)
        def _():
            r4, n4 = _rms_stats(z_acc[...])
            err = x1_ref[...] + n4 * g_ref[...] - t_ref[...]
            dout = err * (1.0 / D_MODEL)
            dout_ref[...] = dout
            loss_ref[...] += 0.5 * jnp.sum(jnp.mean(err * err, axis=-1, keepdims=True), axis=0, keepdims=True)
            dg4_ref[...] += jnp.sum(dout * n4, axis=0, keepdims=True)
            dz_ref[...] = _rms_bwd(n4, r4, dout * g_ref[...]).astype(BF)

    rows = pl.BlockSpec((tm, D_MODEL), lambda i, s: (i, 0))
    one = lambda shape: pl.BlockSpec(shape, lambda i, s: tuple(0 for _ in shape))
    gate = lambda r, w: pl.BlockSpec((1, r, w), lambda i, s: (s, 0, 0))
    val = lambda r, w: pl.BlockSpec((1, r, w), lambda i, s: (s + FF_PAIRS, 0, 0))
    tile = pl.BlockSpec((1, tm, FF_SHARD), lambda i, s: (s, i, 0))
    half = lambda dt: jax.ShapeDtypeStruct((FF_PAIRS, T, FF_SHARD), dt)
    return _pcall(
        body, name="ffn_fwd", grid=(nt, FF_PAIRS),
        in_specs=[rows, _vmem_full(), gate(3, FF_SHARD), val(3, FF_SHARD), gate(1, FF_SHARD), val(1, FF_SHARD),
                  _vmem_full(), rows, one((1, D_MODEL)), rows],
        out_specs=[tile] * 5 + [rows, rows, one((1, D_MODEL)), one((8, 128))],
        out_shape=[half(BF), half(BF), half(F32), half(F32), half(BF), jax.ShapeDtypeStruct((T, D_MODEL), F32),
                   jax.ShapeDtypeStruct((T, D_MODEL), BF), jax.ShapeDtypeStruct((1, D_MODEL), F32),
                   jax.ShapeDtypeStruct((8, 128), F32)],
        scratch_shapes=[pltpu.VMEM((2, FF_PAIRS, 8, FF_SHARD), F32), pltpu.VMEM((tm, D_MODEL), F32)],
        compiler_params=_params(("arbitrary", "arbitrary")),
    )(h2, w_up8, conv_w8, conv_w8, conv_b8, conv_b8, w_down4, x1, g4, target)


def _ffn_bwd_a(dz, h2, w_down4, u_g, u_v, up_g, up_v, y4, conv_w8, tm):
    T = dz.shape[0]
    nt = T // tm

    def body(dz_ref, h_ref, wd_ref, ug_ref, uv_ref, upg_ref, upv_ref, y_ref, cwg_ref, cwv_ref,
             dupg_ref, dupv_ref, dcbg_ref, dcbv_ref, dcwg_ref, dcwv_ref, gwug_ref, gwuv_ref, gwd_ref, carry):
        @pl.when(pl.program_id(1) == 0)
        def _():
            for ref in (dcbg_ref, dcbv_ref, dcwg_ref, dcwv_ref, gwug_ref, gwuv_ref, gwd_ref, carry):
                ref[...] = jnp.zeros_like(ref)

        dz = dz_ref[...]
        h = h_ref[...]
        dy = _dot_nt(dz, wd_ref[0])
        gwd_ref[0] += _dot_tn(y_ref[0], dz)
        gelu, dgelu = _gelu_parts(ug_ref[0])
        parts = ((0, dy * uv_ref[0] * dgelu, upg_ref, cwg_ref, dupg_ref, dcbg_ref, dcwg_ref, gwug_ref),
                 (1, dy * gelu, upv_ref, cwv_ref, dupv_ref, dcbv_ref, dcwv_ref, gwuv_ref))
        for part, d, up_ref, cw_ref, dup_ref, dcb_ref, dcw_ref, gwu_ref in parts:
            after = carry[part]
            d1 = _shift_up(d, 1, after)
            d2 = _shift_up(d, 2, after)
            carry[part] = d[0:8, :]
            upc = up_ref[0].astype(F32)
            dcb_ref[0] += jnp.sum(d, axis=0, keepdims=True)
            dcw_ref[0, pl.ds(2, 1), :] += jnp.sum(d * upc, axis=0, keepdims=True)
            dcw_ref[0, pl.ds(1, 1), :] += jnp.sum(d1 * upc, axis=0, keepdims=True)
            dcw_ref[0, pl.ds(0, 1), :] += jnp.sum(d2 * upc, axis=0, keepdims=True)
            dup = (cw_ref[0, pl.ds(2, 1), :] * d + cw_ref[0, pl.ds(1, 1), :] * d1
                   + cw_ref[0, pl.ds(0, 1), :] * d2).astype(BF)
            dup_ref[0] = dup
            gwu_ref[0] += _dot_tn(h, dup)

    rev = pl.BlockSpec((tm, D_MODEL), lambda s, i: (nt - 1 - i, 0))
    tile = pl.BlockSpec((1, tm, FF_SHARD), lambda s, i: (s, nt - 1 - i, 0))
    acc = lambda r, w: pl.BlockSpec((1, r, w), lambda s, i: (s, 0, 0))
    acc_val = pl.BlockSpec((1, 3, FF_SHARD), lambda s, i: (s + FF_PAIRS, 0, 0))
    half = lambda r, dt: jax.ShapeDtypeStruct((FF_PAIRS, r, FF_SHARD), dt)
    return _pcall(
        body, name="ffn_bwd_a", grid=(FF_PAIRS, nt),
        in_specs=[rev, rev, acc(FF_SHARD, D_MODEL), tile, tile, tile, tile, tile, acc(3, FF_SHARD), acc_val],
        out_specs=[tile, tile, acc(1, FF_SHARD), acc(1, FF_SHARD), acc(3, FF_SHARD), acc(3, FF_SHARD),
                   acc(D_MODEL, FF_SHARD), acc(D_MODEL, FF_SHARD), acc(FF_SHARD, D_MODEL)],
        out_shape=[half(T, BF), half(T, BF), half(1, F32), half(1, F32), half(3, F32), half(3, F32),
                   half(D_MODEL, F32), half(D_MODEL, F32), jax.ShapeDtypeStruct((FF_PAIRS, FF_SHARD, D_MODEL), F32)],
        scratch_shapes=[pltpu.VMEM((2, 8, FF_SHARD), F32)],
        compiler_params=_params(("arbitrary", "arbitrary")),
    )(dz, h2, w_down4, u_g, u_v, up_g, up_v, y4, conv_w8, conv_w8)


def _ffn_bwd_b(dup_g, dup_v, w_up8, x1, dout, g3, mixed, g2, w_out, tm, gwu_g, gwu_v):
    T = x1.shape[0]
    nt = T // tm

    def body(dupg_ref, dupv_ref, wup_ref, x1_ref, dout_ref, g3_ref, mixed_ref, g2_ref, wout_ref, gwug_ref, gwuv_ref,
             dx1_ref, dmixed_ref, da_ref, dr_ref, dg3_ref, dg2_ref, pup_ref, *sems):
        ex = _Exchange([], [((gwug_ref, gwuv_ref), pup_ref)], *sems)

        @pl.when(pl.program_id(0) == 0)
        def _():
            ex.start()
            dg3_ref[...] = jnp.zeros_like(dg3_ref)
            dg2_ref[...] = jnp.zeros_like(dg2_ref)

        dh2 = jnp.zeros((tm, D_MODEL), F32)
        for s in range(FF_PAIRS):
            dh2 = dh2 + _dot_nt(dupg_ref[s], wup_ref[s]) + _dot_nt(dupv_ref[s], wup_ref[s + FF_PAIRS])
        r3, n3 = _rms_stats(x1_ref[...])
        dg3_ref[...] += jnp.sum(dh2 * n3, axis=0, keepdims=True)
        dx1 = dout_ref[...] + _rms_bwd(n3, r3, dh2 * g3_ref[...])
        dx1_ref[...] = dx1
        r2, n2 = _rms_stats(mixed_ref[...])
        dg2_ref[...] += jnp.sum(dx1 * n2, axis=0, keepdims=True)
        dmixed = _rms_bwd(n2, r2, dx1 * g2_ref[...]).astype(BF)
        dmixed_ref[...] = dmixed
        da_ref[...] = _dot_nt(dmixed, wout_ref[:ATTN_W, :]).astype(BF)
        dr_ref[...] = _dot_nt(dmixed, wout_ref[ATTN_W:, :])
        pl.when(pl.program_id(0) == nt - 1)(ex.wait)

    half = pl.BlockSpec((FF_PAIRS, tm, FF_SHARD), lambda i: (0, i, 0))
    outs = _pcall(
        body, name="ffn_bwd_b", grid=(nt,),
        in_specs=[half, half, _vmem_full(), _rows(tm, D_MODEL), _rows(tm, D_MODEL), _const((1, D_MODEL)),
                  _rows(tm, D_MODEL), _const((1, D_MODEL)), _vmem_full(), ANY_SPEC, ANY_SPEC],
        out_specs=[_rows(tm, D_MODEL), _rows(tm, D_MODEL), _rows(tm, ATTN_W), _rows(tm, RET_W),
                   _const((1, D_MODEL)), _const((1, D_MODEL)), ANY_SPEC],
        out_shape=[jax.ShapeDtypeStruct((T, D_MODEL), F32), jax.ShapeDtypeStruct((T, D_MODEL), BF),
                   jax.ShapeDtypeStruct((T, ATTN_W), BF), jax.ShapeDtypeStruct((T, RET_W), F32),
                   jax.ShapeDtypeStruct((1, D_MODEL), F32), jax.ShapeDtypeStruct((1, D_MODEL), F32),
                   jax.ShapeDtypeStruct((N_DEV, D_MODEL, FF_SHARD), F32)],
        scratch_shapes=_Exchange.scratch(1),
        compiler_params=_params(("arbitrary",)),
    )(dup_g, dup_v, w_up8, x1, dout, g3, mixed, g2, w_out, gwu_g, gwu_v)
    return outs[:6], outs[6]


def _ret_bwd(decay, qr, kr, vr, gr, o, states, dr, d_intra, d_intra_t, xi_b, zeta_b, cos, sin_s, swaps, ncs):
    T = qr.shape[0]
    nc = T // BLOCK
    H, C = N_RET_HEADS, RET_HEAD_DIM
    ns = len(swaps)

    def body(decay_ref, q_ref, k_ref, v_ref, g_ref, o_ref, s_ref, dr_ref, d_ref, dt_ref, xi_ref, zeta_ref,
             cos_ref, sin_ref, *rest):
        ex = _exchange_of(rest[:ns], rest[ns + 1:2 * ns + 1], rest[2 * ns + 2:], 0)
        dret_ref, gstate = rest[ns], rest[2 * ns + 1]

        @pl.when(pl.program_id(0) == 0)
        def _():
            ex.start()
            gstate[...] = jnp.zeros_like(gstate)

        pairs = [(b, h) for b in range(ncs) for h in range(H)]
        n = len(pairs)
        sl = lambda b, h: (slice(b * BLOCK, (b + 1) * BLOCK), slice(h * C, (h + 1) * C))
        cat = lambda ref: jnp.concatenate([ref[sl(b, h)] for b, h in pairs], axis=0)
        tab = lambda ref: jnp.concatenate([ref[h] for _, h in pairs], axis=0)
        part = lambda x, i: x[i * BLOCK:(i + 1) * BLOCK, :]
        q = [q_ref[sl(b, h)] for b, h in pairs]
        k = [k_ref[sl(b, h)] for b, h in pairs]
        v = [v_ref[sl(b, h)] for b, h in pairs]
        g, o_all, dr_all = cat(g_ref), cat(o_ref), cat(dr_ref)
        mu = jnp.mean(o_all, axis=-1, keepdims=True)
        oc = o_all - mu
        rs = lax.rsqrt(jnp.mean(oc * oc, axis=-1, keepdims=True) + GN_EPS)
        on = oc * rs
        sg = jax.nn.sigmoid(g)
        dg = (dr_all * on * (sg * (1.0 + g * (1.0 - sg)))).astype(BF)
        don = dr_all * (g * sg)
        do = rs * (don - jnp.mean(don, axis=-1, keepdims=True) - on * jnp.mean(don * on, axis=-1, keepdims=True))
        do_b = do.astype(BF)
        dox_b = (do * tab(xi_ref)).astype(BF)
        zeta = tab(zeta_ref)
        kz = (jnp.concatenate(k, axis=0).astype(F32) * zeta).astype(BF)
        d_t = tab(dt_ref)
        da_b = (jnp.concatenate([_dot_nt(part(do_b, i), v[i]) for i in range(n)], axis=0) * tab(d_ref)).astype(BF)
        dat_b = (jnp.concatenate([_dot_nt(v[i], part(do_b, i)) for i in range(n)], axis=0) * d_t).astype(BF)
        mt_b = (jnp.concatenate([_dot_nt(k[i], q[i]) for i in range(n)], axis=0) * d_t).astype(BF)
        dq = [_dot(part(da_b, i), k[i]) + _dot_nt(part(dox_b, i), s_ref[pairs[i]]) for i in range(n)]
        dk1 = [_dot(part(dat_b, i), q[i]) for i in range(n)]
        dv1 = [_dot(part(mt_b, i), part(do_b, i)) for i in range(n)]
        qtd = [_dot_tn(q[i], part(dox_b, i)) for i in range(n)]
        gst_b = [None] * n
        for h in range(H):
            gst = gstate[h]
            for b in reversed(range(ncs)):
                i = b * H + h
                gst_b[i] = gst.astype(BF)
                gst = decay_ref[h] * gst + qtd[i]
            gstate[h] = gst
        dk2 = jnp.concatenate([_dot_nt(v[i], gst_b[i]) for i in range(n)], axis=0) * zeta
        dv = jnp.concatenate([dv1[i] + _dot(part(kz, i), gst_b[i]) for i in range(n)], axis=0).astype(BF)
        even = lax.broadcasted_iota(jnp.int32, (n * BLOCK, C), 1) % 2 == 0
        cos_t = jnp.concatenate([cos_ref[b * BLOCK:(b + 1) * BLOCK, :] for b, _ in pairs], axis=0)
        sin_t = jnp.concatenate([sin_ref[b * BLOCK:(b + 1) * BLOCK, :] for b, _ in pairs], axis=0)
        dq = jnp.concatenate(dq, axis=0)
        dk = jnp.concatenate(dk1, axis=0) + dk2
        dq = (dq * cos_t - _rot(dq, even) * sin_t).astype(BF)
        dk = ((dk * cos_t - _rot(dk, even) * sin_t) * RET_K_SCALE).astype(BF)
        for i, (b, h) in enumerate(pairs):
            rows = slice(b * BLOCK, (b + 1) * BLOCK)
            for j, x in enumerate((dq, dk, dv, dg)):
                dret_ref[rows, j * RET_W + h * C:j * RET_W + (h + 1) * C] = part(x, i)
        pl.when(pl.program_id(0) == steps - 1)(ex.wait)

    steps = nc // ncs
    rev = lambda w: pl.BlockSpec((BLOCK * ncs, w), lambda n: (steps - 1 - n, 0))
    tab = pl.BlockSpec((H, C, C), lambda n: (0, 0, 0))
    outs = _pcall(
        body, name="ret_bwd", grid=(steps,),
        in_specs=[_smem_full(), rev(RET_W), rev(RET_W), rev(RET_W), rev(RET_W), rev(RET_W),
                  pl.BlockSpec((ncs, H, C, C), lambda n: (steps - 1 - n, 0, 0, 0)), rev(RET_W), tab, tab, tab, tab,
                  rev(C), rev(C)] + [ANY_SPEC] * ns,
        out_specs=[rev(4 * RET_W)] + [ANY_SPEC] * ns,
        out_shape=[jax.ShapeDtypeStruct((T, 4 * RET_W), BF)] + _exchange_shapes([], swaps),
        scratch_shapes=[pltpu.VMEM((H, C, C), F32)] + _Exchange.scratch(ns),
        compiler_params=_params(("arbitrary",)),
    )(decay, qr, kr, vr, gr, o, states, dr, d_intra, d_intra_t, xi_b, zeta_b, cos, sin_s, *swaps)
    return outs[0], outs[1:]


def _attn_bwd_dq(sinks, qa, ka, va, da, lse, nbs, swaps):
    T = qa.shape[0]
    steps = T // (BLOCK * nbs)
    R = nbs * N_ATTN_HEADS * BLOCK
    ns = len(swaps)

    def body(sink_ref, q_ref, kc_ref, kp_ref, vc_ref, vp_ref, da_ref, lse_ref, *rest):
        ex = _exchange_of(rest[:ns], rest[ns + 3:2 * ns + 3], rest[2 * ns + 3:], 0)
        dq_ref, deltat_ref, dsink_ref = rest[ns:ns + 3]
        n = pl.program_id(0)

        @pl.when(n == 0)
        def _():
            ex.start()
            dsink_ref[...] = jnp.zeros_like(dsink_ref)

        kcat = _block_variants(kp_ref, kc_ref, nbs)
        vcat = _block_variants(vp_ref, vc_ref, nbs)
        tri1 = _tri(BLOCK)
        lane = lax.broadcasted_iota(jnp.int32, (BLOCK, BLOCK), 1)
        s_tiles, dp_tiles, lse_cols = [], [], []
        for b in range(nbs):
            rows = slice(b * BLOCK, (b + 1) * BLOCK)
            lse_tile = lse_ref[rows, :]
            for pair in range(N_ATTN_HEADS // 2):
                qp = _scaled(q_ref[rows, pair * 128:(pair + 1) * 128])
                dop = da_ref[rows, pair * 128:(pair + 1) * 128].astype(BF)
                for e in range(2):
                    s = _fold(_dot_nt(qp, kcat[b][pair // 2][e]), tri1, True)
                    if b == 0:
                        s = jnp.where(tri1 & (n == 0), MASK_VALUE, s)
                    s_tiles.append(s)
                    dp_tiles.append(_fold(_dot_nt(dop, vcat[b][pair // 2][e]), tri1, True))
                    lse_cols.append(jnp.sum(jnp.where(lane == 2 * pair + e, lse_tile, 0.0), axis=-1, keepdims=True))
        lse_c = jnp.concatenate(lse_cols, axis=0)
        p = jnp.exp(jnp.concatenate(s_tiles, axis=0) - lse_c)
        dp = jnp.concatenate(dp_tiles, axis=0)
        delta = jnp.sum(p * dp, axis=-1, keepdims=True)
        ds2 = _unfold(p * (dp - delta), _tri(R), True)
        for b in range(nbs):
            for pair in range(N_ATTN_HEADS // 2):
                r0 = (b * N_ATTN_HEADS + 2 * pair) * BLOCK
                acc = (_dot(ds2[r0:r0 + BLOCK, :], kcat[b][pair // 2][0])
                       + _dot(ds2[r0 + BLOCK:r0 + 2 * BLOCK, :], kcat[b][pair // 2][1]))
                dq_ref[b * BLOCK:(b + 1) * BLOCK, pair * 128:(pair + 1) * 128] = (acc * ATTN_SCALE).astype(BF)
        for b, t in enumerate(_head_cols(delta, nbs)):
            deltat_ref[:, b * BLOCK:(b + 1) * BLOCK] = t.T[:N_ATTN_HEADS, :]
        sink = jnp.concatenate([jnp.full((BLOCK, 1), sink_ref[head], F32)
                                for _ in range(nbs) for head in range(N_ATTN_HEADS)], axis=0)
        ds_sink = -jnp.exp(sink - lse_c) * delta
        row8 = lax.broadcasted_iota(jnp.int32, (N_ATTN_HEADS, BLOCK), 0)
        dsink = jnp.zeros((N_ATTN_HEADS, BLOCK), F32)
        for b in range(nbs):
            for head in range(N_ATTN_HEADS):
                r0 = (b * N_ATTN_HEADS + head) * BLOCK
                dsink = dsink + jnp.where(row8 == head, jnp.sum(ds_sink[r0:r0 + BLOCK, :], axis=0, keepdims=True), 0.0)
        dsink_ref[...] += dsink
        pl.when(n == steps - 1)(ex.wait)

    cur = lambda w: pl.BlockSpec((BLOCK * nbs, w), lambda n: (n, 0))
    prev = lambda w: pl.BlockSpec((BLOCK, w), lambda n: (jnp.maximum(n * nbs - 1, 0), 0))
    outs = _pcall(
        body, name="attn_bwd_dq", grid=(steps,),
        in_specs=[_smem_full(), cur(ATTN_W), cur(KV_W), prev(KV_W), cur(KV_W), prev(KV_W), cur(ATTN_W), cur(BLOCK)]
        + [ANY_SPEC] * ns,
        out_specs=[cur(ATTN_W), pl.BlockSpec((N_ATTN_HEADS, BLOCK * nbs), lambda n: (0, n)),
                   _const((N_ATTN_HEADS, BLOCK))] + [ANY_SPEC] * ns,
        out_shape=[jax.ShapeDtypeStruct((T, ATTN_W), BF), jax.ShapeDtypeStruct((N_ATTN_HEADS, T), F32),
                   jax.ShapeDtypeStruct((N_ATTN_HEADS, BLOCK), F32)] + _exchange_shapes([], swaps),
        scratch_shapes=_Exchange.scratch(ns),
        compiler_params=_params(("arbitrary",)),
    )(sinks, qa, ka, ka, va, va, da, lse, *swaps)
    return outs[:3], outs[3:]


def _attn_bwd_dkv(qa, ka, va, da, lse_t, delta_t, nbs):
    T = qa.shape[0]
    nb = T // BLOCK
    steps = nb // nbs
    R = nbs * N_ATTN_HEADS * BLOCK

    def body(qc_ref, qn_ref, dac_ref, dan_ref, k_ref, v_ref, lc_ref, ln_ref, dc_ref, dn_ref, dk_ref, dv_ref):
        n = pl.program_id(0)
        tri1 = _tri(BLOCK, True)
        lo = lax.broadcasted_iota(jnp.int32, (BLOCK, 128), 1) < HEAD_DIM
        kv = [_kv_variants(k_ref[b * BLOCK:(b + 1) * BLOCK, :]) for b in range(nbs)]
        vv = [_kv_variants(v_ref[b * BLOCK:(b + 1) * BLOCK, :]) for b in range(nbs)]
        qcat, docat = [], []
        s_tiles, dp_tiles, lse_tiles, delta_tiles = [], [], [], []
        for b in range(nbs):
            rows = slice(b * BLOCK, (b + 1) * BLOCK)
            nrows = slice((b + 1) * BLOCK, (b + 2) * BLOCK)
            inside = b < nbs - 1
            for pair in range(N_ATTN_HEADS // 2):
                ps = slice(pair * 128, (pair + 1) * 128)
                q2 = _scaled(jnp.concatenate([qc_ref[rows, ps], qc_ref[nrows, ps] if inside else qn_ref[:, ps]], axis=0))
                do2 = jnp.concatenate([dac_ref[rows, ps], dac_ref[nrows, ps] if inside else dan_ref[:, ps]],
                                      axis=0).astype(BF)
                qcat.append(q2)
                docat.append(do2)
                for e in range(2):
                    one = pl.ds(2 * pair + e, 1)
                    s = _fold(_dot_nt(kv[b][pair // 2][e], q2), tri1, False)
                    if not inside:
                        s = jnp.where(tri1 & (n == steps - 1), MASK_VALUE, s)
                    s_tiles.append(s)
                    dp_tiles.append(_fold(_dot_nt(vv[b][pair // 2][e], do2), tri1, False))
                    lse_tiles.append(jnp.where(tri1, lc_ref[one, nrows] if inside else ln_ref[one, :], lc_ref[one, rows]))
                    delta_tiles.append(jnp.where(tri1, dc_ref[one, nrows] if inside else dn_ref[one, :],
                                                 dc_ref[one, rows]))
        pt = jnp.exp(jnp.concatenate(s_tiles, axis=0) - jnp.concatenate(lse_tiles, axis=0))
        dst = pt * (jnp.concatenate(dp_tiles, axis=0) - jnp.concatenate(delta_tiles, axis=0))
        tri = _tri(R, True)
        pt2 = _unfold(pt, tri, False)
        dst2 = _unfold(dst, tri, False)
        for b in range(nbs):
            dk = jnp.zeros((BLOCK, 128), F32)
            dv = jnp.zeros((BLOCK, 128), F32)
            for pair in range(N_ATTN_HEADS // 2):
                h = pair // 2
                for e in range(2):
                    r0 = (b * N_ATTN_HEADS + 2 * pair + e) * BLOCK
                    half = lo if e == 0 else jnp.logical_not(lo)
                    dv_e = jnp.where(half, _dot(pt2[r0:r0 + BLOCK, :], docat[b * 4 + pair]), 0.0)
                    dk_e = jnp.where(half, _dot(dst2[r0:r0 + BLOCK, :], qcat[b * 4 + pair]), 0.0)
                    if e != h:
                        dv_e = pltpu.roll(dv_e, HEAD_DIM, 1)
                        dk_e = pltpu.roll(dk_e, HEAD_DIM, 1)
                    dv = dv + dv_e
                    dk = dk + dk_e
            dk_ref[b * BLOCK:(b + 1) * BLOCK, :] = dk.astype(BF)
            dv_ref[b * BLOCK:(b + 1) * BLOCK, :] = dv.astype(BF)

    cur = lambda w: pl.BlockSpec((BLOCK * nbs, w), lambda n: (n, 0))
    nxt = lambda w: pl.BlockSpec((BLOCK, w), lambda n: (jnp.minimum((n + 1) * nbs, nb - 1), 0))
    tcur = pl.BlockSpec((N_ATTN_HEADS, BLOCK * nbs), lambda n: (0, n))
    tnxt = pl.BlockSpec((N_ATTN_HEADS, BLOCK), lambda n: (0, jnp.minimum((n + 1) * nbs, nb - 1)))
    return _pcall(
        body, name="attn_bwd_dkv", grid=(steps,),
        in_specs=[cur(ATTN_W), nxt(ATTN_W), cur(ATTN_W), nxt(ATTN_W), cur(KV_W), cur(KV_W), tcur, tnxt, tcur, tnxt],
        out_specs=[cur(KV_W), cur(KV_W)],
        out_shape=[jax.ShapeDtypeStruct((T, KV_W), BF), jax.ShapeDtypeStruct((T, KV_W), BF)],
        compiler_params=_params(("parallel",)),
    )(qa, qa, da, da, ka, va, lse_t, lse_t, delta_t, delta_t)


def _in_proj_bwd(dqa, dka, dva, dret, w_in, x, g1, dx1, tm):
    T = x.shape[0]

    def body(dqa_ref, dka_ref, dva_ref, dret_ref, w_ref, x_ref, g_ref, dx1_ref, dx_ref, dg1_ref):
        @pl.when(pl.program_id(0) == 0)
        def _():
            dg1_ref[...] = jnp.zeros_like(dg1_ref)

        dh = (_dot_nt(dqa_ref[...], w_ref[:, QA0:QA0 + ATTN_W]) + _dot_nt(dka_ref[...], w_ref[:, KA0:KA0 + KV_W])
              + _dot_nt(dva_ref[...], w_ref[:, VA0:VA0 + KV_W]) + _dot_nt(dret_ref[...], w_ref[:, QR0:IN_W]))
        r, n = _rms_stats(x_ref[...])
        dg1_ref[...] += jnp.sum(dh * n, axis=0, keepdims=True)
        dx_ref[...] = dx1_ref[...] + _rms_bwd(n, r, dh * g_ref[...])

    return _pcall(
        body, name="in_proj_bwd", grid=(T // tm,),
        in_specs=[_rows(tm, ATTN_W), _rows(tm, KV_W), _rows(tm, KV_W), _rows(tm, 4 * RET_W), _vmem_full(),
                  _rows(tm, D_MODEL), _const((1, D_MODEL)), _rows(tm, D_MODEL)],
        out_specs=[_rows(tm, D_MODEL), _const((1, D_MODEL))],
        out_shape=[jax.ShapeDtypeStruct((T, D_MODEL), F32), jax.ShapeDtypeStruct((1, D_MODEL), F32)],
        compiler_params=_params(("arbitrary",)),
    )(dqa, dka, dva, dret, w_in, x, g1, dx1)


def _wgrad(a_list, b_list, tk, name):
    T = a_list[0].shape[0]
    na, nbb = len(a_list), len(b_list)
    m_sizes = [a.shape[1] for a in a_list]
    n_sizes = [b.shape[1] for b in b_list]
    M, N = sum(m_sizes), sum(n_sizes)
    nk = T // tk
    chunk = 512

    def body(*refs):
        a_refs, b_refs = refs[:na], refs[na:na + nbb]
        out_ref, acc = refs[na + nbb], refs[na + nbb + 1]
        k = pl.program_id(0)

        @pl.when(k == 0)
        def _():
            acc[...] = jnp.zeros_like(acc)

        r0 = 0
        for ai in range(na):
            a = a_refs[ai][...]
            c0 = 0
            for bi in range(nbb):
                for s in range(0, n_sizes[bi], chunk):
                    w = min(chunk, n_sizes[bi] - s)
                    acc[r0:r0 + m_sizes[ai], c0 + s:c0 + s + w] += _dot_tn(a, b_refs[bi][:, s:s + w])
                c0 += n_sizes[bi]
            r0 += m_sizes[ai]

        @pl.when(k == nk - 1)
        def _():
            pltpu.sync_copy(acc, out_ref)

    return _pcall(
        body, name=name, grid=(nk,),
        in_specs=[_rows(tk, w) for w in m_sizes + n_sizes],
        out_specs=pl.BlockSpec(memory_space=pl.ANY),
        out_shape=jax.ShapeDtypeStruct((M, N), F32),
        scratch_shapes=[pltpu.VMEM((M, N), F32)],
        compiler_params=_params(("arbitrary",)),
    )(*a_list, *b_list)


def _adamw_math(w, g, m, v):
    m = ADAM_B1 * m + (1.0 - ADAM_B1) * g
    v = ADAM_B2 * v + (1.0 - ADAM_B2) * (g * g)
    m_hat = m / (1.0 - ADAM_B1 ** ADAM_STEP)
    v_hat = v / (1.0 - ADAM_B2 ** ADAM_STEP)
    delta = -ADAM_LR * (m_hat / (jnp.sqrt(v_hat) + ADAM_EPS) + ADAM_WD * w)
    return delta, m, v


def _sum_parts(parts_ref):
    g = parts_ref[0].astype(F32)
    for i in range(1, N_DEV):
        g = g + parts_ref[i].astype(F32)
    return g


def _adamw_shard(parts, w, m, v, tr, name):
    R, C = w.shape

    def body(p_ref, w_ref, m_ref, v_ref, g_ref, d_ref, nm_ref, nv_ref):
        g = _sum_parts(p_ref)
        g_ref[...] = g
        d_ref[...], nm_ref[...], nv_ref[...] = _adamw_math(w_ref[...], g, m_ref[...], v_ref[...])

    blk = pl.BlockSpec((tr, C), lambda i: (i, 0))
    return _pcall(
        body, name=name, grid=(R // tr,),
        in_specs=[pl.BlockSpec((N_DEV, tr, C), lambda i: (0, i, 0)), blk, blk, blk],
        out_specs=[blk] * 4,
        out_shape=[jax.ShapeDtypeStruct((R, C), F32)] * 4,
        compiler_params=_params(("parallel",)),
    )(parts, w, m, v)


def _sum_small(parts):
    def body(p_ref, g_ref):
        g_ref[...] = _sum_parts(p_ref)

    return _pcall(body, name="sum_small", out_shape=jax.ShapeDtypeStruct(parts.shape[1:], F32),
                  in_specs=[_vmem_full()], out_specs=_vmem_full())(parts)


def _adamw_small(g, w, m, v, name):
    def body(g_ref, w_ref, m_ref, v_ref, d_ref, nm_ref, nv_ref):
        d_ref[...], nm_ref[...], nv_ref[...] = _adamw_math(w_ref[...], g_ref[...], m_ref[...], v_ref[...])

    return _pcall(body, name=name, out_shape=[jax.ShapeDtypeStruct(w.shape, F32)] * 3,
                  in_specs=[_vmem_full()] * 4, out_specs=[_vmem_full()] * 3)(g, w, m, v)


def _tables(T):
    h, c = N_RET_HEADS, BLOCK
    pos = jnp.arange(T, dtype=F32)
    angle = 1.0 / jnp.power(10000.0, jnp.linspace(0.0, 1.0, RET_HEAD_DIM // 2, dtype=F32))
    angle = jnp.repeat(angle, 2)
    sin = jnp.sin(pos[:, None] * angle[None])
    cos = jnp.cos(pos[:, None] * angle[None])
    even = (jnp.arange(RET_HEAD_DIM) % 2 == 0)[None, :]
    sin_s = jnp.where(even, -sin, sin)
    log_gamma = jnp.log(1.0 - jnp.power(2.0, -5.0 - jnp.arange(h, dtype=F32)))
    idx = jnp.arange(c, dtype=F32)
    rel = idx[:, None] - idx[None, :]
    d_intra = jnp.where(rel[None] >= 0, jnp.exp(log_gamma[:, None, None] * jnp.maximum(rel, 0.0)[None]), 0.0)
    xi = jnp.exp(log_gamma[None, :] * (idx[:, None] + 1.0))
    zeta = jnp.exp(log_gamma[None, :] * (c - 1.0 - idx[:, None]))
    decay = jnp.exp(log_gamma * c)
    xi_b = jnp.broadcast_to(xi.T[:, :, None], (h, c, RET_HEAD_DIM))
    zeta_b = jnp.broadcast_to(zeta.T[:, :, None], (h, c, RET_HEAD_DIM))
    return cos, sin_s, d_intra, jnp.swapaxes(d_intra, 1, 2), xi_b, zeta_b, decay


def _to_shards(full, cols):
    r = full.shape[0]
    return jnp.swapaxes(full.reshape(r, N_DEV, cols), 0, 1)


def _from_shards(sh):
    n, r, cols = sh.shape
    return jnp.swapaxes(sh, 0, 1).reshape(r, n * cols)


SMALL_ROWS = 216


def _pack_small(gains, conv_b, conv_w, sinks):
    parts = [g.reshape(8, 128) for g in gains] + [conv_b.reshape(44, 128), conv_w.reshape(132, 128),
                                                  jnp.pad(sinks.reshape(1, 8), ((0, 0), (0, 120)))]
    packed = jnp.concatenate(parts, axis=0)
    return jnp.pad(packed, ((0, SMALL_ROWS - packed.shape[0]), (0, 0)))


def kernel(x, mix_pre_norm, w_in, attn_sinks, w_out, mix_post_norm, ffn_pre_norm, w_up, conv_w, conv_b, w_down, ffn_post_norm, loss_target, m_mix_pre_norm, m_w_in, m_attn_sinks, m_w_out, m_mix_post_norm, m_ffn_pre_norm, m_w_up, m_conv_w, m_conv_b, m_w_down, m_ffn_post_norm, v_mix_pre_norm, v_w_in, v_attn_sinks, v_w_out, v_mix_post_norm, v_ffn_pre_norm, v_w_up, v_conv_w, v_conv_b, v_w_down, v_ffn_post_norm):
    T = x.shape[1]
    tm = min(512, T)
    tm_big = min(1024, T)
    nbs = min(8, T // BLOCK)
    x2 = x.reshape(T, D_MODEL)
    target = loss_target.reshape(T, D_MODEL)
    me = 4 * lax.axis_index("x") + 2 * lax.axis_index("y") + lax.axis_index("c")

    g_in, g_cw = _exchange_call([w_in[0].astype(BF), conv_w[0]], [], "gather_w_in")
    w_in_f = _from_shards(g_in)
    cos, sin_s, d_intra, d_intra_t, xi_b, zeta_b, decay = _tables(T)
    sinks = attn_sinks.reshape(N_ATTN_HEADS)

    (h1, qa, ka, va, qr, kr, vr, gr), (w_up8,) = _in_proj(
        x2, mix_pre_norm, w_in_f, cos, sin_s, tm_big, [w_up[0].astype(BF)])
    (a, lse, lse_t), (g_down, g_out) = _attn_fwd(sinks, qa, ka, va, nbs,
                                                 [w_down[0].astype(BF), w_out[0].astype(BF)])
    w_out_f = g_out.reshape(D_MODEL, D_MODEL)
    o, states, r = _ret_fwd(decay, qr, kr, vr, gr, d_intra, xi_b, zeta_b, nbs)
    mixed, x1, h2 = _out_proj(a, r, w_out_f, x2, mix_post_norm, ffn_pre_norm, tm_big)
    w_down4 = g_down.reshape(FF_PAIRS, FF_SHARD, D_MODEL)
    up_g, up_v, u_g, u_v, y4, dout, dz, dg4, loss_acc = _ffn_fwd(
        h2, w_up8, g_cw, conv_b.reshape(N_DEV, 1, FF_SHARD), w_down4, x1, ffn_post_norm, target, tm)
    loss = lax.psum(loss_acc[0, 0], ("x", "y", "c"))

    dup_g, dup_v, dcb_g, dcb_v, dcw_g, dcw_v, gwu_g, gwu_v, gw_down4 = _ffn_bwd_a(
        dz, h2, w_down4, u_g, u_v, up_g, up_v, y4, g_cw, tm)
    dcb = jnp.concatenate([dcb_g, dcb_v], axis=0).reshape(1, 2 * D_FF)
    dcw = _from_shards(jnp.concatenate([dcw_g, dcw_v], axis=0))
    gw_down = gw_down4.reshape(D_FF, D_MODEL)
    (dx1, dmixed, da, dr, dg3, dg2), p_up = _ffn_bwd_b(
        dup_g, dup_v, w_up8, x1, dout, ffn_pre_norm, mixed, mix_post_norm, w_out_f, tm, gwu_g, gwu_v)
    gw_out = _wgrad([a, r], [dmixed], tm, "wgrad_out")
    dret, (p_down,) = _ret_bwd(decay, qr, kr, vr, gr, o, states, dr, d_intra, d_intra_t, xi_b, zeta_b, cos, sin_s,
                               [gw_down.reshape(N_DEV, D_FF // N_DEV, D_MODEL).astype(BF)], nbs)
    (dqa, delta_t, dsink), (p_out,) = _attn_bwd_dq(sinks, qa, ka, va, da, lse, nbs,
                                                   [gw_out.reshape(N_DEV, D_MODEL // N_DEV, D_MODEL)])
    dka, dva = _attn_bwd_dkv(qa, ka, va, da, lse_t, delta_t, nbs)
    grad_x, dg1 = _in_proj_bwd(dqa, dka, dva, dret, w_in_f, x2, mix_pre_norm, dx1, tm)
    gw_in = _wgrad([h1], [dqa, dka, dva, dret], tm, "wgrad_in")

    small = _pack_small([dg1, dg2, dg3, dg4], dcb, dcw, dsink[:, 0])
    small_all, p_in = _exchange_call([small], [_to_shards(gw_in, IN_W // N_DEV).astype(BF)], "exchange_last")
    g_small = _sum_small(small_all)

    g_w_in, d_w_in, nm_w_in, nv_w_in = _adamw_shard(p_in, w_in[0], m_w_in[0], v_w_in[0], 256, "adamw_in")
    g_w_up, d_w_up, nm_w_up, nv_w_up = _adamw_shard(p_up, w_up[0], m_w_up[0], v_w_up[0], 256, "adamw_up")
    g_w_out, d_w_out, nm_w_out, nv_w_out = _adamw_shard(p_out, w_out[0], m_w_out[0], v_w_out[0], 128, "adamw_out")
    g_w_down, d_w_down, nm_w_down, nv_w_down = _adamw_shard(p_down, w_down[0], m_w_down[0], v_w_down[0], 176,
                                                            "adamw_down")
    gains = [mix_pre_norm, mix_post_norm, ffn_pre_norm, ffn_post_norm]
    m_gains = [m_mix_pre_norm, m_mix_post_norm, m_ffn_pre_norm, m_ffn_post_norm]
    v_gains = [v_mix_pre_norm, v_mix_post_norm, v_ffn_pre_norm, v_ffn_post_norm]
    zeros_cw = jnp.zeros((3, 2 * D_FF), F32)
    w_small = _pack_small(gains, conv_b, zeros_cw, attn_sinks)
    m_small = _pack_small(m_gains, m_conv_b, zeros_cw, m_attn_sinks)
    v_small = _pack_small(v_gains, v_conv_b, zeros_cw, v_attn_sinks)
    d_small, nm_small, nv_small = _adamw_small(g_small, w_small, m_small, v_small, "adamw_small")
    shard_cols = 2 * D_FF // N_DEV
    g_cw = lax.dynamic_slice(g_small[76:208].reshape(3, 2 * D_FF), (0, me * shard_cols), (3, shard_cols))
    d_cw, nm_cw, nv_cw = _adamw_small(g_cw, conv_w[0], m_conv_w[0], v_conv_w[0], "adamw_conv_w")

    def unpack(p):
        gains_o = [p[8 * i:8 * i + 8].reshape(1, D_MODEL) for i in range(4)]
        return gains_o, p[32:76].reshape(1, 2 * D_FF), p[208:209, :N_ATTN_HEADS]

    def leaves(p, w_in_s, w_out_s, w_up_s, cw_s, w_down_s):
        (pre1, post1, pre2, post2), cb, sk = unpack(p)
        return [pre1, w_in_s[None], sk, w_out_s[None], post1, pre2, w_up_s[None], cw_s[None], cb, w_down_s[None],
                post2]

    return (loss, grad_x.reshape(1, T, D_MODEL),
            *leaves(g_small, g_w_in, g_w_out, g_w_up, g_cw, g_w_down),
            *leaves(d_small, d_w_in, d_w_out, d_w_up, d_cw, d_w_down),
            *leaves(nm_small, nm_w_in, nm_w_out, nm_w_up, nm_cw, nm_w_down),
            *leaves(nv_small, nv_w_in, nv_w_out, nv_w_up, nv_cw, nv_w_down))
```

```python
import functools
import math

import jax
import jax.numpy as jnp
from jax import lax
from jax.experimental import pallas as pl
from jax.experimental.pallas import tpu as pltpu

F32 = jnp.float32
BF = jnp.bfloat16

N_DEV = 8
D_MODEL = 1024
HEAD_DIM = 64
ATTN_W = 512
N_ATTN_HEADS = 8
KV_W = 128
BLOCK = 128
RET_W = 512
N_RET_HEADS = 4
RET_HEAD_DIM = 128
IN_W = 2816
D_FF = 2816
RMS_EPS = 1e-6
GN_EPS = 1e-6
MASK_VALUE = -1e30
ATTN_SCALE = HEAD_DIM ** -0.5
RET_K_SCALE = RET_HEAD_DIM ** -0.5
GELU_C = math.sqrt(2.0 / math.pi)
GELU_A = 0.044715

ADAM_LR = 0.001
ADAM_B1 = 0.9
ADAM_B2 = 0.999
ADAM_EPS = 1e-08
ADAM_WD = 0.01
ADAM_STEP = 10

VMEM_LIMIT_BYTES = 56 * 1024 * 1024
FF_SHARD = 2 * D_FF // N_DEV
FF_PAIRS = N_DEV // 2

QA0, KA0, VA0, QR0, KR0, VR0, GR0 = 0, 512, 640, 768, 1280, 1792, 2304

MESH_ID = pl.DeviceIdType.MESH


def _pcall(body, **kw):
    return pl.pallas_call(body, **kw)


def _params(sem=None):
    return pltpu.CompilerParams(dimension_semantics=sem, vmem_limit_bytes=VMEM_LIMIT_BYTES)


def _dot(a, b):
    return jnp.dot(a, b, preferred_element_type=F32)


def _dot_nt(a, b):
    return lax.dot_general(a, b, (((1,), (1,)), ((), ())), preferred_element_type=F32)


def _dot_tn(a, b):
    return lax.dot_general(a, b, (((0,), (0,)), ((), ())), preferred_element_type=F32)


def _vmem_full():
    return pl.BlockSpec(memory_space=pltpu.VMEM)


def _smem_full():
    return pl.BlockSpec(memory_space=pltpu.SMEM)


def _rows(tm, w):
    return pl.BlockSpec((tm, w), lambda i: (i, 0))


def _const(shape):
    return pl.BlockSpec(shape, lambda i: tuple(0 for _ in shape))


def _rms_stats(x):
    r = lax.rsqrt(jnp.mean(x * x, axis=-1, keepdims=True) + RMS_EPS)
    return r, x * r


def _rms_bwd(n, r, dn):
    return r * (dn - n * jnp.mean(dn * n, axis=-1, keepdims=True))


def _rot(x, even):
    w = x.shape[1]
    return jnp.where(even, pltpu.roll(x, w - 1, 1), pltpu.roll(x, 1, 1))


def _peers():
    x, y, c = lax.axis_index("x"), lax.axis_index("y"), lax.axis_index("c")
    flips = [(0, 0, 1), (1, 0, 0), (0, 1, 0), (1, 1, 0), (1, 0, 1), (0, 1, 1), (1, 1, 1)]
    peers = [(x ^ fx, y ^ fy, c ^ fc) for fx, fy, fc in flips]
    return 4 * x + 2 * y + c, peers


SAME_CORE_PEERS = 4


class _Exchange:
    def __init__(self, gathers, swaps, send_sems, recv_sems, local_sems):
        self.me, self.peers = _peers()
        self.slots = [4 * px + 2 * py + pc for px, py, pc in self.peers]
        self.pairs = [(src, dst, True) for src, dst in gathers] + [(src, dst, False) for src, dst in swaps]
        self.send_sems, self.recv_sems, self.local_sems = send_sems, recv_sems, local_sems

    @staticmethod
    def scratch(n):
        return [pltpu.SemaphoreType.DMA((n, N_DEV - 1)), pltpu.SemaphoreType.DMA((n, N_DEV - 1)),
                pltpu.SemaphoreType.DMA((n,))]

    def _parts(self, a, slot):
        src, _, whole = self.pairs[a]
        half = N_DEV // 2
        if whole:
            return [(None, src)]
        if isinstance(src, tuple):
            return [(slot < half, src[0].at[jnp.minimum(slot, half - 1)]),
                    (slot >= half, src[1].at[jnp.maximum(slot - half, 0)])]
        return [(None, src.at[slot])]

    def _local(self, a, src):
        return pltpu.make_async_copy(src, self.pairs[a][1].at[self.me], self.local_sems.at[a])

    def _remote(self, a, k, src, slot):
        return pltpu.make_async_remote_copy(
            src_ref=src, dst_ref=self.pairs[a][1].at[slot], send_sem=self.send_sems.at[a, k],
            recv_sem=self.recv_sems.at[a, k], device_id=self.peers[k], device_id_type=MESH_ID)

    def start(self):
        def go(cond, copy):
            if cond is None:
                copy.start()
            else:
                pl.when(cond)(copy.start)

        for a in range(len(self.pairs)):
            for cond, src in self._parts(a, self.me):
                go(cond, self._local(a, src))
            for k in range(SAME_CORE_PEERS if self.pairs[a][2] else N_DEV - 1):
                for cond, src in self._parts(a, self.slots[k]):
                    go(cond, self._remote(a, k, src, self.me))

    def _pass_on(self, a, j):
        k = j + SAME_CORE_PEERS - 1
        block = self.pairs[a][1].at[self.slots[j]]
        return pltpu.make_async_remote_copy(
            src_ref=block, dst_ref=block, send_sem=self.send_sems.at[a, k], recv_sem=self.recv_sems.at[a, k],
            device_id=self.peers[0], device_id_type=MESH_ID)

    def wait(self):
        for a in range(len(self.pairs)):
            src = self._parts(a, self.me)[0][1]
            if self.pairs[a][2]:
                for j in range(1, SAME_CORE_PEERS):
                    self._remote(a, j, src, self.slots[j]).wait_recv()
                    self._pass_on(a, j).start()
                self._remote(a, 0, src, self.slots[0]).wait_recv()
            for k in range(SAME_CORE_PEERS if self.pairs[a][2] else 0, N_DEV - 1):
                self._remote(a, k, src, self.slots[k]).wait_recv()
        for a in range(len(self.pairs)):
            src = self._parts(a, self.me)[0][1]
            for k in range(SAME_CORE_PEERS if self.pairs[a][2] else N_DEV - 1):
                self._remote(a, k, src, self.me).wait_send()
            if self.pairs[a][2]:
                for j in range(1, SAME_CORE_PEERS):
                    self._pass_on(a, j).wait_send()
            self._local(a, src).wait()


ANY_SPEC = pl.BlockSpec(memory_space=pl.ANY)


def _exchange_shapes(gathers, swaps):
    return ([jax.ShapeDtypeStruct((N_DEV,) + a.shape, a.dtype) for a in gathers]
            + [jax.ShapeDtypeStruct(a.shape, a.dtype) for a in swaps])


def _exchange_of(ins, outs, sems, ng):
    return _Exchange(list(zip(ins[:ng], outs[:ng])), list(zip(ins[ng:], outs[ng:])), *sems)


def _exchange_call(gathers, swaps, name):
    ng, ns = len(gathers), len(swaps)
    n = ng + ns

    def body(*refs):
        ex = _exchange_of(refs[:n], refs[n:2 * n], refs[2 * n:], ng)
        ex.start()
        ex.wait()

    return _pcall(
        body, name=name, out_shape=_exchange_shapes(gathers, swaps),
        in_specs=[ANY_SPEC] * (ng + ns), out_specs=[ANY_SPEC] * (ng + ns),
        scratch_shapes=_Exchange.scratch(ng + ns),
    )(*gathers, *swaps)


def _in_proj(x, g1, w_in, cos, sin_s, tm, gathers):
    T = x.shape[0]
    ng = len(gathers)
    nt = T // tm

    def body(x_ref, g_ref, w_ref, cos_ref, sin_ref, *rest):
        ex = _exchange_of(rest[:ng], rest[ng + 8:2 * ng + 8], rest[2 * ng + 8:], ng)
        h_ref, qa_ref, ka_ref, va_ref, qr_ref, kr_ref, vr_ref, gr_ref = rest[ng:ng + 8]
        pl.when(pl.program_id(0) == 0)(ex.start)
        r, n = _rms_stats(x_ref[...])
        h = (n * g_ref[...]).astype(BF)
        h_ref[...] = h

        def proj(c0, w):
            return _dot(h, w_ref[:, c0:c0 + w])

        qa_ref[...] = proj(QA0, ATTN_W).astype(BF)
        kva = proj(KA0, 2 * KV_W)
        ka_ref[...] = kva[:, :KV_W].astype(BF)
        va_ref[...] = kva[:, KV_W:].astype(BF)
        vr_ref[...] = proj(VR0, RET_W).astype(BF)
        gr_ref[...] = proj(GR0, RET_W)
        cos_t, sin_t = cos_ref[...], sin_ref[...]
        even = lax.broadcasted_iota(jnp.int32, (tm, RET_HEAD_DIM), 1) % 2 == 0
        for c0, scale, out_ref in ((QR0, None, qr_ref), (KR0, RET_K_SCALE, kr_ref)):
            full = proj(c0, RET_W)
            for hd in range(N_RET_HEADS):
                cs = slice(hd * RET_HEAD_DIM, (hd + 1) * RET_HEAD_DIM)
                t = full[:, cs] if scale is None else full[:, cs] * scale
                out_ref[:, cs] = (t * cos_t + _rot(t, even) * sin_t).astype(BF)
        pl.when(pl.program_id(0) == nt - 1)(ex.wait)

    widths = [D_MODEL, ATTN_W, KV_W, KV_W, RET_W, RET_W, RET_W, RET_W]
    dts = [BF] * 7 + [F32]
    outs = _pcall(
        body, name="in_proj", grid=(nt,),
        in_specs=[_rows(tm, D_MODEL), _const((1, D_MODEL)), _vmem_full(), _rows(tm, RET_HEAD_DIM),
                  _rows(tm, RET_HEAD_DIM)] + [ANY_SPEC] * ng,
        out_specs=[_rows(tm, w) for w in widths] + [ANY_SPEC] * ng,
        out_shape=[jax.ShapeDtypeStruct((T, w), dt) for w, dt in zip(widths, dts)] + _exchange_shapes(gathers, []),
        scratch_shapes=_Exchange.scratch(ng),
        compiler_params=_params(("arbitrary",)),
    )(x, g1, w_in, cos, sin_s, *gathers)
    return outs[:8], outs[8:]


def _kv_variants(kk):
    kf = kk.astype(F32)
    lo = lax.broadcasted_iota(jnp.int32, kf.shape, 1) < HEAD_DIM
    h0_lo = jnp.where(lo, kf, 0.0)
    h1_hi = jnp.where(lo, 0.0, kf)
    h0_hi = pltpu.roll(h0_lo, HEAD_DIM, 1)
    h1_lo = pltpu.roll(h1_hi, HEAD_DIM, 1)
    return [[h0_lo.astype(BF), h0_hi.astype(BF)], [h1_lo.astype(BF), h1_hi.astype(BF)]]


def _col_to_tile(tile, col, head):
    lane = lax.broadcasted_iota(jnp.int32, tile.shape, 1)
    return jnp.where(lane == head, col, tile)


def _tri(rows, key_major=False):
    i = lax.broadcasted_iota(jnp.int32, (rows, BLOCK), 0) & (BLOCK - 1)
    j = lax.broadcasted_iota(jnp.int32, (rows, BLOCK), 1)
    return i > j if key_major else j > i


def _fold(x2, tri, first_above):
    a, b = x2[:, :BLOCK], x2[:, BLOCK:]
    return jnp.where(tri, a, b) if first_above else jnp.where(tri, b, a)


def _unfold(x, tri, first_above):
    up, low = jnp.where(tri, x, 0.0), jnp.where(tri, 0.0, x)
    return jnp.concatenate([up, low] if first_above else [low, up], axis=1).astype(BF)


def _scaled(q):
    return (q.astype(F32) * ATTN_SCALE).astype(BF)


def _cat_variants(prev, cur):
    return [[jnp.concatenate([prev[h][e], cur[h][e]], axis=0) for e in range(2)] for h in range(2)]


def _block_variants(prev_ref, cur_ref, nbs):
    var = [_kv_variants(prev_ref[...])] + [_kv_variants(cur_ref[b * BLOCK:(b + 1) * BLOCK, :]) for b in range(nbs)]
    return [_cat_variants(var[b], var[b + 1]) for b in range(nbs)]


def _head_cols(col, nbs):
    tiles = []
    for b in range(nbs):
        t = jnp.zeros((BLOCK, BLOCK), F32)
        for head in range(N_ATTN_HEADS):
            r0 = (b * N_ATTN_HEADS + head) * BLOCK
            t = _col_to_tile(t, col[r0:r0 + BLOCK, :], head)
        tiles.append(t)
    return tiles


def _attn_fwd(sinks, qa, ka, va, nbs, gathers):
    T = qa.shape[0]
    steps = T // (BLOCK * nbs)
    R = nbs * N_ATTN_HEADS * BLOCK
    ng = len(gathers)

    def body(sink_ref, q_ref, kc_ref, kp_ref, vc_ref, vp_ref, *rest):
        ex = _exchange_of(rest[:ng], rest[ng + 3:2 * ng + 3], rest[2 * ng + 3:], ng)
        a_ref, lse_ref, lset_ref = rest[ng:ng + 3]
        n = pl.program_id(0)
        pl.when(n == 0)(ex.start)
        kcat = _block_variants(kp_ref, kc_ref, nbs)
        vcat = _block_variants(vp_ref, vc_ref, nbs)
        tri1 = _tri(BLOCK)
        tiles = []
        for b in range(nbs):
            for pair in range(N_ATTN_HEADS // 2):
                qp = _scaled(q_ref[b * BLOCK:(b + 1) * BLOCK, pair * 128:(pair + 1) * 128])
                for e in range(2):
                    s = _fold(_dot_nt(qp, kcat[b][pair // 2][e]), tri1, True)
                    if b == 0:
                        s = jnp.where(tri1 & (n == 0), MASK_VALUE, s)
                    tiles.append(s)
        s = jnp.concatenate(tiles, axis=0)
        sink = jnp.concatenate([jnp.full((BLOCK, 1), sink_ref[head], F32)
                                for _ in range(nbs) for head in range(N_ATTN_HEADS)], axis=0)
        m = jnp.maximum(jnp.max(s, axis=-1, keepdims=True), sink)
        p = jnp.exp(s - m)
        z = jnp.sum(p, axis=-1, keepdims=True) + jnp.exp(sink - m)
        p2 = _unfold(p * (1.0 / z), _tri(R), True)
        for b in range(nbs):
            for pair in range(N_ATTN_HEADS // 2):
                r0 = (b * N_ATTN_HEADS + 2 * pair) * BLOCK
                acc = (_dot(p2[r0:r0 + BLOCK, :], vcat[b][pair // 2][0])
                       + _dot(p2[r0 + BLOCK:r0 + 2 * BLOCK, :], vcat[b][pair // 2][1]))
                a_ref[b * BLOCK:(b + 1) * BLOCK, pair * 128:(pair + 1) * 128] = acc.astype(BF)
        for b, t in enumerate(_head_cols(m + jnp.log(z), nbs)):
            lse_ref[b * BLOCK:(b + 1) * BLOCK, :] = t
            lset_ref[:, b * BLOCK:(b + 1) * BLOCK] = t.T[:N_ATTN_HEADS, :]
        pl.when(n == steps - 1)(ex.wait)

    cur = lambda w: pl.BlockSpec((BLOCK * nbs, w), lambda n: (n, 0))
    prev = lambda w: pl.BlockSpec((BLOCK, w), lambda n: (jnp.maximum(n * nbs - 1, 0), 0))
    outs = _pcall(
        body, name="attn_fwd", grid=(steps,),
        in_specs=[_smem_full(), cur(ATTN_W), cur(KV_W), prev(KV_W), cur(KV_W), prev(KV_W)] + [ANY_SPEC] * ng,
        out_specs=[cur(ATTN_W), cur(BLOCK), pl.BlockSpec((N_ATTN_HEADS, BLOCK * nbs), lambda n: (0, n))]
        + [ANY_SPEC] * ng,
        out_shape=[jax.ShapeDtypeStruct((T, ATTN_W), BF), jax.ShapeDtypeStruct((T, BLOCK), F32),
                   jax.ShapeDtypeStruct((N_ATTN_HEADS, T), F32)] + _exchange_shapes(gathers, []),
        scratch_shapes=_Exchange.scratch(ng),
        compiler_params=_params(("arbitrary",)),
    )(sinks, qa, ka, ka, va, va, *gathers)
    return outs[:3], outs[3:]


def _ret_fwd(decay, qr, kr, vr, gr, d_intra, xi_b, zeta_b, ncs):
    T = qr.shape[0]
    nc = T // BLOCK
    H, C = N_RET_HEADS, RET_HEAD_DIM

    def body(decay_ref, q_ref, k_ref, v_ref, g_ref, d_ref, xi_ref, zeta_ref, o_ref, s_ref, r_ref, state):
        @pl.when(pl.program_id(0) == 0)
        def _():
            state[...] = jnp.zeros_like(state)

        pairs = [(b, h) for b in range(ncs) for h in range(H)]
        sl = lambda b, h: (slice(b * BLOCK, (b + 1) * BLOCK), slice(h * C, (h + 1) * C))
        tab = lambda ref: jnp.concatenate([ref[h] for _, h in pairs], axis=0)
        q = [q_ref[sl(b, h)] for b, h in pairs]
        k = [k_ref[sl(b, h)] for b, h in pairs]
        v = [v_ref[sl(b, h)] for b, h in pairs]
        inner = (jnp.concatenate([_dot_nt(q[i], k[i]) for i in range(len(pairs))], axis=0) * tab(d_ref)).astype(BF)
        kz = (jnp.concatenate(k, axis=0).astype(F32) * tab(zeta_ref)).astype(BF)
        o1 = [_dot(inner[i * BLOCK:(i + 1) * BLOCK, :], v[i]) for i in range(len(pairs))]
        kv = [_dot_tn(kz[i * BLOCK:(i + 1) * BLOCK, :], v[i]) for i in range(len(pairs))]
        st_b = [None] * len(pairs)
        for h in range(H):
            st = state[h]
            for b in range(ncs):
                i = b * H + h
                st_b[i] = st.astype(BF)
                s_ref[b, h] = st_b[i]
                st = decay_ref[h] * st + kv[i]
            state[h] = st
        o2 = jnp.concatenate([_dot(q[i], st_b[i]) for i in range(len(pairs))], axis=0)
        o = jnp.concatenate(o1, axis=0) + o2 * tab(xi_ref)
        mu = jnp.mean(o, axis=-1, keepdims=True)
        oc = o - mu
        rs = lax.rsqrt(jnp.mean(oc * oc, axis=-1, keepdims=True) + GN_EPS)
        g = jnp.concatenate([g_ref[sl(b, h)] for b, h in pairs], axis=0)
        r = (g * jax.nn.sigmoid(g) * (oc * rs)).astype(BF)
        for i, (b, h) in enumerate(pairs):
            o_ref[sl(b, h)] = o[i * BLOCK:(i + 1) * BLOCK, :]
            r_ref[sl(b, h)] = r[i * BLOCK:(i + 1) * BLOCK, :]

    cur = pl.BlockSpec((BLOCK * ncs, RET_W), lambda n: (n, 0))
    tab = pl.BlockSpec((H, C, C), lambda n: (0, 0, 0))
    return _pcall(
        body, name="ret_fwd", grid=(nc // ncs,),
        in_specs=[_smem_full(), cur, cur, cur, cur, tab, tab, tab],
        out_specs=[cur, pl.BlockSpec((ncs, H, C, C), lambda n: (n, 0, 0, 0)), cur],
        out_shape=[jax.ShapeDtypeStruct((T, RET_W), F32), jax.ShapeDtypeStruct((nc, H, C, C), BF),
                   jax.ShapeDtypeStruct((T, RET_W), BF)],
        scratch_shapes=[pltpu.VMEM((H, C, C), F32)],
        compiler_params=_params(("arbitrary",)),
    )(decay, qr, kr, vr, gr, d_intra, xi_b, zeta_b)


def _out_proj(a, r, w_out, x, g2, g3, tm):
    T = x.shape[0]

    def body(a_ref, r_ref, w_ref, x_ref, g2_ref, g3_ref, mixed_ref, x1_ref, h2_ref):
        mixed = _dot(a_ref[...], w_ref[:ATTN_W, :]) + _dot(r_ref[...], w_ref[ATTN_W:, :])
        mixed_ref[...] = mixed
        _, n2 = _rms_stats(mixed)
        x1 = x_ref[...] + n2 * g2_ref[...]
        x1_ref[...] = x1
        _, n3 = _rms_stats(x1)
        h2_ref[...] = (n3 * g3_ref[...]).astype(BF)

    return _pcall(
        body, name="out_proj", grid=(T // tm,),
        in_specs=[_rows(tm, ATTN_W), _rows(tm, RET_W), _vmem_full(), _rows(tm, D_MODEL), _const((1, D_MODEL)),
                  _const((1, D_MODEL))],
        out_specs=[_rows(tm, D_MODEL)] * 3,
        out_shape=[jax.ShapeDtypeStruct((T, D_MODEL), F32), jax.ShapeDtypeStruct((T, D_MODEL), F32),
                   jax.ShapeDtypeStruct((T, D_MODEL), BF)],
        compiler_params=_params(("parallel",)),
    )(a, r, w_out, x, g2, g3)


def _shift_down(cur, k, before):
    out = pltpu.roll(cur, k, 0)
    row = lax.broadcasted_iota(jnp.int32, before.shape, 0)
    top = jnp.where(row < k, pltpu.roll(before, k, 0), out[0:8])
    return jnp.concatenate([top, out[8:]], axis=0)


def _shift_up(cur, k, after):
    tm = cur.shape[0]
    out = pltpu.roll(cur, tm - k, 0)
    row = lax.broadcasted_iota(jnp.int32, after.shape, 0)
    bot = jnp.where(row >= 8 - k, pltpu.roll(after, 8 - k, 0), out[tm - 8:])
    return jnp.concatenate([out[:tm - 8], bot], axis=0)


def _gelu_parts(x):
    m = (-2.0 * GELU_C * GELU_A) * (x * x)
    s = 1.0 / (1.0 + jnp.exp(x * (m - 2.0 * GELU_C)))
    gelu = x * s
    dgelu = s + gelu * (1.0 - s) * (2.0 * GELU_C - 3.0 * m)
    return gelu, dgelu


def _ffn_fwd(h2, w_up8, conv_w8, conv_b8, w_down4, x1, g4, target, tm):
    T = h2.shape[0]
    nt = T // tm

    def body(h_ref, wu_ref, cwg_ref, cwv_ref, cbg_ref, cbv_ref, wd_ref, x1_ref, g_ref, t_ref,
             upg_ref, upv_ref, ug_ref, uv_ref, y_ref, dout_ref, dz_ref, dg4_ref, loss_ref, halo, z_acc):
        s = pl.program_id(1)
        first = pl.program_id(0) == 0

        @pl.when(first & (s == 0))
        def _():
            loss_ref[...] = jnp.zeros_like(loss_ref)
            dg4_ref[...] = jnp.zeros_like(dg4_ref)

        h = h_ref[...]
        u = []
        parts = ((cwg_ref, cbg_ref, upg_ref, ug_ref), (cwv_ref, cbv_ref, upv_ref, uv_ref))
        for part, (cw_ref, cb_ref, up_ref, u_ref) in enumerate(parts):
            cur = _dot(h, wu_ref[s + part * FF_PAIRS])
            up_ref[0] = cur.astype(BF)
            before = jnp.where(first, 0.0, halo[part, s])
            halo[part, s] = cur[tm - 8:tm, :]
            u_c = (cw_ref[0, pl.ds(0, 1), :] * _shift_down(cur, 2, before)
                   + cw_ref[0, pl.ds(1, 1), :] * _shift_down(cur, 1, before)
                   + cw_ref[0, pl.ds(2, 1), :] * cur + cb_ref[0])
            u_ref[0] = u_c
            u.append(u_c)
        gelu, _ = _gelu_parts(u[0])
        y = (gelu * u[1]).astype(BF)
        y_ref[0] = y
        z_part = _dot(y, wd_ref[s])

        @pl.when(s == 0)
        def _():
            z_acc[...] = z_part

        @pl.when(s > 0)
        def _():
            z_acc[...] += z_part

        @pl.when(s == FF_PAIRS - 1)
        def _():
            r4, n4 = _rms_stats(z_acc[...])
            err = x1_ref[...] + n4 * g_ref[...] - t_ref[...]
            dout = err * (1.0 / D_MODEL)
            dout_ref[...] = dout
            loss_ref[...] += 0.5 * jnp.sum(jnp.mean(err * err, axis=-1, keepdims=True), axis=0, keepdims=True)
            dg4_ref[...] += jnp.sum(dout * n4, axis=0, keepdims=True)
            dz_ref[...] = _rms_bwd(n4, r4, dout * g_ref[...]).astype(BF)

    rows = pl.BlockSpec((tm, D_MODEL), lambda i, s: (i, 0))
    one = lambda shape: pl.BlockSpec(shape, lambda i, s: tuple(0 for _ in shape))
    gate = lambda r, w: pl.BlockSpec((1, r, w), lambda i, s: (s, 0, 0))
    val = lambda r, w: pl.BlockSpec((1, r, w), lambda i, s: (s + FF_PAIRS, 0, 0))
    tile = pl.BlockSpec((1, tm, FF_SHARD), lambda i, s: (s, i, 0))
    half = lambda dt: jax.ShapeDtypeStruct((FF_PAIRS, T, FF_SHARD), dt)
    return _pcall(
        body, name="ffn_fwd", grid=(nt, FF_PAIRS),
        in_specs=[rows, _vmem_full(), gate(3, FF_SHARD), val(3, FF_SHARD), gate(1, FF_SHARD), val(1, FF_SHARD),
                  _vmem_full(), rows, one((1, D_MODEL)), rows],
        out_specs=[tile] * 5 + [rows, rows, one((1, D_MODEL)), one((8, 128))],
        out_shape=[half(BF), half(BF), half(F32), half(F32), half(BF), jax.ShapeDtypeStruct((T, D_MODEL), F32),
                   jax.ShapeDtypeStruct((T, D_MODEL), BF), jax.ShapeDtypeStruct((1, D_MODEL), F32),
                   jax.ShapeDtypeStruct((8, 128), F32)],
        scratch_shapes=[pltpu.VMEM((2, FF_PAIRS, 8, FF_SHARD), F32), pltpu.VMEM((tm, D_MODEL), F32)],
        compiler_params=_params(("arbitrary", "arbitrary")),
    )(h2, w_up8, conv_w8, conv_w8, conv_b8, conv_b8, w_down4, x1, g4, target)


def _ffn_bwd_a(dz, h2, w_down4, u_g, u_v, up_g, up_v, y4, conv_w8, tm):
    T = dz.shape[0]
    nt = T // tm

    def body(dz_ref, h_ref, wd_ref, ug_ref, uv_ref, upg_ref, upv_ref, y_ref, cwg_ref, cwv_ref,
             dupg_ref, dupv_ref, dcbg_ref, dcbv_ref, dcwg_ref, dcwv_ref, gwug_out, gwuv_out, gwd_out,
             carry, gwug_ref, gwuv_ref, gwd_ref):
        @pl.when(pl.program_id(1) == 0)
        def _():
            for ref in (dcbg_ref, dcbv_ref, dcwg_ref, dcwv_ref, gwug_ref, gwuv_ref, gwd_ref, carry):
                ref[...] = jnp.zeros_like(ref)

        dz = dz_ref[...]
        h = h_ref[...]
        dy = _dot_nt(dz, wd_ref[0])
        gwd_ref[0] += _dot_tn(y_ref[0], dz)
        gelu, dgelu = _gelu_parts(ug_ref[0])
        parts = ((0, dy * uv_ref[0] * dgelu, upg_ref, cwg_ref, dupg_ref, dcbg_ref, dcwg_ref, gwug_ref),
                 (1, dy * gelu, upv_ref, cwv_ref, dupv_ref, dcbv_ref, dcwv_ref, gwuv_ref))
        for part, d, up_ref, cw_ref, dup_ref, dcb_ref, dcw_ref, gwu_ref in parts:
            after = carry[part]
            d1 = _shift_up(d, 1, after)
            d2 = _shift_up(d, 2, after)
            carry[part] = d[0:8, :]
            upc = up_ref[0].astype(F32)
            dcb_ref[0] += jnp.sum(d, axis=0, keepdims=True)
            dcw_ref[0, pl.ds(2, 1), :] += jnp.sum(d * upc, axis=0, keepdims=True)
            dcw_ref[0, pl.ds(1, 1), :] += jnp.sum(d1 * upc, axis=0, keepdims=True)
            dcw_ref[0, pl.ds(0, 1), :] += jnp.sum(d2 * upc, axis=0, keepdims=True)
            dup = (cw_ref[0, pl.ds(2, 1), :] * d + cw_ref[0, pl.ds(1, 1), :] * d1
                   + cw_ref[0, pl.ds(0, 1), :] * d2).astype(BF)
            dup_ref[0] = dup
            gwu_ref[0] += _dot_tn(h, dup)

        @pl.when(pl.program_id(1) == nt - 1)
        def _():
            s = pl.program_id(0)
            pltpu.sync_copy(gwug_ref, gwug_out.at[pl.ds(s, 1)])
            pltpu.sync_copy(gwuv_ref, gwuv_out.at[pl.ds(s, 1)])
            pltpu.sync_copy(gwd_ref, gwd_out.at[pl.ds(s, 1)])

    rev = pl.BlockSpec((tm, D_MODEL), lambda s, i: (nt - 1 - i, 0))
    tile = pl.BlockSpec((1, tm, FF_SHARD), lambda s, i: (s, nt - 1 - i, 0))
    acc = lambda r, w: pl.BlockSpec((1, r, w), lambda s, i: (s, 0, 0))
    acc_val = pl.BlockSpec((1, 3, FF_SHARD), lambda s, i: (s + FF_PAIRS, 0, 0))
    half = lambda r, dt: jax.ShapeDtypeStruct((FF_PAIRS, r, FF_SHARD), dt)
    return _pcall(
        body, name="ffn_bwd_a", grid=(FF_PAIRS, nt),
        in_specs=[rev, rev, acc(FF_SHARD, D_MODEL), tile, tile, tile, tile, tile, acc(3, FF_SHARD), acc_val],
        out_specs=[tile, tile, acc(1, FF_SHARD), acc(1, FF_SHARD), acc(3, FF_SHARD), acc(3, FF_SHARD),
                   ANY_SPEC, ANY_SPEC, ANY_SPEC],
        out_shape=[half(T, BF), half(T, BF), half(1, F32), half(1, F32), half(3, F32), half(3, F32),
                   half(D_MODEL, F32), half(D_MODEL, F32), jax.ShapeDtypeStruct((FF_PAIRS, FF_SHARD, D_MODEL), F32)],
        scratch_shapes=[pltpu.VMEM((2, 8, FF_SHARD), F32), pltpu.VMEM((1, D_MODEL, FF_SHARD), F32),
                        pltpu.VMEM((1, D_MODEL, FF_SHARD), F32), pltpu.VMEM((1, FF_SHARD, D_MODEL), F32)],
        compiler_params=_params(("arbitrary", "arbitrary")),
    )(dz, h2, w_down4, u_g, u_v, up_g, up_v, y4, conv_w8, conv_w8)


def _ffn_bwd_b(dup_g, dup_v, w_up8, x1, dout, g3, mixed, g2, w_out, tm, gwu_g, gwu_v):
    T = x1.shape[0]
    nt = T // tm

    def body(dupg_ref, dupv_ref, wup_ref, x1_ref, dout_ref, g3_ref, mixed_ref, g2_ref, wout_ref, gwug_ref, gwuv_ref,
             dx1_ref, dmixed_ref, da_ref, dr_ref, dg3_ref, dg2_ref, pup_ref, *sems):
        ex = _Exchange([], [((gwug_ref, gwuv_ref), pup_ref)], *sems)

        @pl.when(pl.program_id(0) == 0)
        def _():
            ex.start()
            dg3_ref[...] = jnp.zeros_like(dg3_ref)
            dg2_ref[...] = jnp.zeros_like(dg2_ref)

        dh2 = jnp.zeros((tm, D_MODEL), F32)
        for s in range(FF_PAIRS):
            dh2 = dh2 + _dot_nt(dupg_ref[s], wup_ref[s]) + _dot_nt(dupv_ref[s], wup_ref[s + FF_PAIRS])
        r3, n3 = _rms_stats(x1_ref[...])
        dg3_ref[...] += jnp.sum(dh2 * n3, axis=0, keepdims=True)
        dx1 = dout_ref[...] + _rms_bwd(n3, r3, dh2 * g3_ref[...])
        dx1_ref[...] = dx1
        r2, n2 = _rms_stats(mixed_ref[...])
        dg2_ref[...] += jnp.sum(dx1 * n2, axis=0, keepdims=True)
        dmixed = _rms_bwd(n2, r2, dx1 * g2_ref[...]).astype(BF)
        dmixed_ref[...] = dmixed
        da_ref[...] = _dot_nt(dmixed, wout_ref[:ATTN_W, :])
        dr_ref[...] = _dot_nt(dmixed, wout_ref[ATTN_W:, :])
        pl.when(pl.program_id(0) == nt - 1)(ex.wait)

    half = pl.BlockSpec((FF_PAIRS, tm, FF_SHARD), lambda i: (0, i, 0))
    outs = _pcall(
        body, name="ffn_bwd_b", grid=(nt,),
        in_specs=[half, half, _vmem_full(), _rows(tm, D_MODEL), _rows(tm, D_MODEL), _const((1, D_MODEL)),
                  _rows(tm, D_MODEL), _const((1, D_MODEL)), _vmem_full(), ANY_SPEC, ANY_SPEC],
        out_specs=[_rows(tm, D_MODEL), _rows(tm, D_MODEL), _rows(tm, ATTN_W), _rows(tm, RET_W),
                   _const((1, D_MODEL)), _const((1, D_MODEL)), ANY_SPEC],
        out_shape=[jax.ShapeDtypeStruct((T, D_MODEL), F32), jax.ShapeDtypeStruct((T, D_MODEL), BF),
                   jax.ShapeDtypeStruct((T, ATTN_W), F32), jax.ShapeDtypeStruct((T, RET_W), F32),
                   jax.ShapeDtypeStruct((1, D_MODEL), F32), jax.ShapeDtypeStruct((1, D_MODEL), F32),
                   jax.ShapeDtypeStruct((N_DEV, D_MODEL, FF_SHARD), F32)],
        scratch_shapes=_Exchange.scratch(1),
        compiler_params=_params(("arbitrary",)),
    )(dup_g, dup_v, w_up8, x1, dout, g3, mixed, g2, w_out, gwu_g, gwu_v)
    return outs[:6], outs[6]


def _ret_bwd(decay, qr, kr, vr, gr, o, states, dr, d_intra, d_intra_t, xi_b, zeta_b, cos, sin_s, swaps, ncs):
    T = qr.shape[0]
    nc = T // BLOCK
    H, C = N_RET_HEADS, RET_HEAD_DIM
    ns = len(swaps)

    def body(decay_ref, q_ref, k_ref, v_ref, g_ref, o_ref, s_ref, dr_ref, d_ref, dt_ref, xi_ref, zeta_ref,
             cos_ref, sin_ref, *rest):
        ex = _exchange_of(rest[:ns], rest[ns + 1:2 * ns + 1], rest[2 * ns + 2:], 0)
        dret_ref, gstate = rest[ns], rest[2 * ns + 1]

        @pl.when(pl.program_id(0) == 0)
        def _():
            ex.start()
            gstate[...] = jnp.zeros_like(gstate)

        pairs = [(b, h) for b in range(ncs) for h in range(H)]
        n = len(pairs)
        sl = lambda b, h: (slice(b * BLOCK, (b + 1) * BLOCK), slice(h * C, (h + 1) * C))
        cat = lambda ref: jnp.concatenate([ref[sl(b, h)] for b, h in pairs], axis=0)
        tab = lambda ref: jnp.concatenate([ref[h] for _, h in pairs], axis=0)
        part = lambda x, i: x[i * BLOCK:(i + 1) * BLOCK, :]
        q = [q_ref[sl(b, h)] for b, h in pairs]
        k = [k_ref[sl(b, h)] for b, h in pairs]
        v = [v_ref[sl(b, h)] for b, h in pairs]
        g, o_all, dr_all = cat(g_ref), cat(o_ref), cat(dr_ref)
        mu = jnp.mean(o_all, axis=-1, keepdims=True)
        oc = o_all - mu
        rs = lax.rsqrt(jnp.mean(oc * oc, axis=-1, keepdims=True) + GN_EPS)
        on = oc * rs
        sg = jax.nn.sigmoid(g)
        dg = (dr_all * on * (sg * (1.0 + g * (1.0 - sg)))).astype(BF)
        don = dr_all * (g * sg)
        do = rs * (don - jnp.mean(don, axis=-1, keepdims=True) - on * jnp.mean(don * on, axis=-1, keepdims=True))
        do_b = do.astype(BF)
        dox_b = (do * tab(xi_ref)).astype(BF)
        zeta = tab(zeta_ref)
        kz = (jnp.concatenate(k, axis=0).astype(F32) * zeta).astype(BF)
        d_t = tab(dt_ref)
        da_b = (jnp.concatenate([_dot_nt(part(do_b, i), v[i]) for i in range(n)], axis=0) * tab(d_ref)).astype(BF)
        dat_b = (jnp.concatenate([_dot_nt(v[i], part(do_b, i)) for i in range(n)], axis=0) * d_t).astype(BF)
        mt_b = (jnp.concatenate([_dot_nt(k[i], q[i]) for i in range(n)], axis=0) * d_t).astype(BF)
        dq = [_dot(part(da_b, i), k[i]) + _dot_nt(part(dox_b, i), s_ref[pairs[i]]) for i in range(n)]
        dk1 = [_dot(part(dat_b, i), q[i]) for i in range(n)]
        dv1 = [_dot(part(mt_b, i), part(do_b, i)) for i in range(n)]
        qtd = [_dot_tn(q[i], part(dox_b, i)) for i in range(n)]
        gst_b = [None] * n
        for h in range(H):
            gst = gstate[h]
            for b in reversed(range(ncs)):
                i = b * H + h
                gst_b[i] = gst.astype(BF)
                gst = decay_ref[h] * gst + qtd[i]
            gstate[h] = gst
        dk2 = jnp.concatenate([_dot_nt(v[i], gst_b[i]) for i in range(n)], axis=0) * zeta
        dv = jnp.concatenate([dv1[i] + _dot(part(kz, i), gst_b[i]) for i in range(n)], axis=0).astype(BF)
        even = lax.broadcasted_iota(jnp.int32, (n * BLOCK, C), 1) % 2 == 0
        cos_t = jnp.concatenate([cos_ref[b * BLOCK:(b + 1) * BLOCK, :] for b, _ in pairs], axis=0)
        sin_t = jnp.concatenate([sin_ref[b * BLOCK:(b + 1) * BLOCK, :] for b, _ in pairs], axis=0)
        dq = jnp.concatenate(dq, axis=0)
        dk = jnp.concatenate(dk1, axis=0) + dk2
        dq = (dq * cos_t - _rot(dq, even) * sin_t).astype(BF)
        dk = ((dk * cos_t - _rot(dk, even) * sin_t) * RET_K_SCALE).astype(BF)
        for i, (b, h) in enumerate(pairs):
            rows = slice(b * BLOCK, (b + 1) * BLOCK)
            for j, x in enumerate((dq, dk, dv, dg)):
                dret_ref[rows, j * RET_W + h * C:j * RET_W + (h + 1) * C] = part(x, i)
        pl.when(pl.program_id(0) == steps - 1)(ex.wait)

    steps = nc // ncs
    rev = lambda w: pl.BlockSpec((BLOCK * ncs, w), lambda n: (steps - 1 - n, 0))
    tab = pl.BlockSpec((H, C, C), lambda n: (0, 0, 0))
    outs = _pcall(
        body, name="ret_bwd", grid=(steps,),
        in_specs=[_smem_full(), rev(RET_W), rev(RET_W), rev(RET_W), rev(RET_W), rev(RET_W),
                  pl.BlockSpec((ncs, H, C, C), lambda n: (steps - 1 - n, 0, 0, 0)), rev(RET_W), tab, tab, tab, tab,
                  rev(C), rev(C)] + [ANY_SPEC] * ns,
        out_specs=[rev(4 * RET_W)] + [ANY_SPEC] * ns,
        out_shape=[jax.ShapeDtypeStruct((T, 4 * RET_W), BF)] + _exchange_shapes([], swaps),
        scratch_shapes=[pltpu.VMEM((H, C, C), F32)] + _Exchange.scratch(ns),
        compiler_params=_params(("arbitrary",)),
    )(decay, qr, kr, vr, gr, o, states, dr, d_intra, d_intra_t, xi_b, zeta_b, cos, sin_s, *swaps)
    return outs[0], outs[1:]


def _attn_bwd_dq(sinks, qa, ka, va, da, lse, nbs, swaps):
    T = qa.shape[0]
    steps = T // (BLOCK * nbs)
    R = nbs * N_ATTN_HEADS * BLOCK
    ns = len(swaps)

    def body(sink_ref, q_ref, kc_ref, kp_ref, vc_ref, vp_ref, da_ref, lse_ref, *rest):
        ex = _exchange_of(rest[:ns], rest[ns + 3:2 * ns + 3], rest[2 * ns + 3:], 0)
        dq_ref, deltat_ref, dsink_ref = rest[ns:ns + 3]
        n = pl.program_id(0)

        @pl.when(n == 0)
        def _():
            ex.start()
            dsink_ref[...] = jnp.zeros_like(dsink_ref)

        kcat = _block_variants(kp_ref, kc_ref, nbs)
        vcat = _block_variants(vp_ref, vc_ref, nbs)
        tri1 = _tri(BLOCK)
        lane = lax.broadcasted_iota(jnp.int32, (BLOCK, BLOCK), 1)
        s_tiles, dp_tiles, lse_cols = [], [], []
        for b in range(nbs):
            rows = slice(b * BLOCK, (b + 1) * BLOCK)
            lse_tile = lse_ref[rows, :]
            for pair in range(N_ATTN_HEADS // 2):
                qp = _scaled(q_ref[rows, pair * 128:(pair + 1) * 128])
                dop = da_ref[rows, pair * 128:(pair + 1) * 128].astype(BF)
                for e in range(2):
                    s = _fold(_dot_nt(qp, kcat[b][pair // 2][e]), tri1, True)
                    if b == 0:
                        s = jnp.where(tri1 & (n == 0), MASK_VALUE, s)
                    s_tiles.append(s)
                    dp_tiles.append(_fold(_dot_nt(dop, vcat[b][pair // 2][e]), tri1, True))
                    lse_cols.append(jnp.sum(jnp.where(lane == 2 * pair + e, lse_tile, 0.0), axis=-1, keepdims=True))
        lse_c = jnp.concatenate(lse_cols, axis=0)
        p = jnp.exp(jnp.concatenate(s_tiles, axis=0) - lse_c)
        dp = jnp.concatenate(dp_tiles, axis=0)
        delta = jnp.sum(p * dp, axis=-1, keepdims=True)
        ds2 = _unfold(p * (dp - delta), _tri(R), True)
        for b in range(nbs):
            for pair in range(N_ATTN_HEADS // 2):
                r0 = (b * N_ATTN_HEADS + 2 * pair) * BLOCK
                acc = (_dot(ds2[r0:r0 + BLOCK, :], kcat[b][pair // 2][0])
                       + _dot(ds2[r0 + BLOCK:r0 + 2 * BLOCK, :], kcat[b][pair // 2][1]))
                dq_ref[b * BLOCK:(b + 1) * BLOCK, pair * 128:(pair + 1) * 128] = (acc * ATTN_SCALE).astype(BF)
        for b, t in enumerate(_head_cols(delta, nbs)):
            deltat_ref[:, b * BLOCK:(b + 1) * BLOCK] = t.T[:N_ATTN_HEADS, :]
        sink = jnp.concatenate([jnp.full((BLOCK, 1), sink_ref[head], F32)
                                for _ in range(nbs) for head in range(N_ATTN_HEADS)], axis=0)
        ds_sink = -jnp.exp(sink - lse_c) * delta
        row8 = lax.broadcasted_iota(jnp.int32, (N_ATTN_HEADS, BLOCK), 0)
        dsink = jnp.zeros((N_ATTN_HEADS, BLOCK), F32)
        for b in range(nbs):
            for head in range(N_ATTN_HEADS):
                r0 = (b * N_ATTN_HEADS + head) * BLOCK
                dsink = dsink + jnp.where(row8 == head, jnp.sum(ds_sink[r0:r0 + BLOCK, :], axis=0, keepdims=True), 0.0)
        dsink_ref[...] += dsink
        pl.when(n == steps - 1)(ex.wait)

    cur = lambda w: pl.BlockSpec((BLOCK * nbs, w), lambda n: (n, 0))
    prev = lambda w: pl.BlockSpec((BLOCK, w), lambda n: (jnp.maximum(n * nbs - 1, 0), 0))
    outs = _pcall(
        body, name="attn_bwd_dq", grid=(steps,),
        in_specs=[_smem_full(), cur(ATTN_W), cur(KV_W), prev(KV_W), cur(KV_W), prev(KV_W), cur(ATTN_W), cur(BLOCK)]
        + [ANY_SPEC] * ns,
        out_specs=[cur(ATTN_W), pl.BlockSpec((N_ATTN_HEADS, BLOCK * nbs), lambda n: (0, n)),
                   _const((N_ATTN_HEADS, BLOCK))] + [ANY_SPEC] * ns,
        out_shape=[jax.ShapeDtypeStruct((T, ATTN_W), BF), jax.ShapeDtypeStruct((N_ATTN_HEADS, T), F32),
                   jax.ShapeDtypeStruct((N_ATTN_HEADS, BLOCK), F32)] + _exchange_shapes([], swaps),
        scratch_shapes=_Exchange.scratch(ns),
        compiler_params=_params(("arbitrary",)),
    )(sinks, qa, ka, ka, va, va, da, lse, *swaps)
    return outs[:3], outs[3:]


def _attn_bwd_dkv(qa, ka, va, da, lse_t, delta_t, nbs):
    T = qa.shape[0]
    nb = T // BLOCK
    steps = nb // nbs
    R = nbs * N_ATTN_HEADS * BLOCK

    def body(qc_ref, qn_ref, dac_ref, dan_ref, k_ref, v_ref, lc_ref, ln_ref, dc_ref, dn_ref, dk_ref, dv_ref):
        n = pl.program_id(0)
        tri1 = _tri(BLOCK, True)
        lo = lax.broadcasted_iota(jnp.int32, (BLOCK, 128), 1) < HEAD_DIM
        kv = [_kv_variants(k_ref[b * BLOCK:(b + 1) * BLOCK, :]) for b in range(nbs)]
        vv = [_kv_variants(v_ref[b * BLOCK:(b + 1) * BLOCK, :]) for b in range(nbs)]
        qcat, docat = [], []
        s_tiles, dp_tiles, lse_tiles, delta_tiles = [], [], [], []
        for b in range(nbs):
            rows = slice(b * BLOCK, (b + 1) * BLOCK)
            nrows = slice((b + 1) * BLOCK, (b + 2) * BLOCK)
            inside = b < nbs - 1
            for pair in range(N_ATTN_HEADS // 2):
                ps = slice(pair * 128, (pair + 1) * 128)
                q2 = _scaled(jnp.concatenate([qc_ref[rows, ps], qc_ref[nrows, ps] if inside else qn_ref[:, ps]], axis=0))
                do2 = jnp.concatenate([dac_ref[rows, ps], dac_ref[nrows, ps] if inside else dan_ref[:, ps]],
                                      axis=0).astype(BF)
                qcat.append(q2)
                docat.append(do2)
                for e in range(2):
                    one = pl.ds(2 * pair + e, 1)
                    s = _fold(_dot_nt(kv[b][pair // 2][e], q2), tri1, False)
                    if not inside:
                        s = jnp.where(tri1 & (n == steps - 1), MASK_VALUE, s)
                    s_tiles.append(s)
                    dp_tiles.append(_fold(_dot_nt(vv[b][pair // 2][e], do2), tri1, False))
                    lse_tiles.append(jnp.where(tri1, lc_ref[one, nrows] if inside else ln_ref[one, :], lc_ref[one, rows]))
                    delta_tiles.append(jnp.where(tri1, dc_ref[one, nrows] if inside else dn_ref[one, :],
                                                 dc_ref[one, rows]))
        pt = jnp.exp(jnp.concatenate(s_tiles, axis=0) - jnp.concatenate(lse_tiles, axis=0))
        dst = pt * (jnp.concatenate(dp_tiles, axis=0) - jnp.concatenate(delta_tiles, axis=0))
        tri = _tri(R, True)
        pt2 = _unfold(pt, tri, False)
        dst2 = _unfold(dst, tri, False)
        for b in range(nbs):
            dk = jnp.zeros((BLOCK, 128), F32)
            dv = jnp.zeros((BLOCK, 128), F32)
            for pair in range(N_ATTN_HEADS // 2):
                h = pair // 2
                for e in range(2):
                    r0 = (b * N_ATTN_HEADS + 2 * pair + e) * BLOCK
                    half = lo if e == 0 else jnp.logical_not(lo)
                    dv_e = jnp.where(half, _dot(pt2[r0:r0 + BLOCK, :], docat[b * 4 + pair]), 0.0)
                    dk_e = jnp.where(half, _dot(dst2[r0:r0 + BLOCK, :], qcat[b * 4 + pair]), 0.0)
                    if e != h:
                        dv_e = pltpu.roll(dv_e, HEAD_DIM, 1)
                        dk_e = pltpu.roll(dk_e, HEAD_DIM, 1)
                    dv = dv + dv_e
                    dk = dk + dk_e
            dk_ref[b * BLOCK:(b + 1) * BLOCK, :] = dk.astype(BF)
            dv_ref[b * BLOCK:(b + 1) * BLOCK, :] = dv.astype(BF)

    cur = lambda w: pl.BlockSpec((BLOCK * nbs, w), lambda n: (n, 0))
    nxt = lambda w: pl.BlockSpec((BLOCK, w), lambda n: (jnp.minimum((n + 1) * nbs, nb - 1), 0))
    tcur = pl.BlockSpec((N_ATTN_HEADS, BLOCK * nbs), lambda n: (0, n))
    tnxt = pl.BlockSpec((N_ATTN_HEADS, BLOCK), lambda n: (0, jnp.minimum((n + 1) * nbs, nb - 1)))
    return _pcall(
        body, name="attn_bwd_dkv", grid=(steps,),
        in_specs=[cur(ATTN_W), nxt(ATTN_W), cur(ATTN_W), nxt(ATTN_W), cur(KV_W), cur(KV_W), tcur, tnxt, tcur, tnxt],
        out_specs=[cur(KV_W), cur(KV_W)],
        out_shape=[jax.ShapeDtypeStruct((T, KV_W), BF), jax.ShapeDtypeStruct((T, KV_W), BF)],
        compiler_params=_params(("parallel",)),
    )(qa, qa, da, da, ka, va, lse_t, lse_t, delta_t, delta_t)


def _in_proj_bwd(dqa, dka, dva, dret, w_in, x, g1, dx1, tm):
    T = x.shape[0]

    def body(dqa_ref, dka_ref, dva_ref, dret_ref, w_ref, x_ref, g_ref, dx1_ref, dx_ref, dg1_ref):
        @pl.when(pl.program_id(0) == 0)
        def _():
            dg1_ref[...] = jnp.zeros_like(dg1_ref)

        dh = (_dot_nt(dqa_ref[...], w_ref[:, QA0:QA0 + ATTN_W]) + _dot_nt(dka_ref[...], w_ref[:, KA0:KA0 + KV_W])
              + _dot_nt(dva_ref[...], w_ref[:, VA0:VA0 + KV_W]) + _dot_nt(dret_ref[...], w_ref[:, QR0:IN_W]))
        r, n = _rms_stats(x_ref[...])
        dg1_ref[...] += jnp.sum(dh * n, axis=0, keepdims=True)
        dx_ref[...] = dx1_ref[...] + _rms_bwd(n, r, dh * g_ref[...])

    return _pcall(
        body, name="in_proj_bwd", grid=(T // tm,),
        in_specs=[_rows(tm, ATTN_W), _rows(tm, KV_W), _rows(tm, KV_W), _rows(tm, 4 * RET_W), _vmem_full(),
                  _rows(tm, D_MODEL), _const((1, D_MODEL)), _rows(tm, D_MODEL)],
        out_specs=[_rows(tm, D_MODEL), _const((1, D_MODEL))],
        out_shape=[jax.ShapeDtypeStruct((T, D_MODEL), F32), jax.ShapeDtypeStruct((1, D_MODEL), F32)],
        compiler_params=_params(("arbitrary",)),
    )(dqa, dka, dva, dret, w_in, x, g1, dx1)


def _wgrad(a_list, b_list, tk, name):
    T = a_list[0].shape[0]
    na, nbb = len(a_list), len(b_list)
    m_sizes = [a.shape[1] for a in a_list]
    n_sizes = [b.shape[1] for b in b_list]
    M, N = sum(m_sizes), sum(n_sizes)
    nk = T // tk
    chunk = 512

    def body(*refs):
        a_refs, b_refs = refs[:na], refs[na:na + nbb]
        out_ref, acc = refs[na + nbb], refs[na + nbb + 1]
        k = pl.program_id(0)

        @pl.when(k == 0)
        def _():
            acc[...] = jnp.zeros_like(acc)

        r0 = 0
        for ai in range(na):
            a = a_refs[ai][...]
            c0 = 0
            for bi in range(nbb):
                for s in range(0, n_sizes[bi], chunk):
                    w = min(chunk, n_sizes[bi] - s)
                    acc[r0:r0 + m_sizes[ai], c0 + s:c0 + s + w] += _dot_tn(a, b_refs[bi][:, s:s + w])
                c0 += n_sizes[bi]
            r0 += m_sizes[ai]

        @pl.when(k == nk - 1)
        def _():
            pltpu.sync_copy(acc, out_ref)

    return _pcall(
        body, name=name, grid=(nk,),
        in_specs=[_rows(tk, w) for w in m_sizes + n_sizes],
        out_specs=pl.BlockSpec(memory_space=pl.ANY),
        out_shape=jax.ShapeDtypeStruct((M, N), F32),
        scratch_shapes=[pltpu.VMEM((M, N), F32)],
        compiler_params=_params(("arbitrary",)),
    )(*a_list, *b_list)


def _adamw_math(w, g, m, v):
    m = ADAM_B1 * m + (1.0 - ADAM_B1) * g
    v = ADAM_B2 * v + (1.0 - ADAM_B2) * (g * g)
    m_hat = m / (1.0 - ADAM_B1 ** ADAM_STEP)
    v_hat = v / (1.0 - ADAM_B2 ** ADAM_STEP)
    delta = -ADAM_LR * (m_hat / (jnp.sqrt(v_hat) + ADAM_EPS) + ADAM_WD * w)
    return delta, m, v


def _sum_parts(parts_ref):
    g = parts_ref[0].astype(F32)
    for i in range(1, N_DEV):
        g = g + parts_ref[i].astype(F32)
    return g


def _adamw_shard(parts, w, m, v, tr, name):
    R, C = w.shape

    def body(p_ref, w_ref, m_ref, v_ref, g_ref, d_ref, nm_ref, nv_ref):
        g = _sum_parts(p_ref)
        g_ref[...] = g
        d_ref[...], nm_ref[...], nv_ref[...] = _adamw_math(w_ref[...], g, m_ref[...], v_ref[...])

    blk = pl.BlockSpec((tr, C), lambda i: (i, 0))
    return _pcall(
        body, name=name, grid=(R // tr,),
        in_specs=[pl.BlockSpec((N_DEV, tr, C), lambda i: (0, i, 0)), blk, blk, blk],
        out_specs=[blk] * 4,
        out_shape=[jax.ShapeDtypeStruct((R, C), F32)] * 4,
        compiler_params=_params(("parallel",)),
    )(parts, w, m, v)


def _sum_small(parts):
    def body(p_ref, g_ref):
        g_ref[...] = _sum_parts(p_ref)

    return _pcall(body, name="sum_small", out_shape=jax.ShapeDtypeStruct(parts.shape[1:], F32),
                  in_specs=[_vmem_full()], out_specs=_vmem_full())(parts)


def _adamw_small(g, w, m, v, name):
    def body(g_ref, w_ref, m_ref, v_ref, d_ref, nm_ref, nv_ref):
        d_ref[...], nm_ref[...], nv_ref[...] = _adamw_math(w_ref[...], g_ref[...], m_ref[...], v_ref[...])

    return _pcall(body, name=name, out_shape=[jax.ShapeDtypeStruct(w.shape, F32)] * 3,
                  in_specs=[_vmem_full()] * 4, out_specs=[_vmem_full()] * 3)(g, w, m, v)


def _tables(T):
    h, c = N_RET_HEADS, BLOCK
    pos = jnp.arange(T, dtype=F32)
    angle = 1.0 / jnp.power(10000.0, jnp.linspace(0.0, 1.0, RET_HEAD_DIM // 2, dtype=F32))
    angle = jnp.repeat(angle, 2)
    sin = jnp.sin(pos[:, None] * angle[None])
    cos = jnp.cos(pos[:, None] * angle[None])
    even = (jnp.arange(RET_HEAD_DIM) % 2 == 0)[None, :]
    sin_s = jnp.where(even, -sin, sin)
    log_gamma = jnp.log(1.0 - jnp.power(2.0, -5.0 - jnp.arange(h, dtype=F32)))
    idx = jnp.arange(c, dtype=F32)
    rel = idx[:, None] - idx[None, :]
    d_intra = jnp.where(rel[None] >= 0, jnp.exp(log_gamma[:, None, None] * jnp.maximum(rel, 0.0)[None]), 0.0)
    xi = jnp.exp(log_gamma[None, :] * (idx[:, None] + 1.0))
    zeta = jnp.exp(log_gamma[None, :] * (c - 1.0 - idx[:, None]))
    decay = jnp.exp(log_gamma * c)
    xi_b = jnp.broadcast_to(xi.T[:, :, None], (h, c, RET_HEAD_DIM))
    zeta_b = jnp.broadcast_to(zeta.T[:, :, None], (h, c, RET_HEAD_DIM))
    return cos, sin_s, d_intra, jnp.swapaxes(d_intra, 1, 2), xi_b, zeta_b, decay


def _to_shards(full, cols):
    r = full.shape[0]
    return jnp.swapaxes(full.reshape(r, N_DEV, cols), 0, 1)


def _from_shards(sh):
    n, r, cols = sh.shape
    return jnp.swapaxes(sh, 0, 1).reshape(r, n * cols)


SMALL_ROWS = 216


def _pack_small(gains, conv_b, conv_w, sinks):
    parts = [g.reshape(8, 128) for g in gains] + [conv_b.reshape(44, 128), conv_w.reshape(132, 128),
                                                  jnp.pad(sinks.reshape(1, 8), ((0, 0), (0, 120)))]
    packed = jnp.concatenate(parts, axis=0)
    return jnp.pad(packed, ((0, SMALL_ROWS - packed.shape[0]), (0, 0)))


def kernel(x, mix_pre_norm, w_in, attn_sinks, w_out, mix_post_norm, ffn_pre_norm, w_up, conv_w, conv_b, w_down, ffn_post_norm, loss_target, m_mix_pre_norm, m_w_in, m_attn_sinks, m_w_out, m_mix_post_norm, m_ffn_pre_norm, m_w_up, m_conv_w, m_conv_b, m_w_down, m_ffn_post_norm, v_mix_pre_norm, v_w_in, v_attn_sinks, v_w_out, v_mix_post_norm, v_ffn_pre_norm, v_w_up, v_conv_w, v_conv_b, v_w_down, v_ffn_post_norm):
    T = x.shape[1]
    tm = min(512, T)
    tm_big = min(1024, T)
    tk_grad = min(2048, T)
    nbs = min(8, T // BLOCK)
    x2 = x.reshape(T, D_MODEL)
    target = loss_target.reshape(T, D_MODEL)
    me = 4 * lax.axis_index("x") + 2 * lax.axis_index("y") + lax.axis_index("c")

    g_in, g_cw = _exchange_call([w_in[0].astype(BF), conv_w[0]], [], "gather_w_in")
    w_in_f = _from_shards(g_in)
    cos, sin_s, d_intra, d_intra_t, xi_b, zeta_b, decay = _tables(T)
    sinks = attn_sinks.reshape(N_ATTN_HEADS)

    (h1, qa, ka, va, qr, kr, vr, gr), (w_up8,) = _in_proj(
        x2, mix_pre_norm, w_in_f, cos, sin_s, tm_big, [w_up[0].astype(BF)])
    (a, lse, lse_t), (g_down, g_out) = _attn_fwd(sinks, qa, ka, va, nbs,
                                                 [w_down[0].astype(BF), w_out[0].astype(BF)])
    w_out_f = g_out.reshape(D_MODEL, D_MODEL)
    o, states, r = _ret_fwd(decay, qr, kr, vr, gr, d_intra, xi_b, zeta_b, nbs)
    mixed, x1, h2 = _out_proj(a, r, w_out_f, x2, mix_post_norm, ffn_pre_norm, tm_big)
    w_down4 = g_down.reshape(FF_PAIRS, FF_SHARD, D_MODEL)
    up_g, up_v, u_g, u_v, y4, dout, dz, dg4, loss_acc = _ffn_fwd(
        h2, w_up8, g_cw, conv_b.reshape(N_DEV, 1, FF_SHARD), w_down4, x1, ffn_post_norm, target, tm)
    loss = lax.psum(loss_acc[0, 0], ("x", "y", "c"))

    dup_g, dup_v, dcb_g, dcb_v, dcw_g, dcw_v, gwu_g, gwu_v, gw_down4 = _ffn_bwd_a(
        dz, h2, w_down4, u_g, u_v, up_g, up_v, y4, g_cw, tm_big)
    dcb = jnp.concatenate([dcb_g, dcb_v], axis=0).reshape(1, 2 * D_FF)
    dcw = _from_shards(jnp.concatenate([dcw_g, dcw_v], axis=0))
    gw_down = gw_down4.reshape(D_FF, D_MODEL)
    (dx1, dmixed, da, dr, dg3, dg2), p_up = _ffn_bwd_b(
        dup_g, dup_v, w_up8, x1, dout, ffn_pre_norm, mixed, mix_post_norm, w_out_f, tm, gwu_g, gwu_v)
    gw_out = _wgrad([a, r], [dmixed], tk_grad, "wgrad_out")
    dret, (p_down,) = _ret_bwd(decay, qr, kr, vr, gr, o, states, dr, d_intra, d_intra_t, xi_b, zeta_b, cos, sin_s,
                               [gw_down.reshape(N_DEV, D_FF // N_DEV, D_MODEL)], nbs)
    (dqa, delta_t, dsink), (p_out,) = _attn_bwd_dq(sinks, qa, ka, va, da, lse, nbs,
                                                   [gw_out.reshape(N_DEV, D_MODEL // N_DEV, D_MODEL)])
    dka, dva = _attn_bwd_dkv(qa, ka, va, da, lse_t, delta_t, nbs)
    grad_x, dg1 = _in_proj_bwd(dqa, dka, dva, dret, w_in_f, x2, mix_pre_norm, dx1, tm)
    gw_in = _wgrad([h1], [dqa, dka, dva, dret], tk_grad, "wgrad_in")

    small = _pack_small([dg1, dg2, dg3, dg4], dcb, dcw, dsink[:, 0])
    small_all, p_in = _exchange_call([small], [_to_shards(gw_in, IN_W // N_DEV).astype(BF)], "exchange_last")
    g_small = _sum_small(small_all)

    g_w_in, d_w_in, nm_w_in, nv_w_in = _adamw_shard(p_in, w_in[0], m_w_in[0], v_w_in[0], 256, "adamw_in")
    g_w_up, d_w_up, nm_w_up, nv_w_up = _adamw_shard(p_up, w_up[0], m_w_up[0], v_w_up[0], 256, "adamw_up")
    g_w_out, d_w_out, nm_w_out, nv_w_out = _adamw_shard(p_out, w_out[0], m_w_out[0], v_w_out[0], 128, "adamw_out")
    g_w_down, d_w_down, nm_w_down, nv_w_down = _adamw_shard(p_down, w_down[0], m_w_down[0], v_w_down[0], 176,
                                                            "adamw_down")
    gains = [mix_pre_norm, mix_post_norm, ffn_pre_norm, ffn_post_norm]
    m_gains = [m_mix_pre_norm, m_mix_post_norm, m_ffn_pre_norm, m_ffn_post_norm]
    v_gains = [v_mix_pre_norm, v_mix_post_norm, v_ffn_pre_norm, v_ffn_post_norm]
    zeros_cw = jnp.zeros((3, 2 * D_FF), F32)
    w_small = _pack_small(gains, conv_b, zeros_cw, attn_sinks)
    m_small = _pack_small(m_gains, m_conv_b, zeros_cw, m_attn_sinks)
    v_small = _pack_small(v_gains, v_conv_b, zeros_cw, v_attn_sinks)
    d_small, nm_small, nv_small = _adamw_small(g_small, w_small, m_small, v_small, "adamw_small")
    shard_cols = 2 * D_FF // N_DEV
    g_cw = lax.dynamic_slice(g_small[76:208].reshape(3, 2 * D_FF), (0, me * shard_cols), (3, shard_cols))
    d_cw, nm_cw, nv_cw = _adamw_small(g_cw, conv_w[0], m_conv_w[0], v_conv_w[0], "adamw_conv_w")

    def unpack(p):
        gains_o = [p[8 * i:8 * i + 8].reshape(1, D_MODEL) for i in range(4)]
        return gains_o, p[32:76].reshape(1, 2 * D_FF), p[208:209, :N_ATTN_HEADS]

    def leaves(p, w_in_s, w_out_s, w_up_s, cw_s, w_down_s):
        (pre1, post1, pre2, post2), cb, sk = unpack(p)
        return [pre1, w_in_s[None], sk, w_out_s[None], post1, pre2, w_up_s[None], cw_s[None], cb, w_down_s[None],
                post2]

    return (loss, grad_x.reshape(1, T, D_MODEL),
            *leaves(g_small, g_w_in, g_w_out, g_w_up, g_cw, g_w_down),
            *leaves(d_small, d_w_in, d_w_out, d_w_up, d_cw, d_w_down),
            *leaves(nm_small, nm_w_in, nm_w_out, nm_w_up, nm_cw, nm_w_down),
            *leaves(nv_small, nv_w_in, nv_w_out, nv_w_up, nv_cw, nv_w_down))
```

```python
import functools
import math

import jax
import jax.numpy as jnp
from jax import lax
from jax.experimental import pallas as pl
from jax.experimental.pallas import tpu as pltpu

F32 = jnp.float32
BF = jnp.bfloat16

N_DEV = 8
D_MODEL = 1024
HEAD_DIM = 64
ATTN_W = 512
N_ATTN_HEADS = 8
KV_W = 128
BLOCK = 128
RET_W = 512
N_RET_HEADS = 4
RET_HEAD_DIM = 128
IN_W = 2816
D_FF = 2816
RMS_EPS = 1e-6
GN_EPS = 1e-6
MASK_VALUE = -1e30
ATTN_SCALE = HEAD_DIM ** -0.5
RET_K_SCALE = RET_HEAD_DIM ** -0.5
GELU_C = math.sqrt(2.0 / math.pi)
GELU_A = 0.044715

ADAM_LR = 0.001
ADAM_B1 = 0.9
ADAM_B2 = 0.999
ADAM_EPS = 1e-08
ADAM_WD = 0.01
ADAM_STEP = 10

VMEM_LIMIT_BYTES = 56 * 1024 * 1024
FF_SHARD = 2 * D_FF // N_DEV
FF_PAIRS = N_DEV // 2

QA0, KA0, VA0, QR0, KR0, VR0, GR0 = 0, 512, 640, 768, 1280, 1792, 2304

MESH_ID = pl.DeviceIdType.MESH


def _pcall(body, **kw):
    return pl.pallas_call(body, **kw)


def _params(sem=None):
    return pltpu.CompilerParams(dimension_semantics=sem, vmem_limit_bytes=VMEM_LIMIT_BYTES)


def _dot(a, b):
    return jnp.dot(a, b, preferred_element_type=F32)


def _dot_nt(a, b):
    return lax.dot_general(a, b, (((1,), (1,)), ((), ())), preferred_element_type=F32)


def _dot_tn(a, b):
    return lax.dot_general(a, b, (((0,), (0,)), ((), ())), preferred_element_type=F32)


def _vmem_full():
    return pl.BlockSpec(memory_space=pltpu.VMEM)


def _smem_full():
    return pl.BlockSpec(memory_space=pltpu.SMEM)


def _rows(tm, w):
    return pl.BlockSpec((tm, w), lambda i: (i, 0))


def _const(shape):
    return pl.BlockSpec(shape, lambda i: tuple(0 for _ in shape))


def _rms_stats(x):
    r = lax.rsqrt(jnp.mean(x * x, axis=-1, keepdims=True) + RMS_EPS)
    return r, x * r


def _rms_bwd(n, r, dn):
    return r * (dn - n * jnp.mean(dn * n, axis=-1, keepdims=True))


def _rot(x, even):
    w = x.shape[1]
    return jnp.where(even, pltpu.roll(x, w - 1, 1), pltpu.roll(x, 1, 1))


def _peers():
    x, y, c = lax.axis_index("x"), lax.axis_index("y"), lax.axis_index("c")
    flips = [(0, 0, 1), (1, 0, 0), (0, 1, 0), (1, 1, 0), (1, 0, 1), (0, 1, 1), (1, 1, 1)]
    peers = [(x ^ fx, y ^ fy, c ^ fc) for fx, fy, fc in flips]
    return 4 * x + 2 * y + c, peers


SAME_CORE_PEERS = 4


class _Exchange:
    def __init__(self, gathers, swaps, send_sems, recv_sems, local_sems):
        self.me, self.peers = _peers()
        self.slots = [4 * px + 2 * py + pc for px, py, pc in self.peers]
        self.pairs = [(src, dst, True) for src, dst in gathers] + [(src, dst, False) for src, dst in swaps]
        self.send_sems, self.recv_sems, self.local_sems = send_sems, recv_sems, local_sems

    @staticmethod
    def scratch(n):
        return [pltpu.SemaphoreType.DMA((n, N_DEV - 1)), pltpu.SemaphoreType.DMA((n, N_DEV - 1)),
                pltpu.SemaphoreType.DMA((n,))]

    def _parts(self, a, slot):
        src, _, whole = self.pairs[a]
        half = N_DEV // 2
        if whole:
            return [(None, src)]
        if isinstance(src, tuple):
            return [(slot < half, src[0].at[jnp.minimum(slot, half - 1)]),
                    (slot >= half, src[1].at[jnp.maximum(slot - half, 0)])]
        return [(None, src.at[slot])]

    def _local(self, a, src):
        return pltpu.make_async_copy(src, self.pairs[a][1].at[self.me], self.local_sems.at[a])

    def _remote(self, a, k, src, slot):
        return pltpu.make_async_remote_copy(
            src_ref=src, dst_ref=self.pairs[a][1].at[slot], send_sem=self.send_sems.at[a, k],
            recv_sem=self.recv_sems.at[a, k], device_id=self.peers[k], device_id_type=MESH_ID)

    def start(self):
        def go(cond, copy):
            if cond is None:
                copy.start()
            else:
                pl.when(cond)(copy.start)

        for a in range(len(self.pairs)):
            for cond, src in self._parts(a, self.me):
                go(cond, self._local(a, src))
            for k in range(SAME_CORE_PEERS if self.pairs[a][2] else N_DEV - 1):
                for cond, src in self._parts(a, self.slots[k]):
                    go(cond, self._remote(a, k, src, self.me))

    def _pass_on(self, a, j):
        k = j + SAME_CORE_PEERS - 1
        block = self.pairs[a][1].at[self.slots[j]]
        return pltpu.make_async_remote_copy(
            src_ref=block, dst_ref=block, send_sem=self.send_sems.at[a, k], recv_sem=self.recv_sems.at[a, k],
            device_id=self.peers[0], device_id_type=MESH_ID)

    def wait(self):
        for a in range(len(self.pairs)):
            src = self._parts(a, self.me)[0][1]
            if self.pairs[a][2]:
                for j in range(1, SAME_CORE_PEERS):
                    self._remote(a, j, src, self.slots[j]).wait_recv()
                    self._pass_on(a, j).start()
                self._remote(a, 0, src, self.slots[0]).wait_recv()
            for k in range(SAME_CORE_PEERS if self.pairs[a][2] else 0, N_DEV - 1):
                self._remote(a, k, src, self.slots[k]).wait_recv()
        for a in range(len(self.pairs)):
            src = self._parts(a, self.me)[0][1]
            for k in range(SAME_CORE_PEERS if self.pairs[a][2] else N_DEV - 1):
                self._remote(a, k, src, self.me).wait_send()
            if self.pairs[a][2]:
                for j in range(1, SAME_CORE_PEERS):
                    self._pass_on(a, j).wait_send()
            self._local(a, src).wait()


ANY_SPEC = pl.BlockSpec(memory_space=pl.ANY)


def _exchange_shapes(gathers, swaps):
    return ([jax.ShapeDtypeStruct((N_DEV,) + a.shape, a.dtype) for a in gathers]
            + [jax.ShapeDtypeStruct(a.shape, a.dtype) for a in swaps])


def _exchange_of(ins, outs, sems, ng):
    return _Exchange(list(zip(ins[:ng], outs[:ng])), list(zip(ins[ng:], outs[ng:])), *sems)


def _exchange_call(gathers, swaps, name):
    ng, ns = len(gathers), len(swaps)
    n = ng + ns

    def body(*refs):
        ex = _exchange_of(refs[:n], refs[n:2 * n], refs[2 * n:], ng)
        ex.start()
        ex.wait()

    return _pcall(
        body, name=name, out_shape=_exchange_shapes(gathers, swaps),
        in_specs=[ANY_SPEC] * (ng + ns), out_specs=[ANY_SPEC] * (ng + ns),
        scratch_shapes=_Exchange.scratch(ng + ns),
    )(*gathers, *swaps)


def _in_proj(x, g1, w_in, cos, sin_s, tm, gathers):
    T = x.shape[0]
    ng = len(gathers)
    nt = T // tm

    def body(x_ref, g_ref, w_ref, cos_ref, sin_ref, *rest):
        ex = _exchange_of(rest[:ng], rest[ng + 8:2 * ng + 8], rest[2 * ng + 8:], ng)
        h_ref, qa_ref, ka_ref, va_ref, qr_ref, kr_ref, vr_ref, gr_ref = rest[ng:ng + 8]
        pl.when(pl.program_id(0) == 0)(ex.start)
        r, n = _rms_stats(x_ref[...])
        h = (n * g_ref[...]).astype(BF)
        h_ref[...] = h

        def proj(c0, w):
            return _dot(h, w_ref[:, c0:c0 + w])

        qa_ref[...] = proj(QA0, ATTN_W).astype(BF)
        kva = proj(KA0, 2 * KV_W)
        ka_ref[...] = kva[:, :KV_W].astype(BF)
        va_ref[...] = kva[:, KV_W:].astype(BF)
        vr_ref[...] = proj(VR0, RET_W).astype(BF)
        gr_ref[...] = proj(GR0, RET_W)
        cos_t, sin_t = cos_ref[...], sin_ref[...]
        even = lax.broadcasted_iota(jnp.int32, (tm, RET_HEAD_DIM), 1) % 2 == 0
        for c0, scale, out_ref in ((QR0, None, qr_ref), (KR0, RET_K_SCALE, kr_ref)):
            full = proj(c0, RET_W)
            for hd in range(N_RET_HEADS):
                cs = slice(hd * RET_HEAD_DIM, (hd + 1) * RET_HEAD_DIM)
                t = full[:, cs] if scale is None else full[:, cs] * scale
                out_ref[:, cs] = (t * cos_t + _rot(t, even) * sin_t).astype(BF)
        pl.when(pl.program_id(0) == nt - 1)(ex.wait)

    widths = [D_MODEL, ATTN_W, KV_W, KV_W, RET_W, RET_W, RET_W, RET_W]
    dts = [BF] * 7 + [F32]
    outs = _pcall(
        body, name="in_proj", grid=(nt,),
        in_specs=[_rows(tm, D_MODEL), _const((1, D_MODEL)), _vmem_full(), _rows(tm, RET_HEAD_DIM),
                  _rows(tm, RET_HEAD_DIM)] + [ANY_SPEC] * ng,
        out_specs=[_rows(tm, w) for w in widths] + [ANY_SPEC] * ng,
        out_shape=[jax.ShapeDtypeStruct((T, w), dt) for w, dt in zip(widths, dts)] + _exchange_shapes(gathers, []),
        scratch_shapes=_Exchange.scratch(ng),
        compiler_params=_params(("arbitrary",)),
    )(x, g1, w_in, cos, sin_s, *gathers)
    return outs[:8], outs[8:]


def _kv_variants(kk):
    kf = kk.astype(F32)
    lo = lax.broadcasted_iota(jnp.int32, kf.shape, 1) < HEAD_DIM
    h0_lo = jnp.where(lo, kf, 0.0)
    h1_hi = jnp.where(lo, 0.0, kf)
    h0_hi = pltpu.roll(h0_lo, HEAD_DIM, 1)
    h1_lo = pltpu.roll(h1_hi, HEAD_DIM, 1)
    return [[h0_lo.astype(BF), h0_hi.astype(BF)], [h1_lo.astype(BF), h1_hi.astype(BF)]]


def _col_to_tile(tile, col, head):
    lane = lax.broadcasted_iota(jnp.int32, tile.shape, 1)
    return jnp.where(lane == head, col, tile)


def _tri(rows, key_major=False):
    i = lax.broadcasted_iota(jnp.int32, (rows, BLOCK), 0) & (BLOCK - 1)
    j = lax.broadcasted_iota(jnp.int32, (rows, BLOCK), 1)
    return i > j if key_major else j > i


def _fold(x2, tri, first_above):
    a, b = x2[:, :BLOCK], x2[:, BLOCK:]
    return jnp.where(tri, a, b) if first_above else jnp.where(tri, b, a)


def _unfold(x, tri, first_above):
    up, low = jnp.where(tri, x, 0.0), jnp.where(tri, 0.0, x)
    return jnp.concatenate([up, low] if first_above else [low, up], axis=1).astype(BF)


def _scaled(q):
    return (q.astype(F32) * ATTN_SCALE).astype(BF)


def _cat_variants(prev, cur):
    return [[jnp.concatenate([prev[h][e], cur[h][e]], axis=0) for e in range(2)] for h in range(2)]


def _block_variants(prev_ref, cur_ref, nbs):
    var = [_kv_variants(prev_ref[...])] + [_kv_variants(cur_ref[b * BLOCK:(b + 1) * BLOCK, :]) for b in range(nbs)]
    return [_cat_variants(var[b], var[b + 1]) for b in range(nbs)]


def _head_cols(col, nbs):
    tiles = []
    for b in range(nbs):
        t = jnp.zeros((BLOCK, BLOCK), F32)
        for head in range(N_ATTN_HEADS):
            r0 = (b * N_ATTN_HEADS + head) * BLOCK
            t = _col_to_tile(t, col[r0:r0 + BLOCK, :], head)
        tiles.append(t)
    return tiles


def _attn_fwd(sinks, qa, ka, va, nbs, gathers):
    T = qa.shape[0]
    steps = T // (BLOCK * nbs)
    R = nbs * N_ATTN_HEADS * BLOCK
    ng = len(gathers)

    def body(sink_ref, q_ref, kc_ref, kp_ref, vc_ref, vp_ref, *rest):
        ex = _exchange_of(rest[:ng], rest[ng + 3:2 * ng + 3], rest[2 * ng + 3:], ng)
        a_ref, lse_ref, lset_ref = rest[ng:ng + 3]
        n = pl.program_id(0)
        pl.when(n == 0)(ex.start)
        kcat = _block_variants(kp_ref, kc_ref, nbs)
        vcat = _block_variants(vp_ref, vc_ref, nbs)
        tri1 = _tri(BLOCK)
        tiles = []
        for b in range(nbs):
            for pair in range(N_ATTN_HEADS // 2):
                qp = _scaled(q_ref[b * BLOCK:(b + 1) * BLOCK, pair * 128:(pair + 1) * 128])
                for e in range(2):
                    s = _fold(_dot_nt(qp, kcat[b][pair // 2][e]), tri1, True)
                    if b == 0:
                        s = jnp.where(tri1 & (n == 0), MASK_VALUE, s)
                    tiles.append(s)
        s = jnp.concatenate(tiles, axis=0)
        sink = jnp.concatenate([jnp.full((BLOCK, 1), sink_ref[head], F32)
                                for _ in range(nbs) for head in range(N_ATTN_HEADS)], axis=0)
        m = jnp.maximum(jnp.max(s, axis=-1, keepdims=True), sink)
        p = jnp.exp(s - m)
        z = jnp.sum(p, axis=-1, keepdims=True) + jnp.exp(sink - m)
        p2 = _unfold(p * (1.0 / z), _tri(R), True)
        for b in range(nbs):
            for pair in range(N_ATTN_HEADS // 2):
                r0 = (b * N_ATTN_HEADS + 2 * pair) * BLOCK
                acc = (_dot(p2[r0:r0 + BLOCK, :], vcat[b][pair // 2][0])
                       + _dot(p2[r0 + BLOCK:r0 + 2 * BLOCK, :], vcat[b][pair // 2][1]))
                a_ref[b * BLOCK:(b + 1) * BLOCK, pair * 128:(pair + 1) * 128] = acc.astype(BF)
        for b, t in enumerate(_head_cols(m + jnp.log(z), nbs)):
            lse_ref[b * BLOCK:(b + 1) * BLOCK, :] = t
            lset_ref[:, b * BLOCK:(b + 1) * BLOCK] = t.T[:N_ATTN_HEADS, :]
        pl.when(n == steps - 1)(ex.wait)

    cur = lambda w: pl.BlockSpec((BLOCK * nbs, w), lambda n: (n, 0))
    prev = lambda w: pl.BlockSpec((BLOCK, w), lambda n: (jnp.maximum(n * nbs - 1, 0), 0))
    outs = _pcall(
        body, name="attn_fwd", grid=(steps,),
        in_specs=[_smem_full(), cur(ATTN_W), cur(KV_W), prev(KV_W), cur(KV_W), prev(KV_W)] + [ANY_SPEC] * ng,
        out_specs=[cur(ATTN_W), cur(BLOCK), pl.BlockSpec((N_ATTN_HEADS, BLOCK * nbs), lambda n: (0, n))]
        + [ANY_SPEC] * ng,
        out_shape=[jax.ShapeDtypeStruct((T, ATTN_W), BF), jax.ShapeDtypeStruct((T, BLOCK), F32),
                   jax.ShapeDtypeStruct((N_ATTN_HEADS, T), F32)] + _exchange_shapes(gathers, []),
        scratch_shapes=_Exchange.scratch(ng),
        compiler_params=_params(("arbitrary",)),
    )(sinks, qa, ka, ka, va, va, *gathers)
    return outs[:3], outs[3:]


def _ret_fwd(decay, qr, kr, vr, gr, d_intra, xi_b, zeta_b, ncs):
    T = qr.shape[0]
    nc = T // BLOCK
    H, C = N_RET_HEADS, RET_HEAD_DIM

    def body(decay_ref, q_ref, k_ref, v_ref, g_ref, d_ref, xi_ref, zeta_ref, o_ref, s_ref, r_ref, state):
        @pl.when(pl.program_id(0) == 0)
        def _():
            state[...] = jnp.zeros_like(state)

        pairs = [(b, h) for b in range(ncs) for h in range(H)]
        sl = lambda b, h: (slice(b * BLOCK, (b + 1) * BLOCK), slice(h * C, (h + 1) * C))
        tab = lambda ref: jnp.concatenate([ref[h] for _, h in pairs], axis=0)
        q = [q_ref[sl(b, h)] for b, h in pairs]
        k = [k_ref[sl(b, h)] for b, h in pairs]
        v = [v_ref[sl(b, h)] for b, h in pairs]
        inner = (jnp.concatenate([_dot_nt(q[i], k[i]) for i in range(len(pairs))], axis=0) * tab(d_ref)).astype(BF)
        kz = (jnp.concatenate(k, axis=0).astype(F32) * tab(zeta_ref)).astype(BF)
        o1 = [_dot(inner[i * BLOCK:(i + 1) * BLOCK, :], v[i]) for i in range(len(pairs))]
        kv = [_dot_tn(kz[i * BLOCK:(i + 1) * BLOCK, :], v[i]) for i in range(len(pairs))]
        st_b = [None] * len(pairs)
        for h in range(H):
            st = state[h]
            for b in range(ncs):
                i = b * H + h
                st_b[i] = st.astype(BF)
                s_ref[b, h] = st_b[i]
                st = decay_ref[h] * st + kv[i]
            state[h] = st
        o2 = jnp.concatenate([_dot(q[i], st_b[i]) for i in range(len(pairs))], axis=0)
        o = jnp.concatenate(o1, axis=0) + o2 * tab(xi_ref)
        mu = jnp.mean(o, axis=-1, keepdims=True)
        oc = o - mu
        rs = lax.rsqrt(jnp.mean(oc * oc, axis=-1, keepdims=True) + GN_EPS)
        g = jnp.concatenate([g_ref[sl(b, h)] for b, h in pairs], axis=0)
        r = (g * jax.nn.sigmoid(g) * (oc * rs)).astype(BF)
        for i, (b, h) in enumerate(pairs):
            o_ref[sl(b, h)] = o[i * BLOCK:(i + 1) * BLOCK, :]
            r_ref[sl(b, h)] = r[i * BLOCK:(i + 1) * BLOCK, :]

    cur = pl.BlockSpec((BLOCK * ncs, RET_W), lambda n: (n, 0))
    tab = pl.BlockSpec((H, C, C), lambda n: (0, 0, 0))
    return _pcall(
        body, name="ret_fwd", grid=(nc // ncs,),
        in_specs=[_smem_full(), cur, cur, cur, cur, tab, tab, tab],
        out_specs=[cur, pl.BlockSpec((ncs, H, C, C), lambda n: (n, 0, 0, 0)), cur],
        out_shape=[jax.ShapeDtypeStruct((T, RET_W), F32), jax.ShapeDtypeStruct((nc, H, C, C), BF),
                   jax.ShapeDtypeStruct((T, RET_W), BF)],
        scratch_shapes=[pltpu.VMEM((H, C, C), F32)],
        compiler_params=_params(("arbitrary",)),
    )(decay, qr, kr, vr, gr, d_intra, xi_b, zeta_b)


def _out_proj(a, r, w_out, x, g2, g3, tm):
    T = x.shape[0]

    def body(a_ref, r_ref, w_ref, x_ref, g2_ref, g3_ref, mixed_ref, x1_ref, h2_ref):
        mixed = _dot(a_ref[...], w_ref[:ATTN_W, :]) + _dot(r_ref[...], w_ref[ATTN_W:, :])
        mixed_ref[...] = mixed
        _, n2 = _rms_stats(mixed)
        x1 = x_ref[...] + n2 * g2_ref[...]
        x1_ref[...] = x1
        _, n3 = _rms_stats(x1)
        h2_ref[...] = (n3 * g3_ref[...]).astype(BF)

    return _pcall(
        body, name="out_proj", grid=(T // tm,),
        in_specs=[_rows(tm, ATTN_W), _rows(tm, RET_W), _vmem_full(), _rows(tm, D_MODEL), _const((1, D_MODEL)),
                  _const((1, D_MODEL))],
        out_specs=[_rows(tm, D_MODEL)] * 3,
        out_shape=[jax.ShapeDtypeStruct((T, D_MODEL), F32), jax.ShapeDtypeStruct((T, D_MODEL), F32),
                   jax.ShapeDtypeStruct((T, D_MODEL), BF)],
        compiler_params=_params(("parallel",)),
    )(a, r, w_out, x, g2, g3)


def _shift_down(cur, k, before):
    out = pltpu.roll(cur, k, 0)
    row = lax.broadcasted_iota(jnp.int32, before.shape, 0)
    top = jnp.where(row < k, pltpu.roll(before, k, 0), out[0:8])
    return jnp.concatenate([top, out[8:]], axis=0)


def _shift_up(cur, k, after):
    tm = cur.shape[0]
    out = pltpu.roll(cur, tm - k, 0)
    row = lax.broadcasted_iota(jnp.int32, after.shape, 0)
    bot = jnp.where(row >= 8 - k, pltpu.roll(after, 8 - k, 0), out[tm - 8:])
    return jnp.concatenate([out[:tm - 8], bot], axis=0)


def _gelu_parts(x):
    m = (-2.0 * GELU_C * GELU_A) * (x * x)
    s = 1.0 / (1.0 + jnp.exp(x * (m - 2.0 * GELU_C)))
    gelu = x * s
    dgelu = s + gelu * (1.0 - s) * (2.0 * GELU_C - 3.0 * m)
    return gelu, dgelu


def _ffn_fwd(h2, w_up8, conv_w8, conv_b8, w_down4, x1, g4, target, tm):
    T = h2.shape[0]
    nt = T // tm

    def body(h_ref, wu_ref, cwg_ref, cwv_ref, cbg_ref, cbv_ref, wd_ref, x1_ref, g_ref, t_ref,
             upg_ref, upv_ref, ug_ref, uv_ref, y_ref, dout_ref, dz_ref, dg4_ref, loss_ref, halo, z_acc):
        s = pl.program_id(1)
        first = pl.program_id(0) == 0

        @pl.when(first & (s == 0))
        def _():
            loss_ref[...] = jnp.zeros_like(loss_ref)
            dg4_ref[...] = jnp.zeros_like(dg4_ref)

        h = h_ref[...]
        u = []
        parts = ((cwg_ref, cbg_ref, upg_ref, ug_ref), (cwv_ref, cbv_ref, upv_ref, uv_ref))
        for part, (cw_ref, cb_ref, up_ref, u_ref) in enumerate(parts):
            cur = _dot(h, wu_ref[s + part * FF_PAIRS])
            up_ref[0] = cur.astype(BF)
            before = jnp.where(first, 0.0, halo[part, s])
            halo[part, s] = cur[tm - 8:tm, :]
            u_c = (cw_ref[0, pl.ds(0, 1), :] * _shift_down(cur, 2, before)
                   + cw_ref[0, pl.ds(1, 1), :] * _shift_down(cur, 1, before)
                   + cw_ref[0, pl.ds(2, 1), :] * cur + cb_ref[0])
            u_ref[0] = u_c
            u.append(u_c)
        gelu, _ = _gelu_parts(u[0])
        y = (gelu * u[1]).astype(BF)
        y_ref[0] = y
        z_part = _dot(y, wd_ref[s])

        @pl.when(s == 0)
        def _():
            z_acc[...] = z_part

        @pl.when(s > 0)
        def _():
            z_acc[...] += z_part

        @pl.when(s == FF_PAIRS - 1)
        def _():
            r4, n4 = _rms_stats(z_acc[...])
            err = x1_ref[...] + n4 * g_ref[...] - t_ref[...]
            dout = err * (1.0 / D_MODEL)
            dout_ref[...] = dout
            loss_ref[...] += 0.5 * jnp.sum(jnp.mean(err * err, axis=-1, keepdims=True), axis=0, keepdims=True)
            dg4_ref[...] += jnp.sum(dout * n4, axis=0, keepdims=True)
            dz_ref[...] = _rms_bwd(n4, r4, dout * g_ref[...]).astype(BF)

    rows = pl.BlockSpec((tm, D_MODEL), lambda i, s: (i, 0))
    one = lambda shape: pl.BlockSpec(shape, lambda i, s: tuple(0 for _ in shape))
    gate = lambda r, w: pl.BlockSpec((1, r, w), lambda i, s: (s, 0, 0))
    val = lambda r, w: pl.BlockSpec((1, r, w), lambda i, s: (s + FF_PAIRS, 0, 0))
    tile = pl.BlockSpec((1, tm, FF_SHARD), lambda i, s: (s, i, 0))
    half = lambda dt: jax.ShapeDtypeStruct((FF_PAIRS, T, FF_SHARD), dt)
    return _pcall(
        body, name="ffn_fwd", grid=(nt, FF_PAIRS),
        in_specs=[rows, _vmem_full(), gate(3, FF_SHARD), val(3, FF_SHARD), gate(1, FF_SHARD), val(1, FF_SHARD),
                  _vmem_full(), rows, one((1, D_MODEL)), rows],
        out_specs=[tile] * 5 + [rows, rows, one((1, D_MODEL)), one((8, 128))],
        out_shape=[half(BF), half(BF), half(F32), half(F32), half(BF), jax.ShapeDtypeStruct((T, D_MODEL), F32),
                   jax.ShapeDtypeStruct((T, D_MODEL), BF), jax.ShapeDtypeStruct((1, D_MODEL), F32),
                   jax.ShapeDtypeStruct((8, 128), F32)],
        scratch_shapes=[pltpu.VMEM((2, FF_PAIRS, 8, FF_SHARD), F32), pltpu.VMEM((tm, D_MODEL), F32)],
        compiler_params=_params(("arbitrary", "arbitrary")),
    )(h2, w_up8, conv_w8, conv_w8, conv_b8, conv_b8, w_down4, x1, g4, target)


def _ffn_bwd_a(dz, h2, w_down4, u_g, u_v, up_g, up_v, y4, conv_w8, tm):
    T = dz.shape[0]
    nt = T // tm

    def body(dz_ref, h_ref, wd_ref, ug_ref, uv_ref, upg_ref, upv_ref, y_ref, cwg_ref, cwv_ref,
             dupg_ref, dupv_ref, dcbg_ref, dcbv_ref, dcwg_ref, dcwv_ref, gwug_out, gwuv_out, gwd_out,
             carry, gwug_ref, gwuv_ref, gwd_ref):
        @pl.when(pl.program_id(1) == 0)
        def _():
            for ref in (dcbg_ref, dcbv_ref, dcwg_ref, dcwv_ref, gwug_ref, gwuv_ref, gwd_ref, carry):
                ref[...] = jnp.zeros_like(ref)

        dz = dz_ref[...]
        h = h_ref[...]
        dy = _dot_nt(dz, wd_ref[0])
        gwd_ref[0] += _dot_tn(y_ref[0], dz)
        gelu, dgelu = _gelu_parts(ug_ref[0])
        parts = ((0, dy * uv_ref[0] * dgelu, upg_ref, cwg_ref, dupg_ref, dcbg_ref, dcwg_ref, gwug_ref),
                 (1, dy * gelu, upv_ref, cwv_ref, dupv_ref, dcbv_ref, dcwv_ref, gwuv_ref))
        for part, d, up_ref, cw_ref, dup_ref, dcb_ref, dcw_ref, gwu_ref in parts:
            after = carry[part]
            d1 = _shift_up(d, 1, after)
            d2 = _shift_up(d, 2, after)
            carry[part] = d[0:8, :]
            upc = up_ref[0].astype(F32)
            dcb_ref[0] += jnp.sum(d, axis=0, keepdims=True)
            dcw_ref[0, pl.ds(2, 1), :] += jnp.sum(d * upc, axis=0, keepdims=True)
            dcw_ref[0, pl.ds(1, 1), :] += jnp.sum(d1 * upc, axis=0, keepdims=True)
            dcw_ref[0, pl.ds(0, 1), :] += jnp.sum(d2 * upc, axis=0, keepdims=True)
            dup = (cw_ref[0, pl.ds(2, 1), :] * d + cw_ref[0, pl.ds(1, 1), :] * d1
                   + cw_ref[0, pl.ds(0, 1), :] * d2).astype(BF)
            dup_ref[0] = dup
            gwu_ref[0] += _dot_tn(h, dup)

        @pl.when(pl.program_id(1) == nt - 1)
        def _():
            s = pl.program_id(0)
            pltpu.sync_copy(gwug_ref, gwug_out.at[pl.ds(s, 1)])
            pltpu.sync_copy(gwuv_ref, gwuv_out.at[pl.ds(s, 1)])
            pltpu.sync_copy(gwd_ref, gwd_out.at[pl.ds(s, 1)])

    rev = pl.BlockSpec((tm, D_MODEL), lambda s, i: (nt - 1 - i, 0))
    tile = pl.BlockSpec((1, tm, FF_SHARD), lambda s, i: (s, nt - 1 - i, 0))
    acc = lambda r, w: pl.BlockSpec((1, r, w), lambda s, i: (s, 0, 0))
    acc_val = pl.BlockSpec((1, 3, FF_SHARD), lambda s, i: (s + FF_PAIRS, 0, 0))
    half = lambda r, dt: jax.ShapeDtypeStruct((FF_PAIRS, r, FF_SHARD), dt)
    return _pcall(
        body, name="ffn_bwd_a", grid=(FF_PAIRS, nt),
        in_specs=[rev, rev, acc(FF_SHARD, D_MODEL), tile, tile, tile, tile, tile, acc(3, FF_SHARD), acc_val],
        out_specs=[tile, tile, acc(1, FF_SHARD), acc(1, FF_SHARD), acc(3, FF_SHARD), acc(3, FF_SHARD),
                   ANY_SPEC, ANY_SPEC, ANY_SPEC],
        out_shape=[half(T, BF), half(T, BF), half(1, F32), half(1, F32), half(3, F32), half(3, F32),
                   half(D_MODEL, F32), half(D_MODEL, F32), jax.ShapeDtypeStruct((FF_PAIRS, FF_SHARD, D_MODEL), F32)],
        scratch_shapes=[pltpu.VMEM((2, 8, FF_SHARD), F32), pltpu.VMEM((1, D_MODEL, FF_SHARD), F32),
                        pltpu.VMEM((1, D_MODEL, FF_SHARD), F32), pltpu.VMEM((1, FF_SHARD, D_MODEL), F32)],
        compiler_params=_params(("arbitrary", "arbitrary")),
    )(dz, h2, w_down4, u_g, u_v, up_g, up_v, y4, conv_w8, conv_w8)


def _ffn_bwd_b(dup_g, dup_v, w_up8, x1, dout, g3, mixed, g2, w_out, tm, gwu_g, gwu_v):
    T = x1.shape[0]
    nt = T // tm

    def body(dupg_ref, dupv_ref, wup_ref, x1_ref, dout_ref, g3_ref, mixed_ref, g2_ref, wout_ref, gwug_ref, gwuv_ref,
             dx1_ref, dmixed_ref, da_ref, dr_ref, dg3_ref, dg2_ref, pup_ref, *sems):
        ex = _Exchange([], [((gwug_ref, gwuv_ref), pup_ref)], *sems)

        @pl.when(pl.program_id(0) == 0)
        def _():
            ex.start()
            dg3_ref[...] = jnp.zeros_like(dg3_ref)
            dg2_ref[...] = jnp.zeros_like(dg2_ref)

        dh2 = jnp.zeros((tm, D_MODEL), F32)
        for s in range(FF_PAIRS):
            dh2 = dh2 + _dot_nt(dupg_ref[s], wup_ref[s]) + _dot_nt(dupv_ref[s], wup_ref[s + FF_PAIRS])
        r3, n3 = _rms_stats(x1_ref[...])
        dg3_ref[...] += jnp.sum(dh2 * n3, axis=0, keepdims=True)
        dx1 = dout_ref[...] + _rms_bwd(n3, r3, dh2 * g3_ref[...])
        dx1_ref[...] = dx1
        r2, n2 = _rms_stats(mixed_ref[...])
        dg2_ref[...] += jnp.sum(dx1 * n2, axis=0, keepdims=True)
        dmixed = _rms_bwd(n2, r2, dx1 * g2_ref[...]).astype(BF)
        dmixed_ref[...] = dmixed
        da_ref[...] = _dot_nt(dmixed, wout_ref[:ATTN_W, :])
        dr_ref[...] = _dot_nt(dmixed, wout_ref[ATTN_W:, :])
        pl.when(pl.program_id(0) == nt - 1)(ex.wait)

    half = pl.BlockSpec((FF_PAIRS, tm, FF_SHARD), lambda i: (0, i, 0))
    outs = _pcall(
        body, name="ffn_bwd_b", grid=(nt,),
        in_specs=[half, half, _vmem_full(), _rows(tm, D_MODEL), _rows(tm, D_MODEL), _const((1, D_MODEL)),
                  _rows(tm, D_MODEL), _const((1, D_MODEL)), _vmem_full(), ANY_SPEC, ANY_SPEC],
        out_specs=[_rows(tm, D_MODEL), _rows(tm, D_MODEL), _rows(tm, ATTN_W), _rows(tm, RET_W),
                   _const((1, D_MODEL)), _const((1, D_MODEL)), ANY_SPEC],
        out_shape=[jax.ShapeDtypeStruct((T, D_MODEL), F32), jax.ShapeDtypeStruct((T, D_MODEL), BF),
                   jax.ShapeDtypeStruct((T, ATTN_W), F32), jax.ShapeDtypeStruct((T, RET_W), F32),
                   jax.ShapeDtypeStruct((1, D_MODEL), F32), jax.ShapeDtypeStruct((1, D_MODEL), F32),
                   jax.ShapeDtypeStruct((N_DEV, D_MODEL, FF_SHARD), F32)],
        scratch_shapes=_Exchange.scratch(1),
        compiler_params=_params(("arbitrary",)),
    )(dup_g, dup_v, w_up8, x1, dout, g3, mixed, g2, w_out, gwu_g, gwu_v)
    return outs[:6], outs[6]


def _ret_bwd(decay, qr, kr, vr, gr, o, states, dr, d_intra, d_intra_t, xi_b, zeta_b, cos, sin_s, swaps, ncs):
    T = qr.shape[0]
    nc = T // BLOCK
    H, C = N_RET_HEADS, RET_HEAD_DIM
    ns = len(swaps)

    def body(decay_ref, q_ref, k_ref, v_ref, g_ref, o_ref, s_ref, dr_ref, d_ref, dt_ref, xi_ref, zeta_ref,
             cos_ref, sin_ref, *rest):
        ex = _exchange_of(rest[:ns], rest[ns + 1:2 * ns + 1], rest[2 * ns + 2:], 0)
        dret_ref, gstate = rest[ns], rest[2 * ns + 1]

        @pl.when(pl.program_id(0) == 0)
        def _():
            ex.start()
            gstate[...] = jnp.zeros_like(gstate)

        pairs = [(b, h) for b in range(ncs) for h in range(H)]
        n = len(pairs)
        sl = lambda b, h: (slice(b * BLOCK, (b + 1) * BLOCK), slice(h * C, (h + 1) * C))
        cat = lambda ref: jnp.concatenate([ref[sl(b, h)] for b, h in pairs], axis=0)
        tab = lambda ref: jnp.concatenate([ref[h] for _, h in pairs], axis=0)
        part = lambda x, i: x[i * BLOCK:(i + 1) * BLOCK, :]
        q = [q_ref[sl(b, h)] for b, h in pairs]
        k = [k_ref[sl(b, h)] for b, h in pairs]
        v = [v_ref[sl(b, h)] for b, h in pairs]
        g, o_all, dr_all = cat(g_ref), cat(o_ref), cat(dr_ref)
        mu = jnp.mean(o_all, axis=-1, keepdims=True)
        oc = o_all - mu
        rs = lax.rsqrt(jnp.mean(oc * oc, axis=-1, keepdims=True) + GN_EPS)
        on = oc * rs
        sg = jax.nn.sigmoid(g)
        dg = (dr_all * on * (sg * (1.0 + g * (1.0 - sg)))).astype(BF)
        don = dr_all * (g * sg)
        do = rs * (don - jnp.mean(don, axis=-1, keepdims=True) - on * jnp.mean(don * on, axis=-1, keepdims=True))
        do_b = do.astype(BF)
        dox_b = (do * tab(xi_ref)).astype(BF)
        zeta = tab(zeta_ref)
        kz = (jnp.concatenate(k, axis=0).astype(F32) * zeta).astype(BF)
        d_t = tab(dt_ref)
        da_b = (jnp.concatenate([_dot_nt(part(do_b, i), v[i]) for i in range(n)], axis=0) * tab(d_ref)).astype(BF)
        dat_b = (jnp.concatenate([_dot_nt(v[i], part(do_b, i)) for i in range(n)], axis=0) * d_t).astype(BF)
        mt_b = (jnp.concatenate([_dot_nt(k[i], q[i]) for i in range(n)], axis=0) * d_t).astype(BF)
        dq = [_dot(part(da_b, i), k[i]) + _dot_nt(part(dox_b, i), s_ref[pairs[i]]) for i in range(n)]
        dk1 = [_dot(part(dat_b, i), q[i]) for i in range(n)]
        dv1 = [_dot(part(mt_b, i), part(do_b, i)) for i in range(n)]
        qtd = [_dot_tn(q[i], part(dox_b, i)) for i in range(n)]
        gst_b = [None] * n
        for h in range(H):
            gst = gstate[h]
            for b in reversed(range(ncs)):
                i = b * H + h
                gst_b[i] = gst.astype(BF)
                gst = decay_ref[h] * gst + qtd[i]
            gstate[h] = gst
        dk2 = jnp.concatenate([_dot_nt(v[i], gst_b[i]) for i in range(n)], axis=0) * zeta
        dv = jnp.concatenate([dv1[i] + _dot(part(kz, i), gst_b[i]) for i in range(n)], axis=0).astype(BF)
        even = lax.broadcasted_iota(jnp.int32, (n * BLOCK, C), 1) % 2 == 0
        cos_t = jnp.concatenate([cos_ref[b * BLOCK:(b + 1) * BLOCK, :] for b, _ in pairs], axis=0)
        sin_t = jnp.concatenate([sin_ref[b * BLOCK:(b + 1) * BLOCK, :] for b, _ in pairs], axis=0)
        dq = jnp.concatenate(dq, axis=0)
        dk = jnp.concatenate(dk1, axis=0) + dk2
        dq = (dq * cos_t - _rot(dq, even) * sin_t).astype(BF)
        dk = ((dk * cos_t - _rot(dk, even) * sin_t) * RET_K_SCALE).astype(BF)
        for i, (b, h) in enumerate(pairs):
            rows = slice(b * BLOCK, (b + 1) * BLOCK)
            for j, x in enumerate((dq, dk, dv, dg)):
                dret_ref[rows, j * RET_W + h * C:j * RET_W + (h + 1) * C] = part(x, i)
        pl.when(pl.program_id(0) == steps - 1)(ex.wait)

    steps = nc // ncs
    rev = lambda w: pl.BlockSpec((BLOCK * ncs, w), lambda n: (steps - 1 - n, 0))
    tab = pl.BlockSpec((H, C, C), lambda n: (0, 0, 0))
    outs = _pcall(
        body, name="ret_bwd", grid=(steps,),
        in_specs=[_smem_full(), rev(RET_W), rev(RET_W), rev(RET_W), rev(RET_W), rev(RET_W),
                  pl.BlockSpec((ncs, H, C, C), lambda n: (steps - 1 - n, 0, 0, 0)), rev(RET_W), tab, tab, tab, tab,
                  rev(C), rev(C)] + [ANY_SPEC] * ns,
        out_specs=[rev(4 * RET_W)] + [ANY_SPEC] * ns,
        out_shape=[jax.ShapeDtypeStruct((T, 4 * RET_W), BF)] + _exchange_shapes([], swaps),
        scratch_shapes=[pltpu.VMEM((H, C, C), F32)] + _Exchange.scratch(ns),
        compiler_params=_params(("arbitrary",)),
    )(decay, qr, kr, vr, gr, o, states, dr, d_intra, d_intra_t, xi_b, zeta_b, cos, sin_s, *swaps)
    return outs[0], outs[1:]


def _attn_bwd_dq(sinks, qa, ka, va, da, lse, nbs, swaps):
    T = qa.shape[0]
    steps = T // (BLOCK * nbs)
    R = nbs * N_ATTN_HEADS * BLOCK
    ns = len(swaps)

    def body(sink_ref, q_ref, kc_ref, kp_ref, vc_ref, vp_ref, da_ref, lse_ref, *rest):
        ex = _exchange_of(rest[:ns], rest[ns + 3:2 * ns + 3], rest[2 * ns + 3:], 0)
        dq_ref, deltat_ref, dsink_ref = rest[ns:ns + 3]
        n = pl.program_id(0)

        @pl.when(n == 0)
        def _():
            ex.start()
            dsink_ref[...] = jnp.zeros_like(dsink_ref)

        kcat = _block_variants(kp_ref, kc_ref, nbs)
        vcat = _block_variants(vp_ref, vc_ref, nbs)
        tri1 = _tri(BLOCK)
        lane = lax.broadcasted_iota(jnp.int32, (BLOCK, BLOCK), 1)
        s_tiles, dp_tiles, lse_cols = [], [], []
        for b in range(nbs):
            rows = slice(b * BLOCK, (b + 1) * BLOCK)
            lse_tile = lse_ref[rows, :]
            for pair in range(N_ATTN_HEADS // 2):
                qp = _scaled(q_ref[rows, pair * 128:(pair + 1) * 128])
                dop = da_ref[rows, pair * 128:(pair + 1) * 128].astype(BF)
                for e in range(2):
                    s = _fold(_dot_nt(qp, kcat[b][pair // 2][e]), tri1, True)
                    if b == 0:
                        s = jnp.where(tri1 & (n == 0), MASK_VALUE, s)
                    s_tiles.append(s)
                    dp_tiles.append(_fold(_dot_nt(dop, vcat[b][pair // 2][e]), tri1, True))
                    lse_cols.append(jnp.sum(jnp.where(lane == 2 * pair + e, lse_tile, 0.0), axis=-1, keepdims=True))
        lse_c = jnp.concatenate(lse_cols, axis=0)
        p = jnp.exp(jnp.concatenate(s_tiles, axis=0) - lse_c)
        dp = jnp.concatenate(dp_tiles, axis=0)
        delta = jnp.sum(p * dp, axis=-1, keepdims=True)
        ds2 = _unfold(p * (dp - delta), _tri(R), True)
        for b in range(nbs):
            for pair in range(N_ATTN_HEADS // 2):
                r0 = (b * N_ATTN_HEADS + 2 * pair) * BLOCK
                acc = (_dot(ds2[r0:r0 + BLOCK, :], kcat[b][pair // 2][0])
                       + _dot(ds2[r0 + BLOCK:r0 + 2 * BLOCK, :], kcat[b][pair // 2][1]))
                dq_ref[b * BLOCK:(b + 1) * BLOCK, pair * 128:(pair + 1) * 128] = (acc * ATTN_SCALE).astype(BF)
        for b, t in enumerate(_head_cols(delta, nbs)):
            deltat_ref[:, b * BLOCK:(b + 1) * BLOCK] = t.T[:N_ATTN_HEADS, :]
        sink = jnp.concatenate([jnp.full((BLOCK, 1), sink_ref[head], F32)
                                for _ in range(nbs) for head in range(N_ATTN_HEADS)], axis=0)
        ds_sink = -jnp.exp(sink - lse_c) * delta
        row8 = lax.broadcasted_iota(jnp.int32, (N_ATTN_HEADS, BLOCK), 0)
        dsink = jnp.zeros((N_ATTN_HEADS, BLOCK), F32)
        for b in range(nbs):
            for head in range(N_ATTN_HEADS):
                r0 = (b * N_ATTN_HEADS + head) * BLOCK
                dsink = dsink + jnp.where(row8 == head, jnp.sum(ds_sink[r0:r0 + BLOCK, :], axis=0, keepdims=True), 0.0)
        dsink_ref[...] += dsink
        pl.when(n == steps - 1)(ex.wait)

    cur = lambda w: pl.BlockSpec((BLOCK * nbs, w), lambda n: (n, 0))
    prev = lambda w: pl.BlockSpec((BLOCK, w), lambda n: (jnp.maximum(n * nbs - 1, 0), 0))
    outs = _pcall(
        body, name="attn_bwd_dq", grid=(steps,),
        in_specs=[_smem_full(), cur(ATTN_W), cur(KV_W), prev(KV_W), cur(KV_W), prev(KV_W), cur(ATTN_W), cur(BLOCK)]
        + [ANY_SPEC] * ns,
        out_specs=[cur(ATTN_W), pl.BlockSpec((N_ATTN_HEADS, BLOCK * nbs), lambda n: (0, n)),
                   _const((N_ATTN_HEADS, BLOCK))] + [ANY_SPEC] * ns,
        out_shape=[jax.ShapeDtypeStruct((T, ATTN_W), BF), jax.ShapeDtypeStruct((N_ATTN_HEADS, T), F32),
                   jax.ShapeDtypeStruct((N_ATTN_HEADS, BLOCK), F32)] + _exchange_shapes([], swaps),
        scratch_shapes=_Exchange.scratch(ns),
        compiler_params=_params(("arbitrary",)),
    )(sinks, qa, ka, ka, va, va, da, lse, *swaps)
    return outs[:3], outs[3:]


def _attn_bwd_dkv(qa, ka, va, da, lse_t, delta_t, nbs):
    T = qa.shape[0]
    nb = T // BLOCK
    steps = nb // nbs
    R = nbs * N_ATTN_HEADS * BLOCK

    def body(qc_ref, qn_ref, dac_ref, dan_ref, k_ref, v_ref, lc_ref, ln_ref, dc_ref, dn_ref, dk_ref, dv_ref):
        n = pl.program_id(0)
        tri1 = _tri(BLOCK, True)
        lo = lax.broadcasted_iota(jnp.int32, (BLOCK, 128), 1) < HEAD_DIM
        kv = [_kv_variants(k_ref[b * BLOCK:(b + 1) * BLOCK, :]) for b in range(nbs)]
        vv = [_kv_variants(v_ref[b * BLOCK:(b + 1) * BLOCK, :]) for b in range(nbs)]
        qcat, docat = [], []
        s_tiles, dp_tiles, lse_tiles, delta_tiles = [], [], [], []
        for b in range(nbs):
            rows = slice(b * BLOCK, (b + 1) * BLOCK)
            nrows = slice((b + 1) * BLOCK, (b + 2) * BLOCK)
            inside = b < nbs - 1
            for pair in range(N_ATTN_HEADS // 2):
                ps = slice(pair * 128, (pair + 1) * 128)
                q2 = _scaled(jnp.concatenate([qc_ref[rows, ps], qc_ref[nrows, ps] if inside else qn_ref[:, ps]], axis=0))
                do2 = jnp.concatenate([dac_ref[rows, ps], dac_ref[nrows, ps] if inside else dan_ref[:, ps]],
                                      axis=0).astype(BF)
                qcat.append(q2)
                docat.append(do2)
                for e in range(2):
                    one = pl.ds(2 * pair + e, 1)
                    s = _fold(_dot_nt(kv[b][pair // 2][e], q2), tri1, False)
                    if not inside:
                        s = jnp.where(tri1 & (n == steps - 1), MASK_VALUE, s)
                    s_tiles.append(s)
                    dp_tiles.append(_fold(_dot_nt(vv[b][pair // 2][e], do2), tri1, False))
                    lse_tiles.append(jnp.where(tri1, lc_ref[one, nrows] if inside else ln_ref[one, :], lc_ref[one, rows]))
                    delta_tiles.append(jnp.where(tri1, dc_ref[one, nrows] if inside else dn_ref[one, :],
                                                 dc_ref[one, rows]))
        pt = jnp.exp(jnp.concatenate(s_tiles, axis=0) - jnp.concatenate(lse_tiles, axis=0))
        dst = pt * (jnp.concatenate(dp_tiles, axis=0) - jnp.concatenate(delta_tiles, axis=0))
        tri = _tri(R, True)
        pt2 = _unfold(pt, tri, False)
        dst2 = _unfold(dst, tri, False)
        for b in range(nbs):
            dk = jnp.zeros((BLOCK, 128), F32)
            dv = jnp.zeros((BLOCK, 128), F32)
            for pair in range(N_ATTN_HEADS // 2):
                h = pair // 2
                for e in range(2):
                    r0 = (b * N_ATTN_HEADS + 2 * pair + e) * BLOCK
                    half = lo if e == 0 else jnp.logical_not(lo)
                    dv_e = jnp.where(half, _dot(pt2[r0:r0 + BLOCK, :], docat[b * 4 + pair]), 0.0)
                    dk_e = jnp.where(half, _dot(dst2[r0:r0 + BLOCK, :], qcat[b * 4 + pair]), 0.0)
                    if e != h:
                        dv_e = pltpu.roll(dv_e, HEAD_DIM, 1)
                        dk_e = pltpu.roll(dk_e, HEAD_DIM, 1)
                    dv = dv + dv_e
                    dk = dk + dk_e
            dk_ref[b * BLOCK:(b + 1) * BLOCK, :] = dk.astype(BF)
            dv_ref[b * BLOCK:(b + 1) * BLOCK, :] = dv.astype(BF)

    cur = lambda w: pl.BlockSpec((BLOCK * nbs, w), lambda n: (n, 0))
    nxt = lambda w: pl.BlockSpec((BLOCK, w), lambda n: (jnp.minimum((n + 1) * nbs, nb - 1), 0))
    tcur = pl.BlockSpec((N_ATTN_HEADS, BLOCK * nbs), lambda n: (0, n))
    tnxt = pl.BlockSpec((N_ATTN_HEADS, BLOCK), lambda n: (0, jnp.minimum((n + 1) * nbs, nb - 1)))
    return _pcall(
        body, name="attn_bwd_dkv", grid=(steps,),
        in_specs=[cur(ATTN_W), nxt(ATTN_W), cur(ATTN_W), nxt(ATTN_W), cur(KV_W), cur(KV_W), tcur, tnxt, tcur, tnxt],
        out_specs=[cur(KV_W), cur(KV_W)],
        out_shape=[jax.ShapeDtypeStruct((T, KV_W), BF), jax.ShapeDtypeStruct((T, KV_W), BF)],
        compiler_params=_params(("parallel",)),
    )(qa, qa, da, da, ka, va, lse_t, lse_t, delta_t, delta_t)


def _in_proj_bwd(dqa, dka, dva, dret, w_in, x, g1, dx1, tm):
    T = x.shape[0]

    def body(dqa_ref, dka_ref, dva_ref, dret_ref, w_ref, x_ref, g_ref, dx1_ref, dx_ref, dg1_ref):
        @pl.when(pl.program_id(0) == 0)
        def _():
            dg1_ref[...] = jnp.zeros_like(dg1_ref)

        dh = (_dot_nt(dqa_ref[...], w_ref[:, QA0:QA0 + ATTN_W]) + _dot_nt(dka_ref[...], w_ref[:, KA0:KA0 + KV_W])
              + _dot_nt(dva_ref[...], w_ref[:, VA0:VA0 + KV_W]) + _dot_nt(dret_ref[...], w_ref[:, QR0:IN_W]))
        r, n = _rms_stats(x_ref[...])
        dg1_ref[...] += jnp.sum(dh * n, axis=0, keepdims=True)
        dx_ref[...] = dx1_ref[...] + _rms_bwd(n, r, dh * g_ref[...])

    return _pcall(
        body, name="in_proj_bwd", grid=(T // tm,),
        in_specs=[_rows(tm, ATTN_W), _rows(tm, KV_W), _rows(tm, KV_W), _rows(tm, 4 * RET_W), _vmem_full(),
                  _rows(tm, D_MODEL), _const((1, D_MODEL)), _rows(tm, D_MODEL)],
        out_specs=[_rows(tm, D_MODEL), _const((1, D_MODEL))],
        out_shape=[jax.ShapeDtypeStruct((T, D_MODEL), F32), jax.ShapeDtypeStruct((1, D_MODEL), F32)],
        compiler_params=_params(("arbitrary",)),
    )(dqa, dka, dva, dret, w_in, x, g1, dx1)


def _wgrad(a_list, b_list, tk, name):
    T = a_list[0].shape[0]
    na, nbb = len(a_list), len(b_list)
    m_sizes = [a.shape[1] for a in a_list]
    n_sizes = [b.shape[1] for b in b_list]
    M, N = sum(m_sizes), sum(n_sizes)
    nk = T // tk
    chunk = 512

    def body(*refs):
        a_refs, b_refs = refs[:na], refs[na:na + nbb]
        out_ref, acc = refs[na + nbb], refs[na + nbb + 1]
        k = pl.program_id(0)

        @pl.when(k == 0)
        def _():
            acc[...] = jnp.zeros_like(acc)

        r0 = 0
        for ai in range(na):
            a = a_refs[ai][...]
            c0 = 0
            for bi in range(nbb):
                for s in range(0, n_sizes[bi], chunk):
                    w = min(chunk, n_sizes[bi] - s)
                    acc[r0:r0 + m_sizes[ai], c0 + s:c0 + s + w] += _dot_tn(a, b_refs[bi][:, s:s + w])
                c0 += n_sizes[bi]
            r0 += m_sizes[ai]

        @pl.when(k == nk - 1)
        def _():
            pltpu.sync_copy(acc, out_ref)

    return _pcall(
        body, name=name, grid=(nk,),
        in_specs=[_rows(tk, w) for w in m_sizes + n_sizes],
        out_specs=pl.BlockSpec(memory_space=pl.ANY),
        out_shape=jax.ShapeDtypeStruct((M, N), F32),
        scratch_shapes=[pltpu.VMEM((M, N), F32)],
        compiler_params=_params(("arbitrary",)),
    )(*a_list, *b_list)


def _adamw_math(w, g, m, v):
    m = ADAM_B1 * m + (1.0 - ADAM_B1) * g
    v = ADAM_B2 * v + (1.0 - ADAM_B2) * (g * g)
    m_hat = m / (1.0 - ADAM_B1 ** ADAM_STEP)
    v_hat = v / (1.0 - ADAM_B2 ** ADAM_STEP)
    delta = -ADAM_LR * (m_hat / (jnp.sqrt(v_hat) + ADAM_EPS) + ADAM_WD * w)
    return delta, m, v


def _sum_parts(parts_ref):
    g = parts_ref[0].astype(F32)
    for i in range(1, N_DEV):
        g = g + parts_ref[i].astype(F32)
    return g


def _adamw_shard(parts, w, m, v, tr, name):
    R, C = w.shape

    def body(p_ref, w_ref, m_ref, v_ref, g_ref, d_ref, nm_ref, nv_ref):
        g = _sum_parts(p_ref)
        g_ref[...] = g
        d_ref[...], nm_ref[...], nv_ref[...] = _adamw_math(w_ref[...], g, m_ref[...], v_ref[...])

    blk = pl.BlockSpec((tr, C), lambda i: (i, 0))
    return _pcall(
        body, name=name, grid=(R // tr,),
        in_specs=[pl.BlockSpec((N_DEV, tr, C), lambda i: (0, i, 0)), blk, blk, blk],
        out_specs=[blk] * 4,
        out_shape=[jax.ShapeDtypeStruct((R, C), F32)] * 4,
        compiler_params=_params(("parallel",)),
    )(parts, w, m, v)


def _sum_small(parts):
    def body(p_ref, g_ref):
        g_ref[...] = _sum_parts(p_ref)

    return _pcall(body, name="sum_small", out_shape=jax.ShapeDtypeStruct(parts.shape[1:], F32),
                  in_specs=[_vmem_full()], out_specs=_vmem_full())(parts)


def _adamw_small(g, w, m, v, name):
    def body(g_ref, w_ref, m_ref, v_ref, d_ref, nm_ref, nv_ref):
        d_ref[...], nm_ref[...], nv_ref[...] = _adamw_math(w_ref[...], g_ref[...], m_ref[...], v_ref[...])

    return _pcall(body, name=name, out_shape=[jax.ShapeDtypeStruct(w.shape, F32)] * 3,
                  in_specs=[_vmem_full()] * 4, out_specs=[_vmem_full()] * 3)(g, w, m, v)


def _tables(T):
    h, c = N_RET_HEADS, BLOCK
    pos = jnp.arange(T, dtype=F32)
    angle = 1.0 / jnp.power(10000.0, jnp.linspace(0.0, 1.0, RET_HEAD_DIM // 2, dtype=F32))
    angle = jnp.repeat(angle, 2)
    sin = jnp.sin(pos[:, None] * angle[None])
    cos = jnp.cos(pos[:, None] * angle[None])
    even = (jnp.arange(RET_HEAD_DIM) % 2 == 0)[None, :]
    sin_s = jnp.where(even, -sin, sin)
    log_gamma = jnp.log(1.0 - jnp.power(2.0, -5.0 - jnp.arange(h, dtype=F32)))
    idx = jnp.arange(c, dtype=F32)
    rel = idx[:, None] - idx[None, :]
    d_intra = jnp.where(rel[None] >= 0, jnp.exp(log_gamma[:, None, None] * jnp.maximum(rel, 0.0)[None]), 0.0)
    xi = jnp.exp(log_gamma[None, :] * (idx[:, None] + 1.0))
    zeta = jnp.exp(log_gamma[None, :] * (c - 1.0 - idx[:, None]))
    decay = jnp.exp(log_gamma * c)
    xi_b = jnp.broadcast_to(xi.T[:, :, None], (h, c, RET_HEAD_DIM))
    zeta_b = jnp.broadcast_to(zeta.T[:, :, None], (h, c, RET_HEAD_DIM))
    return cos, sin_s, d_intra, jnp.swapaxes(d_intra, 1, 2), xi_b, zeta_b, decay


def _to_shards(full, cols):
    r = full.shape[0]
    return jnp.swapaxes(full.reshape(r, N_DEV, cols), 0, 1)


def _from_shards(sh):
    n, r, cols = sh.shape
    return jnp.swapaxes(sh, 0, 1).reshape(r, n * cols)


SMALL_ROWS = 216


def _pack_small(gains, conv_b, conv_w, sinks, scalar=None):
    last = jnp.concatenate([sinks.reshape(1, 8), jnp.zeros((1, 1), F32) if scalar is None else scalar.reshape(1, 1)],
                           axis=1)
    parts = [g.reshape(8, 128) for g in gains] + [conv_b.reshape(44, 128), conv_w.reshape(132, 128),
                                                  jnp.pad(last, ((0, 0), (0, 119)))]
    packed = jnp.concatenate(parts, axis=0)
    return jnp.pad(packed, ((0, SMALL_ROWS - packed.shape[0]), (0, 0)))


def kernel(x, mix_pre_norm, w_in, attn_sinks, w_out, mix_post_norm, ffn_pre_norm, w_up, conv_w, conv_b, w_down, ffn_post_norm, loss_target, m_mix_pre_norm, m_w_in, m_attn_sinks, m_w_out, m_mix_post_norm, m_ffn_pre_norm, m_w_up, m_conv_w, m_conv_b, m_w_down, m_ffn_post_norm, v_mix_pre_norm, v_w_in, v_attn_sinks, v_w_out, v_mix_post_norm, v_ffn_pre_norm, v_w_up, v_conv_w, v_conv_b, v_w_down, v_ffn_post_norm):
    T = x.shape[1]
    tm = min(512, T)
    tm_big = min(1024, T)
    tk_grad = min(2048, T)
    nbs = min(8, T // BLOCK)
    x2 = x.reshape(T, D_MODEL)
    target = loss_target.reshape(T, D_MODEL)
    me = 4 * lax.axis_index("x") + 2 * lax.axis_index("y") + lax.axis_index("c")

    g_in, g_cw = _exchange_call([w_in[0].astype(BF), conv_w[0]], [], "gather_w_in")
    w_in_f = _from_shards(g_in)
    cos, sin_s, d_intra, d_intra_t, xi_b, zeta_b, decay = _tables(T)
    sinks = attn_sinks.reshape(N_ATTN_HEADS)

    (h1, qa, ka, va, qr, kr, vr, gr), (w_up8,) = _in_proj(
        x2, mix_pre_norm, w_in_f, cos, sin_s, tm_big, [w_up[0].astype(BF)])
    (a, lse, lse_t), (g_down, g_out) = _attn_fwd(sinks, qa, ka, va, nbs,
                                                 [w_down[0].astype(BF), w_out[0].astype(BF)])
    w_out_f = g_out.reshape(D_MODEL, D_MODEL)
    o, states, r = _ret_fwd(decay, qr, kr, vr, gr, d_intra, xi_b, zeta_b, nbs)
    mixed, x1, h2 = _out_proj(a, r, w_out_f, x2, mix_post_norm, ffn_pre_norm, tm_big)
    w_down4 = g_down.reshape(FF_PAIRS, FF_SHARD, D_MODEL)
    up_g, up_v, u_g, u_v, y4, dout, dz, dg4, loss_acc = _ffn_fwd(
        h2, w_up8, g_cw, conv_b.reshape(N_DEV, 1, FF_SHARD), w_down4, x1, ffn_post_norm, target, tm)

    dup_g, dup_v, dcb_g, dcb_v, dcw_g, dcw_v, gwu_g, gwu_v, gw_down4 = _ffn_bwd_a(
        dz, h2, w_down4, u_g, u_v, up_g, up_v, y4, g_cw, tm_big)
    dcb = jnp.concatenate([dcb_g, dcb_v], axis=0).reshape(1, 2 * D_FF)
    dcw = _from_shards(jnp.concatenate([dcw_g, dcw_v], axis=0))
    gw_down = gw_down4.reshape(D_FF, D_MODEL)
    (dx1, dmixed, da, dr, dg3, dg2), p_up = _ffn_bwd_b(
        dup_g, dup_v, w_up8, x1, dout, ffn_pre_norm, mixed, mix_post_norm, w_out_f, tm, gwu_g, gwu_v)
    gw_out = _wgrad([a, r], [dmixed], tk_grad, "wgrad_out")
    dret, (p_down,) = _ret_bwd(decay, qr, kr, vr, gr, o, states, dr, d_intra, d_intra_t, xi_b, zeta_b, cos, sin_s,
                               [gw_down.reshape(N_DEV, D_FF // N_DEV, D_MODEL)], nbs)
    (dqa, delta_t, dsink), (p_out,) = _attn_bwd_dq(sinks, qa, ka, va, da, lse, nbs,
                                                   [gw_out.reshape(N_DEV, D_MODEL // N_DEV, D_MODEL)])
    dka, dva = _attn_bwd_dkv(qa, ka, va, da, lse_t, delta_t, nbs)
    grad_x, dg1 = _in_proj_bwd(dqa, dka, dva, dret, w_in_f, x2, mix_pre_norm, dx1, tm)
    gw_in = _wgrad([h1], [dqa, dka, dva, dret], tk_grad, "wgrad_in")

    small = _pack_small([dg1, dg2, dg3, dg4], dcb, dcw, dsink[:, 0], loss_acc[0, 0])
    small_all, p_in = _exchange_call([small], [_to_shards(gw_in, IN_W // N_DEV).astype(BF)], "exchange_last")
    g_small = _sum_small(small_all)
    loss = g_small[208, N_ATTN_HEADS]

    g_w_in, d_w_in, nm_w_in, nv_w_in = _adamw_shard(p_in, w_in[0], m_w_in[0], v_w_in[0], 256, "adamw_in")
    g_w_up, d_w_up, nm_w_up, nv_w_up = _adamw_shard(p_up, w_up[0], m_w_up[0], v_w_up[0], 256, "adamw_up")
    g_w_out, d_w_out, nm_w_out, nv_w_out = _adamw_shard(p_out, w_out[0], m_w_out[0], v_w_out[0], 128, "adamw_out")
    g_w_down, d_w_down, nm_w_down, nv_w_down = _adamw_shard(p_down, w_down[0], m_w_down[0], v_w_down[0], 176,
                                                            "adamw_down")
    gains = [mix_pre_norm, mix_post_norm, ffn_pre_norm, ffn_post_norm]
    m_gains = [m_mix_pre_norm, m_mix_post_norm, m_ffn_pre_norm, m_ffn_post_norm]
    v_gains = [v_mix_pre_norm, v_mix_post_norm, v_ffn_pre_norm, v_ffn_post_norm]
    zeros_cw = jnp.zeros((3, 2 * D_FF), F32)
    w_small = _pack_small(gains, conv_b, zeros_cw, attn_sinks)
    m_small = _pack_small(m_gains, m_conv_b, zeros_cw, m_attn_sinks)
    v_small = _pack_small(v_gains, v_conv_b, zeros_cw, v_attn_sinks)
    d_small, nm_small, nv_small = _adamw_small(g_small, w_small, m_small, v_small, "adamw_small")
    shard_cols = 2 * D_FF // N_DEV
    g_cw = lax.dynamic_slice(g_small[76:208].reshape(3, 2 * D_FF), (0, me * shard_cols), (3, shard_cols))
    d_cw, nm_cw, nv_cw = _adamw_small(g_cw, conv_w[0], m_conv_w[0], v_conv_w[0], "adamw_conv_w")

    def unpack(p):
        gains_o = [p[8 * i:8 * i + 8].reshape(1, D_MODEL) for i in range(4)]
        return gains_o, p[32:76].reshape(1, 2 * D_FF), p[208:209, :N_ATTN_HEADS]

    def leaves(p, w_in_s, w_out_s, w_up_s, cw_s, w_down_s):
        (pre1, post1, pre2, post2), cb, sk = unpack(p)
        return [pre1, w_in_s[None], sk, w_out_s[None], post1, pre2, w_up_s[None], cw_s[None], cb, w_down_s[None],
                post2]

    return (loss, grad_x.reshape(1, T, D_MODEL),
            *leaves(g_small, g_w_in, g_w_out, g_w_up, g_cw, g_w_down),
            *leaves(d_small, d_w_in, d_w_out, d_w_up, d_cw, d_w_down),
            *leaves(nm_small, nm_w_in, nm_w_out, nm_w_up, nm_cw, nm_w_down),
            *leaves(nv_small, nv_w_in, nv_w_out, nv_w_up, nv_cw, nv_w_down))
```

```python
import functools
import math

import jax
import jax.numpy as jnp
from jax import lax
from jax.experimental import pallas as pl
from jax.experimental.pallas import tpu as pltpu

F32 = jnp.float32
BF = jnp.bfloat16

N_DEV = 8
D_MODEL = 1024
HEAD_DIM = 64
ATTN_W = 512
N_ATTN_HEADS = 8
KV_W = 128
BLOCK = 128
RET_W = 512
N_RET_HEADS = 4
RET_HEAD_DIM = 128
IN_W = 2816
D_FF = 2816
RMS_EPS = 1e-6
GN_EPS = 1e-6
MASK_VALUE = -1e30
ATTN_SCALE = HEAD_DIM ** -0.5
RET_K_SCALE = RET_HEAD_DIM ** -0.5
GELU_C = math.sqrt(2.0 / math.pi)
GELU_A = 0.044715

ADAM_LR = 0.001
ADAM_B1 = 0.9
ADAM_B2 = 0.999
ADAM_EPS = 1e-08
ADAM_WD = 0.01
ADAM_STEP = 10

VMEM_LIMIT_BYTES = 56 * 1024 * 1024
FF_SHARD = 2 * D_FF // N_DEV
FF_PAIRS = N_DEV // 2

QA0, KA0, VA0, QR0, KR0, VR0, GR0 = 0, 512, 640, 768, 1280, 1792, 2304

MESH_ID = pl.DeviceIdType.MESH


def _pcall(body, **kw):
    return pl.pallas_call(body, **kw)


def _params(sem=None):
    return pltpu.CompilerParams(dimension_semantics=sem, vmem_limit_bytes=VMEM_LIMIT_BYTES)


def _dot(a, b):
    return jnp.dot(a, b, preferred_element_type=F32)


def _dot_nt(a, b):
    return lax.dot_general(a, b, (((1,), (1,)), ((), ())), preferred_element_type=F32)


def _dot_tn(a, b):
    return lax.dot_general(a, b, (((0,), (0,)), ((), ())), preferred_element_type=F32)


def _vmem_full():
    return pl.BlockSpec(memory_space=pltpu.VMEM)


def _smem_full():
    return pl.BlockSpec(memory_space=pltpu.SMEM)


def _rows(tm, w):
    return pl.BlockSpec((tm, w), lambda i: (i, 0))


def _const(shape):
    return pl.BlockSpec(shape, lambda i: tuple(0 for _ in shape))


def _rms_stats(x):
    r = lax.rsqrt(jnp.mean(x * x, axis=-1, keepdims=True) + RMS_EPS)
    return r, x * r


def _rms_bwd(n, r, dn):
    return r * (dn - n * jnp.mean(dn * n, axis=-1, keepdims=True))


def _rot(x, even):
    w = x.shape[1]
    return jnp.where(even, pltpu.roll(x, w - 1, 1), pltpu.roll(x, 1, 1))


def _peers():
    x, y, c = lax.axis_index("x"), lax.axis_index("y"), lax.axis_index("c")
    flips = [(0, 0, 1), (1, 0, 0), (0, 1, 0), (1, 1, 0), (1, 0, 1), (0, 1, 1), (1, 1, 1)]
    peers = [(x ^ fx, y ^ fy, c ^ fc) for fx, fy, fc in flips]
    return 4 * x + 2 * y + c, peers


SAME_CORE_PEERS = 4


class _Exchange:
    def __init__(self, gathers, swaps, send_sems, recv_sems, local_sems):
        self.me, self.peers = _peers()
        self.slots = [4 * px + 2 * py + pc for px, py, pc in self.peers]
        self.pairs = [(src, dst, True) for src, dst in gathers] + [(src, dst, False) for src, dst in swaps]
        self.send_sems, self.recv_sems, self.local_sems = send_sems, recv_sems, local_sems

    @staticmethod
    def scratch(n):
        return [pltpu.SemaphoreType.DMA((n, N_DEV - 1)), pltpu.SemaphoreType.DMA((n, N_DEV - 1)),
                pltpu.SemaphoreType.DMA((n,))]

    def _parts(self, a, slot):
        src, _, whole = self.pairs[a]
        half = N_DEV // 2
        if whole:
            return [(None, src)]
        if isinstance(src, tuple):
            return [(slot < half, src[0].at[jnp.minimum(slot, half - 1)]),
                    (slot >= half, src[1].at[jnp.maximum(slot - half, 0)])]
        return [(None, src.at[slot])]

    def _local(self, a, src):
        return pltpu.make_async_copy(src, self.pairs[a][1].at[self.me], self.local_sems.at[a])

    def _remote(self, a, k, src, slot):
        return pltpu.make_async_remote_copy(
            src_ref=src, dst_ref=self.pairs[a][1].at[slot], send_sem=self.send_sems.at[a, k],
            recv_sem=self.recv_sems.at[a, k], device_id=self.peers[k], device_id_type=MESH_ID)

    def start(self):
        def go(cond, copy):
            if cond is None:
                copy.start()
            else:
                pl.when(cond)(copy.start)

        for a in range(len(self.pairs)):
            for cond, src in self._parts(a, self.me):
                go(cond, self._local(a, src))
            for k in range(SAME_CORE_PEERS if self.pairs[a][2] else N_DEV - 1):
                for cond, src in self._parts(a, self.slots[k]):
                    go(cond, self._remote(a, k, src, self.me))

    def _pass_on(self, a, j):
        k = j + SAME_CORE_PEERS - 1
        block = self.pairs[a][1].at[self.slots[j]]
        return pltpu.make_async_remote_copy(
            src_ref=block, dst_ref=block, send_sem=self.send_sems.at[a, k], recv_sem=self.recv_sems.at[a, k],
            device_id=self.peers[0], device_id_type=MESH_ID)

    def wait(self):
        for a in range(len(self.pairs)):
            src = self._parts(a, self.me)[0][1]
            if self.pairs[a][2]:
                for j in range(1, SAME_CORE_PEERS):
                    self._remote(a, j, src, self.slots[j]).wait_recv()
                    self._pass_on(a, j).start()
                self._remote(a, 0, src, self.slots[0]).wait_recv()
            for k in range(SAME_CORE_PEERS if self.pairs[a][2] else 0, N_DEV - 1):
                self._remote(a, k, src, self.slots[k]).wait_recv()
        for a in range(len(self.pairs)):
            src = self._parts(a, self.me)[0][1]
            for k in range(SAME_CORE_PEERS if self.pairs[a][2] else N_DEV - 1):
                self._remote(a, k, src, self.me).wait_send()
            if self.pairs[a][2]:
                for j in range(1, SAME_CORE_PEERS):
                    self._pass_on(a, j).wait_send()
            self._local(a, src).wait()


ANY_SPEC = pl.BlockSpec(memory_space=pl.ANY)


def _exchange_shapes(gathers, swaps):
    return ([jax.ShapeDtypeStruct((N_DEV,) + a.shape, a.dtype) for a in gathers]
            + [jax.ShapeDtypeStruct(a.shape, a.dtype) for a in swaps])


def _exchange_of(ins, outs, sems, ng):
    return _Exchange(list(zip(ins[:ng], outs[:ng])), list(zip(ins[ng:], outs[ng:])), *sems)


def _exchange_call(gathers, swaps, name):
    ng, ns = len(gathers), len(swaps)
    n = ng + ns

    def body(*refs):
        ex = _exchange_of(refs[:n], refs[n:2 * n], refs[2 * n:], ng)
        ex.start()
        ex.wait()

    return _pcall(
        body, name=name, out_shape=_exchange_shapes(gathers, swaps),
        in_specs=[ANY_SPEC] * (ng + ns), out_specs=[ANY_SPEC] * (ng + ns),
        scratch_shapes=_Exchange.scratch(ng + ns),
    )(*gathers, *swaps)


def _in_proj(x, g1, w_in, cos, sin_s, tm, gathers):
    T = x.shape[0]
    ng = len(gathers)
    nt = T // tm

    def body(x_ref, g_ref, w_ref, cos_ref, sin_ref, *rest):
        ex = _exchange_of(rest[:ng], rest[ng + 8:2 * ng + 8], rest[2 * ng + 8:], ng)
        h_ref, qa_ref, ka_ref, va_ref, qr_ref, kr_ref, vr_ref, gr_ref = rest[ng:ng + 8]
        pl.when(pl.program_id(0) == 0)(ex.start)
        r, n = _rms_stats(x_ref[...])
        h = (n * g_ref[...]).astype(BF)
        h_ref[...] = h

        def proj(c0, w):
            return _dot(h, w_ref[:, c0:c0 + w])

        qa_ref[...] = proj(QA0, ATTN_W).astype(BF)
        kva = proj(KA0, 2 * KV_W)
        ka_ref[...] = kva[:, :KV_W].astype(BF)
        va_ref[...] = kva[:, KV_W:].astype(BF)
        vr_ref[...] = proj(VR0, RET_W).astype(BF)
        gr_ref[...] = proj(GR0, RET_W)
        cos_t, sin_t = cos_ref[...], sin_ref[...]
        even = lax.broadcasted_iota(jnp.int32, (tm, RET_HEAD_DIM), 1) % 2 == 0
        for c0, scale, out_ref in ((QR0, None, qr_ref), (KR0, RET_K_SCALE, kr_ref)):
            full = proj(c0, RET_W)
            for hd in range(N_RET_HEADS):
                cs = slice(hd * RET_HEAD_DIM, (hd + 1) * RET_HEAD_DIM)
                t = full[:, cs] if scale is None else full[:, cs] * scale
                out_ref[:, cs] = (t * cos_t + _rot(t, even) * sin_t).astype(BF)
        pl.when(pl.program_id(0) == nt - 1)(ex.wait)

    widths = [D_MODEL, ATTN_W, KV_W, KV_W, RET_W, RET_W, RET_W, RET_W]
    dts = [BF] * 7 + [F32]
    outs = _pcall(
        body, name="in_proj", grid=(nt,),
        in_specs=[_rows(tm, D_MODEL), _const((1, D_MODEL)), _vmem_full(), _rows(tm, RET_HEAD_DIM),
                  _rows(tm, RET_HEAD_DIM)] + [ANY_SPEC] * ng,
        out_specs=[_rows(tm, w) for w in widths] + [ANY_SPEC] * ng,
        out_shape=[jax.ShapeDtypeStruct((T, w), dt) for w, dt in zip(widths, dts)] + _exchange_shapes(gathers, []),
        scratch_shapes=_Exchange.scratch(ng),
        compiler_params=_params(("arbitrary",)),
    )(x, g1, w_in, cos, sin_s, *gathers)
    return outs[:8], outs[8:]


def _kv_variants(kk):
    kf = kk.astype(F32)
    lo = lax.broadcasted_iota(jnp.int32, kf.shape, 1) < HEAD_DIM
    h0_lo = jnp.where(lo, kf, 0.0)
    h1_hi = jnp.where(lo, 0.0, kf)
    h0_hi = pltpu.roll(h0_lo, HEAD_DIM, 1)
    h1_lo = pltpu.roll(h1_hi, HEAD_DIM, 1)
    return [[h0_lo.astype(BF), h0_hi.astype(BF)], [h1_lo.astype(BF), h1_hi.astype(BF)]]


def _col_to_tile(tile, col, head):
    lane = lax.broadcasted_iota(jnp.int32, tile.shape, 1)
    return jnp.where(lane == head, col, tile)


def _tri(rows, key_major=False):
    i = lax.broadcasted_iota(jnp.int32, (rows, BLOCK), 0) & (BLOCK - 1)
    j = lax.broadcasted_iota(jnp.int32, (rows, BLOCK), 1)
    return i > j if key_major else j > i


def _fold(x2, tri, first_above):
    a, b = x2[:, :BLOCK], x2[:, BLOCK:]
    return jnp.where(tri, a, b) if first_above else jnp.where(tri, b, a)


def _unfold(x, tri, first_above):
    up, low = jnp.where(tri, x, 0.0), jnp.where(tri, 0.0, x)
    return jnp.concatenate([up, low] if first_above else [low, up], axis=1).astype(BF)


def _scaled(q):
    return (q.astype(F32) * ATTN_SCALE).astype(BF)


def _cat_variants(prev, cur):
    return [[jnp.concatenate([prev[h][e], cur[h][e]], axis=0) for e in range(2)] for h in range(2)]


def _block_variants(prev_ref, cur_ref, nbs):
    var = [_kv_variants(prev_ref[...])] + [_kv_variants(cur_ref[b * BLOCK:(b + 1) * BLOCK, :]) for b in range(nbs)]
    return [_cat_variants(var[b], var[b + 1]) for b in range(nbs)]


def _head_cols(col, nbs):
    tiles = []
    for b in range(nbs):
        t = jnp.zeros((BLOCK, BLOCK), F32)
        for head in range(N_ATTN_HEADS):
            r0 = (b * N_ATTN_HEADS + head) * BLOCK
            t = _col_to_tile(t, col[r0:r0 + BLOCK, :], head)
        tiles.append(t)
    return tiles


def _attn_fwd(sinks, qa, ka, va, nbs, gathers):
    T = qa.shape[0]
    steps = T // (BLOCK * nbs)
    R = nbs * N_ATTN_HEADS * BLOCK
    ng = len(gathers)

    def body(sink_ref, q_ref, kc_ref, kp_ref, vc_ref, vp_ref, *rest):
        ex = _exchange_of(rest[:ng], rest[ng + 3:2 * ng + 3], rest[2 * ng + 3:], ng)
        a_ref, lse_ref, lset_ref = rest[ng:ng + 3]
        n = pl.program_id(0)
        pl.when(n == 0)(ex.start)
        kcat = _block_variants(kp_ref, kc_ref, nbs)
        vcat = _block_variants(vp_ref, vc_ref, nbs)
        tri1 = _tri(BLOCK)
        tiles = []
        for b in range(nbs):
            for pair in range(N_ATTN_HEADS // 2):
                qp = _scaled(q_ref[b * BLOCK:(b + 1) * BLOCK, pair * 128:(pair + 1) * 128])
                for e in range(2):
                    s = _fold(_dot_nt(qp, kcat[b][pair // 2][e]), tri1, True)
                    if b == 0:
                        s = jnp.where(tri1 & (n == 0), MASK_VALUE, s)
                    tiles.append(s)
        s = jnp.concatenate(tiles, axis=0)
        sink = jnp.concatenate([jnp.full((BLOCK, 1), sink_ref[head], F32)
                                for _ in range(nbs) for head in range(N_ATTN_HEADS)], axis=0)
        m = jnp.maximum(jnp.max(s, axis=-1, keepdims=True), sink)
        p = jnp.exp(s - m)
        z = jnp.sum(p, axis=-1, keepdims=True) + jnp.exp(sink - m)
        p2 = _unfold(p * (1.0 / z), _tri(R), True)
        for b in range(nbs):
            for pair in range(N_ATTN_HEADS // 2):
                r0 = (b * N_ATTN_HEADS + 2 * pair) * BLOCK
                acc = (_dot(p2[r0:r0 + BLOCK, :], vcat[b][pair // 2][0])
                       + _dot(p2[r0 + BLOCK:r0 + 2 * BLOCK, :], vcat[b][pair // 2][1]))
                a_ref[b * BLOCK:(b + 1) * BLOCK, pair * 128:(pair + 1) * 128] = acc.astype(BF)
        for b, t in enumerate(_head_cols(m + jnp.log(z), nbs)):
            lse_ref[b * BLOCK:(b + 1) * BLOCK, :] = t
            lset_ref[:, b * BLOCK:(b + 1) * BLOCK] = t.T[:N_ATTN_HEADS, :]
        pl.when(n == steps - 1)(ex.wait)

    cur = lambda w: pl.BlockSpec((BLOCK * nbs, w), lambda n: (n, 0))
    prev = lambda w: pl.BlockSpec((BLOCK, w), lambda n: (jnp.maximum(n * nbs - 1, 0), 0))
    outs = _pcall(
        body, name="attn_fwd", grid=(steps,),
        in_specs=[_smem_full(), cur(ATTN_W), cur(KV_W), prev(KV_W), cur(KV_W), prev(KV_W)] + [ANY_SPEC] * ng,
        out_specs=[cur(ATTN_W), cur(BLOCK), pl.BlockSpec((N_ATTN_HEADS, BLOCK * nbs), lambda n: (0, n))]
        + [ANY_SPEC] * ng,
        out_shape=[jax.ShapeDtypeStruct((T, ATTN_W), BF), jax.ShapeDtypeStruct((T, BLOCK), F32),
                   jax.ShapeDtypeStruct((N_ATTN_HEADS, T), F32)] + _exchange_shapes(gathers, []),
        scratch_shapes=_Exchange.scratch(ng),
        compiler_params=_params(("arbitrary",)),
    )(sinks, qa, ka, ka, va, va, *gathers)
    return outs[:3], outs[3:]


def _ret_fwd(decay, qr, kr, vr, gr, d_intra, xi_b, zeta_b, ncs):
    T = qr.shape[0]
    nc = T // BLOCK
    H, C = N_RET_HEADS, RET_HEAD_DIM

    def body(decay_ref, q_ref, k_ref, v_ref, g_ref, d_ref, xi_ref, zeta_ref, o_ref, s_ref, r_ref, state):
        @pl.when(pl.program_id(0) == 0)
        def _():
            state[...] = jnp.zeros_like(state)

        pairs = [(b, h) for b in range(ncs) for h in range(H)]
        sl = lambda b, h: (slice(b * BLOCK, (b + 1) * BLOCK), slice(h * C, (h + 1) * C))
        tab = lambda ref: jnp.concatenate([ref[h] for _, h in pairs], axis=0)
        q = [q_ref[sl(b, h)] for b, h in pairs]
        k = [k_ref[sl(b, h)] for b, h in pairs]
        v = [v_ref[sl(b, h)] for b, h in pairs]
        inner = (jnp.concatenate([_dot_nt(q[i], k[i]) for i in range(len(pairs))], axis=0) * tab(d_ref)).astype(BF)
        kz = (jnp.concatenate(k, axis=0).astype(F32) * tab(zeta_ref)).astype(BF)
        o1 = [_dot(inner[i * BLOCK:(i + 1) * BLOCK, :], v[i]) for i in range(len(pairs))]
        kv = [_dot_tn(kz[i * BLOCK:(i + 1) * BLOCK, :], v[i]) for i in range(len(pairs))]
        st_b = [None] * len(pairs)
        for h in range(H):
            st = state[h]
            for b in range(ncs):
                i = b * H + h
                st_b[i] = st.astype(BF)
                s_ref[b, h] = st_b[i]
                st = decay_ref[h] * st + kv[i]
            state[h] = st
        o2 = jnp.concatenate([_dot(q[i], st_b[i]) for i in range(len(pairs))], axis=0)
        o = jnp.concatenate(o1, axis=0) + o2 * tab(xi_ref)
        mu = jnp.mean(o, axis=-1, keepdims=True)
        oc = o - mu
        rs = lax.rsqrt(jnp.mean(oc * oc, axis=-1, keepdims=True) + GN_EPS)
        g = jnp.concatenate([g_ref[sl(b, h)] for b, h in pairs], axis=0)
        r = (g * jax.nn.sigmoid(g) * (oc * rs)).astype(BF)
        for i, (b, h) in enumerate(pairs):
            o_ref[sl(b, h)] = o[i * BLOCK:(i + 1) * BLOCK, :]
            r_ref[sl(b, h)] = r[i * BLOCK:(i + 1) * BLOCK, :]

    cur = pl.BlockSpec((BLOCK * ncs, RET_W), lambda n: (n, 0))
    tab = pl.BlockSpec((H, C, C), lambda n: (0, 0, 0))
    return _pcall(
        body, name="ret_fwd", grid=(nc // ncs,),
        in_specs=[_smem_full(), cur, cur, cur, cur, tab, tab, tab],
        out_specs=[cur, pl.BlockSpec((ncs, H, C, C), lambda n: (n, 0, 0, 0)), cur],
        out_shape=[jax.ShapeDtypeStruct((T, RET_W), F32), jax.ShapeDtypeStruct((nc, H, C, C), BF),
                   jax.ShapeDtypeStruct((T, RET_W), BF)],
        scratch_shapes=[pltpu.VMEM((H, C, C), F32)],
        compiler_params=_params(("arbitrary",)),
    )(decay, qr, kr, vr, gr, d_intra, xi_b, zeta_b)


def _out_proj(a, r, w_out, x, g2, g3, tm):
    T = x.shape[0]

    def body(a_ref, r_ref, w_ref, x_ref, g2_ref, g3_ref, mixed_ref, x1_ref, h2_ref):
        mixed = _dot(a_ref[...], w_ref[:ATTN_W, :]) + _dot(r_ref[...], w_ref[ATTN_W:, :])
        mixed_ref[...] = mixed
        _, n2 = _rms_stats(mixed)
        x1 = x_ref[...] + n2 * g2_ref[...]
        x1_ref[...] = x1
        _, n3 = _rms_stats(x1)
        h2_ref[...] = (n3 * g3_ref[...]).astype(BF)

    return _pcall(
        body, name="out_proj", grid=(T // tm,),
        in_specs=[_rows(tm, ATTN_W), _rows(tm, RET_W), _vmem_full(), _rows(tm, D_MODEL), _const((1, D_MODEL)),
                  _const((1, D_MODEL))],
        out_specs=[_rows(tm, D_MODEL)] * 3,
        out_shape=[jax.ShapeDtypeStruct((T, D_MODEL), F32), jax.ShapeDtypeStruct((T, D_MODEL), F32),
                   jax.ShapeDtypeStruct((T, D_MODEL), BF)],
        compiler_params=_params(("parallel",)),
    )(a, r, w_out, x, g2, g3)


def _shift_down(cur, k, before):
    out = pltpu.roll(cur, k, 0)
    row = lax.broadcasted_iota(jnp.int32, before.shape, 0)
    top = jnp.where(row < k, pltpu.roll(before, k, 0), out[0:8])
    return jnp.concatenate([top, out[8:]], axis=0)


def _shift_up(cur, k, after):
    tm = cur.shape[0]
    out = pltpu.roll(cur, tm - k, 0)
    row = lax.broadcasted_iota(jnp.int32, after.shape, 0)
    bot = jnp.where(row >= 8 - k, pltpu.roll(after, 8 - k, 0), out[tm - 8:])
    return jnp.concatenate([out[:tm - 8], bot], axis=0)


def _gelu_parts(x):
    m = (-2.0 * GELU_C * GELU_A) * (x * x)
    s = 1.0 / (1.0 + jnp.exp(x * (m - 2.0 * GELU_C)))
    gelu = x * s
    dgelu = s + gelu * (1.0 - s) * (2.0 * GELU_C - 3.0 * m)
    return gelu, dgelu


def _ffn_fwd(h2, w_up8, conv_w8, conv_b8, w_down4, x1, g4, target, tm):
    T = h2.shape[0]
    nt = T // tm

    def body(h_ref, wu_ref, cwg_ref, cwv_ref, cbg_ref, cbv_ref, wd_ref, x1_ref, g_ref, t_ref,
             upg_ref, upv_ref, ug_ref, uv_ref, y_ref, dout_ref, dz_ref, dg4_ref, loss_ref, halo, z_acc):
        s = pl.program_id(1)
        first = pl.program_id(0) == 0

        @pl.when(first & (s == 0))
        def _():
            loss_ref[...] = jnp.zeros_like(loss_ref)
            dg4_ref[...] = jnp.zeros_like(dg4_ref)

        h = h_ref[...]
        u = []
        parts = ((cwg_ref, cbg_ref, upg_ref, ug_ref), (cwv_ref, cbv_ref, upv_ref, uv_ref))
        for part, (cw_ref, cb_ref, up_ref, u_ref) in enumerate(parts):
            cur = _dot(h, wu_ref[s + part * FF_PAIRS])
            up_ref[0] = cur.astype(BF)
            before = jnp.where(first, 0.0, halo[part, s])
            halo[part, s] = cur[tm - 8:tm, :]
            u_c = (cw_ref[0, pl.ds(0, 1), :] * _shift_down(cur, 2, before)
                   + cw_ref[0, pl.ds(1, 1), :] * _shift_down(cur, 1, before)
                   + cw_ref[0, pl.ds(2, 1), :] * cur + cb_ref[0])
            u_ref[0] = u_c
            u.append(u_c)
        gelu, _ = _gelu_parts(u[0])
        y = (gelu * u[1]).astype(BF)
        y_ref[0] = y
        z_part = _dot(y, wd_ref[s])

        @pl.when(s == 0)
        def _():
            z_acc[...] = z_part

        @pl.when(s > 0)
        def _():
            z_acc[...] += z_part

        @pl.when(s == FF_PAIRS - 1)
        def _():
            r4, n4 = _rms_stats(z_acc[...])
            err = x1_ref[...] + n4 * g_ref[...] - t_ref[...]
            dout = err * (1.0 / D_MODEL)
            dout_ref[...] = dout
            loss_ref[...] += 0.5 * jnp.sum(jnp.mean(err * err, axis=-1, keepdims=True), axis=0, keepdims=True)
            dg4_ref[...] += jnp.sum(dout * n4, axis=0, keepdims=True)
            dz_ref[...] = _rms_bwd(n4, r4, dout * g_ref[...]).astype(BF)

    rows = pl.BlockSpec((tm, D_MODEL), lambda i, s: (i, 0))
    one = lambda shape: pl.BlockSpec(shape, lambda i, s: tuple(0 for _ in shape))
    gate = lambda r, w: pl.BlockSpec((1, r, w), lambda i, s: (s, 0, 0))
    val = lambda r, w: pl.BlockSpec((1, r, w), lambda i, s: (s + FF_PAIRS, 0, 0))
    tile = pl.BlockSpec((1, tm, FF_SHARD), lambda i, s: (s, i, 0))
    half = lambda dt: jax.ShapeDtypeStruct((FF_PAIRS, T, FF_SHARD), dt)
    return _pcall(
        body, name="ffn_fwd", grid=(nt, FF_PAIRS),
        in_specs=[rows, _vmem_full(), gate(3, FF_SHARD), val(3, FF_SHARD), gate(1, FF_SHARD), val(1, FF_SHARD),
                  _vmem_full(), rows, one((1, D_MODEL)), rows],
        out_specs=[tile] * 5 + [rows, rows, one((1, D_MODEL)), one((8, 128))],
        out_shape=[half(BF), half(BF), half(F32), half(F32), half(BF), jax.ShapeDtypeStruct((T, D_MODEL), F32),
                   jax.ShapeDtypeStruct((T, D_MODEL), BF), jax.ShapeDtypeStruct((1, D_MODEL), F32),
                   jax.ShapeDtypeStruct((8, 128), F32)],
        scratch_shapes=[pltpu.VMEM((2, FF_PAIRS, 8, FF_SHARD), F32), pltpu.VMEM((tm, D_MODEL), F32)],
        compiler_params=_params(("arbitrary", "arbitrary")),
    )(h2, w_up8, conv_w8, conv_w8, conv_b8, conv_b8, w_down4, x1, g4, target)


def _ffn_bwd_a(dz, h2, w_down4, u_g, u_v, up_g, up_v, y4, conv_w8, tm):
    T = dz.shape[0]
    nt = T // tm

    def body(dz_ref, h_ref, wd_ref, ug_ref, uv_ref, upg_ref, upv_ref, y_ref, cwg_ref, cwv_ref,
             dupg_ref, dupv_ref, dcbg_ref, dcbv_ref, dcwg_ref, dcwv_ref, gwug_out, gwuv_out, gwd_out,
             carry, gwug_ref, gwuv_ref, gwd_ref):
        @pl.when(pl.program_id(1) == 0)
        def _():
            for ref in (dcbg_ref, dcbv_ref, dcwg_ref, dcwv_ref, gwug_ref, gwuv_ref, gwd_ref, carry):
                ref[...] = jnp.zeros_like(ref)

        dz = dz_ref[...]
        h = h_ref[...]
        dy = _dot_nt(dz, wd_ref[0])
        gwd_ref[0] += _dot_tn(y_ref[0], dz)
        gelu, dgelu = _gelu_parts(ug_ref[0])
        parts = ((0, dy * uv_ref[0] * dgelu, upg_ref, cwg_ref, dupg_ref, dcbg_ref, dcwg_ref, gwug_ref),
                 (1, dy * gelu, upv_ref, cwv_ref, dupv_ref, dcbv_ref, dcwv_ref, gwuv_ref))
        for part, d, up_ref, cw_ref, dup_ref, dcb_ref, dcw_ref, gwu_ref in parts:
            after = carry[part]
            d1 = _shift_up(d, 1, after)
            d2 = _shift_up(d, 2, after)
            carry[part] = d[0:8, :]
            upc = up_ref[0].astype(F32)
            dcb_ref[0] += jnp.sum(d, axis=0, keepdims=True)
            dcw_ref[0, pl.ds(2, 1), :] += jnp.sum(d * upc, axis=0, keepdims=True)
            dcw_ref[0, pl.ds(1, 1), :] += jnp.sum(d1 * upc, axis=0, keepdims=True)
            dcw_ref[0, pl.ds(0, 1), :] += jnp.sum(d2 * upc, axis=0, keepdims=True)
            dup = (cw_ref[0, pl.ds(2, 1), :] * d + cw_ref[0, pl.ds(1, 1), :] * d1
                   + cw_ref[0, pl.ds(0, 1), :] * d2).astype(BF)
            dup_ref[0] = dup
            gwu_ref[0] += _dot_tn(dup, h)

        @pl.when(pl.program_id(1) == nt - 1)
        def _():
            s = pl.program_id(0)
            pltpu.sync_copy(gwug_ref, gwug_out.at[pl.ds(s, 1)])
            pltpu.sync_copy(gwuv_ref, gwuv_out.at[pl.ds(s, 1)])
            pltpu.sync_copy(gwd_ref, gwd_out.at[pl.ds(s, 1)])

    rev = pl.BlockSpec((tm, D_MODEL), lambda s, i: (nt - 1 - i, 0))
    tile = pl.BlockSpec((1, tm, FF_SHARD), lambda s, i: (s, nt - 1 - i, 0))
    acc = lambda r, w: pl.BlockSpec((1, r, w), lambda s, i: (s, 0, 0))
    acc_val = pl.BlockSpec((1, 3, FF_SHARD), lambda s, i: (s + FF_PAIRS, 0, 0))
    half = lambda r, dt: jax.ShapeDtypeStruct((FF_PAIRS, r, FF_SHARD), dt)
    return _pcall(
        body, name="ffn_bwd_a", grid=(FF_PAIRS, nt),
        in_specs=[rev, rev, acc(FF_SHARD, D_MODEL), tile, tile, tile, tile, tile, acc(3, FF_SHARD), acc_val],
        out_specs=[tile, tile, acc(1, FF_SHARD), acc(1, FF_SHARD), acc(3, FF_SHARD), acc(3, FF_SHARD),
                   ANY_SPEC, ANY_SPEC, ANY_SPEC],
        out_shape=[half(T, BF), half(T, BF), half(1, F32), half(1, F32), half(3, F32), half(3, F32),
                   jax.ShapeDtypeStruct((FF_PAIRS, FF_SHARD, D_MODEL), F32),
                   jax.ShapeDtypeStruct((FF_PAIRS, FF_SHARD, D_MODEL), F32),
                   jax.ShapeDtypeStruct((FF_PAIRS, FF_SHARD, D_MODEL), F32)],
        scratch_shapes=[pltpu.VMEM((2, 8, FF_SHARD), F32), pltpu.VMEM((1, FF_SHARD, D_MODEL), F32),
                        pltpu.VMEM((1, FF_SHARD, D_MODEL), F32), pltpu.VMEM((1, FF_SHARD, D_MODEL), F32)],
        compiler_params=_params(("arbitrary", "arbitrary")),
    )(dz, h2, w_down4, u_g, u_v, up_g, up_v, y4, conv_w8, conv_w8)


def _ffn_bwd_b(dup_g, dup_v, w_up8, x1, dout, g3, mixed, g2, w_out, tm, gwu_g, gwu_v):
    T = x1.shape[0]
    nt = T // tm

    def body(dupg_ref, dupv_ref, wup_ref, x1_ref, dout_ref, g3_ref, mixed_ref, g2_ref, wout_ref, gwug_ref, gwuv_ref,
             dx1_ref, dmixed_ref, da_ref, dr_ref, dg3_ref, dg2_ref, pup_ref, *sems):
        ex = _Exchange([], [((gwug_ref, gwuv_ref), pup_ref)], *sems)

        @pl.when(pl.program_id(0) == 0)
        def _():
            ex.start()
            dg3_ref[...] = jnp.zeros_like(dg3_ref)
            dg2_ref[...] = jnp.zeros_like(dg2_ref)

        dh2 = jnp.zeros((tm, D_MODEL), F32)
        for s in range(FF_PAIRS):
            dh2 = dh2 + _dot_nt(dupg_ref[s], wup_ref[s]) + _dot_nt(dupv_ref[s], wup_ref[s + FF_PAIRS])
        r3, n3 = _rms_stats(x1_ref[...])
        dg3_ref[...] += jnp.sum(dh2 * n3, axis=0, keepdims=True)
        dx1 = dout_ref[...] + _rms_bwd(n3, r3, dh2 * g3_ref[...])
        dx1_ref[...] = dx1
        r2, n2 = _rms_stats(mixed_ref[...])
        dg2_ref[...] += jnp.sum(dx1 * n2, axis=0, keepdims=True)
        dmixed = _rms_bwd(n2, r2, dx1 * g2_ref[...]).astype(BF)
        dmixed_ref[...] = dmixed
        da_ref[...] = _dot_nt(dmixed, wout_ref[:ATTN_W, :])
        dr_ref[...] = _dot_nt(dmixed, wout_ref[ATTN_W:, :])
        pl.when(pl.program_id(0) == nt - 1)(ex.wait)

    half = pl.BlockSpec((FF_PAIRS, tm, FF_SHARD), lambda i: (0, i, 0))
    outs = _pcall(
        body, name="ffn_bwd_b", grid=(nt,),
        in_specs=[half, half, _vmem_full(), _rows(tm, D_MODEL), _rows(tm, D_MODEL), _const((1, D_MODEL)),
                  _rows(tm, D_MODEL), _const((1, D_MODEL)), _vmem_full(), ANY_SPEC, ANY_SPEC],
        out_specs=[_rows(tm, D_MODEL), _rows(tm, D_MODEL), _rows(tm, ATTN_W), _rows(tm, RET_W),
                   _const((1, D_MODEL)), _const((1, D_MODEL)), ANY_SPEC],
        out_shape=[jax.ShapeDtypeStruct((T, D_MODEL), F32), jax.ShapeDtypeStruct((T, D_MODEL), BF),
                   jax.ShapeDtypeStruct((T, ATTN_W), F32), jax.ShapeDtypeStruct((T, RET_W), F32),
                   jax.ShapeDtypeStruct((1, D_MODEL), F32), jax.ShapeDtypeStruct((1, D_MODEL), F32),
                   jax.ShapeDtypeStruct((N_DEV, FF_SHARD, D_MODEL), F32)],
        scratch_shapes=_Exchange.scratch(1),
        compiler_params=_params(("arbitrary",)),
    )(dup_g, dup_v, w_up8, x1, dout, g3, mixed, g2, w_out, gwu_g, gwu_v)
    return outs[:6], outs[6]


def _ret_bwd(decay, qr, kr, vr, gr, o, states, dr, d_intra, d_intra_t, xi_b, zeta_b, cos, sin_s, swaps, ncs):
    T = qr.shape[0]
    nc = T // BLOCK
    H, C = N_RET_HEADS, RET_HEAD_DIM
    ns = len(swaps)

    def body(decay_ref, q_ref, k_ref, v_ref, g_ref, o_ref, s_ref, dr_ref, d_ref, dt_ref, xi_ref, zeta_ref,
             cos_ref, sin_ref, *rest):
        ex = _exchange_of(rest[:ns], rest[ns + 1:2 * ns + 1], rest[2 * ns + 2:], 0)
        dret_ref, gstate = rest[ns], rest[2 * ns + 1]

        @pl.when(pl.program_id(0) == 0)
        def _():
            ex.start()
            gstate[...] = jnp.zeros_like(gstate)

        pairs = [(b, h) for b in range(ncs) for h in range(H)]
        n = len(pairs)
        sl = lambda b, h: (slice(b * BLOCK, (b + 1) * BLOCK), slice(h * C, (h + 1) * C))
        cat = lambda ref: jnp.concatenate([ref[sl(b, h)] for b, h in pairs], axis=0)
        tab = lambda ref: jnp.concatenate([ref[h] for _, h in pairs], axis=0)
        part = lambda x, i: x[i * BLOCK:(i + 1) * BLOCK, :]
        q = [q_ref[sl(b, h)] for b, h in pairs]
        k = [k_ref[sl(b, h)] for b, h in pairs]
        v = [v_ref[sl(b, h)] for b, h in pairs]
        g, o_all, dr_all = cat(g_ref), cat(o_ref), cat(dr_ref)
        mu = jnp.mean(o_all, axis=-1, keepdims=True)
        oc = o_all - mu
        rs = lax.rsqrt(jnp.mean(oc * oc, axis=-1, keepdims=True) + GN_EPS)
        on = oc * rs
        sg = jax.nn.sigmoid(g)
        dg = (dr_all * on * (sg * (1.0 + g * (1.0 - sg)))).astype(BF)
        don = dr_all * (g * sg)
        do = rs * (don - jnp.mean(don, axis=-1, keepdims=True) - on * jnp.mean(don * on, axis=-1, keepdims=True))
        do_b = do.astype(BF)
        dox_b = (do * tab(xi_ref)).astype(BF)
        zeta = tab(zeta_ref)
        kz = (jnp.concatenate(k, axis=0).astype(F32) * zeta).astype(BF)
        d_t = tab(dt_ref)
        da_b = (jnp.concatenate([_dot_nt(part(do_b, i), v[i]) for i in range(n)], axis=0) * tab(d_ref)).astype(BF)
        dat_b = (jnp.concatenate([_dot_nt(v[i], part(do_b, i)) for i in range(n)], axis=0) * d_t).astype(BF)
        mt_b = (jnp.concatenate([_dot_nt(k[i], q[i]) for i in range(n)], axis=0) * d_t).astype(BF)
        dq = [_dot(part(da_b, i), k[i]) + _dot_nt(part(dox_b, i), s_ref[pairs[i]]) for i in range(n)]
        dk1 = [_dot(part(dat_b, i), q[i]) for i in range(n)]
        dv1 = [_dot(part(mt_b, i), part(do_b, i)) for i in range(n)]
        qtd = [_dot_tn(q[i], part(dox_b, i)) for i in range(n)]
        gst_b = [None] * n
        for h in range(H):
            gst = gstate[h]
            for b in reversed(range(ncs)):
                i = b * H + h
                gst_b[i] = gst.astype(BF)
                gst = decay_ref[h] * gst + qtd[i]
            gstate[h] = gst
        dk2 = jnp.concatenate([_dot_nt(v[i], gst_b[i]) for i in range(n)], axis=0) * zeta
        dv = jnp.concatenate([dv1[i] + _dot(part(kz, i), gst_b[i]) for i in range(n)], axis=0).astype(BF)
        even = lax.broadcasted_iota(jnp.int32, (n * BLOCK, C), 1) % 2 == 0
        cos_t = jnp.concatenate([cos_ref[b * BLOCK:(b + 1) * BLOCK, :] for b, _ in pairs], axis=0)
        sin_t = jnp.concatenate([sin_ref[b * BLOCK:(b + 1) * BLOCK, :] for b, _ in pairs], axis=0)
        dq = jnp.concatenate(dq, axis=0)
        dk = jnp.concatenate(dk1, axis=0) + dk2
        dq = (dq * cos_t - _rot(dq, even) * sin_t).astype(BF)
        dk = ((dk * cos_t - _rot(dk, even) * sin_t) * RET_K_SCALE).astype(BF)
        for i, (b, h) in enumerate(pairs):
            rows = slice(b * BLOCK, (b + 1) * BLOCK)
            for j, x in enumerate((dq, dk, dv, dg)):
                dret_ref[rows, j * RET_W + h * C:j * RET_W + (h + 1) * C] = part(x, i)
        pl.when(pl.program_id(0) == steps - 1)(ex.wait)

    steps = nc // ncs
    rev = lambda w: pl.BlockSpec((BLOCK * ncs, w), lambda n: (steps - 1 - n, 0))
    tab = pl.BlockSpec((H, C, C), lambda n: (0, 0, 0))
    outs = _pcall(
        body, name="ret_bwd", grid=(steps,),
        in_specs=[_smem_full(), rev(RET_W), rev(RET_W), rev(RET_W), rev(RET_W), rev(RET_W),
                  pl.BlockSpec((ncs, H, C, C), lambda n: (steps - 1 - n, 0, 0, 0)), rev(RET_W), tab, tab, tab, tab,
                  rev(C), rev(C)] + [ANY_SPEC] * ns,
        out_specs=[rev(4 * RET_W)] + [ANY_SPEC] * ns,
        out_shape=[jax.ShapeDtypeStruct((T, 4 * RET_W), BF)] + _exchange_shapes([], swaps),
        scratch_shapes=[pltpu.VMEM((H, C, C), F32)] + _Exchange.scratch(ns),
        compiler_params=_params(("arbitrary",)),
    )(decay, qr, kr, vr, gr, o, states, dr, d_intra, d_intra_t, xi_b, zeta_b, cos, sin_s, *swaps)
    return outs[0], outs[1:]


def _attn_bwd_dq(sinks, qa, ka, va, da, lse, nbs, swaps):
    T = qa.shape[0]
    steps = T // (BLOCK * nbs)
    R = nbs * N_ATTN_HEADS * BLOCK
    ns = len(swaps)

    def body(sink_ref, q_ref, kc_ref, kp_ref, vc_ref, vp_ref, da_ref, lse_ref, *rest):
        ex = _exchange_of(rest[:ns], rest[ns + 3:2 * ns + 3], rest[2 * ns + 3:], 0)
        dq_ref, deltat_ref, dsink_ref = rest[ns:ns + 3]
        n = pl.program_id(0)

        @pl.when(n == 0)
        def _():
            ex.start()
            dsink_ref[...] = jnp.zeros_like(dsink_ref)

        kcat = _block_variants(kp_ref, kc_ref, nbs)
        vcat = _block_variants(vp_ref, vc_ref, nbs)
        tri1 = _tri(BLOCK)
        lane = lax.broadcasted_iota(jnp.int32, (BLOCK, BLOCK), 1)
        s_tiles, dp_tiles, lse_cols = [], [], []
        for b in range(nbs):
            rows = slice(b * BLOCK, (b + 1) * BLOCK)
            lse_tile = lse_ref[rows, :]
            for pair in range(N_ATTN_HEADS // 2):
                qp = _scaled(q_ref[rows, pair * 128:(pair + 1) * 128])
                dop = da_ref[rows, pair * 128:(pair + 1) * 128].astype(BF)
                for e in range(2):
                    s = _fold(_dot_nt(qp, kcat[b][pair // 2][e]), tri1, True)
                    if b == 0:
                        s = jnp.where(tri1 & (n == 0), MASK_VALUE, s)
                    s_tiles.append(s)
                    dp_tiles.append(_fold(_dot_nt(dop, vcat[b][pair // 2][e]), tri1, True))
                    lse_cols.append(jnp.sum(jnp.where(lane == 2 * pair + e, lse_tile, 0.0), axis=-1, keepdims=True))
        lse_c = jnp.concatenate(lse_cols, axis=0)
        p = jnp.exp(jnp.concatenate(s_tiles, axis=0) - lse_c)
        dp = jnp.concatenate(dp_tiles, axis=0)
        delta = jnp.sum(p * dp, axis=-1, keepdims=True)
        ds2 = _unfold(p * (dp - delta), _tri(R), True)
        for b in range(nbs):
            for pair in range(N_ATTN_HEADS // 2):
                r0 = (b * N_ATTN_HEADS + 2 * pair) * BLOCK
                acc = (_dot(ds2[r0:r0 + BLOCK, :], kcat[b][pair // 2][0])
                       + _dot(ds2[r0 + BLOCK:r0 + 2 * BLOCK, :], kcat[b][pair // 2][1]))
                dq_ref[b * BLOCK:(b + 1) * BLOCK, pair * 128:(pair + 1) * 128] = (acc * ATTN_SCALE).astype(BF)
        for b, t in enumerate(_head_cols(delta, nbs)):
            deltat_ref[:, b * BLOCK:(b + 1) * BLOCK] = t.T[:N_ATTN_HEADS, :]
        sink = jnp.concatenate([jnp.full((BLOCK, 1), sink_ref[head], F32)
                                for _ in range(nbs) for head in range(N_ATTN_HEADS)], axis=0)
        ds_sink = -jnp.exp(sink - lse_c) * delta
        row8 = lax.broadcasted_iota(jnp.int32, (N_ATTN_HEADS, BLOCK), 0)
        dsink = jnp.zeros((N_ATTN_HEADS, BLOCK), F32)
        for b in range(nbs):
            for head in range(N_ATTN_HEADS):
                r0 = (b * N_ATTN_HEADS + head) * BLOCK
                dsink = dsink + jnp.where(row8 == head, jnp.sum(ds_sink[r0:r0 + BLOCK, :], axis=0, keepdims=True), 0.0)
        dsink_ref[...] += dsink
        pl.when(n == steps - 1)(ex.wait)

    cur = lambda w: pl.BlockSpec((BLOCK * nbs, w), lambda n: (n, 0))
    prev = lambda w: pl.BlockSpec((BLOCK, w), lambda n: (jnp.maximum(n * nbs - 1, 0), 0))
    outs = _pcall(
        body, name="attn_bwd_dq", grid=(steps,),
        in_specs=[_smem_full(), cur(ATTN_W), cur(KV_W), prev(KV_W), cur(KV_W), prev(KV_W), cur(ATTN_W), cur(BLOCK)]
        + [ANY_SPEC] * ns,
        out_specs=[cur(ATTN_W), pl.BlockSpec((N_ATTN_HEADS, BLOCK * nbs), lambda n: (0, n)),
                   _const((N_ATTN_HEADS, BLOCK))] + [ANY_SPEC] * ns,
        out_shape=[jax.ShapeDtypeStruct((T, ATTN_W), BF), jax.ShapeDtypeStruct((N_ATTN_HEADS, T), F32),
                   jax.ShapeDtypeStruct((N_ATTN_HEADS, BLOCK), F32)] + _exchange_shapes([], swaps),
        scratch_shapes=_Exchange.scratch(ns),
        compiler_params=_params(("arbitrary",)),
    )(sinks, qa, ka, ka, va, va, da, lse, *swaps)
    return outs[:3], outs[3:]


def _attn_bwd_dkv(qa, ka, va, da, lse_t, delta_t, nbs):
    T = qa.shape[0]
    nb = T // BLOCK
    steps = nb // nbs
    R = nbs * N_ATTN_HEADS * BLOCK

    def body(qc_ref, qn_ref, dac_ref, dan_ref, k_ref, v_ref, lc_ref, ln_ref, dc_ref, dn_ref, dk_ref, dv_ref):
        n = pl.program_id(0)
        tri1 = _tri(BLOCK, True)
        lo = lax.broadcasted_iota(jnp.int32, (BLOCK, 128), 1) < HEAD_DIM
        kv = [_kv_variants(k_ref[b * BLOCK:(b + 1) * BLOCK, :]) for b in range(nbs)]
        vv = [_kv_variants(v_ref[b * BLOCK:(b + 1) * BLOCK, :]) for b in range(nbs)]
        qcat, docat = [], []
        s_tiles, dp_tiles, lse_tiles, delta_tiles = [], [], [], []
        for b in range(nbs):
            rows = slice(b * BLOCK, (b + 1) * BLOCK)
            nrows = slice((b + 1) * BLOCK, (b + 2) * BLOCK)
            inside = b < nbs - 1
            for pair in range(N_ATTN_HEADS // 2):
                ps = slice(pair * 128, (pair + 1) * 128)
                q2 = _scaled(jnp.concatenate([qc_ref[rows, ps], qc_ref[nrows, ps] if inside else qn_ref[:, ps]], axis=0))
                do2 = jnp.concatenate([dac_ref[rows, ps], dac_ref[nrows, ps] if inside else dan_ref[:, ps]],
                                      axis=0).astype(BF)
                qcat.append(q2)
                docat.append(do2)
                for e in range(2):
                    one = pl.ds(2 * pair + e, 1)
                    s = _fold(_dot_nt(kv[b][pair // 2][e], q2), tri1, False)
                    if not inside:
                        s = jnp.where(tri1 & (n == steps - 1), MASK_VALUE, s)
                    s_tiles.append(s)
                    dp_tiles.append(_fold(_dot_nt(vv[b][pair // 2][e], do2), tri1, False))
                    lse_tiles.append(jnp.where(tri1, lc_ref[one, nrows] if inside else ln_ref[one, :], lc_ref[one, rows]))
                    delta_tiles.append(jnp.where(tri1, dc_ref[one, nrows] if inside else dn_ref[one, :],
                                                 dc_ref[one, rows]))
        pt = jnp.exp(jnp.concatenate(s_tiles, axis=0) - jnp.concatenate(lse_tiles, axis=0))
        dst = pt * (jnp.concatenate(dp_tiles, axis=0) - jnp.concatenate(delta_tiles, axis=0))
        tri = _tri(R, True)
        pt2 = _unfold(pt, tri, False)
        dst2 = _unfold(dst, tri, False)
        for b in range(nbs):
            dk = jnp.zeros((BLOCK, 128), F32)
            dv = jnp.zeros((BLOCK, 128), F32)
            for pair in range(N_ATTN_HEADS // 2):
                h = pair // 2
                for e in range(2):
                    r0 = (b * N_ATTN_HEADS + 2 * pair + e) * BLOCK
                    half = lo if e == 0 else jnp.logical_not(lo)
                    dv_e = jnp.where(half, _dot(pt2[r0:r0 + BLOCK, :], docat[b * 4 + pair]), 0.0)
                    dk_e = jnp.where(half, _dot(dst2[r0:r0 + BLOCK, :], qcat[b * 4 + pair]), 0.0)
                    if e != h:
                        dv_e = pltpu.roll(dv_e, HEAD_DIM, 1)
                        dk_e = pltpu.roll(dk_e, HEAD_DIM, 1)
                    dv = dv + dv_e
                    dk = dk + dk_e
            dk_ref[b * BLOCK:(b + 1) * BLOCK, :] = dk.astype(BF)
            dv_ref[b * BLOCK:(b + 1) * BLOCK, :] = dv.astype(BF)

    cur = lambda w: pl.BlockSpec((BLOCK * nbs, w), lambda n: (n, 0))
    nxt = lambda w: pl.BlockSpec((BLOCK, w), lambda n: (jnp.minimum((n + 1) * nbs, nb - 1), 0))
    tcur = pl.BlockSpec((N_ATTN_HEADS, BLOCK * nbs), lambda n: (0, n))
    tnxt = pl.BlockSpec((N_ATTN_HEADS, BLOCK), lambda n: (0, jnp.minimum((n + 1) * nbs, nb - 1)))
    return _pcall(
        body, name="attn_bwd_dkv", grid=(steps,),
        in_specs=[cur(ATTN_W), nxt(ATTN_W), cur(ATTN_W), nxt(ATTN_W), cur(KV_W), cur(KV_W), tcur, tnxt, tcur, tnxt],
        out_specs=[cur(KV_W), cur(KV_W)],
        out_shape=[jax.ShapeDtypeStruct((T, KV_W), BF), jax.ShapeDtypeStruct((T, KV_W), BF)],
        compiler_params=_params(("parallel",)),
    )(qa, qa, da, da, ka, va, lse_t, lse_t, delta_t, delta_t)


def _in_proj_bwd(dqa, dka, dva, dret, w_in, x, g1, dx1, tm):
    T = x.shape[0]

    def body(dqa_ref, dka_ref, dva_ref, dret_ref, w_ref, x_ref, g_ref, dx1_ref, dx_ref, dg1_ref):
        @pl.when(pl.program_id(0) == 0)
        def _():
            dg1_ref[...] = jnp.zeros_like(dg1_ref)

        dh = (_dot_nt(dqa_ref[...], w_ref[:, QA0:QA0 + ATTN_W]) + _dot_nt(dka_ref[...], w_ref[:, KA0:KA0 + KV_W])
              + _dot_nt(dva_ref[...], w_ref[:, VA0:VA0 + KV_W]) + _dot_nt(dret_ref[...], w_ref[:, QR0:IN_W]))
        r, n = _rms_stats(x_ref[...])
        dg1_ref[...] += jnp.sum(dh * n, axis=0, keepdims=True)
        dx_ref[...] = dx1_ref[...] + _rms_bwd(n, r, dh * g_ref[...])

    return _pcall(
        body, name="in_proj_bwd", grid=(T // tm,),
        in_specs=[_rows(tm, ATTN_W), _rows(tm, KV_W), _rows(tm, KV_W), _rows(tm, 4 * RET_W), _vmem_full(),
                  _rows(tm, D_MODEL), _const((1, D_MODEL)), _rows(tm, D_MODEL)],
        out_specs=[_rows(tm, D_MODEL), _const((1, D_MODEL))],
        out_shape=[jax.ShapeDtypeStruct((T, D_MODEL), F32), jax.ShapeDtypeStruct((1, D_MODEL), F32)],
        compiler_params=_params(("arbitrary",)),
    )(dqa, dka, dva, dret, w_in, x, g1, dx1)


def _wgrad(a_list, b_list, tk, name):
    T = a_list[0].shape[0]
    na, nbb = len(a_list), len(b_list)
    m_sizes = [a.shape[1] for a in a_list]
    n_sizes = [b.shape[1] for b in b_list]
    M, N = sum(m_sizes), sum(n_sizes)
    nk = T // tk
    chunk = 512

    def body(*refs):
        a_refs, b_refs = refs[:na], refs[na:na + nbb]
        out_ref, acc = refs[na + nbb], refs[na + nbb + 1]
        k = pl.program_id(0)

        @pl.when(k == 0)
        def _():
            acc[...] = jnp.zeros_like(acc)

        r0 = 0
        for ai in range(na):
            a = a_refs[ai][...]
            c0 = 0
            for bi in range(nbb):
                for s in range(0, n_sizes[bi], chunk):
                    w = min(chunk, n_sizes[bi] - s)
                    acc[r0:r0 + m_sizes[ai], c0 + s:c0 + s + w] += _dot_tn(a, b_refs[bi][:, s:s + w])
                c0 += n_sizes[bi]
            r0 += m_sizes[ai]

        @pl.when(k == nk - 1)
        def _():
            pltpu.sync_copy(acc, out_ref)

    return _pcall(
        body, name=name, grid=(nk,),
        in_specs=[_rows(tk, w) for w in m_sizes + n_sizes],
        out_specs=pl.BlockSpec(memory_space=pl.ANY),
        out_shape=jax.ShapeDtypeStruct((M, N), F32),
        scratch_shapes=[pltpu.VMEM((M, N), F32)],
        compiler_params=_params(("arbitrary",)),
    )(*a_list, *b_list)


def _adamw_math(w, g, m, v):
    m = ADAM_B1 * m + (1.0 - ADAM_B1) * g
    v = ADAM_B2 * v + (1.0 - ADAM_B2) * (g * g)
    m_hat = m / (1.0 - ADAM_B1 ** ADAM_STEP)
    v_hat = v / (1.0 - ADAM_B2 ** ADAM_STEP)
    delta = -ADAM_LR * (m_hat / (jnp.sqrt(v_hat) + ADAM_EPS) + ADAM_WD * w)
    return delta, m, v


def _sum_parts(parts_ref):
    g = parts_ref[0].astype(F32)
    for i in range(1, N_DEV):
        g = g + parts_ref[i].astype(F32)
    return g


def _adamw_shard(parts, w, m, v, tr, name):
    R, C = w.shape

    def body(p_ref, w_ref, m_ref, v_ref, g_ref, d_ref, nm_ref, nv_ref):
        g = _sum_parts(p_ref)
        g_ref[...] = g
        d_ref[...], nm_ref[...], nv_ref[...] = _adamw_math(w_ref[...], g, m_ref[...], v_ref[...])

    blk = pl.BlockSpec((tr, C), lambda i: (i, 0))
    return _pcall(
        body, name=name, grid=(R // tr,),
        in_specs=[pl.BlockSpec((N_DEV, tr, C), lambda i: (0, i, 0)), blk, blk, blk],
        out_specs=[blk] * 4,
        out_shape=[jax.ShapeDtypeStruct((R, C), F32)] * 4,
        compiler_params=_params(("parallel",)),
    )(parts, w, m, v)


def _sum_small(parts):
    def body(p_ref, g_ref):
        g_ref[...] = _sum_parts(p_ref)

    return _pcall(body, name="sum_small", out_shape=jax.ShapeDtypeStruct(parts.shape[1:], F32),
                  in_specs=[_vmem_full()], out_specs=_vmem_full())(parts)


def _adamw_small(g, w, m, v, name):
    def body(g_ref, w_ref, m_ref, v_ref, d_ref, nm_ref, nv_ref):
        d_ref[...], nm_ref[...], nv_ref[...] = _adamw_math(w_ref[...], g_ref[...], m_ref[...], v_ref[...])

    return _pcall(body, name=name, out_shape=[jax.ShapeDtypeStruct(w.shape, F32)] * 3,
                  in_specs=[_vmem_full()] * 4, out_specs=[_vmem_full()] * 3)(g, w, m, v)


def _tables(T):
    h, c = N_RET_HEADS, BLOCK
    pos = jnp.arange(T, dtype=F32)
    angle = 1.0 / jnp.power(10000.0, jnp.linspace(0.0, 1.0, RET_HEAD_DIM // 2, dtype=F32))
    angle = jnp.repeat(angle, 2)
    sin = jnp.sin(pos[:, None] * angle[None])
    cos = jnp.cos(pos[:, None] * angle[None])
    even = (jnp.arange(RET_HEAD_DIM) % 2 == 0)[None, :]
    sin_s = jnp.where(even, -sin, sin)
    log_gamma = jnp.log(1.0 - jnp.power(2.0, -5.0 - jnp.arange(h, dtype=F32)))
    idx = jnp.arange(c, dtype=F32)
    rel = idx[:, None] - idx[None, :]
    d_intra = jnp.where(rel[None] >= 0, jnp.exp(log_gamma[:, None, None] * jnp.maximum(rel, 0.0)[None]), 0.0)
    xi = jnp.exp(log_gamma[None, :] * (idx[:, None] + 1.0))
    zeta = jnp.exp(log_gamma[None, :] * (c - 1.0 - idx[:, None]))
    decay = jnp.exp(log_gamma * c)
    xi_b = jnp.broadcast_to(xi.T[:, :, None], (h, c, RET_HEAD_DIM))
    zeta_b = jnp.broadcast_to(zeta.T[:, :, None], (h, c, RET_HEAD_DIM))
    return cos, sin_s, d_intra, jnp.swapaxes(d_intra, 1, 2), xi_b, zeta_b, decay


def _from_shards(sh):
    n, r, cols = sh.shape
    return jnp.swapaxes(sh, 0, 1).reshape(r, n * cols)


SMALL_ROWS = 216


def _pack_small(gains, conv_b, conv_w, sinks, scalar=None):
    last = jnp.concatenate([sinks.reshape(1, 8), jnp.zeros((1, 1), F32) if scalar is None else scalar.reshape(1, 1)],
                           axis=1)
    parts = [g.reshape(8, 128) for g in gains] + [conv_b.reshape(44, 128), conv_w.reshape(132, 128),
                                                  jnp.pad(last, ((0, 0), (0, 119)))]
    packed = jnp.concatenate(parts, axis=0)
    return jnp.pad(packed, ((0, SMALL_ROWS - packed.shape[0]), (0, 0)))


def kernel(x, mix_pre_norm, w_in, attn_sinks, w_out, mix_post_norm, ffn_pre_norm, w_up, conv_w, conv_b, w_down, ffn_post_norm, loss_target, m_mix_pre_norm, m_w_in, m_attn_sinks, m_w_out, m_mix_post_norm, m_ffn_pre_norm, m_w_up, m_conv_w, m_conv_b, m_w_down, m_ffn_post_norm, v_mix_pre_norm, v_w_in, v_attn_sinks, v_w_out, v_mix_post_norm, v_ffn_pre_norm, v_w_up, v_conv_w, v_conv_b, v_w_down, v_ffn_post_norm):
    T = x.shape[1]
    tm = min(512, T)
    tm_big = min(1024, T)
    tk_grad = min(2048, T)
    nbs = min(8, T // BLOCK)
    x2 = x.reshape(T, D_MODEL)
    target = loss_target.reshape(T, D_MODEL)
    me = 4 * lax.axis_index("x") + 2 * lax.axis_index("y") + lax.axis_index("c")

    g_in, g_cw = _exchange_call([w_in[0].astype(BF), conv_w[0]], [], "gather_w_in")
    w_in_f = _from_shards(g_in)
    cos, sin_s, d_intra, d_intra_t, xi_b, zeta_b, decay = _tables(T)
    sinks = attn_sinks.reshape(N_ATTN_HEADS)

    (h1, qa, ka, va, qr, kr, vr, gr), (w_up8,) = _in_proj(
        x2, mix_pre_norm, w_in_f, cos, sin_s, tm_big, [w_up[0].astype(BF)])
    (a, lse, lse_t), (g_down, g_out) = _attn_fwd(sinks, qa, ka, va, nbs,
                                                 [w_down[0].astype(BF), w_out[0].astype(BF)])
    w_out_f = g_out.reshape(D_MODEL, D_MODEL)
    o, states, r = _ret_fwd(decay, qr, kr, vr, gr, d_intra, xi_b, zeta_b, nbs)
    mixed, x1, h2 = _out_proj(a, r, w_out_f, x2, mix_post_norm, ffn_pre_norm, tm_big)
    w_down4 = g_down.reshape(FF_PAIRS, FF_SHARD, D_MODEL)
    up_g, up_v, u_g, u_v, y4, dout, dz, dg4, loss_acc = _ffn_fwd(
        h2, w_up8, g_cw, conv_b.reshape(N_DEV, 1, FF_SHARD), w_down4, x1, ffn_post_norm, target, tm)

    dup_g, dup_v, dcb_g, dcb_v, dcw_g, dcw_v, gwu_g, gwu_v, gw_down4 = _ffn_bwd_a(
        dz, h2, w_down4, u_g, u_v, up_g, up_v, y4, g_cw, tm_big)
    dcb = jnp.concatenate([dcb_g, dcb_v], axis=0).reshape(1, 2 * D_FF)
    dcw = _from_shards(jnp.concatenate([dcw_g, dcw_v], axis=0))
    gw_down = gw_down4.reshape(D_FF, D_MODEL)
    (dx1, dmixed, da, dr, dg3, dg2), p_up = _ffn_bwd_b(
        dup_g, dup_v, w_up8, x1, dout, ffn_pre_norm, mixed, mix_post_norm, w_out_f, tm, gwu_g, gwu_v)
    gw_out = _wgrad([a, r], [dmixed], tk_grad, "wgrad_out")
    dret, (p_down,) = _ret_bwd(decay, qr, kr, vr, gr, o, states, dr, d_intra, d_intra_t, xi_b, zeta_b, cos, sin_s,
                               [gw_down.reshape(N_DEV, D_FF // N_DEV, D_MODEL)], nbs)
    (dqa, delta_t, dsink), (p_out,) = _attn_bwd_dq(sinks, qa, ka, va, da, lse, nbs,
                                                   [gw_out.reshape(N_DEV, D_MODEL // N_DEV, D_MODEL)])
    dka, dva = _attn_bwd_dkv(qa, ka, va, da, lse_t, delta_t, nbs)
    grad_x, dg1 = _in_proj_bwd(dqa, dka, dva, dret, w_in_f, x2, mix_pre_norm, dx1, tm)
    gw_in = _wgrad([h1], [dqa, dka, dva, dret], tk_grad, "wgrad_in")

    small = _pack_small([dg1, dg2, dg3, dg4], dcb, dcw, dsink[:, 0], loss_acc[0, 0])
    gw_in_t = gw_in.T.reshape(N_DEV, IN_W // N_DEV, D_MODEL).astype(BF)
    small_all, p_in = _exchange_call([small], [gw_in_t], "exchange_last")
    g_small = _sum_small(small_all)
    loss = g_small[208, N_ATTN_HEADS]

    t_in = lambda a: jnp.swapaxes(a, 1, 2)[0]
    g_w_in, d_w_in, nm_w_in, nv_w_in = [o.T for o in _adamw_shard(
        p_in, t_in(w_in), t_in(m_w_in), t_in(v_w_in), 176, "adamw_in")]
    g_w_up, d_w_up, nm_w_up, nv_w_up = [o.T for o in _adamw_shard(
        p_up, t_in(w_up), t_in(m_w_up), t_in(v_w_up), 176, "adamw_up")]
    g_w_out, d_w_out, nm_w_out, nv_w_out = _adamw_shard(p_out, w_out[0], m_w_out[0], v_w_out[0], 128, "adamw_out")
    g_w_down, d_w_down, nm_w_down, nv_w_down = _adamw_shard(p_down, w_down[0], m_w_down[0], v_w_down[0], 176,
                                                            "adamw_down")
    gains = [mix_pre_norm, mix_post_norm, ffn_pre_norm, ffn_post_norm]
    m_gains = [m_mix_pre_norm, m_mix_post_norm, m_ffn_pre_norm, m_ffn_post_norm]
    v_gains = [v_mix_pre_norm, v_mix_post_norm, v_ffn_pre_norm, v_ffn_post_norm]
    zeros_cw = jnp.zeros((3, 2 * D_FF), F32)
    w_small = _pack_small(gains, conv_b, zeros_cw, attn_sinks)
    m_small = _pack_small(m_gains, m_conv_b, zeros_cw, m_attn_sinks)
    v_small = _pack_small(v_gains, v_conv_b, zeros_cw, v_attn_sinks)
    d_small, nm_small, nv_small = _adamw_small(g_small, w_small, m_small, v_small, "adamw_small")
    shard_cols = 2 * D_FF // N_DEV
    g_cw = lax.dynamic_slice(g_small[76:208].reshape(3, 2 * D_FF), (0, me * shard_cols), (3, shard_cols))
    d_cw, nm_cw, nv_cw = _adamw_small(g_cw, conv_w[0], m_conv_w[0], v_conv_w[0], "adamw_conv_w")

    def unpack(p):
        gains_o = [p[8 * i:8 * i + 8].reshape(1, D_MODEL) for i in range(4)]
        return gains_o, p[32:76].reshape(1, 2 * D_FF), p[208:209, :N_ATTN_HEADS]

    def leaves(p, w_in_s, w_out_s, w_up_s, cw_s, w_down_s):
        (pre1, post1, pre2, post2), cb, sk = unpack(p)
        return [pre1, w_in_s[None], sk, w_out_s[None], post1, pre2, w_up_s[None], cw_s[None], cb, w_down_s[None],
                post2]

    return (loss, grad_x.reshape(1, T, D_MODEL),
            *leaves(g_small, g_w_in, g_w_out, g_w_up, g_cw, g_w_down),
            *leaves(d_small, d_w_in, d_w_out, d_w_up, d_cw, d_w_down),
            *leaves(nm_small, nm_w_in, nm_w_out, nm_w_up, nm_cw, nm_w_down),
            *leaves(nv_small, nv_w_in, nv_w_out, nv_w_up, nv_cw, nv_w_down))
```

```python
import functools
import math

import jax
import jax.numpy as jnp
from jax import lax
from jax.experimental import pallas as pl
from jax.experimental.pallas import tpu as pltpu

F32 = jnp.float32
BF = jnp.bfloat16

N_DEV = 8
D_MODEL = 1024
HEAD_DIM = 64
ATTN_W = 512
N_ATTN_HEADS = 8
KV_W = 128
BLOCK = 128
RET_W = 512
N_RET_HEADS = 4
RET_HEAD_DIM = 128
IN_W = 2816
D_FF = 2816
RMS_EPS = 1e-6
GN_EPS = 1e-6
MASK_VALUE = -1e30
ATTN_SCALE = HEAD_DIM ** -0.5
RET_K_SCALE = RET_HEAD_DIM ** -0.5
GELU_C = math.sqrt(2.0 / math.pi)
GELU_A = 0.044715

ADAM_LR = 0.001
ADAM_B1 = 0.9
ADAM_B2 = 0.999
ADAM_EPS = 1e-08
ADAM_WD = 0.01
ADAM_STEP = 10

VMEM_LIMIT_BYTES = 56 * 1024 * 1024
FF_SHARD = 2 * D_FF // N_DEV
FF_PAIRS = N_DEV // 2

QA0, KA0, VA0, QR0, KR0, VR0, GR0 = 0, 512, 640, 768, 1280, 1792, 2304

MESH_ID = pl.DeviceIdType.MESH


def _pcall(body, **kw):
    return pl.pallas_call(body, **kw)


def _params(sem=None):
    return pltpu.CompilerParams(dimension_semantics=sem, vmem_limit_bytes=VMEM_LIMIT_BYTES)


def _dot(a, b):
    return jnp.dot(a, b, preferred_element_type=F32)


def _dot_nt(a, b):
    return lax.dot_general(a, b, (((1,), (1,)), ((), ())), preferred_element_type=F32)


def _dot_tn(a, b):
    return lax.dot_general(a, b, (((0,), (0,)), ((), ())), preferred_element_type=F32)


def _vmem_full():
    return pl.BlockSpec(memory_space=pltpu.VMEM)


def _smem_full():
    return pl.BlockSpec(memory_space=pltpu.SMEM)


def _rows(tm, w):
    return pl.BlockSpec((tm, w), lambda i: (i, 0))


def _const(shape):
    return pl.BlockSpec(shape, lambda i: tuple(0 for _ in shape))


def _rms_stats(x):
    r = lax.rsqrt(jnp.mean(x * x, axis=-1, keepdims=True) + RMS_EPS)
    return r, x * r


def _rms_bwd(n, r, dn):
    return r * (dn - n * jnp.mean(dn * n, axis=-1, keepdims=True))


def _rot(x, even):
    w = x.shape[1]
    return jnp.where(even, pltpu.roll(x, w - 1, 1), pltpu.roll(x, 1, 1))


def _peers():
    x, y, c = lax.axis_index("x"), lax.axis_index("y"), lax.axis_index("c")
    flips = [(0, 0, 1), (1, 0, 0), (0, 1, 0), (1, 1, 0), (1, 0, 1), (0, 1, 1), (1, 1, 1)]
    peers = [(x ^ fx, y ^ fy, c ^ fc) for fx, fy, fc in flips]
    return 4 * x + 2 * y + c, peers


SAME_CORE_PEERS = 4


class _Exchange:
    def __init__(self, gathers, swaps, send_sems, recv_sems, local_sems):
        self.me, self.peers = _peers()
        self.slots = [4 * px + 2 * py + pc for px, py, pc in self.peers]
        self.pairs = [(src, dst, True) for src, dst in gathers] + [(src, dst, False) for src, dst in swaps]
        self.send_sems, self.recv_sems, self.local_sems = send_sems, recv_sems, local_sems

    @staticmethod
    def scratch(n):
        return [pltpu.SemaphoreType.DMA((n, N_DEV - 1)), pltpu.SemaphoreType.DMA((n, N_DEV - 1)),
                pltpu.SemaphoreType.DMA((n,))]

    def _parts(self, a, slot):
        src, _, whole = self.pairs[a]
        half = N_DEV // 2
        if whole:
            return [(None, src)]
        if isinstance(src, tuple):
            return [(slot < half, src[0].at[jnp.minimum(slot, half - 1)]),
                    (slot >= half, src[1].at[jnp.maximum(slot - half, 0)])]
        return [(None, src.at[slot])]

    def _local(self, a, src):
        return pltpu.make_async_copy(src, self.pairs[a][1].at[self.me], self.local_sems.at[a])

    def _remote(self, a, k, src, slot):
        return pltpu.make_async_remote_copy(
            src_ref=src, dst_ref=self.pairs[a][1].at[slot], send_sem=self.send_sems.at[a, k],
            recv_sem=self.recv_sems.at[a, k], device_id=self.peers[k], device_id_type=MESH_ID)

    def start(self):
        def go(cond, copy):
            if cond is None:
                copy.start()
            else:
                pl.when(cond)(copy.start)

        for a in range(len(self.pairs)):
            for cond, src in self._parts(a, self.me):
                go(cond, self._local(a, src))
            for k in range(SAME_CORE_PEERS if self.pairs[a][2] else N_DEV - 1):
                for cond, src in self._parts(a, self.slots[k]):
                    go(cond, self._remote(a, k, src, self.me))

    def _pass_on(self, a, j):
        k = j + SAME_CORE_PEERS - 1
        block = self.pairs[a][1].at[self.slots[j]]
        return pltpu.make_async_remote_copy(
            src_ref=block, dst_ref=block, send_sem=self.send_sems.at[a, k], recv_sem=self.recv_sems.at[a, k],
            device_id=self.peers[0], device_id_type=MESH_ID)

    def wait(self):
        for a in range(len(self.pairs)):
            src = self._parts(a, self.me)[0][1]
            if self.pairs[a][2]:
                for j in range(1, SAME_CORE_PEERS):
                    self._remote(a, j, src, self.slots[j]).wait_recv()
                    self._pass_on(a, j).start()
                self._remote(a, 0, src, self.slots[0]).wait_recv()
            for k in range(SAME_CORE_PEERS if self.pairs[a][2] else 0, N_DEV - 1):
                self._remote(a, k, src, self.slots[k]).wait_recv()
        for a in range(len(self.pairs)):
            src = self._parts(a, self.me)[0][1]
            for k in range(SAME_CORE_PEERS if self.pairs[a][2] else N_DEV - 1):
                self._remote(a, k, src, self.me).wait_send()
            if self.pairs[a][2]:
                for j in range(1, SAME_CORE_PEERS):
                    self._pass_on(a, j).wait_send()
            self._local(a, src).wait()


ANY_SPEC = pl.BlockSpec(memory_space=pl.ANY)


def _exchange_shapes(gathers, swaps):
    return ([jax.ShapeDtypeStruct((N_DEV,) + a.shape, a.dtype) for a in gathers]
            + [jax.ShapeDtypeStruct(a.shape, a.dtype) for a in swaps])


def _exchange_of(ins, outs, sems, ng):
    return _Exchange(list(zip(ins[:ng], outs[:ng])), list(zip(ins[ng:], outs[ng:])), *sems)


def _exchange_call(gathers, swaps, name):
    ng, ns = len(gathers), len(swaps)
    n = ng + ns

    def body(*refs):
        ex = _exchange_of(refs[:n], refs[n:2 * n], refs[2 * n:], ng)
        ex.start()
        ex.wait()

    return _pcall(
        body, name=name, out_shape=_exchange_shapes(gathers, swaps),
        in_specs=[ANY_SPEC] * (ng + ns), out_specs=[ANY_SPEC] * (ng + ns),
        scratch_shapes=_Exchange.scratch(ng + ns),
    )(*gathers, *swaps)


def _in_proj(x, g1, w_in, cos, sin_s, tm, gathers):
    T = x.shape[0]
    ng = len(gathers)
    nt = T // tm

    def body(x_ref, g_ref, w_ref, cos_ref, sin_ref, *rest):
        ex = _exchange_of(rest[:ng], rest[ng + 8:2 * ng + 8], rest[2 * ng + 8:], ng)
        h_ref, qa_ref, ka_ref, va_ref, qr_ref, kr_ref, vr_ref, gr_ref = rest[ng:ng + 8]
        pl.when(pl.program_id(0) == 0)(ex.start)
        r, n = _rms_stats(x_ref[...])
        h = (n * g_ref[...]).astype(BF)
        h_ref[...] = h

        def proj(c0, w):
            return _dot(h, w_ref[:, c0:c0 + w])

        qa_ref[...] = proj(QA0, ATTN_W).astype(BF)
        kva = proj(KA0, 2 * KV_W)
        ka_ref[...] = kva[:, :KV_W].astype(BF)
        va_ref[...] = kva[:, KV_W:].astype(BF)
        vr_ref[...] = proj(VR0, RET_W).astype(BF)
        gr_ref[...] = proj(GR0, RET_W)
        cos_t, sin_t = cos_ref[...], sin_ref[...]
        even = lax.broadcasted_iota(jnp.int32, (tm, RET_HEAD_DIM), 1) % 2 == 0
        for c0, scale, out_ref in ((QR0, None, qr_ref), (KR0, RET_K_SCALE, kr_ref)):
            full = proj(c0, RET_W)
            for hd in range(N_RET_HEADS):
                cs = slice(hd * RET_HEAD_DIM, (hd + 1) * RET_HEAD_DIM)
                t = full[:, cs] if scale is None else full[:, cs] * scale
                out_ref[:, cs] = (t * cos_t + _rot(t, even) * sin_t).astype(BF)
        pl.when(pl.program_id(0) == nt - 1)(ex.wait)

    widths = [D_MODEL, ATTN_W, KV_W, KV_W, RET_W, RET_W, RET_W, RET_W]
    dts = [BF] * 7 + [F32]
    outs = _pcall(
        body, name="in_proj", grid=(nt,),
        in_specs=[_rows(tm, D_MODEL), _const((1, D_MODEL)), _vmem_full(), _rows(tm, RET_HEAD_DIM),
                  _rows(tm, RET_HEAD_DIM)] + [ANY_SPEC] * ng,
        out_specs=[_rows(tm, w) for w in widths] + [ANY_SPEC] * ng,
        out_shape=[jax.ShapeDtypeStruct((T, w), dt) for w, dt in zip(widths, dts)] + _exchange_shapes(gathers, []),
        scratch_shapes=_Exchange.scratch(ng),
        compiler_params=_params(("arbitrary",)),
    )(x, g1, w_in, cos, sin_s, *gathers)
    return outs[:8], outs[8:]


def _kv_variants(kk):
    kf = kk.astype(F32)
    lo = lax.broadcasted_iota(jnp.int32, kf.shape, 1) < HEAD_DIM
    h0_lo = jnp.where(lo, kf, 0.0)
    h1_hi = jnp.where(lo, 0.0, kf)
    h0_hi = pltpu.roll(h0_lo, HEAD_DIM, 1)
    h1_lo = pltpu.roll(h1_hi, HEAD_DIM, 1)
    return [[h0_lo.astype(BF), h0_hi.astype(BF)], [h1_lo.astype(BF), h1_hi.astype(BF)]]


def _col_to_tile(tile, col, head):
    lane = lax.broadcasted_iota(jnp.int32, tile.shape, 1)
    return jnp.where(lane == head, col, tile)


def _tri(rows, key_major=False):
    i = lax.broadcasted_iota(jnp.int32, (rows, BLOCK), 0) & (BLOCK - 1)
    j = lax.broadcasted_iota(jnp.int32, (rows, BLOCK), 1)
    return i > j if key_major else j > i


def _fold(x2, tri, first_above):
    a, b = x2[:, :BLOCK], x2[:, BLOCK:]
    return jnp.where(tri, a, b) if first_above else jnp.where(tri, b, a)


def _unfold(x, tri, first_above):
    up, low = jnp.where(tri, x, 0.0), jnp.where(tri, 0.0, x)
    return jnp.concatenate([up, low] if first_above else [low, up], axis=1).astype(BF)


def _scaled(q):
    return (q.astype(F32) * ATTN_SCALE).astype(BF)


def _cat_variants(prev, cur):
    return [[jnp.concatenate([prev[h][e], cur[h][e]], axis=0) for e in range(2)] for h in range(2)]


def _block_variants(prev_ref, cur_ref, nbs):
    var = [_kv_variants(prev_ref[...])] + [_kv_variants(cur_ref[b * BLOCK:(b + 1) * BLOCK, :]) for b in range(nbs)]
    return [_cat_variants(var[b], var[b + 1]) for b in range(nbs)]


def _head_cols(col, nbs):
    tiles = []
    for b in range(nbs):
        t = jnp.zeros((BLOCK, BLOCK), F32)
        for head in range(N_ATTN_HEADS):
            r0 = (b * N_ATTN_HEADS + head) * BLOCK
            t = _col_to_tile(t, col[r0:r0 + BLOCK, :], head)
        tiles.append(t)
    return tiles


def _attn_fwd(sinks, qa, ka, va, nbs, gathers):
    T = qa.shape[0]
    steps = T // (BLOCK * nbs)
    R = nbs * N_ATTN_HEADS * BLOCK
    ng = len(gathers)

    def body(sink_ref, q_ref, kc_ref, kp_ref, vc_ref, vp_ref, *rest):
        ex = _exchange_of(rest[:ng], rest[ng + 3:2 * ng + 3], rest[2 * ng + 3:], ng)
        a_ref, lse_ref, lset_ref = rest[ng:ng + 3]
        n = pl.program_id(0)
        pl.when(n == 0)(ex.start)
        kcat = _block_variants(kp_ref, kc_ref, nbs)
        vcat = _block_variants(vp_ref, vc_ref, nbs)
        tri1 = _tri(BLOCK)
        tiles = []
        for b in range(nbs):
            for pair in range(N_ATTN_HEADS // 2):
                qp = _scaled(q_ref[b * BLOCK:(b + 1) * BLOCK, pair * 128:(pair + 1) * 128])
                for e in range(2):
                    s = _fold(_dot_nt(qp, kcat[b][pair // 2][e]), tri1, True)
                    if b == 0:
                        s = jnp.where(tri1 & (n == 0), MASK_VALUE, s)
                    tiles.append(s)
        s = jnp.concatenate(tiles, axis=0)
        sink = jnp.concatenate([jnp.full((BLOCK, 1), sink_ref[head], F32)
                                for _ in range(nbs) for head in range(N_ATTN_HEADS)], axis=0)
        m = jnp.maximum(jnp.max(s, axis=-1, keepdims=True), sink)
        p = jnp.exp(s - m)
        z = jnp.sum(p, axis=-1, keepdims=True) + jnp.exp(sink - m)
        p2 = _unfold(p * (1.0 / z), _tri(R), True)
        for b in range(nbs):
            for pair in range(N_ATTN_HEADS // 2):
                r0 = (b * N_ATTN_HEADS + 2 * pair) * BLOCK
                acc = (_dot(p2[r0:r0 + BLOCK, :], vcat[b][pair // 2][0])
                       + _dot(p2[r0 + BLOCK:r0 + 2 * BLOCK, :], vcat[b][pair // 2][1]))
                a_ref[b * BLOCK:(b + 1) * BLOCK, pair * 128:(pair + 1) * 128] = acc.astype(BF)
        for b, t in enumerate(_head_cols(m + jnp.log(z), nbs)):
            lse_ref[b * BLOCK:(b + 1) * BLOCK, :] = t
            lset_ref[:, b * BLOCK:(b + 1) * BLOCK] = t.T[:N_ATTN_HEADS, :]
        pl.when(n == steps - 1)(ex.wait)

    cur = lambda w: pl.BlockSpec((BLOCK * nbs, w), lambda n: (n, 0))
    prev = lambda w: pl.BlockSpec((BLOCK, w), lambda n: (jnp.maximum(n * nbs - 1, 0), 0))
    outs = _pcall(
        body, name="attn_fwd", grid=(steps,),
        in_specs=[_smem_full(), cur(ATTN_W), cur(KV_W), prev(KV_W), cur(KV_W), prev(KV_W)] + [ANY_SPEC] * ng,
        out_specs=[cur(ATTN_W), cur(BLOCK), pl.BlockSpec((N_ATTN_HEADS, BLOCK * nbs), lambda n: (0, n))]
        + [ANY_SPEC] * ng,
        out_shape=[jax.ShapeDtypeStruct((T, ATTN_W), BF), jax.ShapeDtypeStruct((T, BLOCK), F32),
                   jax.ShapeDtypeStruct((N_ATTN_HEADS, T), F32)] + _exchange_shapes(gathers, []),
        scratch_shapes=_Exchange.scratch(ng),
        compiler_params=_params(("arbitrary",)),
    )(sinks, qa, ka, ka, va, va, *gathers)
    return outs[:3], outs[3:]


def _ret_fwd(decay, qr, kr, vr, gr, d_intra, xi_b, zeta_b, ncs):
    T = qr.shape[0]
    nc = T // BLOCK
    H, C = N_RET_HEADS, RET_HEAD_DIM

    def body(decay_ref, q_ref, k_ref, v_ref, g_ref, d_ref, xi_ref, zeta_ref, o_ref, s_ref, r_ref, state):
        @pl.when(pl.program_id(0) == 0)
        def _():
            state[...] = jnp.zeros_like(state)

        pairs = [(b, h) for b in range(ncs) for h in range(H)]
        sl = lambda b, h: (slice(b * BLOCK, (b + 1) * BLOCK), slice(h * C, (h + 1) * C))
        tab = lambda ref: jnp.concatenate([ref[h] for _, h in pairs], axis=0)
        q = [q_ref[sl(b, h)] for b, h in pairs]
        k = [k_ref[sl(b, h)] for b, h in pairs]
        v = [v_ref[sl(b, h)] for b, h in pairs]
        inner = (jnp.concatenate([_dot_nt(q[i], k[i]) for i in range(len(pairs))], axis=0) * tab(d_ref)).astype(BF)
        kz = (jnp.concatenate(k, axis=0).astype(F32) * tab(zeta_ref)).astype(BF)
        o1 = [_dot(inner[i * BLOCK:(i + 1) * BLOCK, :], v[i]) for i in range(len(pairs))]
        kv = [_dot_tn(kz[i * BLOCK:(i + 1) * BLOCK, :], v[i]) for i in range(len(pairs))]
        st_b = [None] * len(pairs)
        for h in range(H):
            st = state[h]
            for b in range(ncs):
                i = b * H + h
                st_b[i] = st.astype(BF)
                s_ref[b, h] = st_b[i]
                st = decay_ref[h] * st + kv[i]
            state[h] = st
        o2 = jnp.concatenate([_dot(q[i], st_b[i]) for i in range(len(pairs))], axis=0)
        o = jnp.concatenate(o1, axis=0) + o2 * tab(xi_ref)
        mu = jnp.mean(o, axis=-1, keepdims=True)
        oc = o - mu
        rs = lax.rsqrt(jnp.mean(oc * oc, axis=-1, keepdims=True) + GN_EPS)
        g = jnp.concatenate([g_ref[sl(b, h)] for b, h in pairs], axis=0)
        r = (g * jax.nn.sigmoid(g) * (oc * rs)).astype(BF)
        for i, (b, h) in enumerate(pairs):
            o_ref[sl(b, h)] = o[i * BLOCK:(i + 1) * BLOCK, :]
            r_ref[sl(b, h)] = r[i * BLOCK:(i + 1) * BLOCK, :]

    cur = pl.BlockSpec((BLOCK * ncs, RET_W), lambda n: (n, 0))
    tab = pl.BlockSpec((H, C, C), lambda n: (0, 0, 0))
    return _pcall(
        body, name="ret_fwd", grid=(nc // ncs,),
        in_specs=[_smem_full(), cur, cur, cur, cur, tab, tab, tab],
        out_specs=[cur, pl.BlockSpec((ncs, H, C, C), lambda n: (n, 0, 0, 0)), cur],
        out_shape=[jax.ShapeDtypeStruct((T, RET_W), F32), jax.ShapeDtypeStruct((nc, H, C, C), BF),
                   jax.ShapeDtypeStruct((T, RET_W), BF)],
        scratch_shapes=[pltpu.VMEM((H, C, C), F32)],
        compiler_params=_params(("arbitrary",)),
    )(decay, qr, kr, vr, gr, d_intra, xi_b, zeta_b)


def _out_proj(a, r, w_out, x, g2, g3, tm):
    T = x.shape[0]

    def body(a_ref, r_ref, w_ref, x_ref, g2_ref, g3_ref, mixed_ref, x1_ref, h2_ref):
        mixed = _dot(a_ref[...], w_ref[:ATTN_W, :]) + _dot(r_ref[...], w_ref[ATTN_W:, :])
        mixed_ref[...] = mixed
        _, n2 = _rms_stats(mixed)
        x1 = x_ref[...] + n2 * g2_ref[...]
        x1_ref[...] = x1
        _, n3 = _rms_stats(x1)
        h2_ref[...] = (n3 * g3_ref[...]).astype(BF)

    return _pcall(
        body, name="out_proj", grid=(T // tm,),
        in_specs=[_rows(tm, ATTN_W), _rows(tm, RET_W), _vmem_full(), _rows(tm, D_MODEL), _const((1, D_MODEL)),
                  _const((1, D_MODEL))],
        out_specs=[_rows(tm, D_MODEL)] * 3,
        out_shape=[jax.ShapeDtypeStruct((T, D_MODEL), F32), jax.ShapeDtypeStruct((T, D_MODEL), F32),
                   jax.ShapeDtypeStruct((T, D_MODEL), BF)],
        compiler_params=_params(("parallel",)),
    )(a, r, w_out, x, g2, g3)


def _shift_down(cur, k, before):
    out = pltpu.roll(cur, k, 0)
    row = lax.broadcasted_iota(jnp.int32, before.shape, 0)
    top = jnp.where(row < k, pltpu.roll(before, k, 0), out[0:8])
    return jnp.concatenate([top, out[8:]], axis=0)


def _shift_up(cur, k, after):
    tm = cur.shape[0]
    out = pltpu.roll(cur, tm - k, 0)
    row = lax.broadcasted_iota(jnp.int32, after.shape, 0)
    bot = jnp.where(row >= 8 - k, pltpu.roll(after, 8 - k, 0), out[tm - 8:])
    return jnp.concatenate([out[:tm - 8], bot], axis=0)


def _gelu_parts(x):
    m = (-2.0 * GELU_C * GELU_A) * (x * x)
    s = 1.0 / (1.0 + jnp.exp(x * (m - 2.0 * GELU_C)))
    gelu = x * s
    dgelu = s + gelu * (1.0 - s) * (2.0 * GELU_C - 3.0 * m)
    return gelu, dgelu


def _ffn_fwd(h2, w_up8, conv_w8, conv_b8, w_down4, x1, g4, target, tm):
    T = h2.shape[0]
    nt = T // tm

    def body(h_ref, wu_ref, cwg_ref, cwv_ref, cbg_ref, cbv_ref, wd_ref, x1_ref, g_ref, t_ref,
             upg_ref, upv_ref, ug_ref, uv_ref, y_ref, dout_ref, dz_ref, dg4_ref, loss_ref, halo, z_acc):
        s = pl.program_id(1)
        first = pl.program_id(0) == 0

        @pl.when(first & (s == 0))
        def _():
            loss_ref[...] = jnp.zeros_like(loss_ref)
            dg4_ref[...] = jnp.zeros_like(dg4_ref)

        h = h_ref[...]
        u = []
        parts = ((cwg_ref, cbg_ref, upg_ref, ug_ref), (cwv_ref, cbv_ref, upv_ref, uv_ref))
        for part, (cw_ref, cb_ref, up_ref, u_ref) in enumerate(parts):
            cur = _dot(h, wu_ref[s + part * FF_PAIRS])
            up_ref[0] = cur.astype(BF)
            before = jnp.where(first, 0.0, halo[part, s])
            halo[part, s] = cur[tm - 8:tm, :]
            u_c = (cw_ref[0, pl.ds(0, 1), :] * _shift_down(cur, 2, before)
                   + cw_ref[0, pl.ds(1, 1), :] * _shift_down(cur, 1, before)
                   + cw_ref[0, pl.ds(2, 1), :] * cur + cb_ref[0])
            u_ref[0] = u_c
            u.append(u_c)
        gelu, _ = _gelu_parts(u[0])
        y = (gelu * u[1]).astype(BF)
        y_ref[0] = y
        z_part = _dot(y, wd_ref[s])

        @pl.when(s == 0)
        def _():
            z_acc[...] = z_part

        @pl.when(s > 0)
        def _():
            z_acc[...] += z_part

        @pl.when(s == FF_PAIRS - 1)
        def _():
            r4, n4 = _rms_stats(z_acc[...])
            err = x1_ref[...] + n4 * g_ref[...] - t_ref[...]
            dout = err * (1.0 / D_MODEL)
            dout_ref[...] = dout
            loss_ref[...] += 0.5 * jnp.sum(jnp.mean(err * err, axis=-1, keepdims=True), axis=0, keepdims=True)
            dg4_ref[...] += jnp.sum(dout * n4, axis=0, keepdims=True)
            dz_ref[...] = _rms_bwd(n4, r4, dout * g_ref[...]).astype(BF)

    rows = pl.BlockSpec((tm, D_MODEL), lambda i, s: (i, 0))
    one = lambda shape: pl.BlockSpec(shape, lambda i, s: tuple(0 for _ in shape))
    gate = lambda r, w: pl.BlockSpec((1, r, w), lambda i, s: (s, 0, 0))
    val = lambda r, w: pl.BlockSpec((1, r, w), lambda i, s: (s + FF_PAIRS, 0, 0))
    tile = pl.BlockSpec((1, tm, FF_SHARD), lambda i, s: (s, i, 0))
    half = lambda dt: jax.ShapeDtypeStruct((FF_PAIRS, T, FF_SHARD), dt)
    return _pcall(
        body, name="ffn_fwd", grid=(nt, FF_PAIRS),
        in_specs=[rows, _vmem_full(), gate(3, FF_SHARD), val(3, FF_SHARD), gate(1, FF_SHARD), val(1, FF_SHARD),
                  _vmem_full(), rows, one((1, D_MODEL)), rows],
        out_specs=[tile] * 5 + [rows, rows, one((1, D_MODEL)), one((8, 128))],
        out_shape=[half(BF), half(BF), half(F32), half(F32), half(BF), jax.ShapeDtypeStruct((T, D_MODEL), F32),
                   jax.ShapeDtypeStruct((T, D_MODEL), BF), jax.ShapeDtypeStruct((1, D_MODEL), F32),
                   jax.ShapeDtypeStruct((8, 128), F32)],
        scratch_shapes=[pltpu.VMEM((2, FF_PAIRS, 8, FF_SHARD), F32), pltpu.VMEM((tm, D_MODEL), F32)],
        compiler_params=_params(("arbitrary", "arbitrary")),
    )(h2, w_up8, conv_w8, conv_w8, conv_b8, conv_b8, w_down4, x1, g4, target)


def _ffn_bwd_a(dz, h2, w_down4, u_g, u_v, up_g, up_v, y4, conv_w8, tm):
    T = dz.shape[0]
    nt = T // tm

    def body(dz_ref, h_ref, wd_ref, ug_ref, uv_ref, upg_ref, upv_ref, y_ref, cwg_ref, cwv_ref,
             dupg_ref, dupv_ref, dcbg_ref, dcbv_ref, dcwg_ref, dcwv_ref, gwug_out, gwuv_out, gwd_out,
             carry, gwug_ref, gwuv_ref, gwd_ref):
        @pl.when(pl.program_id(1) == 0)
        def _():
            for ref in (dcbg_ref, dcbv_ref, dcwg_ref, dcwv_ref, gwug_ref, gwuv_ref, gwd_ref, carry):
                ref[...] = jnp.zeros_like(ref)

        dz = dz_ref[...]
        h = h_ref[...]
        dy = _dot_nt(dz, wd_ref[0])
        gwd_ref[0] += _dot_tn(y_ref[0], dz)
        gelu, dgelu = _gelu_parts(ug_ref[0])
        parts = ((0, dy * uv_ref[0] * dgelu, upg_ref, cwg_ref, dupg_ref, dcbg_ref, dcwg_ref, gwug_ref),
                 (1, dy * gelu, upv_ref, cwv_ref, dupv_ref, dcbv_ref, dcwv_ref, gwuv_ref))
        for part, d, up_ref, cw_ref, dup_ref, dcb_ref, dcw_ref, gwu_ref in parts:
            after = carry[part]
            d1 = _shift_up(d, 1, after)
            d2 = _shift_up(d, 2, after)
            carry[part] = d[0:8, :]
            upc = up_ref[0].astype(F32)
            dcb_ref[0] += jnp.sum(d, axis=0, keepdims=True)
            dcw_ref[0, pl.ds(2, 1), :] += jnp.sum(d * upc, axis=0, keepdims=True)
            dcw_ref[0, pl.ds(1, 1), :] += jnp.sum(d1 * upc, axis=0, keepdims=True)
            dcw_ref[0, pl.ds(0, 1), :] += jnp.sum(d2 * upc, axis=0, keepdims=True)
            dup = (cw_ref[0, pl.ds(2, 1), :] * d + cw_ref[0, pl.ds(1, 1), :] * d1
                   + cw_ref[0, pl.ds(0, 1), :] * d2).astype(BF)
            dup_ref[0] = dup
            gwu_ref[0] += _dot_tn(h, dup)

        @pl.when(pl.program_id(1) == nt - 1)
        def _():
            s = pl.program_id(0)
            pltpu.sync_copy(gwd_ref, gwd_out.at[pl.ds(s, 1)])
            for acc_ref, out in ((gwug_ref, gwug_out), (gwuv_ref, gwuv_out)):
                gwd_ref[0] = acc_ref[0].T
                pltpu.sync_copy(gwd_ref, out.at[pl.ds(s, 1)])

    rev = pl.BlockSpec((tm, D_MODEL), lambda s, i: (nt - 1 - i, 0))
    tile = pl.BlockSpec((1, tm, FF_SHARD), lambda s, i: (s, nt - 1 - i, 0))
    acc = lambda r, w: pl.BlockSpec((1, r, w), lambda s, i: (s, 0, 0))
    acc_val = pl.BlockSpec((1, 3, FF_SHARD), lambda s, i: (s + FF_PAIRS, 0, 0))
    half = lambda r, dt: jax.ShapeDtypeStruct((FF_PAIRS, r, FF_SHARD), dt)
    return _pcall(
        body, name="ffn_bwd_a", grid=(FF_PAIRS, nt),
        in_specs=[rev, rev, acc(FF_SHARD, D_MODEL), tile, tile, tile, tile, tile, acc(3, FF_SHARD), acc_val],
        out_specs=[tile, tile, acc(1, FF_SHARD), acc(1, FF_SHARD), acc(3, FF_SHARD), acc(3, FF_SHARD),
                   ANY_SPEC, ANY_SPEC, ANY_SPEC],
        out_shape=[half(T, BF), half(T, BF), half(1, F32), half(1, F32), half(3, F32), half(3, F32),
                   jax.ShapeDtypeStruct((FF_PAIRS, FF_SHARD, D_MODEL), F32),
                   jax.ShapeDtypeStruct((FF_PAIRS, FF_SHARD, D_MODEL), F32),
                   jax.ShapeDtypeStruct((FF_PAIRS, FF_SHARD, D_MODEL), F32)],
        scratch_shapes=[pltpu.VMEM((2, 8, FF_SHARD), F32), pltpu.VMEM((1, D_MODEL, FF_SHARD), F32),
                        pltpu.VMEM((1, D_MODEL, FF_SHARD), F32), pltpu.VMEM((1, FF_SHARD, D_MODEL), F32)],
        compiler_params=_params(("arbitrary", "arbitrary")),
    )(dz, h2, w_down4, u_g, u_v, up_g, up_v, y4, conv_w8, conv_w8)


def _ffn_bwd_b(dup_g, dup_v, w_up8, x1, dout, g3, mixed, g2, w_out, tm, gwu_g, gwu_v):
    T = x1.shape[0]
    nt = T // tm

    def body(dupg_ref, dupv_ref, wup_ref, x1_ref, dout_ref, g3_ref, mixed_ref, g2_ref, wout_ref, gwug_ref, gwuv_ref,
             dx1_ref, dmixed_ref, da_ref, dr_ref, dg3_ref, dg2_ref, pup_ref, *sems):
        ex = _Exchange([], [((gwug_ref, gwuv_ref), pup_ref)], *sems)

        @pl.when(pl.program_id(0) == 0)
        def _():
            ex.start()
            dg3_ref[...] = jnp.zeros_like(dg3_ref)
            dg2_ref[...] = jnp.zeros_like(dg2_ref)

        dh2 = jnp.zeros((tm, D_MODEL), F32)
        for s in range(FF_PAIRS):
            dh2 = dh2 + _dot_nt(dupg_ref[s], wup_ref[s]) + _dot_nt(dupv_ref[s], wup_ref[s + FF_PAIRS])
        r3, n3 = _rms_stats(x1_ref[...])
        dg3_ref[...] += jnp.sum(dh2 * n3, axis=0, keepdims=True)
        dx1 = dout_ref[...] + _rms_bwd(n3, r3, dh2 * g3_ref[...])
        dx1_ref[...] = dx1
        r2, n2 = _rms_stats(mixed_ref[...])
        dg2_ref[...] += jnp.sum(dx1 * n2, axis=0, keepdims=True)
        dmixed = _rms_bwd(n2, r2, dx1 * g2_ref[...]).astype(BF)
        dmixed_ref[...] = dmixed
        da_ref[...] = _dot_nt(dmixed, wout_ref[:ATTN_W, :])
        dr_ref[...] = _dot_nt(dmixed, wout_ref[ATTN_W:, :])
        pl.when(pl.program_id(0) == nt - 1)(ex.wait)

    half = pl.BlockSpec((FF_PAIRS, tm, FF_SHARD), lambda i: (0, i, 0))
    outs = _pcall(
        body, name="ffn_bwd_b", grid=(nt,),
        in_specs=[half, half, _vmem_full(), _rows(tm, D_MODEL), _rows(tm, D_MODEL), _const((1, D_MODEL)),
                  _rows(tm, D_MODEL), _const((1, D_MODEL)), _vmem_full(), ANY_SPEC, ANY_SPEC],
        out_specs=[_rows(tm, D_MODEL), _rows(tm, D_MODEL), _rows(tm, ATTN_W), _rows(tm, RET_W),
                   _const((1, D_MODEL)), _const((1, D_MODEL)), ANY_SPEC],
        out_shape=[jax.ShapeDtypeStruct((T, D_MODEL), F32), jax.ShapeDtypeStruct((T, D_MODEL), BF),
                   jax.ShapeDtypeStruct((T, ATTN_W), F32), jax.ShapeDtypeStruct((T, RET_W), F32),
                   jax.ShapeDtypeStruct((1, D_MODEL), F32), jax.ShapeDtypeStruct((1, D_MODEL), F32),
                   jax.ShapeDtypeStruct((N_DEV, FF_SHARD, D_MODEL), F32)],
        scratch_shapes=_Exchange.scratch(1),
        compiler_params=_params(("arbitrary",)),
    )(dup_g, dup_v, w_up8, x1, dout, g3, mixed, g2, w_out, gwu_g, gwu_v)
    return outs[:6], outs[6]


def _ret_bwd(decay, qr, kr, vr, gr, o, states, dr, d_intra, d_intra_t, xi_b, zeta_b, cos, sin_s, swaps, ncs):
    T = qr.shape[0]
    nc = T // BLOCK
    H, C = N_RET_HEADS, RET_HEAD_DIM
    ns = len(swaps)

    def body(decay_ref, q_ref, k_ref, v_ref, g_ref, o_ref, s_ref, dr_ref, d_ref, dt_ref, xi_ref, zeta_ref,
             cos_ref, sin_ref, *rest):
        ex = _exchange_of(rest[:ns], rest[ns + 1:2 * ns + 1], rest[2 * ns + 2:], 0)
        dret_ref, gstate = rest[ns], rest[2 * ns + 1]

        @pl.when(pl.program_id(0) == 0)
        def _():
            ex.start()
            gstate[...] = jnp.zeros_like(gstate)

        pairs = [(b, h) for b in range(ncs) for h in range(H)]
        n = len(pairs)
        sl = lambda b, h: (slice(b * BLOCK, (b + 1) * BLOCK), slice(h * C, (h + 1) * C))
        cat = lambda ref: jnp.concatenate([ref[sl(b, h)] for b, h in pairs], axis=0)
        tab = lambda ref: jnp.concatenate([ref[h] for _, h in pairs], axis=0)
        part = lambda x, i: x[i * BLOCK:(i + 1) * BLOCK, :]
        q = [q_ref[sl(b, h)] for b, h in pairs]
        k = [k_ref[sl(b, h)] for b, h in pairs]
        v = [v_ref[sl(b, h)] for b, h in pairs]
        g, o_all, dr_all = cat(g_ref), cat(o_ref), cat(dr_ref)
        mu = jnp.mean(o_all, axis=-1, keepdims=True)
        oc = o_all - mu
        rs = lax.rsqrt(jnp.mean(oc * oc, axis=-1, keepdims=True) + GN_EPS)
        on = oc * rs
        sg = jax.nn.sigmoid(g)
        dg = (dr_all * on * (sg * (1.0 + g * (1.0 - sg)))).astype(BF)
        don = dr_all * (g * sg)
        do = rs * (don - jnp.mean(don, axis=-1, keepdims=True) - on * jnp.mean(don * on, axis=-1, keepdims=True))
        do_b = do.astype(BF)
        dox_b = (do * tab(xi_ref)).astype(BF)
        zeta = tab(zeta_ref)
        kz = (jnp.concatenate(k, axis=0).astype(F32) * zeta).astype(BF)
        d_t = tab(dt_ref)
        da_b = (jnp.concatenate([_dot_nt(part(do_b, i), v[i]) for i in range(n)], axis=0) * tab(d_ref)).astype(BF)
        dat_b = (jnp.concatenate([_dot_nt(v[i], part(do_b, i)) for i in range(n)], axis=0) * d_t).astype(BF)
        mt_b = (jnp.concatenate([_dot_nt(k[i], q[i]) for i in range(n)], axis=0) * d_t).astype(BF)
        dq = [_dot(part(da_b, i), k[i]) + _dot_nt(part(dox_b, i), s_ref[pairs[i]]) for i in range(n)]
        dk1 = [_dot(part(dat_b, i), q[i]) for i in range(n)]
        dv1 = [_dot(part(mt_b, i), part(do_b, i)) for i in range(n)]
        qtd = [_dot_tn(q[i], part(dox_b, i)) for i in range(n)]
        gst_b = [None] * n
        for h in range(H):
            gst = gstate[h]
            for b in reversed(range(ncs)):
                i = b * H + h
                gst_b[i] = gst.astype(BF)
                gst = decay_ref[h] * gst + qtd[i]
            gstate[h] = gst
        dk2 = jnp.concatenate([_dot_nt(v[i], gst_b[i]) for i in range(n)], axis=0) * zeta
        dv = jnp.concatenate([dv1[i] + _dot(part(kz, i), gst_b[i]) for i in range(n)], axis=0).astype(BF)
        even = lax.broadcasted_iota(jnp.int32, (n * BLOCK, C), 1) % 2 == 0
        cos_t = jnp.concatenate([cos_ref[b * BLOCK:(b + 1) * BLOCK, :] for b, _ in pairs], axis=0)
        sin_t = jnp.concatenate([sin_ref[b * BLOCK:(b + 1) * BLOCK, :] for b, _ in pairs], axis=0)
        dq = jnp.concatenate(dq, axis=0)
        dk = jnp.concatenate(dk1, axis=0) + dk2
        dq = (dq * cos_t - _rot(dq, even) * sin_t).astype(BF)
        dk = ((dk * cos_t - _rot(dk, even) * sin_t) * RET_K_SCALE).astype(BF)
        for i, (b, h) in enumerate(pairs):
            rows = slice(b * BLOCK, (b + 1) * BLOCK)
            for j, x in enumerate((dq, dk, dv, dg)):
                dret_ref[rows, j * RET_W + h * C:j * RET_W + (h + 1) * C] = part(x, i)
        pl.when(pl.program_id(0) == steps - 1)(ex.wait)

    steps = nc // ncs
    rev = lambda w: pl.BlockSpec((BLOCK * ncs, w), lambda n: (steps - 1 - n, 0))
    tab = pl.BlockSpec((H, C, C), lambda n: (0, 0, 0))
    outs = _pcall(
        body, name="ret_bwd", grid=(steps,),
        in_specs=[_smem_full(), rev(RET_W), rev(RET_W), rev(RET_W), rev(RET_W), rev(RET_W),
                  pl.BlockSpec((ncs, H, C, C), lambda n: (steps - 1 - n, 0, 0, 0)), rev(RET_W), tab, tab, tab, tab,
                  rev(C), rev(C)] + [ANY_SPEC] * ns,
        out_specs=[rev(4 * RET_W)] + [ANY_SPEC] * ns,
        out_shape=[jax.ShapeDtypeStruct((T, 4 * RET_W), BF)] + _exchange_shapes([], swaps),
        scratch_shapes=[pltpu.VMEM((H, C, C), F32)] + _Exchange.scratch(ns),
        compiler_params=_params(("arbitrary",)),
    )(decay, qr, kr, vr, gr, o, states, dr, d_intra, d_intra_t, xi_b, zeta_b, cos, sin_s, *swaps)
    return outs[0], outs[1:]


def _attn_bwd_dq(sinks, qa, ka, va, da, lse, nbs, swaps):
    T = qa.shape[0]
    steps = T // (BLOCK * nbs)
    R = nbs * N_ATTN_HEADS * BLOCK
    ns = len(swaps)

    def body(sink_ref, q_ref, kc_ref, kp_ref, vc_ref, vp_ref, da_ref, lse_ref, *rest):
        ex = _exchange_of(rest[:ns], rest[ns + 3:2 * ns + 3], rest[2 * ns + 3:], 0)
        dq_ref, deltat_ref, dsink_ref = rest[ns:ns + 3]
        n = pl.program_id(0)

        @pl.when(n == 0)
        def _():
            ex.start()
            dsink_ref[...] = jnp.zeros_like(dsink_ref)

        kcat = _block_variants(kp_ref, kc_ref, nbs)
        vcat = _block_variants(vp_ref, vc_ref, nbs)
        tri1 = _tri(BLOCK)
        lane = lax.broadcasted_iota(jnp.int32, (BLOCK, BLOCK), 1)
        s_tiles, dp_tiles, lse_cols = [], [], []
        for b in range(nbs):
            rows = slice(b * BLOCK, (b + 1) * BLOCK)
            lse_tile = lse_ref[rows, :]
            for pair in range(N_ATTN_HEADS // 2):
                qp = _scaled(q_ref[rows, pair * 128:(pair + 1) * 128])
                dop = da_ref[rows, pair * 128:(pair + 1) * 128].astype(BF)
                for e in range(2):
                    s = _fold(_dot_nt(qp, kcat[b][pair // 2][e]), tri1, True)
                    if b == 0:
                        s = jnp.where(tri1 & (n == 0), MASK_VALUE, s)
                    s_tiles.append(s)
                    dp_tiles.append(_fold(_dot_nt(dop, vcat[b][pair // 2][e]), tri1, True))
                    lse_cols.append(jnp.sum(jnp.where(lane == 2 * pair + e, lse_tile, 0.0), axis=-1, keepdims=True))
        lse_c = jnp.concatenate(lse_cols, axis=0)
        p = jnp.exp(jnp.concatenate(s_tiles, axis=0) - lse_c)
        dp = jnp.concatenate(dp_tiles, axis=0)
        delta = jnp.sum(p * dp, axis=-1, keepdims=True)
        ds2 = _unfold(p * (dp - delta), _tri(R), True)
        for b in range(nbs):
            for pair in range(N_ATTN_HEADS // 2):
                r0 = (b * N_ATTN_HEADS + 2 * pair) * BLOCK
                acc = (_dot(ds2[r0:r0 + BLOCK, :], kcat[b][pair // 2][0])
                       + _dot(ds2[r0 + BLOCK:r0 + 2 * BLOCK, :], kcat[b][pair // 2][1]))
                dq_ref[b * BLOCK:(b + 1) * BLOCK, pair * 128:(pair + 1) * 128] = (acc * ATTN_SCALE).astype(BF)
        for b, t in enumerate(_head_cols(delta, nbs)):
            deltat_ref[:, b * BLOCK:(b + 1) * BLOCK] = t.T[:N_ATTN_HEADS, :]
        sink = jnp.concatenate([jnp.full((BLOCK, 1), sink_ref[head], F32)
                                for _ in range(nbs) for head in range(N_ATTN_HEADS)], axis=0)
        ds_sink = -jnp.exp(sink - lse_c) * delta
        row8 = lax.broadcasted_iota(jnp.int32, (N_ATTN_HEADS, BLOCK), 0)
        dsink = jnp.zeros((N_ATTN_HEADS, BLOCK), F32)
        for b in range(nbs):
            for head in range(N_ATTN_HEADS):
                r0 = (b * N_ATTN_HEADS + head) * BLOCK
                dsink = dsink + jnp.where(row8 == head, jnp.sum(ds_sink[r0:r0 + BLOCK, :], axis=0, keepdims=True), 0.0)
        dsink_ref[...] += dsink
        pl.when(n == steps - 1)(ex.wait)

    cur = lambda w: pl.BlockSpec((BLOCK * nbs, w), lambda n: (n, 0))
    prev = lambda w: pl.BlockSpec((BLOCK, w), lambda n: (jnp.maximum(n * nbs - 1, 0), 0))
    outs = _pcall(
        body, name="attn_bwd_dq", grid=(steps,),
        in_specs=[_smem_full(), cur(ATTN_W), cur(KV_W), prev(KV_W), cur(KV_W), prev(KV_W), cur(ATTN_W), cur(BLOCK)]
        + [ANY_SPEC] * ns,
        out_specs=[cur(ATTN_W), pl.BlockSpec((N_ATTN_HEADS, BLOCK * nbs), lambda n: (0, n)),
                   _const((N_ATTN_HEADS, BLOCK))] + [ANY_SPEC] * ns,
        out_shape=[jax.ShapeDtypeStruct((T, ATTN_W), BF), jax.ShapeDtypeStruct((N_ATTN_HEADS, T), F32),
                   jax.ShapeDtypeStruct((N_ATTN_HEADS, BLOCK), F32)] + _exchange_shapes([], swaps),
        scratch_shapes=_Exchange.scratch(ns),
        compiler_params=_params(("arbitrary",)),
    )(sinks, qa, ka, ka, va, va, da, lse, *swaps)
    return outs[:3], outs[3:]


def _attn_bwd_dkv(qa, ka, va, da, lse_t, delta_t, nbs):
    T = qa.shape[0]
    nb = T // BLOCK
    steps = nb // nbs
    R = nbs * N_ATTN_HEADS * BLOCK

    def body(qc_ref, qn_ref, dac_ref, dan_ref, k_ref, v_ref, lc_ref, ln_ref, dc_ref, dn_ref, dk_ref, dv_ref):
        n = pl.program_id(0)
        tri1 = _tri(BLOCK, True)
        lo = lax.broadcasted_iota(jnp.int32, (BLOCK, 128), 1) < HEAD_DIM
        kv = [_kv_variants(k_ref[b * BLOCK:(b + 1) * BLOCK, :]) for b in range(nbs)]
        vv = [_kv_variants(v_ref[b * BLOCK:(b + 1) * BLOCK, :]) for b in range(nbs)]
        qcat, docat = [], []
        s_tiles, dp_tiles, lse_tiles, delta_tiles = [], [], [], []
        for b in range(nbs):
            rows = slice(b * BLOCK, (b + 1) * BLOCK)
            nrows = slice((b + 1) * BLOCK, (b + 2) * BLOCK)
            inside = b < nbs - 1
            for pair in range(N_ATTN_HEADS // 2):
                ps = slice(pair * 128, (pair + 1) * 128)
                q2 = _scaled(jnp.concatenate([qc_ref[rows, ps], qc_ref[nrows, ps] if inside else qn_ref[:, ps]], axis=0))
                do2 = jnp.concatenate([dac_ref[rows, ps], dac_ref[nrows, ps] if inside else dan_ref[:, ps]],
                                      axis=0).astype(BF)
                qcat.append(q2)
                docat.append(do2)
                for e in range(2):
                    one = pl.ds(2 * pair + e, 1)
                    s = _fold(_dot_nt(kv[b][pair // 2][e], q2), tri1, False)
                    if not inside:
                        s = jnp.where(tri1 & (n == steps - 1), MASK_VALUE, s)
                    s_tiles.append(s)
                    dp_tiles.append(_fold(_dot_nt(vv[b][pair // 2][e], do2), tri1, False))
                    lse_tiles.append(jnp.where(tri1, lc_ref[one, nrows] if inside else ln_ref[one, :], lc_ref[one, rows]))
                    delta_tiles.append(jnp.where(tri1, dc_ref[one, nrows] if inside else dn_ref[one, :],
                                                 dc_ref[one, rows]))
        pt = jnp.exp(jnp.concatenate(s_tiles, axis=0) - jnp.concatenate(lse_tiles, axis=0))
        dst = pt * (jnp.concatenate(dp_tiles, axis=0) - jnp.concatenate(delta_tiles, axis=0))
        tri = _tri(R, True)
        pt2 = _unfold(pt, tri, False)
        dst2 = _unfold(dst, tri, False)
        for b in range(nbs):
            dk = jnp.zeros((BLOCK, 128), F32)
            dv = jnp.zeros((BLOCK, 128), F32)
            for pair in range(N_ATTN_HEADS // 2):
                h = pair // 2
                for e in range(2):
                    r0 = (b * N_ATTN_HEADS + 2 * pair + e) * BLOCK
                    half = lo if e == 0 else jnp.logical_not(lo)
                    dv_e = jnp.where(half, _dot(pt2[r0:r0 + BLOCK, :], docat[b * 4 + pair]), 0.0)
                    dk_e = jnp.where(half, _dot(dst2[r0:r0 + BLOCK, :], qcat[b * 4 + pair]), 0.0)
                    if e != h:
                        dv_e = pltpu.roll(dv_e, HEAD_DIM, 1)
                        dk_e = pltpu.roll(dk_e, HEAD_DIM, 1)
                    dv = dv + dv_e
                    dk = dk + dk_e
            dk_ref[b * BLOCK:(b + 1) * BLOCK, :] = dk.astype(BF)
            dv_ref[b * BLOCK:(b + 1) * BLOCK, :] = dv.astype(BF)

    cur = lambda w: pl.BlockSpec((BLOCK * nbs, w), lambda n: (n, 0))
    nxt = lambda w: pl.BlockSpec((BLOCK, w), lambda n: (jnp.minimum((n + 1) * nbs, nb - 1), 0))
    tcur = pl.BlockSpec((N_ATTN_HEADS, BLOCK * nbs), lambda n: (0, n))
    tnxt = pl.BlockSpec((N_ATTN_HEADS, BLOCK), lambda n: (0, jnp.minimum((n + 1) * nbs, nb - 1)))
    return _pcall(
        body, name="attn_bwd_dkv", grid=(steps,),
        in_specs=[cur(ATTN_W), nxt(ATTN_W), cur(ATTN_W), nxt(ATTN_W), cur(KV_W), cur(KV_W), tcur, tnxt, tcur, tnxt],
        out_specs=[cur(KV_W), cur(KV_W)],
        out_shape=[jax.ShapeDtypeStruct((T, KV_W), BF), jax.ShapeDtypeStruct((T, KV_W), BF)],
        compiler_params=_params(("parallel",)),
    )(qa, qa, da, da, ka, va, lse_t, lse_t, delta_t, delta_t)


def _in_proj_bwd(dqa, dka, dva, dret, w_in, x, g1, dx1, tm):
    T = x.shape[0]

    def body(dqa_ref, dka_ref, dva_ref, dret_ref, w_ref, x_ref, g_ref, dx1_ref, dx_ref, dg1_ref):
        @pl.when(pl.program_id(0) == 0)
        def _():
            dg1_ref[...] = jnp.zeros_like(dg1_ref)

        dh = (_dot_nt(dqa_ref[...], w_ref[:, QA0:QA0 + ATTN_W]) + _dot_nt(dka_ref[...], w_ref[:, KA0:KA0 + KV_W])
              + _dot_nt(dva_ref[...], w_ref[:, VA0:VA0 + KV_W]) + _dot_nt(dret_ref[...], w_ref[:, QR0:IN_W]))
        r, n = _rms_stats(x_ref[...])
        dg1_ref[...] += jnp.sum(dh * n, axis=0, keepdims=True)
        dx_ref[...] = dx1_ref[...] + _rms_bwd(n, r, dh * g_ref[...])

    return _pcall(
        body, name="in_proj_bwd", grid=(T // tm,),
        in_specs=[_rows(tm, ATTN_W), _rows(tm, KV_W), _rows(tm, KV_W), _rows(tm, 4 * RET_W), _vmem_full(),
                  _rows(tm, D_MODEL), _const((1, D_MODEL)), _rows(tm, D_MODEL)],
        out_specs=[_rows(tm, D_MODEL), _const((1, D_MODEL))],
        out_shape=[jax.ShapeDtypeStruct((T, D_MODEL), F32), jax.ShapeDtypeStruct((1, D_MODEL), F32)],
        compiler_params=_params(("arbitrary",)),
    )(dqa, dka, dva, dret, w_in, x, g1, dx1)


def _wgrad(a_list, b_list, tk, name):
    T = a_list[0].shape[0]
    na, nbb = len(a_list), len(b_list)
    m_sizes = [a.shape[1] for a in a_list]
    n_sizes = [b.shape[1] for b in b_list]
    M, N = sum(m_sizes), sum(n_sizes)
    nk = T // tk
    chunk = 512

    def body(*refs):
        a_refs, b_refs = refs[:na], refs[na:na + nbb]
        out_ref, acc = refs[na + nbb], refs[na + nbb + 1]
        k = pl.program_id(0)

        @pl.when(k == 0)
        def _():
            acc[...] = jnp.zeros_like(acc)

        r0 = 0
        for ai in range(na):
            a = a_refs[ai][...]
            c0 = 0
            for bi in range(nbb):
                for s in range(0, n_sizes[bi], chunk):
                    w = min(chunk, n_sizes[bi] - s)
                    acc[r0:r0 + m_sizes[ai], c0 + s:c0 + s + w] += _dot_tn(a, b_refs[bi][:, s:s + w])
                c0 += n_sizes[bi]
            r0 += m_sizes[ai]

        @pl.when(k == nk - 1)
        def _():
            pltpu.sync_copy(acc, out_ref)

    return _pcall(
        body, name=name, grid=(nk,),
        in_specs=[_rows(tk, w) for w in m_sizes + n_sizes],
        out_specs=pl.BlockSpec(memory_space=pl.ANY),
        out_shape=jax.ShapeDtypeStruct((M, N), F32),
        scratch_shapes=[pltpu.VMEM((M, N), F32)],
        compiler_params=_params(("arbitrary",)),
    )(*a_list, *b_list)


def _adamw_math(w, g, m, v):
    m = ADAM_B1 * m + (1.0 - ADAM_B1) * g
    v = ADAM_B2 * v + (1.0 - ADAM_B2) * (g * g)
    m_hat = m / (1.0 - ADAM_B1 ** ADAM_STEP)
    v_hat = v / (1.0 - ADAM_B2 ** ADAM_STEP)
    delta = -ADAM_LR * (m_hat / (jnp.sqrt(v_hat) + ADAM_EPS) + ADAM_WD * w)
    return delta, m, v


def _sum_parts(parts_ref):
    g = parts_ref[0].astype(F32)
    for i in range(1, N_DEV):
        g = g + parts_ref[i].astype(F32)
    return g


def _adamw_shard(parts, w, m, v, tr, name):
    R, C = w.shape

    def body(p_ref, w_ref, m_ref, v_ref, g_ref, d_ref, nm_ref, nv_ref):
        g = _sum_parts(p_ref)
        g_ref[...] = g
        d_ref[...], nm_ref[...], nv_ref[...] = _adamw_math(w_ref[...], g, m_ref[...], v_ref[...])

    blk = pl.BlockSpec((tr, C), lambda i: (i, 0))
    return _pcall(
        body, name=name, grid=(R // tr,),
        in_specs=[pl.BlockSpec((N_DEV, tr, C), lambda i: (0, i, 0)), blk, blk, blk],
        out_specs=[blk] * 4,
        out_shape=[jax.ShapeDtypeStruct((R, C), F32)] * 4,
        compiler_params=_params(("parallel",)),
    )(parts, w, m, v)


def _sum_small(parts):
    def body(p_ref, g_ref):
        g_ref[...] = _sum_parts(p_ref)

    return _pcall(body, name="sum_small", out_shape=jax.ShapeDtypeStruct(parts.shape[1:], F32),
                  in_specs=[_vmem_full()], out_specs=_vmem_full())(parts)


def _adamw_small(g, w, m, v, name):
    def body(g_ref, w_ref, m_ref, v_ref, d_ref, nm_ref, nv_ref):
        d_ref[...], nm_ref[...], nv_ref[...] = _adamw_math(w_ref[...], g_ref[...], m_ref[...], v_ref[...])

    return _pcall(body, name=name, out_shape=[jax.ShapeDtypeStruct(w.shape, F32)] * 3,
                  in_specs=[_vmem_full()] * 4, out_specs=[_vmem_full()] * 3)(g, w, m, v)


def _tables(T):
    h, c = N_RET_HEADS, BLOCK
    pos = jnp.arange(T, dtype=F32)
    angle = 1.0 / jnp.power(10000.0, jnp.linspace(0.0, 1.0, RET_HEAD_DIM // 2, dtype=F32))
    angle = jnp.repeat(angle, 2)
    sin = jnp.sin(pos[:, None] * angle[None])
    cos = jnp.cos(pos[:, None] * angle[None])
    even = (jnp.arange(RET_HEAD_DIM) % 2 == 0)[None, :]
    sin_s = jnp.where(even, -sin, sin)
    log_gamma = jnp.log(1.0 - jnp.power(2.0, -5.0 - jnp.arange(h, dtype=F32)))
    idx = jnp.arange(c, dtype=F32)
    rel = idx[:, None] - idx[None, :]
    d_intra = jnp.where(rel[None] >= 0, jnp.exp(log_gamma[:, None, None] * jnp.maximum(rel, 0.0)[None]), 0.0)
    xi = jnp.exp(log_gamma[None, :] * (idx[:, None] + 1.0))
    zeta = jnp.exp(log_gamma[None, :] * (c - 1.0 - idx[:, None]))
    decay = jnp.exp(log_gamma * c)
    xi_b = jnp.broadcast_to(xi.T[:, :, None], (h, c, RET_HEAD_DIM))
    zeta_b = jnp.broadcast_to(zeta.T[:, :, None], (h, c, RET_HEAD_DIM))
    return cos, sin_s, d_intra, jnp.swapaxes(d_intra, 1, 2), xi_b, zeta_b, decay


def _from_shards(sh):
    n, r, cols = sh.shape
    return jnp.swapaxes(sh, 0, 1).reshape(r, n * cols)


SMALL_ROWS = 216


def _pack_small(gains, conv_b, conv_w, sinks, scalar=None):
    last = jnp.concatenate([sinks.reshape(1, 8), jnp.zeros((1, 1), F32) if scalar is None else scalar.reshape(1, 1)],
                           axis=1)
    parts = [g.reshape(8, 128) for g in gains] + [conv_b.reshape(44, 128), conv_w.reshape(132, 128),
                                                  jnp.pad(last, ((0, 0), (0, 119)))]
    packed = jnp.concatenate(parts, axis=0)
    return jnp.pad(packed, ((0, SMALL_ROWS - packed.shape[0]), (0, 0)))


def kernel(x, mix_pre_norm, w_in, attn_sinks, w_out, mix_post_norm, ffn_pre_norm, w_up, conv_w, conv_b, w_down, ffn_post_norm, loss_target, m_mix_pre_norm, m_w_in, m_attn_sinks, m_w_out, m_mix_post_norm, m_ffn_pre_norm, m_w_up, m_conv_w, m_conv_b, m_w_down, m_ffn_post_norm, v_mix_pre_norm, v_w_in, v_attn_sinks, v_w_out, v_mix_post_norm, v_ffn_pre_norm, v_w_up, v_conv_w, v_conv_b, v_w_down, v_ffn_post_norm):
    T = x.shape[1]
    tm = min(512, T)
    tm_big = min(1024, T)
    tk_grad = min(2048, T)
    nbs = min(8, T // BLOCK)
    x2 = x.reshape(T, D_MODEL)
    target = loss_target.reshape(T, D_MODEL)
    me = 4 * lax.axis_index("x") + 2 * lax.axis_index("y") + lax.axis_index("c")

    g_in, g_cw = _exchange_call([w_in[0].astype(BF), conv_w[0]], [], "gather_w_in")
    w_in_f = _from_shards(g_in)
    cos, sin_s, d_intra, d_intra_t, xi_b, zeta_b, decay = _tables(T)
    sinks = attn_sinks.reshape(N_ATTN_HEADS)

    (h1, qa, ka, va, qr, kr, vr, gr), (w_up8,) = _in_proj(
        x2, mix_pre_norm, w_in_f, cos, sin_s, tm_big, [w_up[0].astype(BF)])
    (a, lse, lse_t), (g_down, g_out) = _attn_fwd(sinks, qa, ka, va, nbs,
                                                 [w_down[0].astype(BF), w_out[0].astype(BF)])
    w_out_f = g_out.reshape(D_MODEL, D_MODEL)
    o, states, r = _ret_fwd(decay, qr, kr, vr, gr, d_intra, xi_b, zeta_b, nbs)
    mixed, x1, h2 = _out_proj(a, r, w_out_f, x2, mix_post_norm, ffn_pre_norm, tm_big)
    w_down4 = g_down.reshape(FF_PAIRS, FF_SHARD, D_MODEL)
    up_g, up_v, u_g, u_v, y4, dout, dz, dg4, loss_acc = _ffn_fwd(
        h2, w_up8, g_cw, conv_b.reshape(N_DEV, 1, FF_SHARD), w_down4, x1, ffn_post_norm, target, tm)

    dup_g, dup_v, dcb_g, dcb_v, dcw_g, dcw_v, gwu_g, gwu_v, gw_down4 = _ffn_bwd_a(
        dz, h2, w_down4, u_g, u_v, up_g, up_v, y4, g_cw, tm_big)
    dcb = jnp.concatenate([dcb_g, dcb_v], axis=0).reshape(1, 2 * D_FF)
    dcw = _from_shards(jnp.concatenate([dcw_g, dcw_v], axis=0))
    gw_down = gw_down4.reshape(D_FF, D_MODEL)
    (dx1, dmixed, da, dr, dg3, dg2), p_up = _ffn_bwd_b(
        dup_g, dup_v, w_up8, x1, dout, ffn_pre_norm, mixed, mix_post_norm, w_out_f, tm, gwu_g, gwu_v)
    gw_out = _wgrad([a, r], [dmixed], tk_grad, "wgrad_out")
    dret, (p_down,) = _ret_bwd(decay, qr, kr, vr, gr, o, states, dr, d_intra, d_intra_t, xi_b, zeta_b, cos, sin_s,
                               [gw_down.reshape(N_DEV, D_FF // N_DEV, D_MODEL)], nbs)
    (dqa, delta_t, dsink), (p_out,) = _attn_bwd_dq(sinks, qa, ka, va, da, lse, nbs,
                                                   [gw_out.reshape(N_DEV, D_MODEL // N_DEV, D_MODEL)])
    dka, dva = _attn_bwd_dkv(qa, ka, va, da, lse_t, delta_t, nbs)
    grad_x, dg1 = _in_proj_bwd(dqa, dka, dva, dret, w_in_f, x2, mix_pre_norm, dx1, tm)
    gw_in = _wgrad([h1], [dqa, dka, dva, dret], tk_grad, "wgrad_in")

    small = _pack_small([dg1, dg2, dg3, dg4], dcb, dcw, dsink[:, 0], loss_acc[0, 0])
    gw_in_t = gw_in.T.reshape(N_DEV, IN_W // N_DEV, D_MODEL).astype(BF)
    small_all, p_in = _exchange_call([small], [gw_in_t], "exchange_last")
    g_small = _sum_small(small_all)
    loss = g_small[208, N_ATTN_HEADS]

    t_in = lambda a: jnp.swapaxes(a, 1, 2)[0]
    g_w_in, d_w_in, nm_w_in, nv_w_in = [o.T for o in _adamw_shard(
        p_in, t_in(w_in), t_in(m_w_in), t_in(v_w_in), 176, "adamw_in")]
    g_w_up, d_w_up, nm_w_up, nv_w_up = [o.T for o in _adamw_shard(
        p_up, t_in(w_up), t_in(m_w_up), t_in(v_w_up), 176, "adamw_up")]
    g_w_out, d_w_out, nm_w_out, nv_w_out = _adamw_shard(p_out, w_out[0], m_w_out[0], v_w_out[0], 128, "adamw_out")
    g_w_down, d_w_down, nm_w_down, nv_w_down = _adamw_shard(p_down, w_down[0], m_w_down[0], v_w_down[0], 176,
                                                            "adamw_down")
    gains = [mix_pre_norm, mix_post_norm, ffn_pre_norm, ffn_post_norm]
    m_gains = [m_mix_pre_norm, m_mix_post_norm, m_ffn_pre_norm, m_ffn_post_norm]
    v_gains = [v_mix_pre_norm, v_mix_post_norm, v_ffn_pre_norm, v_ffn_post_norm]
    zeros_cw = jnp.zeros((3, 2 * D_FF), F32)
    w_small = _pack_small(gains, conv_b, zeros_cw, attn_sinks)
    m_small = _pack_small(m_gains, m_conv_b, zeros_cw, m_attn_sinks)
    v_small = _pack_small(v_gains, v_conv_b, zeros_cw, v_attn_sinks)
    d_small, nm_small, nv_small = _adamw_small(g_small, w_small, m_small, v_small, "adamw_small")
    shard_cols = 2 * D_FF // N_DEV
    g_cw = lax.dynamic_slice(g_small[76:208].reshape(3, 2 * D_FF), (0, me * shard_cols), (3, shard_cols))
    d_cw, nm_cw, nv_cw = _adamw_small(g_cw, conv_w[0], m_conv_w[0], v_conv_w[0], "adamw_conv_w")

    def unpack(p):
        gains_o = [p[8 * i:8 * i + 8].reshape(1, D_MODEL) for i in range(4)]
        return gains_o, p[32:76].reshape(1, 2 * D_FF), p[208:209, :N_ATTN_HEADS]

    def leaves(p, w_in_s, w_out_s, w_up_s, cw_s, w_down_s):
        (pre1, post1, pre2, post2), cb, sk = unpack(p)
        return [pre1, w_in_s[None], sk, w_out_s[None], post1, pre2, w_up_s[None], cw_s[None], cb, w_down_s[None],
                post2]

    return (loss, grad_x.reshape(1, T, D_MODEL),
            *leaves(g_small, g_w_in, g_w_out, g_w_up, g_cw, g_w_down),
            *leaves(d_small, d_w_in, d_w_out, d_w_up, d_cw, d_w_down),
            *leaves(nm_small, nm_w_in, nm_w_out, nm_w_up, nm_cw, nm_w_down),
            *leaves(nv_small, nv_w_in, nv_w_out, nv_w_up, nv_cw, nv_w_down))
```

```python
import functools
import math

import jax
import jax.numpy as jnp
from jax import lax
from jax.experimental import pallas as pl
from jax.experimental.pallas import tpu as pltpu

F32 = jnp.float32
BF = jnp.bfloat16

N_DEV = 8
D_MODEL = 1024
HEAD_DIM = 64
ATTN_W = 512
N_ATTN_HEADS = 8
KV_W = 128
BLOCK = 128
RET_W = 512
N_RET_HEADS = 4
RET_HEAD_DIM = 128
IN_W = 2816
D_FF = 2816
RMS_EPS = 1e-6
GN_EPS = 1e-6
MASK_VALUE = -1e30
ATTN_SCALE = HEAD_DIM ** -0.5
RET_K_SCALE = RET_HEAD_DIM ** -0.5
GELU_C = math.sqrt(2.0 / math.pi)
GELU_A = 0.044715

ADAM_LR = 0.001
ADAM_B1 = 0.9
ADAM_B2 = 0.999
ADAM_EPS = 1e-08
ADAM_WD = 0.01
ADAM_STEP = 10

VMEM_LIMIT_BYTES = 56 * 1024 * 1024
FF_SHARD = 2 * D_FF // N_DEV
FF_PAIRS = N_DEV // 2

QA0, KA0, VA0, QR0, KR0, VR0, GR0 = 0, 512, 640, 768, 1280, 1792, 2304

MESH_ID = pl.DeviceIdType.MESH


def _pcall(body, **kw):
    return pl.pallas_call(body, **kw)


def _params(sem=None):
    return pltpu.CompilerParams(dimension_semantics=sem, vmem_limit_bytes=VMEM_LIMIT_BYTES)


def _dot(a, b):
    return jnp.dot(a, b, preferred_element_type=F32)


def _dot_nt(a, b):
    return lax.dot_general(a, b, (((1,), (1,)), ((), ())), preferred_element_type=F32)


def _dot_tn(a, b):
    return lax.dot_general(a, b, (((0,), (0,)), ((), ())), preferred_element_type=F32)


def _vmem_full():
    return pl.BlockSpec(memory_space=pltpu.VMEM)


def _smem_full():
    return pl.BlockSpec(memory_space=pltpu.SMEM)


def _rows(tm, w):
    return pl.BlockSpec((tm, w), lambda i: (i, 0))


def _const(shape):
    return pl.BlockSpec(shape, lambda i: tuple(0 for _ in shape))


def _rms_stats(x):
    r = lax.rsqrt(jnp.mean(x * x, axis=-1, keepdims=True) + RMS_EPS)
    return r, x * r


def _rms_bwd(n, r, dn):
    return r * (dn - n * jnp.mean(dn * n, axis=-1, keepdims=True))


def _rot(x, even):
    w = x.shape[1]
    return jnp.where(even, pltpu.roll(x, w - 1, 1), pltpu.roll(x, 1, 1))


def _peers():
    x, y, c = lax.axis_index("x"), lax.axis_index("y"), lax.axis_index("c")
    flips = [(0, 0, 1), (1, 0, 0), (0, 1, 0), (1, 1, 0), (1, 0, 1), (0, 1, 1), (1, 1, 1)]
    peers = [(x ^ fx, y ^ fy, c ^ fc) for fx, fy, fc in flips]
    return 4 * x + 2 * y + c, peers


SAME_CORE_PEERS = 4


class _Exchange:
    def __init__(self, gathers, swaps, send_sems, recv_sems, local_sems):
        self.me, self.peers = _peers()
        self.slots = [4 * px + 2 * py + pc for px, py, pc in self.peers]
        self.pairs = [(src, dst, True) for src, dst in gathers] + [(src, dst, False) for src, dst in swaps]
        self.send_sems, self.recv_sems, self.local_sems = send_sems, recv_sems, local_sems

    @staticmethod
    def scratch(n):
        return [pltpu.SemaphoreType.DMA((n, N_DEV - 1)), pltpu.SemaphoreType.DMA((n, N_DEV - 1)),
                pltpu.SemaphoreType.DMA((n,))]

    def _parts(self, a, slot):
        src, _, whole = self.pairs[a]
        half = N_DEV // 2
        if whole:
            return [(None, src)]
        if isinstance(src, tuple):
            return [(slot < half, src[0].at[jnp.minimum(slot, half - 1)]),
                    (slot >= half, src[1].at[jnp.maximum(slot - half, 0)])]
        return [(None, src.at[slot])]

    def _local(self, a, src):
        return pltpu.make_async_copy(src, self.pairs[a][1].at[self.me], self.local_sems.at[a])

    def _remote(self, a, k, src, slot):
        return pltpu.make_async_remote_copy(
            src_ref=src, dst_ref=self.pairs[a][1].at[slot], send_sem=self.send_sems.at[a, k],
            recv_sem=self.recv_sems.at[a, k], device_id=self.peers[k], device_id_type=MESH_ID)

    def start(self):
        def go(cond, copy):
            if cond is None:
                copy.start()
            else:
                pl.when(cond)(copy.start)

        for a in range(len(self.pairs)):
            for cond, src in self._parts(a, self.me):
                go(cond, self._local(a, src))
            for k in range(SAME_CORE_PEERS if self.pairs[a][2] else N_DEV - 1):
                for cond, src in self._parts(a, self.slots[k]):
                    go(cond, self._remote(a, k, src, self.me))

    def _pass_on(self, a, j):
        k = j + SAME_CORE_PEERS - 1
        block = self.pairs[a][1].at[self.slots[j]]
        return pltpu.make_async_remote_copy(
            src_ref=block, dst_ref=block, send_sem=self.send_sems.at[a, k], recv_sem=self.recv_sems.at[a, k],
            device_id=self.peers[0], device_id_type=MESH_ID)

    def wait(self):
        for a in range(len(self.pairs)):
            src = self._parts(a, self.me)[0][1]
            if self.pairs[a][2]:
                for j in range(1, SAME_CORE_PEERS):
                    self._remote(a, j, src, self.slots[j]).wait_recv()
                    self._pass_on(a, j).start()
                self._remote(a, 0, src, self.slots[0]).wait_recv()
            for k in range(SAME_CORE_PEERS if self.pairs[a][2] else 0, N_DEV - 1):
                self._remote(a, k, src, self.slots[k]).wait_recv()
        for a in range(len(self.pairs)):
            src = self._parts(a, self.me)[0][1]
            for k in range(SAME_CORE_PEERS if self.pairs[a][2] else N_DEV - 1):
                self._remote(a, k, src, self.me).wait_send()
            if self.pairs[a][2]:
                for j in range(1, SAME_CORE_PEERS):
                    self._pass_on(a, j).wait_send()
            self._local(a, src).wait()


ANY_SPEC = pl.BlockSpec(memory_space=pl.ANY)


def _exchange_shapes(gathers, swaps):
    return ([jax.ShapeDtypeStruct((N_DEV,) + a.shape, a.dtype) for a in gathers]
            + [jax.ShapeDtypeStruct(a.shape, a.dtype) for a in swaps])


def _exchange_of(ins, outs, sems, ng):
    return _Exchange(list(zip(ins[:ng], outs[:ng])), list(zip(ins[ng:], outs[ng:])), *sems)


def _exchange_call(gathers, swaps, name):
    ng, ns = len(gathers), len(swaps)
    n = ng + ns

    def body(*refs):
        ex = _exchange_of(refs[:n], refs[n:2 * n], refs[2 * n:], ng)
        ex.start()
        ex.wait()

    return _pcall(
        body, name=name, out_shape=_exchange_shapes(gathers, swaps),
        in_specs=[ANY_SPEC] * (ng + ns), out_specs=[ANY_SPEC] * (ng + ns),
        scratch_shapes=_Exchange.scratch(ng + ns),
    )(*gathers, *swaps)


def _in_proj(x, g1, w_in, cos, sin_s, tm, gathers):
    T = x.shape[0]
    ng = len(gathers)
    nt = T // tm

    def body(x_ref, g_ref, w_ref, cos_ref, sin_ref, *rest):
        ex = _exchange_of(rest[:ng], rest[ng + 8:2 * ng + 8], rest[2 * ng + 8:], ng)
        h_ref, qa_ref, ka_ref, va_ref, qr_ref, kr_ref, vr_ref, gr_ref = rest[ng:ng + 8]
        pl.when(pl.program_id(0) == 0)(ex.start)
        r, n = _rms_stats(x_ref[...])
        h = (n * g_ref[...]).astype(BF)
        h_ref[...] = h

        def proj(c0, w):
            return _dot_nt(h, w_ref[c0:c0 + w, :])

        qa_ref[...] = proj(QA0, ATTN_W).astype(BF)
        kva = proj(KA0, 2 * KV_W)
        ka_ref[...] = kva[:, :KV_W].astype(BF)
        va_ref[...] = kva[:, KV_W:].astype(BF)
        vr_ref[...] = proj(VR0, RET_W).astype(BF)
        gr_ref[...] = proj(GR0, RET_W)
        cos_t, sin_t = cos_ref[...], sin_ref[...]
        even = lax.broadcasted_iota(jnp.int32, (tm, RET_HEAD_DIM), 1) % 2 == 0
        for c0, scale, out_ref in ((QR0, None, qr_ref), (KR0, RET_K_SCALE, kr_ref)):
            full = proj(c0, RET_W)
            for hd in range(N_RET_HEADS):
                cs = slice(hd * RET_HEAD_DIM, (hd + 1) * RET_HEAD_DIM)
                t = full[:, cs] if scale is None else full[:, cs] * scale
                out_ref[:, cs] = (t * cos_t + _rot(t, even) * sin_t).astype(BF)
        pl.when(pl.program_id(0) == nt - 1)(ex.wait)

    widths = [D_MODEL, ATTN_W, KV_W, KV_W, RET_W, RET_W, RET_W, RET_W]
    dts = [BF] * 7 + [F32]
    outs = _pcall(
        body, name="in_proj", grid=(nt,),
        in_specs=[_rows(tm, D_MODEL), _const((1, D_MODEL)), _vmem_full(), _rows(tm, RET_HEAD_DIM),
                  _rows(tm, RET_HEAD_DIM)] + [ANY_SPEC] * ng,
        out_specs=[_rows(tm, w) for w in widths] + [ANY_SPEC] * ng,
        out_shape=[jax.ShapeDtypeStruct((T, w), dt) for w, dt in zip(widths, dts)] + _exchange_shapes(gathers, []),
        scratch_shapes=_Exchange.scratch(ng),
        compiler_params=_params(("arbitrary",)),
    )(x, g1, w_in, cos, sin_s, *gathers)
    return outs[:8], outs[8:]


def _kv_variants(kk):
    kf = kk.astype(F32)
    lo = lax.broadcasted_iota(jnp.int32, kf.shape, 1) < HEAD_DIM
    h0_lo = jnp.where(lo, kf, 0.0)
    h1_hi = jnp.where(lo, 0.0, kf)
    h0_hi = pltpu.roll(h0_lo, HEAD_DIM, 1)
    h1_lo = pltpu.roll(h1_hi, HEAD_DIM, 1)
    return [[h0_lo.astype(BF), h0_hi.astype(BF)], [h1_lo.astype(BF), h1_hi.astype(BF)]]


def _col_to_tile(tile, col, head):
    lane = lax.broadcasted_iota(jnp.int32, tile.shape, 1)
    return jnp.where(lane == head, col, tile)


def _tri(rows, key_major=False):
    i = lax.broadcasted_iota(jnp.int32, (rows, BLOCK), 0) & (BLOCK - 1)
    j = lax.broadcasted_iota(jnp.int32, (rows, BLOCK), 1)
    return i > j if key_major else j > i


def _fold(x2, tri, first_above):
    a, b = x2[:, :BLOCK], x2[:, BLOCK:]
    return jnp.where(tri, a, b) if first_above else jnp.where(tri, b, a)


def _unfold(x, tri, first_above):
    up, low = jnp.where(tri, x, 0.0), jnp.where(tri, 0.0, x)
    return jnp.concatenate([up, low] if first_above else [low, up], axis=1).astype(BF)


def _scaled(q):
    return (q.astype(F32) * ATTN_SCALE).astype(BF)


def _cat_variants(prev, cur):
    return [[jnp.concatenate([prev[h][e], cur[h][e]], axis=0) for e in range(2)] for h in range(2)]


def _block_variants(prev_ref, cur_ref, nbs):
    var = [_kv_variants(prev_ref[...])] + [_kv_variants(cur_ref[b * BLOCK:(b + 1) * BLOCK, :]) for b in range(nbs)]
    return [_cat_variants(var[b], var[b + 1]) for b in range(nbs)]


def _head_cols(col, nbs):
    tiles = []
    for b in range(nbs):
        t = jnp.zeros((BLOCK, BLOCK), F32)
        for head in range(N_ATTN_HEADS):
            r0 = (b * N_ATTN_HEADS + head) * BLOCK
            t = _col_to_tile(t, col[r0:r0 + BLOCK, :], head)
        tiles.append(t)
    return tiles


def _attn_fwd(sinks, qa, ka, va, nbs, gathers):
    T = qa.shape[0]
    steps = T // (BLOCK * nbs)
    R = nbs * N_ATTN_HEADS * BLOCK
    ng = len(gathers)

    def body(sink_ref, q_ref, kc_ref, kp_ref, vc_ref, vp_ref, *rest):
        ex = _exchange_of(rest[:ng], rest[ng + 3:2 * ng + 3], rest[2 * ng + 3:], ng)
        a_ref, lse_ref, lset_ref = rest[ng:ng + 3]
        n = pl.program_id(0)
        pl.when(n == 0)(ex.start)
        kcat = _block_variants(kp_ref, kc_ref, nbs)
        vcat = _block_variants(vp_ref, vc_ref, nbs)
        tri1 = _tri(BLOCK)
        tiles = []
        for b in range(nbs):
            for pair in range(N_ATTN_HEADS // 2):
                qp = _scaled(q_ref[b * BLOCK:(b + 1) * BLOCK, pair * 128:(pair + 1) * 128])
                for e in range(2):
                    s = _fold(_dot_nt(qp, kcat[b][pair // 2][e]), tri1, True)
                    if b == 0:
                        s = jnp.where(tri1 & (n == 0), MASK_VALUE, s)
                    tiles.append(s)
        s = jnp.concatenate(tiles, axis=0)
        sink = jnp.concatenate([jnp.full((BLOCK, 1), sink_ref[head], F32)
                                for _ in range(nbs) for head in range(N_ATTN_HEADS)], axis=0)
        m = jnp.maximum(jnp.max(s, axis=-1, keepdims=True), sink)
        p = jnp.exp(s - m)
        z = jnp.sum(p, axis=-1, keepdims=True) + jnp.exp(sink - m)
        p2 = _unfold(p * (1.0 / z), _tri(R), True)
        for b in range(nbs):
            for pair in range(N_ATTN_HEADS // 2):
                r0 = (b * N_ATTN_HEADS + 2 * pair) * BLOCK
                acc = (_dot(p2[r0:r0 + BLOCK, :], vcat[b][pair // 2][0])
                       + _dot(p2[r0 + BLOCK:r0 + 2 * BLOCK, :], vcat[b][pair // 2][1]))
                a_ref[b * BLOCK:(b + 1) * BLOCK, pair * 128:(pair + 1) * 128] = acc.astype(BF)
        for b, t in enumerate(_head_cols(m + jnp.log(z), nbs)):
            lse_ref[b * BLOCK:(b + 1) * BLOCK, :] = t
            lset_ref[:, b * BLOCK:(b + 1) * BLOCK] = t.T[:N_ATTN_HEADS, :]
        pl.when(n == steps - 1)(ex.wait)

    cur = lambda w: pl.BlockSpec((BLOCK * nbs, w), lambda n: (n, 0))
    prev = lambda w: pl.BlockSpec((BLOCK, w), lambda n: (jnp.maximum(n * nbs - 1, 0), 0))
    outs = _pcall(
        body, name="attn_fwd", grid=(steps,),
        in_specs=[_smem_full(), cur(ATTN_W), cur(KV_W), prev(KV_W), cur(KV_W), prev(KV_W)] + [ANY_SPEC] * ng,
        out_specs=[cur(ATTN_W), cur(BLOCK), pl.BlockSpec((N_ATTN_HEADS, BLOCK * nbs), lambda n: (0, n))]
        + [ANY_SPEC] * ng,
        out_shape=[jax.ShapeDtypeStruct((T, ATTN_W), BF), jax.ShapeDtypeStruct((T, BLOCK), F32),
                   jax.ShapeDtypeStruct((N_ATTN_HEADS, T), F32)] + _exchange_shapes(gathers, []),
        scratch_shapes=_Exchange.scratch(ng),
        compiler_params=_params(("arbitrary",)),
    )(sinks, qa, ka, ka, va, va, *gathers)
    return outs[:3], outs[3:]


def _ret_fwd(decay, qr, kr, vr, gr, d_intra, xi_b, zeta_b, ncs):
    T = qr.shape[0]
    nc = T // BLOCK
    H, C = N_RET_HEADS, RET_HEAD_DIM

    def body(decay_ref, q_ref, k_ref, v_ref, g_ref, d_ref, xi_ref, zeta_ref, o_ref, s_ref, r_ref, state):
        @pl.when(pl.program_id(0) == 0)
        def _():
            state[...] = jnp.zeros_like(state)

        pairs = [(b, h) for b in range(ncs) for h in range(H)]
        sl = lambda b, h: (slice(b * BLOCK, (b + 1) * BLOCK), slice(h * C, (h + 1) * C))
        tab = lambda ref: jnp.concatenate([ref[h] for _, h in pairs], axis=0)
        q = [q_ref[sl(b, h)] for b, h in pairs]
        k = [k_ref[sl(b, h)] for b, h in pairs]
        v = [v_ref[sl(b, h)] for b, h in pairs]
        inner = (jnp.concatenate([_dot_nt(q[i], k[i]) for i in range(len(pairs))], axis=0) * tab(d_ref)).astype(BF)
        kz = (jnp.concatenate(k, axis=0).astype(F32) * tab(zeta_ref)).astype(BF)
        o1 = [_dot(inner[i * BLOCK:(i + 1) * BLOCK, :], v[i]) for i in range(len(pairs))]
        kv = [_dot_tn(kz[i * BLOCK:(i + 1) * BLOCK, :], v[i]) for i in range(len(pairs))]
        st_b = [None] * len(pairs)
        for h in range(H):
            st = state[h]
            for b in range(ncs):
                i = b * H + h
                st_b[i] = st.astype(BF)
                s_ref[b, h] = st_b[i]
                st = decay_ref[h] * st + kv[i]
            state[h] = st
        o2 = jnp.concatenate([_dot(q[i], st_b[i]) for i in range(len(pairs))], axis=0)
        o = jnp.concatenate(o1, axis=0) + o2 * tab(xi_ref)
        mu = jnp.mean(o, axis=-1, keepdims=True)
        oc = o - mu
        rs = lax.rsqrt(jnp.mean(oc * oc, axis=-1, keepdims=True) + GN_EPS)
        g = jnp.concatenate([g_ref[sl(b, h)] for b, h in pairs], axis=0)
        r = (g * jax.nn.sigmoid(g) * (oc * rs)).astype(BF)
        for i, (b, h) in enumerate(pairs):
            o_ref[sl(b, h)] = o[i * BLOCK:(i + 1) * BLOCK, :]
            r_ref[sl(b, h)] = r[i * BLOCK:(i + 1) * BLOCK, :]

    cur = pl.BlockSpec((BLOCK * ncs, RET_W), lambda n: (n, 0))
    tab = pl.BlockSpec((H, C, C), lambda n: (0, 0, 0))
    return _pcall(
        body, name="ret_fwd", grid=(nc // ncs,),
        in_specs=[_smem_full(), cur, cur, cur, cur, tab, tab, tab],
        out_specs=[cur, pl.BlockSpec((ncs, H, C, C), lambda n: (n, 0, 0, 0)), cur],
        out_shape=[jax.ShapeDtypeStruct((T, RET_W), F32), jax.ShapeDtypeStruct((nc, H, C, C), BF),
                   jax.ShapeDtypeStruct((T, RET_W), BF)],
        scratch_shapes=[pltpu.VMEM((H, C, C), F32)],
        compiler_params=_params(("arbitrary",)),
    )(decay, qr, kr, vr, gr, d_intra, xi_b, zeta_b)


def _out_proj(a, r, w_out, x, g2, g3, tm):
    T = x.shape[0]

    def body(a_ref, r_ref, w_ref, x_ref, g2_ref, g3_ref, mixed_ref, x1_ref, h2_ref):
        mixed = _dot(a_ref[...], w_ref[:ATTN_W, :]) + _dot(r_ref[...], w_ref[ATTN_W:, :])
        mixed_ref[...] = mixed
        _, n2 = _rms_stats(mixed)
        x1 = x_ref[...] + n2 * g2_ref[...]
        x1_ref[...] = x1
        _, n3 = _rms_stats(x1)
        h2_ref[...] = (n3 * g3_ref[...]).astype(BF)

    return _pcall(
        body, name="out_proj", grid=(T // tm,),
        in_specs=[_rows(tm, ATTN_W), _rows(tm, RET_W), _vmem_full(), _rows(tm, D_MODEL), _const((1, D_MODEL)),
                  _const((1, D_MODEL))],
        out_specs=[_rows(tm, D_MODEL)] * 3,
        out_shape=[jax.ShapeDtypeStruct((T, D_MODEL), F32), jax.ShapeDtypeStruct((T, D_MODEL), F32),
                   jax.ShapeDtypeStruct((T, D_MODEL), BF)],
        compiler_params=_params(("parallel",)),
    )(a, r, w_out, x, g2, g3)


def _shift_down(cur, k, before):
    out = pltpu.roll(cur, k, 0)
    row = lax.broadcasted_iota(jnp.int32, before.shape, 0)
    top = jnp.where(row < k, pltpu.roll(before, k, 0), out[0:8])
    return jnp.concatenate([top, out[8:]], axis=0)


def _shift_up(cur, k, after):
    tm = cur.shape[0]
    out = pltpu.roll(cur, tm - k, 0)
    row = lax.broadcasted_iota(jnp.int32, after.shape, 0)
    bot = jnp.where(row >= 8 - k, pltpu.roll(after, 8 - k, 0), out[tm - 8:])
    return jnp.concatenate([out[:tm - 8], bot], axis=0)


def _gelu_parts(x):
    m = (-2.0 * GELU_C * GELU_A) * (x * x)
    s = 1.0 / (1.0 + jnp.exp(x * (m - 2.0 * GELU_C)))
    gelu = x * s
    dgelu = s + gelu * (1.0 - s) * (2.0 * GELU_C - 3.0 * m)
    return gelu, dgelu


def _ffn_fwd(h2, w_up8, conv_w8, conv_b8, w_down4, x1, g4, target, tm):
    T = h2.shape[0]
    nt = T // tm

    def body(h_ref, wu_ref, cwg_ref, cwv_ref, cbg_ref, cbv_ref, wd_ref, x1_ref, g_ref, t_ref,
             upg_ref, upv_ref, ug_ref, uv_ref, y_ref, dout_ref, dz_ref, dg4_ref, loss_ref, halo, z_acc):
        s = pl.program_id(1)
        first = pl.program_id(0) == 0

        @pl.when(first & (s == 0))
        def _():
            loss_ref[...] = jnp.zeros_like(loss_ref)
            dg4_ref[...] = jnp.zeros_like(dg4_ref)

        h = h_ref[...]
        u = []
        parts = ((cwg_ref, cbg_ref, upg_ref, ug_ref), (cwv_ref, cbv_ref, upv_ref, uv_ref))
        for part, (cw_ref, cb_ref, up_ref, u_ref) in enumerate(parts):
            cur = _dot(h, wu_ref[s + part * FF_PAIRS])
            up_ref[0] = cur.astype(BF)
            before = jnp.where(first, 0.0, halo[part, s])
            halo[part, s] = cur[tm - 8:tm, :]
            u_c = (cw_ref[0, pl.ds(0, 1), :] * _shift_down(cur, 2, before)
                   + cw_ref[0, pl.ds(1, 1), :] * _shift_down(cur, 1, before)
                   + cw_ref[0, pl.ds(2, 1), :] * cur + cb_ref[0])
            u_ref[0] = u_c
            u.append(u_c)
        gelu, _ = _gelu_parts(u[0])
        y = (gelu * u[1]).astype(BF)
        y_ref[0] = y
        z_part = _dot(y, wd_ref[s])

        @pl.when(s == 0)
        def _():
            z_acc[...] = z_part

        @pl.when(s > 0)
        def _():
            z_acc[...] += z_part

        @pl.when(s == FF_PAIRS - 1)
        def _():
            r4, n4 = _rms_stats(z_acc[...])
            err = x1_ref[...] + n4 * g_ref[...] - t_ref[...]
            dout = err * (1.0 / D_MODEL)
            dout_ref[...] = dout
            loss_ref[...] += 0.5 * jnp.sum(jnp.mean(err * err, axis=-1, keepdims=True), axis=0, keepdims=True)
            dg4_ref[...] += jnp.sum(dout * n4, axis=0, keepdims=True)
            dz_ref[...] = _rms_bwd(n4, r4, dout * g_ref[...]).astype(BF)

    rows = pl.BlockSpec((tm, D_MODEL), lambda i, s: (i, 0))
    one = lambda shape: pl.BlockSpec(shape, lambda i, s: tuple(0 for _ in shape))
    gate = lambda r, w: pl.BlockSpec((1, r, w), lambda i, s: (s, 0, 0))
    val = lambda r, w: pl.BlockSpec((1, r, w), lambda i, s: (s + FF_PAIRS, 0, 0))
    tile = pl.BlockSpec((1, tm, FF_SHARD), lambda i, s: (s, i, 0))
    half = lambda dt: jax.ShapeDtypeStruct((FF_PAIRS, T, FF_SHARD), dt)
    return _pcall(
        body, name="ffn_fwd", grid=(nt, FF_PAIRS),
        in_specs=[rows, _vmem_full(), gate(3, FF_SHARD), val(3, FF_SHARD), gate(1, FF_SHARD), val(1, FF_SHARD),
                  _vmem_full(), rows, one((1, D_MODEL)), rows],
        out_specs=[tile] * 5 + [rows, rows, one((1, D_MODEL)), one((8, 128))],
        out_shape=[half(BF), half(BF), half(F32), half(F32), half(BF), jax.ShapeDtypeStruct((T, D_MODEL), F32),
                   jax.ShapeDtypeStruct((T, D_MODEL), BF), jax.ShapeDtypeStruct((1, D_MODEL), F32),
                   jax.ShapeDtypeStruct((8, 128), F32)],
        scratch_shapes=[pltpu.VMEM((2, FF_PAIRS, 8, FF_SHARD), F32), pltpu.VMEM((tm, D_MODEL), F32)],
        compiler_params=_params(("arbitrary", "arbitrary")),
    )(h2, w_up8, conv_w8, conv_w8, conv_b8, conv_b8, w_down4, x1, g4, target)


def _ffn_bwd_a(dz, h2, w_down4, u_g, u_v, up_g, up_v, y4, conv_w8, tm):
    T = dz.shape[0]
    nt = T // tm

    def body(dz_ref, h_ref, wd_ref, ug_ref, uv_ref, upg_ref, upv_ref, y_ref, cwg_ref, cwv_ref,
             dupg_ref, dupv_ref, dcbg_ref, dcbv_ref, dcwg_ref, dcwv_ref, gwug_out, gwuv_out, gwd_out,
             carry, gwug_ref, gwuv_ref, gwd_ref):
        @pl.when(pl.program_id(1) == 0)
        def _():
            for ref in (dcbg_ref, dcbv_ref, dcwg_ref, dcwv_ref, gwug_ref, gwuv_ref, gwd_ref, carry):
                ref[...] = jnp.zeros_like(ref)

        dz = dz_ref[...]
        h = h_ref[...]
        dy = _dot_nt(dz, wd_ref[0])
        gwd_ref[0] += _dot_tn(y_ref[0], dz)
        gelu, dgelu = _gelu_parts(ug_ref[0])
        parts = ((0, dy * uv_ref[0] * dgelu, upg_ref, cwg_ref, dupg_ref, dcbg_ref, dcwg_ref, gwug_ref),
                 (1, dy * gelu, upv_ref, cwv_ref, dupv_ref, dcbv_ref, dcwv_ref, gwuv_ref))
        for part, d, up_ref, cw_ref, dup_ref, dcb_ref, dcw_ref, gwu_ref in parts:
            after = carry[part]
            d1 = _shift_up(d, 1, after)
            d2 = _shift_up(d, 2, after)
            carry[part] = d[0:8, :]
            upc = up_ref[0].astype(F32)
            dcb_ref[0] += jnp.sum(d, axis=0, keepdims=True)
            dcw_ref[0, pl.ds(2, 1), :] += jnp.sum(d * upc, axis=0, keepdims=True)
            dcw_ref[0, pl.ds(1, 1), :] += jnp.sum(d1 * upc, axis=0, keepdims=True)
            dcw_ref[0, pl.ds(0, 1), :] += jnp.sum(d2 * upc, axis=0, keepdims=True)
            dup = (cw_ref[0, pl.ds(2, 1), :] * d + cw_ref[0, pl.ds(1, 1), :] * d1
                   + cw_ref[0, pl.ds(0, 1), :] * d2).astype(BF)
            dup_ref[0] = dup
            gwu_ref[0] += _dot_tn(h, dup)

        @pl.when(pl.program_id(1) == nt - 1)
        def _():
            s = pl.program_id(0)
            pltpu.sync_copy(gwd_ref, gwd_out.at[pl.ds(s, 1)])
            for acc_ref, out in ((gwug_ref, gwug_out), (gwuv_ref, gwuv_out)):
                gwd_ref[0] = acc_ref[0].T
                pltpu.sync_copy(gwd_ref, out.at[pl.ds(s, 1)])

    rev = pl.BlockSpec((tm, D_MODEL), lambda s, i: (nt - 1 - i, 0))
    tile = pl.BlockSpec((1, tm, FF_SHARD), lambda s, i: (s, nt - 1 - i, 0))
    acc = lambda r, w: pl.BlockSpec((1, r, w), lambda s, i: (s, 0, 0))
    acc_val = pl.BlockSpec((1, 3, FF_SHARD), lambda s, i: (s + FF_PAIRS, 0, 0))
    half = lambda r, dt: jax.ShapeDtypeStruct((FF_PAIRS, r, FF_SHARD), dt)
    return _pcall(
        body, name="ffn_bwd_a", grid=(FF_PAIRS, nt),
        in_specs=[rev, rev, acc(FF_SHARD, D_MODEL), tile, tile, tile, tile, tile, acc(3, FF_SHARD), acc_val],
        out_specs=[tile, tile, acc(1, FF_SHARD), acc(1, FF_SHARD), acc(3, FF_SHARD), acc(3, FF_SHARD),
                   ANY_SPEC, ANY_SPEC, ANY_SPEC],
        out_shape=[half(T, BF), half(T, BF), half(1, F32), half(1, F32), half(3, F32), half(3, F32),
                   jax.ShapeDtypeStruct((FF_PAIRS, FF_SHARD, D_MODEL), F32),
                   jax.ShapeDtypeStruct((FF_PAIRS, FF_SHARD, D_MODEL), F32),
                   jax.ShapeDtypeStruct((FF_PAIRS, FF_SHARD, D_MODEL), F32)],
        scratch_shapes=[pltpu.VMEM((2, 8, FF_SHARD), F32), pltpu.VMEM((1, D_MODEL, FF_SHARD), F32),
                        pltpu.VMEM((1, D_MODEL, FF_SHARD), F32), pltpu.VMEM((1, FF_SHARD, D_MODEL), F32)],
        compiler_params=_params(("arbitrary", "arbitrary")),
    )(dz, h2, w_down4, u_g, u_v, up_g, up_v, y4, conv_w8, conv_w8)


def _ffn_bwd_b(dup_g, dup_v, w_up8, x1, dout, g3, mixed, g2, w_out, tm, gwu_g, gwu_v):
    T = x1.shape[0]
    nt = T // tm

    def body(dupg_ref, dupv_ref, wup_ref, x1_ref, dout_ref, g3_ref, mixed_ref, g2_ref, wout_ref, gwug_ref, gwuv_ref,
             dx1_ref, dmixed_ref, da_ref, dr_ref, dg3_ref, dg2_ref, pup_ref, *sems):
        ex = _Exchange([], [((gwug_ref, gwuv_ref), pup_ref)], *sems)

        @pl.when(pl.program_id(0) == 0)
        def _():
            ex.start()
            dg3_ref[...] = jnp.zeros_like(dg3_ref)
            dg2_ref[...] = jnp.zeros_like(dg2_ref)

        dh2 = jnp.zeros((tm, D_MODEL), F32)
        for s in range(FF_PAIRS):
            dh2 = dh2 + _dot_nt(dupg_ref[s], wup_ref[s]) + _dot_nt(dupv_ref[s], wup_ref[s + FF_PAIRS])
        r3, n3 = _rms_stats(x1_ref[...])
        dg3_ref[...] += jnp.sum(dh2 * n3, axis=0, keepdims=True)
        dx1 = dout_ref[...] + _rms_bwd(n3, r3, dh2 * g3_ref[...])
        dx1_ref[...] = dx1
        r2, n2 = _rms_stats(mixed_ref[...])
        dg2_ref[...] += jnp.sum(dx1 * n2, axis=0, keepdims=True)
        dmixed = _rms_bwd(n2, r2, dx1 * g2_ref[...]).astype(BF)
        dmixed_ref[...] = dmixed
        da_ref[...] = _dot_nt(dmixed, wout_ref[:ATTN_W, :])
        dr_ref[...] = _dot_nt(dmixed, wout_ref[ATTN_W:, :])
        pl.when(pl.program_id(0) == nt - 1)(ex.wait)

    half = pl.BlockSpec((FF_PAIRS, tm, FF_SHARD), lambda i: (0, i, 0))
    outs = _pcall(
        body, name="ffn_bwd_b", grid=(nt,),
        in_specs=[half, half, _vmem_full(), _rows(tm, D_MODEL), _rows(tm, D_MODEL), _const((1, D_MODEL)),
                  _rows(tm, D_MODEL), _const((1, D_MODEL)), _vmem_full(), ANY_SPEC, ANY_SPEC],
        out_specs=[_rows(tm, D_MODEL), _rows(tm, D_MODEL), _rows(tm, ATTN_W), _rows(tm, RET_W),
                   _const((1, D_MODEL)), _const((1, D_MODEL)), ANY_SPEC],
        out_shape=[jax.ShapeDtypeStruct((T, D_MODEL), F32), jax.ShapeDtypeStruct((T, D_MODEL), BF),
                   jax.ShapeDtypeStruct((T, ATTN_W), F32), jax.ShapeDtypeStruct((T, RET_W), F32),
                   jax.ShapeDtypeStruct((1, D_MODEL), F32), jax.ShapeDtypeStruct((1, D_MODEL), F32),
                   jax.ShapeDtypeStruct((N_DEV, FF_SHARD, D_MODEL), F32)],
        scratch_shapes=_Exchange.scratch(1),
        compiler_params=_params(("arbitrary",)),
    )(dup_g, dup_v, w_up8, x1, dout, g3, mixed, g2, w_out, gwu_g, gwu_v)
    return outs[:6], outs[6]


def _ret_bwd(decay, qr, kr, vr, gr, o, states, dr, d_intra, d_intra_t, xi_b, zeta_b, cos, sin_s, swaps, ncs):
    T = qr.shape[0]
    nc = T // BLOCK
    H, C = N_RET_HEADS, RET_HEAD_DIM
    ns = len(swaps)

    def body(decay_ref, q_ref, k_ref, v_ref, g_ref, o_ref, s_ref, dr_ref, d_ref, dt_ref, xi_ref, zeta_ref,
             cos_ref, sin_ref, *rest):
        ex = _exchange_of(rest[:ns], rest[ns + 1:2 * ns + 1], rest[2 * ns + 2:], 0)
        dret_ref, gstate = rest[ns], rest[2 * ns + 1]

        @pl.when(pl.program_id(0) == 0)
        def _():
            ex.start()
            gstate[...] = jnp.zeros_like(gstate)

        pairs = [(b, h) for b in range(ncs) for h in range(H)]
        n = len(pairs)
        sl = lambda b, h: (slice(b * BLOCK, (b + 1) * BLOCK), slice(h * C, (h + 1) * C))
        cat = lambda ref: jnp.concatenate([ref[sl(b, h)] for b, h in pairs], axis=0)
        tab = lambda ref: jnp.concatenate([ref[h] for _, h in pairs], axis=0)
        part = lambda x, i: x[i * BLOCK:(i + 1) * BLOCK, :]
        q = [q_ref[sl(b, h)] for b, h in pairs]
        k = [k_ref[sl(b, h)] for b, h in pairs]
        v = [v_ref[sl(b, h)] for b, h in pairs]
        g, o_all, dr_all = cat(g_ref), cat(o_ref), cat(dr_ref)
        mu = jnp.mean(o_all, axis=-1, keepdims=True)
        oc = o_all - mu
        rs = lax.rsqrt(jnp.mean(oc * oc, axis=-1, keepdims=True) + GN_EPS)
        on = oc * rs
        sg = jax.nn.sigmoid(g)
        dg = (dr_all * on * (sg * (1.0 + g * (1.0 - sg)))).astype(BF)
        don = dr_all * (g * sg)
        do = rs * (don - jnp.mean(don, axis=-1, keepdims=True) - on * jnp.mean(don * on, axis=-1, keepdims=True))
        do_b = do.astype(BF)
        dox_b = (do * tab(xi_ref)).astype(BF)
        zeta = tab(zeta_ref)
        kz = (jnp.concatenate(k, axis=0).astype(F32) * zeta).astype(BF)
        d_t = tab(dt_ref)
        da_b = (jnp.concatenate([_dot_nt(part(do_b, i), v[i]) for i in range(n)], axis=0) * tab(d_ref)).astype(BF)
        dat_b = (jnp.concatenate([_dot_nt(v[i], part(do_b, i)) for i in range(n)], axis=0) * d_t).astype(BF)
        mt_b = (jnp.concatenate([_dot_nt(k[i], q[i]) for i in range(n)], axis=0) * d_t).astype(BF)
        dq = [_dot(part(da_b, i), k[i]) + _dot_nt(part(dox_b, i), s_ref[pairs[i]]) for i in range(n)]
        dk1 = [_dot(part(dat_b, i), q[i]) for i in range(n)]
        dv1 = [_dot(part(mt_b, i), part(do_b, i)) for i in range(n)]
        qtd = [_dot_tn(q[i], part(dox_b, i)) for i in range(n)]
        gst_b = [None] * n
        for h in range(H):
            gst = gstate[h]
            for b in reversed(range(ncs)):
                i = b * H + h
                gst_b[i] = gst.astype(BF)
                gst = decay_ref[h] * gst + qtd[i]
            gstate[h] = gst
        dk2 = jnp.concatenate([_dot_nt(v[i], gst_b[i]) for i in range(n)], axis=0) * zeta
        dv = jnp.concatenate([dv1[i] + _dot(part(kz, i), gst_b[i]) for i in range(n)], axis=0).astype(BF)
        even = lax.broadcasted_iota(jnp.int32, (n * BLOCK, C), 1) % 2 == 0
        cos_t = jnp.concatenate([cos_ref[b * BLOCK:(b + 1) * BLOCK, :] for b, _ in pairs], axis=0)
        sin_t = jnp.concatenate([sin_ref[b * BLOCK:(b + 1) * BLOCK, :] for b, _ in pairs], axis=0)
        dq = jnp.concatenate(dq, axis=0)
        dk = jnp.concatenate(dk1, axis=0) + dk2
        dq = (dq * cos_t - _rot(dq, even) * sin_t).astype(BF)
        dk = ((dk * cos_t - _rot(dk, even) * sin_t) * RET_K_SCALE).astype(BF)
        for i, (b, h) in enumerate(pairs):
            rows = slice(b * BLOCK, (b + 1) * BLOCK)
            for j, x in enumerate((dq, dk, dv, dg)):
                dret_ref[rows, j * RET_W + h * C:j * RET_W + (h + 1) * C] = part(x, i)
        pl.when(pl.program_id(0) == steps - 1)(ex.wait)

    steps = nc // ncs
    rev = lambda w: pl.BlockSpec((BLOCK * ncs, w), lambda n: (steps - 1 - n, 0))
    tab = pl.BlockSpec((H, C, C), lambda n: (0, 0, 0))
    outs = _pcall(
        body, name="ret_bwd", grid=(steps,),
        in_specs=[_smem_full(), rev(RET_W), rev(RET_W), rev(RET_W), rev(RET_W), rev(RET_W),
                  pl.BlockSpec((ncs, H, C, C), lambda n: (steps - 1 - n, 0, 0, 0)), rev(RET_W), tab, tab, tab, tab,
                  rev(C), rev(C)] + [ANY_SPEC] * ns,
        out_specs=[rev(4 * RET_W)] + [ANY_SPEC] * ns,
        out_shape=[jax.ShapeDtypeStruct((T, 4 * RET_W), BF)] + _exchange_shapes([], swaps),
        scratch_shapes=[pltpu.VMEM((H, C, C), F32)] + _Exchange.scratch(ns),
        compiler_params=_params(("arbitrary",)),
    )(decay, qr, kr, vr, gr, o, states, dr, d_intra, d_intra_t, xi_b, zeta_b, cos, sin_s, *swaps)
    return outs[0], outs[1:]


def _attn_bwd_dq(sinks, qa, ka, va, da, lse, nbs, swaps):
    T = qa.shape[0]
    steps = T // (BLOCK * nbs)
    R = nbs * N_ATTN_HEADS * BLOCK
    ns = len(swaps)

    def body(sink_ref, q_ref, kc_ref, kp_ref, vc_ref, vp_ref, da_ref, lse_ref, *rest):
        ex = _exchange_of(rest[:ns], rest[ns + 3:2 * ns + 3], rest[2 * ns + 3:], 0)
        dq_ref, deltat_ref, dsink_ref = rest[ns:ns + 3]
        n = pl.program_id(0)

        @pl.when(n == 0)
        def _():
            ex.start()
            dsink_ref[...] = jnp.zeros_like(dsink_ref)

        kcat = _block_variants(kp_ref, kc_ref, nbs)
        vcat = _block_variants(vp_ref, vc_ref, nbs)
        tri1 = _tri(BLOCK)
        lane = lax.broadcasted_iota(jnp.int32, (BLOCK, BLOCK), 1)
        s_tiles, dp_tiles, lse_cols = [], [], []
        for b in range(nbs):
            rows = slice(b * BLOCK, (b + 1) * BLOCK)
            lse_tile = lse_ref[rows, :]
            for pair in range(N_ATTN_HEADS // 2):
                qp = _scaled(q_ref[rows, pair * 128:(pair + 1) * 128])
                dop = da_ref[rows, pair * 128:(pair + 1) * 128].astype(BF)
                for e in range(2):
                    s = _fold(_dot_nt(qp, kcat[b][pair // 2][e]), tri1, True)
                    if b == 0:
                        s = jnp.where(tri1 & (n == 0), MASK_VALUE, s)
                    s_tiles.append(s)
                    dp_tiles.append(_fold(_dot_nt(dop, vcat[b][pair // 2][e]), tri1, True))
                    lse_cols.append(jnp.sum(jnp.where(lane == 2 * pair + e, lse_tile, 0.0), axis=-1, keepdims=True))
        lse_c = jnp.concatenate(lse_cols, axis=0)
        p = jnp.exp(jnp.concatenate(s_tiles, axis=0) - lse_c)
        dp = jnp.concatenate(dp_tiles, axis=0)
        delta = jnp.sum(p * dp, axis=-1, keepdims=True)
        ds2 = _unfold(p * (dp - delta), _tri(R), True)
        for b in range(nbs):
            for pair in range(N_ATTN_HEADS // 2):
                r0 = (b * N_ATTN_HEADS + 2 * pair) * BLOCK
                acc = (_dot(ds2[r0:r0 + BLOCK, :], kcat[b][pair // 2][0])
                       + _dot(ds2[r0 + BLOCK:r0 + 2 * BLOCK, :], kcat[b][pair // 2][1]))
                dq_ref[b * BLOCK:(b + 1) * BLOCK, pair * 128:(pair + 1) * 128] = (acc * ATTN_SCALE).astype(BF)
        for b, t in enumerate(_head_cols(delta, nbs)):
            deltat_ref[:, b * BLOCK:(b + 1) * BLOCK] = t.T[:N_ATTN_HEADS, :]
        sink = jnp.concatenate([jnp.full((BLOCK, 1), sink_ref[head], F32)
                                for _ in range(nbs) for head in range(N_ATTN_HEADS)], axis=0)
        ds_sink = -jnp.exp(sink - lse_c) * delta
        row8 = lax.broadcasted_iota(jnp.int32, (N_ATTN_HEADS, BLOCK), 0)
        dsink = jnp.zeros((N_ATTN_HEADS, BLOCK), F32)
        for b in range(nbs):
            for head in range(N_ATTN_HEADS):
                r0 = (b * N_ATTN_HEADS + head) * BLOCK
                dsink = dsink + jnp.where(row8 == head, jnp.sum(ds_sink[r0:r0 + BLOCK, :], axis=0, keepdims=True), 0.0)
        dsink_ref[...] += dsink
        pl.when(n == steps - 1)(ex.wait)

    cur = lambda w: pl.BlockSpec((BLOCK * nbs, w), lambda n: (n, 0))
    prev = lambda w: pl.BlockSpec((BLOCK, w), lambda n: (jnp.maximum(n * nbs - 1, 0), 0))
    outs = _pcall(
        body, name="attn_bwd_dq", grid=(steps,),
        in_specs=[_smem_full(), cur(ATTN_W), cur(KV_W), prev(KV_W), cur(KV_W), prev(KV_W), cur(ATTN_W), cur(BLOCK)]
        + [ANY_SPEC] * ns,
        out_specs=[cur(ATTN_W), pl.BlockSpec((N_ATTN_HEADS, BLOCK * nbs), lambda n: (0, n)),
                   _const((N_ATTN_HEADS, BLOCK))] + [ANY_SPEC] * ns,
        out_shape=[jax.ShapeDtypeStruct((T, ATTN_W), BF), jax.ShapeDtypeStruct((N_ATTN_HEADS, T), F32),
                   jax.ShapeDtypeStruct((N_ATTN_HEADS, BLOCK), F32)] + _exchange_shapes([], swaps),
        scratch_shapes=_Exchange.scratch(ns),
        compiler_params=_params(("arbitrary",)),
    )(sinks, qa, ka, ka, va, va, da, lse, *swaps)
    return outs[:3], outs[3:]


def _attn_bwd_dkv(qa, ka, va, da, lse_t, delta_t, nbs):
    T = qa.shape[0]
    nb = T // BLOCK
    steps = nb // nbs
    R = nbs * N_ATTN_HEADS * BLOCK

    def body(qc_ref, qn_ref, dac_ref, dan_ref, k_ref, v_ref, lc_ref, ln_ref, dc_ref, dn_ref, dk_ref, dv_ref):
        n = pl.program_id(0)
        tri1 = _tri(BLOCK, True)
        lo = lax.broadcasted_iota(jnp.int32, (BLOCK, 128), 1) < HEAD_DIM
        kv = [_kv_variants(k_ref[b * BLOCK:(b + 1) * BLOCK, :]) for b in range(nbs)]
        vv = [_kv_variants(v_ref[b * BLOCK:(b + 1) * BLOCK, :]) for b in range(nbs)]
        qcat, docat = [], []
        s_tiles, dp_tiles, lse_tiles, delta_tiles = [], [], [], []
        for b in range(nbs):
            rows = slice(b * BLOCK, (b + 1) * BLOCK)
            nrows = slice((b + 1) * BLOCK, (b + 2) * BLOCK)
            inside = b < nbs - 1
            for pair in range(N_ATTN_HEADS // 2):
                ps = slice(pair * 128, (pair + 1) * 128)
                q2 = _scaled(jnp.concatenate([qc_ref[rows, ps], qc_ref[nrows, ps] if inside else qn_ref[:, ps]], axis=0))
                do2 = jnp.concatenate([dac_ref[rows, ps], dac_ref[nrows, ps] if inside else dan_ref[:, ps]],
                                      axis=0).astype(BF)
                qcat.append(q2)
                docat.append(do2)
                for e in range(2):
                    one = pl.ds(2 * pair + e, 1)
                    s = _fold(_dot_nt(kv[b][pair // 2][e], q2), tri1, False)
                    if not inside:
                        s = jnp.where(tri1 & (n == steps - 1), MASK_VALUE, s)
                    s_tiles.append(s)
                    dp_tiles.append(_fold(_dot_nt(vv[b][pair // 2][e], do2), tri1, False))
                    lse_tiles.append(jnp.where(tri1, lc_ref[one, nrows] if inside else ln_ref[one, :], lc_ref[one, rows]))
                    delta_tiles.append(jnp.where(tri1, dc_ref[one, nrows] if inside else dn_ref[one, :],
                                                 dc_ref[one, rows]))
        pt = jnp.exp(jnp.concatenate(s_tiles, axis=0) - jnp.concatenate(lse_tiles, axis=0))
        dst = pt * (jnp.concatenate(dp_tiles, axis=0) - jnp.concatenate(delta_tiles, axis=0))
        tri = _tri(R, True)
        pt2 = _unfold(pt, tri, False)
        dst2 = _unfold(dst, tri, False)
        for b in range(nbs):
            dk = jnp.zeros((BLOCK, 128), F32)
            dv = jnp.zeros((BLOCK, 128), F32)
            for pair in range(N_ATTN_HEADS // 2):
                h = pair // 2
                for e in range(2):
                    r0 = (b * N_ATTN_HEADS + 2 * pair + e) * BLOCK
                    half = lo if e == 0 else jnp.logical_not(lo)
                    dv_e = jnp.where(half, _dot(pt2[r0:r0 + BLOCK, :], docat[b * 4 + pair]), 0.0)
                    dk_e = jnp.where(half, _dot(dst2[r0:r0 + BLOCK, :], qcat[b * 4 + pair]), 0.0)
                    if e != h:
                        dv_e = pltpu.roll(dv_e, HEAD_DIM, 1)
                        dk_e = pltpu.roll(dk_e, HEAD_DIM, 1)
                    dv = dv + dv_e
                    dk = dk + dk_e
            dk_ref[b * BLOCK:(b + 1) * BLOCK, :] = dk.astype(BF)
            dv_ref[b * BLOCK:(b + 1) * BLOCK, :] = dv.astype(BF)

    cur = lambda w: pl.BlockSpec((BLOCK * nbs, w), lambda n: (n, 0))
    nxt = lambda w: pl.BlockSpec((BLOCK, w), lambda n: (jnp.minimum((n + 1) * nbs, nb - 1), 0))
    tcur = pl.BlockSpec((N_ATTN_HEADS, BLOCK * nbs), lambda n: (0, n))
    tnxt = pl.BlockSpec((N_ATTN_HEADS, BLOCK), lambda n: (0, jnp.minimum((n + 1) * nbs, nb - 1)))
    return _pcall(
        body, name="attn_bwd_dkv", grid=(steps,),
        in_specs=[cur(ATTN_W), nxt(ATTN_W), cur(ATTN_W), nxt(ATTN_W), cur(KV_W), cur(KV_W), tcur, tnxt, tcur, tnxt],
        out_specs=[cur(KV_W), cur(KV_W)],
        out_shape=[jax.ShapeDtypeStruct((T, KV_W), BF), jax.ShapeDtypeStruct((T, KV_W), BF)],
        compiler_params=_params(("parallel",)),
    )(qa, qa, da, da, ka, va, lse_t, lse_t, delta_t, delta_t)


def _in_proj_bwd(dqa, dka, dva, dret, w_in, x, g1, dx1, tm):
    T = x.shape[0]

    def body(dqa_ref, dka_ref, dva_ref, dret_ref, w_ref, x_ref, g_ref, dx1_ref, dx_ref, dg1_ref):
        @pl.when(pl.program_id(0) == 0)
        def _():
            dg1_ref[...] = jnp.zeros_like(dg1_ref)

        dh = (_dot(dqa_ref[...], w_ref[QA0:QA0 + ATTN_W, :]) + _dot(dka_ref[...], w_ref[KA0:KA0 + KV_W, :])
              + _dot(dva_ref[...], w_ref[VA0:VA0 + KV_W, :]) + _dot(dret_ref[...], w_ref[QR0:IN_W, :]))
        r, n = _rms_stats(x_ref[...])
        dg1_ref[...] += jnp.sum(dh * n, axis=0, keepdims=True)
        dx_ref[...] = dx1_ref[...] + _rms_bwd(n, r, dh * g_ref[...])

    return _pcall(
        body, name="in_proj_bwd", grid=(T // tm,),
        in_specs=[_rows(tm, ATTN_W), _rows(tm, KV_W), _rows(tm, KV_W), _rows(tm, 4 * RET_W), _vmem_full(),
                  _rows(tm, D_MODEL), _const((1, D_MODEL)), _rows(tm, D_MODEL)],
        out_specs=[_rows(tm, D_MODEL), _const((1, D_MODEL))],
        out_shape=[jax.ShapeDtypeStruct((T, D_MODEL), F32), jax.ShapeDtypeStruct((1, D_MODEL), F32)],
        compiler_params=_params(("arbitrary",)),
    )(dqa, dka, dva, dret, w_in, x, g1, dx1)


def _wgrad(a_list, b_list, tk, name):
    T = a_list[0].shape[0]
    na, nbb = len(a_list), len(b_list)
    m_sizes = [a.shape[1] for a in a_list]
    n_sizes = [b.shape[1] for b in b_list]
    M, N = sum(m_sizes), sum(n_sizes)
    nk = T // tk
    chunk = 512

    def body(*refs):
        a_refs, b_refs = refs[:na], refs[na:na + nbb]
        out_ref, acc = refs[na + nbb], refs[na + nbb + 1]
        k = pl.program_id(0)

        @pl.when(k == 0)
        def _():
            acc[...] = jnp.zeros_like(acc)

        r0 = 0
        for ai in range(na):
            a = a_refs[ai][...]
            c0 = 0
            for bi in range(nbb):
                for s in range(0, n_sizes[bi], chunk):
                    w = min(chunk, n_sizes[bi] - s)
                    acc[r0:r0 + m_sizes[ai], c0 + s:c0 + s + w] += _dot_tn(a, b_refs[bi][:, s:s + w])
                c0 += n_sizes[bi]
            r0 += m_sizes[ai]

        @pl.when(k == nk - 1)
        def _():
            pltpu.sync_copy(acc, out_ref)

    return _pcall(
        body, name=name, grid=(nk,),
        in_specs=[_rows(tk, w) for w in m_sizes + n_sizes],
        out_specs=pl.BlockSpec(memory_space=pl.ANY),
        out_shape=jax.ShapeDtypeStruct((M, N), F32),
        scratch_shapes=[pltpu.VMEM((M, N), F32)],
        compiler_params=_params(("arbitrary",)),
    )(*a_list, *b_list)


def _adamw_math(w, g, m, v):
    m = ADAM_B1 * m + (1.0 - ADAM_B1) * g
    v = ADAM_B2 * v + (1.0 - ADAM_B2) * (g * g)
    m_hat = m / (1.0 - ADAM_B1 ** ADAM_STEP)
    v_hat = v / (1.0 - ADAM_B2 ** ADAM_STEP)
    delta = -ADAM_LR * (m_hat / (jnp.sqrt(v_hat) + ADAM_EPS) + ADAM_WD * w)
    return delta, m, v


def _sum_parts(parts_ref):
    g = parts_ref[0].astype(F32)
    for i in range(1, N_DEV):
        g = g + parts_ref[i].astype(F32)
    return g


def _adamw_shard(parts, w, m, v, tr, name):
    R, C = w.shape

    def body(p_ref, w_ref, m_ref, v_ref, g_ref, d_ref, nm_ref, nv_ref):
        g = _sum_parts(p_ref)
        g_ref[...] = g
        d_ref[...], nm_ref[...], nv_ref[...] = _adamw_math(w_ref[...], g, m_ref[...], v_ref[...])

    blk = pl.BlockSpec((tr, C), lambda i: (i, 0))
    return _pcall(
        body, name=name, grid=(R // tr,),
        in_specs=[pl.BlockSpec((N_DEV, tr, C), lambda i: (0, i, 0)), blk, blk, blk],
        out_specs=[blk] * 4,
        out_shape=[jax.ShapeDtypeStruct((R, C), F32)] * 4,
        compiler_params=_params(("parallel",)),
    )(parts, w, m, v)


def _sum_small(parts):
    def body(p_ref, g_ref):
        g_ref[...] = _sum_parts(p_ref)

    return _pcall(body, name="sum_small", out_shape=jax.ShapeDtypeStruct(parts.shape[1:], F32),
                  in_specs=[_vmem_full()], out_specs=_vmem_full())(parts)


def _adamw_small(g, w, m, v, name):
    def body(g_ref, w_ref, m_ref, v_ref, d_ref, nm_ref, nv_ref):
        d_ref[...], nm_ref[...], nv_ref[...] = _adamw_math(w_ref[...], g_ref[...], m_ref[...], v_ref[...])

    return _pcall(body, name=name, out_shape=[jax.ShapeDtypeStruct(w.shape, F32)] * 3,
                  in_specs=[_vmem_full()] * 4, out_specs=[_vmem_full()] * 3)(g, w, m, v)


def _tables(T):
    h, c = N_RET_HEADS, BLOCK
    pos = jnp.arange(T, dtype=F32)
    angle = 1.0 / jnp.power(10000.0, jnp.linspace(0.0, 1.0, RET_HEAD_DIM // 2, dtype=F32))
    angle = jnp.repeat(angle, 2)
    sin = jnp.sin(pos[:, None] * angle[None])
    cos = jnp.cos(pos[:, None] * angle[None])
    even = (jnp.arange(RET_HEAD_DIM) % 2 == 0)[None, :]
    sin_s = jnp.where(even, -sin, sin)
    log_gamma = jnp.log(1.0 - jnp.power(2.0, -5.0 - jnp.arange(h, dtype=F32)))
    idx = jnp.arange(c, dtype=F32)
    rel = idx[:, None] - idx[None, :]
    d_intra = jnp.where(rel[None] >= 0, jnp.exp(log_gamma[:, None, None] * jnp.maximum(rel, 0.0)[None]), 0.0)
    xi = jnp.exp(log_gamma[None, :] * (idx[:, None] + 1.0))
    zeta = jnp.exp(log_gamma[None, :] * (c - 1.0 - idx[:, None]))
    decay = jnp.exp(log_gamma * c)
    xi_b = jnp.broadcast_to(xi.T[:, :, None], (h, c, RET_HEAD_DIM))
    zeta_b = jnp.broadcast_to(zeta.T[:, :, None], (h, c, RET_HEAD_DIM))
    return cos, sin_s, d_intra, jnp.swapaxes(d_intra, 1, 2), xi_b, zeta_b, decay


def _from_shards(sh):
    n, r, cols = sh.shape
    return jnp.swapaxes(sh, 0, 1).reshape(r, n * cols)


SMALL_ROWS = 216


def _pack_small(gains, conv_b, conv_w, sinks, scalar=None):
    last = jnp.concatenate([sinks.reshape(1, 8), jnp.zeros((1, 1), F32) if scalar is None else scalar.reshape(1, 1)],
                           axis=1)
    parts = [g.reshape(8, 128) for g in gains] + [conv_b.reshape(44, 128), conv_w.reshape(132, 128),
                                                  jnp.pad(last, ((0, 0), (0, 119)))]
    packed = jnp.concatenate(parts, axis=0)
    return jnp.pad(packed, ((0, SMALL_ROWS - packed.shape[0]), (0, 0)))


def kernel(x, mix_pre_norm, w_in, attn_sinks, w_out, mix_post_norm, ffn_pre_norm, w_up, conv_w, conv_b, w_down, ffn_post_norm, loss_target, m_mix_pre_norm, m_w_in, m_attn_sinks, m_w_out, m_mix_post_norm, m_ffn_pre_norm, m_w_up, m_conv_w, m_conv_b, m_w_down, m_ffn_post_norm, v_mix_pre_norm, v_w_in, v_attn_sinks, v_w_out, v_mix_post_norm, v_ffn_pre_norm, v_w_up, v_conv_w, v_conv_b, v_w_down, v_ffn_post_norm):
    T = x.shape[1]
    tm = min(512, T)
    tm_big = min(1024, T)
    tk_grad = min(2048, T)
    nbs = min(8, T // BLOCK)
    x2 = x.reshape(T, D_MODEL)
    target = loss_target.reshape(T, D_MODEL)
    me = 4 * lax.axis_index("x") + 2 * lax.axis_index("y") + lax.axis_index("c")

    g_in, g_cw = _exchange_call([jnp.swapaxes(w_in, 1, 2)[0].astype(BF), conv_w[0]], [], "gather_w_in")
    w_in_f = g_in.reshape(IN_W, D_MODEL)
    cos, sin_s, d_intra, d_intra_t, xi_b, zeta_b, decay = _tables(T)
    sinks = attn_sinks.reshape(N_ATTN_HEADS)

    (h1, qa, ka, va, qr, kr, vr, gr), (w_up8,) = _in_proj(
        x2, mix_pre_norm, w_in_f, cos, sin_s, tm_big, [w_up[0].astype(BF)])
    (a, lse, lse_t), (g_down, g_out) = _attn_fwd(sinks, qa, ka, va, nbs,
                                                 [w_down[0].astype(BF), w_out[0].astype(BF)])
    w_out_f = g_out.reshape(D_MODEL, D_MODEL)
    o, states, r = _ret_fwd(decay, qr, kr, vr, gr, d_intra, xi_b, zeta_b, nbs)
    mixed, x1, h2 = _out_proj(a, r, w_out_f, x2, mix_post_norm, ffn_pre_norm, tm_big)
    w_down4 = g_down.reshape(FF_PAIRS, FF_SHARD, D_MODEL)
    up_g, up_v, u_g, u_v, y4, dout, dz, dg4, loss_acc = _ffn_fwd(
        h2, w_up8, g_cw, conv_b.reshape(N_DEV, 1, FF_SHARD), w_down4, x1, ffn_post_norm, target, tm)

    dup_g, dup_v, dcb_g, dcb_v, dcw_g, dcw_v, gwu_g, gwu_v, gw_down4 = _ffn_bwd_a(
        dz, h2, w_down4, u_g, u_v, up_g, up_v, y4, g_cw, tm_big)
    dcb = jnp.concatenate([dcb_g, dcb_v], axis=0).reshape(1, 2 * D_FF)
    dcw = _from_shards(jnp.concatenate([dcw_g, dcw_v], axis=0))
    gw_down = gw_down4.reshape(D_FF, D_MODEL)
    (dx1, dmixed, da, dr, dg3, dg2), p_up = _ffn_bwd_b(
        dup_g, dup_v, w_up8, x1, dout, ffn_pre_norm, mixed, mix_post_norm, w_out_f, tm, gwu_g, gwu_v)
    gw_out = _wgrad([a, r], [dmixed], tk_grad, "wgrad_out")
    dret, (p_down,) = _ret_bwd(decay, qr, kr, vr, gr, o, states, dr, d_intra, d_intra_t, xi_b, zeta_b, cos, sin_s,
                               [gw_down.reshape(N_DEV, D_FF // N_DEV, D_MODEL)], nbs)
    (dqa, delta_t, dsink), (p_out,) = _attn_bwd_dq(sinks, qa, ka, va, da, lse, nbs,
                                                   [gw_out.reshape(N_DEV, D_MODEL // N_DEV, D_MODEL)])
    dka, dva = _attn_bwd_dkv(qa, ka, va, da, lse_t, delta_t, nbs)
    grad_x, dg1 = _in_proj_bwd(dqa, dka, dva, dret, w_in_f, x2, mix_pre_norm, dx1, tm)
    gw_in = _wgrad([h1], [dqa, dka, dva, dret], tk_grad, "wgrad_in")

    small = _pack_small([dg1, dg2, dg3, dg4], dcb, dcw, dsink[:, 0], loss_acc[0, 0])
    gw_in_t = gw_in.T.reshape(N_DEV, IN_W // N_DEV, D_MODEL).astype(BF)
    small_all, p_in = _exchange_call([small], [gw_in_t], "exchange_last")
    g_small = _sum_small(small_all)
    loss = g_small[208, N_ATTN_HEADS]

    t_in = lambda a: jnp.swapaxes(a, 1, 2)[0]
    g_w_in, d_w_in, nm_w_in, nv_w_in = [o.T for o in _adamw_shard(
        p_in, t_in(w_in), t_in(m_w_in), t_in(v_w_in), 176, "adamw_in")]
    g_w_up, d_w_up, nm_w_up, nv_w_up = [o.T for o in _adamw_shard(
        p_up, t_in(w_up), t_in(m_w_up), t_in(v_w_up), 176, "adamw_up")]
    g_w_out, d_w_out, nm_w_out, nv_w_out = _adamw_shard(p_out, w_out[0], m_w_out[0], v_w_out[0], 128, "adamw_out")
    g_w_down, d_w_down, nm_w_down, nv_w_down = _adamw_shard(p_down, w_down[0], m_w_down[0], v_w_down[0], 176,
                                                            "adamw_down")
    gains = [mix_pre_norm, mix_post_norm, ffn_pre_norm, ffn_post_norm]
    m_gains = [m_mix_pre_norm, m_mix_post_norm, m_ffn_pre_norm, m_ffn_post_norm]
    v_gains = [v_mix_pre_norm, v_mix_post_norm, v_ffn_pre_norm, v_ffn_post_norm]
    zeros_cw = jnp.zeros((3, 2 * D_FF), F32)
    w_small = _pack_small(gains, conv_b, zeros_cw, attn_sinks)
    m_small = _pack_small(m_gains, m_conv_b, zeros_cw, m_attn_sinks)
    v_small = _pack_small(v_gains, v_conv_b, zeros_cw, v_attn_sinks)
    d_small, nm_small, nv_small = _adamw_small(g_small, w_small, m_small, v_small, "adamw_small")
    shard_cols = 2 * D_FF // N_DEV
    g_cw = lax.dynamic_slice(g_small[76:208].reshape(3, 2 * D_FF), (0, me * shard_cols), (3, shard_cols))
    d_cw, nm_cw, nv_cw = _adamw_small(g_cw, conv_w[0], m_conv_w[0], v_conv_w[0], "adamw_conv_w")

    def unpack(p):
        gains_o = [p[8 * i:8 * i + 8].reshape(1, D_MODEL) for i in range(4)]
        return gains_o, p[32:76].reshape(1, 2 * D_FF), p[208:209, :N_ATTN_HEADS]

    def leaves(p, w_in_s, w_out_s, w_up_s, cw_s, w_down_s):
        (pre1, post1, pre2, post2), cb, sk = unpack(p)
        return [pre1, w_in_s[None], sk, w_out_s[None], post1, pre2, w_up_s[None], cw_s[None], cb, w_down_s[None],
                post2]

    return (loss, grad_x.reshape(1, T, D_MODEL),
            *leaves(g_small, g_w_in, g_w_out, g_w_up, g_cw, g_w_down),
            *leaves(d_small, d_w_in, d_w_out, d_w_up, d_cw, d_w_down),
            *leaves(nm_small, nm_w_in, nm_w_out, nm_w_up, nm_cw, nm_w_down),
            *leaves(nv_small, nv_w_in, nv_w_out, nv_w_up, nv_cw, nv_w_down))
```

```python
import functools
import math

import jax
import jax.numpy as jnp
from jax import lax
from jax.experimental import pallas as pl
from jax.experimental.pallas import tpu as pltpu

F32 = jnp.float32
BF = jnp.bfloat16

N_DEV = 8
D_MODEL = 1024
HEAD_DIM = 64
ATTN_W = 512
N_ATTN_HEADS = 8
KV_W = 128
BLOCK = 128
RET_W = 512
N_RET_HEADS = 4
RET_HEAD_DIM = 128
IN_W = 2816
D_FF = 2816
RMS_EPS = 1e-6
GN_EPS = 1e-6
MASK_VALUE = -1e30
ATTN_SCALE = HEAD_DIM ** -0.5
RET_K_SCALE = RET_HEAD_DIM ** -0.5
GELU_C = math.sqrt(2.0 / math.pi)
GELU_A = 0.044715

ADAM_LR = 0.001
ADAM_B1 = 0.9
ADAM_B2 = 0.999
ADAM_EPS = 1e-08
ADAM_WD = 0.01
ADAM_STEP = 10

VMEM_LIMIT_BYTES = 56 * 1024 * 1024
FF_SHARD = 2 * D_FF // N_DEV
FF_PAIRS = N_DEV // 2

QA0, KA0, VA0, QR0, KR0, VR0, GR0 = 0, 512, 640, 768, 1280, 1792, 2304

MESH_ID = pl.DeviceIdType.MESH


def _pcall(body, **kw):
    return pl.pallas_call(body, **kw)


def _params(sem=None):
    return pltpu.CompilerParams(dimension_semantics=sem, vmem_limit_bytes=VMEM_LIMIT_BYTES)


def _dot(a, b):
    return jnp.dot(a, b, preferred_element_type=F32)


def _dot_nt(a, b):
    return lax.dot_general(a, b, (((1,), (1,)), ((), ())), preferred_element_type=F32)


def _dot_tn(a, b):
    return lax.dot_general(a, b, (((0,), (0,)), ((), ())), preferred_element_type=F32)


def _vmem_full():
    return pl.BlockSpec(memory_space=pltpu.VMEM)


def _smem_full():
    return pl.BlockSpec(memory_space=pltpu.SMEM)


def _rows(tm, w):
    return pl.BlockSpec((tm, w), lambda i: (i, 0))


def _const(shape):
    return pl.BlockSpec(shape, lambda i: tuple(0 for _ in shape))


def _rms_stats(x):
    r = lax.rsqrt(jnp.mean(x * x, axis=-1, keepdims=True) + RMS_EPS)
    return r, x * r


def _rms_bwd(n, r, dn):
    return r * (dn - n * jnp.mean(dn * n, axis=-1, keepdims=True))


def _rot(x, even):
    w = x.shape[1]
    return jnp.where(even, pltpu.roll(x, w - 1, 1), pltpu.roll(x, 1, 1))


def _peers():
    x, y, c = lax.axis_index("x"), lax.axis_index("y"), lax.axis_index("c")
    flips = [(0, 0, 1), (1, 0, 0), (0, 1, 0), (1, 1, 0), (1, 0, 1), (0, 1, 1), (1, 1, 1)]
    peers = [(x ^ fx, y ^ fy, c ^ fc) for fx, fy, fc in flips]
    return 4 * x + 2 * y + c, peers


SAME_CORE_PEERS = 4


class _Exchange:
    def __init__(self, gathers, swaps, send_sems, recv_sems, local_sems):
        self.me, self.peers = _peers()
        self.slots = [4 * px + 2 * py + pc for px, py, pc in self.peers]
        self.pairs = [(src, dst, True) for src, dst in gathers] + [(src, dst, False) for src, dst in swaps]
        self.send_sems, self.recv_sems, self.local_sems = send_sems, recv_sems, local_sems

    @staticmethod
    def scratch(n):
        return [pltpu.SemaphoreType.DMA((n, N_DEV - 1)), pltpu.SemaphoreType.DMA((n, N_DEV - 1)),
                pltpu.SemaphoreType.DMA((n,))]

    def _parts(self, a, slot):
        src, _, whole = self.pairs[a]
        half = N_DEV // 2
        if whole:
            return [(None, src)]
        if isinstance(src, tuple):
            return [(slot < half, src[0].at[jnp.minimum(slot, half - 1)]),
                    (slot >= half, src[1].at[jnp.maximum(slot - half, 0)])]
        return [(None, src.at[slot])]

    def _local(self, a, src):
        return pltpu.make_async_copy(src, self.pairs[a][1].at[self.me], self.local_sems.at[a])

    def _remote(self, a, k, src, slot):
        return pltpu.make_async_remote_copy(
            src_ref=src, dst_ref=self.pairs[a][1].at[slot], send_sem=self.send_sems.at[a, k],
            recv_sem=self.recv_sems.at[a, k], device_id=self.peers[k], device_id_type=MESH_ID)

    def start(self):
        def go(cond, copy):
            if cond is None:
                copy.start()
            else:
                pl.when(cond)(copy.start)

        for a in range(len(self.pairs)):
            for cond, src in self._parts(a, self.me):
                go(cond, self._local(a, src))
            for k in range(SAME_CORE_PEERS if self.pairs[a][2] else N_DEV - 1):
                for cond, src in self._parts(a, self.slots[k]):
                    go(cond, self._remote(a, k, src, self.me))

    def _pass_on(self, a, j):
        k = j + SAME_CORE_PEERS - 1
        block = self.pairs[a][1].at[self.slots[j]]
        return pltpu.make_async_remote_copy(
            src_ref=block, dst_ref=block, send_sem=self.send_sems.at[a, k], recv_sem=self.recv_sems.at[a, k],
            device_id=self.peers[0], device_id_type=MESH_ID)

    def wait(self):
        for a in range(len(self.pairs)):
            src = self._parts(a, self.me)[0][1]
            if self.pairs[a][2]:
                for j in range(1, SAME_CORE_PEERS):
                    self._remote(a, j, src, self.slots[j]).wait_recv()
                    self._pass_on(a, j).start()
                self._remote(a, 0, src, self.slots[0]).wait_recv()
            for k in range(SAME_CORE_PEERS if self.pairs[a][2] else 0, N_DEV - 1):
                self._remote(a, k, src, self.slots[k]).wait_recv()
        for a in range(len(self.pairs)):
            src = self._parts(a, self.me)[0][1]
            for k in range(SAME_CORE_PEERS if self.pairs[a][2] else N_DEV - 1):
                self._remote(a, k, src, self.me).wait_send()
            if self.pairs[a][2]:
                for j in range(1, SAME_CORE_PEERS):
                    self._pass_on(a, j).wait_send()
            self._local(a, src).wait()


ANY_SPEC = pl.BlockSpec(memory_space=pl.ANY)


def _exchange_shapes(gathers, swaps):
    return ([jax.ShapeDtypeStruct((N_DEV,) + a.shape, a.dtype) for a in gathers]
            + [jax.ShapeDtypeStruct(a.shape, a.dtype) for a in swaps])


def _exchange_of(ins, outs, sems, ng):
    return _Exchange(list(zip(ins[:ng], outs[:ng])), list(zip(ins[ng:], outs[ng:])), *sems)


def _exchange_call(gathers, swaps, name):
    ng, ns = len(gathers), len(swaps)
    n = ng + ns

    def body(*refs):
        ex = _exchange_of(refs[:n], refs[n:2 * n], refs[2 * n:], ng)
        ex.start()
        ex.wait()

    return _pcall(
        body, name=name, out_shape=_exchange_shapes(gathers, swaps),
        in_specs=[ANY_SPEC] * (ng + ns), out_specs=[ANY_SPEC] * (ng + ns),
        scratch_shapes=_Exchange.scratch(ng + ns),
    )(*gathers, *swaps)


def _in_proj(x, g1, w_in, cos, sin_s, tm, gathers):
    T = x.shape[0]
    ng = len(gathers)
    nt = T // tm

    def body(x_ref, g_ref, w_ref, cos_ref, sin_ref, *rest):
        ex = _exchange_of(rest[:ng], rest[ng + 8:2 * ng + 8], rest[2 * ng + 8:], ng)
        h_ref, qa_ref, ka_ref, va_ref, qr_ref, kr_ref, vr_ref, gr_ref = rest[ng:ng + 8]
        pl.when(pl.program_id(0) == 0)(ex.start)
        r, n = _rms_stats(x_ref[...])
        h = (n * g_ref[...]).astype(BF)
        h_ref[...] = h

        def proj(c0, w):
            return _dot_nt(h, w_ref[c0:c0 + w, :])

        qa_ref[...] = proj(QA0, ATTN_W).astype(BF)
        kva = proj(KA0, 2 * KV_W)
        ka_ref[...] = kva[:, :KV_W].astype(BF)
        va_ref[...] = kva[:, KV_W:].astype(BF)
        vr_ref[...] = proj(VR0, RET_W).astype(BF)
        gr_ref[...] = proj(GR0, RET_W)
        cos_t, sin_t = cos_ref[...], sin_ref[...]
        even = lax.broadcasted_iota(jnp.int32, (tm, RET_HEAD_DIM), 1) % 2 == 0
        for c0, scale, out_ref in ((QR0, None, qr_ref), (KR0, RET_K_SCALE, kr_ref)):
            full = proj(c0, RET_W)
            for hd in range(N_RET_HEADS):
                cs = slice(hd * RET_HEAD_DIM, (hd + 1) * RET_HEAD_DIM)
                t = full[:, cs] if scale is None else full[:, cs] * scale
                out_ref[:, cs] = (t * cos_t + _rot(t, even) * sin_t).astype(BF)
        pl.when(pl.program_id(0) == nt - 1)(ex.wait)

    widths = [D_MODEL, ATTN_W, KV_W, KV_W, RET_W, RET_W, RET_W, RET_W]
    dts = [BF] * 7 + [F32]
    outs = _pcall(
        body, name="in_proj", grid=(nt,),
        in_specs=[_rows(tm, D_MODEL), _const((1, D_MODEL)), _vmem_full(), _rows(tm, RET_HEAD_DIM),
                  _rows(tm, RET_HEAD_DIM)] + [ANY_SPEC] * ng,
        out_specs=[_rows(tm, w) for w in widths] + [ANY_SPEC] * ng,
        out_shape=[jax.ShapeDtypeStruct((T, w), dt) for w, dt in zip(widths, dts)] + _exchange_shapes(gathers, []),
        scratch_shapes=_Exchange.scratch(ng),
        compiler_params=_params(("arbitrary",)),
    )(x, g1, w_in, cos, sin_s, *gathers)
    return outs[:8], outs[8:]


def _kv_variants(kk):
    kf = kk.astype(F32)
    lo = lax.broadcasted_iota(jnp.int32, kf.shape, 1) < HEAD_DIM
    h0_lo = jnp.where(lo, kf, 0.0)
    h1_hi = jnp.where(lo, 0.0, kf)
    h0_hi = pltpu.roll(h0_lo, HEAD_DIM, 1)
    h1_lo = pltpu.roll(h1_hi, HEAD_DIM, 1)
    return [[h0_lo.astype(BF), h0_hi.astype(BF)], [h1_lo.astype(BF), h1_hi.astype(BF)]]


def _col_to_tile(tile, col, head):
    lane = lax.broadcasted_iota(jnp.int32, tile.shape, 1)
    return jnp.where(lane == head, col, tile)


def _tri(rows, key_major=False):
    i = lax.broadcasted_iota(jnp.int32, (rows, BLOCK), 0) & (BLOCK - 1)
    j = lax.broadcasted_iota(jnp.int32, (rows, BLOCK), 1)
    return i > j if key_major else j > i


def _fold(x2, tri, first_above):
    a, b = x2[:, :BLOCK], x2[:, BLOCK:]
    return jnp.where(tri, a, b) if first_above else jnp.where(tri, b, a)


def _unfold(x, tri, first_above):
    up, low = jnp.where(tri, x, 0.0), jnp.where(tri, 0.0, x)
    return jnp.concatenate([up, low] if first_above else [low, up], axis=1).astype(BF)


def _scaled(q):
    return (q.astype(F32) * ATTN_SCALE).astype(BF)


def _cat_variants(prev, cur):
    return [[jnp.concatenate([prev[h][e], cur[h][e]], axis=0) for e in range(2)] for h in range(2)]


def _block_variants(prev_ref, cur_ref, nbs):
    var = [_kv_variants(prev_ref[...])] + [_kv_variants(cur_ref[b * BLOCK:(b + 1) * BLOCK, :]) for b in range(nbs)]
    return [_cat_variants(var[b], var[b + 1]) for b in range(nbs)]


def _head_cols(col, nbs):
    tiles = []
    for b in range(nbs):
        t = jnp.zeros((BLOCK, BLOCK), F32)
        for head in range(N_ATTN_HEADS):
            r0 = (b * N_ATTN_HEADS + head) * BLOCK
            t = _col_to_tile(t, col[r0:r0 + BLOCK, :], head)
        tiles.append(t)
    return tiles


def _attn_fwd(sinks, qa, ka, va, nbs, gathers):
    T = qa.shape[0]
    steps = T // (BLOCK * nbs)
    R = nbs * N_ATTN_HEADS * BLOCK
    ng = len(gathers)

    def body(sink_ref, q_ref, kc_ref, kp_ref, vc_ref, vp_ref, *rest):
        ex = _exchange_of(rest[:ng], rest[ng + 3:2 * ng + 3], rest[2 * ng + 3:], ng)
        a_ref, lse_ref, lset_ref = rest[ng:ng + 3]
        n = pl.program_id(0)
        pl.when(n == 0)(ex.start)
        kcat = _block_variants(kp_ref, kc_ref, nbs)
        vcat = _block_variants(vp_ref, vc_ref, nbs)
        tri1 = _tri(BLOCK)
        tiles = []
        for b in range(nbs):
            for pair in range(N_ATTN_HEADS // 2):
                qp = _scaled(q_ref[b * BLOCK:(b + 1) * BLOCK, pair * 128:(pair + 1) * 128])
                for e in range(2):
                    s = _fold(_dot_nt(qp, kcat[b][pair // 2][e]), tri1, True)
                    if b == 0:
                        s = jnp.where(tri1 & (n == 0), MASK_VALUE, s)
                    tiles.append(s)
        s = jnp.concatenate(tiles, axis=0)
        sink = jnp.concatenate([jnp.full((BLOCK, 1), sink_ref[head], F32)
                                for _ in range(nbs) for head in range(N_ATTN_HEADS)], axis=0)
        m = jnp.maximum(jnp.max(s, axis=-1, keepdims=True), sink)
        p = jnp.exp(s - m)
        z = jnp.sum(p, axis=-1, keepdims=True) + jnp.exp(sink - m)
        p2 = _unfold(p * (1.0 / z), _tri(R), True)
        for b in range(nbs):
            for pair in range(N_ATTN_HEADS // 2):
                r0 = (b * N_ATTN_HEADS + 2 * pair) * BLOCK
                acc = (_dot(p2[r0:r0 + BLOCK, :], vcat[b][pair // 2][0])
                       + _dot(p2[r0 + BLOCK:r0 + 2 * BLOCK, :], vcat[b][pair // 2][1]))
                a_ref[b * BLOCK:(b + 1) * BLOCK, pair * 128:(pair + 1) * 128] = acc.astype(BF)
        for b, t in enumerate(_head_cols(m + jnp.log(z), nbs)):
            lse_ref[b * BLOCK:(b + 1) * BLOCK, :] = t
            lset_ref[:, b * BLOCK:(b + 1) * BLOCK] = t.T[:N_ATTN_HEADS, :]
        pl.when(n == steps - 1)(ex.wait)

    cur = lambda w: pl.BlockSpec((BLOCK * nbs, w), lambda n: (n, 0))
    prev = lambda w: pl.BlockSpec((BLOCK, w), lambda n: (jnp.maximum(n * nbs - 1, 0), 0))
    outs = _pcall(
        body, name="attn_fwd", grid=(steps,),
        in_specs=[_smem_full(), cur(ATTN_W), cur(KV_W), prev(KV_W), cur(KV_W), prev(KV_W)] + [ANY_SPEC] * ng,
        out_specs=[cur(ATTN_W), cur(BLOCK), pl.BlockSpec((N_ATTN_HEADS, BLOCK * nbs), lambda n: (0, n))]
        + [ANY_SPEC] * ng,
        out_shape=[jax.ShapeDtypeStruct((T, ATTN_W), BF), jax.ShapeDtypeStruct((T, BLOCK), F32),
                   jax.ShapeDtypeStruct((N_ATTN_HEADS, T), F32)] + _exchange_shapes(gathers, []),
        scratch_shapes=_Exchange.scratch(ng),
        compiler_params=_params(("arbitrary",)),
    )(sinks, qa, ka, ka, va, va, *gathers)
    return outs[:3], outs[3:]


def _ret_fwd(decay, qr, kr, vr, gr, d_intra, xi_b, zeta_b, ncs):
    T = qr.shape[0]
    nc = T // BLOCK
    H, C = N_RET_HEADS, RET_HEAD_DIM

    def body(decay_ref, q_ref, k_ref, v_ref, g_ref, d_ref, xi_ref, zeta_ref, o_ref, s_ref, r_ref, state):
        @pl.when(pl.program_id(0) == 0)
        def _():
            state[...] = jnp.zeros_like(state)

        pairs = [(b, h) for b in range(ncs) for h in range(H)]
        sl = lambda b, h: (slice(b * BLOCK, (b + 1) * BLOCK), slice(h * C, (h + 1) * C))
        tab = lambda ref: jnp.concatenate([ref[h] for _, h in pairs], axis=0)
        q = [q_ref[sl(b, h)] for b, h in pairs]
        k = [k_ref[sl(b, h)] for b, h in pairs]
        v = [v_ref[sl(b, h)] for b, h in pairs]
        inner = (jnp.concatenate([_dot_nt(q[i], k[i]) for i in range(len(pairs))], axis=0) * tab(d_ref)).astype(BF)
        kz = (jnp.concatenate(k, axis=0).astype(F32) * tab(zeta_ref)).astype(BF)
        o1 = [_dot(inner[i * BLOCK:(i + 1) * BLOCK, :], v[i]) for i in range(len(pairs))]
        kv = [_dot_tn(kz[i * BLOCK:(i + 1) * BLOCK, :], v[i]) for i in range(len(pairs))]
        st_b = [None] * len(pairs)
        for h in range(H):
            st = state[h]
            for b in range(ncs):
                i = b * H + h
                st_b[i] = st.astype(BF)
                s_ref[b, h] = st_b[i]
                st = decay_ref[h] * st + kv[i]
            state[h] = st
        o2 = jnp.concatenate([_dot(q[i], st_b[i]) for i in range(len(pairs))], axis=0)
        o = jnp.concatenate(o1, axis=0) + o2 * tab(xi_ref)
        mu = jnp.mean(o, axis=-1, keepdims=True)
        oc = o - mu
        rs = lax.rsqrt(jnp.mean(oc * oc, axis=-1, keepdims=True) + GN_EPS)
        g = jnp.concatenate([g_ref[sl(b, h)] for b, h in pairs], axis=0)
        r = (g * jax.nn.sigmoid(g) * (oc * rs)).astype(BF)
        for i, (b, h) in enumerate(pairs):
            o_ref[sl(b, h)] = o[i * BLOCK:(i + 1) * BLOCK, :]
            r_ref[sl(b, h)] = r[i * BLOCK:(i + 1) * BLOCK, :]

    cur = pl.BlockSpec((BLOCK * ncs, RET_W), lambda n: (n, 0))
    tab = pl.BlockSpec((H, C, C), lambda n: (0, 0, 0))
    return _pcall(
        body, name="ret_fwd", grid=(nc // ncs,),
        in_specs=[_smem_full(), cur, cur, cur, cur, tab, tab, tab],
        out_specs=[cur, pl.BlockSpec((ncs, H, C, C), lambda n: (n, 0, 0, 0)), cur],
        out_shape=[jax.ShapeDtypeStruct((T, RET_W), F32), jax.ShapeDtypeStruct((nc, H, C, C), BF),
                   jax.ShapeDtypeStruct((T, RET_W), BF)],
        scratch_shapes=[pltpu.VMEM((H, C, C), F32)],
        compiler_params=_params(("arbitrary",)),
    )(decay, qr, kr, vr, gr, d_intra, xi_b, zeta_b)


def _out_proj(a, r, w_out, x, g2, g3, tm):
    T = x.shape[0]

    def body(a_ref, r_ref, w_ref, x_ref, g2_ref, g3_ref, mixed_ref, x1_ref, h2_ref):
        mixed = _dot(a_ref[...], w_ref[:ATTN_W, :]) + _dot(r_ref[...], w_ref[ATTN_W:, :])
        mixed_ref[...] = mixed
        _, n2 = _rms_stats(mixed)
        x1 = x_ref[...] + n2 * g2_ref[...]
        x1_ref[...] = x1
        _, n3 = _rms_stats(x1)
        h2_ref[...] = (n3 * g3_ref[...]).astype(BF)

    return _pcall(
        body, name="out_proj", grid=(T // tm,),
        in_specs=[_rows(tm, ATTN_W), _rows(tm, RET_W), _vmem_full(), _rows(tm, D_MODEL), _const((1, D_MODEL)),
                  _const((1, D_MODEL))],
        out_specs=[_rows(tm, D_MODEL)] * 3,
        out_shape=[jax.ShapeDtypeStruct((T, D_MODEL), F32), jax.ShapeDtypeStruct((T, D_MODEL), F32),
                   jax.ShapeDtypeStruct((T, D_MODEL), BF)],
        compiler_params=_params(("parallel",)),
    )(a, r, w_out, x, g2, g3)


def _shift_down(cur, k, before):
    out = pltpu.roll(cur, k, 0)
    row = lax.broadcasted_iota(jnp.int32, before.shape, 0)
    top = jnp.where(row < k, pltpu.roll(before, k, 0), out[0:8])
    return jnp.concatenate([top, out[8:]], axis=0)


def _shift_up(cur, k, after):
    tm = cur.shape[0]
    out = pltpu.roll(cur, tm - k, 0)
    row = lax.broadcasted_iota(jnp.int32, after.shape, 0)
    bot = jnp.where(row >= 8 - k, pltpu.roll(after, 8 - k, 0), out[tm - 8:])
    return jnp.concatenate([out[:tm - 8], bot], axis=0)


def _gelu_parts(x):
    m = (-2.0 * GELU_C * GELU_A) * (x * x)
    s = 1.0 / (1.0 + jnp.exp(x * (m - 2.0 * GELU_C)))
    gelu = x * s
    dgelu = s + gelu * (1.0 - s) * (2.0 * GELU_C - 3.0 * m)
    return gelu, dgelu


def _ffn_fwd(h2, w_up8, conv_w8, conv_b8, w_down4, x1, g4, target, tm):
    T = h2.shape[0]
    nt = T // tm

    def body(h_ref, wu_ref, cwg_ref, cwv_ref, cbg_ref, cbv_ref, wd_ref, x1_ref, g_ref, t_ref,
             upg_ref, upv_ref, ug_ref, uv_ref, y_ref, dout_ref, dz_ref, dg4_ref, loss_ref, halo, z_acc):
        s = pl.program_id(1)
        first = pl.program_id(0) == 0

        @pl.when(first & (s == 0))
        def _():
            loss_ref[...] = jnp.zeros_like(loss_ref)
            dg4_ref[...] = jnp.zeros_like(dg4_ref)

        h = h_ref[...]
        u = []
        parts = ((cwg_ref, cbg_ref, upg_ref, ug_ref), (cwv_ref, cbv_ref, upv_ref, uv_ref))
        for part, (cw_ref, cb_ref, up_ref, u_ref) in enumerate(parts):
            cur = _dot(h, wu_ref[s + part * FF_PAIRS])
            up_ref[0] = cur.astype(BF)
            before = jnp.where(first, 0.0, halo[part, s])
            halo[part, s] = cur[tm - 8:tm, :]
            u_c = (cw_ref[0, pl.ds(0, 1), :] * _shift_down(cur, 2, before)
                   + cw_ref[0, pl.ds(1, 1), :] * _shift_down(cur, 1, before)
                   + cw_ref[0, pl.ds(2, 1), :] * cur + cb_ref[0])
            u_ref[0] = u_c
            u.append(u_c)
        gelu, _ = _gelu_parts(u[0])
        y = (gelu * u[1]).astype(BF)
        y_ref[0] = y
        z_part = _dot(y, wd_ref[s])

        @pl.when(s == 0)
        def _():
            z_acc[...] = z_part

        @pl.when(s > 0)
        def _():
            z_acc[...] += z_part

        @pl.when(s == FF_PAIRS - 1)
        def _():
            r4, n4 = _rms_stats(z_acc[...])
            err = x1_ref[...] + n4 * g_ref[...] - t_ref[...]
            dout = err * (1.0 / D_MODEL)
            dout_ref[...] = dout
            loss_ref[...] += 0.5 * jnp.sum(jnp.mean(err * err, axis=-1, keepdims=True), axis=0, keepdims=True)
            dg4_ref[...] += jnp.sum(dout * n4, axis=0, keepdims=True)
            dz_ref[...] = _rms_bwd(n4, r4, dout * g_ref[...]).astype(BF)

    rows = pl.BlockSpec((tm, D_MODEL), lambda i, s: (i, 0))
    one = lambda shape: pl.BlockSpec(shape, lambda i, s: tuple(0 for _ in shape))
    gate = lambda r, w: pl.BlockSpec((1, r, w), lambda i, s: (s, 0, 0))
    val = lambda r, w: pl.BlockSpec((1, r, w), lambda i, s: (s + FF_PAIRS, 0, 0))
    tile = pl.BlockSpec((1, tm, FF_SHARD), lambda i, s: (s, i, 0))
    half = lambda dt: jax.ShapeDtypeStruct((FF_PAIRS, T, FF_SHARD), dt)
    return _pcall(
        body, name="ffn_fwd", grid=(nt, FF_PAIRS),
        in_specs=[rows, _vmem_full(), gate(3, FF_SHARD), val(3, FF_SHARD), gate(1, FF_SHARD), val(1, FF_SHARD),
                  _vmem_full(), rows, one((1, D_MODEL)), rows],
        out_specs=[tile] * 5 + [rows, rows, one((1, D_MODEL)), one((8, 128))],
        out_shape=[half(BF), half(BF), half(F32), half(F32), half(BF), jax.ShapeDtypeStruct((T, D_MODEL), F32),
                   jax.ShapeDtypeStruct((T, D_MODEL), BF), jax.ShapeDtypeStruct((1, D_MODEL), F32),
                   jax.ShapeDtypeStruct((8, 128), F32)],
        scratch_shapes=[pltpu.VMEM((2, FF_PAIRS, 8, FF_SHARD), F32), pltpu.VMEM((tm, D_MODEL), F32)],
        compiler_params=_params(("arbitrary", "arbitrary")),
    )(h2, w_up8, conv_w8, conv_w8, conv_b8, conv_b8, w_down4, x1, g4, target)


def _ffn_bwd_a(dz, h2, w_down4, u_g, u_v, up_g, up_v, y4, conv_w8, tm):
    T = dz.shape[0]
    nt = T // tm

    def body(dz_ref, h_ref, wd_ref, ug_ref, uv_ref, upg_ref, upv_ref, y_ref, cwg_ref, cwv_ref,
             dupg_ref, dupv_ref, dcbg_ref, dcbv_ref, dcwg_ref, dcwv_ref, gwug_out, gwuv_out, gwd_out,
             carry, gwug_ref, gwuv_ref, gwd_ref):
        @pl.when(pl.program_id(1) == 0)
        def _():
            for ref in (dcbg_ref, dcbv_ref, dcwg_ref, dcwv_ref, gwug_ref, gwuv_ref, gwd_ref, carry):
                ref[...] = jnp.zeros_like(ref)

        dz = dz_ref[...]
        h = h_ref[...]
        dy = _dot_nt(dz, wd_ref[0])
        gwd_ref[0] += _dot_tn(y_ref[0], dz)
        gelu, dgelu = _gelu_parts(ug_ref[0])
        parts = ((0, dy * uv_ref[0] * dgelu, upg_ref, cwg_ref, dupg_ref, dcbg_ref, dcwg_ref, gwug_ref),
                 (1, dy * gelu, upv_ref, cwv_ref, dupv_ref, dcbv_ref, dcwv_ref, gwuv_ref))
        for part, d, up_ref, cw_ref, dup_ref, dcb_ref, dcw_ref, gwu_ref in parts:
            after = carry[part]
            d1 = _shift_up(d, 1, after)
            d2 = _shift_up(d, 2, after)
            carry[part] = d[0:8, :]
            upc = up_ref[0].astype(F32)
            dcb_ref[0] += jnp.sum(d, axis=0, keepdims=True)
            dcw_ref[0, pl.ds(2, 1), :] += jnp.sum(d * upc, axis=0, keepdims=True)
            dcw_ref[0, pl.ds(1, 1), :] += jnp.sum(d1 * upc, axis=0, keepdims=True)
            dcw_ref[0, pl.ds(0, 1), :] += jnp.sum(d2 * upc, axis=0, keepdims=True)
            dup = (cw_ref[0, pl.ds(2, 1), :] * d + cw_ref[0, pl.ds(1, 1), :] * d1
                   + cw_ref[0, pl.ds(0, 1), :] * d2).astype(BF)
            dup_ref[0] = dup
            gwu_ref[0] += _dot_tn(h, dup)

        @pl.when(pl.program_id(1) == nt - 1)
        def _():
            s = pl.program_id(0)
            pltpu.sync_copy(gwd_ref, gwd_out.at[pl.ds(s, 1)])
            for acc_ref, out in ((gwug_ref, gwug_out), (gwuv_ref, gwuv_out)):
                gwd_ref[0] = acc_ref[0].T
                pltpu.sync_copy(gwd_ref, out.at[pl.ds(s, 1)])

    rev = pl.BlockSpec((tm, D_MODEL), lambda s, i: (nt - 1 - i, 0))
    tile = pl.BlockSpec((1, tm, FF_SHARD), lambda s, i: (s, nt - 1 - i, 0))
    acc = lambda r, w: pl.BlockSpec((1, r, w), lambda s, i: (s, 0, 0))
    acc_val = pl.BlockSpec((1, 3, FF_SHARD), lambda s, i: (s + FF_PAIRS, 0, 0))
    half = lambda r, dt: jax.ShapeDtypeStruct((FF_PAIRS, r, FF_SHARD), dt)
    return _pcall(
        body, name="ffn_bwd_a", grid=(FF_PAIRS, nt),
        in_specs=[rev, rev, acc(FF_SHARD, D_MODEL), tile, tile, tile, tile, tile, acc(3, FF_SHARD), acc_val],
        out_specs=[tile, tile, acc(1, FF_SHARD), acc(1, FF_SHARD), acc(3, FF_SHARD), acc(3, FF_SHARD),
                   ANY_SPEC, ANY_SPEC, ANY_SPEC],
        out_shape=[half(T, BF), half(T, BF), half(1, F32), half(1, F32), half(3, F32), half(3, F32),
                   jax.ShapeDtypeStruct((FF_PAIRS, FF_SHARD, D_MODEL), F32),
                   jax.ShapeDtypeStruct((FF_PAIRS, FF_SHARD, D_MODEL), F32),
                   jax.ShapeDtypeStruct((FF_PAIRS, FF_SHARD, D_MODEL), F32)],
        scratch_shapes=[pltpu.VMEM((2, 8, FF_SHARD), F32), pltpu.VMEM((1, D_MODEL, FF_SHARD), F32),
                        pltpu.VMEM((1, D_MODEL, FF_SHARD), F32), pltpu.VMEM((1, FF_SHARD, D_MODEL), F32)],
        compiler_params=_params(("arbitrary", "arbitrary")),
    )(dz, h2, w_down4, u_g, u_v, up_g, up_v, y4, conv_w8, conv_w8)


def _ffn_bwd_b(dup_g, dup_v, w_up8, x1, dout, g3, mixed, g2, w_out, tm, gwu_g, gwu_v):
    T = x1.shape[0]
    nt = T // tm

    def body(dupg_ref, dupv_ref, wup_ref, x1_ref, dout_ref, g3_ref, mixed_ref, g2_ref, wout_ref, gwug_ref, gwuv_ref,
             dx1_ref, dmixed_ref, da_ref, dr_ref, dg3_ref, dg2_ref, pup_ref, *sems):
        ex = _Exchange([], [((gwug_ref, gwuv_ref), pup_ref)], *sems)

        @pl.when(pl.program_id(0) == 0)
        def _():
            ex.start()
            dg3_ref[...] = jnp.zeros_like(dg3_ref)
            dg2_ref[...] = jnp.zeros_like(dg2_ref)

        dh2 = jnp.zeros((tm, D_MODEL), F32)
        for s in range(FF_PAIRS):
            dh2 = dh2 + _dot_nt(dupg_ref[s], wup_ref[s]) + _dot_nt(dupv_ref[s], wup_ref[s + FF_PAIRS])
        r3, n3 = _rms_stats(x1_ref[...])
        dg3_ref[...] += jnp.sum(dh2 * n3, axis=0, keepdims=True)
        dx1 = dout_ref[...] + _rms_bwd(n3, r3, dh2 * g3_ref[...])
        dx1_ref[...] = dx1
        r2, n2 = _rms_stats(mixed_ref[...])
        dg2_ref[...] += jnp.sum(dx1 * n2, axis=0, keepdims=True)
        dmixed = _rms_bwd(n2, r2, dx1 * g2_ref[...]).astype(BF)
        dmixed_ref[...] = dmixed
        da_ref[...] = _dot_nt(dmixed, wout_ref[:ATTN_W, :])
        dr_ref[...] = _dot_nt(dmixed, wout_ref[ATTN_W:, :])
        pl.when(pl.program_id(0) == nt - 1)(ex.wait)

    half = pl.BlockSpec((FF_PAIRS, tm, FF_SHARD), lambda i: (0, i, 0))
    outs = _pcall(
        body, name="ffn_bwd_b", grid=(nt,),
        in_specs=[half, half, _vmem_full(), _rows(tm, D_MODEL), _rows(tm, D_MODEL), _const((1, D_MODEL)),
                  _rows(tm, D_MODEL), _const((1, D_MODEL)), _vmem_full(), ANY_SPEC, ANY_SPEC],
        out_specs=[_rows(tm, D_MODEL), _rows(tm, D_MODEL), _rows(tm, ATTN_W), _rows(tm, RET_W),
                   _const((1, D_MODEL)), _const((1, D_MODEL)), ANY_SPEC],
        out_shape=[jax.ShapeDtypeStruct((T, D_MODEL), F32), jax.ShapeDtypeStruct((T, D_MODEL), BF),
                   jax.ShapeDtypeStruct((T, ATTN_W), F32), jax.ShapeDtypeStruct((T, RET_W), F32),
                   jax.ShapeDtypeStruct((1, D_MODEL), F32), jax.ShapeDtypeStruct((1, D_MODEL), F32),
                   jax.ShapeDtypeStruct((N_DEV, FF_SHARD, D_MODEL), F32)],
        scratch_shapes=_Exchange.scratch(1),
        compiler_params=_params(("arbitrary",)),
    )(dup_g, dup_v, w_up8, x1, dout, g3, mixed, g2, w_out, gwu_g, gwu_v)
    return outs[:6], outs[6]


def _ret_bwd(decay, qr, kr, vr, gr, o, states, dr, d_intra, d_intra_t, xi_b, zeta_b, cos, sin_s, swaps, ncs):
    T = qr.shape[0]
    nc = T // BLOCK
    H, C = N_RET_HEADS, RET_HEAD_DIM
    ns = len(swaps)

    def body(decay_ref, q_ref, k_ref, v_ref, g_ref, o_ref, s_ref, dr_ref, d_ref, dt_ref, xi_ref, zeta_ref,
             cos_ref, sin_ref, *rest):
        ex = _exchange_of(rest[:ns], rest[ns + 1:2 * ns + 1], rest[2 * ns + 2:], 0)
        dret_ref, gstate = rest[ns], rest[2 * ns + 1]

        @pl.when(pl.program_id(0) == 0)
        def _():
            ex.start()
            gstate[...] = jnp.zeros_like(gstate)

        pairs = [(b, h) for b in range(ncs) for h in range(H)]
        n = len(pairs)
        sl = lambda b, h: (slice(b * BLOCK, (b + 1) * BLOCK), slice(h * C, (h + 1) * C))
        cat = lambda ref: jnp.concatenate([ref[sl(b, h)] for b, h in pairs], axis=0)
        tab = lambda ref: jnp.concatenate([ref[h] for _, h in pairs], axis=0)
        part = lambda x, i: x[i * BLOCK:(i + 1) * BLOCK, :]
        q = [q_ref[sl(b, h)] for b, h in pairs]
        k = [k_ref[sl(b, h)] for b, h in pairs]
        v = [v_ref[sl(b, h)] for b, h in pairs]
        g, o_all, dr_all = cat(g_ref), cat(o_ref), cat(dr_ref)
        mu = jnp.mean(o_all, axis=-1, keepdims=True)
        oc = o_all - mu
        rs = lax.rsqrt(jnp.mean(oc * oc, axis=-1, keepdims=True) + GN_EPS)
        on = oc * rs
        sg = jax.nn.sigmoid(g)
        dg = (dr_all * on * (sg * (1.0 + g * (1.0 - sg)))).astype(BF)
        don = dr_all * (g * sg)
        do = rs * (don - jnp.mean(don, axis=-1, keepdims=True) - on * jnp.mean(don * on, axis=-1, keepdims=True))
        do_b = do.astype(BF)
        dox_b = (do * tab(xi_ref)).astype(BF)
        zeta = tab(zeta_ref)
        kz = (jnp.concatenate(k, axis=0).astype(F32) * zeta).astype(BF)
        d_t = tab(dt_ref)
        da_b = (jnp.concatenate([_dot_nt(part(do_b, i), v[i]) for i in range(n)], axis=0) * tab(d_ref)).astype(BF)
        dat_b = (jnp.concatenate([_dot_nt(v[i], part(do_b, i)) for i in range(n)], axis=0) * d_t).astype(BF)
        mt_b = (jnp.concatenate([_dot_nt(k[i], q[i]) for i in range(n)], axis=0) * d_t).astype(BF)
        dq = [_dot(part(da_b, i), k[i]) + _dot_nt(part(dox_b, i), s_ref[pairs[i]]) for i in range(n)]
        dk1 = [_dot(part(dat_b, i), q[i]) for i in range(n)]
        dv1 = [_dot(part(mt_b, i), part(do_b, i)) for i in range(n)]
        qtd = [_dot_tn(q[i], part(dox_b, i)) for i in range(n)]
        gst_b = [None] * n
        for h in range(H):
            gst = gstate[h]
            for b in reversed(range(ncs)):
                i = b * H + h
                gst_b[i] = gst.astype(BF)
                gst = decay_ref[h] * gst + qtd[i]
            gstate[h] = gst
        dk2 = jnp.concatenate([_dot_nt(v[i], gst_b[i]) for i in range(n)], axis=0) * zeta
        dv = jnp.concatenate([dv1[i] + _dot(part(kz, i), gst_b[i]) for i in range(n)], axis=0).astype(BF)
        even = lax.broadcasted_iota(jnp.int32, (n * BLOCK, C), 1) % 2 == 0
        cos_t = jnp.concatenate([cos_ref[b * BLOCK:(b + 1) * BLOCK, :] for b, _ in pairs], axis=0)
        sin_t = jnp.concatenate([sin_ref[b * BLOCK:(b + 1) * BLOCK, :] for b, _ in pairs], axis=0)
        dq = jnp.concatenate(dq, axis=0)
        dk = jnp.concatenate(dk1, axis=0) + dk2
        dq = (dq * cos_t - _rot(dq, even) * sin_t).astype(BF)
        dk = ((dk * cos_t - _rot(dk, even) * sin_t) * RET_K_SCALE).astype(BF)
        for i, (b, h) in enumerate(pairs):
            rows = slice(b * BLOCK, (b + 1) * BLOCK)
            for j, x in enumerate((dq, dk, dv, dg)):
                dret_ref[rows, j * RET_W + h * C:j * RET_W + (h + 1) * C] = part(x, i)
        pl.when(pl.program_id(0) == steps - 1)(ex.wait)

    steps = nc // ncs
    rev = lambda w: pl.BlockSpec((BLOCK * ncs, w), lambda n: (steps - 1 - n, 0))
    tab = pl.BlockSpec((H, C, C), lambda n: (0, 0, 0))
    outs = _pcall(
        body, name="ret_bwd", grid=(steps,),
        in_specs=[_smem_full(), rev(RET_W), rev(RET_W), rev(RET_W), rev(RET_W), rev(RET_W),
                  pl.BlockSpec((ncs, H, C, C), lambda n: (steps - 1 - n, 0, 0, 0)), rev(RET_W), tab, tab, tab, tab,
                  rev(C), rev(C)] + [ANY_SPEC] * ns,
        out_specs=[rev(4 * RET_W)] + [ANY_SPEC] * ns,
        out_shape=[jax.ShapeDtypeStruct((T, 4 * RET_W), BF)] + _exchange_shapes([], swaps),
        scratch_shapes=[pltpu.VMEM((H, C, C), F32)] + _Exchange.scratch(ns),
        compiler_params=_params(("arbitrary",)),
    )(decay, qr, kr, vr, gr, o, states, dr, d_intra, d_intra_t, xi_b, zeta_b, cos, sin_s, *swaps)
    return outs[0], outs[1:]


def _attn_bwd_dq(sinks, qa, ka, va, da, lse, nbs, swaps):
    T = qa.shape[0]
    steps = T // (BLOCK * nbs)
    R = nbs * N_ATTN_HEADS * BLOCK
    ns = len(swaps)

    def body(sink_ref, q_ref, kc_ref, kp_ref, vc_ref, vp_ref, da_ref, lse_ref, *rest):
        ex = _exchange_of(rest[:ns], rest[ns + 3:2 * ns + 3], rest[2 * ns + 3:], 0)
        dq_ref, deltat_ref, dsink_ref = rest[ns:ns + 3]
        n = pl.program_id(0)

        @pl.when(n == 0)
        def _():
            ex.start()
            dsink_ref[...] = jnp.zeros_like(dsink_ref)

        kcat = _block_variants(kp_ref, kc_ref, nbs)
        vcat = _block_variants(vp_ref, vc_ref, nbs)
        tri1 = _tri(BLOCK)
        lane = lax.broadcasted_iota(jnp.int32, (BLOCK, BLOCK), 1)
        s_tiles, dp_tiles, lse_cols = [], [], []
        for b in range(nbs):
            rows = slice(b * BLOCK, (b + 1) * BLOCK)
            lse_tile = lse_ref[rows, :]
            for pair in range(N_ATTN_HEADS // 2):
                qp = _scaled(q_ref[rows, pair * 128:(pair + 1) * 128])
                dop = da_ref[rows, pair * 128:(pair + 1) * 128].astype(BF)
                for e in range(2):
                    s = _fold(_dot_nt(qp, kcat[b][pair // 2][e]), tri1, True)
                    if b == 0:
                        s = jnp.where(tri1 & (n == 0), MASK_VALUE, s)
                    s_tiles.append(s)
                    dp_tiles.append(_fold(_dot_nt(dop, vcat[b][pair // 2][e]), tri1, True))
                    lse_cols.append(jnp.sum(jnp.where(lane == 2 * pair + e, lse_tile, 0.0), axis=-1, keepdims=True))
        lse_c = jnp.concatenate(lse_cols, axis=0)
        p = jnp.exp(jnp.concatenate(s_tiles, axis=0) - lse_c)
        dp = jnp.concatenate(dp_tiles, axis=0)
        delta = jnp.sum(p * dp, axis=-1, keepdims=True)
        ds2 = _unfold(p * (dp - delta), _tri(R), True)
        for b in range(nbs):
            for pair in range(N_ATTN_HEADS // 2):
                r0 = (b * N_ATTN_HEADS + 2 * pair) * BLOCK
                acc = (_dot(ds2[r0:r0 + BLOCK, :], kcat[b][pair // 2][0])
                       + _dot(ds2[r0 + BLOCK:r0 + 2 * BLOCK, :], kcat[b][pair // 2][1]))
                dq_ref[b * BLOCK:(b + 1) * BLOCK, pair * 128:(pair + 1) * 128] = (acc * ATTN_SCALE).astype(BF)
        for b, t in enumerate(_head_cols(delta, nbs)):
            deltat_ref[:, b * BLOCK:(b + 1) * BLOCK] = t.T[:N_ATTN_HEADS, :]
        sink = jnp.concatenate([jnp.full((BLOCK, 1), sink_ref[head], F32)
                                for _ in range(nbs) for head in range(N_ATTN_HEADS)], axis=0)
        ds_sink = -jnp.exp(sink - lse_c) * delta
        row8 = lax.broadcasted_iota(jnp.int32, (N_ATTN_HEADS, BLOCK), 0)
        dsink = jnp.zeros((N_ATTN_HEADS, BLOCK), F32)
        for b in range(nbs):
            for head in range(N_ATTN_HEADS):
                r0 = (b * N_ATTN_HEADS + head) * BLOCK
                dsink = dsink + jnp.where(row8 == head, jnp.sum(ds_sink[r0:r0 + BLOCK, :], axis=0, keepdims=True), 0.0)
        dsink_ref[...] += dsink
        pl.when(n == steps - 1)(ex.wait)

    cur = lambda w: pl.BlockSpec((BLOCK * nbs, w), lambda n: (n, 0))
    prev = lambda w: pl.BlockSpec((BLOCK, w), lambda n: (jnp.maximum(n * nbs - 1, 0), 0))
    outs = _pcall(
        body, name="attn_bwd_dq", grid=(steps,),
        in_specs=[_smem_full(), cur(ATTN_W), cur(KV_W), prev(KV_W), cur(KV_W), prev(KV_W), cur(ATTN_W), cur(BLOCK)]
        + [ANY_SPEC] * ns,
        out_specs=[cur(ATTN_W), pl.BlockSpec((N_ATTN_HEADS, BLOCK * nbs), lambda n: (0, n)),
                   _const((N_ATTN_HEADS, BLOCK))] + [ANY_SPEC] * ns,
        out_shape=[jax.ShapeDtypeStruct((T, ATTN_W), BF), jax.ShapeDtypeStruct((N_ATTN_HEADS, T), F32),
                   jax.ShapeDtypeStruct((N_ATTN_HEADS, BLOCK), F32)] + _exchange_shapes([], swaps),
        scratch_shapes=_Exchange.scratch(ns),
        compiler_params=_params(("arbitrary",)),
    )(sinks, qa, ka, ka, va, va, da, lse, *swaps)
    return outs[:3], outs[3:]


def _attn_bwd_dkv(qa, ka, va, da, lse_t, delta_t, nbs):
    T = qa.shape[0]
    nb = T // BLOCK
    steps = nb // nbs
    R = nbs * N_ATTN_HEADS * BLOCK

    def body(qc_ref, qn_ref, dac_ref, dan_ref, k_ref, v_ref, lc_ref, ln_ref, dc_ref, dn_ref, dk_ref, dv_ref):
        n = pl.program_id(0)
        tri1 = _tri(BLOCK, True)
        lo = lax.broadcasted_iota(jnp.int32, (BLOCK, 128), 1) < HEAD_DIM
        kv = [_kv_variants(k_ref[b * BLOCK:(b + 1) * BLOCK, :]) for b in range(nbs)]
        vv = [_kv_variants(v_ref[b * BLOCK:(b + 1) * BLOCK, :]) for b in range(nbs)]
        qcat, docat = [], []
        s_tiles, dp_tiles, lse_tiles, delta_tiles = [], [], [], []
        for b in range(nbs):
            rows = slice(b * BLOCK, (b + 1) * BLOCK)
            nrows = slice((b + 1) * BLOCK, (b + 2) * BLOCK)
            inside = b < nbs - 1
            for pair in range(N_ATTN_HEADS // 2):
                ps = slice(pair * 128, (pair + 1) * 128)
                q2 = _scaled(jnp.concatenate([qc_ref[rows, ps], qc_ref[nrows, ps] if inside else qn_ref[:, ps]], axis=0))
                do2 = jnp.concatenate([dac_ref[rows, ps], dac_ref[nrows, ps] if inside else dan_ref[:, ps]],
                                      axis=0).astype(BF)
                qcat.append(q2)
                docat.append(do2)
                for e in range(2):
                    one = pl.ds(2 * pair + e, 1)
                    s = _fold(_dot_nt(kv[b][pair // 2][e], q2), tri1, False)
                    if not inside:
                        s = jnp.where(tri1 & (n == steps - 1), MASK_VALUE, s)
                    s_tiles.append(s)
                    dp_tiles.append(_fold(_dot_nt(vv[b][pair // 2][e], do2), tri1, False))
                    lse_tiles.append(jnp.where(tri1, lc_ref[one, nrows] if inside else ln_ref[one, :], lc_ref[one, rows]))
                    delta_tiles.append(jnp.where(tri1, dc_ref[one, nrows] if inside else dn_ref[one, :],
                                                 dc_ref[one, rows]))
        pt = jnp.exp(jnp.concatenate(s_tiles, axis=0) - jnp.concatenate(lse_tiles, axis=0))
        dst = pt * (jnp.concatenate(dp_tiles, axis=0) - jnp.concatenate(delta_tiles, axis=0))
        tri = _tri(R, True)
        pt2 = _unfold(pt, tri, False)
        dst2 = _unfold(dst, tri, False)
        for b in range(nbs):
            dk = jnp.zeros((BLOCK, 128), F32)
            dv = jnp.zeros((BLOCK, 128), F32)
            for pair in range(N_ATTN_HEADS // 2):
                h = pair // 2
                for e in range(2):
                    r0 = (b * N_ATTN_HEADS + 2 * pair + e) * BLOCK
                    half = lo if e == 0 else jnp.logical_not(lo)
                    dv_e = jnp.where(half, _dot(pt2[r0:r0 + BLOCK, :], docat[b * 4 + pair]), 0.0)
                    dk_e = jnp.where(half, _dot(dst2[r0:r0 + BLOCK, :], qcat[b * 4 + pair]), 0.0)
                    if e != h:
                        dv_e = pltpu.roll(dv_e, HEAD_DIM, 1)
                        dk_e = pltpu.roll(dk_e, HEAD_DIM, 1)
                    dv = dv + dv_e
                    dk = dk + dk_e
            dk_ref[b * BLOCK:(b + 1) * BLOCK, :] = dk.astype(BF)
            dv_ref[b * BLOCK:(b + 1) * BLOCK, :] = dv.astype(BF)

    cur = lambda w: pl.BlockSpec((BLOCK * nbs, w), lambda n: (n, 0))
    nxt = lambda w: pl.BlockSpec((BLOCK, w), lambda n: (jnp.minimum((n + 1) * nbs, nb - 1), 0))
    tcur = pl.BlockSpec((N_ATTN_HEADS, BLOCK * nbs), lambda n: (0, n))
    tnxt = pl.BlockSpec((N_ATTN_HEADS, BLOCK), lambda n: (0, jnp.minimum((n + 1) * nbs, nb - 1)))
    return _pcall(
        body, name="attn_bwd_dkv", grid=(steps,),
        in_specs=[cur(ATTN_W), nxt(ATTN_W), cur(ATTN_W), nxt(ATTN_W), cur(KV_W), cur(KV_W), tcur, tnxt, tcur, tnxt],
        out_specs=[cur(KV_W), cur(KV_W)],
        out_shape=[jax.ShapeDtypeStruct((T, KV_W), BF), jax.ShapeDtypeStruct((T, KV_W), BF)],
        compiler_params=_params(("parallel",)),
    )(qa, qa, da, da, ka, va, lse_t, lse_t, delta_t, delta_t)


def _in_proj_bwd(dqa, dka, dva, dret, w_in, x, g1, dx1, tm):
    T = x.shape[0]

    def body(dqa_ref, dka_ref, dva_ref, dret_ref, w_ref, x_ref, g_ref, dx1_ref, dx_ref, dg1_ref):
        @pl.when(pl.program_id(0) == 0)
        def _():
            dg1_ref[...] = jnp.zeros_like(dg1_ref)

        dh = (_dot(dqa_ref[...], w_ref[QA0:QA0 + ATTN_W, :]) + _dot(dka_ref[...], w_ref[KA0:KA0 + KV_W, :])
              + _dot(dva_ref[...], w_ref[VA0:VA0 + KV_W, :]) + _dot(dret_ref[...], w_ref[QR0:IN_W, :]))
        r, n = _rms_stats(x_ref[...])
        dg1_ref[...] += jnp.sum(dh * n, axis=0, keepdims=True)
        dx_ref[...] = dx1_ref[...] + _rms_bwd(n, r, dh * g_ref[...])

    return _pcall(
        body, name="in_proj_bwd", grid=(T // tm,),
        in_specs=[_rows(tm, ATTN_W), _rows(tm, KV_W), _rows(tm, KV_W), _rows(tm, 4 * RET_W), _vmem_full(),
                  _rows(tm, D_MODEL), _const((1, D_MODEL)), _rows(tm, D_MODEL)],
        out_specs=[_rows(tm, D_MODEL), _const((1, D_MODEL))],
        out_shape=[jax.ShapeDtypeStruct((T, D_MODEL), F32), jax.ShapeDtypeStruct((1, D_MODEL), F32)],
        compiler_params=_params(("arbitrary",)),
    )(dqa, dka, dva, dret, w_in, x, g1, dx1)


def _wgrad(a_list, b_list, tk, name):
    T = a_list[0].shape[0]
    na, nbb = len(a_list), len(b_list)
    m_sizes = [a.shape[1] for a in a_list]
    n_sizes = [b.shape[1] for b in b_list]
    M, N = sum(m_sizes), sum(n_sizes)
    nk = T // tk
    chunk = 512

    def body(*refs):
        a_refs, b_refs = refs[:na], refs[na:na + nbb]
        out_ref, acc = refs[na + nbb], refs[na + nbb + 1]
        k = pl.program_id(0)

        @pl.when(k == 0)
        def _():
            acc[...] = jnp.zeros_like(acc)

        r0 = 0
        for ai in range(na):
            a = a_refs[ai][...]
            c0 = 0
            for bi in range(nbb):
                for s in range(0, n_sizes[bi], chunk):
                    w = min(chunk, n_sizes[bi] - s)
                    acc[r0:r0 + m_sizes[ai], c0 + s:c0 + s + w] += _dot_tn(a, b_refs[bi][:, s:s + w])
                c0 += n_sizes[bi]
            r0 += m_sizes[ai]

        @pl.when(k == nk - 1)
        def _():
            pltpu.sync_copy(acc, out_ref)

    return _pcall(
        body, name=name, grid=(nk,),
        in_specs=[_rows(tk, w) for w in m_sizes + n_sizes],
        out_specs=pl.BlockSpec(memory_space=pl.ANY),
        out_shape=jax.ShapeDtypeStruct((M, N), F32),
        scratch_shapes=[pltpu.VMEM((M, N), F32)],
        compiler_params=_params(("arbitrary",)),
    )(*a_list, *b_list)


def _adamw_math(w, g, m, v):
    m = ADAM_B1 * m + (1.0 - ADAM_B1) * g
    v = ADAM_B2 * v + (1.0 - ADAM_B2) * (g * g)
    m_hat = m / (1.0 - ADAM_B1 ** ADAM_STEP)
    v_hat = v / (1.0 - ADAM_B2 ** ADAM_STEP)
    delta = -ADAM_LR * (m_hat / (jnp.sqrt(v_hat) + ADAM_EPS) + ADAM_WD * w)
    return delta, m, v


def _sum_parts(parts_ref):
    g = parts_ref[0].astype(F32)
    for i in range(1, N_DEV):
        g = g + parts_ref[i].astype(F32)
    return g


def _adamw_shard(parts, w, m, v, tr, name):
    R, C = w.shape

    def body(p_ref, w_ref, m_ref, v_ref, g_ref, d_ref, nm_ref, nv_ref):
        g = _sum_parts(p_ref)
        g_ref[...] = g
        d_ref[...], nm_ref[...], nv_ref[...] = _adamw_math(w_ref[...], g, m_ref[...], v_ref[...])

    blk = pl.BlockSpec((tr, C), lambda i: (i, 0))
    return _pcall(
        body, name=name, grid=(R // tr,),
        in_specs=[pl.BlockSpec((N_DEV, tr, C), lambda i: (0, i, 0)), blk, blk, blk],
        out_specs=[blk] * 4,
        out_shape=[jax.ShapeDtypeStruct((R, C), F32)] * 4,
        compiler_params=_params(("parallel",)),
    )(parts, w, m, v)


def _sum_small(parts):
    def body(p_ref, g_ref):
        g_ref[...] = _sum_parts(p_ref)

    return _pcall(body, name="sum_small", out_shape=jax.ShapeDtypeStruct(parts.shape[1:], F32),
                  in_specs=[_vmem_full()], out_specs=_vmem_full())(parts)


def _adamw_small(g, w, m, v, name):
    def body(g_ref, w_ref, m_ref, v_ref, d_ref, nm_ref, nv_ref):
        d_ref[...], nm_ref[...], nv_ref[...] = _adamw_math(w_ref[...], g_ref[...], m_ref[...], v_ref[...])

    return _pcall(body, name=name, out_shape=[jax.ShapeDtypeStruct(w.shape, F32)] * 3,
                  in_specs=[_vmem_full()] * 4, out_specs=[_vmem_full()] * 3)(g, w, m, v)


def _tables(T):
    h, c = N_RET_HEADS, BLOCK
    pos = jnp.arange(T, dtype=F32)
    angle = 1.0 / jnp.power(10000.0, jnp.linspace(0.0, 1.0, RET_HEAD_DIM // 2, dtype=F32))
    phase = pos[:, None] * angle[None]
    sin = jnp.repeat(jnp.sin(phase), 2, axis=1)
    cos = jnp.repeat(jnp.cos(phase), 2, axis=1)
    even = (jnp.arange(RET_HEAD_DIM) % 2 == 0)[None, :]
    sin_s = jnp.where(even, -sin, sin)
    log_gamma = jnp.log(1.0 - jnp.power(2.0, -5.0 - jnp.arange(h, dtype=F32)))
    idx = jnp.arange(c, dtype=F32)
    rel = idx[:, None] - idx[None, :]
    d_intra = jnp.where(rel[None] >= 0, jnp.exp(log_gamma[:, None, None] * jnp.maximum(rel, 0.0)[None]), 0.0)
    xi = jnp.exp(log_gamma[None, :] * (idx[:, None] + 1.0))
    zeta = jnp.exp(log_gamma[None, :] * (c - 1.0 - idx[:, None]))
    decay = jnp.exp(log_gamma * c)
    xi_b = jnp.broadcast_to(xi.T[:, :, None], (h, c, RET_HEAD_DIM))
    zeta_b = jnp.broadcast_to(zeta.T[:, :, None], (h, c, RET_HEAD_DIM))
    return cos, sin_s, d_intra, jnp.swapaxes(d_intra, 1, 2), xi_b, zeta_b, decay


def _from_shards(sh):
    n, r, cols = sh.shape
    return jnp.swapaxes(sh, 0, 1).reshape(r, n * cols)


SMALL_ROWS = 216


def _pack_small(gains, conv_b, conv_w, sinks, scalar=None):
    last = jnp.concatenate([sinks.reshape(1, 8), jnp.zeros((1, 1), F32) if scalar is None else scalar.reshape(1, 1)],
                           axis=1)
    parts = [g.reshape(8, 128) for g in gains] + [conv_b.reshape(44, 128), conv_w.reshape(132, 128),
                                                  jnp.pad(last, ((0, 0), (0, 119)))]
    packed = jnp.concatenate(parts, axis=0)
    return jnp.pad(packed, ((0, SMALL_ROWS - packed.shape[0]), (0, 0)))


def kernel(x, mix_pre_norm, w_in, attn_sinks, w_out, mix_post_norm, ffn_pre_norm, w_up, conv_w, conv_b, w_down, ffn_post_norm, loss_target, m_mix_pre_norm, m_w_in, m_attn_sinks, m_w_out, m_mix_post_norm, m_ffn_pre_norm, m_w_up, m_conv_w, m_conv_b, m_w_down, m_ffn_post_norm, v_mix_pre_norm, v_w_in, v_attn_sinks, v_w_out, v_mix_post_norm, v_ffn_pre_norm, v_w_up, v_conv_w, v_conv_b, v_w_down, v_ffn_post_norm):
    T = x.shape[1]
    tm = min(512, T)
    tm_big = min(1024, T)
    tk_grad = min(2048, T)
    nbs = min(8, T // BLOCK)
    x2 = x.reshape(T, D_MODEL)
    target = loss_target.reshape(T, D_MODEL)
    me = 4 * lax.axis_index("x") + 2 * lax.axis_index("y") + lax.axis_index("c")

    g_in, g_cw = _exchange_call([jnp.swapaxes(w_in, 1, 2)[0].astype(BF), conv_w[0]], [], "gather_w_in")
    w_in_f = g_in.reshape(IN_W, D_MODEL)
    cos, sin_s, d_intra, d_intra_t, xi_b, zeta_b, decay = _tables(T)
    sinks = attn_sinks.reshape(N_ATTN_HEADS)

    (h1, qa, ka, va, qr, kr, vr, gr), (w_up8,) = _in_proj(
        x2, mix_pre_norm, w_in_f, cos, sin_s, tm_big, [w_up[0].astype(BF)])
    (a, lse, lse_t), (g_down, g_out) = _attn_fwd(sinks, qa, ka, va, nbs,
                                                 [w_down[0].astype(BF), w_out[0].astype(BF)])
    w_out_f = g_out.reshape(D_MODEL, D_MODEL)
    o, states, r = _ret_fwd(decay, qr, kr, vr, gr, d_intra, xi_b, zeta_b, nbs)
    mixed, x1, h2 = _out_proj(a, r, w_out_f, x2, mix_post_norm, ffn_pre_norm, tm_big)
    w_down4 = g_down.reshape(FF_PAIRS, FF_SHARD, D_MODEL)
    up_g, up_v, u_g, u_v, y4, dout, dz, dg4, loss_acc = _ffn_fwd(
        h2, w_up8, g_cw, conv_b.reshape(N_DEV, 1, FF_SHARD), w_down4, x1, ffn_post_norm, target, tm)

    dup_g, dup_v, dcb_g, dcb_v, dcw_g, dcw_v, gwu_g, gwu_v, gw_down4 = _ffn_bwd_a(
        dz, h2, w_down4, u_g, u_v, up_g, up_v, y4, g_cw, tm_big)
    dcb = jnp.concatenate([dcb_g, dcb_v], axis=0).reshape(1, 2 * D_FF)
    dcw = _from_shards(jnp.concatenate([dcw_g, dcw_v], axis=0))
    gw_down = gw_down4.reshape(D_FF, D_MODEL)
    (dx1, dmixed, da, dr, dg3, dg2), p_up = _ffn_bwd_b(
        dup_g, dup_v, w_up8, x1, dout, ffn_pre_norm, mixed, mix_post_norm, w_out_f, tm, gwu_g, gwu_v)
    gw_out = _wgrad([a, r], [dmixed], tk_grad, "wgrad_out")
    dret, (p_down,) = _ret_bwd(decay, qr, kr, vr, gr, o, states, dr, d_intra, d_intra_t, xi_b, zeta_b, cos, sin_s,
                               [gw_down.reshape(N_DEV, D_FF // N_DEV, D_MODEL)], nbs)
    (dqa, delta_t, dsink), (p_out,) = _attn_bwd_dq(sinks, qa, ka, va, da, lse, nbs,
                                                   [gw_out.reshape(N_DEV, D_MODEL // N_DEV, D_MODEL)])
    dka, dva = _attn_bwd_dkv(qa, ka, va, da, lse_t, delta_t, nbs)
    grad_x, dg1 = _in_proj_bwd(dqa, dka, dva, dret, w_in_f, x2, mix_pre_norm, dx1, tm)
    gw_in = _wgrad([h1], [dqa, dka, dva, dret], tk_grad, "wgrad_in")

    small = _pack_small([dg1, dg2, dg3, dg4], dcb, dcw, dsink[:, 0], loss_acc[0, 0])
    gw_in_t = gw_in.T.reshape(N_DEV, IN_W // N_DEV, D_MODEL).astype(BF)
    small_all, p_in = _exchange_call([small], [gw_in_t], "exchange_last")
    g_small = _sum_small(small_all)
    loss = g_small[208, N_ATTN_HEADS]

    t_in = lambda a: jnp.swapaxes(a, 1, 2)[0]
    g_w_in, d_w_in, nm_w_in, nv_w_in = [o.T for o in _adamw_shard(
        p_in, t_in(w_in), t_in(m_w_in), t_in(v_w_in), 176, "adamw_in")]
    g_w_up, d_w_up, nm_w_up, nv_w_up = [o.T for o in _adamw_shard(
        p_up, t_in(w_up), t_in(m_w_up), t_in(v_w_up), 176, "adamw_up")]
    g_w_out, d_w_out, nm_w_out, nv_w_out = _adamw_shard(p_out, w_out[0], m_w_out[0], v_w_out[0], 128, "adamw_out")
    g_w_down, d_w_down, nm_w_down, nv_w_down = _adamw_shard(p_down, w_down[0], m_w_down[0], v_w_down[0], 176,
                                                            "adamw_down")
    gains = [mix_pre_norm, mix_post_norm, ffn_pre_norm, ffn_post_norm]
    m_gains = [m_mix_pre_norm, m_mix_post_norm, m_ffn_pre_norm, m_ffn_post_norm]
    v_gains = [v_mix_pre_norm, v_mix_post_norm, v_ffn_pre_norm, v_ffn_post_norm]
    zeros_cw = jnp.zeros((3, 2 * D_FF), F32)
    w_small = _pack_small(gains, conv_b, zeros_cw, attn_sinks)
    m_small = _pack_small(m_gains, m_conv_b, zeros_cw, m_attn_sinks)
    v_small = _pack_small(v_gains, v_conv_b, zeros_cw, v_attn_sinks)
    d_small, nm_small, nv_small = _adamw_small(g_small, w_small, m_small, v_small, "adamw_small")
    shard_cols = 2 * D_FF // N_DEV
    g_cw = lax.dynamic_slice(g_small[76:208].reshape(3, 2 * D_FF), (0, me * shard_cols), (3, shard_cols))
    d_cw, nm_cw, nv_cw = _adamw_small(g_cw, conv_w[0], m_conv_w[0], v_conv_w[0], "adamw_conv_w")

    def unpack(p):
        gains_o = [p[8 * i:8 * i + 8].reshape(1, D_MODEL) for i in range(4)]
        return gains_o, p[32:76].reshape(1, 2 * D_FF), p[208:209, :N_ATTN_HEADS]

    def leaves(p, w_in_s, w_out_s, w_up_s, cw_s, w_down_s):
        (pre1, post1, pre2, post2), cb, sk = unpack(p)
        return [pre1, w_in_s[None], sk, w_out_s[None], post1, pre2, w_up_s[None], cw_s[None], cb, w_down_s[None],
                post2]

    return (loss, grad_x.reshape(1, T, D_MODEL),
            *leaves(g_small, g_w_in, g_w_out, g_w_up, g_cw, g_w_down),
            *leaves(d_small, d_w_in, d_w_out, d_w_up, d_cw, d_w_down),
            *leaves(nm_small, nm_w_in, nm_w_out, nm_w_up, nm_cw, nm_w_down),
            *leaves(nv_small, nv_w_in, nv_w_out, nv_w_up, nv_cw, nv_w_down))
```

```python
import functools
import math

import jax
import jax.numpy as jnp
from jax import lax
from jax.experimental import pallas as pl
from jax.experimental.pallas import tpu as pltpu
from jax.experimental.pallas import tpu_sc as plsc

F32 = jnp.float32
BF = jnp.bfloat16

N_DEV = 8
D_MODEL = 1024
HEAD_DIM = 64
ATTN_W = 512
N_ATTN_HEADS = 8
KV_W = 128
BLOCK = 128
RET_W = 512
N_RET_HEADS = 4
RET_HEAD_DIM = 128
IN_W = 2816
D_FF = 2816
RMS_EPS = 1e-6
GN_EPS = 1e-6
MASK_VALUE = -1e30
ATTN_SCALE = HEAD_DIM ** -0.5
RET_K_SCALE = RET_HEAD_DIM ** -0.5
GELU_C = math.sqrt(2.0 / math.pi)
GELU_A = 0.044715

ADAM_LR = 0.001
ADAM_B1 = 0.9
ADAM_B2 = 0.999
ADAM_EPS = 1e-08
ADAM_WD = 0.01
ADAM_STEP = 10

VMEM_LIMIT_BYTES = 56 * 1024 * 1024
FF_SHARD = 2 * D_FF // N_DEV
FF_PAIRS = N_DEV // 2

QA0, KA0, VA0, QR0, KR0, VR0, GR0 = 0, 512, 640, 768, 1280, 1792, 2304

MESH_ID = pl.DeviceIdType.MESH


def _pcall(body, **kw):
    return pl.pallas_call(body, **kw)


def _params(sem=None):
    return pltpu.CompilerParams(dimension_semantics=sem, vmem_limit_bytes=VMEM_LIMIT_BYTES)


def _dot(a, b):
    return jnp.dot(a, b, preferred_element_type=F32)


def _dot_nt(a, b):
    return lax.dot_general(a, b, (((1,), (1,)), ((), ())), preferred_element_type=F32)


def _dot_tn(a, b):
    return lax.dot_general(a, b, (((0,), (0,)), ((), ())), preferred_element_type=F32)


def _vmem_full():
    return pl.BlockSpec(memory_space=pltpu.VMEM)


def _smem_full():
    return pl.BlockSpec(memory_space=pltpu.SMEM)


def _rows(tm, w):
    return pl.BlockSpec((tm, w), lambda i: (i, 0))


def _const(shape):
    return pl.BlockSpec(shape, lambda i: tuple(0 for _ in shape))


def _rms_stats(x):
    r = lax.rsqrt(jnp.mean(x * x, axis=-1, keepdims=True) + RMS_EPS)
    return r, x * r


def _rms_bwd(n, r, dn):
    return r * (dn - n * jnp.mean(dn * n, axis=-1, keepdims=True))


def _rot(x, even):
    w = x.shape[1]
    return jnp.where(even, pltpu.roll(x, w - 1, 1), pltpu.roll(x, 1, 1))


def _peers():
    x, y, c = lax.axis_index("x"), lax.axis_index("y"), lax.axis_index("c")
    flips = [(0, 0, 1), (1, 0, 0), (0, 1, 0), (1, 1, 0), (1, 0, 1), (0, 1, 1), (1, 1, 1)]
    peers = [(x ^ fx, y ^ fy, c ^ fc) for fx, fy, fc in flips]
    return 4 * x + 2 * y + c, peers


SAME_CORE_PEERS = 4


class _Exchange:
    def __init__(self, gathers, swaps, send_sems, recv_sems, local_sems):
        self.me, self.peers = _peers()
        self.slots = [4 * px + 2 * py + pc for px, py, pc in self.peers]
        self.pairs = [(src, dst, True) for src, dst in gathers] + [(src, dst, False) for src, dst in swaps]
        self.send_sems, self.recv_sems, self.local_sems = send_sems, recv_sems, local_sems

    @staticmethod
    def scratch(n):
        return [pltpu.SemaphoreType.DMA((n, N_DEV - 1)), pltpu.SemaphoreType.DMA((n, N_DEV - 1)),
                pltpu.SemaphoreType.DMA((n,))]

    def _parts(self, a, slot):
        src, _, whole = self.pairs[a]
        half = N_DEV // 2
        if whole:
            return [(None, src)]
        if isinstance(src, tuple):
            return [(slot < half, src[0].at[jnp.minimum(slot, half - 1)]),
                    (slot >= half, src[1].at[jnp.maximum(slot - half, 0)])]
        return [(None, src.at[slot])]

    def _local(self, a, src):
        return pltpu.make_async_copy(src, self.pairs[a][1].at[self.me], self.local_sems.at[a])

    def _remote(self, a, k, src, slot):
        return pltpu.make_async_remote_copy(
            src_ref=src, dst_ref=self.pairs[a][1].at[slot], send_sem=self.send_sems.at[a, k],
            recv_sem=self.recv_sems.at[a, k], device_id=self.peers[k], device_id_type=MESH_ID)

    def start(self):
        def go(cond, copy):
            if cond is None:
                copy.start()
            else:
                pl.when(cond)(copy.start)

        for a in range(len(self.pairs)):
            for cond, src in self._parts(a, self.me):
                go(cond, self._local(a, src))
            for k in range(SAME_CORE_PEERS if self.pairs[a][2] else N_DEV - 1):
                for cond, src in self._parts(a, self.slots[k]):
                    go(cond, self._remote(a, k, src, self.me))

    def _pass_on(self, a, j):
        k = j + SAME_CORE_PEERS - 1
        block = self.pairs[a][1].at[self.slots[j]]
        return pltpu.make_async_remote_copy(
            src_ref=block, dst_ref=block, send_sem=self.send_sems.at[a, k], recv_sem=self.recv_sems.at[a, k],
            device_id=self.peers[0], device_id_type=MESH_ID)

    def wait(self):
        for a in range(len(self.pairs)):
            src = self._parts(a, self.me)[0][1]
            if self.pairs[a][2]:
                for j in range(1, SAME_CORE_PEERS):
                    self._remote(a, j, src, self.slots[j]).wait_recv()
                    self._pass_on(a, j).start()
                self._remote(a, 0, src, self.slots[0]).wait_recv()
            for k in range(SAME_CORE_PEERS if self.pairs[a][2] else 0, N_DEV - 1):
                self._remote(a, k, src, self.slots[k]).wait_recv()
        for a in range(len(self.pairs)):
            src = self._parts(a, self.me)[0][1]
            for k in range(SAME_CORE_PEERS if self.pairs[a][2] else N_DEV - 1):
                self._remote(a, k, src, self.me).wait_send()
            if self.pairs[a][2]:
                for j in range(1, SAME_CORE_PEERS):
                    self._pass_on(a, j).wait_send()
            self._local(a, src).wait()


def _sequencer_gather(v, name):
    src = jax.new_ref(v, memory_space=pltpu.MemorySpace.HBM)
    dst = jax.empty_ref(jax.ShapeDtypeStruct((N_DEV,) + v.shape, v.dtype), memory_space=pltpu.MemorySpace.HBM)

    @pl.kernel(mesh=plsc.ScalarSubcoreMesh(axis_name="seq", num_cores=1), name=name,
               scratch_types=(pltpu.SemaphoreType.DMA((N_DEV - 1,)), pltpu.SemaphoreType.DMA((N_DEV - 1,)),
                              pltpu.SemaphoreType.DMA),
               compiler_params=pltpu.CompilerParams(collective_id=0))
    def launch(send_sems, recv_sems, local_sem):
        me, peers = _peers()
        slots = [4 * px + 2 * py + pc for px, py, pc in peers]
        barrier = pltpu.get_barrier_semaphore()
        for peer in peers:
            pl.semaphore_signal(barrier, inc=1, device_id=peer, device_id_type=MESH_ID)
        pl.semaphore_wait(barrier, N_DEV - 1)

        def remote(k, slot):
            return pltpu.make_async_remote_copy(
                src_ref=src, dst_ref=dst.at[slot], send_sem=send_sems.at[k], recv_sem=recv_sems.at[k],
                device_id=peers[k], device_id_type=MESH_ID)

        local = pltpu.make_async_copy(src, dst.at[me], local_sem)
        local.start()
        for k in range(N_DEV - 1):
            remote(k, me).start()
        for k in range(N_DEV - 1):
            remote(k, slots[k]).wait_recv()
        for k in range(N_DEV - 1):
            remote(k, me).wait_send()
        local.wait()

    launch()
    return dst[...]


ANY_SPEC = pl.BlockSpec(memory_space=pl.ANY)


def _exchange_shapes(gathers, swaps):
    return ([jax.ShapeDtypeStruct((N_DEV,) + a.shape, a.dtype) for a in gathers]
            + [jax.ShapeDtypeStruct(a.shape, a.dtype) for a in swaps])


def _exchange_of(ins, outs, sems, ng):
    return _Exchange(list(zip(ins[:ng], outs[:ng])), list(zip(ins[ng:], outs[ng:])), *sems)


def _exchange_call(gathers, swaps, name):
    ng, ns = len(gathers), len(swaps)
    n = ng + ns

    def body(*refs):
        ex = _exchange_of(refs[:n], refs[n:2 * n], refs[2 * n:], ng)
        ex.start()
        ex.wait()

    return _pcall(
        body, name=name, out_shape=_exchange_shapes(gathers, swaps),
        in_specs=[ANY_SPEC] * (ng + ns), out_specs=[ANY_SPEC] * (ng + ns),
        scratch_shapes=_Exchange.scratch(ng + ns),
    )(*gathers, *swaps)


def _in_proj(x, g1, w_in, cos, sin_s, tm, gathers):
    T = x.shape[0]
    ng = len(gathers)
    nt = T // tm

    def body(x_ref, g_ref, w_ref, cos_ref, sin_ref, *rest):
        ex = _exchange_of(rest[:ng], rest[ng + 8:2 * ng + 8], rest[2 * ng + 8:], ng)
        h_ref, qa_ref, ka_ref, va_ref, qr_ref, kr_ref, vr_ref, gr_ref = rest[ng:ng + 8]
        pl.when(pl.program_id(0) == 0)(ex.start)
        r, n = _rms_stats(x_ref[...])
        h = (n * g_ref[...]).astype(BF)
        h_ref[...] = h

        def proj(c0, w):
            return _dot_nt(h, w_ref[c0:c0 + w, :])

        qa_ref[...] = proj(QA0, ATTN_W).astype(BF)
        kva = proj(KA0, 2 * KV_W)
        ka_ref[...] = kva[:, :KV_W].astype(BF)
        va_ref[...] = kva[:, KV_W:].astype(BF)
        vr_ref[...] = proj(VR0, RET_W).astype(BF)
        gr_ref[...] = proj(GR0, RET_W)
        cos_t, sin_t = cos_ref[...], sin_ref[...]
        even = lax.broadcasted_iota(jnp.int32, (tm, RET_HEAD_DIM), 1) % 2 == 0
        for c0, scale, out_ref in ((QR0, None, qr_ref), (KR0, RET_K_SCALE, kr_ref)):
            full = proj(c0, RET_W)
            for hd in range(N_RET_HEADS):
                cs = slice(hd * RET_HEAD_DIM, (hd + 1) * RET_HEAD_DIM)
                t = full[:, cs] if scale is None else full[:, cs] * scale
                out_ref[:, cs] = (t * cos_t + _rot(t, even) * sin_t).astype(BF)
        pl.when(pl.program_id(0) == nt - 1)(ex.wait)

    widths = [D_MODEL, ATTN_W, KV_W, KV_W, RET_W, RET_W, RET_W, RET_W]
    dts = [BF] * 7 + [F32]
    outs = _pcall(
        body, name="in_proj", grid=(nt,),
        in_specs=[_rows(tm, D_MODEL), _const((1, D_MODEL)), _vmem_full(), _rows(tm, RET_HEAD_DIM),
                  _rows(tm, RET_HEAD_DIM)] + [ANY_SPEC] * ng,
        out_specs=[_rows(tm, w) for w in widths] + [ANY_SPEC] * ng,
        out_shape=[jax.ShapeDtypeStruct((T, w), dt) for w, dt in zip(widths, dts)] + _exchange_shapes(gathers, []),
        scratch_shapes=_Exchange.scratch(ng),
        compiler_params=_params(("arbitrary",)),
    )(x, g1, w_in, cos, sin_s, *gathers)
    return outs[:8], outs[8:]


def _kv_variants(kk):
    kf = kk.astype(F32)
    lo = lax.broadcasted_iota(jnp.int32, kf.shape, 1) < HEAD_DIM
    h0_lo = jnp.where(lo, kf, 0.0)
    h1_hi = jnp.where(lo, 0.0, kf)
    h0_hi = pltpu.roll(h0_lo, HEAD_DIM, 1)
    h1_lo = pltpu.roll(h1_hi, HEAD_DIM, 1)
    return [[h0_lo.astype(BF), h0_hi.astype(BF)], [h1_lo.astype(BF), h1_hi.astype(BF)]]


def _col_to_tile(tile, col, head):
    lane = lax.broadcasted_iota(jnp.int32, tile.shape, 1)
    return jnp.where(lane == head, col, tile)


def _tri(rows, key_major=False):
    i = lax.broadcasted_iota(jnp.int32, (rows, BLOCK), 0) & (BLOCK - 1)
    j = lax.broadcasted_iota(jnp.int32, (rows, BLOCK), 1)
    return i > j if key_major else j > i


def _fold(x2, tri, first_above):
    a, b = x2[:, :BLOCK], x2[:, BLOCK:]
    return jnp.where(tri, a, b) if first_above else jnp.where(tri, b, a)


def _unfold(x, tri, first_above):
    up, low = jnp.where(tri, x, 0.0), jnp.where(tri, 0.0, x)
    return jnp.concatenate([up, low] if first_above else [low, up], axis=1).astype(BF)


def _scaled(q):
    return (q.astype(F32) * ATTN_SCALE).astype(BF)


def _cat_variants(prev, cur):
    return [[jnp.concatenate([prev[h][e], cur[h][e]], axis=0) for e in range(2)] for h in range(2)]


def _block_variants(prev_ref, cur_ref, nbs):
    var = [_kv_variants(prev_ref[...])] + [_kv_variants(cur_ref[b * BLOCK:(b + 1) * BLOCK, :]) for b in range(nbs)]
    return [_cat_variants(var[b], var[b + 1]) for b in range(nbs)]


def _head_cols(col, nbs):
    tiles = []
    for b in range(nbs):
        t = jnp.zeros((BLOCK, BLOCK), F32)
        for head in range(N_ATTN_HEADS):
            r0 = (b * N_ATTN_HEADS + head) * BLOCK
            t = _col_to_tile(t, col[r0:r0 + BLOCK, :], head)
        tiles.append(t)
    return tiles


def _attn_fwd(sinks, qa, ka, va, nbs, gathers):
    T = qa.shape[0]
    steps = T // (BLOCK * nbs)
    R = nbs * N_ATTN_HEADS * BLOCK
    ng = len(gathers)

    def body(sink_ref, q_ref, kc_ref, kp_ref, vc_ref, vp_ref, *rest):
        ex = _exchange_of(rest[:ng], rest[ng + 3:2 * ng + 3], rest[2 * ng + 3:], ng)
        a_ref, lse_ref, lset_ref = rest[ng:ng + 3]
        n = pl.program_id(0)
        pl.when(n == 0)(ex.start)
        kcat = _block_variants(kp_ref, kc_ref, nbs)
        vcat = _block_variants(vp_ref, vc_ref, nbs)
        tri1 = _tri(BLOCK)
        tiles = []
        for b in range(nbs):
            for pair in range(N_ATTN_HEADS // 2):
                qp = _scaled(q_ref[b * BLOCK:(b + 1) * BLOCK, pair * 128:(pair + 1) * 128])
                for e in range(2):
                    s = _fold(_dot_nt(qp, kcat[b][pair // 2][e]), tri1, True)
                    if b == 0:
                        s = jnp.where(tri1 & (n == 0), MASK_VALUE, s)
                    tiles.append(s)
        s = jnp.concatenate(tiles, axis=0)
        sink = jnp.concatenate([jnp.full((BLOCK, 1), sink_ref[head], F32)
                                for _ in range(nbs) for head in range(N_ATTN_HEADS)], axis=0)
        m = jnp.maximum(jnp.max(s, axis=-1, keepdims=True), sink)
        p = jnp.exp(s - m)
        z = jnp.sum(p, axis=-1, keepdims=True) + jnp.exp(sink - m)
        p2 = _unfold(p * (1.0 / z), _tri(R), True)
        for b in range(nbs):
            for pair in range(N_ATTN_HEADS // 2):
                r0 = (b * N_ATTN_HEADS + 2 * pair) * BLOCK
                acc = (_dot(p2[r0:r0 + BLOCK, :], vcat[b][pair // 2][0])
                       + _dot(p2[r0 + BLOCK:r0 + 2 * BLOCK, :], vcat[b][pair // 2][1]))
                a_ref[b * BLOCK:(b + 1) * BLOCK, pair * 128:(pair + 1) * 128] = acc.astype(BF)
        for b, t in enumerate(_head_cols(m + jnp.log(z), nbs)):
            lse_ref[b * BLOCK:(b + 1) * BLOCK, :] = t
            lset_ref[:, b * BLOCK:(b + 1) * BLOCK] = t.T[:N_ATTN_HEADS, :]
        pl.when(n == steps - 1)(ex.wait)

    cur = lambda w: pl.BlockSpec((BLOCK * nbs, w), lambda n: (n, 0))
    prev = lambda w: pl.BlockSpec((BLOCK, w), lambda n: (jnp.maximum(n * nbs - 1, 0), 0))
    outs = _pcall(
        body, name="attn_fwd", grid=(steps,),
        in_specs=[_smem_full(), cur(ATTN_W), cur(KV_W), prev(KV_W), cur(KV_W), prev(KV_W)] + [ANY_SPEC] * ng,
        out_specs=[cur(ATTN_W), cur(BLOCK), pl.BlockSpec((N_ATTN_HEADS, BLOCK * nbs), lambda n: (0, n))]
        + [ANY_SPEC] * ng,
        out_shape=[jax.ShapeDtypeStruct((T, ATTN_W), BF), jax.ShapeDtypeStruct((T, BLOCK), F32),
                   jax.ShapeDtypeStruct((N_ATTN_HEADS, T), F32)] + _exchange_shapes(gathers, []),
        scratch_shapes=_Exchange.scratch(ng),
        compiler_params=_params(("arbitrary",)),
    )(sinks, qa, ka, ka, va, va, *gathers)
    return outs[:3], outs[3:]


def _ret_fwd(decay, qr, kr, vr, gr, d_intra, xi_b, zeta_b, ncs):
    T = qr.shape[0]
    nc = T // BLOCK
    H, C = N_RET_HEADS, RET_HEAD_DIM

    def body(decay_ref, q_ref, k_ref, v_ref, g_ref, d_ref, xi_ref, zeta_ref, o_ref, s_ref, r_ref, state):
        @pl.when(pl.program_id(0) == 0)
        def _():
            state[...] = jnp.zeros_like(state)

        pairs = [(b, h) for b in range(ncs) for h in range(H)]
        sl = lambda b, h: (slice(b * BLOCK, (b + 1) * BLOCK), slice(h * C, (h + 1) * C))
        tab = lambda ref: jnp.concatenate([ref[h] for _, h in pairs], axis=0)
        q = [q_ref[sl(b, h)] for b, h in pairs]
        k = [k_ref[sl(b, h)] for b, h in pairs]
        v = [v_ref[sl(b, h)] for b, h in pairs]
        inner = (jnp.concatenate([_dot_nt(q[i], k[i]) for i in range(len(pairs))], axis=0) * tab(d_ref)).astype(BF)
        kz = (jnp.concatenate(k, axis=0).astype(F32) * tab(zeta_ref)).astype(BF)
        o1 = [_dot(inner[i * BLOCK:(i + 1) * BLOCK, :], v[i]) for i in range(len(pairs))]
        kv = [_dot_tn(kz[i * BLOCK:(i + 1) * BLOCK, :], v[i]) for i in range(len(pairs))]
        st_b = [None] * len(pairs)
        for h in range(H):
            st = state[h]
            for b in range(ncs):
                i = b * H + h
                st_b[i] = st.astype(BF)
                s_ref[b, h] = st_b[i]
                st = decay_ref[h] * st + kv[i]
            state[h] = st
        o2 = jnp.concatenate([_dot(q[i], st_b[i]) for i in range(len(pairs))], axis=0)
        o = jnp.concatenate(o1, axis=0) + o2 * tab(xi_ref)
        mu = jnp.mean(o, axis=-1, keepdims=True)
        oc = o - mu
        rs = lax.rsqrt(jnp.mean(oc * oc, axis=-1, keepdims=True) + GN_EPS)
        g = jnp.concatenate([g_ref[sl(b, h)] for b, h in pairs], axis=0)
        r = (g * jax.nn.sigmoid(g) * (oc * rs)).astype(BF)
        for i, (b, h) in enumerate(pairs):
            o_ref[sl(b, h)] = o[i * BLOCK:(i + 1) * BLOCK, :]
            r_ref[sl(b, h)] = r[i * BLOCK:(i + 1) * BLOCK, :]

    cur = pl.BlockSpec((BLOCK * ncs, RET_W), lambda n: (n, 0))
    tab = pl.BlockSpec((H, C, C), lambda n: (0, 0, 0))
    return _pcall(
        body, name="ret_fwd", grid=(nc // ncs,),
        in_specs=[_smem_full(), cur, cur, cur, cur, tab, tab, tab],
        out_specs=[cur, pl.BlockSpec((ncs, H, C, C), lambda n: (n, 0, 0, 0)), cur],
        out_shape=[jax.ShapeDtypeStruct((T, RET_W), F32), jax.ShapeDtypeStruct((nc, H, C, C), BF),
                   jax.ShapeDtypeStruct((T, RET_W), BF)],
        scratch_shapes=[pltpu.VMEM((H, C, C), F32)],
        compiler_params=_params(("arbitrary",)),
    )(decay, qr, kr, vr, gr, d_intra, xi_b, zeta_b)


def _out_proj(a, r, w_out, x, g2, g3, tm):
    T = x.shape[0]

    def body(a_ref, r_ref, w_ref, x_ref, g2_ref, g3_ref, mixed_ref, x1_ref, h2_ref):
        mixed = _dot(a_ref[...], w_ref[:ATTN_W, :]) + _dot(r_ref[...], w_ref[ATTN_W:, :])
        mixed_ref[...] = mixed
        _, n2 = _rms_stats(mixed)
        x1 = x_ref[...] + n2 * g2_ref[...]
        x1_ref[...] = x1
        _, n3 = _rms_stats(x1)
        h2_ref[...] = (n3 * g3_ref[...]).astype(BF)

    return _pcall(
        body, name="out_proj", grid=(T // tm,),
        in_specs=[_rows(tm, ATTN_W), _rows(tm, RET_W), _vmem_full(), _rows(tm, D_MODEL), _const((1, D_MODEL)),
                  _const((1, D_MODEL))],
        out_specs=[_rows(tm, D_MODEL)] * 3,
        out_shape=[jax.ShapeDtypeStruct((T, D_MODEL), F32), jax.ShapeDtypeStruct((T, D_MODEL), F32),
                   jax.ShapeDtypeStruct((T, D_MODEL), BF)],
        compiler_params=_params(("parallel",)),
    )(a, r, w_out, x, g2, g3)


def _shift_down(cur, k, before):
    out = pltpu.roll(cur, k, 0)
    row = lax.broadcasted_iota(jnp.int32, before.shape, 0)
    top = jnp.where(row < k, pltpu.roll(before, k, 0), out[0:8])
    return jnp.concatenate([top, out[8:]], axis=0)


def _shift_up(cur, k, after):
    tm = cur.shape[0]
    out = pltpu.roll(cur, tm - k, 0)
    row = lax.broadcasted_iota(jnp.int32, after.shape, 0)
    bot = jnp.where(row >= 8 - k, pltpu.roll(after, 8 - k, 0), out[tm - 8:])
    return jnp.concatenate([out[:tm - 8], bot], axis=0)


def _gelu_parts(x):
    m = (-2.0 * GELU_C * GELU_A) * (x * x)
    s = 1.0 / (1.0 + jnp.exp(x * (m - 2.0 * GELU_C)))
    gelu = x * s
    dgelu = s + gelu * (1.0 - s) * (2.0 * GELU_C - 3.0 * m)
    return gelu, dgelu


def _ffn_fwd(h2, w_up8, conv_w8, conv_b8, w_down4, x1, g4, target, tm):
    T = h2.shape[0]
    nt = T // tm

    def body(h_ref, wu_ref, cwg_ref, cwv_ref, cbg_ref, cbv_ref, wd_ref, x1_ref, g_ref, t_ref,
             upg_ref, upv_ref, ug_ref, uv_ref, y_ref, dout_ref, dz_ref, dg4_ref, loss_ref, halo, z_acc):
        s = pl.program_id(1)
        first = pl.program_id(0) == 0

        @pl.when(first & (s == 0))
        def _():
            loss_ref[...] = jnp.zeros_like(loss_ref)
            dg4_ref[...] = jnp.zeros_like(dg4_ref)

        h = h_ref[...]
        u = []
        parts = ((cwg_ref, cbg_ref, upg_ref, ug_ref), (cwv_ref, cbv_ref, upv_ref, uv_ref))
        for part, (cw_ref, cb_ref, up_ref, u_ref) in enumerate(parts):
            cur = _dot(h, wu_ref[s + part * FF_PAIRS])
            up_ref[0] = cur.astype(BF)
            before = jnp.where(first, 0.0, halo[part, s])
            halo[part, s] = cur[tm - 8:tm, :]
            u_c = (cw_ref[0, pl.ds(0, 1), :] * _shift_down(cur, 2, before)
                   + cw_ref[0, pl.ds(1, 1), :] * _shift_down(cur, 1, before)
                   + cw_ref[0, pl.ds(2, 1), :] * cur + cb_ref[0])
            u_ref[0] = u_c
            u.append(u_c)
        gelu, _ = _gelu_parts(u[0])
        y = (gelu * u[1]).astype(BF)
        y_ref[0] = y
        z_part = _dot(y, wd_ref[s])

        @pl.when(s == 0)
        def _():
            z_acc[...] = z_part

        @pl.when(s > 0)
        def _():
            z_acc[...] += z_part

        @pl.when(s == FF_PAIRS - 1)
        def _():
            r4, n4 = _rms_stats(z_acc[...])
            err = x1_ref[...] + n4 * g_ref[...] - t_ref[...]
            dout = err * (1.0 / D_MODEL)
            dout_ref[...] = dout
            loss_ref[...] += 0.5 * jnp.sum(jnp.mean(err * err, axis=-1, keepdims=True), axis=0, keepdims=True)
            dg4_ref[...] += jnp.sum(dout * n4, axis=0, keepdims=True)
            dz_ref[...] = _rms_bwd(n4, r4, dout * g_ref[...]).astype(BF)

    rows = pl.BlockSpec((tm, D_MODEL), lambda i, s: (i, 0))
    one = lambda shape: pl.BlockSpec(shape, lambda i, s: tuple(0 for _ in shape))
    gate = lambda r, w: pl.BlockSpec((1, r, w), lambda i, s: (s, 0, 0))
    val = lambda r, w: pl.BlockSpec((1, r, w), lambda i, s: (s + FF_PAIRS, 0, 0))
    tile = pl.BlockSpec((1, tm, FF_SHARD), lambda i, s: (s, i, 0))
    half = lambda dt: jax.ShapeDtypeStruct((FF_PAIRS, T, FF_SHARD), dt)
    return _pcall(
        body, name="ffn_fwd", grid=(nt, FF_PAIRS),
        in_specs=[rows, _vmem_full(), gate(3, FF_SHARD), val(3, FF_SHARD), gate(1, FF_SHARD), val(1, FF_SHARD),
                  _vmem_full(), rows, one((1, D_MODEL)), rows],
        out_specs=[tile] * 5 + [rows, rows, one((1, D_MODEL)), one((8, 128))],
        out_shape=[half(BF), half(BF), half(F32), half(F32), half(BF), jax.ShapeDtypeStruct((T, D_MODEL), F32),
                   jax.ShapeDtypeStruct((T, D_MODEL), BF), jax.ShapeDtypeStruct((1, D_MODEL), F32),
                   jax.ShapeDtypeStruct((8, 128), F32)],
        scratch_shapes=[pltpu.VMEM((2, FF_PAIRS, 8, FF_SHARD), F32), pltpu.VMEM((tm, D_MODEL), F32)],
        compiler_params=_params(("arbitrary", "arbitrary")),
    )(h2, w_up8, conv_w8, conv_w8, conv_b8, conv_b8, w_down4, x1, g4, target)


def _ffn_bwd_a(dz, h2, w_down4, u_g, u_v, up_g, up_v, y4, conv_w8, tm):
    T = dz.shape[0]
    nt = T // tm

    def body(dz_ref, h_ref, wd_ref, ug_ref, uv_ref, upg_ref, upv_ref, y_ref, cwg_ref, cwv_ref,
             dupg_ref, dupv_ref, dcbg_ref, dcbv_ref, dcwg_ref, dcwv_ref, gwug_out, gwuv_out, gwd_out,
             carry, gwug_ref, gwuv_ref, gwd_ref):
        @pl.when(pl.program_id(1) == 0)
        def _():
            for ref in (dcbg_ref, dcbv_ref, dcwg_ref, dcwv_ref, gwug_ref, gwuv_ref, gwd_ref, carry):
                ref[...] = jnp.zeros_like(ref)

        dz = dz_ref[...]
        h = h_ref[...]
        dy = _dot_nt(dz, wd_ref[0])
        gwd_ref[0] += _dot_tn(y_ref[0], dz)
        gelu, dgelu = _gelu_parts(ug_ref[0])
        parts = ((0, dy * uv_ref[0] * dgelu, upg_ref, cwg_ref, dupg_ref, dcbg_ref, dcwg_ref, gwug_ref),
                 (1, dy * gelu, upv_ref, cwv_ref, dupv_ref, dcbv_ref, dcwv_ref, gwuv_ref))
        for part, d, up_ref, cw_ref, dup_ref, dcb_ref, dcw_ref, gwu_ref in parts:
            after = carry[part]
            d1 = _shift_up(d, 1, after)
            d2 = _shift_up(d, 2, after)
            carry[part] = d[0:8, :]
            upc = up_ref[0].astype(F32)
            dcb_ref[0] += jnp.sum(d, axis=0, keepdims=True)
            dcw_ref[0, pl.ds(2, 1), :] += jnp.sum(d * upc, axis=0, keepdims=True)
            dcw_ref[0, pl.ds(1, 1), :] += jnp.sum(d1 * upc, axis=0, keepdims=True)
            dcw_ref[0, pl.ds(0, 1), :] += jnp.sum(d2 * upc, axis=0, keepdims=True)
            dup = (cw_ref[0, pl.ds(2, 1), :] * d + cw_ref[0, pl.ds(1, 1), :] * d1
                   + cw_ref[0, pl.ds(0, 1), :] * d2).astype(BF)
            dup_ref[0] = dup
            gwu_ref[0] += _dot_tn(h, dup)

        @pl.when(pl.program_id(1) == nt - 1)
        def _():
            s = pl.program_id(0)
            pltpu.sync_copy(gwd_ref, gwd_out.at[pl.ds(s, 1)])
            for acc_ref, out in ((gwug_ref, gwug_out), (gwuv_ref, gwuv_out)):
                gwd_ref[0] = acc_ref[0].T
                pltpu.sync_copy(gwd_ref, out.at[pl.ds(s, 1)])

    rev = pl.BlockSpec((tm, D_MODEL), lambda s, i: (nt - 1 - i, 0))
    tile = pl.BlockSpec((1, tm, FF_SHARD), lambda s, i: (s, nt - 1 - i, 0))
    acc = lambda r, w: pl.BlockSpec((1, r, w), lambda s, i: (s, 0, 0))
    acc_val = pl.BlockSpec((1, 3, FF_SHARD), lambda s, i: (s + FF_PAIRS, 0, 0))
    half = lambda r, dt: jax.ShapeDtypeStruct((FF_PAIRS, r, FF_SHARD), dt)
    return _pcall(
        body, name="ffn_bwd_a", grid=(FF_PAIRS, nt),
        in_specs=[rev, rev, acc(FF_SHARD, D_MODEL), tile, tile, tile, tile, tile, acc(3, FF_SHARD), acc_val],
        out_specs=[tile, tile, acc(1, FF_SHARD), acc(1, FF_SHARD), acc(3, FF_SHARD), acc(3, FF_SHARD),
                   ANY_SPEC, ANY_SPEC, ANY_SPEC],
        out_shape=[half(T, BF), half(T, BF), half(1, F32), half(1, F32), half(3, F32), half(3, F32),
                   jax.ShapeDtypeStruct((FF_PAIRS, FF_SHARD, D_MODEL), F32),
                   jax.ShapeDtypeStruct((FF_PAIRS, FF_SHARD, D_MODEL), F32),
                   jax.ShapeDtypeStruct((FF_PAIRS, FF_SHARD, D_MODEL), F32)],
        scratch_shapes=[pltpu.VMEM((2, 8, FF_SHARD), F32), pltpu.VMEM((1, D_MODEL, FF_SHARD), F32),
                        pltpu.VMEM((1, D_MODEL, FF_SHARD), F32), pltpu.VMEM((1, FF_SHARD, D_MODEL), F32)],
        compiler_params=_params(("arbitrary", "arbitrary")),
    )(dz, h2, w_down4, u_g, u_v, up_g, up_v, y4, conv_w8, conv_w8)


def _ffn_bwd_b(dup_g, dup_v, w_up8, x1, dout, g3, mixed, g2, w_out, tm, gwu_g, gwu_v):
    T = x1.shape[0]
    nt = T // tm

    def body(dupg_ref, dupv_ref, wup_ref, x1_ref, dout_ref, g3_ref, mixed_ref, g2_ref, wout_ref, gwug_ref, gwuv_ref,
             dx1_ref, dmixed_ref, da_ref, dr_ref, dg3_ref, dg2_ref, pup_ref, *sems):
        ex = _Exchange([], [((gwug_ref, gwuv_ref), pup_ref)], *sems)

        @pl.when(pl.program_id(0) == 0)
        def _():
            ex.start()
            dg3_ref[...] = jnp.zeros_like(dg3_ref)
            dg2_ref[...] = jnp.zeros_like(dg2_ref)

        dh2 = jnp.zeros((tm, D_MODEL), F32)
        for s in range(FF_PAIRS):
            dh2 = dh2 + _dot_nt(dupg_ref[s], wup_ref[s]) + _dot_nt(dupv_ref[s], wup_ref[s + FF_PAIRS])
        r3, n3 = _rms_stats(x1_ref[...])
        dg3_ref[...] += jnp.sum(dh2 * n3, axis=0, keepdims=True)
        dx1 = dout_ref[...] + _rms_bwd(n3, r3, dh2 * g3_ref[...])
        dx1_ref[...] = dx1
        r2, n2 = _rms_stats(mixed_ref[...])
        dg2_ref[...] += jnp.sum(dx1 * n2, axis=0, keepdims=True)
        dmixed = _rms_bwd(n2, r2, dx1 * g2_ref[...]).astype(BF)
        dmixed_ref[...] = dmixed
        da_ref[...] = _dot_nt(dmixed, wout_ref[:ATTN_W, :])
        dr_ref[...] = _dot_nt(dmixed, wout_ref[ATTN_W:, :])
        pl.when(pl.program_id(0) == nt - 1)(ex.wait)

    half = pl.BlockSpec((FF_PAIRS, tm, FF_SHARD), lambda i: (0, i, 0))
    outs = _pcall(
        body, name="ffn_bwd_b", grid=(nt,),
        in_specs=[half, half, _vmem_full(), _rows(tm, D_MODEL), _rows(tm, D_MODEL), _const((1, D_MODEL)),
                  _rows(tm, D_MODEL), _const((1, D_MODEL)), _vmem_full(), ANY_SPEC, ANY_SPEC],
        out_specs=[_rows(tm, D_MODEL), _rows(tm, D_MODEL), _rows(tm, ATTN_W), _rows(tm, RET_W),
                   _const((1, D_MODEL)), _const((1, D_MODEL)), ANY_SPEC],
        out_shape=[jax.ShapeDtypeStruct((T, D_MODEL), F32), jax.ShapeDtypeStruct((T, D_MODEL), BF),
                   jax.ShapeDtypeStruct((T, ATTN_W), F32), jax.ShapeDtypeStruct((T, RET_W), F32),
                   jax.ShapeDtypeStruct((1, D_MODEL), F32), jax.ShapeDtypeStruct((1, D_MODEL), F32),
                   jax.ShapeDtypeStruct((N_DEV, FF_SHARD, D_MODEL), F32)],
        scratch_shapes=_Exchange.scratch(1),
        compiler_params=_params(("arbitrary",)),
    )(dup_g, dup_v, w_up8, x1, dout, g3, mixed, g2, w_out, gwu_g, gwu_v)
    return outs[:6], outs[6]


def _ret_bwd(decay, qr, kr, vr, gr, o, states, dr, d_intra, d_intra_t, xi_b, zeta_b, cos, sin_s, swaps, ncs):
    T = qr.shape[0]
    nc = T // BLOCK
    H, C = N_RET_HEADS, RET_HEAD_DIM
    ns = len(swaps)

    def body(decay_ref, q_ref, k_ref, v_ref, g_ref, o_ref, s_ref, dr_ref, d_ref, dt_ref, xi_ref, zeta_ref,
             cos_ref, sin_ref, *rest):
        ex = _exchange_of(rest[:ns], rest[ns + 1:2 * ns + 1], rest[2 * ns + 2:], 0)
        dret_ref, gstate = rest[ns], rest[2 * ns + 1]

        @pl.when(pl.program_id(0) == 0)
        def _():
            ex.start()
            gstate[...] = jnp.zeros_like(gstate)

        pairs = [(b, h) for b in range(ncs) for h in range(H)]
        n = len(pairs)
        sl = lambda b, h: (slice(b * BLOCK, (b + 1) * BLOCK), slice(h * C, (h + 1) * C))
        cat = lambda ref: jnp.concatenate([ref[sl(b, h)] for b, h in pairs], axis=0)
        tab = lambda ref: jnp.concatenate([ref[h] for _, h in pairs], axis=0)
        part = lambda x, i: x[i * BLOCK:(i + 1) * BLOCK, :]
        q = [q_ref[sl(b, h)] for b, h in pairs]
        k = [k_ref[sl(b, h)] for b, h in pairs]
        v = [v_ref[sl(b, h)] for b, h in pairs]
        g, o_all, dr_all = cat(g_ref), cat(o_ref), cat(dr_ref)
        mu = jnp.mean(o_all, axis=-1, keepdims=True)
        oc = o_all - mu
        rs = lax.rsqrt(jnp.mean(oc * oc, axis=-1, keepdims=True) + GN_EPS)
        on = oc * rs
        sg = jax.nn.sigmoid(g)
        dg = (dr_all * on * (sg * (1.0 + g * (1.0 - sg)))).astype(BF)
        don = dr_all * (g * sg)
        do = rs * (don - jnp.mean(don, axis=-1, keepdims=True) - on * jnp.mean(don * on, axis=-1, keepdims=True))
        do_b = do.astype(BF)
        dox_b = (do * tab(xi_ref)).astype(BF)
        zeta = tab(zeta_ref)
        kz = (jnp.concatenate(k, axis=0).astype(F32) * zeta).astype(BF)
        d_t = tab(dt_ref)
        da_b = (jnp.concatenate([_dot_nt(part(do_b, i), v[i]) for i in range(n)], axis=0) * tab(d_ref)).astype(BF)
        dat_b = (jnp.concatenate([_dot_nt(v[i], part(do_b, i)) for i in range(n)], axis=0) * d_t).astype(BF)
        mt_b = (jnp.concatenate([_dot_nt(k[i], q[i]) for i in range(n)], axis=0) * d_t).astype(BF)
        dq = [_dot(part(da_b, i), k[i]) + _dot_nt(part(dox_b, i), s_ref[pairs[i]]) for i in range(n)]
        dk1 = [_dot(part(dat_b, i), q[i]) for i in range(n)]
        dv1 = [_dot(part(mt_b, i), part(do_b, i)) for i in range(n)]
        qtd = [_dot_tn(q[i], part(dox_b, i)) for i in range(n)]
        gst_b = [None] * n
        for h in range(H):
            gst = gstate[h]
            for b in reversed(range(ncs)):
                i = b * H + h
                gst_b[i] = gst.astype(BF)
                gst = decay_ref[h] * gst + qtd[i]
            gstate[h] = gst
        dk2 = jnp.concatenate([_dot_nt(v[i], gst_b[i]) for i in range(n)], axis=0) * zeta
        dv = jnp.concatenate([dv1[i] + _dot(part(kz, i), gst_b[i]) for i in range(n)], axis=0).astype(BF)
        even = lax.broadcasted_iota(jnp.int32, (n * BLOCK, C), 1) % 2 == 0
        cos_t = jnp.concatenate([cos_ref[b * BLOCK:(b + 1) * BLOCK, :] for b, _ in pairs], axis=0)
        sin_t = jnp.concatenate([sin_ref[b * BLOCK:(b + 1) * BLOCK, :] for b, _ in pairs], axis=0)
        dq = jnp.concatenate(dq, axis=0)
        dk = jnp.concatenate(dk1, axis=0) + dk2
        dq = (dq * cos_t - _rot(dq, even) * sin_t).astype(BF)
        dk = ((dk * cos_t - _rot(dk, even) * sin_t) * RET_K_SCALE).astype(BF)
        for i, (b, h) in enumerate(pairs):
            rows = slice(b * BLOCK, (b + 1) * BLOCK)
            for j, x in enumerate((dq, dk, dv, dg)):
                dret_ref[rows, j * RET_W + h * C:j * RET_W + (h + 1) * C] = part(x, i)
        pl.when(pl.program_id(0) == steps - 1)(ex.wait)

    steps = nc // ncs
    rev = lambda w: pl.BlockSpec((BLOCK * ncs, w), lambda n: (steps - 1 - n, 0))
    tab = pl.BlockSpec((H, C, C), lambda n: (0, 0, 0))
    outs = _pcall(
        body, name="ret_bwd", grid=(steps,),
        in_specs=[_smem_full(), rev(RET_W), rev(RET_W), rev(RET_W), rev(RET_W), rev(RET_W),
                  pl.BlockSpec((ncs, H, C, C), lambda n: (steps - 1 - n, 0, 0, 0)), rev(RET_W), tab, tab, tab, tab,
                  rev(C), rev(C)] + [ANY_SPEC] * ns,
        out_specs=[rev(4 * RET_W)] + [ANY_SPEC] * ns,
        out_shape=[jax.ShapeDtypeStruct((T, 4 * RET_W), BF)] + _exchange_shapes([], swaps),
        scratch_shapes=[pltpu.VMEM((H, C, C), F32)] + _Exchange.scratch(ns),
        compiler_params=_params(("arbitrary",)),
    )(decay, qr, kr, vr, gr, o, states, dr, d_intra, d_intra_t, xi_b, zeta_b, cos, sin_s, *swaps)
    return outs[0], outs[1:]


def _attn_bwd_dq(sinks, qa, ka, va, da, lse, nbs, swaps):
    T = qa.shape[0]
    steps = T // (BLOCK * nbs)
    R = nbs * N_ATTN_HEADS * BLOCK
    ns = len(swaps)

    def body(sink_ref, q_ref, kc_ref, kp_ref, vc_ref, vp_ref, da_ref, lse_ref, *rest):
        ex = _exchange_of(rest[:ns], rest[ns + 3:2 * ns + 3], rest[2 * ns + 3:], 0)
        dq_ref, deltat_ref, dsink_ref = rest[ns:ns + 3]
        n = pl.program_id(0)

        @pl.when(n == 0)
        def _():
            ex.start()
            dsink_ref[...] = jnp.zeros_like(dsink_ref)

        kcat = _block_variants(kp_ref, kc_ref, nbs)
        vcat = _block_variants(vp_ref, vc_ref, nbs)
        tri1 = _tri(BLOCK)
        lane = lax.broadcasted_iota(jnp.int32, (BLOCK, BLOCK), 1)
        s_tiles, dp_tiles, lse_cols = [], [], []
        for b in range(nbs):
            rows = slice(b * BLOCK, (b + 1) * BLOCK)
            lse_tile = lse_ref[rows, :]
            for pair in range(N_ATTN_HEADS // 2):
                qp = _scaled(q_ref[rows, pair * 128:(pair + 1) * 128])
                dop = da_ref[rows, pair * 128:(pair + 1) * 128].astype(BF)
                for e in range(2):
                    s = _fold(_dot_nt(qp, kcat[b][pair // 2][e]), tri1, True)
                    if b == 0:
                        s = jnp.where(tri1 & (n == 0), MASK_VALUE, s)
                    s_tiles.append(s)
                    dp_tiles.append(_fold(_dot_nt(dop, vcat[b][pair // 2][e]), tri1, True))
                    lse_cols.append(jnp.sum(jnp.where(lane == 2 * pair + e, lse_tile, 0.0), axis=-1, keepdims=True))
        lse_c = jnp.concatenate(lse_cols, axis=0)
        p = jnp.exp(jnp.concatenate(s_tiles, axis=0) - lse_c)
        dp = jnp.concatenate(dp_tiles, axis=0)
        delta = jnp.sum(p * dp, axis=-1, keepdims=True)
        ds2 = _unfold(p * (dp - delta), _tri(R), True)
        for b in range(nbs):
            for pair in range(N_ATTN_HEADS // 2):
                r0 = (b * N_ATTN_HEADS + 2 * pair) * BLOCK
                acc = (_dot(ds2[r0:r0 + BLOCK, :], kcat[b][pair // 2][0])
                       + _dot(ds2[r0 + BLOCK:r0 + 2 * BLOCK, :], kcat[b][pair // 2][1]))
                dq_ref[b * BLOCK:(b + 1) * BLOCK, pair * 128:(pair + 1) * 128] = (acc * ATTN_SCALE).astype(BF)
        for b, t in enumerate(_head_cols(delta, nbs)):
            deltat_ref[:, b * BLOCK:(b + 1) * BLOCK] = t.T[:N_ATTN_HEADS, :]
        sink = jnp.concatenate([jnp.full((BLOCK, 1), sink_ref[head], F32)
                                for _ in range(nbs) for head in range(N_ATTN_HEADS)], axis=0)
        ds_sink = -jnp.exp(sink - lse_c) * delta
        row8 = lax.broadcasted_iota(jnp.int32, (N_ATTN_HEADS, BLOCK), 0)
        dsink = jnp.zeros((N_ATTN_HEADS, BLOCK), F32)
        for b in range(nbs):
            for head in range(N_ATTN_HEADS):
                r0 = (b * N_ATTN_HEADS + head) * BLOCK
                dsink = dsink + jnp.where(row8 == head, jnp.sum(ds_sink[r0:r0 + BLOCK, :], axis=0, keepdims=True), 0.0)
        dsink_ref[...] += dsink
        pl.when(n == steps - 1)(ex.wait)

    cur = lambda w: pl.BlockSpec((BLOCK * nbs, w), lambda n: (n, 0))
    prev = lambda w: pl.BlockSpec((BLOCK, w), lambda n: (jnp.maximum(n * nbs - 1, 0), 0))
    outs = _pcall(
        body, name="attn_bwd_dq", grid=(steps,),
        in_specs=[_smem_full(), cur(ATTN_W), cur(KV_W), prev(KV_W), cur(KV_W), prev(KV_W), cur(ATTN_W), cur(BLOCK)]
        + [ANY_SPEC] * ns,
        out_specs=[cur(ATTN_W), pl.BlockSpec((N_ATTN_HEADS, BLOCK * nbs), lambda n: (0, n)),
                   _const((N_ATTN_HEADS, BLOCK))] + [ANY_SPEC] * ns,
        out_shape=[jax.ShapeDtypeStruct((T, ATTN_W), BF), jax.ShapeDtypeStruct((N_ATTN_HEADS, T), F32),
                   jax.ShapeDtypeStruct((N_ATTN_HEADS, BLOCK), F32)] + _exchange_shapes([], swaps),
        scratch_shapes=_Exchange.scratch(ns),
        compiler_params=_params(("arbitrary",)),
    )(sinks, qa, ka, ka, va, va, da, lse, *swaps)
    return outs[:3], outs[3:]


def _attn_bwd_dkv(qa, ka, va, da, lse_t, delta_t, nbs):
    T = qa.shape[0]
    nb = T // BLOCK
    steps = nb // nbs
    R = nbs * N_ATTN_HEADS * BLOCK

    def body(qc_ref, qn_ref, dac_ref, dan_ref, k_ref, v_ref, lc_ref, ln_ref, dc_ref, dn_ref, dk_ref, dv_ref):
        n = pl.program_id(0)
        tri1 = _tri(BLOCK, True)
        lo = lax.broadcasted_iota(jnp.int32, (BLOCK, 128), 1) < HEAD_DIM
        kv = [_kv_variants(k_ref[b * BLOCK:(b + 1) * BLOCK, :]) for b in range(nbs)]
        vv = [_kv_variants(v_ref[b * BLOCK:(b + 1) * BLOCK, :]) for b in range(nbs)]
        qcat, docat = [], []
        s_tiles, dp_tiles, lse_tiles, delta_tiles = [], [], [], []
        for b in range(nbs):
            rows = slice(b * BLOCK, (b + 1) * BLOCK)
            nrows = slice((b + 1) * BLOCK, (b + 2) * BLOCK)
            inside = b < nbs - 1
            for pair in range(N_ATTN_HEADS // 2):
                ps = slice(pair * 128, (pair + 1) * 128)
                q2 = _scaled(jnp.concatenate([qc_ref[rows, ps], qc_ref[nrows, ps] if inside else qn_ref[:, ps]], axis=0))
                do2 = jnp.concatenate([dac_ref[rows, ps], dac_ref[nrows, ps] if inside else dan_ref[:, ps]],
                                      axis=0).astype(BF)
                qcat.append(q2)
                docat.append(do2)
                for e in range(2):
                    one = pl.ds(2 * pair + e, 1)
                    s = _fold(_dot_nt(kv[b][pair // 2][e], q2), tri1, False)
                    if not inside:
                        s = jnp.where(tri1 & (n == steps - 1), MASK_VALUE, s)
                    s_tiles.append(s)
                    dp_tiles.append(_fold(_dot_nt(vv[b][pair // 2][e], do2), tri1, False))
                    lse_tiles.append(jnp.where(tri1, lc_ref[one, nrows] if inside else ln_ref[one, :], lc_ref[one, rows]))
                    delta_tiles.append(jnp.where(tri1, dc_ref[one, nrows] if inside else dn_ref[one, :],
                                                 dc_ref[one, rows]))
        pt = jnp.exp(jnp.concatenate(s_tiles, axis=0) - jnp.concatenate(lse_tiles, axis=0))
        dst = pt * (jnp.concatenate(dp_tiles, axis=0) - jnp.concatenate(delta_tiles, axis=0))
        tri = _tri(R, True)
        pt2 = _unfold(pt, tri, False)
        dst2 = _unfold(dst, tri, False)
        for b in range(nbs):
            dk = jnp.zeros((BLOCK, 128), F32)
            dv = jnp.zeros((BLOCK, 128), F32)
            for pair in range(N_ATTN_HEADS // 2):
                h = pair // 2
                for e in range(2):
                    r0 = (b * N_ATTN_HEADS + 2 * pair + e) * BLOCK
                    half = lo if e == 0 else jnp.logical_not(lo)
                    dv_e = jnp.where(half, _dot(pt2[r0:r0 + BLOCK, :], docat[b * 4 + pair]), 0.0)
                    dk_e = jnp.where(half, _dot(dst2[r0:r0 + BLOCK, :], qcat[b * 4 + pair]), 0.0)
                    if e != h:
                        dv_e = pltpu.roll(dv_e, HEAD_DIM, 1)
                        dk_e = pltpu.roll(dk_e, HEAD_DIM, 1)
                    dv = dv + dv_e
                    dk = dk + dk_e
            dk_ref[b * BLOCK:(b + 1) * BLOCK, :] = dk.astype(BF)
            dv_ref[b * BLOCK:(b + 1) * BLOCK, :] = dv.astype(BF)

    cur = lambda w: pl.BlockSpec((BLOCK * nbs, w), lambda n: (n, 0))
    nxt = lambda w: pl.BlockSpec((BLOCK, w), lambda n: (jnp.minimum((n + 1) * nbs, nb - 1), 0))
    tcur = pl.BlockSpec((N_ATTN_HEADS, BLOCK * nbs), lambda n: (0, n))
    tnxt = pl.BlockSpec((N_ATTN_HEADS, BLOCK), lambda n: (0, jnp.minimum((n + 1) * nbs, nb - 1)))
    return _pcall(
        body, name="attn_bwd_dkv", grid=(steps,),
        in_specs=[cur(ATTN_W), nxt(ATTN_W), cur(ATTN_W), nxt(ATTN_W), cur(KV_W), cur(KV_W), tcur, tnxt, tcur, tnxt],
        out_specs=[cur(KV_W), cur(KV_W)],
        out_shape=[jax.ShapeDtypeStruct((T, KV_W), BF), jax.ShapeDtypeStruct((T, KV_W), BF)],
        compiler_params=_params(("parallel",)),
    )(qa, qa, da, da, ka, va, lse_t, lse_t, delta_t, delta_t)


def _in_proj_bwd(dqa, dka, dva, dret, w_in, x, g1, dx1, tm):
    T = x.shape[0]

    def body(dqa_ref, dka_ref, dva_ref, dret_ref, w_ref, x_ref, g_ref, dx1_ref, dx_ref, dg1_ref):
        @pl.when(pl.program_id(0) == 0)
        def _():
            dg1_ref[...] = jnp.zeros_like(dg1_ref)

        dh = (_dot(dqa_ref[...], w_ref[QA0:QA0 + ATTN_W, :]) + _dot(dka_ref[...], w_ref[KA0:KA0 + KV_W, :])
              + _dot(dva_ref[...], w_ref[VA0:VA0 + KV_W, :]) + _dot(dret_ref[...], w_ref[QR0:IN_W, :]))
        r, n = _rms_stats(x_ref[...])
        dg1_ref[...] += jnp.sum(dh * n, axis=0, keepdims=True)
        dx_ref[...] = dx1_ref[...] + _rms_bwd(n, r, dh * g_ref[...])

    return _pcall(
        body, name="in_proj_bwd", grid=(T // tm,),
        in_specs=[_rows(tm, ATTN_W), _rows(tm, KV_W), _rows(tm, KV_W), _rows(tm, 4 * RET_W), _vmem_full(),
                  _rows(tm, D_MODEL), _const((1, D_MODEL)), _rows(tm, D_MODEL)],
        out_specs=[_rows(tm, D_MODEL), _const((1, D_MODEL))],
        out_shape=[jax.ShapeDtypeStruct((T, D_MODEL), F32), jax.ShapeDtypeStruct((1, D_MODEL), F32)],
        compiler_params=_params(("arbitrary",)),
    )(dqa, dka, dva, dret, w_in, x, g1, dx1)


def _wgrad(a_list, b_list, tk, name):
    T = a_list[0].shape[0]
    na, nbb = len(a_list), len(b_list)
    m_sizes = [a.shape[1] for a in a_list]
    n_sizes = [b.shape[1] for b in b_list]
    M, N = sum(m_sizes), sum(n_sizes)
    nk = T // tk
    chunk = 512

    def body(*refs):
        a_refs, b_refs = refs[:na], refs[na:na + nbb]
        out_ref, acc = refs[na + nbb], refs[na + nbb + 1]
        k = pl.program_id(0)

        @pl.when(k == 0)
        def _():
            acc[...] = jnp.zeros_like(acc)

        r0 = 0
        for ai in range(na):
            a = a_refs[ai][...]
            c0 = 0
            for bi in range(nbb):
                for s in range(0, n_sizes[bi], chunk):
                    w = min(chunk, n_sizes[bi] - s)
                    acc[r0:r0 + m_sizes[ai], c0 + s:c0 + s + w] += _dot_tn(a, b_refs[bi][:, s:s + w])
                c0 += n_sizes[bi]
            r0 += m_sizes[ai]

        @pl.when(k == nk - 1)
        def _():
            pltpu.sync_copy(acc, out_ref)

    return _pcall(
        body, name=name, grid=(nk,),
        in_specs=[_rows(tk, w) for w in m_sizes + n_sizes],
        out_specs=pl.BlockSpec(memory_space=pl.ANY),
        out_shape=jax.ShapeDtypeStruct((M, N), F32),
        scratch_shapes=[pltpu.VMEM((M, N), F32)],
        compiler_params=_params(("arbitrary",)),
    )(*a_list, *b_list)


def _adamw_math(w, g, m, v):
    m = ADAM_B1 * m + (1.0 - ADAM_B1) * g
    v = ADAM_B2 * v + (1.0 - ADAM_B2) * (g * g)
    m_hat = m / (1.0 - ADAM_B1 ** ADAM_STEP)
    v_hat = v / (1.0 - ADAM_B2 ** ADAM_STEP)
    delta = -ADAM_LR * (m_hat / (jnp.sqrt(v_hat) + ADAM_EPS) + ADAM_WD * w)
    return delta, m, v


def _sum_parts(parts_ref):
    g = parts_ref[0].astype(F32)
    for i in range(1, N_DEV):
        g = g + parts_ref[i].astype(F32)
    return g


def _adamw_shard(parts, w, m, v, tr, name):
    R, C = w.shape

    def body(p_ref, w_ref, m_ref, v_ref, g_ref, d_ref, nm_ref, nv_ref):
        g = _sum_parts(p_ref)
        g_ref[...] = g
        d_ref[...], nm_ref[...], nv_ref[...] = _adamw_math(w_ref[...], g, m_ref[...], v_ref[...])

    blk = pl.BlockSpec((tr, C), lambda i: (i, 0))
    return _pcall(
        body, name=name, grid=(R // tr,),
        in_specs=[pl.BlockSpec((N_DEV, tr, C), lambda i: (0, i, 0)), blk, blk, blk],
        out_specs=[blk] * 4,
        out_shape=[jax.ShapeDtypeStruct((R, C), F32)] * 4,
        compiler_params=_params(("parallel",)),
    )(parts, w, m, v)


def _sum_small(parts):
    def body(p_ref, g_ref):
        g_ref[...] = _sum_parts(p_ref)

    return _pcall(body, name="sum_small", out_shape=jax.ShapeDtypeStruct(parts.shape[1:], F32),
                  in_specs=[_vmem_full()], out_specs=_vmem_full())(parts)


def _adamw_small(g, w, m, v, name):
    def body(g_ref, w_ref, m_ref, v_ref, d_ref, nm_ref, nv_ref):
        d_ref[...], nm_ref[...], nv_ref[...] = _adamw_math(w_ref[...], g_ref[...], m_ref[...], v_ref[...])

    return _pcall(body, name=name, out_shape=[jax.ShapeDtypeStruct(w.shape, F32)] * 3,
                  in_specs=[_vmem_full()] * 4, out_specs=[_vmem_full()] * 3)(g, w, m, v)


def _tables(T):
    h, c = N_RET_HEADS, BLOCK
    pos = jnp.arange(T, dtype=F32)
    angle = 1.0 / jnp.power(10000.0, jnp.linspace(0.0, 1.0, RET_HEAD_DIM // 2, dtype=F32))
    angle = jnp.repeat(angle, 2)
    sin = jnp.sin(pos[:, None] * angle[None])
    cos = jnp.cos(pos[:, None] * angle[None])
    even = (jnp.arange(RET_HEAD_DIM) % 2 == 0)[None, :]
    sin_s = jnp.where(even, -sin, sin)
    log_gamma = jnp.log(1.0 - jnp.power(2.0, -5.0 - jnp.arange(h, dtype=F32)))
    idx = jnp.arange(c, dtype=F32)
    rel = idx[:, None] - idx[None, :]
    d_intra = jnp.where(rel[None] >= 0, jnp.exp(log_gamma[:, None, None] * jnp.maximum(rel, 0.0)[None]), 0.0)
    xi = jnp.exp(log_gamma[None, :] * (idx[:, None] + 1.0))
    zeta = jnp.exp(log_gamma[None, :] * (c - 1.0 - idx[:, None]))
    decay = jnp.exp(log_gamma * c)
    xi_b = jnp.broadcast_to(xi.T[:, :, None], (h, c, RET_HEAD_DIM))
    zeta_b = jnp.broadcast_to(zeta.T[:, :, None], (h, c, RET_HEAD_DIM))
    return cos, sin_s, d_intra, jnp.swapaxes(d_intra, 1, 2), xi_b, zeta_b, decay


def _from_shards(sh):
    n, r, cols = sh.shape
    return jnp.swapaxes(sh, 0, 1).reshape(r, n * cols)


SMALL_ROWS = 216


def _pack_small(gains, conv_b, conv_w, sinks, scalar=None):
    last = jnp.concatenate([sinks.reshape(1, 8), jnp.zeros((1, 1), F32) if scalar is None else scalar.reshape(1, 1)],
                           axis=1)
    parts = [g.reshape(8, 128) for g in gains] + [conv_b.reshape(44, 128), conv_w.reshape(132, 128),
                                                  jnp.pad(last, ((0, 0), (0, 119)))]
    packed = jnp.concatenate(parts, axis=0)
    return jnp.pad(packed, ((0, SMALL_ROWS - packed.shape[0]), (0, 0)))


def kernel(x, mix_pre_norm, w_in, attn_sinks, w_out, mix_post_norm, ffn_pre_norm, w_up, conv_w, conv_b, w_down, ffn_post_norm, loss_target, m_mix_pre_norm, m_w_in, m_attn_sinks, m_w_out, m_mix_post_norm, m_ffn_pre_norm, m_w_up, m_conv_w, m_conv_b, m_w_down, m_ffn_post_norm, v_mix_pre_norm, v_w_in, v_attn_sinks, v_w_out, v_mix_post_norm, v_ffn_pre_norm, v_w_up, v_conv_w, v_conv_b, v_w_down, v_ffn_post_norm):
    T = x.shape[1]
    tm = min(512, T)
    tm_big = min(1024, T)
    tk_grad = min(2048, T)
    nbs = min(8, T // BLOCK)
    x2 = x.reshape(T, D_MODEL)
    target = loss_target.reshape(T, D_MODEL)
    me = 4 * lax.axis_index("x") + 2 * lax.axis_index("y") + lax.axis_index("c")

    g_in = _sequencer_gather(jnp.swapaxes(w_in, 1, 2)[0].astype(BF), "gather_w_in")
    w_in_f = g_in.reshape(IN_W, D_MODEL)
    cos, sin_s, d_intra, d_intra_t, xi_b, zeta_b, decay = _tables(T)
    sinks = attn_sinks.reshape(N_ATTN_HEADS)

    (h1, qa, ka, va, qr, kr, vr, gr), (w_up8, g_cw) = _in_proj(
        x2, mix_pre_norm, w_in_f, cos, sin_s, tm_big, [w_up[0].astype(BF), conv_w[0]])
    (a, lse, lse_t), (g_down, g_out) = _attn_fwd(sinks, qa, ka, va, nbs,
                                                 [w_down[0].astype(BF), w_out[0].astype(BF)])
    w_out_f = g_out.reshape(D_MODEL, D_MODEL)
    o, states, r = _ret_fwd(decay, qr, kr, vr, gr, d_intra, xi_b, zeta_b, nbs)
    mixed, x1, h2 = _out_proj(a, r, w_out_f, x2, mix_post_norm, ffn_pre_norm, tm_big)
    w_down4 = g_down.reshape(FF_PAIRS, FF_SHARD, D_MODEL)
    up_g, up_v, u_g, u_v, y4, dout, dz, dg4, loss_acc = _ffn_fwd(
        h2, w_up8, g_cw, conv_b.reshape(N_DEV, 1, FF_SHARD), w_down4, x1, ffn_post_norm, target, tm)

    dup_g, dup_v, dcb_g, dcb_v, dcw_g, dcw_v, gwu_g, gwu_v, gw_down4 = _ffn_bwd_a(
        dz, h2, w_down4, u_g, u_v, up_g, up_v, y4, g_cw, tm_big)
    dcb = jnp.concatenate([dcb_g, dcb_v], axis=0).reshape(1, 2 * D_FF)
    dcw = _from_shards(jnp.concatenate([dcw_g, dcw_v], axis=0))
    gw_down = gw_down4.reshape(D_FF, D_MODEL)
    (dx1, dmixed, da, dr, dg3, dg2), p_up = _ffn_bwd_b(
        dup_g, dup_v, w_up8, x1, dout, ffn_pre_norm, mixed, mix_post_norm, w_out_f, tm, gwu_g, gwu_v)
    gw_out = _wgrad([a, r], [dmixed], tk_grad, "wgrad_out")
    dret, (p_down,) = _ret_bwd(decay, qr, kr, vr, gr, o, states, dr, d_intra, d_intra_t, xi_b, zeta_b, cos, sin_s,
                               [gw_down.reshape(N_DEV, D_FF // N_DEV, D_MODEL)], nbs)
    (dqa, delta_t, dsink), (p_out,) = _attn_bwd_dq(sinks, qa, ka, va, da, lse, nbs,
                                                   [gw_out.reshape(N_DEV, D_MODEL // N_DEV, D_MODEL)])
    dka, dva = _attn_bwd_dkv(qa, ka, va, da, lse_t, delta_t, nbs)
    grad_x, dg1 = _in_proj_bwd(dqa, dka, dva, dret, w_in_f, x2, mix_pre_norm, dx1, tm)
    gw_in = _wgrad([h1], [dqa, dka, dva, dret], tk_grad, "wgrad_in")

    small = _pack_small([dg1, dg2, dg3, dg4], dcb, dcw, dsink[:, 0], loss_acc[0, 0])
    gw_in_t = gw_in.T.reshape(N_DEV, IN_W // N_DEV, D_MODEL).astype(BF)
    small_all, p_in = _exchange_call([small], [gw_in_t], "exchange_last")
    g_small = _sum_small(small_all)
    loss = g_small[208, N_ATTN_HEADS]

    t_in = lambda a: jnp.swapaxes(a, 1, 2)[0]
    g_w_in, d_w_in, nm_w_in, nv_w_in = [o.T for o in _adamw_shard(
        p_in, t_in(w_in), t_in(m_w_in), t_in(v_w_in), 176, "adamw_in")]
    g_w_up, d_w_up, nm_w_up, nv_w_up = [o.T for o in _adamw_shard(
        p_up, t_in(w_up), t_in(m_w_up), t_in(v_w_up), 176, "adamw_up")]
    g_w_out, d_w_out, nm_w_out, nv_w_out = _adamw_shard(p_out, w_out[0], m_w_out[0], v_w_out[0], 128, "adamw_out")
    g_w_down, d_w_down, nm_w_down, nv_w_down = _adamw_shard(p_down, w_down[0], m_w_down[0], v_w_down[0], 176,
                                                            "adamw_down")
    gains = [mix_pre_norm, mix_post_norm, ffn_pre_norm, ffn_post_norm]
    m_gains = [m_mix_pre_norm, m_mix_post_norm, m_ffn_pre_norm, m_ffn_post_norm]
    v_gains = [v_mix_pre_norm, v_mix_post_norm, v_ffn_pre_norm, v_ffn_post_norm]
    zeros_cw = jnp.zeros((3, 2 * D_FF), F32)
    w_small = _pack_small(gains, conv_b, zeros_cw, attn_sinks)
    m_small = _pack_small(m_gains, m_conv_b, zeros_cw, m_attn_sinks)
    v_small = _pack_small(v_gains, v_conv_b, zeros_cw, v_attn_sinks)
    d_small, nm_small, nv_small = _adamw_small(g_small, w_small, m_small, v_small, "adamw_small")
    shard_cols = 2 * D_FF // N_DEV
    g_cw = lax.dynamic_slice(g_small[76:208].reshape(3, 2 * D_FF), (0, me * shard_cols), (3, shard_cols))
    d_cw, nm_cw, nv_cw = _adamw_small(g_cw, conv_w[0], m_conv_w[0], v_conv_w[0], "adamw_conv_w")

    def unpack(p):
        gains_o = [p[8 * i:8 * i + 8].reshape(1, D_MODEL) for i in range(4)]
        return gains_o, p[32:76].reshape(1, 2 * D_FF), p[208:209, :N_ATTN_HEADS]

    def leaves(p, w_in_s, w_out_s, w_up_s, cw_s, w_down_s):
        (pre1, post1, pre2, post2), cb, sk = unpack(p)
        return [pre1, w_in_s[None], sk, w_out_s[None], post1, pre2, w_up_s[None], cw_s[None], cb, w_down_s[None],
                post2]

    return (loss, grad_x.reshape(1, T, D_MODEL),
            *leaves(g_small, g_w_in, g_w_out, g_w_up, g_cw, g_w_down),
            *leaves(d_small, d_w_in, d_w_out, d_w_up, d_cw, d_w_down),
            *leaves(nm_small, nm_w_in, nm_w_out, nm_w_up, nm_cw, nm_w_down),
            *leaves(nv_small, nv_w_in, nv_w_out, nv_w_up, nv_cw, nv_w_down))
```

```python
import functools
import math

import jax
import jax.numpy as jnp
from jax import lax
from jax.experimental import pallas as pl
from jax.experimental.pallas import tpu as pltpu
from jax.experimental.pallas import tpu_sc as plsc

F32 = jnp.float32
BF = jnp.bfloat16

N_DEV = 8
D_MODEL = 1024
HEAD_DIM = 64
ATTN_W = 512
N_ATTN_HEADS = 8
KV_W = 128
BLOCK = 128
RET_W = 512
N_RET_HEADS = 4
RET_HEAD_DIM = 128
IN_W = 2816
D_FF = 2816
RMS_EPS = 1e-6
GN_EPS = 1e-6
MASK_VALUE = -1e30
ATTN_SCALE = HEAD_DIM ** -0.5
RET_K_SCALE = RET_HEAD_DIM ** -0.5
GELU_C = math.sqrt(2.0 / math.pi)
GELU_A = 0.044715

ADAM_LR = 0.001
ADAM_B1 = 0.9
ADAM_B2 = 0.999
ADAM_EPS = 1e-08
ADAM_WD = 0.01
ADAM_STEP = 10

VMEM_LIMIT_BYTES = 56 * 1024 * 1024
FF_SHARD = 2 * D_FF // N_DEV
FF_PAIRS = N_DEV // 2

QA0, KA0, VA0, QR0, KR0, VR0, GR0 = 0, 512, 640, 768, 1280, 1792, 2304

MESH_ID = pl.DeviceIdType.MESH


def _pcall(body, **kw):
    return pl.pallas_call(body, **kw)


def _params(sem=None):
    return pltpu.CompilerParams(dimension_semantics=sem, vmem_limit_bytes=VMEM_LIMIT_BYTES)


def _dot(a, b):
    return jnp.dot(a, b, preferred_element_type=F32)


def _dot_nt(a, b):
    return lax.dot_general(a, b, (((1,), (1,)), ((), ())), preferred_element_type=F32)


def _dot_tn(a, b):
    return lax.dot_general(a, b, (((0,), (0,)), ((), ())), preferred_element_type=F32)


def _vmem_full():
    return pl.BlockSpec(memory_space=pltpu.VMEM)


def _smem_full():
    return pl.BlockSpec(memory_space=pltpu.SMEM)


def _rows(tm, w):
    return pl.BlockSpec((tm, w), lambda i: (i, 0))


def _const(shape):
    return pl.BlockSpec(shape, lambda i: tuple(0 for _ in shape))


def _rms_stats(x):
    r = lax.rsqrt(jnp.mean(x * x, axis=-1, keepdims=True) + RMS_EPS)
    return r, x * r


def _rms_bwd(n, r, dn):
    return r * (dn - n * jnp.mean(dn * n, axis=-1, keepdims=True))


def _rot(x, even):
    w = x.shape[1]
    return jnp.where(even, pltpu.roll(x, w - 1, 1), pltpu.roll(x, 1, 1))


def _peers():
    x, y, c = lax.axis_index("x"), lax.axis_index("y"), lax.axis_index("c")
    flips = [(0, 0, 1), (1, 0, 0), (0, 1, 0), (1, 1, 0), (1, 0, 1), (0, 1, 1), (1, 1, 1)]
    peers = [(x ^ fx, y ^ fy, c ^ fc) for fx, fy, fc in flips]
    return 4 * x + 2 * y + c, peers


SAME_CORE_PEERS = 4


class _Exchange:
    def __init__(self, gathers, swaps, send_sems, recv_sems, local_sems):
        self.me, self.peers = _peers()
        self.slots = [4 * px + 2 * py + pc for px, py, pc in self.peers]
        self.pairs = [(src, dst, True) for src, dst in gathers] + [(src, dst, False) for src, dst in swaps]
        self.send_sems, self.recv_sems, self.local_sems = send_sems, recv_sems, local_sems

    @staticmethod
    def scratch(n):
        return [pltpu.SemaphoreType.DMA((n, N_DEV - 1)), pltpu.SemaphoreType.DMA((n, N_DEV - 1)),
                pltpu.SemaphoreType.DMA((n,))]

    def _parts(self, a, slot):
        src, _, whole = self.pairs[a]
        half = N_DEV // 2
        if whole:
            return [(None, src)]
        if isinstance(src, tuple):
            return [(slot < half, src[0].at[jnp.minimum(slot, half - 1)]),
                    (slot >= half, src[1].at[jnp.maximum(slot - half, 0)])]
        return [(None, src.at[slot])]

    def _local(self, a, src):
        return pltpu.make_async_copy(src, self.pairs[a][1].at[self.me], self.local_sems.at[a])

    def _remote(self, a, k, src, slot):
        return pltpu.make_async_remote_copy(
            src_ref=src, dst_ref=self.pairs[a][1].at[slot], send_sem=self.send_sems.at[a, k],
            recv_sem=self.recv_sems.at[a, k], device_id=self.peers[k], device_id_type=MESH_ID)

    def start(self):
        def go(cond, copy):
            if cond is None:
                copy.start()
            else:
                pl.when(cond)(copy.start)

        for a in range(len(self.pairs)):
            for cond, src in self._parts(a, self.me):
                go(cond, self._local(a, src))
            for k in range(SAME_CORE_PEERS if self.pairs[a][2] else N_DEV - 1):
                for cond, src in self._parts(a, self.slots[k]):
                    go(cond, self._remote(a, k, src, self.me))

    def _pass_on(self, a, j):
        k = j + SAME_CORE_PEERS - 1
        block = self.pairs[a][1].at[self.slots[j]]
        return pltpu.make_async_remote_copy(
            src_ref=block, dst_ref=block, send_sem=self.send_sems.at[a, k], recv_sem=self.recv_sems.at[a, k],
            device_id=self.peers[0], device_id_type=MESH_ID)

    def wait(self):
        for a in range(len(self.pairs)):
            src = self._parts(a, self.me)[0][1]
            if self.pairs[a][2]:
                for j in range(1, SAME_CORE_PEERS):
                    self._remote(a, j, src, self.slots[j]).wait_recv()
                    self._pass_on(a, j).start()
                self._remote(a, 0, src, self.slots[0]).wait_recv()
            for k in range(SAME_CORE_PEERS if self.pairs[a][2] else 0, N_DEV - 1):
                self._remote(a, k, src, self.slots[k]).wait_recv()
        for a in range(len(self.pairs)):
            src = self._parts(a, self.me)[0][1]
            for k in range(SAME_CORE_PEERS if self.pairs[a][2] else N_DEV - 1):
                self._remote(a, k, src, self.me).wait_send()
            if self.pairs[a][2]:
                for j in range(1, SAME_CORE_PEERS):
                    self._pass_on(a, j).wait_send()
            self._local(a, src).wait()


def _sequencer_gather(v, name):
    src = jax.new_ref(v, memory_space=pltpu.MemorySpace.HBM)
    dst = jax.empty_ref(jax.ShapeDtypeStruct((N_DEV,) + v.shape, v.dtype), memory_space=pltpu.MemorySpace.HBM)

    @pl.kernel(mesh=plsc.ScalarSubcoreMesh(axis_name="seq", num_cores=1), name=name,
               scratch_types=(pltpu.SemaphoreType.DMA((N_DEV - 1,)), pltpu.SemaphoreType.DMA((N_DEV - 1,)),
                              pltpu.SemaphoreType.DMA),
               compiler_params=pltpu.CompilerParams(collective_id=0))
    def launch(send_sems, recv_sems, local_sem):
        me, peers = _peers()
        slots = [4 * px + 2 * py + pc for px, py, pc in peers]
        barrier = pltpu.get_barrier_semaphore()
        for peer in peers:
            pl.semaphore_signal(barrier, inc=1, device_id=peer, device_id_type=MESH_ID)
        pl.semaphore_wait(barrier, N_DEV - 1)

        def remote(k, slot):
            return pltpu.make_async_remote_copy(
                src_ref=src, dst_ref=dst.at[slot], send_sem=send_sems.at[k], recv_sem=recv_sems.at[k],
                device_id=peers[k], device_id_type=MESH_ID)

        local = pltpu.make_async_copy(src, dst.at[me], local_sem)
        local.start()
        for k in range(N_DEV - 1):
            remote(k, me).start()
        for k in range(N_DEV - 1):
            remote(k, slots[k]).wait_recv()
        for k in range(N_DEV - 1):
            remote(k, me).wait_send()
        local.wait()

    launch()
    return dst[...]


def _sequencer_last_exchange(small, shards, name):
    s_src = jax.new_ref(small, memory_space=pltpu.MemorySpace.HBM)
    s_dst = jax.empty_ref(jax.ShapeDtypeStruct((N_DEV,) + small.shape, small.dtype), memory_space=pltpu.MemorySpace.HBM)
    g_src = jax.new_ref(shards, memory_space=pltpu.MemorySpace.HBM)
    g_dst = jax.empty_ref(jax.ShapeDtypeStruct(shards.shape, shards.dtype), memory_space=pltpu.MemorySpace.HBM)

    @pl.kernel(mesh=plsc.ScalarSubcoreMesh(axis_name="seq", num_cores=1), name=name,
               scratch_types=(pltpu.SemaphoreType.DMA((2, N_DEV - 1)), pltpu.SemaphoreType.DMA((2, N_DEV - 1)),
                              pltpu.SemaphoreType.DMA((2,))),
               compiler_params=pltpu.CompilerParams(collective_id=1))
    def launch(send_sems, recv_sems, local_sems):
        me, peers = _peers()
        slots = [4 * px + 2 * py + pc for px, py, pc in peers]
        barrier = pltpu.get_barrier_semaphore()
        for peer in peers:
            pl.semaphore_signal(barrier, inc=1, device_id=peer, device_id_type=MESH_ID)
        pl.semaphore_wait(barrier, N_DEV - 1)

        def remote(a, k, src_slot, dst_slot):
            src = s_src if a == 0 else g_src.at[src_slot]
            dst = (s_dst if a == 0 else g_dst).at[dst_slot]
            return pltpu.make_async_remote_copy(
                src_ref=src, dst_ref=dst, send_sem=send_sems.at[a, k], recv_sem=recv_sems.at[a, k],
                device_id=peers[k], device_id_type=MESH_ID)

        local = [pltpu.make_async_copy(s_src, s_dst.at[me], local_sems.at[0]),
                 pltpu.make_async_copy(g_src.at[me], g_dst.at[me], local_sems.at[1])]
        for cp in local:
            cp.start()
        for a in range(2):
            for k in range(N_DEV - 1):
                remote(a, k, slots[k], me).start()
        for a in range(2):
            for k in range(N_DEV - 1):
                remote(a, k, me, slots[k]).wait_recv()
        for a in range(2):
            for k in range(N_DEV - 1):
                remote(a, k, slots[k], me).wait_send()
        for cp in local:
            cp.wait()

    launch()
    return s_dst[...], g_dst[...]


ANY_SPEC = pl.BlockSpec(memory_space=pl.ANY)


def _exchange_shapes(gathers, swaps):
    return ([jax.ShapeDtypeStruct((N_DEV,) + a.shape, a.dtype) for a in gathers]
            + [jax.ShapeDtypeStruct(a.shape, a.dtype) for a in swaps])


def _exchange_of(ins, outs, sems, ng):
    return _Exchange(list(zip(ins[:ng], outs[:ng])), list(zip(ins[ng:], outs[ng:])), *sems)


def _exchange_call(gathers, swaps, name):
    ng, ns = len(gathers), len(swaps)
    n = ng + ns

    def body(*refs):
        ex = _exchange_of(refs[:n], refs[n:2 * n], refs[2 * n:], ng)
        ex.start()
        ex.wait()

    return _pcall(
        body, name=name, out_shape=_exchange_shapes(gathers, swaps),
        in_specs=[ANY_SPEC] * (ng + ns), out_specs=[ANY_SPEC] * (ng + ns),
        scratch_shapes=_Exchange.scratch(ng + ns),
    )(*gathers, *swaps)


def _in_proj(x, g1, w_in, cos, sin_s, tm, gathers):
    T = x.shape[0]
    ng = len(gathers)
    nt = T // tm

    def body(x_ref, g_ref, w_ref, cos_ref, sin_ref, *rest):
        ex = _exchange_of(rest[:ng], rest[ng + 8:2 * ng + 8], rest[2 * ng + 8:], ng)
        h_ref, qa_ref, ka_ref, va_ref, qr_ref, kr_ref, vr_ref, gr_ref = rest[ng:ng + 8]
        pl.when(pl.program_id(0) == 0)(ex.start)
        r, n = _rms_stats(x_ref[...])
        h = (n * g_ref[...]).astype(BF)
        h_ref[...] = h

        def proj(c0, w):
            return _dot_nt(h, w_ref[c0:c0 + w, :])

        qa_ref[...] = proj(QA0, ATTN_W).astype(BF)
        kva = proj(KA0, 2 * KV_W)
        ka_ref[...] = kva[:, :KV_W].astype(BF)
        va_ref[...] = kva[:, KV_W:].astype(BF)
        vr_ref[...] = proj(VR0, RET_W).astype(BF)
        gr_ref[...] = proj(GR0, RET_W)
        cos_t, sin_t = cos_ref[...], sin_ref[...]
        even = lax.broadcasted_iota(jnp.int32, (tm, RET_HEAD_DIM), 1) % 2 == 0
        for c0, scale, out_ref in ((QR0, None, qr_ref), (KR0, RET_K_SCALE, kr_ref)):
            full = proj(c0, RET_W)
            for hd in range(N_RET_HEADS):
                cs = slice(hd * RET_HEAD_DIM, (hd + 1) * RET_HEAD_DIM)
                t = full[:, cs] if scale is None else full[:, cs] * scale
                out_ref[:, cs] = (t * cos_t + _rot(t, even) * sin_t).astype(BF)
        pl.when(pl.program_id(0) == nt - 1)(ex.wait)

    widths = [D_MODEL, ATTN_W, KV_W, KV_W, RET_W, RET_W, RET_W, RET_W]
    dts = [BF] * 7 + [F32]
    outs = _pcall(
        body, name="in_proj", grid=(nt,),
        in_specs=[_rows(tm, D_MODEL), _const((1, D_MODEL)), _vmem_full(), _rows(tm, RET_HEAD_DIM),
                  _rows(tm, RET_HEAD_DIM)] + [ANY_SPEC] * ng,
        out_specs=[_rows(tm, w) for w in widths] + [ANY_SPEC] * ng,
        out_shape=[jax.ShapeDtypeStruct((T, w), dt) for w, dt in zip(widths, dts)] + _exchange_shapes(gathers, []),
        scratch_shapes=_Exchange.scratch(ng),
        compiler_params=_params(("arbitrary",)),
    )(x, g1, w_in, cos, sin_s, *gathers)
    return outs[:8], outs[8:]


def _kv_variants(kk):
    kf = kk.astype(F32)
    lo = lax.broadcasted_iota(jnp.int32, kf.shape, 1) < HEAD_DIM
    h0_lo = jnp.where(lo, kf, 0.0)
    h1_hi = jnp.where(lo, 0.0, kf)
    h0_hi = pltpu.roll(h0_lo, HEAD_DIM, 1)
    h1_lo = pltpu.roll(h1_hi, HEAD_DIM, 1)
    return [[h0_lo.astype(BF), h0_hi.astype(BF)], [h1_lo.astype(BF), h1_hi.astype(BF)]]


def _col_to_tile(tile, col, head):
    lane = lax.broadcasted_iota(jnp.int32, tile.shape, 1)
    return jnp.where(lane == head, col, tile)


def _tri(rows, key_major=False):
    i = lax.broadcasted_iota(jnp.int32, (rows, BLOCK), 0) & (BLOCK - 1)
    j = lax.broadcasted_iota(jnp.int32, (rows, BLOCK), 1)
    return i > j if key_major else j > i


def _fold(x2, tri, first_above):
    a, b = x2[:, :BLOCK], x2[:, BLOCK:]
    return jnp.where(tri, a, b) if first_above else jnp.where(tri, b, a)


def _unfold(x, tri, first_above):
    up, low = jnp.where(tri, x, 0.0), jnp.where(tri, 0.0, x)
    return jnp.concatenate([up, low] if first_above else [low, up], axis=1).astype(BF)


def _scaled(q):
    return (q.astype(F32) * ATTN_SCALE).astype(BF)


def _cat_variants(prev, cur):
    return [[jnp.concatenate([prev[h][e], cur[h][e]], axis=0) for e in range(2)] for h in range(2)]


def _block_variants(prev_ref, cur_ref, nbs):
    var = [_kv_variants(prev_ref[...])] + [_kv_variants(cur_ref[b * BLOCK:(b + 1) * BLOCK, :]) for b in range(nbs)]
    return [_cat_variants(var[b], var[b + 1]) for b in range(nbs)]


def _head_cols(col, nbs):
    tiles = []
    for b in range(nbs):
        t = jnp.zeros((BLOCK, BLOCK), F32)
        for head in range(N_ATTN_HEADS):
            r0 = (b * N_ATTN_HEADS + head) * BLOCK
            t = _col_to_tile(t, col[r0:r0 + BLOCK, :], head)
        tiles.append(t)
    return tiles


def _attn_fwd(sinks, qa, ka, va, nbs, gathers):
    T = qa.shape[0]
    steps = T // (BLOCK * nbs)
    R = nbs * N_ATTN_HEADS * BLOCK
    ng = len(gathers)

    def body(sink_ref, q_ref, kc_ref, kp_ref, vc_ref, vp_ref, *rest):
        ex = _exchange_of(rest[:ng], rest[ng + 3:2 * ng + 3], rest[2 * ng + 3:], ng)
        a_ref, lse_ref, lset_ref = rest[ng:ng + 3]
        n = pl.program_id(0)
        pl.when(n == 0)(ex.start)
        kcat = _block_variants(kp_ref, kc_ref, nbs)
        vcat = _block_variants(vp_ref, vc_ref, nbs)
        tri1 = _tri(BLOCK)
        tiles = []
        for b in range(nbs):
            for pair in range(N_ATTN_HEADS // 2):
                qp = _scaled(q_ref[b * BLOCK:(b + 1) * BLOCK, pair * 128:(pair + 1) * 128])
                for e in range(2):
                    s = _fold(_dot_nt(qp, kcat[b][pair // 2][e]), tri1, True)
                    if b == 0:
                        s = jnp.where(tri1 & (n == 0), MASK_VALUE, s)
                    tiles.append(s)
        s = jnp.concatenate(tiles, axis=0)
        sink = jnp.concatenate([jnp.full((BLOCK, 1), sink_ref[head], F32)
                                for _ in range(nbs) for head in range(N_ATTN_HEADS)], axis=0)
        m = jnp.maximum(jnp.max(s, axis=-1, keepdims=True), sink)
        p = jnp.exp(s - m)
        z = jnp.sum(p, axis=-1, keepdims=True) + jnp.exp(sink - m)
        p2 = _unfold(p * (1.0 / z), _tri(R), True)
        for b in range(nbs):
            for pair in range(N_ATTN_HEADS // 2):
                r0 = (b * N_ATTN_HEADS + 2 * pair) * BLOCK
                acc = (_dot(p2[r0:r0 + BLOCK, :], vcat[b][pair // 2][0])
                       + _dot(p2[r0 + BLOCK:r0 + 2 * BLOCK, :], vcat[b][pair // 2][1]))
                a_ref[b * BLOCK:(b + 1) * BLOCK, pair * 128:(pair + 1) * 128] = acc.astype(BF)
        for b, t in enumerate(_head_cols(m + jnp.log(z), nbs)):
            lse_ref[b * BLOCK:(b + 1) * BLOCK, :] = t
            lset_ref[:, b * BLOCK:(b + 1) * BLOCK] = t.T[:N_ATTN_HEADS, :]
        pl.when(n == steps - 1)(ex.wait)

    cur = lambda w: pl.BlockSpec((BLOCK * nbs, w), lambda n: (n, 0))
    prev = lambda w: pl.BlockSpec((BLOCK, w), lambda n: (jnp.maximum(n * nbs - 1, 0), 0))
    outs = _pcall(
        body, name="attn_fwd", grid=(steps,),
        in_specs=[_smem_full(), cur(ATTN_W), cur(KV_W), prev(KV_W), cur(KV_W), prev(KV_W)] + [ANY_SPEC] * ng,
        out_specs=[cur(ATTN_W), cur(BLOCK), pl.BlockSpec((N_ATTN_HEADS, BLOCK * nbs), lambda n: (0, n))]
        + [ANY_SPEC] * ng,
        out_shape=[jax.ShapeDtypeStruct((T, ATTN_W), BF), jax.ShapeDtypeStruct((T, BLOCK), F32),
                   jax.ShapeDtypeStruct((N_ATTN_HEADS, T), F32)] + _exchange_shapes(gathers, []),
        scratch_shapes=_Exchange.scratch(ng),
        compiler_params=_params(("arbitrary",)),
    )(sinks, qa, ka, ka, va, va, *gathers)
    return outs[:3], outs[3:]


def _ret_fwd(decay, qr, kr, vr, gr, d_intra, xi_b, zeta_b, ncs):
    T = qr.shape[0]
    nc = T // BLOCK
    H, C = N_RET_HEADS, RET_HEAD_DIM

    def body(decay_ref, q_ref, k_ref, v_ref, g_ref, d_ref, xi_ref, zeta_ref, o_ref, s_ref, r_ref, state):
        @pl.when(pl.program_id(0) == 0)
        def _():
            state[...] = jnp.zeros_like(state)

        pairs = [(b, h) for b in range(ncs) for h in range(H)]
        sl = lambda b, h: (slice(b * BLOCK, (b + 1) * BLOCK), slice(h * C, (h + 1) * C))
        tab = lambda ref: jnp.concatenate([ref[h] for _, h in pairs], axis=0)
        q = [q_ref[sl(b, h)] for b, h in pairs]
        k = [k_ref[sl(b, h)] for b, h in pairs]
        v = [v_ref[sl(b, h)] for b, h in pairs]
        inner = (jnp.concatenate([_dot_nt(q[i], k[i]) for i in range(len(pairs))], axis=0) * tab(d_ref)).astype(BF)
        kz = (jnp.concatenate(k, axis=0).astype(F32) * tab(zeta_ref)).astype(BF)
        o1 = [_dot(inner[i * BLOCK:(i + 1) * BLOCK, :], v[i]) for i in range(len(pairs))]
        kv = [_dot_tn(kz[i * BLOCK:(i + 1) * BLOCK, :], v[i]) for i in range(len(pairs))]
        st_b = [None] * len(pairs)
        for h in range(H):
            st = state[h]
            for b in range(ncs):
                i = b * H + h
                st_b[i] = st.astype(BF)
                s_ref[b, h] = st_b[i]
                st = decay_ref[h] * st + kv[i]
            state[h] = st
        o2 = jnp.concatenate([_dot(q[i], st_b[i]) for i in range(len(pairs))], axis=0)
        o = jnp.concatenate(o1, axis=0) + o2 * tab(xi_ref)
        mu = jnp.mean(o, axis=-1, keepdims=True)
        oc = o - mu
        rs = lax.rsqrt(jnp.mean(oc * oc, axis=-1, keepdims=True) + GN_EPS)
        g = jnp.concatenate([g_ref[sl(b, h)] for b, h in pairs], axis=0)
        r = (g * jax.nn.sigmoid(g) * (oc * rs)).astype(BF)
        for i, (b, h) in enumerate(pairs):
            o_ref[sl(b, h)] = o[i * BLOCK:(i + 1) * BLOCK, :]
            r_ref[sl(b, h)] = r[i * BLOCK:(i + 1) * BLOCK, :]

    cur = pl.BlockSpec((BLOCK * ncs, RET_W), lambda n: (n, 0))
    tab = pl.BlockSpec((H, C, C), lambda n: (0, 0, 0))
    return _pcall(
        body, name="ret_fwd", grid=(nc // ncs,),
        in_specs=[_smem_full(), cur, cur, cur, cur, tab, tab, tab],
        out_specs=[cur, pl.BlockSpec((ncs, H, C, C), lambda n: (n, 0, 0, 0)), cur],
        out_shape=[jax.ShapeDtypeStruct((T, RET_W), F32), jax.ShapeDtypeStruct((nc, H, C, C), BF),
                   jax.ShapeDtypeStruct((T, RET_W), BF)],
        scratch_shapes=[pltpu.VMEM((H, C, C), F32)],
        compiler_params=_params(("arbitrary",)),
    )(decay, qr, kr, vr, gr, d_intra, xi_b, zeta_b)


def _out_proj(a, r, w_out, x, g2, g3, tm):
    T = x.shape[0]

    def body(a_ref, r_ref, w_ref, x_ref, g2_ref, g3_ref, mixed_ref, x1_ref, h2_ref):
        mixed = _dot(a_ref[...], w_ref[:ATTN_W, :]) + _dot(r_ref[...], w_ref[ATTN_W:, :])
        mixed_ref[...] = mixed
        _, n2 = _rms_stats(mixed)
        x1 = x_ref[...] + n2 * g2_ref[...]
        x1_ref[...] = x1
        _, n3 = _rms_stats(x1)
        h2_ref[...] = (n3 * g3_ref[...]).astype(BF)

    return _pcall(
        body, name="out_proj", grid=(T // tm,),
        in_specs=[_rows(tm, ATTN_W), _rows(tm, RET_W), _vmem_full(), _rows(tm, D_MODEL), _const((1, D_MODEL)),
                  _const((1, D_MODEL))],
        out_specs=[_rows(tm, D_MODEL)] * 3,
        out_shape=[jax.ShapeDtypeStruct((T, D_MODEL), F32), jax.ShapeDtypeStruct((T, D_MODEL), F32),
                   jax.ShapeDtypeStruct((T, D_MODEL), BF)],
        compiler_params=_params(("parallel",)),
    )(a, r, w_out, x, g2, g3)


def _shift_down(cur, k, before):
    out = pltpu.roll(cur, k, 0)
    row = lax.broadcasted_iota(jnp.int32, before.shape, 0)
    top = jnp.where(row < k, pltpu.roll(before, k, 0), out[0:8])
    return jnp.concatenate([top, out[8:]], axis=0)


def _shift_up(cur, k, after):
    tm = cur.shape[0]
    out = pltpu.roll(cur, tm - k, 0)
    row = lax.broadcasted_iota(jnp.int32, after.shape, 0)
    bot = jnp.where(row >= 8 - k, pltpu.roll(after, 8 - k, 0), out[tm - 8:])
    return jnp.concatenate([out[:tm - 8], bot], axis=0)


def _gelu_parts(x):
    m = (-2.0 * GELU_C * GELU_A) * (x * x)
    s = 1.0 / (1.0 + jnp.exp(x * (m - 2.0 * GELU_C)))
    gelu = x * s
    dgelu = s + gelu * (1.0 - s) * (2.0 * GELU_C - 3.0 * m)
    return gelu, dgelu


def _ffn_fwd(h2, w_up8, conv_w8, conv_b8, w_down4, x1, g4, target, tm):
    T = h2.shape[0]
    nt = T // tm

    def body(h_ref, wu_ref, cwg_ref, cwv_ref, cbg_ref, cbv_ref, wd_ref, x1_ref, g_ref, t_ref,
             upg_ref, upv_ref, ug_ref, uv_ref, y_ref, dout_ref, dz_ref, dg4_ref, loss_ref, halo, z_acc):
        s = pl.program_id(1)
        first = pl.program_id(0) == 0

        @pl.when(first & (s == 0))
        def _():
            loss_ref[...] = jnp.zeros_like(loss_ref)
            dg4_ref[...] = jnp.zeros_like(dg4_ref)

        h = h_ref[...]
        u = []
        parts = ((cwg_ref, cbg_ref, upg_ref, ug_ref), (cwv_ref, cbv_ref, upv_ref, uv_ref))
        for part, (cw_ref, cb_ref, up_ref, u_ref) in enumerate(parts):
            cur = _dot(h, wu_ref[s + part * FF_PAIRS])
            up_ref[0] = cur.astype(BF)
            before = jnp.where(first, 0.0, halo[part, s])
            halo[part, s] = cur[tm - 8:tm, :]
            u_c = (cw_ref[0, pl.ds(0, 1), :] * _shift_down(cur, 2, before)
                   + cw_ref[0, pl.ds(1, 1), :] * _shift_down(cur, 1, before)
                   + cw_ref[0, pl.ds(2, 1), :] * cur + cb_ref[0])
            u_ref[0] = u_c
            u.append(u_c)
        gelu, _ = _gelu_parts(u[0])
        y = (gelu * u[1]).astype(BF)
        y_ref[0] = y
        z_part = _dot(y, wd_ref[s])

        @pl.when(s == 0)
        def _():
            z_acc[...] = z_part

        @pl.when(s > 0)
        def _():
            z_acc[...] += z_part

        @pl.when(s == FF_PAIRS - 1)
        def _():
            r4, n4 = _rms_stats(z_acc[...])
            err = x1_ref[...] + n4 * g_ref[...] - t_ref[...]
            dout = err * (1.0 / D_MODEL)
            dout_ref[...] = dout
            loss_ref[...] += 0.5 * jnp.sum(jnp.mean(err * err, axis=-1, keepdims=True), axis=0, keepdims=True)
            dg4_ref[...] += jnp.sum(dout * n4, axis=0, keepdims=True)
            dz_ref[...] = _rms_bwd(n4, r4, dout * g_ref[...]).astype(BF)

    rows = pl.BlockSpec((tm, D_MODEL), lambda i, s: (i, 0))
    one = lambda shape: pl.BlockSpec(shape, lambda i, s: tuple(0 for _ in shape))
    gate = lambda r, w: pl.BlockSpec((1, r, w), lambda i, s: (s, 0, 0))
    val = lambda r, w: pl.BlockSpec((1, r, w), lambda i, s: (s + FF_PAIRS, 0, 0))
    tile = pl.BlockSpec((1, tm, FF_SHARD), lambda i, s: (s, i, 0))
    half = lambda dt: jax.ShapeDtypeStruct((FF_PAIRS, T, FF_SHARD), dt)
    return _pcall(
        body, name="ffn_fwd", grid=(nt, FF_PAIRS),
        in_specs=[rows, _vmem_full(), gate(3, FF_SHARD), val(3, FF_SHARD), gate(1, FF_SHARD), val(1, FF_SHARD),
                  _vmem_full(), rows, one((1, D_MODEL)), rows],
        out_specs=[tile] * 5 + [rows, rows, one((1, D_MODEL)), one((8, 128))],
        out_shape=[half(BF), half(BF), half(F32), half(F32), half(BF), jax.ShapeDtypeStruct((T, D_MODEL), F32),
                   jax.ShapeDtypeStruct((T, D_MODEL), BF), jax.ShapeDtypeStruct((1, D_MODEL), F32),
                   jax.ShapeDtypeStruct((8, 128), F32)],
        scratch_shapes=[pltpu.VMEM((2, FF_PAIRS, 8, FF_SHARD), F32), pltpu.VMEM((tm, D_MODEL), F32)],
        compiler_params=_params(("arbitrary", "arbitrary")),
    )(h2, w_up8, conv_w8, conv_w8, conv_b8, conv_b8, w_down4, x1, g4, target)


def _ffn_bwd_a(dz, h2, w_down4, u_g, u_v, up_g, up_v, y4, conv_w8, tm):
    T = dz.shape[0]
    nt = T // tm

    def body(dz_ref, h_ref, wd_ref, ug_ref, uv_ref, upg_ref, upv_ref, y_ref, cwg_ref, cwv_ref,
             dupg_ref, dupv_ref, dcbg_ref, dcbv_ref, dcwg_ref, dcwv_ref, gwug_out, gwuv_out, gwd_out,
             carry, gwug_ref, gwuv_ref, gwd_ref):
        @pl.when(pl.program_id(1) == 0)
        def _():
            for ref in (dcbg_ref, dcbv_ref, dcwg_ref, dcwv_ref, gwug_ref, gwuv_ref, gwd_ref, carry):
                ref[...] = jnp.zeros_like(ref)

        dz = dz_ref[...]
        h = h_ref[...]
        dy = _dot_nt(dz, wd_ref[0])
        gwd_ref[0] += _dot_tn(y_ref[0], dz)
        gelu, dgelu = _gelu_parts(ug_ref[0])
        parts = ((0, dy * uv_ref[0] * dgelu, upg_ref, cwg_ref, dupg_ref, dcbg_ref, dcwg_ref, gwug_ref),
                 (1, dy * gelu, upv_ref, cwv_ref, dupv_ref, dcbv_ref, dcwv_ref, gwuv_ref))
        for part, d, up_ref, cw_ref, dup_ref, dcb_ref, dcw_ref, gwu_ref in parts:
            after = carry[part]
            d1 = _shift_up(d, 1, after)
            d2 = _shift_up(d, 2, after)
            carry[part] = d[0:8, :]
            upc = up_ref[0].astype(F32)
            dcb_ref[0] += jnp.sum(d, axis=0, keepdims=True)
            dcw_ref[0, pl.ds(2, 1), :] += jnp.sum(d * upc, axis=0, keepdims=True)
            dcw_ref[0, pl.ds(1, 1), :] += jnp.sum(d1 * upc, axis=0, keepdims=True)
            dcw_ref[0, pl.ds(0, 1), :] += jnp.sum(d2 * upc, axis=0, keepdims=True)
            dup = (cw_ref[0, pl.ds(2, 1), :] * d + cw_ref[0, pl.ds(1, 1), :] * d1
                   + cw_ref[0, pl.ds(0, 1), :] * d2).astype(BF)
            dup_ref[0] = dup
            gwu_ref[0] += _dot_tn(h, dup)

        @pl.when(pl.program_id(1) == nt - 1)
        def _():
            s = pl.program_id(0)
            pltpu.sync_copy(gwd_ref, gwd_out.at[pl.ds(s, 1)])
            for acc_ref, out in ((gwug_ref, gwug_out), (gwuv_ref, gwuv_out)):
                gwd_ref[0] = acc_ref[0].T
                pltpu.sync_copy(gwd_ref, out.at[pl.ds(s, 1)])

    rev = pl.BlockSpec((tm, D_MODEL), lambda s, i: (nt - 1 - i, 0))
    tile = pl.BlockSpec((1, tm, FF_SHARD), lambda s, i: (s, nt - 1 - i, 0))
    acc = lambda r, w: pl.BlockSpec((1, r, w), lambda s, i: (s, 0, 0))
    acc_val = pl.BlockSpec((1, 3, FF_SHARD), lambda s, i: (s + FF_PAIRS, 0, 0))
    half = lambda r, dt: jax.ShapeDtypeStruct((FF_PAIRS, r, FF_SHARD), dt)
    return _pcall(
        body, name="ffn_bwd_a", grid=(FF_PAIRS, nt),
        in_specs=[rev, rev, acc(FF_SHARD, D_MODEL), tile, tile, tile, tile, tile, acc(3, FF_SHARD), acc_val],
        out_specs=[tile, tile, acc(1, FF_SHARD), acc(1, FF_SHARD), acc(3, FF_SHARD), acc(3, FF_SHARD),
                   ANY_SPEC, ANY_SPEC, ANY_SPEC],
        out_shape=[half(T, BF), half(T, BF), half(1, F32), half(1, F32), half(3, F32), half(3, F32),
                   jax.ShapeDtypeStruct((FF_PAIRS, FF_SHARD, D_MODEL), F32),
                   jax.ShapeDtypeStruct((FF_PAIRS, FF_SHARD, D_MODEL), F32),
                   jax.ShapeDtypeStruct((FF_PAIRS, FF_SHARD, D_MODEL), F32)],
        scratch_shapes=[pltpu.VMEM((2, 8, FF_SHARD), F32), pltpu.VMEM((1, D_MODEL, FF_SHARD), F32),
                        pltpu.VMEM((1, D_MODEL, FF_SHARD), F32), pltpu.VMEM((1, FF_SHARD, D_MODEL), F32)],
        compiler_params=_params(("arbitrary", "arbitrary")),
    )(dz, h2, w_down4, u_g, u_v, up_g, up_v, y4, conv_w8, conv_w8)


def _ffn_bwd_b(dup_g, dup_v, w_up8, x1, dout, g3, mixed, g2, w_out, tm, gwu_g, gwu_v):
    T = x1.shape[0]
    nt = T // tm

    def body(dupg_ref, dupv_ref, wup_ref, x1_ref, dout_ref, g3_ref, mixed_ref, g2_ref, wout_ref, gwug_ref, gwuv_ref,
             dx1_ref, dmixed_ref, da_ref, dr_ref, dg3_ref, dg2_ref, pup_ref, *sems):
        ex = _Exchange([], [((gwug_ref, gwuv_ref), pup_ref)], *sems)

        @pl.when(pl.program_id(0) == 0)
        def _():
            ex.start()
            dg3_ref[...] = jnp.zeros_like(dg3_ref)
            dg2_ref[...] = jnp.zeros_like(dg2_ref)

        dh2 = jnp.zeros((tm, D_MODEL), F32)
        for s in range(FF_PAIRS):
            dh2 = dh2 + _dot_nt(dupg_ref[s], wup_ref[s]) + _dot_nt(dupv_ref[s], wup_ref[s + FF_PAIRS])
        r3, n3 = _rms_stats(x1_ref[...])
        dg3_ref[...] += jnp.sum(dh2 * n3, axis=0, keepdims=True)
        dx1 = dout_ref[...] + _rms_bwd(n3, r3, dh2 * g3_ref[...])
        dx1_ref[...] = dx1
        r2, n2 = _rms_stats(mixed_ref[...])
        dg2_ref[...] += jnp.sum(dx1 * n2, axis=0, keepdims=True)
        dmixed = _rms_bwd(n2, r2, dx1 * g2_ref[...]).astype(BF)
        dmixed_ref[...] = dmixed
        da_ref[...] = _dot_nt(dmixed, wout_ref[:ATTN_W, :])
        dr_ref[...] = _dot_nt(dmixed, wout_ref[ATTN_W:, :])
        pl.when(pl.program_id(0) == nt - 1)(ex.wait)

    half = pl.BlockSpec((FF_PAIRS, tm, FF_SHARD), lambda i: (0, i, 0))
    outs = _pcall(
        body, name="ffn_bwd_b", grid=(nt,),
        in_specs=[half, half, _vmem_full(), _rows(tm, D_MODEL), _rows(tm, D_MODEL), _const((1, D_MODEL)),
                  _rows(tm, D_MODEL), _const((1, D_MODEL)), _vmem_full(), ANY_SPEC, ANY_SPEC],
        out_specs=[_rows(tm, D_MODEL), _rows(tm, D_MODEL), _rows(tm, ATTN_W), _rows(tm, RET_W),
                   _const((1, D_MODEL)), _const((1, D_MODEL)), ANY_SPEC],
        out_shape=[jax.ShapeDtypeStruct((T, D_MODEL), F32), jax.ShapeDtypeStruct((T, D_MODEL), BF),
                   jax.ShapeDtypeStruct((T, ATTN_W), F32), jax.ShapeDtypeStruct((T, RET_W), F32),
                   jax.ShapeDtypeStruct((1, D_MODEL), F32), jax.ShapeDtypeStruct((1, D_MODEL), F32),
                   jax.ShapeDtypeStruct((N_DEV, FF_SHARD, D_MODEL), F32)],
        scratch_shapes=_Exchange.scratch(1),
        compiler_params=_params(("arbitrary",)),
    )(dup_g, dup_v, w_up8, x1, dout, g3, mixed, g2, w_out, gwu_g, gwu_v)
    return outs[:6], outs[6]


def _ret_bwd(decay, qr, kr, vr, gr, o, states, dr, d_intra, d_intra_t, xi_b, zeta_b, cos, sin_s, swaps, ncs):
    T = qr.shape[0]
    nc = T // BLOCK
    H, C = N_RET_HEADS, RET_HEAD_DIM
    ns = len(swaps)

    def body(decay_ref, q_ref, k_ref, v_ref, g_ref, o_ref, s_ref, dr_ref, d_ref, dt_ref, xi_ref, zeta_ref,
             cos_ref, sin_ref, *rest):
        ex = _exchange_of(rest[:ns], rest[ns + 1:2 * ns + 1], rest[2 * ns + 2:], 0)
        dret_ref, gstate = rest[ns], rest[2 * ns + 1]

        @pl.when(pl.program_id(0) == 0)
        def _():
            ex.start()
            gstate[...] = jnp.zeros_like(gstate)

        pairs = [(b, h) for b in range(ncs) for h in range(H)]
        n = len(pairs)
        sl = lambda b, h: (slice(b * BLOCK, (b + 1) * BLOCK), slice(h * C, (h + 1) * C))
        cat = lambda ref: jnp.concatenate([ref[sl(b, h)] for b, h in pairs], axis=0)
        tab = lambda ref: jnp.concatenate([ref[h] for _, h in pairs], axis=0)
        part = lambda x, i: x[i * BLOCK:(i + 1) * BLOCK, :]
        q = [q_ref[sl(b, h)] for b, h in pairs]
        k = [k_ref[sl(b, h)] for b, h in pairs]
        v = [v_ref[sl(b, h)] for b, h in pairs]
        g, o_all, dr_all = cat(g_ref), cat(o_ref), cat(dr_ref)
        mu = jnp.mean(o_all, axis=-1, keepdims=True)
        oc = o_all - mu
        rs = lax.rsqrt(jnp.mean(oc * oc, axis=-1, keepdims=True) + GN_EPS)
        on = oc * rs
        sg = jax.nn.sigmoid(g)
        dg = (dr_all * on * (sg * (1.0 + g * (1.0 - sg)))).astype(BF)
        don = dr_all * (g * sg)
        do = rs * (don - jnp.mean(don, axis=-1, keepdims=True) - on * jnp.mean(don * on, axis=-1, keepdims=True))
        do_b = do.astype(BF)
        dox_b = (do * tab(xi_ref)).astype(BF)
        zeta = tab(zeta_ref)
        kz = (jnp.concatenate(k, axis=0).astype(F32) * zeta).astype(BF)
        d_t = tab(dt_ref)
        da_b = (jnp.concatenate([_dot_nt(part(do_b, i), v[i]) for i in range(n)], axis=0) * tab(d_ref)).astype(BF)
        dat_b = (jnp.concatenate([_dot_nt(v[i], part(do_b, i)) for i in range(n)], axis=0) * d_t).astype(BF)
        mt_b = (jnp.concatenate([_dot_nt(k[i], q[i]) for i in range(n)], axis=0) * d_t).astype(BF)
        dq = [_dot(part(da_b, i), k[i]) + _dot_nt(part(dox_b, i), s_ref[pairs[i]]) for i in range(n)]
        dk1 = [_dot(part(dat_b, i), q[i]) for i in range(n)]
        dv1 = [_dot(part(mt_b, i), part(do_b, i)) for i in range(n)]
        qtd = [_dot_tn(q[i], part(dox_b, i)) for i in range(n)]
        gst_b = [None] * n
        for h in range(H):
            gst = gstate[h]
            for b in reversed(range(ncs)):
                i = b * H + h
                gst_b[i] = gst.astype(BF)
                gst = decay_ref[h] * gst + qtd[i]
            gstate[h] = gst
        dk2 = jnp.concatenate([_dot_nt(v[i], gst_b[i]) for i in range(n)], axis=0) * zeta
        dv = jnp.concatenate([dv1[i] + _dot(part(kz, i), gst_b[i]) for i in range(n)], axis=0).astype(BF)
        even = lax.broadcasted_iota(jnp.int32, (n * BLOCK, C), 1) % 2 == 0
        cos_t = jnp.concatenate([cos_ref[b * BLOCK:(b + 1) * BLOCK, :] for b, _ in pairs], axis=0)
        sin_t = jnp.concatenate([sin_ref[b * BLOCK:(b + 1) * BLOCK, :] for b, _ in pairs], axis=0)
        dq = jnp.concatenate(dq, axis=0)
        dk = jnp.concatenate(dk1, axis=0) + dk2
        dq = (dq * cos_t - _rot(dq, even) * sin_t).astype(BF)
        dk = ((dk * cos_t - _rot(dk, even) * sin_t) * RET_K_SCALE).astype(BF)
        for i, (b, h) in enumerate(pairs):
            rows = slice(b * BLOCK, (b + 1) * BLOCK)
            for j, x in enumerate((dq, dk, dv, dg)):
                dret_ref[rows, j * RET_W + h * C:j * RET_W + (h + 1) * C] = part(x, i)
        pl.when(pl.program_id(0) == steps - 1)(ex.wait)

    steps = nc // ncs
    rev = lambda w: pl.BlockSpec((BLOCK * ncs, w), lambda n: (steps - 1 - n, 0))
    tab = pl.BlockSpec((H, C, C), lambda n: (0, 0, 0))
    outs = _pcall(
        body, name="ret_bwd", grid=(steps,),
        in_specs=[_smem_full(), rev(RET_W), rev(RET_W), rev(RET_W), rev(RET_W), rev(RET_W),
                  pl.BlockSpec((ncs, H, C, C), lambda n: (steps - 1 - n, 0, 0, 0)), rev(RET_W), tab, tab, tab, tab,
                  rev(C), rev(C)] + [ANY_SPEC] * ns,
        out_specs=[rev(4 * RET_W)] + [ANY_SPEC] * ns,
        out_shape=[jax.ShapeDtypeStruct((T, 4 * RET_W), BF)] + _exchange_shapes([], swaps),
        scratch_shapes=[pltpu.VMEM((H, C, C), F32)] + _Exchange.scratch(ns),
        compiler_params=_params(("arbitrary",)),
    )(decay, qr, kr, vr, gr, o, states, dr, d_intra, d_intra_t, xi_b, zeta_b, cos, sin_s, *swaps)
    return outs[0], outs[1:]


def _attn_bwd_dq(sinks, qa, ka, va, da, lse, nbs, swaps):
    T = qa.shape[0]
    steps = T // (BLOCK * nbs)
    R = nbs * N_ATTN_HEADS * BLOCK
    ns = len(swaps)

    def body(sink_ref, q_ref, kc_ref, kp_ref, vc_ref, vp_ref, da_ref, lse_ref, *rest):
        ex = _exchange_of(rest[:ns], rest[ns + 3:2 * ns + 3], rest[2 * ns + 3:], 0)
        dq_ref, deltat_ref, dsink_ref = rest[ns:ns + 3]
        n = pl.program_id(0)

        @pl.when(n == 0)
        def _():
            ex.start()
            dsink_ref[...] = jnp.zeros_like(dsink_ref)

        kcat = _block_variants(kp_ref, kc_ref, nbs)
        vcat = _block_variants(vp_ref, vc_ref, nbs)
        tri1 = _tri(BLOCK)
        lane = lax.broadcasted_iota(jnp.int32, (BLOCK, BLOCK), 1)
        s_tiles, dp_tiles, lse_cols = [], [], []
        for b in range(nbs):
            rows = slice(b * BLOCK, (b + 1) * BLOCK)
            lse_tile = lse_ref[rows, :]
            for pair in range(N_ATTN_HEADS // 2):
                qp = _scaled(q_ref[rows, pair * 128:(pair + 1) * 128])
                dop = da_ref[rows, pair * 128:(pair + 1) * 128].astype(BF)
                for e in range(2):
                    s = _fold(_dot_nt(qp, kcat[b][pair // 2][e]), tri1, True)
                    if b == 0:
                        s = jnp.where(tri1 & (n == 0), MASK_VALUE, s)
                    s_tiles.append(s)
                    dp_tiles.append(_fold(_dot_nt(dop, vcat[b][pair // 2][e]), tri1, True))
                    lse_cols.append(jnp.sum(jnp.where(lane == 2 * pair + e, lse_tile, 0.0), axis=-1, keepdims=True))
        lse_c = jnp.concatenate(lse_cols, axis=0)
        p = jnp.exp(jnp.concatenate(s_tiles, axis=0) - lse_c)
        dp = jnp.concatenate(dp_tiles, axis=0)
        delta = jnp.sum(p * dp, axis=-1, keepdims=True)
        ds2 = _unfold(p * (dp - delta), _tri(R), True)
        for b in range(nbs):
            for pair in range(N_ATTN_HEADS // 2):
                r0 = (b * N_ATTN_HEADS + 2 * pair) * BLOCK
                acc = (_dot(ds2[r0:r0 + BLOCK, :], kcat[b][pair // 2][0])
                       + _dot(ds2[r0 + BLOCK:r0 + 2 * BLOCK, :], kcat[b][pair // 2][1]))
                dq_ref[b * BLOCK:(b + 1) * BLOCK, pair * 128:(pair + 1) * 128] = (acc * ATTN_SCALE).astype(BF)
        for b, t in enumerate(_head_cols(delta, nbs)):
            deltat_ref[:, b * BLOCK:(b + 1) * BLOCK] = t.T[:N_ATTN_HEADS, :]
        sink = jnp.concatenate([jnp.full((BLOCK, 1), sink_ref[head], F32)
                                for _ in range(nbs) for head in range(N_ATTN_HEADS)], axis=0)
        ds_sink = -jnp.exp(sink - lse_c) * delta
        row8 = lax.broadcasted_iota(jnp.int32, (N_ATTN_HEADS, BLOCK), 0)
        dsink = jnp.zeros((N_ATTN_HEADS, BLOCK), F32)
        for b in range(nbs):
            for head in range(N_ATTN_HEADS):
                r0 = (b * N_ATTN_HEADS + head) * BLOCK
                dsink = dsink + jnp.where(row8 == head, jnp.sum(ds_sink[r0:r0 + BLOCK, :], axis=0, keepdims=True), 0.0)
        dsink_ref[...] += dsink
        pl.when(n == steps - 1)(ex.wait)

    cur = lambda w: pl.BlockSpec((BLOCK * nbs, w), lambda n: (n, 0))
    prev = lambda w: pl.BlockSpec((BLOCK, w), lambda n: (jnp.maximum(n * nbs - 1, 0), 0))
    outs = _pcall(
        body, name="attn_bwd_dq", grid=(steps,),
        in_specs=[_smem_full(), cur(ATTN_W), cur(KV_W), prev(KV_W), cur(KV_W), prev(KV_W), cur(ATTN_W), cur(BLOCK)]
        + [ANY_SPEC] * ns,
        out_specs=[cur(ATTN_W), pl.BlockSpec((N_ATTN_HEADS, BLOCK * nbs), lambda n: (0, n)),
                   _const((N_ATTN_HEADS, BLOCK))] + [ANY_SPEC] * ns,
        out_shape=[jax.ShapeDtypeStruct((T, ATTN_W), BF), jax.ShapeDtypeStruct((N_ATTN_HEADS, T), F32),
                   jax.ShapeDtypeStruct((N_ATTN_HEADS, BLOCK), F32)] + _exchange_shapes([], swaps),
        scratch_shapes=_Exchange.scratch(ns),
        compiler_params=_params(("arbitrary",)),
    )(sinks, qa, ka, ka, va, va, da, lse, *swaps)
    return outs[:3], outs[3:]


def _attn_bwd_dkv(qa, ka, va, da, lse_t, delta_t, nbs):
    T = qa.shape[0]
    nb = T // BLOCK
    steps = nb // nbs
    R = nbs * N_ATTN_HEADS * BLOCK

    def body(qc_ref, qn_ref, dac_ref, dan_ref, k_ref, v_ref, lc_ref, ln_ref, dc_ref, dn_ref, dk_ref, dv_ref):
        n = pl.program_id(0)
        tri1 = _tri(BLOCK, True)
        lo = lax.broadcasted_iota(jnp.int32, (BLOCK, 128), 1) < HEAD_DIM
        kv = [_kv_variants(k_ref[b * BLOCK:(b + 1) * BLOCK, :]) for b in range(nbs)]
        vv = [_kv_variants(v_ref[b * BLOCK:(b + 1) * BLOCK, :]) for b in range(nbs)]
        qcat, docat = [], []
        s_tiles, dp_tiles, lse_tiles, delta_tiles = [], [], [], []
        for b in range(nbs):
            rows = slice(b * BLOCK, (b + 1) * BLOCK)
            nrows = slice((b + 1) * BLOCK, (b + 2) * BLOCK)
            inside = b < nbs - 1
            for pair in range(N_ATTN_HEADS // 2):
                ps = slice(pair * 128, (pair + 1) * 128)
                q2 = _scaled(jnp.concatenate([qc_ref[rows, ps], qc_ref[nrows, ps] if inside else qn_ref[:, ps]], axis=0))
                do2 = jnp.concatenate([dac_ref[rows, ps], dac_ref[nrows, ps] if inside else dan_ref[:, ps]],
                                      axis=0).astype(BF)
                qcat.append(q2)
                docat.append(do2)
                for e in range(2):
                    one = pl.ds(2 * pair + e, 1)
                    s = _fold(_dot_nt(kv[b][pair // 2][e], q2), tri1, False)
                    if not inside:
                        s = jnp.where(tri1 & (n == steps - 1), MASK_VALUE, s)
                    s_tiles.append(s)
                    dp_tiles.append(_fold(_dot_nt(vv[b][pair // 2][e], do2), tri1, False))
                    lse_tiles.append(jnp.where(tri1, lc_ref[one, nrows] if inside else ln_ref[one, :], lc_ref[one, rows]))
                    delta_tiles.append(jnp.where(tri1, dc_ref[one, nrows] if inside else dn_ref[one, :],
                                                 dc_ref[one, rows]))
        pt = jnp.exp(jnp.concatenate(s_tiles, axis=0) - jnp.concatenate(lse_tiles, axis=0))
        dst = pt * (jnp.concatenate(dp_tiles, axis=0) - jnp.concatenate(delta_tiles, axis=0))
        tri = _tri(R, True)
        pt2 = _unfold(pt, tri, False)
        dst2 = _unfold(dst, tri, False)
        for b in range(nbs):
            dk = jnp.zeros((BLOCK, 128), F32)
            dv = jnp.zeros((BLOCK, 128), F32)
            for pair in range(N_ATTN_HEADS // 2):
                h = pair // 2
                for e in range(2):
                    r0 = (b * N_ATTN_HEADS + 2 * pair + e) * BLOCK
                    half = lo if e == 0 else jnp.logical_not(lo)
                    dv_e = jnp.where(half, _dot(pt2[r0:r0 + BLOCK, :], docat[b * 4 + pair]), 0.0)
                    dk_e = jnp.where(half, _dot(dst2[r0:r0 + BLOCK, :], qcat[b * 4 + pair]), 0.0)
                    if e != h:
                        dv_e = pltpu.roll(dv_e, HEAD_DIM, 1)
                        dk_e = pltpu.roll(dk_e, HEAD_DIM, 1)
                    dv = dv + dv_e
                    dk = dk + dk_e
            dk_ref[b * BLOCK:(b + 1) * BLOCK, :] = dk.astype(BF)
            dv_ref[b * BLOCK:(b + 1) * BLOCK, :] = dv.astype(BF)

    cur = lambda w: pl.BlockSpec((BLOCK * nbs, w), lambda n: (n, 0))
    nxt = lambda w: pl.BlockSpec((BLOCK, w), lambda n: (jnp.minimum((n + 1) * nbs, nb - 1), 0))
    tcur = pl.BlockSpec((N_ATTN_HEADS, BLOCK * nbs), lambda n: (0, n))
    tnxt = pl.BlockSpec((N_ATTN_HEADS, BLOCK), lambda n: (0, jnp.minimum((n + 1) * nbs, nb - 1)))
    return _pcall(
        body, name="attn_bwd_dkv", grid=(steps,),
        in_specs=[cur(ATTN_W), nxt(ATTN_W), cur(ATTN_W), nxt(ATTN_W), cur(KV_W), cur(KV_W), tcur, tnxt, tcur, tnxt],
        out_specs=[cur(KV_W), cur(KV_W)],
        out_shape=[jax.ShapeDtypeStruct((T, KV_W), BF), jax.ShapeDtypeStruct((T, KV_W), BF)],
        compiler_params=_params(("parallel",)),
    )(qa, qa, da, da, ka, va, lse_t, lse_t, delta_t, delta_t)


def _in_proj_bwd(dqa, dka, dva, dret, w_in, x, g1, dx1, tm):
    T = x.shape[0]

    def body(dqa_ref, dka_ref, dva_ref, dret_ref, w_ref, x_ref, g_ref, dx1_ref, dx_ref, dg1_ref):
        @pl.when(pl.program_id(0) == 0)
        def _():
            dg1_ref[...] = jnp.zeros_like(dg1_ref)

        dh = (_dot(dqa_ref[...], w_ref[QA0:QA0 + ATTN_W, :]) + _dot(dka_ref[...], w_ref[KA0:KA0 + KV_W, :])
              + _dot(dva_ref[...], w_ref[VA0:VA0 + KV_W, :]) + _dot(dret_ref[...], w_ref[QR0:IN_W, :]))
        r, n = _rms_stats(x_ref[...])
        dg1_ref[...] += jnp.sum(dh * n, axis=0, keepdims=True)
        dx_ref[...] = dx1_ref[...] + _rms_bwd(n, r, dh * g_ref[...])

    return _pcall(
        body, name="in_proj_bwd", grid=(T // tm,),
        in_specs=[_rows(tm, ATTN_W), _rows(tm, KV_W), _rows(tm, KV_W), _rows(tm, 4 * RET_W), _vmem_full(),
                  _rows(tm, D_MODEL), _const((1, D_MODEL)), _rows(tm, D_MODEL)],
        out_specs=[_rows(tm, D_MODEL), _const((1, D_MODEL))],
        out_shape=[jax.ShapeDtypeStruct((T, D_MODEL), F32), jax.ShapeDtypeStruct((1, D_MODEL), F32)],
        compiler_params=_params(("arbitrary",)),
    )(dqa, dka, dva, dret, w_in, x, g1, dx1)


def _wgrad(a_list, b_list, tk, name):
    T = a_list[0].shape[0]
    na, nbb = len(a_list), len(b_list)
    m_sizes = [a.shape[1] for a in a_list]
    n_sizes = [b.shape[1] for b in b_list]
    M, N = sum(m_sizes), sum(n_sizes)
    nk = T // tk
    chunk = 512

    def body(*refs):
        a_refs, b_refs = refs[:na], refs[na:na + nbb]
        out_ref, acc = refs[na + nbb], refs[na + nbb + 1]
        k = pl.program_id(0)

        @pl.when(k == 0)
        def _():
            acc[...] = jnp.zeros_like(acc)

        r0 = 0
        for ai in range(na):
            a = a_refs[ai][...]
            c0 = 0
            for bi in range(nbb):
                for s in range(0, n_sizes[bi], chunk):
                    w = min(chunk, n_sizes[bi] - s)
                    acc[r0:r0 + m_sizes[ai], c0 + s:c0 + s + w] += _dot_tn(a, b_refs[bi][:, s:s + w])
                c0 += n_sizes[bi]
            r0 += m_sizes[ai]

        @pl.when(k == nk - 1)
        def _():
            pltpu.sync_copy(acc, out_ref)

    return _pcall(
        body, name=name, grid=(nk,),
        in_specs=[_rows(tk, w) for w in m_sizes + n_sizes],
        out_specs=pl.BlockSpec(memory_space=pl.ANY),
        out_shape=jax.ShapeDtypeStruct((M, N), F32),
        scratch_shapes=[pltpu.VMEM((M, N), F32)],
        compiler_params=_params(("arbitrary",)),
    )(*a_list, *b_list)


def _adamw_math(w, g, m, v):
    m = ADAM_B1 * m + (1.0 - ADAM_B1) * g
    v = ADAM_B2 * v + (1.0 - ADAM_B2) * (g * g)
    m_hat = m / (1.0 - ADAM_B1 ** ADAM_STEP)
    v_hat = v / (1.0 - ADAM_B2 ** ADAM_STEP)
    delta = -ADAM_LR * (m_hat / (jnp.sqrt(v_hat) + ADAM_EPS) + ADAM_WD * w)
    return delta, m, v


def _sum_parts(parts_ref):
    g = parts_ref[0].astype(F32)
    for i in range(1, N_DEV):
        g = g + parts_ref[i].astype(F32)
    return g


def _adamw_shard(parts, w, m, v, tr, name):
    R, C = w.shape

    def body(p_ref, w_ref, m_ref, v_ref, g_ref, d_ref, nm_ref, nv_ref):
        g = _sum_parts(p_ref)
        g_ref[...] = g
        d_ref[...], nm_ref[...], nv_ref[...] = _adamw_math(w_ref[...], g, m_ref[...], v_ref[...])

    blk = pl.BlockSpec((tr, C), lambda i: (i, 0))
    return _pcall(
        body, name=name, grid=(R // tr,),
        in_specs=[pl.BlockSpec((N_DEV, tr, C), lambda i: (0, i, 0)), blk, blk, blk],
        out_specs=[blk] * 4,
        out_shape=[jax.ShapeDtypeStruct((R, C), F32)] * 4,
        compiler_params=_params(("parallel",)),
    )(parts, w, m, v)


def _sum_small(parts):
    def body(p_ref, g_ref):
        g_ref[...] = _sum_parts(p_ref)

    return _pcall(body, name="sum_small", out_shape=jax.ShapeDtypeStruct(parts.shape[1:], F32),
                  in_specs=[_vmem_full()], out_specs=_vmem_full())(parts)


def _adamw_small(g, w, m, v, name):
    def body(g_ref, w_ref, m_ref, v_ref, d_ref, nm_ref, nv_ref):
        d_ref[...], nm_ref[...], nv_ref[...] = _adamw_math(w_ref[...], g_ref[...], m_ref[...], v_ref[...])

    return _pcall(body, name=name, out_shape=[jax.ShapeDtypeStruct(w.shape, F32)] * 3,
                  in_specs=[_vmem_full()] * 4, out_specs=[_vmem_full()] * 3)(g, w, m, v)


def _tables(T):
    h, c = N_RET_HEADS, BLOCK
    pos = jnp.arange(T, dtype=F32)
    angle = 1.0 / jnp.power(10000.0, jnp.linspace(0.0, 1.0, RET_HEAD_DIM // 2, dtype=F32))
    angle = jnp.repeat(angle, 2)
    sin = jnp.sin(pos[:, None] * angle[None])
    cos = jnp.cos(pos[:, None] * angle[None])
    even = (jnp.arange(RET_HEAD_DIM) % 2 == 0)[None, :]
    sin_s = jnp.where(even, -sin, sin)
    log_gamma = jnp.log(1.0 - jnp.power(2.0, -5.0 - jnp.arange(h, dtype=F32)))
    idx = jnp.arange(c, dtype=F32)
    rel = idx[:, None] - idx[None, :]
    d_intra = jnp.where(rel[None] >= 0, jnp.exp(log_gamma[:, None, None] * jnp.maximum(rel, 0.0)[None]), 0.0)
    xi = jnp.exp(log_gamma[None, :] * (idx[:, None] + 1.0))
    zeta = jnp.exp(log_gamma[None, :] * (c - 1.0 - idx[:, None]))
    decay = jnp.exp(log_gamma * c)
    xi_b = jnp.broadcast_to(xi.T[:, :, None], (h, c, RET_HEAD_DIM))
    zeta_b = jnp.broadcast_to(zeta.T[:, :, None], (h, c, RET_HEAD_DIM))
    return cos, sin_s, d_intra, jnp.swapaxes(d_intra, 1, 2), xi_b, zeta_b, decay


def _from_shards(sh):
    n, r, cols = sh.shape
    return jnp.swapaxes(sh, 0, 1).reshape(r, n * cols)


SMALL_ROWS = 216


def _pack_small(gains, conv_b, conv_w, sinks, scalar=None):
    last = jnp.concatenate([sinks.reshape(1, 8), jnp.zeros((1, 1), F32) if scalar is None else scalar.reshape(1, 1)],
                           axis=1)
    parts = [g.reshape(8, 128) for g in gains] + [conv_b.reshape(44, 128), conv_w.reshape(132, 128),
                                                  jnp.pad(last, ((0, 0), (0, 119)))]
    packed = jnp.concatenate(parts, axis=0)
    return jnp.pad(packed, ((0, SMALL_ROWS - packed.shape[0]), (0, 0)))


def kernel(x, mix_pre_norm, w_in, attn_sinks, w_out, mix_post_norm, ffn_pre_norm, w_up, conv_w, conv_b, w_down, ffn_post_norm, loss_target, m_mix_pre_norm, m_w_in, m_attn_sinks, m_w_out, m_mix_post_norm, m_ffn_pre_norm, m_w_up, m_conv_w, m_conv_b, m_w_down, m_ffn_post_norm, v_mix_pre_norm, v_w_in, v_attn_sinks, v_w_out, v_mix_post_norm, v_ffn_pre_norm, v_w_up, v_conv_w, v_conv_b, v_w_down, v_ffn_post_norm):
    T = x.shape[1]
    tm = min(512, T)
    tm_big = min(1024, T)
    tk_grad = min(2048, T)
    nbs = min(8, T // BLOCK)
    x2 = x.reshape(T, D_MODEL)
    target = loss_target.reshape(T, D_MODEL)
    me = 4 * lax.axis_index("x") + 2 * lax.axis_index("y") + lax.axis_index("c")

    g_in = _sequencer_gather(jnp.swapaxes(w_in, 1, 2)[0].astype(BF), "gather_w_in")
    w_in_f = g_in.reshape(IN_W, D_MODEL)
    cos, sin_s, d_intra, d_intra_t, xi_b, zeta_b, decay = _tables(T)
    sinks = attn_sinks.reshape(N_ATTN_HEADS)

    (h1, qa, ka, va, qr, kr, vr, gr), (w_up8, g_cw) = _in_proj(
        x2, mix_pre_norm, w_in_f, cos, sin_s, tm_big, [w_up[0].astype(BF), conv_w[0]])
    (a, lse, lse_t), (g_down, g_out) = _attn_fwd(sinks, qa, ka, va, nbs,
                                                 [w_down[0].astype(BF), w_out[0].astype(BF)])
    w_out_f = g_out.reshape(D_MODEL, D_MODEL)
    o, states, r = _ret_fwd(decay, qr, kr, vr, gr, d_intra, xi_b, zeta_b, nbs)
    mixed, x1, h2 = _out_proj(a, r, w_out_f, x2, mix_post_norm, ffn_pre_norm, tm_big)
    w_down4 = g_down.reshape(FF_PAIRS, FF_SHARD, D_MODEL)
    up_g, up_v, u_g, u_v, y4, dout, dz, dg4, loss_acc = _ffn_fwd(
        h2, w_up8, g_cw, conv_b.reshape(N_DEV, 1, FF_SHARD), w_down4, x1, ffn_post_norm, target, tm)

    dup_g, dup_v, dcb_g, dcb_v, dcw_g, dcw_v, gwu_g, gwu_v, gw_down4 = _ffn_bwd_a(
        dz, h2, w_down4, u_g, u_v, up_g, up_v, y4, g_cw, tm_big)
    dcb = jnp.concatenate([dcb_g, dcb_v], axis=0).reshape(1, 2 * D_FF)
    dcw = _from_shards(jnp.concatenate([dcw_g, dcw_v], axis=0))
    gw_down = gw_down4.reshape(D_FF, D_MODEL)
    (dx1, dmixed, da, dr, dg3, dg2), p_up = _ffn_bwd_b(
        dup_g, dup_v, w_up8, x1, dout, ffn_pre_norm, mixed, mix_post_norm, w_out_f, tm, gwu_g, gwu_v)
    gw_out = _wgrad([a, r], [dmixed], tk_grad, "wgrad_out")
    dret, (p_down,) = _ret_bwd(decay, qr, kr, vr, gr, o, states, dr, d_intra, d_intra_t, xi_b, zeta_b, cos, sin_s,
                               [gw_down.reshape(N_DEV, D_FF // N_DEV, D_MODEL)], nbs)
    (dqa, delta_t, dsink), (p_out,) = _attn_bwd_dq(sinks, qa, ka, va, da, lse, nbs,
                                                   [gw_out.reshape(N_DEV, D_MODEL // N_DEV, D_MODEL)])
    dka, dva = _attn_bwd_dkv(qa, ka, va, da, lse_t, delta_t, nbs)
    grad_x, dg1 = _in_proj_bwd(dqa, dka, dva, dret, w_in_f, x2, mix_pre_norm, dx1, tm)
    gw_in = _wgrad([h1], [dqa, dka, dva, dret], tk_grad, "wgrad_in")

    small = _pack_small([dg1, dg2, dg3, dg4], dcb, dcw, dsink[:, 0], loss_acc[0, 0])
    gw_in_t = gw_in.T.reshape(N_DEV, IN_W // N_DEV, D_MODEL).astype(BF)
    small_all, p_in = _sequencer_last_exchange(small, gw_in_t, "exchange_last")
    g_small = _sum_small(small_all)
    loss = g_small[208, N_ATTN_HEADS]

    t_in = lambda a: jnp.swapaxes(a, 1, 2)[0]
    g_w_in, d_w_in, nm_w_in, nv_w_in = [o.T for o in _adamw_shard(
        p_in, t_in(w_in), t_in(m_w_in), t_in(v_w_in), 176, "adamw_in")]
    g_w_up, d_w_up, nm_w_up, nv_w_up = [o.T for o in _adamw_shard(
        p_up, t_in(w_up), t_in(m_w_up), t_in(v_w_up), 176, "adamw_up")]
    g_w_out, d_w_out, nm_w_out, nv_w_out = _adamw_shard(p_out, w_out[0], m_w_out[0], v_w_out[0], 128, "adamw_out")
    g_w_down, d_w_down, nm_w_down, nv_w_down = _adamw_shard(p_down, w_down[0], m_w_down[0], v_w_down[0], 176,
                                                            "adamw_down")
    gains = [mix_pre_norm, mix_post_norm, ffn_pre_norm, ffn_post_norm]
    m_gains = [m_mix_pre_norm, m_mix_post_norm, m_ffn_pre_norm, m_ffn_post_norm]
    v_gains = [v_mix_pre_norm, v_mix_post_norm, v_ffn_pre_norm, v_ffn_post_norm]
    zeros_cw = jnp.zeros((3, 2 * D_FF), F32)
    w_small = _pack_small(gains, conv_b, zeros_cw, attn_sinks)
    m_small = _pack_small(m_gains, m_conv_b, zeros_cw, m_attn_sinks)
    v_small = _pack_small(v_gains, v_conv_b, zeros_cw, v_attn_sinks)
    d_small, nm_small, nv_small = _adamw_small(g_small, w_small, m_small, v_small, "adamw_small")
    shard_cols = 2 * D_FF // N_DEV
    g_cw = lax.dynamic_slice(g_small[76:208].reshape(3, 2 * D_FF), (0, me * shard_cols), (3, shard_cols))
    d_cw, nm_cw, nv_cw = _adamw_small(g_cw, conv_w[0], m_conv_w[0], v_conv_w[0], "adamw_conv_w")

    def unpack(p):
        gains_o = [p[8 * i:8 * i + 8].reshape(1, D_MODEL) for i in range(4)]
        return gains_o, p[32:76].reshape(1, 2 * D_FF), p[208:209, :N_ATTN_HEADS]

    def leaves(p, w_in_s, w_out_s, w_up_s, cw_s, w_down_s):
        (pre1, post1, pre2, post2), cb, sk = unpack(p)
        return [pre1, w_in_s[None], sk, w_out_s[None], post1, pre2, w_up_s[None], cw_s[None], cb, w_down_s[None],
                post2]

    return (loss, grad_x.reshape(1, T, D_MODEL),
            *leaves(g_small, g_w_in, g_w_out, g_w_up, g_cw, g_w_down),
            *leaves(d_small, d_w_in, d_w_out, d_w_up, d_cw, d_w_down),
            *leaves(nm_small, nm_w_in, nm_w_out, nm_w_up, nm_cw, nm_w_down),
            *leaves(nv_small, nv_w_in, nv_w_out, nv_w_up, nv_cw, nv_w_down))
```
